```python
import math
import jax, jax.numpy as jnp
from jax import lax
import numpy as np

D_MODEL = 1024
BATCH = 8
SEQ = 16384
DEPTH = 2

N_MIXERS = 2
N_CONV = (DEPTH + 1) // 2
N_ATTN = DEPTH // 2
CONV_WIDTH = 3
HEAD_DIM = 64
N_HEADS = D_MODEL // HEAD_DIM
Q_BLOCK = 128
RMS_EPS = 1e-6

kernel_name = "hybrid_shortconv_forgetting_attention"


def _rmsnorm(x, g):
    xf = x.astype(jnp.float32)
    inv = lax.rsqrt(jnp.mean(xf * xf, axis=-1, keepdims=True) + RMS_EPS)
    return (xf * inv * g.astype(jnp.float32)).astype(x.dtype)


def _short_conv_layer(x, norm_g, w_in, conv_w, w_out):
    h = _rmsnorm(x, norm_g)
    proj = jnp.einsum('bsd,de->bse', h, w_in)
    b_g, c_g, xin, z = jnp.split(proj, 4, axis=-1)
    u = c_g * xin
    y = lax.conv_general_dilated(
        u, conv_w[:, None, :].astype(u.dtype),
        window_strides=(1,), padding=[(CONV_WIDTH - 1, 0)],
        dimension_numbers=('NWC', 'WIO', 'NWC'),
        feature_group_count=D_MODEL)
    y = b_g * y * jax.nn.silu(z)
    return x + jnp.einsum('bse,ed->bsd', y, w_out)


def _forgetting_attention(q, k, v, log_f):
    S = q.shape[2]
    c = jnp.cumsum(log_f, axis=-1)
    scale = 1.0 / math.sqrt(HEAD_DIM)
    kpos = jnp.arange(S)
    neg = jnp.finfo(jnp.float32).min

    def block(i):
        start = i * Q_BLOCK
        qb = lax.dynamic_slice_in_dim(q, start, Q_BLOCK, axis=2)
        cb = lax.dynamic_slice_in_dim(c, start, Q_BLOCK, axis=2)
        s = jnp.einsum('bhqd,bhkd->bhqk', qb, k) * scale
        s = s + cb[..., :, None] - c[..., None, :]
        qpos = start + jnp.arange(Q_BLOCK)
        s = jnp.where(kpos[None, :] <= qpos[:, None], s, neg)
        p = jax.nn.softmax(s, axis=-1)
        return jnp.einsum('bhqk,bhkd->bhqd', p, v)

    out = lax.map(block, jnp.arange(S // Q_BLOCK))
    nb, b, h, qn, dh = out.shape
    return jnp.transpose(out, (1, 2, 0, 3, 4)).reshape(b, h, nb * qn, dh)


def _attn_layer(x, norm_g, w_in, b_f, q_g, k_g, w_out):
    B, S, D = x.shape
    h = _rmsnorm(x, norm_g)
    proj = jnp.einsum('bsd,de->bse', h, w_in)
    q = proj[..., 0 * D:1 * D]
    k = proj[..., 1 * D:2 * D]
    v = proj[..., 2 * D:3 * D]
    z = proj[..., 3 * D:4 * D]
    f_logit = proj[..., 4 * D:].astype(jnp.float32) + b_f.astype(jnp.float32)
    to_heads = lambda t: jnp.transpose(t.reshape(B, S, N_HEADS, HEAD_DIM), (0, 2, 1, 3))
    q = _rmsnorm(to_heads(q), q_g).astype(jnp.float32)
    k = _rmsnorm(to_heads(k), k_g).astype(jnp.float32)
    v = to_heads(v).astype(jnp.float32)
    log_f = jnp.transpose(jax.nn.log_sigmoid(f_logit), (0, 2, 1))
    o = _forgetting_attention(q, k, v, log_f)
    o = jnp.transpose(o, (0, 2, 1, 3)).reshape(B, S, D).astype(x.dtype)
    o = o * jax.nn.silu(z)
    return x + jnp.einsum('bse,ed->bsd', o, w_out)


def _fwd_setup_inputs(seed: int = 0) -> dict:
    key = jax.random.key(seed)
    ks = jax.random.split(key, 12)
    D = D_MODEL
    s_d = D ** -0.5
    x = jax.random.normal(ks[0], (BATCH, SEQ, D), jnp.float32)
    conv_norm_g = 1.0 + 0.02 * jax.random.normal(ks[1], (N_CONV, D), jnp.float32)
    conv_w_in = jax.random.normal(ks[2], (N_CONV, D, 4 * D), jnp.float32) * s_d
    conv_w = jax.random.normal(ks[3], (N_CONV, CONV_WIDTH, D), jnp.float32) * (CONV_WIDTH ** -0.5)
    conv_w_out = jax.random.normal(ks[4], (N_CONV, D, D), jnp.float32) * s_d
    attn_norm_g = 1.0 + 0.02 * jax.random.normal(ks[5], (N_ATTN, D), jnp.float32)
    attn_w_in = jax.random.normal(ks[6], (N_ATTN, D, 4 * D + N_HEADS), jnp.float32) * s_d
    attn_b_f = jax.random.uniform(ks[7], (N_ATTN, N_HEADS), jnp.float32, 1.0, 4.0)
    attn_q_norm_g = 1.0 + 0.02 * jax.random.normal(ks[8], (N_ATTN, HEAD_DIM), jnp.float32)
    attn_k_norm_g = 1.0 + 0.02 * jax.random.normal(ks[9], (N_ATTN, HEAD_DIM), jnp.float32)
    attn_w_out = jax.random.normal(ks[10], (N_ATTN, D, D), jnp.float32) * s_d
    return {"x": x, "conv_norm_g": conv_norm_g, "conv_w_in": conv_w_in, "conv_w": conv_w,
            "conv_w_out": conv_w_out, "attn_norm_g": attn_norm_g, "attn_w_in": attn_w_in,
            "attn_b_f": attn_b_f, "attn_q_norm_g": attn_q_norm_g, "attn_k_norm_g": attn_k_norm_g,
            "attn_w_out": attn_w_out}


def _fwd_reference(x, conv_norm_g, conv_w_in, conv_w, conv_w_out, attn_norm_g, attn_w_in,
              attn_b_f, attn_q_norm_g, attn_k_norm_g, attn_w_out):
    for i in range(DEPTH):
        j = i // N_MIXERS
        if i % N_MIXERS == 0:
            x = _short_conv_layer(x, conv_norm_g[j], conv_w_in[j], conv_w[j], conv_w_out[j])
        else:
            x = _attn_layer(x, attn_norm_g[j], attn_w_in[j], attn_b_f[j],
                            attn_q_norm_g[j], attn_k_norm_g[j], attn_w_out[j])
    return x


import jax as _jax
import jax.numpy as _jnp

TWIN_FORMAT = 'train_step'
FWD_PARAMS = ['x', 'conv_norm_g', 'conv_w_in', 'conv_w', 'conv_w_out', 'attn_norm_g', 'attn_w_in', 'attn_b_f', 'attn_q_norm_g', 'attn_k_norm_g', 'attn_w_out']
TWIN_WEIGHTS = ['conv_norm_g', 'conv_w_in', 'conv_w', 'conv_w_out', 'attn_norm_g', 'attn_w_in', 'attn_b_f', 'attn_q_norm_g', 'attn_k_norm_g', 'attn_w_out']
TWIN_DIFF_INPUT = 'x'
TWIN_INPUTS = ['x', 'conv_norm_g', 'conv_w_in', 'conv_w', 'conv_w_out', 'attn_norm_g', 'attn_w_in', 'attn_b_f', 'attn_q_norm_g', 'attn_k_norm_g', 'attn_w_out', 'loss_target', 'm_conv_norm_g', 'm_conv_w_in', 'm_conv_w', 'm_conv_w_out', 'm_attn_norm_g', 'm_attn_w_in', 'm_attn_b_f', 'm_attn_q_norm_g', 'm_attn_k_norm_g', 'm_attn_w_out', 'v_conv_norm_g', 'v_conv_w_in', 'v_conv_w', 'v_conv_w_out', 'v_attn_norm_g', 'v_attn_w_in', 'v_attn_b_f', 'v_attn_q_norm_g', 'v_attn_k_norm_g', 'v_attn_w_out']
TWIN_OUTPUTS = ['loss', 'grad_x', 'grad_conv_norm_g', 'grad_conv_w_in', 'grad_conv_w', 'grad_conv_w_out', 'grad_attn_norm_g', 'grad_attn_w_in', 'grad_attn_b_f', 'grad_attn_q_norm_g', 'grad_attn_k_norm_g', 'grad_attn_w_out', 'delta_conv_norm_g', 'delta_conv_w_in', 'delta_conv_w', 'delta_conv_w_out', 'delta_attn_norm_g', 'delta_attn_w_in', 'delta_attn_b_f', 'delta_attn_q_norm_g', 'delta_attn_k_norm_g', 'delta_attn_w_out', 'new_m_conv_norm_g', 'new_m_conv_w_in', 'new_m_conv_w', 'new_m_conv_w_out', 'new_m_attn_norm_g', 'new_m_attn_w_in', 'new_m_attn_b_f', 'new_m_attn_q_norm_g', 'new_m_attn_k_norm_g', 'new_m_attn_w_out', 'new_v_conv_norm_g', 'new_v_conv_w_in', 'new_v_conv_w', 'new_v_conv_w_out', 'new_v_attn_norm_g', 'new_v_attn_w_in', 'new_v_attn_b_f', 'new_v_attn_q_norm_g', 'new_v_attn_k_norm_g', 'new_v_attn_w_out']
TWIN_LEAF_KINDS = {'loss': 'loss', 'grad_x': 'grad_x', 'grad_conv_norm_g': 'grad_w', 'grad_conv_w_in': 'grad_w', 'grad_conv_w': 'grad_w', 'grad_conv_w_out': 'grad_w', 'grad_attn_norm_g': 'grad_w', 'grad_attn_w_in': 'grad_w', 'grad_attn_b_f': 'grad_w', 'grad_attn_q_norm_g': 'grad_w', 'grad_attn_k_norm_g': 'grad_w', 'grad_attn_w_out': 'grad_w', 'delta_conv_norm_g': 'delta_w', 'delta_conv_w_in': 'delta_w', 'delta_conv_w': 'delta_w', 'delta_conv_w_out': 'delta_w', 'delta_attn_norm_g': 'delta_w', 'delta_attn_w_in': 'delta_w', 'delta_attn_b_f': 'delta_w', 'delta_attn_q_norm_g': 'delta_w', 'delta_attn_k_norm_g': 'delta_w', 'delta_attn_w_out': 'delta_w', 'new_m_conv_norm_g': 'new_m', 'new_m_conv_w_in': 'new_m', 'new_m_conv_w': 'new_m', 'new_m_conv_w_out': 'new_m', 'new_m_attn_norm_g': 'new_m', 'new_m_attn_w_in': 'new_m', 'new_m_attn_b_f': 'new_m', 'new_m_attn_q_norm_g': 'new_m', 'new_m_attn_k_norm_g': 'new_m', 'new_m_attn_w_out': 'new_m', 'new_v_conv_norm_g': 'new_v', 'new_v_conv_w_in': 'new_v', 'new_v_conv_w': 'new_v', 'new_v_conv_w_out': 'new_v', 'new_v_attn_norm_g': 'new_v', 'new_v_attn_w_in': 'new_v', 'new_v_attn_b_f': 'new_v', 'new_v_attn_q_norm_g': 'new_v', 'new_v_attn_k_norm_g': 'new_v', 'new_v_attn_w_out': 'new_v'}


def _forward(args):
    return _fwd_reference(*[args[k] for k in FWD_PARAMS])


def _output_shape():
    def fwd():
        inp = _fwd_setup_inputs(0)
        return _fwd_reference(*[inp[k] for k in FWD_PARAMS])
    out = _jax.eval_shape(fwd)
    return out.shape, out.dtype

N_MICROBATCH = 1
ADAM_LR = 0.001
ADAM_B1 = 0.9
ADAM_B2 = 0.999
ADAM_EPS = 1e-08
ADAM_WD = 0.01
ADAM_STEP = 10
PER_EXAMPLE_BATCH_AXIS = {'x': 0, 'loss_target': 0}
SHARED_INPUTS = []
_WEIGHT_DTYPES = {'conv_norm_g': _jnp.float32, 'conv_w_in': _jnp.float32, 'conv_w': _jnp.float32, 'conv_w_out': _jnp.float32, 'attn_norm_g': _jnp.float32, 'attn_w_in': _jnp.float32, 'attn_b_f': _jnp.float32, 'attn_q_norm_g': _jnp.float32, 'attn_k_norm_g': _jnp.float32, 'attn_w_out': _jnp.float32}
MOMENT_SCALE = {'conv_norm_g': 1.828672e+02, 'conv_w_in': 1.315134e+00, 'conv_w': 2.434347e+01, 'conv_w_out': 1.048462e+00, 'attn_norm_g': 1.423116e+01, 'attn_w_in': 1.820583e-01, 'attn_b_f': 1.328993e+02, 'attn_q_norm_g': 3.832679e+01, 'attn_k_norm_g': 3.825051e+01, 'attn_w_out': 1.648612e-01}


def _to_microbatches(a, axis):
    t = _jnp.moveaxis(a, axis, 0)
    t = t.reshape((N_MICROBATCH, t.shape[0] // N_MICROBATCH) + t.shape[1:])
    return _jnp.moveaxis(t, 1, axis + 1)


def setup_inputs(seed: int = 0) -> dict:
    inp = _fwd_setup_inputs(seed)
    key = _jax.random.fold_in(_jax.random.key(seed), 7919)
    shape, _ = _output_shape()
    out = dict(inp)
    out["loss_target"] = _jax.random.normal(_jax.random.fold_in(key, 0), shape, _jnp.float32)
    for i, name in enumerate(TWIN_WEIGHTS):
        w = inp[name].astype(_jnp.float32)
        if MOMENT_SCALE is None:
            s = _jnp.sqrt(_jnp.mean(_jnp.square(w)) + 1e-30)
        else:
            s = MOMENT_SCALE[name]
        km, kv = _jax.random.split(_jax.random.fold_in(key, i + 1))
        out[name] = w
        out["m_" + name] = s * _jax.random.normal(km, w.shape, _jnp.float32)
        out["v_" + name] = (s * s) * _jax.random.uniform(kv, w.shape, _jnp.float32, 0.5, 1.5)
    if N_MICROBATCH > 1:
        for name, axis in PER_EXAMPLE_BATCH_AXIS.items():
            out[name] = _to_microbatches(out[name], axis)
    return {'x': out['x'], 'conv_norm_g': out['conv_norm_g'], 'conv_w_in': out['conv_w_in'], 'conv_w': out['conv_w'], 'conv_w_out': out['conv_w_out'], 'attn_norm_g': out['attn_norm_g'], 'attn_w_in': out['attn_w_in'], 'attn_b_f': out['attn_b_f'], 'attn_q_norm_g': out['attn_q_norm_g'], 'attn_k_norm_g': out['attn_k_norm_g'], 'attn_w_out': out['attn_w_out'], 'loss_target': out['loss_target'], 'm_conv_norm_g': out['m_conv_norm_g'], 'm_conv_w_in': out['m_conv_w_in'], 'm_conv_w': out['m_conv_w'], 'm_conv_w_out': out['m_conv_w_out'], 'm_attn_norm_g': out['m_attn_norm_g'], 'm_attn_w_in': out['m_attn_w_in'], 'm_attn_b_f': out['m_attn_b_f'], 'm_attn_q_norm_g': out['m_attn_q_norm_g'], 'm_attn_k_norm_g': out['m_attn_k_norm_g'], 'm_attn_w_out': out['m_attn_w_out'], 'v_conv_norm_g': out['v_conv_norm_g'], 'v_conv_w_in': out['v_conv_w_in'], 'v_conv_w': out['v_conv_w'], 'v_conv_w_out': out['v_conv_w_out'], 'v_attn_norm_g': out['v_attn_norm_g'], 'v_attn_w_in': out['v_attn_w_in'], 'v_attn_b_f': out['v_attn_b_f'], 'v_attn_q_norm_g': out['v_attn_q_norm_g'], 'v_attn_k_norm_g': out['v_attn_k_norm_g'], 'v_attn_w_out': out['v_attn_w_out']}


def _loss(weights, diff, rest, loss_target):
    with _jax.named_scope("forward"):
        args = {**rest, TWIN_DIFF_INPUT: diff, **{k: w.astype(_WEIGHT_DTYPES[k]) for k, w in weights.items()}}
        y = _forward(args)
    with _jax.named_scope("loss_head"):
        err = _jnp.square(y.astype(_jnp.float32) - loss_target)
        return 0.5 * _jnp.sum(_jnp.mean(err, axis=-1)) if err.ndim else 0.5 * err


def _adamw(w, g, m, v):
    m = ADAM_B1 * m + (1.0 - ADAM_B1) * g
    v = ADAM_B2 * v + (1.0 - ADAM_B2) * _jnp.square(g)
    m_hat = m / (1.0 - ADAM_B1 ** ADAM_STEP)
    v_hat = v / (1.0 - ADAM_B2 ** ADAM_STEP)
    delta = -ADAM_LR * (m_hat / (_jnp.sqrt(v_hat) + ADAM_EPS) + ADAM_WD * w)
    return delta, m, v


def reference(x, conv_norm_g, conv_w_in, conv_w, conv_w_out, attn_norm_g, attn_w_in, attn_b_f, attn_q_norm_g, attn_k_norm_g, attn_w_out, loss_target, m_conv_norm_g, m_conv_w_in, m_conv_w, m_conv_w_out, m_attn_norm_g, m_attn_w_in, m_attn_b_f, m_attn_q_norm_g, m_attn_k_norm_g, m_attn_w_out, v_conv_norm_g, v_conv_w_in, v_conv_w, v_conv_w_out, v_attn_norm_g, v_attn_w_in, v_attn_b_f, v_attn_q_norm_g, v_attn_k_norm_g, v_attn_w_out):
    given = dict(x=x, conv_norm_g=conv_norm_g, conv_w_in=conv_w_in, conv_w=conv_w, conv_w_out=conv_w_out, attn_norm_g=attn_norm_g, attn_w_in=attn_w_in, attn_b_f=attn_b_f, attn_q_norm_g=attn_q_norm_g, attn_k_norm_g=attn_k_norm_g, attn_w_out=attn_w_out, loss_target=loss_target, m_conv_norm_g=m_conv_norm_g, m_conv_w_in=m_conv_w_in, m_conv_w=m_conv_w, m_conv_w_out=m_conv_w_out, m_attn_norm_g=m_attn_norm_g, m_attn_w_in=m_attn_w_in, m_attn_b_f=m_attn_b_f, m_attn_q_norm_g=m_attn_q_norm_g, m_attn_k_norm_g=m_attn_k_norm_g, m_attn_w_out=m_attn_w_out, v_conv_norm_g=v_conv_norm_g, v_conv_w_in=v_conv_w_in, v_conv_w=v_conv_w, v_conv_w_out=v_conv_w_out, v_attn_norm_g=v_attn_norm_g, v_attn_w_in=v_attn_w_in, v_attn_b_f=v_attn_b_f, v_attn_q_norm_g=v_attn_q_norm_g, v_attn_k_norm_g=v_attn_k_norm_g, v_attn_w_out=v_attn_w_out)
    weights = {n: given[n] for n in TWIN_WEIGHTS}
    shared = {n: given[n] for n in SHARED_INPUTS}
    per_example = {n: given[n] for n in ['x']}
    grad_fn = _jax.value_and_grad(_loss, argnums=(0, 1))

    def one_microbatch(ex, loss_target):
        ex = dict(ex)
        diff = ex.pop(TWIN_DIFF_INPUT)
        return grad_fn(weights, diff, {**shared, **ex}, loss_target)

    if N_MICROBATCH == 1:
        loss, (grad_w, grad_x) = one_microbatch(per_example, given["loss_target"])
    else:
        def body(carry, xs):
            loss_sum, grad_sum = carry
            l_k, (gw_k, gx_k) = one_microbatch(xs[0], xs[1])
            with _jax.named_scope("update"):
                return (loss_sum + l_k, _jax.tree.map(_jnp.add, grad_sum, gw_k)), gx_k

        init = (_jnp.zeros((), _jnp.float32), _jax.tree.map(_jnp.zeros_like, weights))
        (loss, grad_w), grad_x = _jax.lax.scan(body, init, (per_example, given["loss_target"]))
    with _jax.named_scope("update"):
        delta_w, new_m, new_v = {}, {}, {}
        for n in TWIN_WEIGHTS:
            delta_w[n], new_m[n], new_v[n] = _adamw(weights[n], grad_w[n], given["m_" + n], given["v_" + n])
    return (loss, grad_x, *[grad_w[n] for n in TWIN_WEIGHTS], *[delta_w[n] for n in TWIN_WEIGHTS],
            *[new_m[n] for n in TWIN_WEIGHTS], *[new_v[n] for n in TWIN_WEIGHTS])
```

```python
import functools
import math

import jax
import jax.numpy as jnp
from jax import lax
from jax.experimental import pallas as pl
from jax.experimental.pallas import tpu as pltpu

F32 = jnp.float32
BF16 = jnp.bfloat16

D = 1024
H = 16
DH = 64
NDEV = 8
RMS_EPS = 1e-6
LANES = 128
HA = 128
TM_FWD = 512
TM_BWD = 256
TQ = 512
TK = 512
CH = 256
PACK_W = 512
PACK_ROWS = 2576
ROWS_W1IN, ROWS_W2IN, ROWS_WOUT = 1024, 1028, 256
ADAM_LR, ADAM_B1, ADAM_B2, ADAM_EPS, ADAM_WD, ADAM_STEP = 0.001, 0.9, 0.999, 1e-08, 0.01, 10
VMEM_LIMIT = 56 * 1024 * 1024
MASKED = -1e30
MESH = pl.DeviceIdType.MESH


def _params(*sem, vmem=VMEM_LIMIT):
    return pltpu.CompilerParams(dimension_semantics=sem or None, vmem_limit_bytes=vmem)


def _const(shape):
    nd = len(shape)
    return pl.BlockSpec(shape, lambda *_: (0,) * nd, pipeline_mode=pl.Buffered(1))


def _rows(tm, n, rev=False, steps=None):
    if rev:
        return pl.BlockSpec((tm, n), lambda i: (steps - 1 - i, 0))
    return pl.BlockSpec((tm, n), lambda i: (i, 0))


def _dot(a, b):
    return jnp.dot(a, b, preferred_element_type=F32)


def _top16(x):
    bits = lax.bitcast_convert_type(x, jnp.uint32) & jnp.uint32(0xFFFF0000)
    return lax.bitcast_convert_type(bits, F32)


def _split2(x):
    hi = _top16(x)
    return hi.astype(BF16), (x - hi).astype(BF16)


def _split3(x):
    hi = _top16(x)
    r = x - hi
    mid = _top16(r)
    return hi.astype(BF16), mid.astype(BF16), (r - mid).astype(BF16)


def _seg_sum(a, e):
    hi, lo = _split2(a)
    return _dot(hi, e) + _dot(lo, e)


def _seg_bcast(s, et):
    hi, mid, lo = _split3(s)
    return _dot(hi, et) + _dot(mid, et) + _dot(lo, et)


def _tri_sum(t, v):
    hi, mid, lo = _split3(v)
    return _dot(t, hi) + _dot(t, mid) + _dot(t, lo)


def _sigmoid(z):
    return 1.0 / (1.0 + jnp.exp(-z))


def _place():
    return lax.axis_index("x"), lax.axis_index("y"), lax.axis_index("c")


def _all_gather(xb, name):
    r, c_ = xb.shape

    def body(x_ref, out_ref, send_sems, recv_sems, local_sem):
        x, y, c = _place()
        me, sibling = (x, y, c), (x, y, 1 - c)
        chips = [(1 - x, y), (x, 1 - y), (1 - x, 1 - y)]

        def slab(px, py, pc):
            return out_ref.at[4 * px + 2 * py + pc]

        def copy(k, block, to, src=None):
            return pltpu.make_async_remote_copy(
                src_ref=slab(*block) if src is None else src, dst_ref=slab(*block),
                send_sem=send_sems.at[k], recv_sem=recv_sems.at[k], device_id=to, device_id_type=MESH)

        mine = pltpu.make_async_copy(x_ref, slab(*me), local_sem)
        mine.start()
        first = [copy(0, me, sibling, src=x_ref)]
        first += [copy(1 + j, me, (*chip, c), src=x_ref) for j, chip in enumerate(chips)]
        for cp in first:
            cp.start()
        passed = [copy(4 + j, (*chip, c), sibling) for j, chip in enumerate(chips)]
        for j, chip in enumerate(chips):
            copy(1 + j, (*chip, c), me).wait_recv()
            passed[j].start()
        copy(0, sibling, me).wait_recv()
        for j, chip in enumerate(chips):
            copy(4 + j, (*chip, 1 - c), me).wait_recv()
        for cp in first + passed:
            cp.wait_send()
        mine.wait()

    return pl.pallas_call(
        body, name=name,
        out_shape=jax.ShapeDtypeStruct((NDEV, r, c_), xb.dtype),
        in_specs=[pl.BlockSpec(memory_space=pl.ANY)],
        out_specs=pl.BlockSpec(memory_space=pl.ANY),
        scratch_shapes=[pltpu.SemaphoreType.DMA((7,)), pltpu.SemaphoreType.DMA((7,)), pltpu.SemaphoreType.DMA],
    )(xb)


def _exchange_blocks(g, name):
    _, r, c_ = g.shape

    def body(g_ref, out_ref, send_sems, recv_sems, local_sem):
        x, y, c = _place()
        me = 4 * x + 2 * y + c
        mine = pltpu.make_async_copy(g_ref.at[me], out_ref.at[0], local_sem)
        mine.start()
        copies = []
        for k in range(1, NDEV):
            px = 1 - x if k & 4 else x
            py = 1 - y if k & 2 else y
            pc = 1 - c if k & 1 else c
            cp = pltpu.make_async_remote_copy(
                src_ref=g_ref.at[4 * px + 2 * py + pc], dst_ref=out_ref.at[k],
                send_sem=send_sems.at[k - 1], recv_sem=recv_sems.at[k - 1],
                device_id=(px, py, pc), device_id_type=MESH)
            cp.start()
            copies.append(cp)
        for cp in copies:
            cp.wait()
        mine.wait()

    return pl.pallas_call(
        body, name=name,
        out_shape=jax.ShapeDtypeStruct((NDEV, r, c_), g.dtype),
        in_specs=[pl.BlockSpec(memory_space=pl.ANY)],
        out_specs=pl.BlockSpec(memory_space=pl.ANY),
        scratch_shapes=[pltpu.SemaphoreType.DMA((7,)), pltpu.SemaphoreType.DMA((7,)), pltpu.SemaphoreType.DMA],
    )(g)


def _conv_fwd(x, g1, w1in, cw, w1out):
    s = x.shape[0]
    tm = min(TM_FWD, s)
    steps = s // tm

    def body(x_ref, g_ref, win_ref, cw_ref, wout_ref, x1_ref, h1_ref, p1_ref, yg_ref, tail_ref):
        @pl.when(pl.program_id(0) == 0)
        def _():
            tail_ref[...] = jnp.zeros_like(tail_ref)

        xv = x_ref[...]
        inv = lax.rsqrt(jnp.mean(xv * xv, axis=-1, keepdims=True) + RMS_EPS)
        h = (xv * inv * g_ref[...]).astype(BF16)
        h1_ref[...] = h
        row = lax.broadcasted_iota(jnp.int32, (tm, CH), 0)
        for ci in range(D // CH):
            lo, hi = ci * CH, (ci + 1) * CH
            parts = []
            for k in range(4):
                pk = _dot(h, win_ref[:, k * D + lo:k * D + hi]).astype(BF16)
                p1_ref[:, k * D + lo:k * D + hi] = pk
                parts.append(pk.astype(F32))
            b, c, xin, z = parts
            u = c * xin
            t6 = tail_ref[6:7, lo:hi]
            t7 = tail_ref[7:8, lo:hi]
            u1 = jnp.where(row == 0, t7, pltpu.roll(u, 1, 0))
            u2 = jnp.where(row == 0, t6, jnp.where(row == 1, t7, pltpu.roll(u, 2, 0)))
            tail_ref[:, lo:hi] = u[tm - 8:, :]
            y = cw_ref[2:3, lo:hi] * u + cw_ref[1:2, lo:hi] * u1 + cw_ref[0:1, lo:hi] * u2
            yg_ref[:, lo:hi] = (b * y * (z * _sigmoid(z))).astype(BF16)
        x1_ref[...] = xv + _dot(yg_ref[...], wout_ref[...])

    return pl.pallas_call(
        body, name="conv_fwd", grid=(steps,),
        in_specs=[_rows(tm, D), _const((1, D)), _const((D, 4 * D)), _const((8, D)), _const((D, D))],
        out_specs=[_rows(tm, D), _rows(tm, D), _rows(tm, 4 * D), _rows(tm, D)],
        out_shape=[jax.ShapeDtypeStruct((s, D), F32), jax.ShapeDtypeStruct((s, D), BF16),
                   jax.ShapeDtypeStruct((s, 4 * D), BF16), jax.ShapeDtypeStruct((s, D), BF16)],
        scratch_shapes=[pltpu.VMEM((8, D), F32)],
        compiler_params=_params("arbitrary"),
    )(x, g1, w1in, cw, w1out)


def _attn_proj_fwd(x1, g2, w2in, wf, bf, qg, kg, e, et, tril):
    s = x1.shape[0]
    tm = min(TM_FWD, s)
    steps = s // tm

    def body(x_ref, g_ref, win_ref, wf_ref, bf_ref, qg_ref, kg_ref, e_ref, et_ref, tril_ref,
             h2_ref, qp_ref, kp_ref, qs_ref, kn_ref, v_ref, z_ref, iq_ref, ik_ref, c_ref, fl_ref, carry_ref):
        @pl.when(pl.program_id(0) == 0)
        def _():
            carry_ref[...] = jnp.zeros_like(carry_ref)

        xv = x_ref[...]
        inv = lax.rsqrt(jnp.mean(xv * xv, axis=-1, keepdims=True) + RMS_EPS)
        h = (xv * inv * g_ref[...]).astype(BF16)
        h2_ref[...] = h

        def normed(col, pre_ref, inv_ref, gain_ref):
            pre = _dot(h, win_ref[:, col * D:(col + 1) * D]).astype(BF16)
            pre_ref[...] = pre
            t = pre.astype(F32)
            invh = lax.rsqrt(_seg_sum(t * t, e_ref[...]) * (1.0 / DH) + RMS_EPS)
            inv_ref[...] = invh
            return t * _seg_bcast(invh, et_ref[...]) * gain_ref[...]

        qs_ref[...] = (normed(0, qp_ref, iq_ref, qg_ref) * (1.0 / math.sqrt(DH))).astype(BF16)
        kn_ref[...] = normed(1, kp_ref, ik_ref, kg_ref).astype(BF16)
        v_ref[...] = _dot(h, win_ref[:, 2 * D:3 * D]).astype(BF16)
        z_ref[...] = _dot(h, win_ref[:, 3 * D:4 * D]).astype(BF16)

        fl = _dot(h, wf_ref[...]) + bf_ref[...]
        fl_ref[...] = fl
        ex = jnp.exp(-jnp.abs(fl))
        up = 1.0 + ex
        log1p = jnp.where(up == 1.0, ex, jnp.log(up) * ex / (up - 1.0))
        lane = lax.broadcasted_iota(jnp.int32, (tm, LANES), 1)
        logf = jnp.where(lane < H, jnp.minimum(fl, 0.0) - log1p, 0.0)
        c_ref[...] = _tri_sum(tril_ref[...], logf) + carry_ref[0:1, :]
        carry_ref[0:1, :] = c_ref[tm - 1:tm, :]

    row_bf = lambda: _rows(tm, D)
    row_sm = lambda: _rows(tm, LANES)
    return pl.pallas_call(
        body, name="attn_proj_fwd", grid=(steps,),
        in_specs=[_rows(tm, D), _const((1, D)), _const((D, 4 * D)), _const((D, LANES)), _const((1, LANES)),
                  _const((1, D)), _const((1, D)), _const((D, LANES)), _const((LANES, D)), _const((tm, tm))],
        out_specs=[row_bf() for _ in range(7)] + [row_sm() for _ in range(4)],
        out_shape=[jax.ShapeDtypeStruct((s, D), BF16)] * 7 + [jax.ShapeDtypeStruct((s, LANES), F32)] * 4,
        scratch_shapes=[pltpu.VMEM((8, LANES), F32)],
        compiler_params=_params("arbitrary"),
    )(x1, g2, w2in, wf, bf, qg, kg, e, et, tril)


def _flash_fwd(qa, kat, va):
    h_, nq, tq, _ = qa.shape
    nk, tk = kat.shape[1], kat.shape[3]

    def body(q_ref, kt_ref, v_ref, o_ref, m_ref, acc_ref):
        i = pl.program_id(1)
        q = q_ref[...]
        m_ref[...] = jnp.full_like(m_ref, MASKED)
        acc_ref[...] = jnp.zeros_like(acc_ref)

        def step(j, masked):
            z = _dot(q, kt_ref[j])
            if masked:
                rows = lax.broadcasted_iota(jnp.int32, (tq, tk), 0)
                cols = lax.broadcasted_iota(jnp.int32, (tq, tk), 1)
                z = jnp.where(rows >= cols, z, MASKED)
            m_old = m_ref[...]
            m_new = jnp.maximum(m_old, jnp.max(z, axis=1, keepdims=True))
            p = jnp.exp(z - m_new)
            acc_ref[...] = jnp.exp(m_old - m_new) * acc_ref[...] + _dot(p.astype(BF16), v_ref[j])
            m_ref[...] = m_new

        def loop_body(j, carry):
            step(j, False)
            return carry

        lax.fori_loop(0, i, loop_body, 0)
        step(i, True)
        acc = acc_ref[...]
        lane = lax.broadcasted_iota(jnp.int32, (tq, HA), 1)
        l = jnp.sum(jnp.where(lane == DH, acc, 0.0), axis=1, keepdims=True)
        o_ref[...] = jnp.where(lane < DH, acc / l, m_ref[...] + jnp.log(l))

    return pl.pallas_call(
        body, name="flash_fwd", grid=(h_, nq),
        in_specs=[pl.BlockSpec((None, None, tq, HA), lambda h, i: (h, i, 0, 0)),
                  pl.BlockSpec((None, nk, HA, tk), lambda h, i: (h, 0, 0, 0)),
                  pl.BlockSpec((None, nk, tk, HA), lambda h, i: (h, 0, 0, 0))],
        out_specs=pl.BlockSpec((None, None, tq, HA), lambda h, i: (h, i, 0, 0)),
        out_shape=jax.ShapeDtypeStruct((h_, nq, tq, HA), F32),
        scratch_shapes=[pltpu.VMEM((tq, 1), F32), pltpu.VMEM((tq, HA), F32)],
        compiler_params=_params("arbitrary", "arbitrary"),
    )(qa, kat, va)


def _attn_out(o, z, x1, tgt, w2out, w2out_t, e):
    s = o.shape[0]
    tm = min(TM_BWD, s)
    steps = s // tm

    def body(o_ref, z_ref, x1_ref, t_ref, w_ref, wt_ref, e_ref, og_ref, dx2_ref, do_ref, dz_ref, dl_ref, loss_ref):
        @pl.when(pl.program_id(0) == 0)
        def _():
            loss_ref[...] = jnp.zeros_like(loss_ref)

        ov = o_ref[...]
        zv = z_ref[...].astype(F32)
        sg = _sigmoid(zv)
        sil = zv * sg
        og = (ov * sil).astype(BF16)
        og_ref[...] = og
        err = x1_ref[...] + _dot(og, w_ref[...]) - t_ref[...]
        loss_ref[...] += (0.5 / D) * jnp.sum(err * err, axis=0, keepdims=True)
        dx2 = err * (1.0 / D)
        dx2_ref[...] = dx2
        dog = _dot(dx2.astype(BF16), wt_ref[...])
        do = (dog * sil).astype(BF16)
        do_ref[...] = do
        dz_ref[...] = (dog * ov * (sg * (1.0 + zv * (1.0 - sg)))).astype(BF16)
        dl_ref[...] = _seg_sum(do.astype(F32) * ov, e_ref[...])

    return pl.pallas_call(
        body, name="attn_out", grid=(steps,),
        in_specs=[_rows(tm, D), _rows(tm, D), _rows(tm, D), _rows(tm, D), _const((D, D)), _const((D, D)),
                  _const((D, LANES))],
        out_specs=[_rows(tm, D), _rows(tm, D), _rows(tm, D), _rows(tm, D), _rows(tm, LANES),
                   pl.BlockSpec((1, D), lambda i: (0, 0))],
        out_shape=[jax.ShapeDtypeStruct((s, D), BF16), jax.ShapeDtypeStruct((s, D), F32),
                   jax.ShapeDtypeStruct((s, D), BF16), jax.ShapeDtypeStruct((s, D), BF16),
                   jax.ShapeDtypeStruct((s, LANES), F32), jax.ShapeDtypeStruct((1, D), F32)],
        compiler_params=_params("arbitrary"),
    )(o, z, x1, tgt, w2out, w2out_t, e)


def _flash_bwd(qab, qabt, doa, doat, kat, ka, vat):
    h_, nq, tq, _ = qab.shape
    nk, tk = kat.shape[1], kat.shape[3]
    s = nk * tk

    def body(qa_ref, qat_ref, da_ref, dat_ref, kat_ref, ka_ref, vat_ref, dq_hbm, dkt_ref, dvt_ref, dq_acc, sem):
        hh = pl.program_id(0)
        j = pl.program_id(1)

        @pl.when(j == 0)
        def _():
            dq_acc[...] = jnp.zeros_like(dq_acc)

        dkt_ref[...] = jnp.zeros_like(dkt_ref)
        dvt_ref[...] = jnp.zeros_like(dvt_ref)
        kat_v = kat_ref[...]
        ka_v = ka_ref[...]
        vat_v = vat_ref[...]

        def step(i, masked):
            zz = _dot(qa_ref[i], kat_v)
            if masked:
                rows = lax.broadcasted_iota(jnp.int32, (tq, tk), 0)
                cols = lax.broadcasted_iota(jnp.int32, (tq, tk), 1)
                zz = jnp.where(rows >= cols, zz, MASKED)
            p = jnp.exp(zz)
            ds = (p * _dot(da_ref[i], vat_v)).astype(BF16)
            pb = p.astype(BF16)
            dq_acc[i] += _dot(ds, ka_v)
            dvt_ref[...] += _dot(dat_ref[i], pb)
            dkt_ref[...] += _dot(qat_ref[i], ds)

        step(j, True)

        def loop_body(i, carry):
            step(i, False)
            return carry

        lax.fori_loop(j + 1, nq, loop_body, 0)

        @pl.when(j == nk - 1)
        def _():
            cp = pltpu.make_async_copy(dq_acc, dq_hbm.at[hh], sem)
            cp.start()
            cp.wait()

    whole = lambda a, b: pl.BlockSpec((None, nq, a, b), lambda h, j: (h, 0, 0, 0), pipeline_mode=pl.Buffered(1))
    return pl.pallas_call(
        body, name="flash_bwd", grid=(h_, nk),
        in_specs=[whole(tq, HA), whole(HA, tq), whole(tq, HA), whole(HA, tq),
                  pl.BlockSpec((None, None, HA, tk), lambda h, j: (h, j, 0, 0)),
                  pl.BlockSpec((None, None, tk, HA), lambda h, j: (h, j, 0, 0)),
                  pl.BlockSpec((None, None, HA, tk), lambda h, j: (h, j, 0, 0))],
        out_specs=[pl.BlockSpec(memory_space=pl.ANY),
                   pl.BlockSpec((None, HA, tk), lambda h, j: (h, 0, j)),
                   pl.BlockSpec((None, HA, tk), lambda h, j: (h, 0, j))],
        out_shape=[jax.ShapeDtypeStruct((h_, nq, tq, HA), F32), jax.ShapeDtypeStruct((h_, HA, s), F32),
                   jax.ShapeDtypeStruct((h_, HA, s), F32)],
        scratch_shapes=[pltpu.VMEM((nq, tq, HA), F32), pltpu.SemaphoreType.DMA],
        compiler_params=_params("arbitrary", "arbitrary"),
    )(qab, qabt, doa, doat, kat, ka, vat)


def _attn_proj_bwd(dqs, dkn, dv, dz, dcs, qpre, kpre, invq, invk, fl, x1, dx2, g2, qg, kg, w2in_t, wf_t, e, et,
                   triu, fold):
    s = x1.shape[0]
    tm = min(TM_BWD, s)
    steps = s // tm

    def body(dqs_ref, dkn_ref, dv_ref, dz_ref, dcs_ref, qp_ref, kp_ref, iq_ref, ik_ref, fl_ref, x1_ref, dx2_ref,
             g2_ref, qg_ref, kg_ref, wt_ref, wft_ref, e_ref, et_ref, triu_ref, fold_ref,
             dx1_ref, dp_ref, df_ref, dg2_ref, dqg_ref, dkg_ref, dbf_ref, carry_ref, rc_ref, qcol_ref, kcol_ref):
        step = pl.program_id(0)

        @pl.when(step == 0)
        def _():
            carry_ref[...] = jnp.zeros_like(carry_ref)
            dg2_ref[...] = jnp.zeros_like(dg2_ref)
            dbf_ref[...] = jnp.zeros_like(dbf_ref)
            qcol_ref[...] = jnp.zeros_like(qcol_ref)
            kcol_ref[...] = jnp.zeros_like(kcol_ref)
            dqg_ref[...] = jnp.zeros_like(dqg_ref)
            dkg_ref[...] = jnp.zeros_like(dkg_ref)

        rc_ref[...] = _tri_sum(triu_ref[...], dcs_ref[...]) + carry_ref[0:1, :]
        carry_ref[0:1, :] = rc_ref[0:1, :]
        df = rc_ref[...] * _sigmoid(-fl_ref[...])
        dfb = df.astype(BF16)
        df_ref[...] = dfb
        dbf_ref[...] += jnp.sum(df, axis=0, keepdims=True)

        def norm_bwd(dn, pre_ref, inv_ref, gain_ref, col_ref):
            pre = pre_ref[...].astype(F32)
            invh = inv_ref[...]
            invb = _seg_bcast(invh, et_ref[...])
            col_ref[...] += jnp.sum(dn * pre * invb, axis=0, keepdims=True)
            gd = dn * gain_ref[...]
            mean = _seg_sum(gd * pre, e_ref[...]) * (1.0 / DH)
            return invb * gd - pre * _seg_bcast(mean * invh * invh * invh, et_ref[...])

        dq = norm_bwd(dqs_ref[...] * (1.0 / math.sqrt(DH)), qp_ref, iq_ref, qg_ref, qcol_ref).astype(BF16)
        dp_ref[:, 0:D] = dq
        dh = _dot(dq, wt_ref[0:D, :])
        dk = norm_bwd(dkn_ref[...], kp_ref, ik_ref, kg_ref, kcol_ref).astype(BF16)
        dp_ref[:, D:2 * D] = dk
        dh += _dot(dk, wt_ref[D:2 * D, :])
        dvb = dv_ref[...].astype(BF16)
        dp_ref[:, 2 * D:3 * D] = dvb
        dh += _dot(dvb, wt_ref[2 * D:3 * D, :])
        dzb = dz_ref[...]
        dp_ref[:, 3 * D:4 * D] = dzb
        dh += _dot(dzb, wt_ref[3 * D:4 * D, :])
        dh += _dot(dfb, wft_ref[...])

        xv = x1_ref[...]
        inv = lax.rsqrt(jnp.mean(xv * xv, axis=-1, keepdims=True) + RMS_EPS)
        dg2_ref[...] += jnp.sum(dh * xv * inv, axis=0, keepdims=True)
        gh = dh * g2_ref[...]
        dx1_ref[...] = dx2_ref[...] + inv * gh - xv * (inv * inv * inv * jnp.mean(gh * xv, axis=-1, keepdims=True))

        @pl.when(step == steps - 1)
        def _():
            dqg_ref[...] = _fold_heads(qcol_ref[...], fold_ref[...])
            dkg_ref[...] = _fold_heads(kcol_ref[...], fold_ref[...])

    rr = lambda n: _rows(tm, n, rev=True, steps=steps)
    acc = lambda n, r=1: pl.BlockSpec((r, n), lambda i: (0, 0))
    return pl.pallas_call(
        body, name="attn_proj_bwd", grid=(steps,),
        in_specs=[rr(D), rr(D), rr(D), rr(D), rr(LANES), rr(D), rr(D), rr(LANES), rr(LANES), rr(LANES), rr(D), rr(D),
                  _const((1, D)), _const((1, D)), _const((1, D)), _const((4 * D, D)), _const((LANES, D)),
                  _const((D, LANES)), _const((LANES, D)), _const((tm, tm)), _const((D, LANES))],
        out_specs=[rr(D), rr(4 * D), rr(LANES), acc(D), acc(LANES, 8), acc(LANES, 8), acc(LANES)],
        out_shape=[jax.ShapeDtypeStruct((s, D), F32), jax.ShapeDtypeStruct((s, 4 * D), BF16),
                   jax.ShapeDtypeStruct((s, LANES), BF16), jax.ShapeDtypeStruct((1, D), F32),
                   jax.ShapeDtypeStruct((8, LANES), F32), jax.ShapeDtypeStruct((8, LANES), F32),
                   jax.ShapeDtypeStruct((1, LANES), F32)],
        scratch_shapes=[pltpu.VMEM((8, LANES), F32), pltpu.VMEM((tm, LANES), F32), pltpu.VMEM((1, D), F32),
                        pltpu.VMEM((1, D), F32)],
        compiler_params=_params("arbitrary"),
    )(dqs, dkn, dv, dz, dcs, qpre, kpre, invq, invk, fl, x1, dx2, g2, qg, kg, w2in_t, wf_t, e, et, triu, fold)


def _fold_heads(col, fold):
    hi, mid, lo = _split3(jnp.broadcast_to(col, (8, D)))
    return _dot(hi, fold) + _dot(mid, fold) + _dot(lo, fold)


def _conv_bwd(dx1, p1, x, g1, cw, w1out_t, w1in_t):
    s = x.shape[0]
    tm = min(TM_BWD, s)
    steps = s // tm
    halo = tm // 8

    def body(dx1_ref, b_ref, c_ref, xi_ref, z_ref, ch_ref, xh_ref, x_ref, g_ref, cw_ref, wot_ref, wit_ref,
             gx_ref, dp_ref, dcw_ref, dg1_ref, head_ref):
        step = pl.program_id(0)

        @pl.when(step == 0)
        def _():
            head_ref[...] = jnp.zeros_like(head_ref)
            dcw_ref[...] = jnp.zeros_like(dcw_ref)
            dg1_ref[...] = jnp.zeros_like(dg1_ref)

        first_tile = step == steps - 1
        dx1 = dx1_ref[...]
        row = lax.broadcasted_iota(jnp.int32, (tm, CH), 0)
        dyb = dx1.astype(BF16)
        for ci in range(D // CH):
            lo, hi = ci * CH, (ci + 1) * CH
            dyg = _dot(dyb, wot_ref[:, lo:hi])
            b = b_ref[:, lo:hi].astype(F32)
            c = c_ref[:, lo:hi].astype(F32)
            xin = xi_ref[:, lo:hi].astype(F32)
            z = z_ref[:, lo:hi].astype(F32)
            u = c * xin
            t6 = jnp.where(first_tile, 0.0, ch_ref[6:7, lo:hi].astype(F32) * xh_ref[6:7, lo:hi].astype(F32))
            t7 = jnp.where(first_tile, 0.0, ch_ref[7:8, lo:hi].astype(F32) * xh_ref[7:8, lo:hi].astype(F32))
            u1 = jnp.where(row == 0, t7, pltpu.roll(u, 1, 0))
            u2 = jnp.where(row == 0, t6, jnp.where(row == 1, t7, pltpu.roll(u, 2, 0)))
            w0, w1, w2 = cw_ref[0:1, lo:hi], cw_ref[1:2, lo:hi], cw_ref[2:3, lo:hi]
            y = w2 * u + w1 * u1 + w0 * u2
            sg = _sigmoid(z)
            sil = z * sg
            dp_ref[:, lo:hi] = (dyg * y * sil).astype(BF16)
            dy = dyg * b * sil
            dp_ref[:, 3 * D + lo:3 * D + hi] = (dyg * b * y * (sg * (1.0 + z * (1.0 - sg)))).astype(BF16)
            dcw_ref[2:3, lo:hi] += jnp.sum(dy * u, axis=0, keepdims=True)
            dcw_ref[1:2, lo:hi] += jnp.sum(dy * u1, axis=0, keepdims=True)
            dcw_ref[0:1, lo:hi] += jnp.sum(dy * u2, axis=0, keepdims=True)
            n0 = head_ref[0:1, lo:hi]
            n1 = head_ref[1:2, lo:hi]
            dyn1 = jnp.where(row == tm - 1, n0, pltpu.roll(dy, tm - 1, 0))
            dyn2 = jnp.where(row == tm - 2, n0, jnp.where(row == tm - 1, n1, pltpu.roll(dy, tm - 2, 0)))
            head_ref[:, lo:hi] = dy[0:8, :]
            du = w2 * dy + w1 * dyn1 + w0 * dyn2
            dp_ref[:, D + lo:D + hi] = (du * xin).astype(BF16)
            dp_ref[:, 2 * D + lo:2 * D + hi] = (du * c).astype(BF16)
        dh = _dot(dp_ref[:, 0:D], wit_ref[0:D, :])
        for k in range(1, 4):
            dh += _dot(dp_ref[:, k * D:(k + 1) * D], wit_ref[k * D:(k + 1) * D, :])
        xv = x_ref[...]
        inv = lax.rsqrt(jnp.mean(xv * xv, axis=-1, keepdims=True) + RMS_EPS)
        dg1_ref[...] += jnp.sum(dh * xv * inv, axis=0, keepdims=True)
        gh = dh * g_ref[...]
        gx_ref[...] = dx1 + inv * gh - xv * (inv * inv * inv * jnp.mean(gh * xv, axis=-1, keepdims=True))

    rr = lambda n: _rows(tm, n, rev=True, steps=steps)
    part = lambda k: pl.BlockSpec((tm, D), lambda i: (steps - 1 - i, k))
    prev8 = lambda k: pl.BlockSpec((8, D), lambda i: (jnp.maximum((steps - 1 - i) * halo - 1, 0), k))
    return pl.pallas_call(
        body, name="conv_bwd", grid=(steps,),
        in_specs=[rr(D), part(0), part(1), part(2), part(3), prev8(1), prev8(2), rr(D), _const((1, D)),
                  _const((8, D)), _const((D, D)), _const((4 * D, D))],
        out_specs=[rr(D), rr(4 * D), pl.BlockSpec((8, D), lambda i: (0, 0)), pl.BlockSpec((1, D), lambda i: (0, 0))],
        out_shape=[jax.ShapeDtypeStruct((s, D), F32), jax.ShapeDtypeStruct((s, 4 * D), BF16),
                   jax.ShapeDtypeStruct((8, D), F32), jax.ShapeDtypeStruct((1, D), F32)],
        scratch_shapes=[pltpu.VMEM((8, D), F32)],
        compiler_params=_params("arbitrary"),
    )(dx1, p1, p1, p1, p1, p1, p1, x, g1, cw, w1out_t, w1in_t)


def _wgrad(at, g, name):
    k, s = at.shape
    n = g.shape[1]
    bn = min(n, 1024)
    ts = min(512, s)

    def body(at_ref, g_ref, out_ref, acc_ref):
        t = pl.program_id(1)

        @pl.when(t == 0)
        def _():
            acc_ref[...] = jnp.zeros_like(acc_ref)

        acc_ref[...] += _dot(at_ref[...], g_ref[...].astype(BF16))

        @pl.when(t == s // ts - 1)
        def _():
            out_ref[...] = acc_ref[...]

    return pl.pallas_call(
        body, name=name, grid=(n // bn, s // ts),
        in_specs=[pl.BlockSpec((k, ts), lambda j, t: (0, t)), pl.BlockSpec((ts, bn), lambda j, t: (t, j))],
        out_specs=pl.BlockSpec((k, bn), lambda j, t: (0, j)),
        out_shape=jax.ShapeDtypeStruct((k, n), F32),
        scratch_shapes=[pltpu.VMEM((k, bn), F32)],
        compiler_params=_params("arbitrary", "arbitrary"),
    )(at, g)


def _adamw_math(w, g, m, v):
    m = ADAM_B1 * m + (1.0 - ADAM_B1) * g
    v = ADAM_B2 * v + (1.0 - ADAM_B2) * (g * g)
    m_hat = m / (1.0 - ADAM_B1 ** ADAM_STEP)
    v_hat = v / (1.0 - ADAM_B2 ** ADAM_STEP)
    delta = -ADAM_LR * (m_hat / (jnp.sqrt(v_hat) + ADAM_EPS) + ADAM_WD * w)
    return delta, m, v


def _adamw_big(parts, w, m, v):
    _, r, c_ = parts.shape
    rb = 368
    assert r % rb == 0

    def body(p_ref, w_ref, m_ref, v_ref, g_ref, d_ref, mo_ref, vo_ref):
        g = p_ref[0]
        for k in range(1, NDEV):
            g = g + p_ref[k]
        g_ref[...] = g
        d_ref[...], mo_ref[...], vo_ref[...] = _adamw_math(w_ref[...], g, m_ref[...], v_ref[...])

    blk = pl.BlockSpec((rb, c_), lambda i: (i, 0))
    return pl.pallas_call(
        body, name="adamw_big", grid=(r // rb,),
        in_specs=[pl.BlockSpec((NDEV, rb, c_), lambda i: (0, i, 0)), blk, blk, blk],
        out_specs=[blk, blk, blk, blk],
        out_shape=[jax.ShapeDtypeStruct((r, c_), F32)] * 4,
        compiler_params=_params("arbitrary"),
    )(parts, w, m, v)


def _sum_parts(parts, loss_row):
    _, r, c_ = parts.shape

    def body(p_ref, o_ref, l_ref):
        g = p_ref[0]
        for k in range(1, NDEV):
            g = g + p_ref[k]
        o_ref[...] = g
        per_row = jnp.sum(o_ref[loss_row:loss_row + 8, :], axis=1, keepdims=True)
        l_ref[...] = jnp.broadcast_to(jnp.sum(per_row, axis=0, keepdims=True), (8, c_))

    return pl.pallas_call(body, name="sum_small", out_shape=[jax.ShapeDtypeStruct((r, c_), F32),
                                                             jax.ShapeDtypeStruct((8, c_), F32)])(parts)


def _adamw_small(g, w, m, v):
    def body(g_ref, w_ref, m_ref, v_ref, d_ref, mo_ref, vo_ref):
        d_ref[...], mo_ref[...], vo_ref[...] = _adamw_math(w_ref[...], g_ref[...], m_ref[...], v_ref[...])

    return pl.pallas_call(body, name="adamw_small", out_shape=[jax.ShapeDtypeStruct(g.shape, F32)] * 3)(g, w, m, v)


def _pack_big(w1in, w2in, w1out, w2out, dtype):
    parts = [w1in.reshape(ROWS_W1IN, PACK_W), w2in.reshape(ROWS_W2IN, PACK_W), w1out.reshape(ROWS_WOUT, PACK_W),
             w2out.reshape(ROWS_WOUT, PACK_W)]
    used = ROWS_W1IN + ROWS_W2IN + 2 * ROWS_WOUT
    parts.append(jnp.zeros((PACK_ROWS - used, PACK_W), parts[0].dtype))
    return jnp.concatenate(parts, axis=0).astype(dtype)


def _unpack_big(p):
    a, b, c = ROWS_W1IN, ROWS_W1IN + ROWS_W2IN, ROWS_W1IN + ROWS_W2IN + ROWS_WOUT
    return (p[:a].reshape(1, D, 512), p[a:b].reshape(1, D, 514), p[b:c].reshape(1, 128, D),
            p[c:c + ROWS_WOUT].reshape(1, 128, D))


def _pad_lanes(a):
    return jnp.pad(a, ((0, 0), (0, LANES - a.shape[1])))


def _heads(a, s):
    return a.reshape(s, H, DH).transpose(1, 0, 2)


def _unheads(a, s):
    return a.transpose(1, 0, 2).reshape(s, D)


def _aug(cols, s):
    used = sum(c.shape[-1] for c in cols)
    return jnp.concatenate(cols + [jnp.zeros((H, s, HA - used), BF16)], axis=-1)


def _terms(v):
    return [t[..., None] for t in _split3(v)]


def _tiles(a, t):
    return a.reshape(H, a.shape[1] // t, t, HA)


def _tiles_t(a, t):
    return _tiles(a, t).transpose(0, 1, 3, 2)


def kernel(x, conv_norm_g, conv_w_in, conv_w, conv_w_out, attn_norm_g, attn_w_in, attn_b_f, attn_q_norm_g, attn_k_norm_g, attn_w_out, loss_target, m_conv_norm_g, m_conv_w_in, m_conv_w, m_conv_w_out, m_attn_norm_g, m_attn_w_in, m_attn_b_f, m_attn_q_norm_g, m_attn_k_norm_g, m_attn_w_out, v_conv_norm_g, v_conv_w_in, v_conv_w, v_conv_w_out, v_attn_norm_g, v_attn_w_in, v_attn_b_f, v_attn_q_norm_g, v_attn_k_norm_g, v_attn_w_out):
    s = x.shape[1]
    tq = min(TQ, s)
    tmf, tmb = min(TM_FWD, s), min(TM_BWD, s)
    me = 4 * lax.axis_index("x") + 2 * lax.axis_index("y") + lax.axis_index("c")
    xv, tgt = x[0], loss_target[0]

    wg = _all_gather(_pack_big(conv_w_in[0], attn_w_in[0], conv_w_out[0], attn_w_out[0], BF16), "gather_weights")
    small_w = jnp.concatenate([conv_w[0], attn_norm_g, jnp.zeros((4, 128), F32)], axis=0)
    sg_ = _all_gather(small_w, "gather_small_weights")
    a, b, c = ROWS_W1IN, ROWS_W1IN + ROWS_W2IN, ROWS_W1IN + ROWS_W2IN + ROWS_WOUT
    w1in = wg[:, :a].transpose(1, 0, 2).reshape(D, 4 * D)
    w2all = wg[:, a:b].reshape(NDEV, D, 514).transpose(1, 0, 2).reshape(D, 4 * D + H)
    w2in, wf = w2all[:, :4 * D], _pad_lanes(w2all[:, 4 * D:])
    w1out = wg[:, b:c].reshape(D, D)
    w2out = wg[:, c:c + ROWS_WOUT].reshape(D, D)
    cw = jnp.concatenate([sg_[:, 0:3, :].transpose(1, 0, 2).reshape(3, D), jnp.zeros((5, D), F32)], axis=0)
    g2 = sg_[:, 3, :].reshape(1, D)
    qg_t, kg_t = jnp.tile(attn_q_norm_g, (1, H)), jnp.tile(attn_k_norm_g, (1, H))
    bf = _pad_lanes(attn_b_f)

    e = (jnp.arange(D)[:, None] // DH == jnp.arange(LANES)[None, :]).astype(BF16)
    fold = (jnp.arange(D)[:, None] % DH == jnp.arange(LANES)[None, :]).astype(BF16)
    tril = (jnp.arange(tmf)[:, None] >= jnp.arange(tmf)[None, :]).astype(BF16)
    triu = (jnp.arange(tmb)[:, None] <= jnp.arange(tmb)[None, :]).astype(BF16)

    x1, h1, p1, yg = _conv_fwd(xv, conv_norm_g, w1in, cw, w1out)
    h2, qpre, kpre, qs, kn, v, z, invq, invk, cc, fl = _attn_proj_fwd(x1, g2, w2in, wf, bf, qg_t, kg_t, e, e.T, tril)
    ones3 = [jnp.ones((H, s, 3), BF16)]
    qh, kh, vh = _heads(qs, s), _heads(kn, s), _heads(v, s)
    ka = _aug([kh] + _terms(-cc[:, :H].T) + ones3, s)
    va = _aug([vh] + ones3, s)
    kat = _tiles_t(ka, tq)
    ol = _flash_fwd(_tiles(_aug([qh] + ones3, s), tq), kat, _tiles(va, tq)).reshape(H, s, HA)
    o = _unheads(ol[:, :, :DH], s)
    og, dx2, do, dz, delta, lossp = _attn_out(o, z, x1, tgt, w2out, w2out.T, e)

    qab = _aug([qh] + ones3 + _terms(-ol[:, :, DH]), s)
    doa = _aug([_heads(do, s)] + _terms(-delta[:, :H].T), s)
    dq, dkt, dvt = _flash_bwd(_tiles(qab, tq), _tiles_t(qab, tq), _tiles(doa, tq), _tiles_t(doa, tq), kat,
                              _tiles(ka, tq), _tiles_t(va, tq))
    dq = dq.reshape(H, s, HA)
    dqs = _unheads(dq[:, :, :DH], s)
    dkn = dkt[:, :DH, :].transpose(2, 0, 1).reshape(s, D)
    dvv = dvt[:, :DH, :].transpose(2, 0, 1).reshape(s, D)
    dcs = _pad_lanes((dq[:, :, DH + 3] - dkt[:, DH, :]).T)
    dx1, dp2, df, dg2, dqg, dkg, dbf = _attn_proj_bwd(
        dqs, dkn, dvv, dz, dcs, qpre, kpre, invq, invk, fl, x1, dx2, g2, qg_t, kg_t, w2in.T, wf.T, e, e.T, triu, fold)
    gx, dp1, dcw, dg1 = _conv_bwd(dx1, p1, xv, conv_norm_g, cw, w1out.T, w1in.T)
    dw2out = _wgrad(og.T, dx2, "wgrad_attn_out")
    dw2in = _wgrad(h2.T, dp2, "wgrad_attn_in")
    dwf = _wgrad(h2.T, df, "wgrad_attn_forget")
    dw1out = _wgrad(yg.T, dx1, "wgrad_conv_out")
    dw1in = _wgrad(h1.T, dp1, "wgrad_conv_in")

    dw2all = jnp.concatenate([dw2in, dwf[:, :H]], axis=1)
    gfull = jnp.concatenate([
        dw1in.reshape(D, NDEV, 512).transpose(1, 0, 2),
        dw2all.reshape(D, NDEV, 514).transpose(1, 0, 2).reshape(NDEV, ROWS_W2IN, PACK_W),
        dw1out.reshape(NDEV, ROWS_WOUT, PACK_W), dw2out.reshape(NDEV, ROWS_WOUT, PACK_W),
        jnp.zeros((NDEV, PACK_ROWS - c - ROWS_WOUT, PACK_W), F32)], axis=1)
    parts = _exchange_blocks(gfull, "scatter_grads")
    pack = lambda t1, t2, t3, t4: _pack_big(t1[0], t2[0], t3[0], t4[0], F32)
    gb, db, mb, vb = _adamw_big(parts, pack(conv_w_in, attn_w_in, conv_w_out, attn_w_out),
                                pack(m_conv_w_in, m_attn_w_in, m_conv_w_out, m_attn_w_out),
                                pack(v_conv_w_in, v_attn_w_in, v_conv_w_out, v_attn_w_out))

    shard_rows = jnp.concatenate([dcw[0:3].reshape(3, NDEV, 128).transpose(1, 0, 2), dg2.reshape(NDEV, 1, 128),
                                  jnp.zeros((NDEV, 4, 128), F32)], axis=1).reshape(64, 128)
    small_g = jnp.concatenate([shard_rows, dg1.reshape(8, 128), dbf, dqg[0:1], dkg[0:1], jnp.zeros((5, 128), F32),
                               lossp.reshape(8, 128)], axis=0)
    gs, loss8 = _sum_parts(_all_gather(small_g, "gather_small_grads"), 80)
    loss = loss8[0, 0]
    mine = lax.dynamic_slice(gs, (8 * me, 0), (8, 128))
    g_small = jnp.concatenate([mine, gs[64:75], jnp.zeros((5, 128), F32)], axis=0)

    def pack_small(cwk, ang, cng, bfk, qgk, kgk):
        return jnp.concatenate([cwk[0], ang, jnp.zeros((4, 128), F32), cng.reshape(8, 128), _pad_lanes(bfk),
                                _pad_lanes(qgk), _pad_lanes(kgk), jnp.zeros((5, 128), F32)], axis=0)

    ds_, ms_, vs_ = _adamw_small(
        g_small, pack_small(conv_w, attn_norm_g, conv_norm_g, attn_b_f, attn_q_norm_g, attn_k_norm_g),
        pack_small(m_conv_w, m_attn_norm_g, m_conv_norm_g, m_attn_b_f, m_attn_q_norm_g, m_attn_k_norm_g),
        pack_small(v_conv_w, v_attn_norm_g, v_conv_norm_g, v_attn_b_f, v_attn_q_norm_g, v_attn_k_norm_g))

    def leaves(big, small):
        w1i, w2i, w1o, w2o = _unpack_big(big)
        return (small[8:16].reshape(1, D), w1i, small[0:3].reshape(1, 3, 128), w1o, small[3:4], w2i,
                small[16:17, :H], small[17:18, :DH], small[18:19, :DH], w2o)

    return (loss, gx[None], *leaves(gb, g_small), *leaves(db, ds_), *leaves(mb, ms_), *leaves(vb, vs_))
```

```python
import functools
import math

import jax
import jax.numpy as jnp
from jax import lax
from jax.experimental import pallas as pl
from jax.experimental.pallas import tpu as pltpu

F32 = jnp.float32
BF16 = jnp.bfloat16

D = 1024
H = 16
DH = 64
NDEV = 8
RMS_EPS = 1e-6
LANES = 128
HA = 128
TM_FWD = 512
TM_BWD = 256
TQ = 512
TK = 512
CH = 256
PACK_W = 512
PACK_ROWS = 2576
ROWS_W1IN, ROWS_W2IN, ROWS_WOUT = 1024, 1028, 256
ADAM_LR, ADAM_B1, ADAM_B2, ADAM_EPS, ADAM_WD, ADAM_STEP = 0.001, 0.9, 0.999, 1e-08, 0.01, 10
VMEM_LIMIT = 56 * 1024 * 1024
MASKED = -1e30
BOUNDED_SOFTMAX_REACH = 60.0
MESH = pl.DeviceIdType.MESH


def _params(*sem, vmem=VMEM_LIMIT):
    return pltpu.CompilerParams(dimension_semantics=sem or None, vmem_limit_bytes=vmem)


def _const(shape):
    nd = len(shape)
    return pl.BlockSpec(shape, lambda *_: (0,) * nd, pipeline_mode=pl.Buffered(1))


def _rows(tm, n, rev=False, steps=None):
    if rev:
        return pl.BlockSpec((tm, n), lambda i: (steps - 1 - i, 0))
    return pl.BlockSpec((tm, n), lambda i: (i, 0))


def _dot(a, b):
    return jnp.dot(a, b, preferred_element_type=F32)


def _top16(x):
    bits = lax.bitcast_convert_type(x, jnp.uint32) & jnp.uint32(0xFFFF0000)
    return lax.bitcast_convert_type(bits, F32)


def _split2(x):
    hi = _top16(x)
    return hi.astype(BF16), (x - hi).astype(BF16)


def _split3(x):
    hi = _top16(x)
    r = x - hi
    mid = _top16(r)
    return hi.astype(BF16), mid.astype(BF16), (r - mid).astype(BF16)


def _seg_sum(a, e):
    hi, lo = _split2(a)
    return _dot(hi, e) + _dot(lo, e)


def _seg_bcast(s, et):
    hi, mid, lo = _split3(s)
    return _dot(hi, et) + _dot(mid, et) + _dot(lo, et)


def _tri_sum(t, v):
    hi, mid, lo = _split3(v)
    return _dot(t, hi) + _dot(t, mid) + _dot(t, lo)


def _sigmoid(z):
    return 1.0 / (1.0 + jnp.exp(-z))


def _place():
    return lax.axis_index("x"), lax.axis_index("y"), lax.axis_index("c")


def _all_gather(xb, name):
    r, c_ = xb.shape

    def body(x_ref, out_ref, send_sems, recv_sems, local_sem):
        x, y, c = _place()
        me, sibling = (x, y, c), (x, y, 1 - c)
        chips = [(1 - x, y), (x, 1 - y), (1 - x, 1 - y)]

        def slab(px, py, pc):
            return out_ref.at[4 * px + 2 * py + pc]

        def copy(k, block, to, src=None):
            return pltpu.make_async_remote_copy(
                src_ref=slab(*block) if src is None else src, dst_ref=slab(*block),
                send_sem=send_sems.at[k], recv_sem=recv_sems.at[k], device_id=to, device_id_type=MESH)

        mine = pltpu.make_async_copy(x_ref, slab(*me), local_sem)
        mine.start()
        first = [copy(0, me, sibling, src=x_ref)]
        first += [copy(1 + j, me, (*chip, c), src=x_ref) for j, chip in enumerate(chips)]
        for cp in first:
            cp.start()
        passed = [copy(4 + j, (*chip, c), sibling) for j, chip in enumerate(chips)]
        for j, chip in enumerate(chips):
            copy(1 + j, (*chip, c), me).wait_recv()
            passed[j].start()
        copy(0, sibling, me).wait_recv()
        for j, chip in enumerate(chips):
            copy(4 + j, (*chip, 1 - c), me).wait_recv()
        for cp in first + passed:
            cp.wait_send()
        mine.wait()

    return pl.pallas_call(
        body, name=name,
        out_shape=jax.ShapeDtypeStruct((NDEV, r, c_), xb.dtype),
        in_specs=[pl.BlockSpec(memory_space=pl.ANY)],
        out_specs=pl.BlockSpec(memory_space=pl.ANY),
        scratch_shapes=[pltpu.SemaphoreType.DMA((7,)), pltpu.SemaphoreType.DMA((7,)), pltpu.SemaphoreType.DMA],
    )(xb)


def _exchange_blocks(g, name):
    _, r, c_ = g.shape

    def body(g_ref, out_ref, send_sems, recv_sems, local_sem):
        x, y, c = _place()
        me = 4 * x + 2 * y + c
        mine = pltpu.make_async_copy(g_ref.at[me], out_ref.at[0], local_sem)
        mine.start()
        copies = []
        for k in range(1, NDEV):
            px = 1 - x if k & 4 else x
            py = 1 - y if k & 2 else y
            pc = 1 - c if k & 1 else c
            cp = pltpu.make_async_remote_copy(
                src_ref=g_ref.at[4 * px + 2 * py + pc], dst_ref=out_ref.at[k],
                send_sem=send_sems.at[k - 1], recv_sem=recv_sems.at[k - 1],
                device_id=(px, py, pc), device_id_type=MESH)
            cp.start()
            copies.append(cp)
        for cp in copies:
            cp.wait()
        mine.wait()

    return pl.pallas_call(
        body, name=name,
        out_shape=jax.ShapeDtypeStruct((NDEV, r, c_), g.dtype),
        in_specs=[pl.BlockSpec(memory_space=pl.ANY)],
        out_specs=pl.BlockSpec(memory_space=pl.ANY),
        scratch_shapes=[pltpu.SemaphoreType.DMA((7,)), pltpu.SemaphoreType.DMA((7,)), pltpu.SemaphoreType.DMA],
    )(g)


def _conv_fwd(x, g1, w1in, cw, w1out):
    s = x.shape[0]
    tm = min(TM_FWD, s)
    steps = s // tm

    def body(x_ref, g_ref, win_ref, cw_ref, wout_ref, x1_ref, h1_ref, p1_ref, yg_ref, tail_ref):
        @pl.when(pl.program_id(0) == 0)
        def _():
            tail_ref[...] = jnp.zeros_like(tail_ref)

        xv = x_ref[...]
        inv = lax.rsqrt(jnp.mean(xv * xv, axis=-1, keepdims=True) + RMS_EPS)
        h = (xv * inv * g_ref[...]).astype(BF16)
        h1_ref[...] = h
        row = lax.broadcasted_iota(jnp.int32, (tm, CH), 0)
        for ci in range(D // CH):
            lo, hi = ci * CH, (ci + 1) * CH
            parts = []
            for k in range(4):
                pk = _dot(h, win_ref[:, k * D + lo:k * D + hi]).astype(BF16)
                p1_ref[:, k * D + lo:k * D + hi] = pk
                parts.append(pk.astype(F32))
            b, c, xin, z = parts
            u = c * xin
            t6 = tail_ref[6:7, lo:hi]
            t7 = tail_ref[7:8, lo:hi]
            u1 = jnp.where(row == 0, t7, pltpu.roll(u, 1, 0))
            u2 = jnp.where(row == 0, t6, jnp.where(row == 1, t7, pltpu.roll(u, 2, 0)))
            tail_ref[:, lo:hi] = u[tm - 8:, :]
            y = cw_ref[2:3, lo:hi] * u + cw_ref[1:2, lo:hi] * u1 + cw_ref[0:1, lo:hi] * u2
            yg_ref[:, lo:hi] = (b * y * (z * _sigmoid(z))).astype(BF16)
        x1_ref[...] = xv + _dot(yg_ref[...], wout_ref[...])

    return pl.pallas_call(
        body, name="conv_fwd", grid=(steps,),
        in_specs=[_rows(tm, D), _const((1, D)), _const((D, 4 * D)), _const((8, D)), _const((D, D))],
        out_specs=[_rows(tm, D), _rows(tm, D), _rows(tm, 4 * D), _rows(tm, D)],
        out_shape=[jax.ShapeDtypeStruct((s, D), F32), jax.ShapeDtypeStruct((s, D), BF16),
                   jax.ShapeDtypeStruct((s, 4 * D), BF16), jax.ShapeDtypeStruct((s, D), BF16)],
        scratch_shapes=[pltpu.VMEM((8, D), F32)],
        compiler_params=_params("arbitrary"),
    )(x, g1, w1in, cw, w1out)


def _attn_proj_fwd(x1, g2, w2in, wf, bf, qg, kg, e, et, tril):
    s = x1.shape[0]
    tm = min(TM_FWD, s)
    steps = s // tm

    def body(x_ref, g_ref, win_ref, wf_ref, bf_ref, qg_ref, kg_ref, e_ref, et_ref, tril_ref,
             h2_ref, qp_ref, kp_ref, qs_ref, kn_ref, v_ref, z_ref, iq_ref, ik_ref, c_ref, fl_ref, nq_ref, nk_ref,
             kmax_ref, carry_ref):
        @pl.when(pl.program_id(0) == 0)
        def _():
            carry_ref[...] = jnp.zeros_like(carry_ref)
            kmax_ref[...] = jnp.zeros_like(kmax_ref)

        xv = x_ref[...]
        inv = lax.rsqrt(jnp.mean(xv * xv, axis=-1, keepdims=True) + RMS_EPS)
        h = (xv * inv * g_ref[...]).astype(BF16)
        h2_ref[...] = h

        def normed(col, pre_ref, inv_ref, gain_ref, len_ref, scale):
            pre = _dot(h, win_ref[:, col * D:(col + 1) * D]).astype(BF16)
            pre_ref[...] = pre
            t = pre.astype(F32)
            invh = lax.rsqrt(_seg_sum(t * t, e_ref[...]) * (1.0 / DH) + RMS_EPS)
            inv_ref[...] = invh
            n = t * _seg_bcast(invh, et_ref[...]) * (gain_ref[...] * scale)
            len_ref[...] = jnp.sqrt(_seg_sum(n * n, e_ref[...]))
            return n.astype(BF16)

        qs_ref[...] = normed(0, qp_ref, iq_ref, qg_ref, nq_ref, 1.0 / math.sqrt(DH))
        kn_ref[...] = normed(1, kp_ref, ik_ref, kg_ref, nk_ref, 1.0)
        kmax_ref[...] = jnp.maximum(kmax_ref[...], jnp.max(nk_ref[...], axis=0, keepdims=True))
        v_ref[...] = _dot(h, win_ref[:, 2 * D:3 * D]).astype(BF16)
        z_ref[...] = _dot(h, win_ref[:, 3 * D:4 * D]).astype(BF16)

        fl = _dot(h, wf_ref[...]) + bf_ref[...]
        fl_ref[...] = fl
        ex = jnp.exp(-jnp.abs(fl))
        up = 1.0 + ex
        log1p = jnp.where(up == 1.0, ex, jnp.log(up) * ex / (up - 1.0))
        lane = lax.broadcasted_iota(jnp.int32, (tm, LANES), 1)
        logf = jnp.where(lane < H, jnp.minimum(fl, 0.0) - log1p, 0.0)
        c_ref[...] = _tri_sum(tril_ref[...], logf) + carry_ref[0:1, :]
        carry_ref[0:1, :] = c_ref[tm - 1:tm, :]

    row_bf = lambda: _rows(tm, D)
    row_sm = lambda: _rows(tm, LANES)
    return pl.pallas_call(
        body, name="attn_proj_fwd", grid=(steps,),
        in_specs=[_rows(tm, D), _const((1, D)), _const((D, 4 * D)), _const((D, LANES)), _const((1, LANES)),
                  _const((1, D)), _const((1, D)), _const((D, LANES)), _const((LANES, D)), _const((tm, tm))],
        out_specs=[row_bf() for _ in range(7)] + [row_sm() for _ in range(6)] + [
            pl.BlockSpec((1, LANES), lambda i: (0, 0))],
        out_shape=[jax.ShapeDtypeStruct((s, D), BF16)] * 7 + [jax.ShapeDtypeStruct((s, LANES), F32)] * 6 + [
            jax.ShapeDtypeStruct((1, LANES), F32)],
        scratch_shapes=[pltpu.VMEM((8, LANES), F32)],
        compiler_params=_params("arbitrary"),
    )(x1, g2, w2in, wf, bf, qg, kg, e, et, tril)


ROW_BIAS = DH + 3


def _flash_fwd(qa, kat, va, bounded):
    h_, nq, tq, _ = qa.shape
    nk, tk = kat.shape[1], kat.shape[3]

    def body(q_ref, kt_ref, v_ref, o_ref, qb_ref, m_ref, acc_ref):
        i = pl.program_id(1)
        q = q_ref[...]
        lane = lax.broadcasted_iota(jnp.int32, (tq, HA), 1)
        in_bias = (lane >= ROW_BIAS) & (lane < ROW_BIAS + 3)
        acc_ref[...] = jnp.zeros_like(acc_ref)
        if bounded:
            m_ref[...] = -jnp.sum(jnp.where(in_bias, q.astype(F32), 0.0), axis=1, keepdims=True)
        else:
            q = jnp.where(in_bias, jnp.zeros_like(q), q)
            m_ref[...] = jnp.full_like(m_ref, MASKED)

        def step(j, masked):
            z = _dot(q, kt_ref[j])
            if masked:
                rows = lax.broadcasted_iota(jnp.int32, (tq, tk), 0)
                cols = lax.broadcasted_iota(jnp.int32, (tq, tk), 1)
                z = jnp.where(rows >= cols, z, MASKED)
            if bounded:
                acc_ref[...] += _dot(jnp.exp(z).astype(BF16), v_ref[j])
            else:
                m_old = m_ref[...]
                m_new = jnp.maximum(m_old, jnp.max(z, axis=1, keepdims=True))
                p = jnp.exp(z - m_new)
                acc_ref[...] = jnp.exp(m_old - m_new) * acc_ref[...] + _dot(p.astype(BF16), v_ref[j])
                m_ref[...] = m_new

        def loop_body(j, carry):
            step(j, False)
            return carry

        lax.fori_loop(0, i, loop_body, 0)
        step(i, True)
        acc = acc_ref[...]
        l = jnp.sum(jnp.where(lane == DH, acc, 0.0), axis=1, keepdims=True)
        lse = m_ref[...] + jnp.log(l)
        o_ref[...] = jnp.where(lane < DH, acc / l, lse)
        hi, mid, lo = _split3(-lse)
        qb_ref[...] = jnp.where(lane == ROW_BIAS, hi, jnp.where(lane == ROW_BIAS + 1, mid,
                                                                jnp.where(lane == ROW_BIAS + 2, lo, q_ref[...])))

    tile = lambda: pl.BlockSpec((None, None, tq, HA), lambda h, i: (h, i, 0, 0))
    return pl.pallas_call(
        body, name="flash_fwd_bounded" if bounded else "flash_fwd_online", grid=(h_, nq),
        in_specs=[tile(), pl.BlockSpec((None, nk, HA, tk), lambda h, i: (h, 0, 0, 0)),
                  pl.BlockSpec((None, nk, tk, HA), lambda h, i: (h, 0, 0, 0))],
        out_specs=[tile(), tile()],
        out_shape=[jax.ShapeDtypeStruct((h_, nq, tq, HA), F32), jax.ShapeDtypeStruct((h_, nq, tq, HA), BF16)],
        scratch_shapes=[pltpu.VMEM((tq, 1), F32), pltpu.VMEM((tq, HA), F32)],
        compiler_params=_params("arbitrary", "arbitrary"),
    )(qa, kat, va)


def _attn_out(o, z, x1, tgt, w2out, w2out_t, e):
    s = o.shape[0]
    tm = min(TM_BWD, s)
    steps = s // tm

    def body(o_ref, z_ref, x1_ref, t_ref, w_ref, wt_ref, e_ref, og_ref, dx2_ref, do_ref, dz_ref, dl_ref, loss_ref):
        @pl.when(pl.program_id(0) == 0)
        def _():
            loss_ref[...] = jnp.zeros_like(loss_ref)

        ov = o_ref[...]
        zv = z_ref[...].astype(F32)
        sg = _sigmoid(zv)
        sil = zv * sg
        og = (ov * sil).astype(BF16)
        og_ref[...] = og
        err = x1_ref[...] + _dot(og, w_ref[...]) - t_ref[...]
        loss_ref[...] += (0.5 / D) * jnp.sum(err * err, axis=0, keepdims=True)
        dx2 = err * (1.0 / D)
        dx2_ref[...] = dx2
        dog = _dot(dx2.astype(BF16), wt_ref[...])
        do = (dog * sil).astype(BF16)
        do_ref[...] = do
        dz_ref[...] = (dog * ov * (sg * (1.0 + zv * (1.0 - sg)))).astype(BF16)
        dl_ref[...] = _seg_sum(do.astype(F32) * ov, e_ref[...])

    return pl.pallas_call(
        body, name="attn_out", grid=(steps,),
        in_specs=[_rows(tm, D), _rows(tm, D), _rows(tm, D), _rows(tm, D), _const((D, D)), _const((D, D)),
                  _const((D, LANES))],
        out_specs=[_rows(tm, D), _rows(tm, D), _rows(tm, D), _rows(tm, D), _rows(tm, LANES),
                   pl.BlockSpec((1, D), lambda i: (0, 0))],
        out_shape=[jax.ShapeDtypeStruct((s, D), BF16), jax.ShapeDtypeStruct((s, D), F32),
                   jax.ShapeDtypeStruct((s, D), BF16), jax.ShapeDtypeStruct((s, D), BF16),
                   jax.ShapeDtypeStruct((s, LANES), F32), jax.ShapeDtypeStruct((1, D), F32)],
        compiler_params=_params("arbitrary"),
    )(o, z, x1, tgt, w2out, w2out_t, e)


def _flash_bwd(qab, qabt, doa, doat, kat, ka, vat):
    h_, nq, tq, _ = qab.shape
    nk, tk = kat.shape[1], kat.shape[3]
    s = nk * tk

    def body(qa_ref, qat_ref, da_ref, dat_ref, kat_ref, ka_ref, vat_ref, dq_hbm, dkt_ref, dvt_ref, dq_acc, sem):
        hh = pl.program_id(0)
        j = pl.program_id(1)

        @pl.when(j == 0)
        def _():
            dq_acc[...] = jnp.zeros_like(dq_acc)

        dkt_ref[...] = jnp.zeros_like(dkt_ref)
        dvt_ref[...] = jnp.zeros_like(dvt_ref)
        kat_v = kat_ref[...]
        ka_v = ka_ref[...]
        vat_v = vat_ref[...]

        def step(i, masked):
            zz = _dot(qa_ref[i], kat_v)
            if masked:
                rows = lax.broadcasted_iota(jnp.int32, (tq, tk), 0)
                cols = lax.broadcasted_iota(jnp.int32, (tq, tk), 1)
                zz = jnp.where(rows >= cols, zz, MASKED)
            p = jnp.exp(zz)
            ds = (p * _dot(da_ref[i], vat_v)).astype(BF16)
            pb = p.astype(BF16)
            dq_acc[i] += _dot(ds, ka_v)
            dvt_ref[...] += _dot(dat_ref[i], pb)
            dkt_ref[...] += _dot(qat_ref[i], ds)

        step(j, True)

        def loop_body(i, carry):
            step(i, False)
            return carry

        lax.fori_loop(j + 1, nq, loop_body, 0)

        @pl.when(j == nk - 1)
        def _():
            cp = pltpu.make_async_copy(dq_acc, dq_hbm.at[hh], sem)
            cp.start()
            cp.wait()

    whole = lambda a, b: pl.BlockSpec((None, nq, a, b), lambda h, j: (h, 0, 0, 0), pipeline_mode=pl.Buffered(1))
    return pl.pallas_call(
        body, name="flash_bwd", grid=(h_, nk),
        in_specs=[whole(tq, HA), whole(HA, tq), whole(tq, HA), whole(HA, tq),
                  pl.BlockSpec((None, None, HA, tk), lambda h, j: (h, j, 0, 0)),
                  pl.BlockSpec((None, None, tk, HA), lambda h, j: (h, j, 0, 0)),
                  pl.BlockSpec((None, None, HA, tk), lambda h, j: (h, j, 0, 0))],
        out_specs=[pl.BlockSpec(memory_space=pl.ANY),
                   pl.BlockSpec((None, HA, tk), lambda h, j: (h, 0, j)),
                   pl.BlockSpec((None, HA, tk), lambda h, j: (h, 0, j))],
        out_shape=[jax.ShapeDtypeStruct((h_, nq, tq, HA), F32), jax.ShapeDtypeStruct((h_, HA, s), F32),
                   jax.ShapeDtypeStruct((h_, HA, s), F32)],
        scratch_shapes=[pltpu.VMEM((nq, tq, HA), F32), pltpu.SemaphoreType.DMA],
        compiler_params=_params("arbitrary", "arbitrary"),
    )(qab, qabt, doa, doat, kat, ka, vat)


def _attn_proj_bwd(dqs, dkn, dv, dz, dcs, qpre, kpre, invq, invk, fl, x1, dx2, g2, qg, kg, w2in_t, wf_t, e, et,
                   triu, fold):
    s = x1.shape[0]
    tm = min(TM_BWD, s)
    steps = s // tm

    def body(dqs_ref, dkn_ref, dv_ref, dz_ref, dcs_ref, qp_ref, kp_ref, iq_ref, ik_ref, fl_ref, x1_ref, dx2_ref,
             g2_ref, qg_ref, kg_ref, wt_ref, wft_ref, e_ref, et_ref, triu_ref, fold_ref,
             dx1_ref, dp_ref, df_ref, dg2_ref, dqg_ref, dkg_ref, dbf_ref, carry_ref, rc_ref, qcol_ref, kcol_ref):
        step = pl.program_id(0)

        @pl.when(step == 0)
        def _():
            carry_ref[...] = jnp.zeros_like(carry_ref)
            dg2_ref[...] = jnp.zeros_like(dg2_ref)
            dbf_ref[...] = jnp.zeros_like(dbf_ref)
            qcol_ref[...] = jnp.zeros_like(qcol_ref)
            kcol_ref[...] = jnp.zeros_like(kcol_ref)
            dqg_ref[...] = jnp.zeros_like(dqg_ref)
            dkg_ref[...] = jnp.zeros_like(dkg_ref)

        rc_ref[...] = _tri_sum(triu_ref[...], dcs_ref[...]) + carry_ref[0:1, :]
        carry_ref[0:1, :] = rc_ref[0:1, :]
        df = rc_ref[...] * _sigmoid(-fl_ref[...])
        dfb = df.astype(BF16)
        df_ref[...] = dfb
        dbf_ref[...] += jnp.sum(df, axis=0, keepdims=True)

        def norm_bwd(dn, pre_ref, inv_ref, gain_ref, col_ref):
            pre = pre_ref[...].astype(F32)
            invh = inv_ref[...]
            invb = _seg_bcast(invh, et_ref[...])
            col_ref[...] += jnp.sum(dn * pre * invb, axis=0, keepdims=True)
            gd = dn * gain_ref[...]
            mean = _seg_sum(gd * pre, e_ref[...]) * (1.0 / DH)
            return invb * gd - pre * _seg_bcast(mean * invh * invh * invh, et_ref[...])

        dq = norm_bwd(dqs_ref[...] * (1.0 / math.sqrt(DH)), qp_ref, iq_ref, qg_ref, qcol_ref).astype(BF16)
        dp_ref[:, 0:D] = dq
        dh = _dot(dq, wt_ref[0:D, :])
        dk = norm_bwd(dkn_ref[...], kp_ref, ik_ref, kg_ref, kcol_ref).astype(BF16)
        dp_ref[:, D:2 * D] = dk
        dh += _dot(dk, wt_ref[D:2 * D, :])
        dvb = dv_ref[...].astype(BF16)
        dp_ref[:, 2 * D:3 * D] = dvb
        dh += _dot(dvb, wt_ref[2 * D:3 * D, :])
        dzb = dz_ref[...]
        dp_ref[:, 3 * D:4 * D] = dzb
        dh += _dot(dzb, wt_ref[3 * D:4 * D, :])
        dh += _dot(dfb, wft_ref[...])

        xv = x1_ref[...]
        inv = lax.rsqrt(jnp.mean(xv * xv, axis=-1, keepdims=True) + RMS_EPS)
        dg2_ref[...] += jnp.sum(dh * xv * inv, axis=0, keepdims=True)
        gh = dh * g2_ref[...]
        dx1_ref[...] = dx2_ref[...] + inv * gh - xv * (inv * inv * inv * jnp.mean(gh * xv, axis=-1, keepdims=True))

        @pl.when(step == steps - 1)
        def _():
            dqg_ref[...] = _fold_heads(qcol_ref[...], fold_ref[...])
            dkg_ref[...] = _fold_heads(kcol_ref[...], fold_ref[...])

    rr = lambda n: _rows(tm, n, rev=True, steps=steps)
    acc = lambda n, r=1: pl.BlockSpec((r, n), lambda i: (0, 0))
    return pl.pallas_call(
        body, name="attn_proj_bwd", grid=(steps,),
        in_specs=[rr(D), rr(D), rr(D), rr(D), rr(LANES), rr(D), rr(D), rr(LANES), rr(LANES), rr(LANES), rr(D), rr(D),
                  _const((1, D)), _const((1, D)), _const((1, D)), _const((4 * D, D)), _const((LANES, D)),
                  _const((D, LANES)), _const((LANES, D)), _const((tm, tm)), _const((D, LANES))],
        out_specs=[rr(D), rr(4 * D), rr(LANES), acc(D), acc(LANES, 8), acc(LANES, 8), acc(LANES)],
        out_shape=[jax.ShapeDtypeStruct((s, D), F32), jax.ShapeDtypeStruct((s, 4 * D), BF16),
                   jax.ShapeDtypeStruct((s, LANES), BF16), jax.ShapeDtypeStruct((1, D), F32),
                   jax.ShapeDtypeStruct((8, LANES), F32), jax.ShapeDtypeStruct((8, LANES), F32),
                   jax.ShapeDtypeStruct((1, LANES), F32)],
        scratch_shapes=[pltpu.VMEM((8, LANES), F32), pltpu.VMEM((tm, LANES), F32), pltpu.VMEM((1, D), F32),
                        pltpu.VMEM((1, D), F32)],
        compiler_params=_params("arbitrary"),
    )(dqs, dkn, dv, dz, dcs, qpre, kpre, invq, invk, fl, x1, dx2, g2, qg, kg, w2in_t, wf_t, e, et, triu, fold)


def _fold_heads(col, fold):
    hi, mid, lo = _split3(jnp.broadcast_to(col, (8, D)))
    return _dot(hi, fold) + _dot(mid, fold) + _dot(lo, fold)


def _conv_bwd(dx1, p1, x, g1, cw, w1out_t, w1in_t):
    s = x.shape[0]
    tm = min(TM_BWD, s)
    steps = s // tm
    halo = tm // 8

    def body(dx1_ref, b_ref, c_ref, xi_ref, z_ref, ch_ref, xh_ref, x_ref, g_ref, cw_ref, wot_ref, wit_ref,
             gx_ref, dp_ref, dcw_ref, dg1_ref, head_ref):
        step = pl.program_id(0)

        @pl.when(step == 0)
        def _():
            head_ref[...] = jnp.zeros_like(head_ref)
            dcw_ref[...] = jnp.zeros_like(dcw_ref)
            dg1_ref[...] = jnp.zeros_like(dg1_ref)

        first_tile = step == steps - 1
        dx1 = dx1_ref[...]
        row = lax.broadcasted_iota(jnp.int32, (tm, CH), 0)
        dyb = dx1.astype(BF16)
        for ci in range(D // CH):
            lo, hi = ci * CH, (ci + 1) * CH
            dyg = _dot(dyb, wot_ref[:, lo:hi])
            b = b_ref[:, lo:hi].astype(F32)
            c = c_ref[:, lo:hi].astype(F32)
            xin = xi_ref[:, lo:hi].astype(F32)
            z = z_ref[:, lo:hi].astype(F32)
            u = c * xin
            t6 = jnp.where(first_tile, 0.0, ch_ref[6:7, lo:hi].astype(F32) * xh_ref[6:7, lo:hi].astype(F32))
            t7 = jnp.where(first_tile, 0.0, ch_ref[7:8, lo:hi].astype(F32) * xh_ref[7:8, lo:hi].astype(F32))
            u1 = jnp.where(row == 0, t7, pltpu.roll(u, 1, 0))
            u2 = jnp.where(row == 0, t6, jnp.where(row == 1, t7, pltpu.roll(u, 2, 0)))
            w0, w1, w2 = cw_ref[0:1, lo:hi], cw_ref[1:2, lo:hi], cw_ref[2:3, lo:hi]
            y = w2 * u + w1 * u1 + w0 * u2
            sg = _sigmoid(z)
            sil = z * sg
            dp_ref[:, lo:hi] = (dyg * y * sil).astype(BF16)
            dy = dyg * b * sil
            dp_ref[:, 3 * D + lo:3 * D + hi] = (dyg * b * y * (sg * (1.0 + z * (1.0 - sg)))).astype(BF16)
            dcw_ref[2:3, lo:hi] += jnp.sum(dy * u, axis=0, keepdims=True)
            dcw_ref[1:2, lo:hi] += jnp.sum(dy * u1, axis=0, keepdims=True)
            dcw_ref[0:1, lo:hi] += jnp.sum(dy * u2, axis=0, keepdims=True)
            n0 = head_ref[0:1, lo:hi]
            n1 = head_ref[1:2, lo:hi]
            dyn1 = jnp.where(row == tm - 1, n0, pltpu.roll(dy, tm - 1, 0))
            dyn2 = jnp.where(row == tm - 2, n0, jnp.where(row == tm - 1, n1, pltpu.roll(dy, tm - 2, 0)))
            head_ref[:, lo:hi] = dy[0:8, :]
            du = w2 * dy + w1 * dyn1 + w0 * dyn2
            dp_ref[:, D + lo:D + hi] = (du * xin).astype(BF16)
            dp_ref[:, 2 * D + lo:2 * D + hi] = (du * c).astype(BF16)
        dh = _dot(dp_ref[:, 0:D], wit_ref[0:D, :])
        for k in range(1, 4):
            dh += _dot(dp_ref[:, k * D:(k + 1) * D], wit_ref[k * D:(k + 1) * D, :])
        xv = x_ref[...]
        inv = lax.rsqrt(jnp.mean(xv * xv, axis=-1, keepdims=True) + RMS_EPS)
        dg1_ref[...] += jnp.sum(dh * xv * inv, axis=0, keepdims=True)
        gh = dh * g_ref[...]
        gx_ref[...] = dx1 + inv * gh - xv * (inv * inv * inv * jnp.mean(gh * xv, axis=-1, keepdims=True))

    rr = lambda n: _rows(tm, n, rev=True, steps=steps)
    part = lambda k: pl.BlockSpec((tm, D), lambda i: (steps - 1 - i, k))
    prev8 = lambda k: pl.BlockSpec((8, D), lambda i: (jnp.maximum((steps - 1 - i) * halo - 1, 0), k))
    return pl.pallas_call(
        body, name="conv_bwd", grid=(steps,),
        in_specs=[rr(D), part(0), part(1), part(2), part(3), prev8(1), prev8(2), rr(D), _const((1, D)),
                  _const((8, D)), _const((D, D)), _const((4 * D, D))],
        out_specs=[rr(D), rr(4 * D), pl.BlockSpec((8, D), lambda i: (0, 0)), pl.BlockSpec((1, D), lambda i: (0, 0))],
        out_shape=[jax.ShapeDtypeStruct((s, D), F32), jax.ShapeDtypeStruct((s, 4 * D), BF16),
                   jax.ShapeDtypeStruct((8, D), F32), jax.ShapeDtypeStruct((1, D), F32)],
        scratch_shapes=[pltpu.VMEM((8, D), F32)],
        compiler_params=_params("arbitrary"),
    )(dx1, p1, p1, p1, p1, p1, p1, x, g1, cw, w1out_t, w1in_t)


def _wgrad(at, g, name):
    k, s = at.shape
    n = g.shape[1]
    bn = min(n, 1024)
    ts = min(512, s)

    def body(at_ref, g_ref, out_ref, acc_ref):
        t = pl.program_id(1)

        @pl.when(t == 0)
        def _():
            acc_ref[...] = jnp.zeros_like(acc_ref)

        acc_ref[...] += _dot(at_ref[...], g_ref[...].astype(BF16))

        @pl.when(t == s // ts - 1)
        def _():
            out_ref[...] = acc_ref[...]

    return pl.pallas_call(
        body, name=name, grid=(n // bn, s // ts),
        in_specs=[pl.BlockSpec((k, ts), lambda j, t: (0, t)), pl.BlockSpec((ts, bn), lambda j, t: (t, j))],
        out_specs=pl.BlockSpec((k, bn), lambda j, t: (0, j)),
        out_shape=jax.ShapeDtypeStruct((k, n), F32),
        scratch_shapes=[pltpu.VMEM((k, bn), F32)],
        compiler_params=_params("arbitrary", "arbitrary"),
    )(at, g)


def _adamw_math(w, g, m, v):
    m = ADAM_B1 * m + (1.0 - ADAM_B1) * g
    v = ADAM_B2 * v + (1.0 - ADAM_B2) * (g * g)
    m_hat = m / (1.0 - ADAM_B1 ** ADAM_STEP)
    v_hat = v / (1.0 - ADAM_B2 ** ADAM_STEP)
    delta = -ADAM_LR * (m_hat / (jnp.sqrt(v_hat) + ADAM_EPS) + ADAM_WD * w)
    return delta, m, v


def _adamw_big(parts, w, m, v):
    _, r, c_ = parts.shape
    rb = 368
    assert r % rb == 0

    def body(p_ref, w_ref, m_ref, v_ref, g_ref, d_ref, mo_ref, vo_ref):
        g = p_ref[0]
        for k in range(1, NDEV):
            g = g + p_ref[k]
        g_ref[...] = g
        d_ref[...], mo_ref[...], vo_ref[...] = _adamw_math(w_ref[...], g, m_ref[...], v_ref[...])

    blk = pl.BlockSpec((rb, c_), lambda i: (i, 0))
    return pl.pallas_call(
        body, name="adamw_big", grid=(r // rb,),
        in_specs=[pl.BlockSpec((NDEV, rb, c_), lambda i: (0, i, 0)), blk, blk, blk],
        out_specs=[blk, blk, blk, blk],
        out_shape=[jax.ShapeDtypeStruct((r, c_), F32)] * 4,
        compiler_params=_params("arbitrary"),
    )(parts, w, m, v)


def _sum_parts(parts, loss_row):
    _, r, c_ = parts.shape

    def body(p_ref, o_ref, l_ref):
        g = p_ref[0]
        for k in range(1, NDEV):
            g = g + p_ref[k]
        o_ref[...] = g
        per_row = jnp.sum(o_ref[loss_row:loss_row + 8, :], axis=1, keepdims=True)
        l_ref[...] = jnp.broadcast_to(jnp.sum(per_row, axis=0, keepdims=True), (8, c_))

    return pl.pallas_call(body, name="sum_small", out_shape=[jax.ShapeDtypeStruct((r, c_), F32),
                                                             jax.ShapeDtypeStruct((8, c_), F32)])(parts)


def _adamw_small(g, w, m, v):
    def body(g_ref, w_ref, m_ref, v_ref, d_ref, mo_ref, vo_ref):
        d_ref[...], mo_ref[...], vo_ref[...] = _adamw_math(w_ref[...], g_ref[...], m_ref[...], v_ref[...])

    return pl.pallas_call(body, name="adamw_small", out_shape=[jax.ShapeDtypeStruct(g.shape, F32)] * 3)(g, w, m, v)


def _pack_big(w1in, w2in, w1out, w2out, dtype):
    parts = [w1in.reshape(ROWS_W1IN, PACK_W), w2in.reshape(ROWS_W2IN, PACK_W), w1out.reshape(ROWS_WOUT, PACK_W),
             w2out.reshape(ROWS_WOUT, PACK_W)]
    used = ROWS_W1IN + ROWS_W2IN + 2 * ROWS_WOUT
    parts.append(jnp.zeros((PACK_ROWS - used, PACK_W), parts[0].dtype))
    return jnp.concatenate(parts, axis=0).astype(dtype)


def _unpack_big(p):
    a, b, c = ROWS_W1IN, ROWS_W1IN + ROWS_W2IN, ROWS_W1IN + ROWS_W2IN + ROWS_WOUT
    return (p[:a].reshape(1, D, 512), p[a:b].reshape(1, D, 514), p[b:c].reshape(1, 128, D),
            p[c:c + ROWS_WOUT].reshape(1, 128, D))


def _pad_lanes(a):
    return jnp.pad(a, ((0, 0), (0, LANES - a.shape[1])))


def _heads(a, s):
    return a.reshape(s, H, DH).transpose(1, 0, 2)


def _unheads(a, s):
    return a.transpose(1, 0, 2).reshape(s, D)


def _aug(cols, s):
    used = sum(c.shape[-1] for c in cols)
    return jnp.concatenate(cols + [jnp.zeros((H, s, HA - used), BF16)], axis=-1)


def _terms(v):
    return [t[..., None] for t in _split3(v)]


def _tiles(a, t):
    return a.reshape(H, a.shape[1] // t, t, HA)


def _tiles_t(a, t):
    return _tiles(a, t).transpose(0, 1, 3, 2)


def kernel(x, conv_norm_g, conv_w_in, conv_w, conv_w_out, attn_norm_g, attn_w_in, attn_b_f, attn_q_norm_g, attn_k_norm_g, attn_w_out, loss_target, m_conv_norm_g, m_conv_w_in, m_conv_w, m_conv_w_out, m_attn_norm_g, m_attn_w_in, m_attn_b_f, m_attn_q_norm_g, m_attn_k_norm_g, m_attn_w_out, v_conv_norm_g, v_conv_w_in, v_conv_w, v_conv_w_out, v_attn_norm_g, v_attn_w_in, v_attn_b_f, v_attn_q_norm_g, v_attn_k_norm_g, v_attn_w_out):
    s = x.shape[1]
    tq = min(TQ, s)
    tmf, tmb = min(TM_FWD, s), min(TM_BWD, s)
    me = 4 * lax.axis_index("x") + 2 * lax.axis_index("y") + lax.axis_index("c")
    xv, tgt = x[0], loss_target[0]

    wg = _all_gather(_pack_big(conv_w_in[0], attn_w_in[0], conv_w_out[0], attn_w_out[0], BF16), "gather_weights")
    small_w = jnp.concatenate([conv_w[0], attn_norm_g, jnp.zeros((4, 128), F32)], axis=0)
    sg_ = _all_gather(small_w, "gather_small_weights")
    a, b, c = ROWS_W1IN, ROWS_W1IN + ROWS_W2IN, ROWS_W1IN + ROWS_W2IN + ROWS_WOUT
    w1in = wg[:, :a].transpose(1, 0, 2).reshape(D, 4 * D)
    w2all = wg[:, a:b].reshape(NDEV, D, 514).transpose(1, 0, 2).reshape(D, 4 * D + H)
    w2in, wf = w2all[:, :4 * D], _pad_lanes(w2all[:, 4 * D:])
    w1out = wg[:, b:c].reshape(D, D)
    w2out = wg[:, c:c + ROWS_WOUT].reshape(D, D)
    cw = jnp.concatenate([sg_[:, 0:3, :].transpose(1, 0, 2).reshape(3, D), jnp.zeros((5, D), F32)], axis=0)
    g2 = sg_[:, 3, :].reshape(1, D)
    qg_t, kg_t = jnp.tile(attn_q_norm_g, (1, H)), jnp.tile(attn_k_norm_g, (1, H))
    bf = _pad_lanes(attn_b_f)

    e = (jnp.arange(D)[:, None] // DH == jnp.arange(LANES)[None, :]).astype(BF16)
    fold = (jnp.arange(D)[:, None] % DH == jnp.arange(LANES)[None, :]).astype(BF16)
    tril = (jnp.arange(tmf)[:, None] >= jnp.arange(tmf)[None, :]).astype(BF16)
    triu = (jnp.arange(tmb)[:, None] <= jnp.arange(tmb)[None, :]).astype(BF16)

    x1, h1, p1, yg = _conv_fwd(xv, conv_norm_g, w1in, cw, w1out)
    h2, qpre, kpre, qs, kn, v, z, invq, invk, cc, fl, qlen, klen, kmax = _attn_proj_fwd(
        x1, g2, w2in, wf, bf, qg_t, kg_t, e, e.T, tril)
    ones3 = [jnp.ones((H, s, 3), BF16)]
    qh, kh, vh = _heads(qs, s), _heads(kn, s), _heads(v, s)
    ka = _aug([kh] + _terms(-cc[:, :H].T) + ones3, s)
    va = _aug([vh] + ones3, s)
    kat = _tiles_t(ka, tq)
    reach = qlen[:, :H] * kmax[:, :H]
    qa = _tiles(_aug([qh] + ones3 + _terms((cc[:, :H] - reach).T), s), tq)
    ol, qab = lax.cond(2.0 * jnp.max(reach) <= BOUNDED_SOFTMAX_REACH,
                       lambda: _flash_fwd(qa, kat, _tiles(va, tq), True),
                       lambda: _flash_fwd(qa, kat, _tiles(va, tq), False))
    ol = ol.reshape(H, s, HA)
    o = _unheads(ol[:, :, :DH], s)
    og, dx2, do, dz, delta, lossp = _attn_out(o, z, x1, tgt, w2out, w2out.T, e)

    doa = _aug([_heads(do, s)] + _terms(-delta[:, :H].T), s)
    dq, dkt, dvt = _flash_bwd(qab, qab.transpose(0, 1, 3, 2), _tiles(doa, tq), _tiles_t(doa, tq), kat,
                              _tiles(ka, tq), _tiles_t(va, tq))
    dq = dq.reshape(H, s, HA)
    dqs = _unheads(dq[:, :, :DH], s)
    dkn = dkt[:, :DH, :].transpose(2, 0, 1).reshape(s, D)
    dvv = dvt[:, :DH, :].transpose(2, 0, 1).reshape(s, D)
    dcs = _pad_lanes((dq[:, :, DH + 3] - dkt[:, DH, :]).T)
    dx1, dp2, df, dg2, dqg, dkg, dbf = _attn_proj_bwd(
        dqs, dkn, dvv, dz, dcs, qpre, kpre, invq, invk, fl, x1, dx2, g2, qg_t, kg_t, w2in.T, wf.T, e, e.T, triu, fold)
    gx, dp1, dcw, dg1 = _conv_bwd(dx1, p1, xv, conv_norm_g, cw, w1out.T, w1in.T)
    dw2out = _wgrad(og.T, dx2, "wgrad_attn_out")
    dw2in = _wgrad(h2.T, dp2, "wgrad_attn_in")
    dwf = _wgrad(h2.T, df, "wgrad_attn_forget")
    dw1out = _wgrad(yg.T, dx1, "wgrad_conv_out")
    dw1in = _wgrad(h1.T, dp1, "wgrad_conv_in")

    dw2all = jnp.concatenate([dw2in, dwf[:, :H]], axis=1)
    gfull = jnp.concatenate([
        dw1in.reshape(D, NDEV, 512).transpose(1, 0, 2),
        dw2all.reshape(D, NDEV, 514).transpose(1, 0, 2).reshape(NDEV, ROWS_W2IN, PACK_W),
        dw1out.reshape(NDEV, ROWS_WOUT, PACK_W), dw2out.reshape(NDEV, ROWS_WOUT, PACK_W),
        jnp.zeros((NDEV, PACK_ROWS - c - ROWS_WOUT, PACK_W), F32)], axis=1)
    parts = _exchange_blocks(gfull, "scatter_grads")
    pack = lambda t1, t2, t3, t4: _pack_big(t1[0], t2[0], t3[0], t4[0], F32)
    gb, db, mb, vb = _adamw_big(parts, pack(conv_w_in, attn_w_in, conv_w_out, attn_w_out),
                                pack(m_conv_w_in, m_attn_w_in, m_conv_w_out, m_attn_w_out),
                                pack(v_conv_w_in, v_attn_w_in, v_conv_w_out, v_attn_w_out))

    shard_rows = jnp.concatenate([dcw[0:3].reshape(3, NDEV, 128).transpose(1, 0, 2), dg2.reshape(NDEV, 1, 128),
                                  jnp.zeros((NDEV, 4, 128), F32)], axis=1).reshape(64, 128)
    small_g = jnp.concatenate([shard_rows, dg1.reshape(8, 128), dbf, dqg[0:1], dkg[0:1], jnp.zeros((5, 128), F32),
                               lossp.reshape(8, 128)], axis=0)
    gs, loss8 = _sum_parts(_all_gather(small_g, "gather_small_grads"), 80)
    loss = loss8[0, 0]
    mine = lax.dynamic_slice(gs, (8 * me, 0), (8, 128))
    g_small = jnp.concatenate([mine, gs[64:75], jnp.zeros((5, 128), F32)], axis=0)

    def pack_small(cwk, ang, cng, bfk, qgk, kgk):
        return jnp.concatenate([cwk[0], ang, jnp.zeros((4, 128), F32), cng.reshape(8, 128), _pad_lanes(bfk),
                                _pad_lanes(qgk), _pad_lanes(kgk), jnp.zeros((5, 128), F32)], axis=0)

    ds_, ms_, vs_ = _adamw_small(
        g_small, pack_small(conv_w, attn_norm_g, conv_norm_g, attn_b_f, attn_q_norm_g, attn_k_norm_g),
        pack_small(m_conv_w, m_attn_norm_g, m_conv_norm_g, m_attn_b_f, m_attn_q_norm_g, m_attn_k_norm_g),
        pack_small(v_conv_w, v_attn_norm_g, v_conv_norm_g, v_attn_b_f, v_attn_q_norm_g, v_attn_k_norm_g))

    def leaves(big, small):
        w1i, w2i, w1o, w2o = _unpack_big(big)
        return (small[8:16].reshape(1, D), w1i, small[0:3].reshape(1, 3, 128), w1o, small[3:4], w2i,
                small[16:17, :H], small[17:18, :DH], small[18:19, :DH], w2o)

    return (loss, gx[None], *leaves(gb, g_small), *leaves(db, ds_), *leaves(mb, ms_), *leaves(vb, vs_))
```

```python
import functools
import math

import jax
import jax.numpy as jnp
from jax import lax
from jax.experimental import pallas as pl
from jax.experimental.pallas import tpu as pltpu

F32 = jnp.float32
BF16 = jnp.bfloat16

D = 1024
H = 16
DH = 64
NDEV = 8
RMS_EPS = 1e-6
LANES = 128
HA = 128
TM_FWD = 512
TM_BWD = 256
TQ = 512
TK = 512
CH = 256
PACK_W = 512
PACK_ROWS = 2576
ROWS_W1IN, ROWS_W2IN, ROWS_WOUT = 1024, 1028, 256
ADAM_LR, ADAM_B1, ADAM_B2, ADAM_EPS, ADAM_WD, ADAM_STEP = 0.001, 0.9, 0.999, 1e-08, 0.01, 10
VMEM_LIMIT = 56 * 1024 * 1024
MASKED = -1e30
BOUNDED_SOFTMAX_REACH = 60.0
MESH = pl.DeviceIdType.MESH


def _params(*sem, vmem=VMEM_LIMIT):
    return pltpu.CompilerParams(dimension_semantics=sem or None, vmem_limit_bytes=vmem)


def _const(shape):
    nd = len(shape)
    return pl.BlockSpec(shape, lambda *_: (0,) * nd, pipeline_mode=pl.Buffered(1))


def _rows(tm, n, rev=False, steps=None):
    if rev:
        return pl.BlockSpec((tm, n), lambda i: (steps - 1 - i, 0))
    return pl.BlockSpec((tm, n), lambda i: (i, 0))


def _dot(a, b):
    return jnp.dot(a, b, preferred_element_type=F32)


def _top16(x):
    bits = lax.bitcast_convert_type(x, jnp.uint32) & jnp.uint32(0xFFFF0000)
    return lax.bitcast_convert_type(bits, F32)


def _split2(x):
    hi = _top16(x)
    return hi.astype(BF16), (x - hi).astype(BF16)


def _split3(x):
    hi = _top16(x)
    r = x - hi
    mid = _top16(r)
    return hi.astype(BF16), mid.astype(BF16), (r - mid).astype(BF16)


def _seg_sum(a, e):
    hi, lo = _split2(a)
    return _dot(hi, e) + _dot(lo, e)


def _seg_bcast(s, et):
    hi, mid, lo = _split3(s)
    return _dot(hi, et) + _dot(mid, et) + _dot(lo, et)


def _tri_sum(t, v):
    hi, mid, lo = _split3(v)
    return _dot(t, hi) + _dot(t, mid) + _dot(t, lo)


def _sigmoid(z):
    return 1.0 / (1.0 + jnp.exp(-z))


def _place():
    return lax.axis_index("x"), lax.axis_index("y"), lax.axis_index("c")


def _all_gather(xb, name):
    r, c_ = xb.shape

    def body(x_ref, out_ref, send_sems, recv_sems, local_sem):
        x, y, c = _place()
        me, sibling = (x, y, c), (x, y, 1 - c)
        chips = [(1 - x, y), (x, 1 - y), (1 - x, 1 - y)]

        def slab(px, py, pc):
            return out_ref.at[4 * px + 2 * py + pc]

        def copy(k, block, to, src=None):
            return pltpu.make_async_remote_copy(
                src_ref=slab(*block) if src is None else src, dst_ref=slab(*block),
                send_sem=send_sems.at[k], recv_sem=recv_sems.at[k], device_id=to, device_id_type=MESH)

        mine = pltpu.make_async_copy(x_ref, slab(*me), local_sem)
        mine.start()
        first = [copy(0, me, sibling, src=x_ref)]
        first += [copy(1 + j, me, (*chip, c), src=x_ref) for j, chip in enumerate(chips)]
        for cp in first:
            cp.start()
        passed = [copy(4 + j, (*chip, c), sibling) for j, chip in enumerate(chips)]
        for j, chip in enumerate(chips):
            copy(1 + j, (*chip, c), me).wait_recv()
            passed[j].start()
        copy(0, sibling, me).wait_recv()
        for j, chip in enumerate(chips):
            copy(4 + j, (*chip, 1 - c), me).wait_recv()
        for cp in first + passed:
            cp.wait_send()
        mine.wait()

    return pl.pallas_call(
        body, name=name,
        out_shape=jax.ShapeDtypeStruct((NDEV, r, c_), xb.dtype),
        in_specs=[pl.BlockSpec(memory_space=pl.ANY)],
        out_specs=pl.BlockSpec(memory_space=pl.ANY),
        scratch_shapes=[pltpu.SemaphoreType.DMA((7,)), pltpu.SemaphoreType.DMA((7,)), pltpu.SemaphoreType.DMA],
    )(xb)


def _exchange_blocks(g, name):
    _, r, c_ = g.shape

    def body(g_ref, out_ref, send_sems, recv_sems, local_sem):
        x, y, c = _place()
        me = 4 * x + 2 * y + c
        mine = pltpu.make_async_copy(g_ref.at[me], out_ref.at[0], local_sem)
        mine.start()
        copies = []
        for k in range(1, NDEV):
            px = 1 - x if k & 4 else x
            py = 1 - y if k & 2 else y
            pc = 1 - c if k & 1 else c
            cp = pltpu.make_async_remote_copy(
                src_ref=g_ref.at[4 * px + 2 * py + pc], dst_ref=out_ref.at[k],
                send_sem=send_sems.at[k - 1], recv_sem=recv_sems.at[k - 1],
                device_id=(px, py, pc), device_id_type=MESH)
            cp.start()
            copies.append(cp)
        for cp in copies:
            cp.wait()
        mine.wait()

    return pl.pallas_call(
        body, name=name,
        out_shape=jax.ShapeDtypeStruct((NDEV, r, c_), g.dtype),
        in_specs=[pl.BlockSpec(memory_space=pl.ANY)],
        out_specs=pl.BlockSpec(memory_space=pl.ANY),
        scratch_shapes=[pltpu.SemaphoreType.DMA((7,)), pltpu.SemaphoreType.DMA((7,)), pltpu.SemaphoreType.DMA],
    )(g)


def _conv_fwd(x, g1, w1in, cw, w1out):
    s = x.shape[0]
    tm = min(TM_FWD, s)
    steps = s // tm

    def body(x_ref, g_ref, win_ref, cw_ref, wout_ref, x1_ref, h1_ref, p1_ref, yg_ref, tail_ref):
        @pl.when(pl.program_id(0) == 0)
        def _():
            tail_ref[...] = jnp.zeros_like(tail_ref)

        xv = x_ref[...]
        inv = lax.rsqrt(jnp.mean(xv * xv, axis=-1, keepdims=True) + RMS_EPS)
        h = (xv * inv * g_ref[...]).astype(BF16)
        h1_ref[...] = h
        row = lax.broadcasted_iota(jnp.int32, (tm, CH), 0)
        for ci in range(D // CH):
            lo, hi = ci * CH, (ci + 1) * CH
            parts = []
            for k in range(4):
                pk = _dot(h, win_ref[:, k * D + lo:k * D + hi]).astype(BF16)
                p1_ref[:, k * D + lo:k * D + hi] = pk
                parts.append(pk.astype(F32))
            b, c, xin, z = parts
            u = c * xin
            t6 = tail_ref[6:7, lo:hi]
            t7 = tail_ref[7:8, lo:hi]
            u1 = jnp.where(row == 0, t7, pltpu.roll(u, 1, 0))
            u2 = jnp.where(row == 0, t6, jnp.where(row == 1, t7, pltpu.roll(u, 2, 0)))
            tail_ref[:, lo:hi] = u[tm - 8:, :]
            y = cw_ref[2:3, lo:hi] * u + cw_ref[1:2, lo:hi] * u1 + cw_ref[0:1, lo:hi] * u2
            yg_ref[:, lo:hi] = (b * y * (z * _sigmoid(z))).astype(BF16)
        x1_ref[...] = xv + _dot(yg_ref[...], wout_ref[...])

    return pl.pallas_call(
        body, name="conv_fwd", grid=(steps,),
        in_specs=[_rows(tm, D), _const((1, D)), _const((D, 4 * D)), _const((8, D)), _const((D, D))],
        out_specs=[_rows(tm, D), _rows(tm, D), _rows(tm, 4 * D), _rows(tm, D)],
        out_shape=[jax.ShapeDtypeStruct((s, D), F32), jax.ShapeDtypeStruct((s, D), BF16),
                   jax.ShapeDtypeStruct((s, 4 * D), BF16), jax.ShapeDtypeStruct((s, D), BF16)],
        scratch_shapes=[pltpu.VMEM((8, D), F32)],
        compiler_params=_params("arbitrary"),
    )(x, g1, w1in, cw, w1out)


COL_BIAS = DH
ROW_BIAS = DH + 3


def _terms_cat(val):
    return jnp.concatenate(_split3(val), axis=1)


def _split_heads(x, aug, out_ref, lane):
    for hp in range(H // 2):
        pair = x[:, hp * HA:(hp + 1) * HA]
        out_ref[2 * hp] = jnp.where(lane < DH, pair, aug(2 * hp)).astype(BF16)
        out_ref[2 * hp + 1] = jnp.where(lane < DH, pltpu.roll(pair, DH, 1), aug(2 * hp + 1)).astype(BF16)


def _ones_at(lane, first):
    return jnp.where((lane >= first) & (lane < first + 3), 1.0, 0.0)


def _attn_proj_fwd(x1, g2, w2in, wf, bf, qg, kg, shift, e, et, tril, place_k, place_q):
    s = x1.shape[0]
    tm = min(TM_BWD, s)
    steps = s // tm

    def body(x_ref, g_ref, win_ref, wf_ref, bf_ref, qg_ref, kg_ref, sh_ref, e_ref, et_ref, tril_ref, pk_ref, pq_ref,
             h2_ref, qp_ref, kp_ref, z_ref, qa_ref, ka_ref, va_ref, iq_ref, ik_ref, c_ref, fl_ref, carry_ref):
        @pl.when(pl.program_id(0) == 0)
        def _():
            carry_ref[...] = jnp.zeros_like(carry_ref)

        xv = x_ref[...]
        inv = lax.rsqrt(jnp.mean(xv * xv, axis=-1, keepdims=True) + RMS_EPS)
        h = (xv * inv * g_ref[...]).astype(BF16)
        h2_ref[...] = h

        fl = _dot(h, wf_ref[...]) + bf_ref[...]
        fl_ref[...] = fl
        ex = jnp.exp(-jnp.abs(fl))
        up = 1.0 + ex
        log1p = jnp.where(up == 1.0, ex, jnp.log(up) * ex / (up - 1.0))
        lane = lax.broadcasted_iota(jnp.int32, (tm, LANES), 1)
        logf = jnp.where(lane < H, jnp.minimum(fl, 0.0) - log1p, 0.0)
        c_ref[...] = _tri_sum(tril_ref[...], logf) + carry_ref[0:1, :]
        carry_ref[0:1, :] = c_ref[tm - 1:tm, :]
        c = c_ref[...]

        def normed(col, pre_ref, inv_ref, gain_ref, scale):
            pre = _dot(h, win_ref[:, col * D:(col + 1) * D]).astype(BF16)
            pre_ref[...] = pre
            t = pre.astype(F32)
            invh = lax.rsqrt(_seg_sum(t * t, e_ref[...]) * (1.0 / DH) + RMS_EPS)
            inv_ref[...] = invh
            return t * _seg_bcast(invh, et_ref[...]) * (gain_ref[...] * scale)

        ones_col, ones_row = _ones_at(lane, COL_BIAS), _ones_at(lane, ROW_BIAS)
        q_bias = _dot(_terms_cat(c - sh_ref[...]), pq_ref[...])
        _split_heads(normed(0, qp_ref, iq_ref, qg_ref, 1.0 / math.sqrt(DH)),
                     lambda hh: q_bias[:, hh * HA:(hh + 1) * HA] + ones_col, qa_ref, lane)
        k_bias = _dot(_terms_cat(-c), pk_ref[...])
        _split_heads(normed(1, kp_ref, ik_ref, kg_ref, 1.0),
                     lambda hh: k_bias[:, hh * HA:(hh + 1) * HA] + ones_row, ka_ref, lane)
        _split_heads(_dot(h, win_ref[:, 2 * D:3 * D]), lambda hh: ones_col, va_ref, lane)
        z_ref[...] = _dot(h, win_ref[:, 3 * D:4 * D]).astype(BF16)

    row_bf = lambda: _rows(tm, D)
    row_sm = lambda: _rows(tm, LANES)
    heads = lambda: pl.BlockSpec((H, tm, HA), lambda i: (0, i, 0))
    return pl.pallas_call(
        body, name="attn_proj_fwd", grid=(steps,),
        in_specs=[_rows(tm, D), _const((1, D)), _const((D, 4 * D)), _const((D, LANES)), _const((1, LANES)),
                  _const((1, D)), _const((1, D)), _const((1, LANES)), _const((D, LANES)), _const((LANES, D)),
                  _const((tm, tm)), _const((3 * LANES, H * HA)), _const((3 * LANES, H * HA))],
        out_specs=[row_bf() for _ in range(4)] + [heads() for _ in range(3)] + [row_sm() for _ in range(4)],
        out_shape=[jax.ShapeDtypeStruct((s, D), BF16)] * 4 + [jax.ShapeDtypeStruct((H, s, HA), BF16)] * 3 + [
            jax.ShapeDtypeStruct((s, LANES), F32)] * 4,
        scratch_shapes=[pltpu.VMEM((8, LANES), F32)],
        compiler_params=_params("arbitrary"),
    )(x1, g2, w2in, wf, bf, qg, kg, shift, e, et, tril, place_k, place_q)


def _dot_nt(a, b):
    return lax.dot_general(a, b, (((1,), (1,)), ((), ())), preferred_element_type=F32)


def _dot_tn(a, b):
    return lax.dot_general(a, b, (((0,), (0,)), ((), ())), preferred_element_type=F32)


def _flash_fwd(bounded, qa, ka, va):
    h_, nq, tq, _ = qa.shape
    nk, tk = ka.shape[1], ka.shape[2]

    def body(flag_ref, q_ref, k_ref, v_ref, o_ref, qb_ref, m_ref, acc_ref):
        i = pl.program_id(1)
        lane = lax.broadcasted_iota(jnp.int32, (tq, HA), 1)
        in_bias = (lane >= ROW_BIAS) & (lane < ROW_BIAS + 3)
        acc_ref[...] = jnp.zeros_like(acc_ref)

        def run(use_bound):
            q = q_ref[...]
            if use_bound:
                m_ref[...] = -jnp.sum(jnp.where(in_bias, q.astype(F32), 0.0), axis=1, keepdims=True)
            else:
                q = jnp.where(in_bias, jnp.zeros_like(q), q)
                m_ref[...] = jnp.full_like(m_ref, MASKED)

            def step(j, masked):
                z = _dot_nt(q, k_ref[j])
                if masked:
                    rows = lax.broadcasted_iota(jnp.int32, (tq, tk), 0)
                    cols = lax.broadcasted_iota(jnp.int32, (tq, tk), 1)
                    z = jnp.where(rows >= cols, z, MASKED)
                if use_bound:
                    acc_ref[...] += _dot(jnp.exp(z).astype(BF16), v_ref[j])
                else:
                    m_old = m_ref[...]
                    m_new = jnp.maximum(m_old, jnp.max(z, axis=1, keepdims=True))
                    p = jnp.exp(z - m_new)
                    acc_ref[...] = jnp.exp(m_old - m_new) * acc_ref[...] + _dot(p.astype(BF16), v_ref[j])
                    m_ref[...] = m_new

            def loop_body(j, carry):
                step(j, False)
                return carry

            lax.fori_loop(0, i, loop_body, 0)
            step(i, True)

        use_bound = flag_ref[0, 0] > 0.5
        pl.when(use_bound)(lambda: run(True))
        pl.when(jnp.logical_not(use_bound))(lambda: run(False))

        acc = acc_ref[...]
        l = jnp.sum(jnp.where(lane == COL_BIAS, acc, 0.0), axis=1, keepdims=True)
        lse = m_ref[...] + jnp.log(l)
        o_ref[...] = jnp.where(lane < DH, acc / l, lse)
        hi, mid, lo = _split3(-lse)
        qb_ref[...] = jnp.where(lane == ROW_BIAS, hi, jnp.where(lane == ROW_BIAS + 1, mid,
                                                                jnp.where(lane == ROW_BIAS + 2, lo, q_ref[...])))

    tile = lambda: pl.BlockSpec((None, None, tq, HA), lambda h, i: (h, i, 0, 0))
    head = lambda: pl.BlockSpec((None, nk, tk, HA), lambda h, i: (h, 0, 0, 0))
    return pl.pallas_call(
        body, name="flash_fwd", grid=(h_, nq),
        in_specs=[pl.BlockSpec(memory_space=pltpu.SMEM), tile(), head(), head()],
        out_specs=[tile(), tile()],
        out_shape=[jax.ShapeDtypeStruct((h_, nq, tq, HA), F32), jax.ShapeDtypeStruct((h_, nq, tq, HA), BF16)],
        scratch_shapes=[pltpu.VMEM((tq, 1), F32), pltpu.VMEM((tq, HA), F32)],
        compiler_params=_params("arbitrary", "arbitrary"),
    )(bounded, qa, ka, va)


def _merge_heads(src, out_ref, lane):
    for hp in range(H // 2):
        out_ref[:, hp * HA:(hp + 1) * HA] = jnp.where(lane < DH, src(2 * hp), pltpu.roll(src(2 * hp + 1), DH, 1))


def _attn_out(ol, z, x1, tgt, w2out, w2out_t, e, place_k):
    s = z.shape[0]
    tm = min(TM_BWD, s)
    steps = s // tm

    def body(ol_ref, z_ref, x1_ref, t_ref, w_ref, wt_ref, e_ref, pk_ref, og_ref, dx2_ref, dz_ref, doa_ref, loss_ref,
             o_ref):
        @pl.when(pl.program_id(0) == 0)
        def _():
            loss_ref[...] = jnp.zeros_like(loss_ref)

        lane = lax.broadcasted_iota(jnp.int32, (tm, HA), 1)
        _merge_heads(lambda hh: ol_ref[hh], o_ref, lane)
        ov = o_ref[...]
        zv = z_ref[...].astype(F32)
        sg = _sigmoid(zv)
        sil = zv * sg
        og = (ov * sil).astype(BF16)
        og_ref[...] = og
        err = x1_ref[...] + _dot(og, w_ref[...]) - t_ref[...]
        loss_ref[...] += (0.5 / D) * jnp.sum(err * err, axis=0, keepdims=True)
        dx2 = err * (1.0 / D)
        dx2_ref[...] = dx2
        dog = _dot(dx2.astype(BF16), wt_ref[...])
        do = (dog * sil).astype(BF16).astype(F32)
        dz_ref[...] = (dog * ov * (sg * (1.0 + zv * (1.0 - sg)))).astype(BF16)
        delta = _seg_sum(do * ov, e_ref[...])
        d_bias = _dot(_terms_cat(-delta), pk_ref[...])
        _split_heads(do, lambda hh: d_bias[:, hh * HA:(hh + 1) * HA], doa_ref, lane)

    heads = lambda: pl.BlockSpec((H, tm, HA), lambda i: (0, i, 0))
    return pl.pallas_call(
        body, name="attn_out", grid=(steps,),
        in_specs=[heads(), _rows(tm, D), _rows(tm, D), _rows(tm, D), _const((D, D)), _const((D, D)),
                  _const((D, LANES)), _const((3 * LANES, H * HA))],
        out_specs=[_rows(tm, D), _rows(tm, D), _rows(tm, D), heads(), pl.BlockSpec((1, D), lambda i: (0, 0))],
        out_shape=[jax.ShapeDtypeStruct((s, D), BF16), jax.ShapeDtypeStruct((s, D), F32),
                   jax.ShapeDtypeStruct((s, D), BF16), jax.ShapeDtypeStruct((H, s, HA), BF16),
                   jax.ShapeDtypeStruct((1, D), F32)],
        scratch_shapes=[pltpu.VMEM((tm, D), F32)],
        compiler_params=_params("arbitrary"),
    )(ol, z, x1, tgt, w2out, w2out_t, e, place_k)


def _flash_bwd(qab, doa, ka, va):
    h_, nq, tq, _ = qab.shape
    nk, tk = ka.shape[1], ka.shape[2]
    s = nk * tk

    def body(qa_ref, da_ref, ka_ref, va_ref, dq_hbm, dkt_ref, dvt_ref, dq_acc, sem):
        hh = pl.program_id(0)
        j = pl.program_id(1)

        @pl.when(j == 0)
        def _():
            dq_acc[...] = jnp.zeros_like(dq_acc)

        dkt_ref[...] = jnp.zeros_like(dkt_ref)
        dvt_ref[...] = jnp.zeros_like(dvt_ref)
        ka_v = ka_ref[...]
        va_v = va_ref[...]

        def step(i, masked):
            qa_i, da_i = qa_ref[i], da_ref[i]
            zz = _dot_nt(qa_i, ka_v)
            if masked:
                rows = lax.broadcasted_iota(jnp.int32, (tq, tk), 0)
                cols = lax.broadcasted_iota(jnp.int32, (tq, tk), 1)
                zz = jnp.where(rows >= cols, zz, MASKED)
            p = jnp.exp(zz)
            ds = (p * _dot_nt(da_i, va_v)).astype(BF16)
            pb = p.astype(BF16)
            dq_acc[i] += _dot(ds, ka_v)
            dvt_ref[...] += _dot_tn(da_i, pb)
            dkt_ref[...] += _dot_tn(qa_i, ds)

        step(j, True)

        def loop_body(i, carry):
            step(i, False)
            return carry

        lax.fori_loop(j + 1, nq, loop_body, 0)

        @pl.when(j == nk - 1)
        def _():
            cp = pltpu.make_async_copy(dq_acc, dq_hbm.at[hh], sem)
            cp.start()
            cp.wait()

    whole = lambda: pl.BlockSpec((None, nq, tq, HA), lambda h, j: (h, 0, 0, 0))
    tile = lambda: pl.BlockSpec((None, None, tk, HA), lambda h, j: (h, j, 0, 0))
    return pl.pallas_call(
        body, name="flash_bwd", grid=(h_, nk),
        in_specs=[whole(), whole(), tile(), tile()],
        out_specs=[pl.BlockSpec(memory_space=pl.ANY),
                   pl.BlockSpec((None, HA, tk), lambda h, j: (h, 0, j)),
                   pl.BlockSpec((None, HA, tk), lambda h, j: (h, 0, j))],
        out_shape=[jax.ShapeDtypeStruct((h_, nq, tq, HA), F32), jax.ShapeDtypeStruct((h_, HA, s), F32),
                   jax.ShapeDtypeStruct((h_, HA, s), F32)],
        scratch_shapes=[pltpu.VMEM((nq, tq, HA), F32), pltpu.SemaphoreType.DMA],
        compiler_params=_params("arbitrary", "arbitrary"),
    )(qab, doa, ka, va)


def _attn_proj_bwd(dq, dkt, dvt, dz, dcs, qpre, kpre, invq, invk, fl, x1, dx2, g2, qg, kg, w2in_t, wf_t, e, et,
                   triu, fold):
    s = x1.shape[0]
    tm = min(TM_BWD, s)
    steps = s // tm

    def body(dq_ref, dkt_ref, dvt_ref, dz_ref, dcs_ref, qp_ref, kp_ref, iq_ref, ik_ref, fl_ref, x1_ref, dx2_ref,
             g2_ref, qg_ref, kg_ref, wt_ref, wft_ref, e_ref, et_ref, triu_ref, fold_ref,
             dx1_ref, dp_ref, df_ref, dg2_ref, dqg_ref, dkg_ref, dbf_ref, carry_ref, rc_ref, qcol_ref, kcol_ref,
             dqs_ref, dkn_ref, dv_ref):
        step = pl.program_id(0)
        lane = lax.broadcasted_iota(jnp.int32, (tm, HA), 1)
        _merge_heads(lambda hh: dq_ref[hh], dqs_ref, lane)
        _merge_heads(lambda hh: dkt_ref[hh].T, dkn_ref, lane)
        _merge_heads(lambda hh: dvt_ref[hh].T, dv_ref, lane)

        @pl.when(step == 0)
        def _():
            carry_ref[...] = jnp.zeros_like(carry_ref)
            dg2_ref[...] = jnp.zeros_like(dg2_ref)
            dbf_ref[...] = jnp.zeros_like(dbf_ref)
            qcol_ref[...] = jnp.zeros_like(qcol_ref)
            kcol_ref[...] = jnp.zeros_like(kcol_ref)
            dqg_ref[...] = jnp.zeros_like(dqg_ref)
            dkg_ref[...] = jnp.zeros_like(dkg_ref)

        rc_ref[...] = _tri_sum(triu_ref[...], dcs_ref[...]) + carry_ref[0:1, :]
        carry_ref[0:1, :] = rc_ref[0:1, :]
        df = rc_ref[...] * _sigmoid(-fl_ref[...])
        dfb = df.astype(BF16)
        df_ref[...] = dfb
        dbf_ref[...] += jnp.sum(df, axis=0, keepdims=True)

        def norm_bwd(dn, pre_ref, inv_ref, gain_ref, col_ref):
            pre = pre_ref[...].astype(F32)
            invh = inv_ref[...]
            invb = _seg_bcast(invh, et_ref[...])
            col_ref[...] += jnp.sum(dn * pre * invb, axis=0, keepdims=True)
            gd = dn * gain_ref[...]
            mean = _seg_sum(gd * pre, e_ref[...]) * (1.0 / DH)
            return invb * gd - pre * _seg_bcast(mean * invh * invh * invh, et_ref[...])

        dq = norm_bwd(dqs_ref[...] * (1.0 / math.sqrt(DH)), qp_ref, iq_ref, qg_ref, qcol_ref).astype(BF16)
        dp_ref[:, 0:D] = dq
        dh = _dot(dq, wt_ref[0:D, :])
        dk = norm_bwd(dkn_ref[...], kp_ref, ik_ref, kg_ref, kcol_ref).astype(BF16)
        dp_ref[:, D:2 * D] = dk
        dh += _dot(dk, wt_ref[D:2 * D, :])
        dvb = dv_ref[...].astype(BF16)
        dp_ref[:, 2 * D:3 * D] = dvb
        dh += _dot(dvb, wt_ref[2 * D:3 * D, :])
        dzb = dz_ref[...]
        dp_ref[:, 3 * D:4 * D] = dzb
        dh += _dot(dzb, wt_ref[3 * D:4 * D, :])
        dh += _dot(dfb, wft_ref[...])

        xv = x1_ref[...]
        inv = lax.rsqrt(jnp.mean(xv * xv, axis=-1, keepdims=True) + RMS_EPS)
        dg2_ref[...] += jnp.sum(dh * xv * inv, axis=0, keepdims=True)
        gh = dh * g2_ref[...]
        dx1_ref[...] = dx2_ref[...] + inv * gh - xv * (inv * inv * inv * jnp.mean(gh * xv, axis=-1, keepdims=True))

        @pl.when(step == steps - 1)
        def _():
            dqg_ref[...] = _fold_heads(qcol_ref[...], fold_ref[...])
            dkg_ref[...] = _fold_heads(kcol_ref[...], fold_ref[...])

    rr = lambda n: _rows(tm, n, rev=True, steps=steps)
    acc = lambda n, r=1: pl.BlockSpec((r, n), lambda i: (0, 0))
    heads_t = lambda: pl.BlockSpec((H, HA, tm), lambda i: (0, 0, steps - 1 - i))
    return pl.pallas_call(
        body, name="attn_proj_bwd", grid=(steps,),
        in_specs=[pl.BlockSpec((H, tm, HA), lambda i: (0, steps - 1 - i, 0)), heads_t(), heads_t(),
                  rr(D), rr(LANES), rr(D), rr(D), rr(LANES), rr(LANES), rr(LANES), rr(D), rr(D),
                  _const((1, D)), _const((1, D)), _const((1, D)), _const((4 * D, D)), _const((LANES, D)),
                  _const((D, LANES)), _const((LANES, D)), _const((tm, tm)), _const((D, LANES))],
        out_specs=[rr(D), rr(4 * D), rr(LANES), acc(D), acc(LANES, 8), acc(LANES, 8), acc(LANES)],
        out_shape=[jax.ShapeDtypeStruct((s, D), F32), jax.ShapeDtypeStruct((s, 4 * D), BF16),
                   jax.ShapeDtypeStruct((s, LANES), BF16), jax.ShapeDtypeStruct((1, D), F32),
                   jax.ShapeDtypeStruct((8, LANES), F32), jax.ShapeDtypeStruct((8, LANES), F32),
                   jax.ShapeDtypeStruct((1, LANES), F32)],
        scratch_shapes=[pltpu.VMEM((8, LANES), F32), pltpu.VMEM((tm, LANES), F32), pltpu.VMEM((1, D), F32),
                        pltpu.VMEM((1, D), F32), pltpu.VMEM((tm, D), F32), pltpu.VMEM((tm, D), F32),
                        pltpu.VMEM((tm, D), F32)],
        compiler_params=_params("arbitrary"),
    )(dq, dkt, dvt, dz, dcs, qpre, kpre, invq, invk, fl, x1, dx2, g2, qg, kg, w2in_t, wf_t, e, et, triu, fold)


def _fold_heads(col, fold):
    hi, mid, lo = _split3(jnp.broadcast_to(col, (8, D)))
    return _dot(hi, fold) + _dot(mid, fold) + _dot(lo, fold)


def _conv_bwd(dx1, p1, x, g1, cw, w1out_t, w1in_t):
    s = x.shape[0]
    tm = min(TM_BWD, s)
    steps = s // tm
    halo = tm // 8

    def body(dx1_ref, b_ref, c_ref, xi_ref, z_ref, ch_ref, xh_ref, x_ref, g_ref, cw_ref, wot_ref, wit_ref,
             gx_ref, dp_ref, dcw_ref, dg1_ref, head_ref):
        step = pl.program_id(0)

        @pl.when(step == 0)
        def _():
            head_ref[...] = jnp.zeros_like(head_ref)
            dcw_ref[...] = jnp.zeros_like(dcw_ref)
            dg1_ref[...] = jnp.zeros_like(dg1_ref)

        first_tile = step == steps - 1
        dx1 = dx1_ref[...]
        row = lax.broadcasted_iota(jnp.int32, (tm, CH), 0)
        dyb = dx1.astype(BF16)
        for ci in range(D // CH):
            lo, hi = ci * CH, (ci + 1) * CH
            dyg = _dot(dyb, wot_ref[:, lo:hi])
            b = b_ref[:, lo:hi].astype(F32)
            c = c_ref[:, lo:hi].astype(F32)
            xin = xi_ref[:, lo:hi].astype(F32)
            z = z_ref[:, lo:hi].astype(F32)
            u = c * xin
            t6 = jnp.where(first_tile, 0.0, ch_ref[6:7, lo:hi].astype(F32) * xh_ref[6:7, lo:hi].astype(F32))
            t7 = jnp.where(first_tile, 0.0, ch_ref[7:8, lo:hi].astype(F32) * xh_ref[7:8, lo:hi].astype(F32))
            u1 = jnp.where(row == 0, t7, pltpu.roll(u, 1, 0))
            u2 = jnp.where(row == 0, t6, jnp.where(row == 1, t7, pltpu.roll(u, 2, 0)))
            w0, w1, w2 = cw_ref[0:1, lo:hi], cw_ref[1:2, lo:hi], cw_ref[2:3, lo:hi]
            y = w2 * u + w1 * u1 + w0 * u2
            sg = _sigmoid(z)
            sil = z * sg
            dp_ref[:, lo:hi] = (dyg * y * sil).astype(BF16)
            dy = dyg * b * sil
            dp_ref[:, 3 * D + lo:3 * D + hi] = (dyg * b * y * (sg * (1.0 + z * (1.0 - sg)))).astype(BF16)
            dcw_ref[2:3, lo:hi] += jnp.sum(dy * u, axis=0, keepdims=True)
            dcw_ref[1:2, lo:hi] += jnp.sum(dy * u1, axis=0, keepdims=True)
            dcw_ref[0:1, lo:hi] += jnp.sum(dy * u2, axis=0, keepdims=True)
            n0 = head_ref[0:1, lo:hi]
            n1 = head_ref[1:2, lo:hi]
            dyn1 = jnp.where(row == tm - 1, n0, pltpu.roll(dy, tm - 1, 0))
            dyn2 = jnp.where(row == tm - 2, n0, jnp.where(row == tm - 1, n1, pltpu.roll(dy, tm - 2, 0)))
            head_ref[:, lo:hi] = dy[0:8, :]
            du = w2 * dy + w1 * dyn1 + w0 * dyn2
            dp_ref[:, D + lo:D + hi] = (du * xin).astype(BF16)
            dp_ref[:, 2 * D + lo:2 * D + hi] = (du * c).astype(BF16)
        dh = _dot(dp_ref[:, 0:D], wit_ref[0:D, :])
        for k in range(1, 4):
            dh += _dot(dp_ref[:, k * D:(k + 1) * D], wit_ref[k * D:(k + 1) * D, :])
        xv = x_ref[...]
        inv = lax.rsqrt(jnp.mean(xv * xv, axis=-1, keepdims=True) + RMS_EPS)
        dg1_ref[...] += jnp.sum(dh * xv * inv, axis=0, keepdims=True)
        gh = dh * g_ref[...]
        gx_ref[...] = dx1 + inv * gh - xv * (inv * inv * inv * jnp.mean(gh * xv, axis=-1, keepdims=True))

    rr = lambda n: _rows(tm, n, rev=True, steps=steps)
    part = lambda k: pl.BlockSpec((tm, D), lambda i: (steps - 1 - i, k))
    prev8 = lambda k: pl.BlockSpec((8, D), lambda i: (jnp.maximum((steps - 1 - i) * halo - 1, 0), k))
    return pl.pallas_call(
        body, name="conv_bwd", grid=(steps,),
        in_specs=[rr(D), part(0), part(1), part(2), part(3), prev8(1), prev8(2), rr(D), _const((1, D)),
                  _const((8, D)), _const((D, D)), _const((4 * D, D))],
        out_specs=[rr(D), rr(4 * D), pl.BlockSpec((8, D), lambda i: (0, 0)), pl.BlockSpec((1, D), lambda i: (0, 0))],
        out_shape=[jax.ShapeDtypeStruct((s, D), F32), jax.ShapeDtypeStruct((s, 4 * D), BF16),
                   jax.ShapeDtypeStruct((8, D), F32), jax.ShapeDtypeStruct((1, D), F32)],
        scratch_shapes=[pltpu.VMEM((8, D), F32)],
        compiler_params=_params("arbitrary"),
    )(dx1, p1, p1, p1, p1, p1, p1, x, g1, cw, w1out_t, w1in_t)


def _wgrad(a, g, name):
    s, k = a.shape
    n = g.shape[1]
    bn = min(n, 1024)
    ts = min(512, s)

    def body(a_ref, g_ref, out_ref, acc_ref):
        t = pl.program_id(1)

        @pl.when(t == 0)
        def _():
            acc_ref[...] = jnp.zeros_like(acc_ref)

        acc_ref[...] += _dot_tn(a_ref[...], g_ref[...].astype(BF16))

        @pl.when(t == s // ts - 1)
        def _():
            out_ref[...] = acc_ref[...]

    return pl.pallas_call(
        body, name=name, grid=(n // bn, s // ts),
        in_specs=[pl.BlockSpec((ts, k), lambda j, t: (t, 0)), pl.BlockSpec((ts, bn), lambda j, t: (t, j))],
        out_specs=pl.BlockSpec((k, bn), lambda j, t: (0, j)),
        out_shape=jax.ShapeDtypeStruct((k, n), F32),
        scratch_shapes=[pltpu.VMEM((k, bn), F32)],
        compiler_params=_params("arbitrary", "arbitrary"),
    )(a, g)


def _adamw_math(w, g, m, v):
    m = ADAM_B1 * m + (1.0 - ADAM_B1) * g
    v = ADAM_B2 * v + (1.0 - ADAM_B2) * (g * g)
    m_hat = m / (1.0 - ADAM_B1 ** ADAM_STEP)
    v_hat = v / (1.0 - ADAM_B2 ** ADAM_STEP)
    delta = -ADAM_LR * (m_hat / (jnp.sqrt(v_hat) + ADAM_EPS) + ADAM_WD * w)
    return delta, m, v


def _adamw_big(parts, w, m, v):
    _, r, c_ = parts.shape
    rb = 368
    assert r % rb == 0

    def body(p_ref, w_ref, m_ref, v_ref, g_ref, d_ref, mo_ref, vo_ref):
        g = p_ref[0]
        for k in range(1, NDEV):
            g = g + p_ref[k]
        g_ref[...] = g
        d_ref[...], mo_ref[...], vo_ref[...] = _adamw_math(w_ref[...], g, m_ref[...], v_ref[...])

    blk = pl.BlockSpec((rb, c_), lambda i: (i, 0))
    return pl.pallas_call(
        body, name="adamw_big", grid=(r // rb,),
        in_specs=[pl.BlockSpec((NDEV, rb, c_), lambda i: (0, i, 0)), blk, blk, blk],
        out_specs=[blk, blk, blk, blk],
        out_shape=[jax.ShapeDtypeStruct((r, c_), F32)] * 4,
        compiler_params=_params("arbitrary"),
    )(parts, w, m, v)


def _sum_parts(parts, loss_row):
    _, r, c_ = parts.shape

    def body(p_ref, o_ref, l_ref):
        g = p_ref[0]
        for k in range(1, NDEV):
            g = g + p_ref[k]
        o_ref[...] = g
        per_row = jnp.sum(o_ref[loss_row:loss_row + 8, :], axis=1, keepdims=True)
        l_ref[...] = jnp.broadcast_to(jnp.sum(per_row, axis=0, keepdims=True), (8, c_))

    return pl.pallas_call(body, name="sum_small", out_shape=[jax.ShapeDtypeStruct((r, c_), F32),
                                                             jax.ShapeDtypeStruct((8, c_), F32)])(parts)


def _adamw_small(g, w, m, v):
    def body(g_ref, w_ref, m_ref, v_ref, d_ref, mo_ref, vo_ref):
        d_ref[...], mo_ref[...], vo_ref[...] = _adamw_math(w_ref[...], g_ref[...], m_ref[...], v_ref[...])

    return pl.pallas_call(body, name="adamw_small", out_shape=[jax.ShapeDtypeStruct(g.shape, F32)] * 3)(g, w, m, v)


def _pack_big(w1in, w2in, w1out, w2out, dtype):
    parts = [w1in.reshape(ROWS_W1IN, PACK_W), w2in.reshape(ROWS_W2IN, PACK_W), w1out.reshape(ROWS_WOUT, PACK_W),
             w2out.reshape(ROWS_WOUT, PACK_W)]
    used = ROWS_W1IN + ROWS_W2IN + 2 * ROWS_WOUT
    parts.append(jnp.zeros((PACK_ROWS - used, PACK_W), parts[0].dtype))
    return jnp.concatenate(parts, axis=0).astype(dtype)


def _unpack_big(p):
    a, b, c = ROWS_W1IN, ROWS_W1IN + ROWS_W2IN, ROWS_W1IN + ROWS_W2IN + ROWS_WOUT
    return (p[:a].reshape(1, D, 512), p[a:b].reshape(1, D, 514), p[b:c].reshape(1, 128, D),
            p[c:c + ROWS_WOUT].reshape(1, 128, D))


def _pad_lanes(a):
    return jnp.pad(a, ((0, 0), (0, LANES - a.shape[1])))


def _heads(a, s):
    return a.reshape(s, H, DH).transpose(1, 0, 2)


def _unheads(a, s):
    return a.transpose(1, 0, 2).reshape(s, D)


def _aug(cols, s):
    used = sum(c.shape[-1] for c in cols)
    return jnp.concatenate(cols + [jnp.zeros((H, s, HA - used), BF16)], axis=-1)


def _terms(v):
    return [t[..., None] for t in _split3(v)]


def _tiles(a, t):
    return a.reshape(H, a.shape[1] // t, t, HA)


def _tiles_t(a, t):
    return _tiles(a, t).transpose(0, 1, 3, 2)


def kernel(x, conv_norm_g, conv_w_in, conv_w, conv_w_out, attn_norm_g, attn_w_in, attn_b_f, attn_q_norm_g, attn_k_norm_g, attn_w_out, loss_target, m_conv_norm_g, m_conv_w_in, m_conv_w, m_conv_w_out, m_attn_norm_g, m_attn_w_in, m_attn_b_f, m_attn_q_norm_g, m_attn_k_norm_g, m_attn_w_out, v_conv_norm_g, v_conv_w_in, v_conv_w, v_conv_w_out, v_attn_norm_g, v_attn_w_in, v_attn_b_f, v_attn_q_norm_g, v_attn_k_norm_g, v_attn_w_out):
    s = x.shape[1]
    tq = min(TQ, s)
    tmf, tmb = min(TM_FWD, s), min(TM_BWD, s)
    me = 4 * lax.axis_index("x") + 2 * lax.axis_index("y") + lax.axis_index("c")
    xv, tgt = x[0], loss_target[0]

    wg = _all_gather(_pack_big(conv_w_in[0], attn_w_in[0], conv_w_out[0], attn_w_out[0], BF16), "gather_weights")
    small_w = jnp.concatenate([conv_w[0], attn_norm_g, jnp.zeros((4, 128), F32)], axis=0)
    sg_ = _all_gather(small_w, "gather_small_weights")
    a, b, c = ROWS_W1IN, ROWS_W1IN + ROWS_W2IN, ROWS_W1IN + ROWS_W2IN + ROWS_WOUT
    w1in = wg[:, :a].transpose(1, 0, 2).reshape(D, 4 * D)
    w2all = wg[:, a:b].reshape(NDEV, D, 514).transpose(1, 0, 2).reshape(D, 4 * D + H)
    w2in, wf = w2all[:, :4 * D], _pad_lanes(w2all[:, 4 * D:])
    w1out = wg[:, b:c].reshape(D, D)
    w2out = wg[:, c:c + ROWS_WOUT].reshape(D, D)
    cw = jnp.concatenate([sg_[:, 0:3, :].transpose(1, 0, 2).reshape(3, D), jnp.zeros((5, D), F32)], axis=0)
    g2 = sg_[:, 3, :].reshape(1, D)
    qg_t, kg_t = jnp.tile(attn_q_norm_g, (1, H)), jnp.tile(attn_k_norm_g, (1, H))
    bf = _pad_lanes(attn_b_f)

    e = (jnp.arange(D)[:, None] // DH == jnp.arange(LANES)[None, :]).astype(BF16)
    fold = (jnp.arange(D)[:, None] % DH == jnp.arange(LANES)[None, :]).astype(BF16)
    tril = (jnp.arange(tmb)[:, None] >= jnp.arange(tmb)[None, :]).astype(BF16)
    triu = tril.T
    src = jnp.arange(3 * LANES)
    dst = jnp.arange(H * HA)[None, :] - (HA * (src % LANES) + src // LANES)[:, None]
    place_k = ((dst == COL_BIAS) & (src % LANES < H)[:, None]).astype(BF16)
    place_q = ((dst == ROW_BIAS) & (src % LANES < H)[:, None]).astype(BF16)

    x1, h1, p1, yg = _conv_fwd(xv, conv_norm_g, w1in, cw, w1out)
    reach = 1.01 * math.sqrt(DH) * jnp.max(jnp.abs(attn_q_norm_g)) * jnp.max(jnp.abs(attn_k_norm_g))
    bounded = (2.0 * reach <= BOUNDED_SOFTMAX_REACH).astype(F32).reshape(1, 1)
    h2, qpre, kpre, z, qa, ka, va, invq, invk, cc, fl = _attn_proj_fwd(
        x1, g2, w2in, wf, bf, qg_t, kg_t, jnp.broadcast_to(reach, (1, LANES)), e, e.T, tril, place_k, place_q)
    ka, va = _tiles(ka, tq), _tiles(va, tq)
    ol, qab = _flash_fwd(bounded, _tiles(qa, tq), ka, va)
    og, dx2, dz, doa, lossp = _attn_out(ol.reshape(H, s, HA), z, x1, tgt, w2out, w2out.T, e, place_k)

    dq, dkt, dvt = _flash_bwd(qab, _tiles(doa, tq), ka, va)
    dq = dq.reshape(H, s, HA)
    dcs = _pad_lanes((dq[:, :, ROW_BIAS] - dkt[:, COL_BIAS, :]).T)
    dx1, dp2, df, dg2, dqg, dkg, dbf = _attn_proj_bwd(
        dq, dkt, dvt, dz, dcs, qpre, kpre, invq, invk, fl, x1, dx2, g2, qg_t, kg_t, w2in.T, wf.T, e, e.T, triu, fold)
    gx, dp1, dcw, dg1 = _conv_bwd(dx1, p1, xv, conv_norm_g, cw, w1out.T, w1in.T)
    dw2out = _wgrad(og, dx2, "wgrad_attn_out")
    dw2in = _wgrad(h2, dp2, "wgrad_attn_in")
    dwf = _wgrad(h2, df, "wgrad_attn_forget")
    dw1out = _wgrad(yg, dx1, "wgrad_conv_out")
    dw1in = _wgrad(h1, dp1, "wgrad_conv_in")

    dw2all = jnp.concatenate([dw2in, dwf[:, :H]], axis=1)
    gfull = jnp.concatenate([
        dw1in.reshape(D, NDEV, 512).transpose(1, 0, 2),
        dw2all.reshape(D, NDEV, 514).transpose(1, 0, 2).reshape(NDEV, ROWS_W2IN, PACK_W),
        dw1out.reshape(NDEV, ROWS_WOUT, PACK_W), dw2out.reshape(NDEV, ROWS_WOUT, PACK_W),
        jnp.zeros((NDEV, PACK_ROWS - c - ROWS_WOUT, PACK_W), F32)], axis=1)
    parts = _exchange_blocks(gfull, "scatter_grads")
    pack = lambda t1, t2, t3, t4: _pack_big(t1[0], t2[0], t3[0], t4[0], F32)
    gb, db, mb, vb = _adamw_big(parts, pack(conv_w_in, attn_w_in, conv_w_out, attn_w_out),
                                pack(m_conv_w_in, m_attn_w_in, m_conv_w_out, m_attn_w_out),
                                pack(v_conv_w_in, v_attn_w_in, v_conv_w_out, v_attn_w_out))

    shard_rows = jnp.concatenate([dcw[0:3].reshape(3, NDEV, 128).transpose(1, 0, 2), dg2.reshape(NDEV, 1, 128),
                                  jnp.zeros((NDEV, 4, 128), F32)], axis=1).reshape(64, 128)
    small_g = jnp.concatenate([shard_rows, dg1.reshape(8, 128), dbf, dqg[0:1], dkg[0:1], jnp.zeros((5, 128), F32),
                               lossp.reshape(8, 128)], axis=0)
    gs, loss8 = _sum_parts(_all_gather(small_g, "gather_small_grads"), 80)
    loss = loss8[0, 0]
    mine = lax.dynamic_slice(gs, (8 * me, 0), (8, 128))
    g_small = jnp.concatenate([mine, gs[64:75], jnp.zeros((5, 128), F32)], axis=0)

    def pack_small(cwk, ang, cng, bfk, qgk, kgk):
        return jnp.concatenate([cwk[0], ang, jnp.zeros((4, 128), F32), cng.reshape(8, 128), _pad_lanes(bfk),
                                _pad_lanes(qgk), _pad_lanes(kgk), jnp.zeros((5, 128), F32)], axis=0)

    ds_, ms_, vs_ = _adamw_small(
        g_small, pack_small(conv_w, attn_norm_g, conv_norm_g, attn_b_f, attn_q_norm_g, attn_k_norm_g),
        pack_small(m_conv_w, m_attn_norm_g, m_conv_norm_g, m_attn_b_f, m_attn_q_norm_g, m_attn_k_norm_g),
        pack_small(v_conv_w, v_attn_norm_g, v_conv_norm_g, v_attn_b_f, v_attn_q_norm_g, v_attn_k_norm_g))

    def leaves(big, small):
        w1i, w2i, w1o, w2o = _unpack_big(big)
        return (small[8:16].reshape(1, D), w1i, small[0:3].reshape(1, 3, 128), w1o, small[3:4], w2i,
                small[16:17, :H], small[17:18, :DH], small[18:19, :DH], w2o)

    return (loss, gx[None], *leaves(gb, g_small), *leaves(db, ds_), *leaves(mb, ms_), *leaves(vb, vs_))
```

```python
import functools
import math

import jax
import jax.numpy as jnp
from jax import lax
from jax.experimental import pallas as pl
from jax.experimental.pallas import tpu as pltpu

F32 = jnp.float32
BF16 = jnp.bfloat16

D = 1024
H = 16
DH = 64
NDEV = 8
RMS_EPS = 1e-6
LANES = 128
HA = 128
TM_FWD = 512
TM_BWD = 256
TQ = 512
TK = 512
CH = 256
PACK_W = 512
PACK_ROWS = 2576
ROWS_W1IN, ROWS_W2IN, ROWS_WOUT = 1024, 1028, 256
ADAM_LR, ADAM_B1, ADAM_B2, ADAM_EPS, ADAM_WD, ADAM_STEP = 0.001, 0.9, 0.999, 1e-08, 0.01, 10
VMEM_LIMIT = 56 * 1024 * 1024
MASKED = -1e30
BOUNDED_SOFTMAX_REACH = 60.0
MESH = pl.DeviceIdType.MESH


def _params(*sem, vmem=VMEM_LIMIT):
    return pltpu.CompilerParams(dimension_semantics=sem or None, vmem_limit_bytes=vmem)


def _const(shape):
    nd = len(shape)
    return pl.BlockSpec(shape, lambda *_: (0,) * nd, pipeline_mode=pl.Buffered(1))


def _rows(tm, n, rev=False, steps=None):
    if rev:
        return pl.BlockSpec((tm, n), lambda i: (steps - 1 - i, 0))
    return pl.BlockSpec((tm, n), lambda i: (i, 0))


def _dot(a, b):
    return jnp.dot(a, b, preferred_element_type=F32)


def _top16(x):
    bits = lax.bitcast_convert_type(x, jnp.uint32) & jnp.uint32(0xFFFF0000)
    return lax.bitcast_convert_type(bits, F32)


def _split2(x):
    hi = _top16(x)
    return hi.astype(BF16), (x - hi).astype(BF16)


def _split3(x):
    hi = _top16(x)
    r = x - hi
    mid = _top16(r)
    return hi.astype(BF16), mid.astype(BF16), (r - mid).astype(BF16)


def _seg_sum(a, e):
    hi, lo = _split2(a)
    return _dot(hi, e) + _dot(lo, e)


def _seg_bcast(s, et):
    hi, mid, lo = _split3(s)
    return _dot(hi, et) + _dot(mid, et) + _dot(lo, et)


def _tri_sum(t, v):
    hi, mid, lo = _split3(v)
    return _dot(t, hi) + _dot(t, mid) + _dot(t, lo)


def _sigmoid(z):
    return 1.0 / (1.0 + jnp.exp(-z))


def _place():
    return lax.axis_index("x"), lax.axis_index("y"), lax.axis_index("c")


def _all_gather(xb, name):
    r, c_ = xb.shape

    def body(x_ref, out_ref, send_sems, recv_sems, local_sem):
        x, y, c = _place()
        me, sibling = (x, y, c), (x, y, 1 - c)
        chips = [(1 - x, y), (x, 1 - y), (1 - x, 1 - y)]

        def slab(px, py, pc):
            return out_ref.at[4 * px + 2 * py + pc]

        def copy(k, block, to, src=None):
            return pltpu.make_async_remote_copy(
                src_ref=slab(*block) if src is None else src, dst_ref=slab(*block),
                send_sem=send_sems.at[k], recv_sem=recv_sems.at[k], device_id=to, device_id_type=MESH)

        mine = pltpu.make_async_copy(x_ref, slab(*me), local_sem)
        mine.start()
        first = [copy(0, me, sibling, src=x_ref)]
        first += [copy(1 + j, me, (*chip, c), src=x_ref) for j, chip in enumerate(chips)]
        for cp in first:
            cp.start()
        passed = [copy(4 + j, (*chip, c), sibling) for j, chip in enumerate(chips)]
        for j, chip in enumerate(chips):
            copy(1 + j, (*chip, c), me).wait_recv()
            passed[j].start()
        copy(0, sibling, me).wait_recv()
        for j, chip in enumerate(chips):
            copy(4 + j, (*chip, 1 - c), me).wait_recv()
        for cp in first + passed:
            cp.wait_send()
        mine.wait()

    return pl.pallas_call(
        body, name=name,
        out_shape=jax.ShapeDtypeStruct((NDEV, r, c_), xb.dtype),
        in_specs=[pl.BlockSpec(memory_space=pl.ANY)],
        out_specs=pl.BlockSpec(memory_space=pl.ANY),
        scratch_shapes=[pltpu.SemaphoreType.DMA((7,)), pltpu.SemaphoreType.DMA((7,)), pltpu.SemaphoreType.DMA],
    )(xb)


def _exchange_blocks(g, name):
    _, r, c_ = g.shape

    def body(g_ref, out_ref, send_sems, recv_sems, local_sem):
        x, y, c = _place()
        me = 4 * x + 2 * y + c
        mine = pltpu.make_async_copy(g_ref.at[me], out_ref.at[0], local_sem)
        mine.start()
        copies = []
        for k in range(1, NDEV):
            px = 1 - x if k & 4 else x
            py = 1 - y if k & 2 else y
            pc = 1 - c if k & 1 else c
            cp = pltpu.make_async_remote_copy(
                src_ref=g_ref.at[4 * px + 2 * py + pc], dst_ref=out_ref.at[k],
                send_sem=send_sems.at[k - 1], recv_sem=recv_sems.at[k - 1],
                device_id=(px, py, pc), device_id_type=MESH)
            cp.start()
            copies.append(cp)
        for cp in copies:
            cp.wait()
        mine.wait()

    return pl.pallas_call(
        body, name=name,
        out_shape=jax.ShapeDtypeStruct((NDEV, r, c_), g.dtype),
        in_specs=[pl.BlockSpec(memory_space=pl.ANY)],
        out_specs=pl.BlockSpec(memory_space=pl.ANY),
        scratch_shapes=[pltpu.SemaphoreType.DMA((7,)), pltpu.SemaphoreType.DMA((7,)), pltpu.SemaphoreType.DMA],
    )(g)


def _conv_fwd(x, g1, w1in, cw, w1out):
    s = x.shape[0]
    tm = min(TM_FWD, s)
    steps = s // tm

    def body(x_ref, g_ref, win_ref, cw_ref, wout_ref, x1_ref, h1_ref, p1_ref, yg_ref, tail_ref):
        @pl.when(pl.program_id(0) == 0)
        def _():
            tail_ref[...] = jnp.zeros_like(tail_ref)

        xv = x_ref[...]
        inv = lax.rsqrt(jnp.mean(xv * xv, axis=-1, keepdims=True) + RMS_EPS)
        h = (xv * inv * g_ref[...]).astype(BF16)
        h1_ref[...] = h
        row = lax.broadcasted_iota(jnp.int32, (tm, CH), 0)
        for ci in range(D // CH):
            lo, hi = ci * CH, (ci + 1) * CH
            parts = []
            for k in range(4):
                pk = _dot(h, win_ref[:, k * D + lo:k * D + hi]).astype(BF16)
                p1_ref[:, k * D + lo:k * D + hi] = pk
                parts.append(pk.astype(F32))
            b, c, xin, z = parts
            u = c * xin
            t6 = tail_ref[6:7, lo:hi]
            t7 = tail_ref[7:8, lo:hi]
            u1 = jnp.where(row == 0, t7, pltpu.roll(u, 1, 0))
            u2 = jnp.where(row == 0, t6, jnp.where(row == 1, t7, pltpu.roll(u, 2, 0)))
            tail_ref[:, lo:hi] = u[tm - 8:, :]
            y = cw_ref[2:3, lo:hi] * u + cw_ref[1:2, lo:hi] * u1 + cw_ref[0:1, lo:hi] * u2
            yg_ref[:, lo:hi] = (b * y * (z * _sigmoid(z))).astype(BF16)
        x1_ref[...] = xv + _dot(yg_ref[...], wout_ref[...])

    return pl.pallas_call(
        body, name="conv_fwd", grid=(steps,),
        in_specs=[_rows(tm, D), _const((1, D)), _const((D, 4 * D)), _const((8, D)), _const((D, D))],
        out_specs=[_rows(tm, D), _rows(tm, D), _rows(tm, 4 * D), _rows(tm, D)],
        out_shape=[jax.ShapeDtypeStruct((s, D), F32), jax.ShapeDtypeStruct((s, D), BF16),
                   jax.ShapeDtypeStruct((s, 4 * D), BF16), jax.ShapeDtypeStruct((s, D), BF16)],
        scratch_shapes=[pltpu.VMEM((8, D), F32)],
        compiler_params=_params("arbitrary"),
    )(x, g1, w1in, cw, w1out)


COL_BIAS = DH
ROW_BIAS = DH + 3


def _terms_cat(val):
    return jnp.concatenate(_split3(val), axis=1)


def _split_heads(x, aug, out_ref, lane):
    for hp in range(H // 2):
        pair = x[:, hp * HA:(hp + 1) * HA]
        out_ref[2 * hp] = jnp.where(lane < DH, pair, aug(2 * hp)).astype(BF16)
        out_ref[2 * hp + 1] = jnp.where(lane < DH, pltpu.roll(pair, DH, 1), aug(2 * hp + 1)).astype(BF16)


def _ones_at(lane, first):
    return jnp.where((lane >= first) & (lane < first + 3), 1.0, 0.0)


def _attn_proj_fwd(x1, g2, w2in, wf, bf, qg, kg, shift, e, et, tril, place_k, place_q):
    s = x1.shape[0]
    tm = min(TM_BWD, s)
    steps = s // tm

    def body(x_ref, g_ref, win_ref, wf_ref, bf_ref, qg_ref, kg_ref, sh_ref, e_ref, et_ref, tril_ref, pk_ref, pq_ref,
             h2_ref, qp_ref, kp_ref, z_ref, qa_ref, ka_ref, va_ref, iq_ref, ik_ref, c_ref, fl_ref, carry_ref):
        @pl.when(pl.program_id(0) == 0)
        def _():
            carry_ref[...] = jnp.zeros_like(carry_ref)

        xv = x_ref[...]
        inv = lax.rsqrt(jnp.mean(xv * xv, axis=-1, keepdims=True) + RMS_EPS)
        h = (xv * inv * g_ref[...]).astype(BF16)
        h2_ref[...] = h

        fl = _dot(h, wf_ref[...]) + bf_ref[...]
        fl_ref[...] = fl
        ex = jnp.exp(-jnp.abs(fl))
        up = 1.0 + ex
        log1p = jnp.where(up == 1.0, ex, jnp.log(up) * ex / (up - 1.0))
        lane = lax.broadcasted_iota(jnp.int32, (tm, LANES), 1)
        logf = jnp.where(lane < H, jnp.minimum(fl, 0.0) - log1p, 0.0)
        c_ref[...] = _tri_sum(tril_ref[...], logf) + carry_ref[0:1, :]
        carry_ref[0:1, :] = c_ref[tm - 1:tm, :]
        c = c_ref[...]

        def normed(col, pre_ref, inv_ref, gain_ref, scale):
            pre = _dot(h, win_ref[:, col * D:(col + 1) * D]).astype(BF16)
            pre_ref[...] = pre
            t = pre.astype(F32)
            invh = lax.rsqrt(_seg_sum(t * t, e_ref[...]) * (1.0 / DH) + RMS_EPS)
            inv_ref[...] = invh
            return t * _seg_bcast(invh, et_ref[...]) * (gain_ref[...] * scale)

        ones_col, ones_row = _ones_at(lane, COL_BIAS), _ones_at(lane, ROW_BIAS)
        q_bias = _dot(_terms_cat(c - sh_ref[...]), pq_ref[...])
        _split_heads(normed(0, qp_ref, iq_ref, qg_ref, 1.0 / math.sqrt(DH)),
                     lambda hh: q_bias[:, hh * HA:(hh + 1) * HA] + ones_col, qa_ref, lane)
        k_bias = _dot(_terms_cat(-c), pk_ref[...])
        _split_heads(normed(1, kp_ref, ik_ref, kg_ref, 1.0),
                     lambda hh: k_bias[:, hh * HA:(hh + 1) * HA] + ones_row, ka_ref, lane)
        _split_heads(_dot(h, win_ref[:, 2 * D:3 * D]), lambda hh: ones_col, va_ref, lane)
        z_ref[...] = _dot(h, win_ref[:, 3 * D:4 * D]).astype(BF16)

    row_bf = lambda: _rows(tm, D)
    row_sm = lambda: _rows(tm, LANES)
    heads = lambda: pl.BlockSpec((H, tm, HA), lambda i: (0, i, 0))
    return pl.pallas_call(
        body, name="attn_proj_fwd", grid=(steps,),
        in_specs=[_rows(tm, D), _const((1, D)), _const((D, 4 * D)), _const((D, LANES)), _const((1, LANES)),
                  _const((1, D)), _const((1, D)), _const((1, LANES)), _const((D, LANES)), _const((LANES, D)),
                  _const((tm, tm)), _const((3 * LANES, H * HA)), _const((3 * LANES, H * HA))],
        out_specs=[row_bf() for _ in range(4)] + [heads() for _ in range(3)] + [row_sm() for _ in range(4)],
        out_shape=[jax.ShapeDtypeStruct((s, D), BF16)] * 4 + [jax.ShapeDtypeStruct((H, s, HA), BF16)] * 3 + [
            jax.ShapeDtypeStruct((s, LANES), F32)] * 4,
        scratch_shapes=[pltpu.VMEM((8, LANES), F32)],
        compiler_params=_params("arbitrary"),
    )(x1, g2, w2in, wf, bf, qg, kg, shift, e, et, tril, place_k, place_q)


def _dot_nt(a, b):
    return lax.dot_general(a, b, (((1,), (1,)), ((), ())), preferred_element_type=F32)


def _dot_tn(a, b):
    return lax.dot_general(a, b, (((0,), (0,)), ((), ())), preferred_element_type=F32)


def _flash_fwd(bounded, qa, ka, va):
    h_, nq, tq, _ = qa.shape
    nk, tk = ka.shape[1], ka.shape[2]

    def body(flag_ref, q_ref, k_ref, v_ref, o_ref, qb_ref, m_ref, acc_ref):
        i = pl.program_id(1)
        lane = lax.broadcasted_iota(jnp.int32, (tq, HA), 1)
        in_bias = (lane >= ROW_BIAS) & (lane < ROW_BIAS + 3)
        acc_ref[...] = jnp.zeros_like(acc_ref)

        def run(use_bound):
            q = q_ref[...]
            if use_bound:
                m_ref[...] = -jnp.sum(jnp.where(in_bias, q.astype(F32), 0.0), axis=1, keepdims=True)
            else:
                q = jnp.where(in_bias, jnp.zeros_like(q), q)
                m_ref[...] = jnp.full_like(m_ref, MASKED)

            def step(j, tiles, masked):
                keys = pl.ds(pl.multiple_of(j * tk, tk), tiles * tk)
                z = _dot_nt(q, k_ref[keys, :])
                if masked:
                    rows = lax.broadcasted_iota(jnp.int32, (tq, tk), 0)
                    cols = lax.broadcasted_iota(jnp.int32, (tq, tk), 1)
                    z = jnp.where(rows >= cols, z, MASKED)
                if use_bound:
                    acc_ref[...] += _dot(jnp.exp(z).astype(BF16), v_ref[keys, :])
                else:
                    m_old = m_ref[...]
                    m_new = jnp.maximum(m_old, jnp.max(z, axis=1, keepdims=True))
                    p = jnp.exp(z - m_new)
                    acc_ref[...] = jnp.exp(m_old - m_new) * acc_ref[...] + _dot(p.astype(BF16), v_ref[keys, :])
                    m_ref[...] = m_new

            def two_tiles(jj, carry):
                step(2 * jj, 2, False)
                return carry

            lax.fori_loop(0, lax.shift_right_logical(i, 1), two_tiles, 0)
            pl.when((i & 1) == 1)(lambda: step(i - 1, 1, False))
            step(i, 1, True)

        use_bound = flag_ref[0, 0] > 0.5
        pl.when(use_bound)(lambda: run(True))
        pl.when(jnp.logical_not(use_bound))(lambda: run(False))

        acc = acc_ref[...]
        l = jnp.sum(jnp.where(lane == COL_BIAS, acc, 0.0), axis=1, keepdims=True)
        lse = m_ref[...] + jnp.log(l)
        o_ref[...] = jnp.where(lane < DH, acc / l, lse)
        hi, mid, lo = _split3(-lse)
        qb_ref[...] = jnp.where(lane == ROW_BIAS, hi, jnp.where(lane == ROW_BIAS + 1, mid,
                                                                jnp.where(lane == ROW_BIAS + 2, lo, q_ref[...])))

    tile = lambda: pl.BlockSpec((None, None, tq, HA), lambda h, i: (h, i, 0, 0))
    head = lambda: pl.BlockSpec((None, nk * tk, HA), lambda h, i: (h, 0, 0))
    return pl.pallas_call(
        body, name="flash_fwd", grid=(h_, nq),
        in_specs=[pl.BlockSpec(memory_space=pltpu.SMEM), tile(), head(), head()],
        out_specs=[tile(), tile()],
        out_shape=[jax.ShapeDtypeStruct((h_, nq, tq, HA), F32), jax.ShapeDtypeStruct((h_, nq, tq, HA), BF16)],
        scratch_shapes=[pltpu.VMEM((tq, 1), F32), pltpu.VMEM((tq, HA), F32)],
        compiler_params=_params("arbitrary", "arbitrary"),
    )(bounded, qa, ka.reshape(h_, nk * tk, HA), va.reshape(h_, nk * tk, HA))


def _merge_heads(src, out_ref, lane):
    for hp in range(H // 2):
        out_ref[:, hp * HA:(hp + 1) * HA] = jnp.where(lane < DH, src(2 * hp), pltpu.roll(src(2 * hp + 1), DH, 1))


def _attn_out(ol, z, x1, tgt, w2out, w2out_t, e, place_k):
    s = z.shape[0]
    tm = min(TM_BWD, s)
    steps = s // tm

    def body(ol_ref, z_ref, x1_ref, t_ref, w_ref, wt_ref, e_ref, pk_ref, og_ref, dx2_ref, dz_ref, doa_ref, loss_ref,
             o_ref):
        @pl.when(pl.program_id(0) == 0)
        def _():
            loss_ref[...] = jnp.zeros_like(loss_ref)

        lane = lax.broadcasted_iota(jnp.int32, (tm, HA), 1)
        _merge_heads(lambda hh: ol_ref[hh], o_ref, lane)
        ov = o_ref[...]
        zv = z_ref[...].astype(F32)
        sg = _sigmoid(zv)
        sil = zv * sg
        og = (ov * sil).astype(BF16)
        og_ref[...] = og
        err = x1_ref[...] + _dot(og, w_ref[...]) - t_ref[...]
        loss_ref[...] += (0.5 / D) * jnp.sum(err * err, axis=0, keepdims=True)
        dx2 = err * (1.0 / D)
        dx2_ref[...] = dx2
        dog = _dot(dx2.astype(BF16), wt_ref[...])
        do = (dog * sil).astype(BF16).astype(F32)
        dz_ref[...] = (dog * ov * (sg * (1.0 + zv * (1.0 - sg)))).astype(BF16)
        delta = _seg_sum(do * ov, e_ref[...])
        d_bias = _dot(_terms_cat(-delta), pk_ref[...])
        _split_heads(do, lambda hh: d_bias[:, hh * HA:(hh + 1) * HA], doa_ref, lane)

    heads = lambda: pl.BlockSpec((H, tm, HA), lambda i: (0, i, 0))
    return pl.pallas_call(
        body, name="attn_out", grid=(steps,),
        in_specs=[heads(), _rows(tm, D), _rows(tm, D), _rows(tm, D), _const((D, D)), _const((D, D)),
                  _const((D, LANES)), _const((3 * LANES, H * HA))],
        out_specs=[_rows(tm, D), _rows(tm, D), _rows(tm, D), heads(), pl.BlockSpec((1, D), lambda i: (0, 0))],
        out_shape=[jax.ShapeDtypeStruct((s, D), BF16), jax.ShapeDtypeStruct((s, D), F32),
                   jax.ShapeDtypeStruct((s, D), BF16), jax.ShapeDtypeStruct((H, s, HA), BF16),
                   jax.ShapeDtypeStruct((1, D), F32)],
        scratch_shapes=[pltpu.VMEM((tm, D), F32)],
        compiler_params=_params("arbitrary"),
    )(ol, z, x1, tgt, w2out, w2out_t, e, place_k)


def _flash_bwd(qab, doa, ka, va):
    h_, nq, tq, _ = qab.shape
    nk, tk = ka.shape[1], ka.shape[2]
    s = nk * tk

    def body(qa_ref, da_ref, ka_ref, va_ref, dq_hbm, dkt_ref, dvt_ref, dq_acc, sem):
        hh = pl.program_id(0)
        j = pl.program_id(1)

        @pl.when(j == 0)
        def _():
            dq_acc[...] = jnp.zeros_like(dq_acc)

        dkt_ref[...] = jnp.zeros_like(dkt_ref)
        dvt_ref[...] = jnp.zeros_like(dvt_ref)
        ka_v = ka_ref[...]
        va_v = va_ref[...]

        def step(i, masked):
            qa_i, da_i = qa_ref[i], da_ref[i]
            zz = _dot_nt(qa_i, ka_v)
            if masked:
                rows = lax.broadcasted_iota(jnp.int32, (tq, tk), 0)
                cols = lax.broadcasted_iota(jnp.int32, (tq, tk), 1)
                zz = jnp.where(rows >= cols, zz, MASKED)
            p = jnp.exp(zz)
            ds = (p * _dot_nt(da_i, va_v)).astype(BF16)
            pb = p.astype(BF16)
            dq_acc[i] += _dot(ds, ka_v)
            dvt_ref[...] += _dot_tn(da_i, pb)
            dkt_ref[...] += _dot_tn(qa_i, ds)

        step(j, True)
        below = nq - 1 - j

        def two_tiles(t, carry):
            step(j + 1 + 2 * t, False)
            step(j + 2 + 2 * t, False)
            return carry

        lax.fori_loop(0, lax.shift_right_logical(below, 1), two_tiles, 0)
        pl.when((below & 1) == 1)(lambda: step(nq - 1, False))

        @pl.when(j == nk - 1)
        def _():
            cp = pltpu.make_async_copy(dq_acc, dq_hbm.at[hh], sem)
            cp.start()
            cp.wait()

    whole = lambda: pl.BlockSpec((None, nq, tq, HA), lambda h, j: (h, 0, 0, 0))
    tile = lambda: pl.BlockSpec((None, None, tk, HA), lambda h, j: (h, j, 0, 0))
    return pl.pallas_call(
        body, name="flash_bwd", grid=(h_, nk),
        in_specs=[whole(), whole(), tile(), tile()],
        out_specs=[pl.BlockSpec(memory_space=pl.ANY),
                   pl.BlockSpec((None, HA, tk), lambda h, j: (h, 0, j)),
                   pl.BlockSpec((None, HA, tk), lambda h, j: (h, 0, j))],
        out_shape=[jax.ShapeDtypeStruct((h_, nq, tq, HA), F32), jax.ShapeDtypeStruct((h_, HA, s), F32),
                   jax.ShapeDtypeStruct((h_, HA, s), F32)],
        scratch_shapes=[pltpu.VMEM((nq, tq, HA), F32), pltpu.SemaphoreType.DMA],
        compiler_params=_params("arbitrary", "arbitrary"),
    )(qab, doa, ka, va)


def _attn_proj_bwd(dq, dkt, dvt, dz, qpre, kpre, invq, invk, fl, x1, dx2, g2, qg, kg, w2in_t, wf_t, e, et,
                   triu, fold):
    s = x1.shape[0]
    tm = min(TM_BWD, s)
    steps = s // tm

    def body(dq_ref, dkt_ref, dvt_ref, dz_ref, qp_ref, kp_ref, iq_ref, ik_ref, fl_ref, x1_ref, dx2_ref,
             g2_ref, qg_ref, kg_ref, wt_ref, wft_ref, e_ref, et_ref, triu_ref, fold_ref,
             dx1_ref, dp_ref, df_ref, dg2_ref, dqg_ref, dkg_ref, dbf_ref, carry_ref, rc_ref, qcol_ref, kcol_ref,
             dqs_ref, dkn_ref, dv_ref):
        step = pl.program_id(0)
        lane = lax.broadcasted_iota(jnp.int32, (tm, HA), 1)
        _merge_heads(lambda hh: dvt_ref[hh].T, dv_ref, lane)
        dc = jnp.zeros((tm, HA), F32)
        for hp in range(H // 2):
            dqh = [dq_ref[2 * hp], dq_ref[2 * hp + 1]]
            dkh = [dkt_ref[2 * hp].T, dkt_ref[2 * hp + 1].T]
            dqs_ref[:, hp * HA:(hp + 1) * HA] = jnp.where(lane < DH, dqh[0], pltpu.roll(dqh[1], DH, 1))
            dkn_ref[:, hp * HA:(hp + 1) * HA] = jnp.where(lane < DH, dkh[0], pltpu.roll(dkh[1], DH, 1))
            for k in range(2):
                sums = jnp.where(lane == ROW_BIAS, dqh[k], 0.0) - pltpu.roll(
                    jnp.where(lane == COL_BIAS, dkh[k], 0.0), ROW_BIAS - COL_BIAS, 1)
                dc = dc + pltpu.roll(sums, (2 * hp + k - ROW_BIAS) % HA, 1)

        @pl.when(step == 0)
        def _():
            carry_ref[...] = jnp.zeros_like(carry_ref)
            dg2_ref[...] = jnp.zeros_like(dg2_ref)
            dbf_ref[...] = jnp.zeros_like(dbf_ref)
            qcol_ref[...] = jnp.zeros_like(qcol_ref)
            kcol_ref[...] = jnp.zeros_like(kcol_ref)
            dqg_ref[...] = jnp.zeros_like(dqg_ref)
            dkg_ref[...] = jnp.zeros_like(dkg_ref)

        rc_ref[...] = _tri_sum(triu_ref[...], dc) + carry_ref[0:1, :]
        carry_ref[0:1, :] = rc_ref[0:1, :]
        df = rc_ref[...] * _sigmoid(-fl_ref[...])
        dfb = df.astype(BF16)
        df_ref[...] = dfb
        dbf_ref[...] += jnp.sum(df, axis=0, keepdims=True)

        def norm_bwd(dn, pre_ref, inv_ref, gain_ref, col_ref):
            pre = pre_ref[...].astype(F32)
            invh = inv_ref[...]
            invb = _seg_bcast(invh, et_ref[...])
            col_ref[...] += jnp.sum(dn * pre * invb, axis=0, keepdims=True)
            gd = dn * gain_ref[...]
            mean = _seg_sum(gd * pre, e_ref[...]) * (1.0 / DH)
            return invb * gd - pre * _seg_bcast(mean * invh * invh * invh, et_ref[...])

        dq = norm_bwd(dqs_ref[...] * (1.0 / math.sqrt(DH)), qp_ref, iq_ref, qg_ref, qcol_ref).astype(BF16)
        dp_ref[:, 0:D] = dq
        dh = _dot(dq, wt_ref[0:D, :])
        dk = norm_bwd(dkn_ref[...], kp_ref, ik_ref, kg_ref, kcol_ref).astype(BF16)
        dp_ref[:, D:2 * D] = dk
        dh += _dot(dk, wt_ref[D:2 * D, :])
        dvb = dv_ref[...].astype(BF16)
        dp_ref[:, 2 * D:3 * D] = dvb
        dh += _dot(dvb, wt_ref[2 * D:3 * D, :])
        dzb = dz_ref[...]
        dp_ref[:, 3 * D:4 * D] = dzb
        dh += _dot(dzb, wt_ref[3 * D:4 * D, :])
        dh += _dot(dfb, wft_ref[...])

        xv = x1_ref[...]
        inv = lax.rsqrt(jnp.mean(xv * xv, axis=-1, keepdims=True) + RMS_EPS)
        dg2_ref[...] += jnp.sum(dh * xv * inv, axis=0, keepdims=True)
        gh = dh * g2_ref[...]
        dx1_ref[...] = dx2_ref[...] + inv * gh - xv * (inv * inv * inv * jnp.mean(gh * xv, axis=-1, keepdims=True))

        @pl.when(step == steps - 1)
        def _():
            dqg_ref[...] = _fold_heads(qcol_ref[...], fold_ref[...])
            dkg_ref[...] = _fold_heads(kcol_ref[...], fold_ref[...])

    rr = lambda n: _rows(tm, n, rev=True, steps=steps)
    acc = lambda n, r=1: pl.BlockSpec((r, n), lambda i: (0, 0))
    heads_t = lambda: pl.BlockSpec((H, HA, tm), lambda i: (0, 0, steps - 1 - i))
    return pl.pallas_call(
        body, name="attn_proj_bwd", grid=(steps,),
        in_specs=[pl.BlockSpec((H, tm, HA), lambda i: (0, steps - 1 - i, 0)), heads_t(), heads_t(),
                  rr(D), rr(D), rr(D), rr(LANES), rr(LANES), rr(LANES), rr(D), rr(D),
                  _const((1, D)), _const((1, D)), _const((1, D)), _const((4 * D, D)), _const((LANES, D)),
                  _const((D, LANES)), _const((LANES, D)), _const((tm, tm)), _const((D, LANES))],
        out_specs=[rr(D), rr(4 * D), rr(LANES), acc(D), acc(LANES, 8), acc(LANES, 8), acc(LANES)],
        out_shape=[jax.ShapeDtypeStruct((s, D), F32), jax.ShapeDtypeStruct((s, 4 * D), BF16),
                   jax.ShapeDtypeStruct((s, LANES), BF16), jax.ShapeDtypeStruct((1, D), F32),
                   jax.ShapeDtypeStruct((8, LANES), F32), jax.ShapeDtypeStruct((8, LANES), F32),
                   jax.ShapeDtypeStruct((1, LANES), F32)],
        scratch_shapes=[pltpu.VMEM((8, LANES), F32), pltpu.VMEM((tm, LANES), F32), pltpu.VMEM((1, D), F32),
                        pltpu.VMEM((1, D), F32), pltpu.VMEM((tm, D), F32), pltpu.VMEM((tm, D), F32),
                        pltpu.VMEM((tm, D), F32)],
        compiler_params=_params("arbitrary"),
    )(dq, dkt, dvt, dz, qpre, kpre, invq, invk, fl, x1, dx2, g2, qg, kg, w2in_t, wf_t, e, et, triu, fold)


def _fold_heads(col, fold):
    hi, mid, lo = _split3(jnp.broadcast_to(col, (8, D)))
    return _dot(hi, fold) + _dot(mid, fold) + _dot(lo, fold)


def _conv_bwd(dx1, p1, x, g1, cw, w1out_t, w1in_t):
    s = x.shape[0]
    tm = min(TM_BWD, s)
    steps = s // tm
    halo = tm // 8

    def body(dx1_ref, b_ref, c_ref, xi_ref, z_ref, ch_ref, xh_ref, x_ref, g_ref, cw_ref, wot_ref, wit_ref,
             gx_ref, dp_ref, dcw_ref, dg1_ref, head_ref):
        step = pl.program_id(0)

        @pl.when(step == 0)
        def _():
            head_ref[...] = jnp.zeros_like(head_ref)
            dcw_ref[...] = jnp.zeros_like(dcw_ref)
            dg1_ref[...] = jnp.zeros_like(dg1_ref)

        first_tile = step == steps - 1
        dx1 = dx1_ref[...]
        row = lax.broadcasted_iota(jnp.int32, (tm, CH), 0)
        dyb = dx1.astype(BF16)
        for ci in range(D // CH):
            lo, hi = ci * CH, (ci + 1) * CH
            dyg = _dot(dyb, wot_ref[:, lo:hi])
            b = b_ref[:, lo:hi].astype(F32)
            c = c_ref[:, lo:hi].astype(F32)
            xin = xi_ref[:, lo:hi].astype(F32)
            z = z_ref[:, lo:hi].astype(F32)
            u = c * xin
            t6 = jnp.where(first_tile, 0.0, ch_ref[6:7, lo:hi].astype(F32) * xh_ref[6:7, lo:hi].astype(F32))
            t7 = jnp.where(first_tile, 0.0, ch_ref[7:8, lo:hi].astype(F32) * xh_ref[7:8, lo:hi].astype(F32))
            u1 = jnp.where(row == 0, t7, pltpu.roll(u, 1, 0))
            u2 = jnp.where(row == 0, t6, jnp.where(row == 1, t7, pltpu.roll(u, 2, 0)))
            w0, w1, w2 = cw_ref[0:1, lo:hi], cw_ref[1:2, lo:hi], cw_ref[2:3, lo:hi]
            y = w2 * u + w1 * u1 + w0 * u2
            sg = _sigmoid(z)
            sil = z * sg
            dp_ref[:, lo:hi] = (dyg * y * sil).astype(BF16)
            dy = dyg * b * sil
            dp_ref[:, 3 * D + lo:3 * D + hi] = (dyg * b * y * (sg * (1.0 + z * (1.0 - sg)))).astype(BF16)
            dcw_ref[2:3, lo:hi] += jnp.sum(dy * u, axis=0, keepdims=True)
            dcw_ref[1:2, lo:hi] += jnp.sum(dy * u1, axis=0, keepdims=True)
            dcw_ref[0:1, lo:hi] += jnp.sum(dy * u2, axis=0, keepdims=True)
            n0 = head_ref[0:1, lo:hi]
            n1 = head_ref[1:2, lo:hi]
            dyn1 = jnp.where(row == tm - 1, n0, pltpu.roll(dy, tm - 1, 0))
            dyn2 = jnp.where(row == tm - 2, n0, jnp.where(row == tm - 1, n1, pltpu.roll(dy, tm - 2, 0)))
            head_ref[:, lo:hi] = dy[0:8, :]
            du = w2 * dy + w1 * dyn1 + w0 * dyn2
            dp_ref[:, D + lo:D + hi] = (du * xin).astype(BF16)
            dp_ref[:, 2 * D + lo:2 * D + hi] = (du * c).astype(BF16)
        dh = _dot(dp_ref[:, 0:D], wit_ref[0:D, :])
        for k in range(1, 4):
            dh += _dot(dp_ref[:, k * D:(k + 1) * D], wit_ref[k * D:(k + 1) * D, :])
        xv = x_ref[...]
        inv = lax.rsqrt(jnp.mean(xv * xv, axis=-1, keepdims=True) + RMS_EPS)
        dg1_ref[...] += jnp.sum(dh * xv * inv, axis=0, keepdims=True)
        gh = dh * g_ref[...]
        gx_ref[...] = dx1 + inv * gh - xv * (inv * inv * inv * jnp.mean(gh * xv, axis=-1, keepdims=True))

    rr = lambda n: _rows(tm, n, rev=True, steps=steps)
    part = lambda k: pl.BlockSpec((tm, D), lambda i: (steps - 1 - i, k))
    prev8 = lambda k: pl.BlockSpec((8, D), lambda i: (jnp.maximum((steps - 1 - i) * halo - 1, 0), k))
    return pl.pallas_call(
        body, name="conv_bwd", grid=(steps,),
        in_specs=[rr(D), part(0), part(1), part(2), part(3), prev8(1), prev8(2), rr(D), _const((1, D)),
                  _const((8, D)), _const((D, D)), _const((4 * D, D))],
        out_specs=[rr(D), rr(4 * D), pl.BlockSpec((8, D), lambda i: (0, 0)), pl.BlockSpec((1, D), lambda i: (0, 0))],
        out_shape=[jax.ShapeDtypeStruct((s, D), F32), jax.ShapeDtypeStruct((s, 4 * D), BF16),
                   jax.ShapeDtypeStruct((8, D), F32), jax.ShapeDtypeStruct((1, D), F32)],
        scratch_shapes=[pltpu.VMEM((8, D), F32)],
        compiler_params=_params("arbitrary"),
    )(dx1, p1, p1, p1, p1, p1, p1, x, g1, cw, w1out_t, w1in_t)


def _wgrad(a, g, name):
    s, k = a.shape
    n = g.shape[1]
    bn = min(n, 1024)
    ts = min(512, s)

    def body(a_ref, g_ref, out_ref, acc_ref):
        t = pl.program_id(1)

        @pl.when(t == 0)
        def _():
            acc_ref[...] = jnp.zeros_like(acc_ref)

        acc_ref[...] += _dot_tn(a_ref[...], g_ref[...].astype(BF16))

        @pl.when(t == s // ts - 1)
        def _():
            out_ref[...] = acc_ref[...]

    return pl.pallas_call(
        body, name=name, grid=(n // bn, s // ts),
        in_specs=[pl.BlockSpec((ts, k), lambda j, t: (t, 0)), pl.BlockSpec((ts, bn), lambda j, t: (t, j))],
        out_specs=pl.BlockSpec((k, bn), lambda j, t: (0, j)),
        out_shape=jax.ShapeDtypeStruct((k, n), F32),
        scratch_shapes=[pltpu.VMEM((k, bn), F32)],
        compiler_params=_params("arbitrary", "arbitrary"),
    )(a, g)


def _adamw_math(w, g, m, v):
    m = ADAM_B1 * m + (1.0 - ADAM_B1) * g
    v = ADAM_B2 * v + (1.0 - ADAM_B2) * (g * g)
    m_hat = m / (1.0 - ADAM_B1 ** ADAM_STEP)
    v_hat = v / (1.0 - ADAM_B2 ** ADAM_STEP)
    delta = -ADAM_LR * (m_hat / (jnp.sqrt(v_hat) + ADAM_EPS) + ADAM_WD * w)
    return delta, m, v


def _adamw_big(parts, w, m, v):
    _, r, c_ = parts.shape
    rb = 368
    assert r % rb == 0

    def body(p_ref, w_ref, m_ref, v_ref, g_ref, d_ref, mo_ref, vo_ref):
        g = p_ref[0]
        for k in range(1, NDEV):
            g = g + p_ref[k]
        g_ref[...] = g
        d_ref[...], mo_ref[...], vo_ref[...] = _adamw_math(w_ref[...], g, m_ref[...], v_ref[...])

    blk = pl.BlockSpec((rb, c_), lambda i: (i, 0))
    return pl.pallas_call(
        body, name="adamw_big", grid=(r // rb,),
        in_specs=[pl.BlockSpec((NDEV, rb, c_), lambda i: (0, i, 0)), blk, blk, blk],
        out_specs=[blk, blk, blk, blk],
        out_shape=[jax.ShapeDtypeStruct((r, c_), F32)] * 4,
        compiler_params=_params("arbitrary"),
    )(parts, w, m, v)


def _sum_parts(parts, loss_row):
    _, r, c_ = parts.shape

    def body(p_ref, o_ref, l_ref):
        g = p_ref[0]
        for k in range(1, NDEV):
            g = g + p_ref[k]
        o_ref[...] = g
        per_row = jnp.sum(o_ref[loss_row:loss_row + 8, :], axis=1, keepdims=True)
        l_ref[...] = jnp.broadcast_to(jnp.sum(per_row, axis=0, keepdims=True), (8, c_))

    return pl.pallas_call(body, name="sum_small", out_shape=[jax.ShapeDtypeStruct((r, c_), F32),
                                                             jax.ShapeDtypeStruct((8, c_), F32)])(parts)


def _adamw_small(g, w, m, v):
    def body(g_ref, w_ref, m_ref, v_ref, d_ref, mo_ref, vo_ref):
        d_ref[...], mo_ref[...], vo_ref[...] = _adamw_math(w_ref[...], g_ref[...], m_ref[...], v_ref[...])

    return pl.pallas_call(body, name="adamw_small", out_shape=[jax.ShapeDtypeStruct(g.shape, F32)] * 3)(g, w, m, v)


def _pack_big(w1in, w2in, w1out, w2out, dtype):
    parts = [w1in.reshape(ROWS_W1IN, PACK_W), w2in.reshape(ROWS_W2IN, PACK_W), w1out.reshape(ROWS_WOUT, PACK_W),
             w2out.reshape(ROWS_WOUT, PACK_W)]
    used = ROWS_W1IN + ROWS_W2IN + 2 * ROWS_WOUT
    parts.append(jnp.zeros((PACK_ROWS - used, PACK_W), parts[0].dtype))
    return jnp.concatenate(parts, axis=0).astype(dtype)


def _unpack_big(p):
    a, b, c = ROWS_W1IN, ROWS_W1IN + ROWS_W2IN, ROWS_W1IN + ROWS_W2IN + ROWS_WOUT
    return (p[:a].reshape(1, D, 512), p[a:b].reshape(1, D, 514), p[b:c].reshape(1, 128, D),
            p[c:c + ROWS_WOUT].reshape(1, 128, D))


def _pad_lanes(a):
    return jnp.pad(a, ((0, 0), (0, LANES - a.shape[1])))


def _heads(a, s):
    return a.reshape(s, H, DH).transpose(1, 0, 2)


def _unheads(a, s):
    return a.transpose(1, 0, 2).reshape(s, D)


def _aug(cols, s):
    used = sum(c.shape[-1] for c in cols)
    return jnp.concatenate(cols + [jnp.zeros((H, s, HA - used), BF16)], axis=-1)


def _terms(v):
    return [t[..., None] for t in _split3(v)]


def _tiles(a, t):
    return a.reshape(H, a.shape[1] // t, t, HA)


def _tiles_t(a, t):
    return _tiles(a, t).transpose(0, 1, 3, 2)


def kernel(x, conv_norm_g, conv_w_in, conv_w, conv_w_out, attn_norm_g, attn_w_in, attn_b_f, attn_q_norm_g, attn_k_norm_g, attn_w_out, loss_target, m_conv_norm_g, m_conv_w_in, m_conv_w, m_conv_w_out, m_attn_norm_g, m_attn_w_in, m_attn_b_f, m_attn_q_norm_g, m_attn_k_norm_g, m_attn_w_out, v_conv_norm_g, v_conv_w_in, v_conv_w, v_conv_w_out, v_attn_norm_g, v_attn_w_in, v_attn_b_f, v_attn_q_norm_g, v_attn_k_norm_g, v_attn_w_out):
    s = x.shape[1]
    tq = min(TQ, s)
    tmf, tmb = min(TM_FWD, s), min(TM_BWD, s)
    me = 4 * lax.axis_index("x") + 2 * lax.axis_index("y") + lax.axis_index("c")
    xv, tgt = x[0], loss_target[0]

    wg = _all_gather(_pack_big(conv_w_in[0], attn_w_in[0], conv_w_out[0], attn_w_out[0], BF16), "gather_weights")
    small_w = jnp.concatenate([conv_w[0], attn_norm_g, jnp.zeros((4, 128), F32)], axis=0)
    sg_ = _all_gather(small_w, "gather_small_weights")
    a, b, c = ROWS_W1IN, ROWS_W1IN + ROWS_W2IN, ROWS_W1IN + ROWS_W2IN + ROWS_WOUT
    w1in = wg[:, :a].transpose(1, 0, 2).reshape(D, 4 * D)
    w2all = wg[:, a:b].reshape(NDEV, D, 514).transpose(1, 0, 2).reshape(D, 4 * D + H)
    w2in, wf = w2all[:, :4 * D], _pad_lanes(w2all[:, 4 * D:])
    w1out = wg[:, b:c].reshape(D, D)
    w2out = wg[:, c:c + ROWS_WOUT].reshape(D, D)
    cw = jnp.concatenate([sg_[:, 0:3, :].transpose(1, 0, 2).reshape(3, D), jnp.zeros((5, D), F32)], axis=0)
    g2 = sg_[:, 3, :].reshape(1, D)
    qg_t, kg_t = jnp.tile(attn_q_norm_g, (1, H)), jnp.tile(attn_k_norm_g, (1, H))
    bf = _pad_lanes(attn_b_f)

    e = (jnp.arange(D)[:, None] // DH == jnp.arange(LANES)[None, :]).astype(BF16)
    fold = (jnp.arange(D)[:, None] % DH == jnp.arange(LANES)[None, :]).astype(BF16)
    tril = (jnp.arange(tmb)[:, None] >= jnp.arange(tmb)[None, :]).astype(BF16)
    triu = tril.T
    src = jnp.arange(3 * LANES)
    dst = jnp.arange(H * HA)[None, :] - (HA * (src % LANES) + src // LANES)[:, None]
    place_k = ((dst == COL_BIAS) & (src % LANES < H)[:, None]).astype(BF16)
    place_q = ((dst == ROW_BIAS) & (src % LANES < H)[:, None]).astype(BF16)

    x1, h1, p1, yg = _conv_fwd(xv, conv_norm_g, w1in, cw, w1out)
    reach = 1.01 * math.sqrt(DH) * jnp.max(jnp.abs(attn_q_norm_g)) * jnp.max(jnp.abs(attn_k_norm_g))
    bounded = (2.0 * reach <= BOUNDED_SOFTMAX_REACH).astype(F32).reshape(1, 1)
    h2, qpre, kpre, z, qa, ka, va, invq, invk, cc, fl = _attn_proj_fwd(
        x1, g2, w2in, wf, bf, qg_t, kg_t, jnp.broadcast_to(reach, (1, LANES)), e, e.T, tril, place_k, place_q)
    ka, va = _tiles(ka, tq), _tiles(va, tq)
    ol, qab = _flash_fwd(bounded, _tiles(qa, tq), ka, va)
    og, dx2, dz, doa, lossp = _attn_out(ol.reshape(H, s, HA), z, x1, tgt, w2out, w2out.T, e, place_k)

    dq, dkt, dvt = _flash_bwd(qab, _tiles(doa, tq), ka, va)
    dx1, dp2, df, dg2, dqg, dkg, dbf = _attn_proj_bwd(
        dq.reshape(H, s, HA), dkt, dvt, dz, qpre, kpre, invq, invk, fl, x1, dx2, g2, qg_t, kg_t, w2in.T, wf.T, e, e.T, triu, fold)
    gx, dp1, dcw, dg1 = _conv_bwd(dx1, p1, xv, conv_norm_g, cw, w1out.T, w1in.T)
    dw2out = _wgrad(og, dx2, "wgrad_attn_out")
    dw2in = _wgrad(h2, dp2, "wgrad_attn_in")
    dwf = _wgrad(h2, df, "wgrad_attn_forget")
    dw1out = _wgrad(yg, dx1, "wgrad_conv_out")
    dw1in = _wgrad(h1, dp1, "wgrad_conv_in")

    dw2all = jnp.concatenate([dw2in, dwf[:, :H]], axis=1)
    gfull = jnp.concatenate([
        dw1in.reshape(D, NDEV, 512).transpose(1, 0, 2),
        dw2all.reshape(D, NDEV, 514).transpose(1, 0, 2).reshape(NDEV, ROWS_W2IN, PACK_W),
        dw1out.reshape(NDEV, ROWS_WOUT, PACK_W), dw2out.reshape(NDEV, ROWS_WOUT, PACK_W),
        jnp.zeros((NDEV, PACK_ROWS - c - ROWS_WOUT, PACK_W), F32)], axis=1)
    parts = _exchange_blocks(gfull, "scatter_grads")
    pack = lambda t1, t2, t3, t4: _pack_big(t1[0], t2[0], t3[0], t4[0], F32)
    gb, db, mb, vb = _adamw_big(parts, pack(conv_w_in, attn_w_in, conv_w_out, attn_w_out),
                                pack(m_conv_w_in, m_attn_w_in, m_conv_w_out, m_attn_w_out),
                                pack(v_conv_w_in, v_attn_w_in, v_conv_w_out, v_attn_w_out))

    shard_rows = jnp.concatenate([dcw[0:3].reshape(3, NDEV, 128).transpose(1, 0, 2), dg2.reshape(NDEV, 1, 128),
                                  jnp.zeros((NDEV, 4, 128), F32)], axis=1).reshape(64, 128)
    small_g = jnp.concatenate([shard_rows, dg1.reshape(8, 128), dbf, dqg[0:1], dkg[0:1], jnp.zeros((5, 128), F32),
                               lossp.reshape(8, 128)], axis=0)
    gs, loss8 = _sum_parts(_all_gather(small_g, "gather_small_grads"), 80)
    loss = loss8[0, 0]
    mine = lax.dynamic_slice(gs, (8 * me, 0), (8, 128))
    g_small = jnp.concatenate([mine, gs[64:75], jnp.zeros((5, 128), F32)], axis=0)

    def pack_small(cwk, ang, cng, bfk, qgk, kgk):
        return jnp.concatenate([cwk[0], ang, jnp.zeros((4, 128), F32), cng.reshape(8, 128), _pad_lanes(bfk),
                                _pad_lanes(qgk), _pad_lanes(kgk), jnp.zeros((5, 128), F32)], axis=0)

    ds_, ms_, vs_ = _adamw_small(
        g_small, pack_small(conv_w, attn_norm_g, conv_norm_g, attn_b_f, attn_q_norm_g, attn_k_norm_g),
        pack_small(m_conv_w, m_attn_norm_g, m_conv_norm_g, m_attn_b_f, m_attn_q_norm_g, m_attn_k_norm_g),
        pack_small(v_conv_w, v_attn_norm_g, v_conv_norm_g, v_attn_b_f, v_attn_q_norm_g, v_attn_k_norm_g))

    def leaves(big, small):
        w1i, w2i, w1o, w2o = _unpack_big(big)
        return (small[8:16].reshape(1, D), w1i, small[0:3].reshape(1, 3, 128), w1o, small[3:4], w2i,
                small[16:17, :H], small[17:18, :DH], small[18:19, :DH], w2o)

    return (loss, gx[None], *leaves(gb, g_small), *leaves(db, ds_), *leaves(mb, ms_), *leaves(vb, vs_))
```

```python
import functools
import math

import jax
import jax.numpy as jnp
from jax import lax
from jax.experimental import pallas as pl
from jax.experimental.pallas import tpu as pltpu

F32 = jnp.float32
BF16 = jnp.bfloat16

D = 1024
H = 16
DH = 64
NDEV = 8
RMS_EPS = 1e-6
LANES = 128
HA = 128
TM_FWD = 512
TM_BWD = 256
TQ = 512
TK = 512
BWD_TILES = 4
CH = 256
PACK_W = 512
PACK_ROWS = 2576
ROWS_W1IN, ROWS_W2IN, ROWS_WOUT = 1024, 1028, 256
ADAM_LR, ADAM_B1, ADAM_B2, ADAM_EPS, ADAM_WD, ADAM_STEP = 0.001, 0.9, 0.999, 1e-08, 0.01, 10
VMEM_LIMIT = 56 * 1024 * 1024
MASKED = -1e30
BOUNDED_SOFTMAX_REACH = 60.0
MESH = pl.DeviceIdType.MESH


def _params(*sem, vmem=VMEM_LIMIT):
    return pltpu.CompilerParams(dimension_semantics=sem or None, vmem_limit_bytes=vmem)


def _const(shape):
    nd = len(shape)
    return pl.BlockSpec(shape, lambda *_: (0,) * nd, pipeline_mode=pl.Buffered(1))


def _rows(tm, n, rev=False, steps=None):
    if rev:
        return pl.BlockSpec((tm, n), lambda i: (steps - 1 - i, 0))
    return pl.BlockSpec((tm, n), lambda i: (i, 0))


def _dot(a, b):
    return jnp.dot(a, b, preferred_element_type=F32)


def _top16(x):
    bits = lax.bitcast_convert_type(x, jnp.uint32) & jnp.uint32(0xFFFF0000)
    return lax.bitcast_convert_type(bits, F32)


def _split2(x):
    hi = _top16(x)
    return hi.astype(BF16), (x - hi).astype(BF16)


def _split3(x):
    hi = _top16(x)
    r = x - hi
    mid = _top16(r)
    return hi.astype(BF16), mid.astype(BF16), (r - mid).astype(BF16)


def _seg_sum(a, e):
    hi, lo = _split2(a)
    return _dot(hi, e) + _dot(lo, e)


def _seg_bcast(s, et):
    hi, mid, lo = _split3(s)
    return _dot(hi, et) + _dot(mid, et) + _dot(lo, et)


def _tri_sum(t, v):
    hi, mid, lo = _split3(v)
    return _dot(t, hi) + _dot(t, mid) + _dot(t, lo)


def _sigmoid(z):
    return 1.0 / (1.0 + jnp.exp(-z))


def _place():
    return lax.axis_index("x"), lax.axis_index("y"), lax.axis_index("c")


def _all_gather(xb, name):
    r, c_ = xb.shape

    def body(x_ref, out_ref, send_sems, recv_sems, local_sem):
        x, y, c = _place()
        me, sibling = (x, y, c), (x, y, 1 - c)
        chips = [(1 - x, y), (x, 1 - y), (1 - x, 1 - y)]

        def slab(px, py, pc):
            return out_ref.at[4 * px + 2 * py + pc]

        def copy(k, block, to, src=None):
            return pltpu.make_async_remote_copy(
                src_ref=slab(*block) if src is None else src, dst_ref=slab(*block),
                send_sem=send_sems.at[k], recv_sem=recv_sems.at[k], device_id=to, device_id_type=MESH)

        mine = pltpu.make_async_copy(x_ref, slab(*me), local_sem)
        mine.start()
        first = [copy(0, me, sibling, src=x_ref)]
        first += [copy(1 + j, me, (*chip, c), src=x_ref) for j, chip in enumerate(chips)]
        for cp in first:
            cp.start()
        passed = [copy(4 + j, (*chip, c), sibling) for j, chip in enumerate(chips)]
        for j, chip in enumerate(chips):
            copy(1 + j, (*chip, c), me).wait_recv()
            passed[j].start()
        copy(0, sibling, me).wait_recv()
        for j, chip in enumerate(chips):
            copy(4 + j, (*chip, 1 - c), me).wait_recv()
        for cp in first + passed:
            cp.wait_send()
        mine.wait()

    return pl.pallas_call(
        body, name=name,
        out_shape=jax.ShapeDtypeStruct((NDEV, r, c_), xb.dtype),
        in_specs=[pl.BlockSpec(memory_space=pl.ANY)],
        out_specs=pl.BlockSpec(memory_space=pl.ANY),
        scratch_shapes=[pltpu.SemaphoreType.DMA((7,)), pltpu.SemaphoreType.DMA((7,)), pltpu.SemaphoreType.DMA],
    )(xb)


def _exchange_blocks(g, name):
    _, r, c_ = g.shape

    def body(g_ref, out_ref, send_sems, recv_sems, local_sem):
        x, y, c = _place()
        me = 4 * x + 2 * y + c
        mine = pltpu.make_async_copy(g_ref.at[me], out_ref.at[0], local_sem)
        mine.start()
        copies = []
        for k in range(1, NDEV):
            px = 1 - x if k & 4 else x
            py = 1 - y if k & 2 else y
            pc = 1 - c if k & 1 else c
            cp = pltpu.make_async_remote_copy(
                src_ref=g_ref.at[4 * px + 2 * py + pc], dst_ref=out_ref.at[k],
                send_sem=send_sems.at[k - 1], recv_sem=recv_sems.at[k - 1],
                device_id=(px, py, pc), device_id_type=MESH)
            cp.start()
            copies.append(cp)
        for cp in copies:
            cp.wait()
        mine.wait()

    return pl.pallas_call(
        body, name=name,
        out_shape=jax.ShapeDtypeStruct((NDEV, r, c_), g.dtype),
        in_specs=[pl.BlockSpec(memory_space=pl.ANY)],
        out_specs=pl.BlockSpec(memory_space=pl.ANY),
        scratch_shapes=[pltpu.SemaphoreType.DMA((7,)), pltpu.SemaphoreType.DMA((7,)), pltpu.SemaphoreType.DMA],
    )(g)


def _conv_fwd(x, g1, w1in, cw, w1out):
    s = x.shape[0]
    tm = min(TM_FWD, s)
    steps = s // tm

    def body(x_ref, g_ref, win_ref, cw_ref, wout_ref, x1_ref, h1_ref, p1_ref, yg_ref, tail_ref):
        @pl.when(pl.program_id(0) == 0)
        def _():
            tail_ref[...] = jnp.zeros_like(tail_ref)

        xv = x_ref[...]
        inv = lax.rsqrt(jnp.mean(xv * xv, axis=-1, keepdims=True) + RMS_EPS)
        h = (xv * inv * g_ref[...]).astype(BF16)
        h1_ref[...] = h
        row = lax.broadcasted_iota(jnp.int32, (tm, CH), 0)
        for ci in range(D // CH):
            lo, hi = ci * CH, (ci + 1) * CH
            parts = []
            for k in range(4):
                pk = _dot(h, win_ref[:, k * D + lo:k * D + hi]).astype(BF16)
                p1_ref[:, k * D + lo:k * D + hi] = pk
                parts.append(pk.astype(F32))
            b, c, xin, z = parts
            u = c * xin
            t6 = tail_ref[6:7, lo:hi]
            t7 = tail_ref[7:8, lo:hi]
            u1 = jnp.where(row == 0, t7, pltpu.roll(u, 1, 0))
            u2 = jnp.where(row == 0, t6, jnp.where(row == 1, t7, pltpu.roll(u, 2, 0)))
            tail_ref[:, lo:hi] = u[tm - 8:, :]
            y = cw_ref[2:3, lo:hi] * u + cw_ref[1:2, lo:hi] * u1 + cw_ref[0:1, lo:hi] * u2
            yg_ref[:, lo:hi] = (b * y * (z * _sigmoid(z))).astype(BF16)
        x1_ref[...] = xv + _dot(yg_ref[...], wout_ref[...])

    return pl.pallas_call(
        body, name="conv_fwd", grid=(steps,),
        in_specs=[_rows(tm, D), _const((1, D)), _const((D, 4 * D)), _const((8, D)), _const((D, D))],
        out_specs=[_rows(tm, D), _rows(tm, D), _rows(tm, 4 * D), _rows(tm, D)],
        out_shape=[jax.ShapeDtypeStruct((s, D), F32), jax.ShapeDtypeStruct((s, D), BF16),
                   jax.ShapeDtypeStruct((s, 4 * D), BF16), jax.ShapeDtypeStruct((s, D), BF16)],
        scratch_shapes=[pltpu.VMEM((8, D), F32)],
        compiler_params=_params("arbitrary"),
    )(x, g1, w1in, cw, w1out)


COL_BIAS = DH
ROW_BIAS = DH + 3


def _terms_cat(val):
    return jnp.concatenate(_split3(val), axis=1)


def _split_heads(x, aug, out_ref, lane):
    for hp in range(H // 2):
        pair = x[:, hp * HA:(hp + 1) * HA]
        out_ref[2 * hp] = jnp.where(lane < DH, pair, aug(2 * hp)).astype(BF16)
        out_ref[2 * hp + 1] = jnp.where(lane < DH, pltpu.roll(pair, DH, 1), aug(2 * hp + 1)).astype(BF16)


def _ones_at(lane, first):
    return jnp.where((lane >= first) & (lane < first + 3), 1.0, 0.0)


def _attn_proj_fwd(x1, g2, w2in, wf, bf, qg, kg, shift, e, et, tril, place_k, place_q):
    s = x1.shape[0]
    tm = min(TM_BWD, s)
    steps = s // tm

    def body(x_ref, g_ref, win_ref, wf_ref, bf_ref, qg_ref, kg_ref, sh_ref, e_ref, et_ref, tril_ref, pk_ref, pq_ref,
             h2_ref, qp_ref, kp_ref, z_ref, qa_ref, ka_ref, va_ref, iq_ref, ik_ref, c_ref, fl_ref, carry_ref):
        @pl.when(pl.program_id(0) == 0)
        def _():
            carry_ref[...] = jnp.zeros_like(carry_ref)

        xv = x_ref[...]
        inv = lax.rsqrt(jnp.mean(xv * xv, axis=-1, keepdims=True) + RMS_EPS)
        h = (xv * inv * g_ref[...]).astype(BF16)
        h2_ref[...] = h

        fl = _dot(h, wf_ref[...]) + bf_ref[...]
        fl_ref[...] = fl
        ex = jnp.exp(-jnp.abs(fl))
        up = 1.0 + ex
        log1p = jnp.where(up == 1.0, ex, jnp.log(up) * ex / (up - 1.0))
        lane = lax.broadcasted_iota(jnp.int32, (tm, LANES), 1)
        logf = jnp.where(lane < H, jnp.minimum(fl, 0.0) - log1p, 0.0)
        c_ref[...] = _tri_sum(tril_ref[...], logf) + carry_ref[0:1, :]
        carry_ref[0:1, :] = c_ref[tm - 1:tm, :]
        c = c_ref[...]

        def normed(col, pre_ref, inv_ref, gain_ref, scale):
            pre = _dot(h, win_ref[:, col * D:(col + 1) * D]).astype(BF16)
            pre_ref[...] = pre
            t = pre.astype(F32)
            invh = lax.rsqrt(_seg_sum(t * t, e_ref[...]) * (1.0 / DH) + RMS_EPS)
            inv_ref[...] = invh
            return t * _seg_bcast(invh, et_ref[...]) * (gain_ref[...] * scale)

        ones_col, ones_row = _ones_at(lane, COL_BIAS), _ones_at(lane, ROW_BIAS)
        q_bias = _dot(_terms_cat(c - sh_ref[...]), pq_ref[...])
        _split_heads(normed(0, qp_ref, iq_ref, qg_ref, 1.0 / math.sqrt(DH)),
                     lambda hh: q_bias[:, hh * HA:(hh + 1) * HA] + ones_col, qa_ref, lane)
        k_bias = _dot(_terms_cat(-c), pk_ref[...])
        _split_heads(normed(1, kp_ref, ik_ref, kg_ref, 1.0),
                     lambda hh: k_bias[:, hh * HA:(hh + 1) * HA] + ones_row, ka_ref, lane)
        _split_heads(_dot(h, win_ref[:, 2 * D:3 * D]), lambda hh: ones_col, va_ref, lane)
        z_ref[...] = _dot(h, win_ref[:, 3 * D:4 * D]).astype(BF16)

    row_bf = lambda: _rows(tm, D)
    row_sm = lambda: _rows(tm, LANES)
    heads = lambda: pl.BlockSpec((H, tm, HA), lambda i: (0, i, 0))
    return pl.pallas_call(
        body, name="attn_proj_fwd", grid=(steps,),
        in_specs=[_rows(tm, D), _const((1, D)), _const((D, 4 * D)), _const((D, LANES)), _const((1, LANES)),
                  _const((1, D)), _const((1, D)), _const((1, LANES)), _const((D, LANES)), _const((LANES, D)),
                  _const((tm, tm)), _const((3 * LANES, H * HA)), _const((3 * LANES, H * HA))],
        out_specs=[row_bf() for _ in range(4)] + [heads() for _ in range(3)] + [row_sm() for _ in range(4)],
        out_shape=[jax.ShapeDtypeStruct((s, D), BF16)] * 4 + [jax.ShapeDtypeStruct((H, s, HA), BF16)] * 3 + [
            jax.ShapeDtypeStruct((s, LANES), F32)] * 4,
        scratch_shapes=[pltpu.VMEM((8, LANES), F32)],
        compiler_params=_params("arbitrary"),
    )(x1, g2, w2in, wf, bf, qg, kg, shift, e, et, tril, place_k, place_q)


def _dot_nt(a, b):
    return lax.dot_general(a, b, (((1,), (1,)), ((), ())), preferred_element_type=F32)


def _dot_tn(a, b):
    return lax.dot_general(a, b, (((0,), (0,)), ((), ())), preferred_element_type=F32)


def _flash_fwd(bounded, qa, ka, va):
    h_, nq, tq, _ = qa.shape
    nk, tk = ka.shape[1], ka.shape[2]

    def body(flag_ref, q_ref, k_ref, v_ref, o_ref, qb_ref, m_ref, acc_ref):
        i = pl.program_id(1)
        lane = lax.broadcasted_iota(jnp.int32, (tq, HA), 1)
        in_bias = (lane >= ROW_BIAS) & (lane < ROW_BIAS + 3)
        acc_ref[...] = jnp.zeros_like(acc_ref)

        def run(use_bound):
            q = q_ref[...]
            if use_bound:
                m_ref[...] = -jnp.sum(jnp.where(in_bias, q.astype(F32), 0.0), axis=1, keepdims=True)
            else:
                q = jnp.where(in_bias, jnp.zeros_like(q), q)
                m_ref[...] = jnp.full_like(m_ref, MASKED)

            def step(j, tiles, masked):
                keys = pl.ds(pl.multiple_of(j * tk, tk), tiles * tk)
                z = _dot_nt(q, k_ref[keys, :])
                if masked:
                    rows = lax.broadcasted_iota(jnp.int32, (tq, tk), 0)
                    cols = lax.broadcasted_iota(jnp.int32, (tq, tk), 1)
                    z = jnp.where(rows >= cols, z, MASKED)
                if use_bound:
                    acc_ref[...] += _dot(jnp.exp(z).astype(BF16), v_ref[keys, :])
                else:
                    m_old = m_ref[...]
                    m_new = jnp.maximum(m_old, jnp.max(z, axis=1, keepdims=True))
                    p = jnp.exp(z - m_new)
                    acc_ref[...] = jnp.exp(m_old - m_new) * acc_ref[...] + _dot(p.astype(BF16), v_ref[keys, :])
                    m_ref[...] = m_new

            def four_tiles(jj, carry):
                step(4 * jj, 4, False)
                return carry

            lax.fori_loop(0, lax.shift_right_logical(i, 2), four_tiles, 0)
            pl.when((i & 2) == 2)(lambda: step(i - (i & 3), 2, False))
            pl.when((i & 1) == 1)(lambda: step(i - 1, 1, False))
            step(i, 1, True)

        use_bound = flag_ref[0, 0] > 0.5
        pl.when(use_bound)(lambda: run(True))
        pl.when(jnp.logical_not(use_bound))(lambda: run(False))

        acc = acc_ref[...]
        l = jnp.sum(jnp.where(lane == COL_BIAS, acc, 0.0), axis=1, keepdims=True)
        lse = m_ref[...] + jnp.log(l)
        o_ref[...] = jnp.where(lane < DH, acc / l, lse)
        hi, mid, lo = _split3(-lse)
        qb_ref[...] = jnp.where(lane == ROW_BIAS, hi, jnp.where(lane == ROW_BIAS + 1, mid,
                                                                jnp.where(lane == ROW_BIAS + 2, lo, q_ref[...])))

    tile = lambda: pl.BlockSpec((None, None, tq, HA), lambda h, i: (h, i, 0, 0))
    head = lambda: pl.BlockSpec((None, nk * tk, HA), lambda h, i: (h, 0, 0))
    return pl.pallas_call(
        body, name="flash_fwd", grid=(h_, nq),
        in_specs=[pl.BlockSpec(memory_space=pltpu.SMEM), tile(), head(), head()],
        out_specs=[tile(), tile()],
        out_shape=[jax.ShapeDtypeStruct((h_, nq, tq, HA), F32), jax.ShapeDtypeStruct((h_, nq, tq, HA), BF16)],
        scratch_shapes=[pltpu.VMEM((tq, 1), F32), pltpu.VMEM((tq, HA), F32)],
        compiler_params=_params("arbitrary", "arbitrary"),
    )(bounded, qa, ka.reshape(h_, nk * tk, HA), va.reshape(h_, nk * tk, HA))


def _merge_heads(src, out_ref, lane):
    for hp in range(H // 2):
        out_ref[:, hp * HA:(hp + 1) * HA] = jnp.where(lane < DH, src(2 * hp), pltpu.roll(src(2 * hp + 1), DH, 1))


def _attn_out(ol, z, x1, tgt, w2out, w2out_t, e, place_k):
    s = z.shape[0]
    tm = min(TM_BWD, s)
    steps = s // tm

    def body(ol_ref, z_ref, x1_ref, t_ref, w_ref, wt_ref, e_ref, pk_ref, og_ref, dx2_ref, dz_ref, doa_ref, loss_ref,
             o_ref):
        @pl.when(pl.program_id(0) == 0)
        def _():
            loss_ref[...] = jnp.zeros_like(loss_ref)

        lane = lax.broadcasted_iota(jnp.int32, (tm, HA), 1)
        _merge_heads(lambda hh: ol_ref[hh], o_ref, lane)
        ov = o_ref[...]
        zv = z_ref[...].astype(F32)
        sg = _sigmoid(zv)
        sil = zv * sg
        og = (ov * sil).astype(BF16)
        og_ref[...] = og
        err = x1_ref[...] + _dot(og, w_ref[...]) - t_ref[...]
        loss_ref[...] += (0.5 / D) * jnp.sum(err * err, axis=0, keepdims=True)
        dx2 = err * (1.0 / D)
        dx2_ref[...] = dx2
        dog = _dot(dx2.astype(BF16), wt_ref[...])
        do = (dog * sil).astype(BF16).astype(F32)
        dz_ref[...] = (dog * ov * (sg * (1.0 + zv * (1.0 - sg)))).astype(BF16)
        delta = _seg_sum(do * ov, e_ref[...])
        d_bias = _dot(_terms_cat(-delta), pk_ref[...])
        _split_heads(do, lambda hh: d_bias[:, hh * HA:(hh + 1) * HA], doa_ref, lane)

    heads = lambda: pl.BlockSpec((H, tm, HA), lambda i: (0, i, 0))
    return pl.pallas_call(
        body, name="attn_out", grid=(steps,),
        in_specs=[heads(), _rows(tm, D), _rows(tm, D), _rows(tm, D), _const((D, D)), _const((D, D)),
                  _const((D, LANES)), _const((3 * LANES, H * HA))],
        out_specs=[_rows(tm, D), _rows(tm, D), _rows(tm, D), heads(), pl.BlockSpec((1, D), lambda i: (0, 0))],
        out_shape=[jax.ShapeDtypeStruct((s, D), BF16), jax.ShapeDtypeStruct((s, D), F32),
                   jax.ShapeDtypeStruct((s, D), BF16), jax.ShapeDtypeStruct((H, s, HA), BF16),
                   jax.ShapeDtypeStruct((1, D), F32)],
        scratch_shapes=[pltpu.VMEM((tm, D), F32)],
        compiler_params=_params("arbitrary"),
    )(ol, z, x1, tgt, w2out, w2out_t, e, place_k)


def _flash_bwd(qab, doa, ka, va):
    h_, nq, tq, _ = qab.shape
    nk, tk = ka.shape[1], ka.shape[2]
    s = nk * tk

    def body(qa_ref, da_ref, ka_ref, va_ref, dq_hbm, dkt_ref, dvt_ref, dq_acc, sem):
        hh = pl.program_id(0)
        j = pl.program_id(1)

        @pl.when(j == 0)
        def _():
            dq_acc[...] = jnp.zeros_like(dq_acc)

        dkt_ref[...] = jnp.zeros_like(dkt_ref)
        dvt_ref[...] = jnp.zeros_like(dvt_ref)
        ka_v = ka_ref[...]
        va_v = va_ref[...]

        def step(i, tiles, masked):
            rows = pl.ds(pl.multiple_of(i * tq, tq), tiles * tq)
            qa_i, da_i = qa_ref[rows, :], da_ref[rows, :]
            zz = _dot_nt(qa_i, ka_v)
            if masked:
                row = lax.broadcasted_iota(jnp.int32, (tq, tk), 0)
                col = lax.broadcasted_iota(jnp.int32, (tq, tk), 1)
                zz = jnp.where(row >= col, zz, MASKED)
            p = jnp.exp(zz)
            ds = (p * _dot_nt(da_i, va_v)).astype(BF16)
            pb = p.astype(BF16)
            dq_acc[rows, :] += _dot(ds, ka_v)
            dvt_ref[...] += _dot_tn(da_i, pb)
            dkt_ref[...] += _dot_tn(qa_i, ds)

        step(j, 1, True)
        below = nq - 1 - j

        def wide(t, carry):
            step(j + 1 + BWD_TILES * t, BWD_TILES, False)
            return carry

        trips = below // BWD_TILES
        lax.fori_loop(0, trips, wide, 0)
        for k in range(BWD_TILES - 1):
            pl.when(trips * BWD_TILES + k < below)(
                functools.partial(step, j + 1 + trips * BWD_TILES + k, 1, False))

        @pl.when(j == nk - 1)
        def _():
            cp = pltpu.make_async_copy(dq_acc, dq_hbm.at[hh], sem)
            cp.start()
            cp.wait()

    whole = lambda: pl.BlockSpec((None, nq * tq, HA), lambda h, j: (h, 0, 0))
    tile = lambda: pl.BlockSpec((None, None, tk, HA), lambda h, j: (h, j, 0, 0))
    return pl.pallas_call(
        body, name="flash_bwd", grid=(h_, nk),
        in_specs=[whole(), whole(), tile(), tile()],
        out_specs=[pl.BlockSpec(memory_space=pl.ANY),
                   pl.BlockSpec((None, HA, tk), lambda h, j: (h, 0, j)),
                   pl.BlockSpec((None, HA, tk), lambda h, j: (h, 0, j))],
        out_shape=[jax.ShapeDtypeStruct((h_, s, HA), F32), jax.ShapeDtypeStruct((h_, HA, s), F32),
                   jax.ShapeDtypeStruct((h_, HA, s), F32)],
        scratch_shapes=[pltpu.VMEM((s, HA), F32), pltpu.SemaphoreType.DMA],
        compiler_params=_params("arbitrary", "arbitrary"),
    )(qab.reshape(h_, s, HA), doa.reshape(h_, s, HA), ka, va)


def _attn_proj_bwd(dq, dkt, dvt, dz, qpre, kpre, invq, invk, fl, x1, dx2, g2, qg, kg, w2in_t, wf_t, e, et,
                   triu, fold):
    s = x1.shape[0]
    tm = min(TM_BWD, s)
    steps = s // tm

    def body(dq_ref, dkt_ref, dvt_ref, dz_ref, qp_ref, kp_ref, iq_ref, ik_ref, fl_ref, x1_ref, dx2_ref,
             g2_ref, qg_ref, kg_ref, wt_ref, wft_ref, e_ref, et_ref, triu_ref, fold_ref,
             dx1_ref, dp_ref, df_ref, dg2_ref, dqg_ref, dkg_ref, dbf_ref, carry_ref, rc_ref, qcol_ref, kcol_ref,
             dqs_ref, dkn_ref, dv_ref):
        step = pl.program_id(0)
        lane = lax.broadcasted_iota(jnp.int32, (tm, HA), 1)
        _merge_heads(lambda hh: dvt_ref[hh].T, dv_ref, lane)
        dc = jnp.zeros((tm, HA), F32)
        for hp in range(H // 2):
            dqh = [dq_ref[2 * hp], dq_ref[2 * hp + 1]]
            dkh = [dkt_ref[2 * hp].T, dkt_ref[2 * hp + 1].T]
            dqs_ref[:, hp * HA:(hp + 1) * HA] = jnp.where(lane < DH, dqh[0], pltpu.roll(dqh[1], DH, 1))
            dkn_ref[:, hp * HA:(hp + 1) * HA] = jnp.where(lane < DH, dkh[0], pltpu.roll(dkh[1], DH, 1))
            for k in range(2):
                sums = jnp.where(lane == ROW_BIAS, dqh[k], 0.0) - pltpu.roll(
                    jnp.where(lane == COL_BIAS, dkh[k], 0.0), ROW_BIAS - COL_BIAS, 1)
                dc = dc + pltpu.roll(sums, (2 * hp + k - ROW_BIAS) % HA, 1)

        @pl.when(step == 0)
        def _():
            carry_ref[...] = jnp.zeros_like(carry_ref)
            dg2_ref[...] = jnp.zeros_like(dg2_ref)
            dbf_ref[...] = jnp.zeros_like(dbf_ref)
            qcol_ref[...] = jnp.zeros_like(qcol_ref)
            kcol_ref[...] = jnp.zeros_like(kcol_ref)
            dqg_ref[...] = jnp.zeros_like(dqg_ref)
            dkg_ref[...] = jnp.zeros_like(dkg_ref)

        rc_ref[...] = _tri_sum(triu_ref[...], dc) + carry_ref[0:1, :]
        carry_ref[0:1, :] = rc_ref[0:1, :]
        df = rc_ref[...] * _sigmoid(-fl_ref[...])
        dfb = df.astype(BF16)
        df_ref[...] = dfb
        dbf_ref[...] += jnp.sum(df, axis=0, keepdims=True)

        def norm_bwd(dn, pre_ref, inv_ref, gain_ref, col_ref):
            pre = pre_ref[...].astype(F32)
            invh = inv_ref[...]
            invb = _seg_bcast(invh, et_ref[...])
            col_ref[...] += jnp.sum(dn * pre * invb, axis=0, keepdims=True)
            gd = dn * gain_ref[...]
            mean = _seg_sum(gd * pre, e_ref[...]) * (1.0 / DH)
            return invb * gd - pre * _seg_bcast(mean * invh * invh * invh, et_ref[...])

        dq = norm_bwd(dqs_ref[...] * (1.0 / math.sqrt(DH)), qp_ref, iq_ref, qg_ref, qcol_ref).astype(BF16)
        dp_ref[:, 0:D] = dq
        dh = _dot(dq, wt_ref[0:D, :])
        dk = norm_bwd(dkn_ref[...], kp_ref, ik_ref, kg_ref, kcol_ref).astype(BF16)
        dp_ref[:, D:2 * D] = dk
        dh += _dot(dk, wt_ref[D:2 * D, :])
        dvb = dv_ref[...].astype(BF16)
        dp_ref[:, 2 * D:3 * D] = dvb
        dh += _dot(dvb, wt_ref[2 * D:3 * D, :])
        dzb = dz_ref[...]
        dp_ref[:, 3 * D:4 * D] = dzb
        dh += _dot(dzb, wt_ref[3 * D:4 * D, :])
        dh += _dot(dfb, wft_ref[...])

        xv = x1_ref[...]
        inv = lax.rsqrt(jnp.mean(xv * xv, axis=-1, keepdims=True) + RMS_EPS)
        dg2_ref[...] += jnp.sum(dh * xv * inv, axis=0, keepdims=True)
        gh = dh * g2_ref[...]
        dx1_ref[...] = dx2_ref[...] + inv * gh - xv * (inv * inv * inv * jnp.mean(gh * xv, axis=-1, keepdims=True))

        @pl.when(step == steps - 1)
        def _():
            dqg_ref[...] = _fold_heads(qcol_ref[...], fold_ref[...])
            dkg_ref[...] = _fold_heads(kcol_ref[...], fold_ref[...])

    rr = lambda n: _rows(tm, n, rev=True, steps=steps)
    acc = lambda n, r=1: pl.BlockSpec((r, n), lambda i: (0, 0))
    heads_t = lambda: pl.BlockSpec((H, HA, tm), lambda i: (0, 0, steps - 1 - i))
    return pl.pallas_call(
        body, name="attn_proj_bwd", grid=(steps,),
        in_specs=[pl.BlockSpec((H, tm, HA), lambda i: (0, steps - 1 - i, 0)), heads_t(), heads_t(),
                  rr(D), rr(D), rr(D), rr(LANES), rr(LANES), rr(LANES), rr(D), rr(D),
                  _const((1, D)), _const((1, D)), _const((1, D)), _const((4 * D, D)), _const((LANES, D)),
                  _const((D, LANES)), _const((LANES, D)), _const((tm, tm)), _const((D, LANES))],
        out_specs=[rr(D), rr(4 * D), rr(LANES), acc(D), acc(LANES, 8), acc(LANES, 8), acc(LANES)],
        out_shape=[jax.ShapeDtypeStruct((s, D), F32), jax.ShapeDtypeStruct((s, 4 * D), BF16),
                   jax.ShapeDtypeStruct((s, LANES), BF16), jax.ShapeDtypeStruct((1, D), F32),
                   jax.ShapeDtypeStruct((8, LANES), F32), jax.ShapeDtypeStruct((8, LANES), F32),
                   jax.ShapeDtypeStruct((1, LANES), F32)],
        scratch_shapes=[pltpu.VMEM((8, LANES), F32), pltpu.VMEM((tm, LANES), F32), pltpu.VMEM((1, D), F32),
                        pltpu.VMEM((1, D), F32), pltpu.VMEM((tm, D), F32), pltpu.VMEM((tm, D), F32),
                        pltpu.VMEM((tm, D), F32)],
        compiler_params=_params("arbitrary"),
    )(dq, dkt, dvt, dz, qpre, kpre, invq, invk, fl, x1, dx2, g2, qg, kg, w2in_t, wf_t, e, et, triu, fold)


def _fold_heads(col, fold):
    hi, mid, lo = _split3(jnp.broadcast_to(col, (8, D)))
    return _dot(hi, fold) + _dot(mid, fold) + _dot(lo, fold)


def _conv_bwd(dx1, p1, x, g1, cw, w1out_t, w1in_t):
    s = x.shape[0]
    tm = min(TM_BWD, s)
    steps = s // tm
    halo = tm // 8

    def body(dx1_ref, b_ref, c_ref, xi_ref, z_ref, ch_ref, xh_ref, x_ref, g_ref, cw_ref, wot_ref, wit_ref,
             gx_ref, dp_ref, dcw_ref, dg1_ref, head_ref):
        step = pl.program_id(0)

        @pl.when(step == 0)
        def _():
            head_ref[...] = jnp.zeros_like(head_ref)
            dcw_ref[...] = jnp.zeros_like(dcw_ref)
            dg1_ref[...] = jnp.zeros_like(dg1_ref)

        first_tile = step == steps - 1
        dx1 = dx1_ref[...]
        row = lax.broadcasted_iota(jnp.int32, (tm, CH), 0)
        dyb = dx1.astype(BF16)
        for ci in range(D // CH):
            lo, hi = ci * CH, (ci + 1) * CH
            dyg = _dot(dyb, wot_ref[:, lo:hi])
            b = b_ref[:, lo:hi].astype(F32)
            c = c_ref[:, lo:hi].astype(F32)
            xin = xi_ref[:, lo:hi].astype(F32)
            z = z_ref[:, lo:hi].astype(F32)
            u = c * xin
            t6 = jnp.where(first_tile, 0.0, ch_ref[6:7, lo:hi].astype(F32) * xh_ref[6:7, lo:hi].astype(F32))
            t7 = jnp.where(first_tile, 0.0, ch_ref[7:8, lo:hi].astype(F32) * xh_ref[7:8, lo:hi].astype(F32))
            u1 = jnp.where(row == 0, t7, pltpu.roll(u, 1, 0))
            u2 = jnp.where(row == 0, t6, jnp.where(row == 1, t7, pltpu.roll(u, 2, 0)))
            w0, w1, w2 = cw_ref[0:1, lo:hi], cw_ref[1:2, lo:hi], cw_ref[2:3, lo:hi]
            y = w2 * u + w1 * u1 + w0 * u2
            sg = _sigmoid(z)
            sil = z * sg
            dp_ref[:, lo:hi] = (dyg * y * sil).astype(BF16)
            dy = dyg * b * sil
            dp_ref[:, 3 * D + lo:3 * D + hi] = (dyg * b * y * (sg * (1.0 + z * (1.0 - sg)))).astype(BF16)
            dcw_ref[2:3, lo:hi] += jnp.sum(dy * u, axis=0, keepdims=True)
            dcw_ref[1:2, lo:hi] += jnp.sum(dy * u1, axis=0, keepdims=True)
            dcw_ref[0:1, lo:hi] += jnp.sum(dy * u2, axis=0, keepdims=True)
            n0 = head_ref[0:1, lo:hi]
            n1 = head_ref[1:2, lo:hi]
            dyn1 = jnp.where(row == tm - 1, n0, pltpu.roll(dy, tm - 1, 0))
            dyn2 = jnp.where(row == tm - 2, n0, jnp.where(row == tm - 1, n1, pltpu.roll(dy, tm - 2, 0)))
            head_ref[:, lo:hi] = dy[0:8, :]
            du = w2 * dy + w1 * dyn1 + w0 * dyn2
            dp_ref[:, D + lo:D + hi] = (du * xin).astype(BF16)
            dp_ref[:, 2 * D + lo:2 * D + hi] = (du * c).astype(BF16)
        dh = _dot(dp_ref[:, 0:D], wit_ref[0:D, :])
        for k in range(1, 4):
            dh += _dot(dp_ref[:, k * D:(k + 1) * D], wit_ref[k * D:(k + 1) * D, :])
        xv = x_ref[...]
        inv = lax.rsqrt(jnp.mean(xv * xv, axis=-1, keepdims=True) + RMS_EPS)
        dg1_ref[...] += jnp.sum(dh * xv * inv, axis=0, keepdims=True)
        gh = dh * g_ref[...]
        gx_ref[...] = dx1 + inv * gh - xv * (inv * inv * inv * jnp.mean(gh * xv, axis=-1, keepdims=True))

    rr = lambda n: _rows(tm, n, rev=True, steps=steps)
    part = lambda k: pl.BlockSpec((tm, D), lambda i: (steps - 1 - i, k))
    prev8 = lambda k: pl.BlockSpec((8, D), lambda i: (jnp.maximum((steps - 1 - i) * halo - 1, 0), k))
    return pl.pallas_call(
        body, name="conv_bwd", grid=(steps,),
        in_specs=[rr(D), part(0), part(1), part(2), part(3), prev8(1), prev8(2), rr(D), _const((1, D)),
                  _const((8, D)), _const((D, D)), _const((4 * D, D))],
        out_specs=[rr(D), rr(4 * D), pl.BlockSpec((8, D), lambda i: (0, 0)), pl.BlockSpec((1, D), lambda i: (0, 0))],
        out_shape=[jax.ShapeDtypeStruct((s, D), F32), jax.ShapeDtypeStruct((s, 4 * D), BF16),
                   jax.ShapeDtypeStruct((8, D), F32), jax.ShapeDtypeStruct((1, D), F32)],
        scratch_shapes=[pltpu.VMEM((8, D), F32)],
        compiler_params=_params("arbitrary"),
    )(dx1, p1, p1, p1, p1, p1, p1, x, g1, cw, w1out_t, w1in_t)


def _wgrad(a, g, name):
    s, k = a.shape
    n = g.shape[1]
    bn = min(n, 1024)
    ts = min(512, s)

    def body(a_ref, g_ref, out_ref, acc_ref):
        t = pl.program_id(1)

        @pl.when(t == 0)
        def _():
            acc_ref[...] = jnp.zeros_like(acc_ref)

        acc_ref[...] += _dot_tn(a_ref[...], g_ref[...].astype(BF16))

        @pl.when(t == s // ts - 1)
        def _():
            out_ref[...] = acc_ref[...]

    return pl.pallas_call(
        body, name=name, grid=(n // bn, s // ts),
        in_specs=[pl.BlockSpec((ts, k), lambda j, t: (t, 0)), pl.BlockSpec((ts, bn), lambda j, t: (t, j))],
        out_specs=pl.BlockSpec((k, bn), lambda j, t: (0, j)),
        out_shape=jax.ShapeDtypeStruct((k, n), F32),
        scratch_shapes=[pltpu.VMEM((k, bn), F32)],
        compiler_params=_params("arbitrary", "arbitrary"),
    )(a, g)


def _adamw_math(w, g, m, v):
    m = ADAM_B1 * m + (1.0 - ADAM_B1) * g
    v = ADAM_B2 * v + (1.0 - ADAM_B2) * (g * g)
    m_hat = m / (1.0 - ADAM_B1 ** ADAM_STEP)
    v_hat = v / (1.0 - ADAM_B2 ** ADAM_STEP)
    delta = -ADAM_LR * (m_hat / (jnp.sqrt(v_hat) + ADAM_EPS) + ADAM_WD * w)
    return delta, m, v


def _adamw_big(parts, w, m, v):
    _, r, c_ = parts.shape
    rb = 368
    assert r % rb == 0

    def body(p_ref, w_ref, m_ref, v_ref, g_ref, d_ref, mo_ref, vo_ref):
        g = p_ref[0]
        for k in range(1, NDEV):
            g = g + p_ref[k]
        g_ref[...] = g
        d_ref[...], mo_ref[...], vo_ref[...] = _adamw_math(w_ref[...], g, m_ref[...], v_ref[...])

    blk = pl.BlockSpec((rb, c_), lambda i: (i, 0))
    return pl.pallas_call(
        body, name="adamw_big", grid=(r // rb,),
        in_specs=[pl.BlockSpec((NDEV, rb, c_), lambda i: (0, i, 0)), blk, blk, blk],
        out_specs=[blk, blk, blk, blk],
        out_shape=[jax.ShapeDtypeStruct((r, c_), F32)] * 4,
        compiler_params=_params("arbitrary"),
    )(parts, w, m, v)


def _sum_parts(parts, loss_row):
    _, r, c_ = parts.shape

    def body(p_ref, o_ref, l_ref):
        g = p_ref[0]
        for k in range(1, NDEV):
            g = g + p_ref[k]
        o_ref[...] = g
        per_row = jnp.sum(o_ref[loss_row:loss_row + 8, :], axis=1, keepdims=True)
        l_ref[...] = jnp.broadcast_to(jnp.sum(per_row, axis=0, keepdims=True), (8, c_))

    return pl.pallas_call(body, name="sum_small", out_shape=[jax.ShapeDtypeStruct((r, c_), F32),
                                                             jax.ShapeDtypeStruct((8, c_), F32)])(parts)


def _adamw_small(g, w, m, v):
    def body(g_ref, w_ref, m_ref, v_ref, d_ref, mo_ref, vo_ref):
        d_ref[...], mo_ref[...], vo_ref[...] = _adamw_math(w_ref[...], g_ref[...], m_ref[...], v_ref[...])

    return pl.pallas_call(body, name="adamw_small", out_shape=[jax.ShapeDtypeStruct(g.shape, F32)] * 3)(g, w, m, v)


def _pack_big(w1in, w2in, w1out, w2out, dtype):
    parts = [w1in.reshape(ROWS_W1IN, PACK_W), w2in.reshape(ROWS_W2IN, PACK_W), w1out.reshape(ROWS_WOUT, PACK_W),
             w2out.reshape(ROWS_WOUT, PACK_W)]
    used = ROWS_W1IN + ROWS_W2IN + 2 * ROWS_WOUT
    parts.append(jnp.zeros((PACK_ROWS - used, PACK_W), parts[0].dtype))
    return jnp.concatenate(parts, axis=0).astype(dtype)


def _unpack_big(p):
    a, b, c = ROWS_W1IN, ROWS_W1IN + ROWS_W2IN, ROWS_W1IN + ROWS_W2IN + ROWS_WOUT
    return (p[:a].reshape(1, D, 512), p[a:b].reshape(1, D, 514), p[b:c].reshape(1, 128, D),
            p[c:c + ROWS_WOUT].reshape(1, 128, D))


def _pad_lanes(a):
    return jnp.pad(a, ((0, 0), (0, LANES - a.shape[1])))


def _heads(a, s):
    return a.reshape(s, H, DH).transpose(1, 0, 2)


def _unheads(a, s):
    return a.transpose(1, 0, 2).reshape(s, D)


def _aug(cols, s):
    used = sum(c.shape[-1] for c in cols)
    return jnp.concatenate(cols + [jnp.zeros((H, s, HA - used), BF16)], axis=-1)


def _terms(v):
    return [t[..., None] for t in _split3(v)]


def _tiles(a, t):
    return a.reshape(H, a.shape[1] // t, t, HA)


def _tiles_t(a, t):
    return _tiles(a, t).transpose(0, 1, 3, 2)


def kernel(x, conv_norm_g, conv_w_in, conv_w, conv_w_out, attn_norm_g, attn_w_in, attn_b_f, attn_q_norm_g, attn_k_norm_g, attn_w_out, loss_target, m_conv_norm_g, m_conv_w_in, m_conv_w, m_conv_w_out, m_attn_norm_g, m_attn_w_in, m_attn_b_f, m_attn_q_norm_g, m_attn_k_norm_g, m_attn_w_out, v_conv_norm_g, v_conv_w_in, v_conv_w, v_conv_w_out, v_attn_norm_g, v_attn_w_in, v_attn_b_f, v_attn_q_norm_g, v_attn_k_norm_g, v_attn_w_out):
    s = x.shape[1]
    tq = min(TQ, s)
    tmf, tmb = min(TM_FWD, s), min(TM_BWD, s)
    me = 4 * lax.axis_index("x") + 2 * lax.axis_index("y") + lax.axis_index("c")
    xv, tgt = x[0], loss_target[0]

    wg = _all_gather(_pack_big(conv_w_in[0], attn_w_in[0], conv_w_out[0], attn_w_out[0], BF16), "gather_weights")
    small_w = jnp.concatenate([conv_w[0], attn_norm_g, jnp.zeros((4, 128), F32)], axis=0)
    sg_ = _all_gather(small_w, "gather_small_weights")
    a, b, c = ROWS_W1IN, ROWS_W1IN + ROWS_W2IN, ROWS_W1IN + ROWS_W2IN + ROWS_WOUT
    w1in = wg[:, :a].transpose(1, 0, 2).reshape(D, 4 * D)
    w2all = wg[:, a:b].reshape(NDEV, D, 514).transpose(1, 0, 2).reshape(D, 4 * D + H)
    w2in, wf = w2all[:, :4 * D], _pad_lanes(w2all[:, 4 * D:])
    w1out = wg[:, b:c].reshape(D, D)
    w2out = wg[:, c:c + ROWS_WOUT].reshape(D, D)
    cw = jnp.concatenate([sg_[:, 0:3, :].transpose(1, 0, 2).reshape(3, D), jnp.zeros((5, D), F32)], axis=0)
    g2 = sg_[:, 3, :].reshape(1, D)
    qg_t, kg_t = jnp.tile(attn_q_norm_g, (1, H)), jnp.tile(attn_k_norm_g, (1, H))
    bf = _pad_lanes(attn_b_f)

    e = (jnp.arange(D)[:, None] // DH == jnp.arange(LANES)[None, :]).astype(BF16)
    fold = (jnp.arange(D)[:, None] % DH == jnp.arange(LANES)[None, :]).astype(BF16)
    tril = (jnp.arange(tmb)[:, None] >= jnp.arange(tmb)[None, :]).astype(BF16)
    triu = tril.T
    src = jnp.arange(3 * LANES)
    dst = jnp.arange(H * HA)[None, :] - (HA * (src % LANES) + src // LANES)[:, None]
    place_k = ((dst == COL_BIAS) & (src % LANES < H)[:, None]).astype(BF16)
    place_q = ((dst == ROW_BIAS) & (src % LANES < H)[:, None]).astype(BF16)

    x1, h1, p1, yg = _conv_fwd(xv, conv_norm_g, w1in, cw, w1out)
    reach = 1.01 * math.sqrt(DH) * jnp.max(jnp.abs(attn_q_norm_g)) * jnp.max(jnp.abs(attn_k_norm_g))
    bounded = (2.0 * reach <= BOUNDED_SOFTMAX_REACH).astype(F32).reshape(1, 1)
    h2, qpre, kpre, z, qa, ka, va, invq, invk, cc, fl = _attn_proj_fwd(
        x1, g2, w2in, wf, bf, qg_t, kg_t, jnp.broadcast_to(reach, (1, LANES)), e, e.T, tril, place_k, place_q)
    ka, va = _tiles(ka, tq), _tiles(va, tq)
    ol, qab = _flash_fwd(bounded, _tiles(qa, tq), ka, va)
    og, dx2, dz, doa, lossp = _attn_out(ol.reshape(H, s, HA), z, x1, tgt, w2out, w2out.T, e, place_k)

    dq, dkt, dvt = _flash_bwd(qab, _tiles(doa, tq), ka, va)
    dx1, dp2, df, dg2, dqg, dkg, dbf = _attn_proj_bwd(
        dq.reshape(H, s, HA), dkt, dvt, dz, qpre, kpre, invq, invk, fl, x1, dx2, g2, qg_t, kg_t, w2in.T, wf.T, e, e.T, triu, fold)
    gx, dp1, dcw, dg1 = _conv_bwd(dx1, p1, xv, conv_norm_g, cw, w1out.T, w1in.T)
    dw2out = _wgrad(og, dx2, "wgrad_attn_out")
    dw2in = _wgrad(h2, dp2, "wgrad_attn_in")
    dwf = _wgrad(h2, df, "wgrad_attn_forget")
    dw1out = _wgrad(yg, dx1, "wgrad_conv_out")
    dw1in = _wgrad(h1, dp1, "wgrad_conv_in")

    dw2all = jnp.concatenate([dw2in, dwf[:, :H]], axis=1)
    gfull = jnp.concatenate([
        dw1in.reshape(D, NDEV, 512).transpose(1, 0, 2),
        dw2all.reshape(D, NDEV, 514).transpose(1, 0, 2).reshape(NDEV, ROWS_W2IN, PACK_W),
        dw1out.reshape(NDEV, ROWS_WOUT, PACK_W), dw2out.reshape(NDEV, ROWS_WOUT, PACK_W),
        jnp.zeros((NDEV, PACK_ROWS - c - ROWS_WOUT, PACK_W), F32)], axis=1)
    parts = _exchange_blocks(gfull, "scatter_grads")
    pack = lambda t1, t2, t3, t4: _pack_big(t1[0], t2[0], t3[0], t4[0], F32)
    gb, db, mb, vb = _adamw_big(parts, pack(conv_w_in, attn_w_in, conv_w_out, attn_w_out),
                                pack(m_conv_w_in, m_attn_w_in, m_conv_w_out, m_attn_w_out),
                                pack(v_conv_w_in, v_attn_w_in, v_conv_w_out, v_attn_w_out))

    shard_rows = jnp.concatenate([dcw[0:3].reshape(3, NDEV, 128).transpose(1, 0, 2), dg2.reshape(NDEV, 1, 128),
                                  jnp.zeros((NDEV, 4, 128), F32)], axis=1).reshape(64, 128)
    small_g = jnp.concatenate([shard_rows, dg1.reshape(8, 128), dbf, dqg[0:1], dkg[0:1], jnp.zeros((5, 128), F32),
                               lossp.reshape(8, 128)], axis=0)
    gs, loss8 = _sum_parts(_all_gather(small_g, "gather_small_grads"), 80)
    loss = loss8[0, 0]
    mine = lax.dynamic_slice(gs, (8 * me, 0), (8, 128))
    g_small = jnp.concatenate([mine, gs[64:75], jnp.zeros((5, 128), F32)], axis=0)

    def pack_small(cwk, ang, cng, bfk, qgk, kgk):
        return jnp.concatenate([cwk[0], ang, jnp.zeros((4, 128), F32), cng.reshape(8, 128), _pad_lanes(bfk),
                                _pad_lanes(qgk), _pad_lanes(kgk), jnp.zeros((5, 128), F32)], axis=0)

    ds_, ms_, vs_ = _adamw_small(
        g_small, pack_small(conv_w, attn_norm_g, conv_norm_g, attn_b_f, attn_q_norm_g, attn_k_norm_g),
        pack_small(m_conv_w, m_attn_norm_g, m_conv_norm_g, m_attn_b_f, m_attn_q_norm_g, m_attn_k_norm_g),
        pack_small(v_conv_w, v_attn_norm_g, v_conv_norm_g, v_attn_b_f, v_attn_q_norm_g, v_attn_k_norm_g))

    def leaves(big, small):
        w1i, w2i, w1o, w2o = _unpack_big(big)
        return (small[8:16].reshape(1, D), w1i, small[0:3].reshape(1, 3, 128), w1o, small[3:4], w2i,
                small[16:17, :H], small[17:18, :DH], small[18:19, :DH], w2o)

    return (loss, gx[None], *leaves(gb, g_small), *leaves(db, ds_), *leaves(mb, ms_), *leaves(vb, vs_))
```

```python
import functools
import math

import jax
import jax.numpy as jnp
from jax import lax
from jax.experimental import pallas as pl
from jax.experimental.pallas import tpu as pltpu

F32 = jnp.float32
BF16 = jnp.bfloat16

D = 1024
H = 16
DH = 64
NDEV = 8
RMS_EPS = 1e-6
LANES = 128
HA = 128
TM_FWD = 512
TM_BWD = 256
TQ = 512
TK = 512
BWD_TILES = 4
CH = 256
PACK_W = 512
PACK_ROWS = 2576
ROWS_W1IN, ROWS_W2IN, ROWS_WOUT = 1024, 1028, 256
ADAM_LR, ADAM_B1, ADAM_B2, ADAM_EPS, ADAM_WD, ADAM_STEP = 0.001, 0.9, 0.999, 1e-08, 0.01, 10
VMEM_LIMIT = 56 * 1024 * 1024
MASKED = -1e30
BOUNDED_SOFTMAX_REACH = 60.0
MESH = pl.DeviceIdType.MESH


def _params(*sem, vmem=VMEM_LIMIT):
    return pltpu.CompilerParams(dimension_semantics=sem or None, vmem_limit_bytes=vmem)


def _const(shape):
    nd = len(shape)
    return pl.BlockSpec(shape, lambda *_: (0,) * nd, pipeline_mode=pl.Buffered(1))


def _rows(tm, n, rev=False, steps=None):
    if rev:
        return pl.BlockSpec((tm, n), lambda i: (steps - 1 - i, 0))
    return pl.BlockSpec((tm, n), lambda i: (i, 0))


def _dot(a, b):
    return jnp.dot(a, b, preferred_element_type=F32)


def _top16(x):
    bits = lax.bitcast_convert_type(x, jnp.uint32) & jnp.uint32(0xFFFF0000)
    return lax.bitcast_convert_type(bits, F32)


def _split2(x):
    hi = _top16(x)
    return hi.astype(BF16), (x - hi).astype(BF16)


def _split3(x):
    hi = _top16(x)
    r = x - hi
    mid = _top16(r)
    return hi.astype(BF16), mid.astype(BF16), (r - mid).astype(BF16)


def _seg_sum(a, e):
    hi, lo = _split2(a)
    return _dot(hi, e) + _dot(lo, e)


def _seg_bcast(s, et):
    hi, lo = _split2(s)
    return _dot(hi, et) + _dot(lo, et)


def _tri_sum(t, v):
    hi, mid, lo = _split3(v)
    return _dot(t, hi) + _dot(t, mid) + _dot(t, lo)


def _sigmoid(z):
    return 1.0 / (1.0 + jnp.exp(-z))


def _place():
    return lax.axis_index("x"), lax.axis_index("y"), lax.axis_index("c")


def _all_gather(xb, name):
    r, c_ = xb.shape

    def body(x_ref, out_ref, send_sems, recv_sems, local_sem):
        x, y, c = _place()
        me, sibling = (x, y, c), (x, y, 1 - c)
        chips = [(1 - x, y), (x, 1 - y), (1 - x, 1 - y)]

        def slab(px, py, pc):
            return out_ref.at[4 * px + 2 * py + pc]

        def copy(k, block, to, src=None):
            return pltpu.make_async_remote_copy(
                src_ref=slab(*block) if src is None else src, dst_ref=slab(*block),
                send_sem=send_sems.at[k], recv_sem=recv_sems.at[k], device_id=to, device_id_type=MESH)

        mine = pltpu.make_async_copy(x_ref, slab(*me), local_sem)
        mine.start()
        first = [copy(0, me, sibling, src=x_ref)]
        first += [copy(1 + j, me, (*chip, c), src=x_ref) for j, chip in enumerate(chips)]
        for cp in first:
            cp.start()
        passed = [copy(4 + j, (*chip, c), sibling) for j, chip in enumerate(chips)]
        for j, chip in enumerate(chips):
            copy(1 + j, (*chip, c), me).wait_recv()
            passed[j].start()
        copy(0, sibling, me).wait_recv()
        for j, chip in enumerate(chips):
            copy(4 + j, (*chip, 1 - c), me).wait_recv()
        for cp in first + passed:
            cp.wait_send()
        mine.wait()

    return pl.pallas_call(
        body, name=name,
        out_shape=jax.ShapeDtypeStruct((NDEV, r, c_), xb.dtype),
        in_specs=[pl.BlockSpec(memory_space=pl.ANY)],
        out_specs=pl.BlockSpec(memory_space=pl.ANY),
        scratch_shapes=[pltpu.SemaphoreType.DMA((7,)), pltpu.SemaphoreType.DMA((7,)), pltpu.SemaphoreType.DMA],
    )(xb)


def _exchange_blocks(g, name):
    _, r, c_ = g.shape

    def body(g_ref, out_ref, send_sems, recv_sems, local_sem):
        x, y, c = _place()
        me = 4 * x + 2 * y + c
        mine = pltpu.make_async_copy(g_ref.at[me], out_ref.at[0], local_sem)
        mine.start()
        copies = []
        for k in range(1, NDEV):
            px = 1 - x if k & 4 else x
            py = 1 - y if k & 2 else y
            pc = 1 - c if k & 1 else c
            cp = pltpu.make_async_remote_copy(
                src_ref=g_ref.at[4 * px + 2 * py + pc], dst_ref=out_ref.at[k],
                send_sem=send_sems.at[k - 1], recv_sem=recv_sems.at[k - 1],
                device_id=(px, py, pc), device_id_type=MESH)
            cp.start()
            copies.append(cp)
        for cp in copies:
            cp.wait()
        mine.wait()

    return pl.pallas_call(
        body, name=name,
        out_shape=jax.ShapeDtypeStruct((NDEV, r, c_), g.dtype),
        in_specs=[pl.BlockSpec(memory_space=pl.ANY)],
        out_specs=pl.BlockSpec(memory_space=pl.ANY),
        scratch_shapes=[pltpu.SemaphoreType.DMA((7,)), pltpu.SemaphoreType.DMA((7,)), pltpu.SemaphoreType.DMA],
    )(g)


def _conv_fwd(x, g1, w1in, cw, w1out):
    s = x.shape[0]
    tm = min(TM_FWD, s)
    steps = s // tm

    def body(x_ref, g_ref, win_ref, cw_ref, wout_ref, x1_ref, h1_ref, p1_ref, yg_ref, tail_ref):
        @pl.when(pl.program_id(0) == 0)
        def _():
            tail_ref[...] = jnp.zeros_like(tail_ref)

        xv = x_ref[...]
        inv = lax.rsqrt(jnp.mean(xv * xv, axis=-1, keepdims=True) + RMS_EPS)
        h = (xv * inv * g_ref[...]).astype(BF16)
        h1_ref[...] = h
        row = lax.broadcasted_iota(jnp.int32, (tm, CH), 0)
        for ci in range(D // CH):
            lo, hi = ci * CH, (ci + 1) * CH
            parts = []
            for k in range(4):
                pk = _dot(h, win_ref[:, k * D + lo:k * D + hi]).astype(BF16)
                p1_ref[:, k * D + lo:k * D + hi] = pk
                parts.append(pk.astype(F32))
            b, c, xin, z = parts
            u = c * xin
            t6 = tail_ref[6:7, lo:hi]
            t7 = tail_ref[7:8, lo:hi]
            u1 = jnp.where(row == 0, t7, pltpu.roll(u, 1, 0))
            u2 = jnp.where(row == 0, t6, jnp.where(row == 1, t7, pltpu.roll(u, 2, 0)))
            tail_ref[:, lo:hi] = u[tm - 8:, :]
            y = cw_ref[2:3, lo:hi] * u + cw_ref[1:2, lo:hi] * u1 + cw_ref[0:1, lo:hi] * u2
            yg_ref[:, lo:hi] = (b * y * (z * _sigmoid(z))).astype(BF16)
        x1_ref[...] = xv + _dot(yg_ref[...], wout_ref[...])

    return pl.pallas_call(
        body, name="conv_fwd", grid=(steps,),
        in_specs=[_rows(tm, D), _const((1, D)), _const((D, 4 * D)), _const((8, D)), _const((D, D))],
        out_specs=[_rows(tm, D), _rows(tm, D), _rows(tm, 4 * D), _rows(tm, D)],
        out_shape=[jax.ShapeDtypeStruct((s, D), F32), jax.ShapeDtypeStruct((s, D), BF16),
                   jax.ShapeDtypeStruct((s, 4 * D), BF16), jax.ShapeDtypeStruct((s, D), BF16)],
        scratch_shapes=[pltpu.VMEM((8, D), F32)],
        compiler_params=_params("arbitrary"),
    )(x, g1, w1in, cw, w1out)


COL_BIAS = DH
ROW_BIAS = DH + 3


def _terms_cat(val):
    return jnp.concatenate(_split3(val), axis=1)


def _split_heads(x, aug, out_ref, lane):
    for hp in range(H // 2):
        pair = x[:, hp * HA:(hp + 1) * HA]
        out_ref[2 * hp] = jnp.where(lane < DH, pair, aug(2 * hp)).astype(BF16)
        out_ref[2 * hp + 1] = jnp.where(lane < DH, pltpu.roll(pair, DH, 1), aug(2 * hp + 1)).astype(BF16)


def _ones_at(lane, first):
    return jnp.where((lane >= first) & (lane < first + 3), 1.0, 0.0)


def _attn_proj_fwd(x1, g2, w2in, wf, bf, qg, kg, shift, e, et, tril, place_k, place_q):
    s = x1.shape[0]
    tm = min(TM_BWD, s)
    steps = s // tm

    def body(x_ref, g_ref, win_ref, wf_ref, bf_ref, qg_ref, kg_ref, sh_ref, e_ref, et_ref, tril_ref, pk_ref, pq_ref,
             h2_ref, qp_ref, kp_ref, z_ref, qa_ref, ka_ref, va_ref, iq_ref, ik_ref, c_ref, fl_ref, carry_ref):
        @pl.when(pl.program_id(0) == 0)
        def _():
            carry_ref[...] = jnp.zeros_like(carry_ref)

        xv = x_ref[...]
        inv = lax.rsqrt(jnp.mean(xv * xv, axis=-1, keepdims=True) + RMS_EPS)
        h = (xv * inv * g_ref[...]).astype(BF16)
        h2_ref[...] = h

        fl = _dot(h, wf_ref[...]) + bf_ref[...]
        fl_ref[...] = fl
        ex = jnp.exp(-jnp.abs(fl))
        up = 1.0 + ex
        log1p = jnp.where(up == 1.0, ex, jnp.log(up) * ex / (up - 1.0))
        lane = lax.broadcasted_iota(jnp.int32, (tm, LANES), 1)
        logf = jnp.where(lane < H, jnp.minimum(fl, 0.0) - log1p, 0.0)
        c_ref[...] = _tri_sum(tril_ref[...], logf) + carry_ref[0:1, :]
        carry_ref[0:1, :] = c_ref[tm - 1:tm, :]
        c = c_ref[...]

        def normed(col, pre_ref, inv_ref, gain_ref, scale):
            pre = _dot(h, win_ref[:, col * D:(col + 1) * D]).astype(BF16)
            pre_ref[...] = pre
            t = pre.astype(F32)
            invh = lax.rsqrt(_seg_sum(t * t, e_ref[...]) * (1.0 / DH) + RMS_EPS)
            inv_ref[...] = invh
            return t * _seg_bcast(invh, et_ref[...]) * (gain_ref[...] * scale)

        ones_col, ones_row = _ones_at(lane, COL_BIAS), _ones_at(lane, ROW_BIAS)
        q_bias = _dot(_terms_cat(c - sh_ref[...]), pq_ref[...])
        _split_heads(normed(0, qp_ref, iq_ref, qg_ref, 1.0 / math.sqrt(DH)),
                     lambda hh: q_bias[:, hh * HA:(hh + 1) * HA] + ones_col, qa_ref, lane)
        k_bias = _dot(_terms_cat(-c), pk_ref[...])
        _split_heads(normed(1, kp_ref, ik_ref, kg_ref, 1.0),
                     lambda hh: k_bias[:, hh * HA:(hh + 1) * HA] + ones_row, ka_ref, lane)
        _split_heads(_dot(h, win_ref[:, 2 * D:3 * D]), lambda hh: ones_col, va_ref, lane)
        z_ref[...] = _dot(h, win_ref[:, 3 * D:4 * D]).astype(BF16)

    row_bf = lambda: _rows(tm, D)
    row_sm = lambda: _rows(tm, LANES)
    heads = lambda: pl.BlockSpec((H, tm, HA), lambda i: (0, i, 0))
    return pl.pallas_call(
        body, name="attn_proj_fwd", grid=(steps,),
        in_specs=[_rows(tm, D), _const((1, D)), _const((D, 4 * D)), _const((D, LANES)), _const((1, LANES)),
                  _const((1, D)), _const((1, D)), _const((1, LANES)), _const((D, LANES)), _const((LANES, D)),
                  _const((tm, tm)), _const((3 * LANES, H * HA)), _const((3 * LANES, H * HA))],
        out_specs=[row_bf() for _ in range(4)] + [heads() for _ in range(3)] + [row_sm() for _ in range(4)],
        out_shape=[jax.ShapeDtypeStruct((s, D), BF16)] * 4 + [jax.ShapeDtypeStruct((H, s, HA), BF16)] * 3 + [
            jax.ShapeDtypeStruct((s, LANES), F32)] * 4,
        scratch_shapes=[pltpu.VMEM((8, LANES), F32)],
        compiler_params=_params("arbitrary"),
    )(x1, g2, w2in, wf, bf, qg, kg, shift, e, et, tril, place_k, place_q)


def _dot_nt(a, b):
    return lax.dot_general(a, b, (((1,), (1,)), ((), ())), preferred_element_type=F32)


def _dot_tn(a, b):
    return lax.dot_general(a, b, (((0,), (0,)), ((), ())), preferred_element_type=F32)


def _flash_fwd(bounded, qa, ka, va):
    h_, nq, tq, _ = qa.shape
    nk, tk = ka.shape[1], ka.shape[2]

    def body(flag_ref, q_ref, k_ref, v_ref, o_ref, qb_ref, m_ref, acc_ref):
        i = pl.program_id(1)
        lane = lax.broadcasted_iota(jnp.int32, (tq, HA), 1)
        in_bias = (lane >= ROW_BIAS) & (lane < ROW_BIAS + 3)
        acc_ref[...] = jnp.zeros_like(acc_ref)

        def run(use_bound):
            q = q_ref[...]
            if use_bound:
                m_ref[...] = -jnp.sum(jnp.where(in_bias, q.astype(F32), 0.0), axis=1, keepdims=True)
            else:
                q = jnp.where(in_bias, jnp.zeros_like(q), q)
                m_ref[...] = jnp.full_like(m_ref, MASKED)

            def step(j, tiles, masked):
                keys = pl.ds(pl.multiple_of(j * tk, tk), tiles * tk)
                z = _dot_nt(q, k_ref[keys, :])
                if masked:
                    rows = lax.broadcasted_iota(jnp.int32, (tq, tk), 0)
                    cols = lax.broadcasted_iota(jnp.int32, (tq, tk), 1)
                    z = jnp.where(rows >= cols, z, MASKED)
                if use_bound:
                    acc_ref[...] += _dot(jnp.exp(z).astype(BF16), v_ref[keys, :])
                else:
                    m_old = m_ref[...]
                    m_new = jnp.maximum(m_old, jnp.max(z, axis=1, keepdims=True))
                    p = jnp.exp(z - m_new)
                    acc_ref[...] = jnp.exp(m_old - m_new) * acc_ref[...] + _dot(p.astype(BF16), v_ref[keys, :])
                    m_ref[...] = m_new

            def four_tiles(jj, carry):
                step(4 * jj, 4, False)
                return carry

            lax.fori_loop(0, lax.shift_right_logical(i, 2), four_tiles, 0)
            pl.when((i & 2) == 2)(lambda: step(i - (i & 3), 2, False))
            pl.when((i & 1) == 1)(lambda: step(i - 1, 1, False))
            step(i, 1, True)

        use_bound = flag_ref[0, 0] > 0.5
        pl.when(use_bound)(lambda: run(True))
        pl.when(jnp.logical_not(use_bound))(lambda: run(False))

        acc = acc_ref[...]
        l = jnp.sum(jnp.where(lane == COL_BIAS, acc, 0.0), axis=1, keepdims=True)
        lse = m_ref[...] + jnp.log(l)
        o_ref[...] = jnp.where(lane < DH, acc / l, lse)
        hi, mid, lo = _split3(-lse)
        qb_ref[...] = jnp.where(lane == ROW_BIAS, hi, jnp.where(lane == ROW_BIAS + 1, mid,
                                                                jnp.where(lane == ROW_BIAS + 2, lo, q_ref[...])))

    tile = lambda: pl.BlockSpec((None, None, tq, HA), lambda h, i: (h, i, 0, 0))
    head = lambda: pl.BlockSpec((None, nk * tk, HA), lambda h, i: (h, 0, 0))
    return pl.pallas_call(
        body, name="flash_fwd", grid=(h_, nq),
        in_specs=[pl.BlockSpec(memory_space=pltpu.SMEM), tile(), head(), head()],
        out_specs=[tile(), tile()],
        out_shape=[jax.ShapeDtypeStruct((h_, nq, tq, HA), F32), jax.ShapeDtypeStruct((h_, nq, tq, HA), BF16)],
        scratch_shapes=[pltpu.VMEM((tq, 1), F32), pltpu.VMEM((tq, HA), F32)],
        compiler_params=_params("arbitrary", "arbitrary"),
    )(bounded, qa, ka.reshape(h_, nk * tk, HA), va.reshape(h_, nk * tk, HA))


def _merge_heads(src, out_ref, lane):
    for hp in range(H // 2):
        out_ref[:, hp * HA:(hp + 1) * HA] = jnp.where(lane < DH, src(2 * hp), pltpu.roll(src(2 * hp + 1), DH, 1))


def _attn_out(ol, z, x1, tgt, w2out, w2out_t, e, place_k):
    s = z.shape[0]
    tm = min(TM_BWD, s)
    steps = s // tm

    def body(ol_ref, z_ref, x1_ref, t_ref, w_ref, wt_ref, e_ref, pk_ref, og_ref, dx2_ref, dz_ref, doa_ref, loss_ref,
             o_ref):
        @pl.when(pl.program_id(0) == 0)
        def _():
            loss_ref[...] = jnp.zeros_like(loss_ref)

        lane = lax.broadcasted_iota(jnp.int32, (tm, HA), 1)
        _merge_heads(lambda hh: ol_ref[hh], o_ref, lane)
        ov = o_ref[...]
        zv = z_ref[...].astype(F32)
        sg = _sigmoid(zv)
        sil = zv * sg
        og = (ov * sil).astype(BF16)
        og_ref[...] = og
        err = x1_ref[...] + _dot(og, w_ref[...]) - t_ref[...]
        loss_ref[...] += (0.5 / D) * jnp.sum(err * err, axis=0, keepdims=True)
        dx2 = err * (1.0 / D)
        dx2_ref[...] = dx2
        dog = _dot(dx2.astype(BF16), wt_ref[...])
        do = (dog * sil).astype(BF16).astype(F32)
        dz_ref[...] = (dog * ov * (sg * (1.0 + zv * (1.0 - sg)))).astype(BF16)
        delta = _seg_sum(do * ov, e_ref[...])
        d_bias = _dot(_terms_cat(-delta), pk_ref[...])
        _split_heads(do, lambda hh: d_bias[:, hh * HA:(hh + 1) * HA], doa_ref, lane)

    heads = lambda: pl.BlockSpec((H, tm, HA), lambda i: (0, i, 0))
    return pl.pallas_call(
        body, name="attn_out", grid=(steps,),
        in_specs=[heads(), _rows(tm, D), _rows(tm, D), _rows(tm, D), _const((D, D)), _const((D, D)),
                  _const((D, LANES)), _const((3 * LANES, H * HA))],
        out_specs=[_rows(tm, D), _rows(tm, D), _rows(tm, D), heads(), pl.BlockSpec((1, D), lambda i: (0, 0))],
        out_shape=[jax.ShapeDtypeStruct((s, D), BF16), jax.ShapeDtypeStruct((s, D), F32),
                   jax.ShapeDtypeStruct((s, D), BF16), jax.ShapeDtypeStruct((H, s, HA), BF16),
                   jax.ShapeDtypeStruct((1, D), F32)],
        scratch_shapes=[pltpu.VMEM((tm, D), F32)],
        compiler_params=_params("arbitrary"),
    )(ol, z, x1, tgt, w2out, w2out_t, e, place_k)


def _flash_bwd(qab, doa, ka, va):
    h_, nq, tq, _ = qab.shape
    nk, tk = ka.shape[1], ka.shape[2]
    s = nk * tk

    def body(qa_ref, da_ref, ka_ref, va_ref, dq_hbm, dkt_ref, dvt_ref, dq_acc, sem):
        hh = pl.program_id(0)
        j = pl.program_id(1)

        @pl.when(j == 0)
        def _():
            dq_acc[...] = jnp.zeros_like(dq_acc)

        dkt_ref[...] = jnp.zeros_like(dkt_ref)
        dvt_ref[...] = jnp.zeros_like(dvt_ref)
        ka_v = ka_ref[...]
        va_v = va_ref[...]

        def step(i, tiles, masked):
            rows = pl.ds(pl.multiple_of(i * tq, tq), tiles * tq)
            qa_i, da_i = qa_ref[rows, :], da_ref[rows, :]
            zz = _dot_nt(qa_i, ka_v)
            if masked:
                row = lax.broadcasted_iota(jnp.int32, (tq, tk), 0)
                col = lax.broadcasted_iota(jnp.int32, (tq, tk), 1)
                zz = jnp.where(row >= col, zz, MASKED)
            p = jnp.exp(zz)
            ds = (p * _dot_nt(da_i, va_v)).astype(BF16)
            pb = p.astype(BF16)
            dq_acc[rows, :] += _dot(ds, ka_v)
            dvt_ref[...] += _dot_tn(da_i, pb)
            dkt_ref[...] += _dot_tn(qa_i, ds)

        step(j, 1, True)
        below = nq - 1 - j

        def wide(t, carry):
            step(j + 1 + BWD_TILES * t, BWD_TILES, False)
            return carry

        trips = below // BWD_TILES
        lax.fori_loop(0, trips, wide, 0)
        for k in range(BWD_TILES - 1):
            pl.when(trips * BWD_TILES + k < below)(
                functools.partial(step, j + 1 + trips * BWD_TILES + k, 1, False))

        @pl.when(j == nk - 1)
        def _():
            cp = pltpu.make_async_copy(dq_acc, dq_hbm.at[hh], sem)
            cp.start()
            cp.wait()

    whole = lambda: pl.BlockSpec((None, nq * tq, HA), lambda h, j: (h, 0, 0))
    tile = lambda: pl.BlockSpec((None, None, tk, HA), lambda h, j: (h, j, 0, 0))
    return pl.pallas_call(
        body, name="flash_bwd", grid=(h_, nk),
        in_specs=[whole(), whole(), tile(), tile()],
        out_specs=[pl.BlockSpec(memory_space=pl.ANY),
                   pl.BlockSpec((None, HA, tk), lambda h, j: (h, 0, j)),
                   pl.BlockSpec((None, HA, tk), lambda h, j: (h, 0, j))],
        out_shape=[jax.ShapeDtypeStruct((h_, s, HA), F32), jax.ShapeDtypeStruct((h_, HA, s), F32),
                   jax.ShapeDtypeStruct((h_, HA, s), F32)],
        scratch_shapes=[pltpu.VMEM((s, HA), F32), pltpu.SemaphoreType.DMA],
        compiler_params=_params("arbitrary", "arbitrary"),
    )(qab.reshape(h_, s, HA), doa.reshape(h_, s, HA), ka, va)


def _attn_proj_bwd(dq, dkt, dvt, dz, qpre, kpre, invq, invk, fl, x1, dx2, g2, qg, kg, w2in_t, wf_t, e, et,
                   triu, fold):
    s = x1.shape[0]
    tm = min(TM_BWD, s)
    steps = s // tm

    def body(dq_ref, dkt_ref, dvt_ref, dz_ref, qp_ref, kp_ref, iq_ref, ik_ref, fl_ref, x1_ref, dx2_ref,
             g2_ref, qg_ref, kg_ref, wt_ref, wft_ref, e_ref, et_ref, triu_ref, fold_ref,
             dx1_ref, dp_ref, df_ref, dg2_ref, dqg_ref, dkg_ref, dbf_ref, carry_ref, rc_ref, qcol_ref, kcol_ref,
             dqs_ref, dkn_ref, dv_ref):
        step = pl.program_id(0)
        lane = lax.broadcasted_iota(jnp.int32, (tm, HA), 1)
        _merge_heads(lambda hh: dvt_ref[hh].T, dv_ref, lane)
        dc = jnp.zeros((tm, HA), F32)
        for hp in range(H // 2):
            dqh = [dq_ref[2 * hp], dq_ref[2 * hp + 1]]
            dkh = [dkt_ref[2 * hp].T, dkt_ref[2 * hp + 1].T]
            dqs_ref[:, hp * HA:(hp + 1) * HA] = jnp.where(lane < DH, dqh[0], pltpu.roll(dqh[1], DH, 1))
            dkn_ref[:, hp * HA:(hp + 1) * HA] = jnp.where(lane < DH, dkh[0], pltpu.roll(dkh[1], DH, 1))
            for k in range(2):
                sums = jnp.where(lane == ROW_BIAS, dqh[k], 0.0) - pltpu.roll(
                    jnp.where(lane == COL_BIAS, dkh[k], 0.0), ROW_BIAS - COL_BIAS, 1)
                dc = dc + pltpu.roll(sums, (2 * hp + k - ROW_BIAS) % HA, 1)

        @pl.when(step == 0)
        def _():
            carry_ref[...] = jnp.zeros_like(carry_ref)
            dg2_ref[...] = jnp.zeros_like(dg2_ref)
            dbf_ref[...] = jnp.zeros_like(dbf_ref)
            qcol_ref[...] = jnp.zeros_like(qcol_ref)
            kcol_ref[...] = jnp.zeros_like(kcol_ref)
            dqg_ref[...] = jnp.zeros_like(dqg_ref)
            dkg_ref[...] = jnp.zeros_like(dkg_ref)

        rc_ref[...] = _tri_sum(triu_ref[...], dc) + carry_ref[0:1, :]
        carry_ref[0:1, :] = rc_ref[0:1, :]
        df = rc_ref[...] * _sigmoid(-fl_ref[...])
        dfb = df.astype(BF16)
        df_ref[...] = dfb
        dbf_ref[...] += jnp.sum(df, axis=0, keepdims=True)

        def norm_bwd(dn, pre_ref, inv_ref, gain_ref, col_ref):
            pre = pre_ref[...].astype(F32)
            invh = inv_ref[...]
            invb = _seg_bcast(invh, et_ref[...])
            col_ref[...] += jnp.sum(dn * pre * invb, axis=0, keepdims=True)
            gd = dn * gain_ref[...]
            mean = _seg_sum(gd * pre, e_ref[...]) * (1.0 / DH)
            return invb * gd - pre * _seg_bcast(mean * invh * invh * invh, et_ref[...])

        dq = norm_bwd(dqs_ref[...] * (1.0 / math.sqrt(DH)), qp_ref, iq_ref, qg_ref, qcol_ref).astype(BF16)
        dp_ref[:, 0:D] = dq
        dh = _dot(dq, wt_ref[0:D, :])
        dk = norm_bwd(dkn_ref[...], kp_ref, ik_ref, kg_ref, kcol_ref).astype(BF16)
        dp_ref[:, D:2 * D] = dk
        dh += _dot(dk, wt_ref[D:2 * D, :])
        dvb = dv_ref[...].astype(BF16)
        dp_ref[:, 2 * D:3 * D] = dvb
        dh += _dot(dvb, wt_ref[2 * D:3 * D, :])
        dzb = dz_ref[...]
        dp_ref[:, 3 * D:4 * D] = dzb
        dh += _dot(dzb, wt_ref[3 * D:4 * D, :])
        dh += _dot(dfb, wft_ref[...])

        xv = x1_ref[...]
        inv = lax.rsqrt(jnp.mean(xv * xv, axis=-1, keepdims=True) + RMS_EPS)
        dg2_ref[...] += jnp.sum(dh * xv * inv, axis=0, keepdims=True)
        gh = dh * g2_ref[...]
        dx1_ref[...] = dx2_ref[...] + inv * gh - xv * (inv * inv * inv * jnp.mean(gh * xv, axis=-1, keepdims=True))

        @pl.when(step == steps - 1)
        def _():
            dqg_ref[...] = _fold_heads(qcol_ref[...], fold_ref[...])
            dkg_ref[...] = _fold_heads(kcol_ref[...], fold_ref[...])

    rr = lambda n: _rows(tm, n, rev=True, steps=steps)
    acc = lambda n, r=1: pl.BlockSpec((r, n), lambda i: (0, 0))
    heads_t = lambda: pl.BlockSpec((H, HA, tm), lambda i: (0, 0, steps - 1 - i))
    return pl.pallas_call(
        body, name="attn_proj_bwd", grid=(steps,),
        in_specs=[pl.BlockSpec((H, tm, HA), lambda i: (0, steps - 1 - i, 0)), heads_t(), heads_t(),
                  rr(D), rr(D), rr(D), rr(LANES), rr(LANES), rr(LANES), rr(D), rr(D),
                  _const((1, D)), _const((1, D)), _const((1, D)), _const((4 * D, D)), _const((LANES, D)),
                  _const((D, LANES)), _const((LANES, D)), _const((tm, tm)), _const((D, LANES))],
        out_specs=[rr(D), rr(4 * D), rr(LANES), acc(D), acc(LANES, 8), acc(LANES, 8), acc(LANES)],
        out_shape=[jax.ShapeDtypeStruct((s, D), F32), jax.ShapeDtypeStruct((s, 4 * D), BF16),
                   jax.ShapeDtypeStruct((s, LANES), BF16), jax.ShapeDtypeStruct((1, D), F32),
                   jax.ShapeDtypeStruct((8, LANES), F32), jax.ShapeDtypeStruct((8, LANES), F32),
                   jax.ShapeDtypeStruct((1, LANES), F32)],
        scratch_shapes=[pltpu.VMEM((8, LANES), F32), pltpu.VMEM((tm, LANES), F32), pltpu.VMEM((1, D), F32),
                        pltpu.VMEM((1, D), F32), pltpu.VMEM((tm, D), F32), pltpu.VMEM((tm, D), F32),
                        pltpu.VMEM((tm, D), F32)],
        compiler_params=_params("arbitrary"),
    )(dq, dkt, dvt, dz, qpre, kpre, invq, invk, fl, x1, dx2, g2, qg, kg, w2in_t, wf_t, e, et, triu, fold)


def _fold_heads(col, fold):
    hi, mid, lo = _split3(jnp.broadcast_to(col, (8, D)))
    return _dot(hi, fold) + _dot(mid, fold) + _dot(lo, fold)


def _conv_bwd(dx1, p1, x, g1, cw, w1out_t, w1in_t):
    s = x.shape[0]
    tm = min(TM_BWD, s)
    steps = s // tm
    halo = tm // 8

    def body(dx1_ref, b_ref, c_ref, xi_ref, z_ref, ch_ref, xh_ref, x_ref, g_ref, cw_ref, wot_ref, wit_ref,
             gx_ref, dp_ref, dcw_ref, dg1_ref, head_ref):
        step = pl.program_id(0)

        @pl.when(step == 0)
        def _():
            head_ref[...] = jnp.zeros_like(head_ref)
            dcw_ref[...] = jnp.zeros_like(dcw_ref)
            dg1_ref[...] = jnp.zeros_like(dg1_ref)

        first_tile = step == steps - 1
        dx1 = dx1_ref[...]
        row = lax.broadcasted_iota(jnp.int32, (tm, CH), 0)
        dyb = dx1.astype(BF16)
        for ci in range(D // CH):
            lo, hi = ci * CH, (ci + 1) * CH
            dyg = _dot(dyb, wot_ref[:, lo:hi])
            b = b_ref[:, lo:hi].astype(F32)
            c = c_ref[:, lo:hi].astype(F32)
            xin = xi_ref[:, lo:hi].astype(F32)
            z = z_ref[:, lo:hi].astype(F32)
            u = c * xin
            t6 = jnp.where(first_tile, 0.0, ch_ref[6:7, lo:hi].astype(F32) * xh_ref[6:7, lo:hi].astype(F32))
            t7 = jnp.where(first_tile, 0.0, ch_ref[7:8, lo:hi].astype(F32) * xh_ref[7:8, lo:hi].astype(F32))
            u1 = jnp.where(row == 0, t7, pltpu.roll(u, 1, 0))
            u2 = jnp.where(row == 0, t6, jnp.where(row == 1, t7, pltpu.roll(u, 2, 0)))
            w0, w1, w2 = cw_ref[0:1, lo:hi], cw_ref[1:2, lo:hi], cw_ref[2:3, lo:hi]
            y = w2 * u + w1 * u1 + w0 * u2
            sg = _sigmoid(z)
            sil = z * sg
            dp_ref[:, lo:hi] = (dyg * y * sil).astype(BF16)
            dy = dyg * b * sil
            dp_ref[:, 3 * D + lo:3 * D + hi] = (dyg * b * y * (sg * (1.0 + z * (1.0 - sg)))).astype(BF16)
            dcw_ref[2:3, lo:hi] += jnp.sum(dy * u, axis=0, keepdims=True)
            dcw_ref[1:2, lo:hi] += jnp.sum(dy * u1, axis=0, keepdims=True)
            dcw_ref[0:1, lo:hi] += jnp.sum(dy * u2, axis=0, keepdims=True)
            n0 = head_ref[0:1, lo:hi]
            n1 = head_ref[1:2, lo:hi]
            dyn1 = jnp.where(row == tm - 1, n0, pltpu.roll(dy, tm - 1, 0))
            dyn2 = jnp.where(row == tm - 2, n0, jnp.where(row == tm - 1, n1, pltpu.roll(dy, tm - 2, 0)))
            head_ref[:, lo:hi] = dy[0:8, :]
            du = w2 * dy + w1 * dyn1 + w0 * dyn2
            dp_ref[:, D + lo:D + hi] = (du * xin).astype(BF16)
            dp_ref[:, 2 * D + lo:2 * D + hi] = (du * c).astype(BF16)
        dh = _dot(dp_ref[:, 0:D], wit_ref[0:D, :])
        for k in range(1, 4):
            dh += _dot(dp_ref[:, k * D:(k + 1) * D], wit_ref[k * D:(k + 1) * D, :])
        xv = x_ref[...]
        inv = lax.rsqrt(jnp.mean(xv * xv, axis=-1, keepdims=True) + RMS_EPS)
        dg1_ref[...] += jnp.sum(dh * xv * inv, axis=0, keepdims=True)
        gh = dh * g_ref[...]
        gx_ref[...] = dx1 + inv * gh - xv * (inv * inv * inv * jnp.mean(gh * xv, axis=-1, keepdims=True))

    rr = lambda n: _rows(tm, n, rev=True, steps=steps)
    part = lambda k: pl.BlockSpec((tm, D), lambda i: (steps - 1 - i, k))
    prev8 = lambda k: pl.BlockSpec((8, D), lambda i: (jnp.maximum((steps - 1 - i) * halo - 1, 0), k))
    return pl.pallas_call(
        body, name="conv_bwd", grid=(steps,),
        in_specs=[rr(D), part(0), part(1), part(2), part(3), prev8(1), prev8(2), rr(D), _const((1, D)),
                  _const((8, D)), _const((D, D)), _const((4 * D, D))],
        out_specs=[rr(D), rr(4 * D), pl.BlockSpec((8, D), lambda i: (0, 0)), pl.BlockSpec((1, D), lambda i: (0, 0))],
        out_shape=[jax.ShapeDtypeStruct((s, D), F32), jax.ShapeDtypeStruct((s, 4 * D), BF16),
                   jax.ShapeDtypeStruct((8, D), F32), jax.ShapeDtypeStruct((1, D), F32)],
        scratch_shapes=[pltpu.VMEM((8, D), F32)],
        compiler_params=_params("arbitrary"),
    )(dx1, p1, p1, p1, p1, p1, p1, x, g1, cw, w1out_t, w1in_t)


def _wgrad(a, g, name):
    s, k = a.shape
    n = g.shape[1]
    bn = min(n, 1024)
    ts = min(1024, s)

    def body(a_ref, g_ref, out_ref, acc_ref):
        t = pl.program_id(1)

        @pl.when(t == 0)
        def _():
            acc_ref[...] = jnp.zeros_like(acc_ref)

        acc_ref[...] += _dot_tn(a_ref[...], g_ref[...].astype(BF16))

        @pl.when(t == s // ts - 1)
        def _():
            out_ref[...] = acc_ref[...].astype(BF16)

    return pl.pallas_call(
        body, name=name, grid=(n // bn, s // ts),
        in_specs=[pl.BlockSpec((ts, k), lambda j, t: (t, 0)), pl.BlockSpec((ts, bn), lambda j, t: (t, j))],
        out_specs=pl.BlockSpec((k, bn), lambda j, t: (0, j)),
        out_shape=jax.ShapeDtypeStruct((k, n), BF16),
        scratch_shapes=[pltpu.VMEM((k, bn), F32)],
        compiler_params=_params("arbitrary", "arbitrary"),
    )(a, g)


def _adamw_math(w, g, m, v):
    m = ADAM_B1 * m + (1.0 - ADAM_B1) * g
    v = ADAM_B2 * v + (1.0 - ADAM_B2) * (g * g)
    m_hat = m / (1.0 - ADAM_B1 ** ADAM_STEP)
    v_hat = v / (1.0 - ADAM_B2 ** ADAM_STEP)
    delta = -ADAM_LR * (m_hat / (jnp.sqrt(v_hat) + ADAM_EPS) + ADAM_WD * w)
    return delta, m, v


def _adamw_big(parts, w, m, v):
    _, r, c_ = parts.shape
    rb = 368
    assert r % rb == 0

    def body(p_ref, w_ref, m_ref, v_ref, g_ref, d_ref, mo_ref, vo_ref):
        g = p_ref[0].astype(F32)
        for k in range(1, NDEV):
            g = g + p_ref[k].astype(F32)
        g_ref[...] = g
        d_ref[...], mo_ref[...], vo_ref[...] = _adamw_math(w_ref[...], g, m_ref[...], v_ref[...])

    blk = pl.BlockSpec((rb, c_), lambda i: (i, 0))
    return pl.pallas_call(
        body, name="adamw_big", grid=(r // rb,),
        in_specs=[pl.BlockSpec((NDEV, rb, c_), lambda i: (0, i, 0)), blk, blk, blk],
        out_specs=[blk, blk, blk, blk],
        out_shape=[jax.ShapeDtypeStruct((r, c_), F32)] * 4,
        compiler_params=_params("arbitrary"),
    )(parts, w, m, v)


def _sum_parts(parts, loss_row):
    _, r, c_ = parts.shape

    def body(p_ref, o_ref, l_ref):
        g = p_ref[0]
        for k in range(1, NDEV):
            g = g + p_ref[k]
        o_ref[...] = g
        per_row = jnp.sum(o_ref[loss_row:loss_row + 8, :], axis=1, keepdims=True)
        l_ref[...] = jnp.broadcast_to(jnp.sum(per_row, axis=0, keepdims=True), (8, c_))

    return pl.pallas_call(body, name="sum_small", out_shape=[jax.ShapeDtypeStruct((r, c_), F32),
                                                             jax.ShapeDtypeStruct((8, c_), F32)])(parts)


def _adamw_small(g, w, m, v):
    def body(g_ref, w_ref, m_ref, v_ref, d_ref, mo_ref, vo_ref):
        d_ref[...], mo_ref[...], vo_ref[...] = _adamw_math(w_ref[...], g_ref[...], m_ref[...], v_ref[...])

    return pl.pallas_call(body, name="adamw_small", out_shape=[jax.ShapeDtypeStruct(g.shape, F32)] * 3)(g, w, m, v)


def _pack_big(w1in, w2in, w1out, w2out, dtype):
    parts = [w1in.reshape(ROWS_W1IN, PACK_W), w2in.reshape(ROWS_W2IN, PACK_W), w1out.reshape(ROWS_WOUT, PACK_W),
             w2out.reshape(ROWS_WOUT, PACK_W)]
    used = ROWS_W1IN + ROWS_W2IN + 2 * ROWS_WOUT
    parts.append(jnp.zeros((PACK_ROWS - used, PACK_W), parts[0].dtype))
    return jnp.concatenate(parts, axis=0).astype(dtype)


def _unpack_big(p):
    a, b, c = ROWS_W1IN, ROWS_W1IN + ROWS_W2IN, ROWS_W1IN + ROWS_W2IN + ROWS_WOUT
    return (p[:a].reshape(1, D, 512), p[a:b].reshape(1, D, 514), p[b:c].reshape(1, 128, D),
            p[c:c + ROWS_WOUT].reshape(1, 128, D))


def _pad_lanes(a):
    return jnp.pad(a, ((0, 0), (0, LANES - a.shape[1])))


def _heads(a, s):
    return a.reshape(s, H, DH).transpose(1, 0, 2)


def _unheads(a, s):
    return a.transpose(1, 0, 2).reshape(s, D)


def _aug(cols, s):
    used = sum(c.shape[-1] for c in cols)
    return jnp.concatenate(cols + [jnp.zeros((H, s, HA - used), BF16)], axis=-1)


def _terms(v):
    return [t[..., None] for t in _split3(v)]


def _tiles(a, t):
    return a.reshape(H, a.shape[1] // t, t, HA)


def _tiles_t(a, t):
    return _tiles(a, t).transpose(0, 1, 3, 2)


def kernel(x, conv_norm_g, conv_w_in, conv_w, conv_w_out, attn_norm_g, attn_w_in, attn_b_f, attn_q_norm_g, attn_k_norm_g, attn_w_out, loss_target, m_conv_norm_g, m_conv_w_in, m_conv_w, m_conv_w_out, m_attn_norm_g, m_attn_w_in, m_attn_b_f, m_attn_q_norm_g, m_attn_k_norm_g, m_attn_w_out, v_conv_norm_g, v_conv_w_in, v_conv_w, v_conv_w_out, v_attn_norm_g, v_attn_w_in, v_attn_b_f, v_attn_q_norm_g, v_attn_k_norm_g, v_attn_w_out):
    s = x.shape[1]
    tq = min(TQ, s)
    tmf, tmb = min(TM_FWD, s), min(TM_BWD, s)
    me = 4 * lax.axis_index("x") + 2 * lax.axis_index("y") + lax.axis_index("c")
    xv, tgt = x[0], loss_target[0]

    wg = _all_gather(_pack_big(conv_w_in[0], attn_w_in[0], conv_w_out[0], attn_w_out[0], BF16), "gather_weights")
    small_w = jnp.concatenate([conv_w[0], attn_norm_g, jnp.zeros((4, 128), F32)], axis=0)
    sg_ = _all_gather(small_w, "gather_small_weights")
    a, b, c = ROWS_W1IN, ROWS_W1IN + ROWS_W2IN, ROWS_W1IN + ROWS_W2IN + ROWS_WOUT
    w1in = wg[:, :a].transpose(1, 0, 2).reshape(D, 4 * D)
    w2all = wg[:, a:b].reshape(NDEV, D, 514).transpose(1, 0, 2).reshape(D, 4 * D + H)
    w2in, wf = w2all[:, :4 * D], _pad_lanes(w2all[:, 4 * D:])
    w1out = wg[:, b:c].reshape(D, D)
    w2out = wg[:, c:c + ROWS_WOUT].reshape(D, D)
    cw = jnp.concatenate([sg_[:, 0:3, :].transpose(1, 0, 2).reshape(3, D), jnp.zeros((5, D), F32)], axis=0)
    g2 = sg_[:, 3, :].reshape(1, D)
    qg_t, kg_t = jnp.tile(attn_q_norm_g, (1, H)), jnp.tile(attn_k_norm_g, (1, H))
    bf = _pad_lanes(attn_b_f)

    e = (jnp.arange(D)[:, None] // DH == jnp.arange(LANES)[None, :]).astype(BF16)
    fold = (jnp.arange(D)[:, None] % DH == jnp.arange(LANES)[None, :]).astype(BF16)
    tril = (jnp.arange(tmb)[:, None] >= jnp.arange(tmb)[None, :]).astype(BF16)
    triu = tril.T
    src = jnp.arange(3 * LANES)
    dst = jnp.arange(H * HA)[None, :] - (HA * (src % LANES) + src // LANES)[:, None]
    place_k = ((dst == COL_BIAS) & (src % LANES < H)[:, None]).astype(BF16)
    place_q = ((dst == ROW_BIAS) & (src % LANES < H)[:, None]).astype(BF16)

    x1, h1, p1, yg = _conv_fwd(xv, conv_norm_g, w1in, cw, w1out)
    reach = 1.01 * math.sqrt(DH) * jnp.max(jnp.abs(attn_q_norm_g)) * jnp.max(jnp.abs(attn_k_norm_g))
    bounded = (2.0 * reach <= BOUNDED_SOFTMAX_REACH).astype(F32).reshape(1, 1)
    h2, qpre, kpre, z, qa, ka, va, invq, invk, cc, fl = _attn_proj_fwd(
        x1, g2, w2in, wf, bf, qg_t, kg_t, jnp.broadcast_to(reach, (1, LANES)), e, e.T, tril, place_k, place_q)
    ka, va = _tiles(ka, tq), _tiles(va, tq)
    ol, qab = _flash_fwd(bounded, _tiles(qa, tq), ka, va)
    og, dx2, dz, doa, lossp = _attn_out(ol.reshape(H, s, HA), z, x1, tgt, w2out, w2out.T, e, place_k)

    dq, dkt, dvt = _flash_bwd(qab, _tiles(doa, tq), ka, va)
    dx1, dp2, df, dg2, dqg, dkg, dbf = _attn_proj_bwd(
        dq.reshape(H, s, HA), dkt, dvt, dz, qpre, kpre, invq, invk, fl, x1, dx2, g2, qg_t, kg_t, w2in.T, wf.T, e, e.T, triu, fold)
    gx, dp1, dcw, dg1 = _conv_bwd(dx1, p1, xv, conv_norm_g, cw, w1out.T, w1in.T)
    dw2out = _wgrad(og, dx2, "wgrad_attn_out")
    dw2in = _wgrad(h2, dp2, "wgrad_attn_in")
    dwf = _wgrad(h2, df, "wgrad_attn_forget")
    dw1out = _wgrad(yg, dx1, "wgrad_conv_out")
    dw1in = _wgrad(h1, dp1, "wgrad_conv_in")

    dw2all = jnp.concatenate([dw2in, dwf[:, :H]], axis=1)
    gfull = jnp.concatenate([
        dw1in.reshape(D, NDEV, 512).transpose(1, 0, 2),
        dw2all.reshape(D, NDEV, 514).transpose(1, 0, 2).reshape(NDEV, ROWS_W2IN, PACK_W),
        dw1out.reshape(NDEV, ROWS_WOUT, PACK_W), dw2out.reshape(NDEV, ROWS_WOUT, PACK_W),
        jnp.zeros((NDEV, PACK_ROWS - c - ROWS_WOUT, PACK_W), BF16)], axis=1)
    parts = _exchange_blocks(gfull, "scatter_grads")
    pack = lambda t1, t2, t3, t4: _pack_big(t1[0], t2[0], t3[0], t4[0], F32)
    gb, db, mb, vb = _adamw_big(parts, pack(conv_w_in, attn_w_in, conv_w_out, attn_w_out),
                                pack(m_conv_w_in, m_attn_w_in, m_conv_w_out, m_attn_w_out),
                                pack(v_conv_w_in, v_attn_w_in, v_conv_w_out, v_attn_w_out))

    shard_rows = jnp.concatenate([dcw[0:3].reshape(3, NDEV, 128).transpose(1, 0, 2), dg2.reshape(NDEV, 1, 128),
                                  jnp.zeros((NDEV, 4, 128), F32)], axis=1).reshape(64, 128)
    small_g = jnp.concatenate([shard_rows, dg1.reshape(8, 128), dbf, dqg[0:1], dkg[0:1], jnp.zeros((5, 128), F32),
                               lossp.reshape(8, 128)], axis=0)
    gs, loss8 = _sum_parts(_all_gather(small_g, "gather_small_grads"), 80)
    loss = loss8[0, 0]
    mine = lax.dynamic_slice(gs, (8 * me, 0), (8, 128))
    g_small = jnp.concatenate([mine, gs[64:75], jnp.zeros((5, 128), F32)], axis=0)

    def pack_small(cwk, ang, cng, bfk, qgk, kgk):
        return jnp.concatenate([cwk[0], ang, jnp.zeros((4, 128), F32), cng.reshape(8, 128), _pad_lanes(bfk),
                                _pad_lanes(qgk), _pad_lanes(kgk), jnp.zeros((5, 128), F32)], axis=0)

    ds_, ms_, vs_ = _adamw_small(
        g_small, pack_small(conv_w, attn_norm_g, conv_norm_g, attn_b_f, attn_q_norm_g, attn_k_norm_g),
        pack_small(m_conv_w, m_attn_norm_g, m_conv_norm_g, m_attn_b_f, m_attn_q_norm_g, m_attn_k_norm_g),
        pack_small(v_conv_w, v_attn_norm_g, v_conv_norm_g, v_attn_b_f, v_attn_q_norm_g, v_attn_k_norm_g))

    def leaves(big, small):
        w1i, w2i, w1o, w2o = _unpack_big(big)
        return (small[8:16].reshape(1, D), w1i, small[0:3].reshape(1, 3, 128), w1o, small[3:4], w2i,
                small[16:17, :H], small[17:18, :DH], small[18:19, :DH], w2o)

    return (loss, gx[None], *leaves(gb, g_small), *leaves(db, ds_), *leaves(mb, ms_), *leaves(vb, vs_))
```

```python
import functools
import math

import jax
import jax.numpy as jnp
from jax import lax
from jax.experimental import pallas as pl
from jax.experimental.pallas import tpu as pltpu

F32 = jnp.float32
BF16 = jnp.bfloat16

D = 1024
H = 16
DH = 64
NDEV = 8
RMS_EPS = 1e-6
LANES = 128
HA = 128
TM_FWD = 512
TM_BWD = 256
TQ = 512
TK = 512
AUG_ROWS = 80
FWD_TILES = 4
BWD_TILES = 4
CH = 256
PACK_W = 512
PACK_ROWS = 2576
ROWS_W1IN, ROWS_W2IN, ROWS_WOUT = 1024, 1028, 256
ADAM_LR, ADAM_B1, ADAM_B2, ADAM_EPS, ADAM_WD, ADAM_STEP = 0.001, 0.9, 0.999, 1e-08, 0.01, 10
VMEM_LIMIT = 56 * 1024 * 1024
MASKED = -1e30
BOUNDED_SOFTMAX_REACH = 60.0
MESH = pl.DeviceIdType.MESH


def _params(*sem, vmem=VMEM_LIMIT):
    return pltpu.CompilerParams(dimension_semantics=sem or None, vmem_limit_bytes=vmem)


def _const(shape):
    nd = len(shape)
    return pl.BlockSpec(shape, lambda *_: (0,) * nd, pipeline_mode=pl.Buffered(1))


def _rows(tm, n, rev=False, steps=None):
    if rev:
        return pl.BlockSpec((tm, n), lambda i: (steps - 1 - i, 0))
    return pl.BlockSpec((tm, n), lambda i: (i, 0))


def _dot(a, b):
    return jnp.dot(a, b, preferred_element_type=F32)


def _top16(x):
    bits = lax.bitcast_convert_type(x, jnp.uint32) & jnp.uint32(0xFFFF0000)
    return lax.bitcast_convert_type(bits, F32)


def _split2(x):
    hi = _top16(x)
    return hi.astype(BF16), (x - hi).astype(BF16)


def _split3(x):
    hi = _top16(x)
    r = x - hi
    mid = _top16(r)
    return hi.astype(BF16), mid.astype(BF16), (r - mid).astype(BF16)


def _seg_sum(a, e):
    hi, lo = _split2(a)
    return _dot(hi, e) + _dot(lo, e)


def _seg_bcast(s, et):
    hi, lo = _split2(s)
    return _dot(hi, et) + _dot(lo, et)


def _tri_sum(t, v):
    hi, mid, lo = _split3(v)
    return _dot(t, hi) + _dot(t, mid) + _dot(t, lo)


def _sigmoid(z):
    return 1.0 / (1.0 + jnp.exp(-z))


def _place():
    return lax.axis_index("x"), lax.axis_index("y"), lax.axis_index("c")


def _all_gather(xb, name):
    r, c_ = xb.shape

    def body(x_ref, out_ref, send_sems, recv_sems, local_sem):
        x, y, c = _place()
        me, sibling = (x, y, c), (x, y, 1 - c)
        chips = [(1 - x, y), (x, 1 - y), (1 - x, 1 - y)]

        def slab(px, py, pc):
            return out_ref.at[4 * px + 2 * py + pc]

        def copy(k, block, to, src=None):
            return pltpu.make_async_remote_copy(
                src_ref=slab(*block) if src is None else src, dst_ref=slab(*block),
                send_sem=send_sems.at[k], recv_sem=recv_sems.at[k], device_id=to, device_id_type=MESH)

        mine = pltpu.make_async_copy(x_ref, slab(*me), local_sem)
        mine.start()
        first = [copy(0, me, sibling, src=x_ref)]
        first += [copy(1 + j, me, (*chip, c), src=x_ref) for j, chip in enumerate(chips)]
        for cp in first:
            cp.start()
        passed = [copy(4 + j, (*chip, c), sibling) for j, chip in enumerate(chips)]
        for j, chip in enumerate(chips):
            copy(1 + j, (*chip, c), me).wait_recv()
            passed[j].start()
        copy(0, sibling, me).wait_recv()
        for j, chip in enumerate(chips):
            copy(4 + j, (*chip, 1 - c), me).wait_recv()
        for cp in first + passed:
            cp.wait_send()
        mine.wait()

    return pl.pallas_call(
        body, name=name,
        out_shape=jax.ShapeDtypeStruct((NDEV, r, c_), xb.dtype),
        in_specs=[pl.BlockSpec(memory_space=pl.ANY)],
        out_specs=pl.BlockSpec(memory_space=pl.ANY),
        scratch_shapes=[pltpu.SemaphoreType.DMA((7,)), pltpu.SemaphoreType.DMA((7,)), pltpu.SemaphoreType.DMA],
    )(xb)


def _exchange_blocks(g, name):
    _, r, c_ = g.shape

    def body(g_ref, out_ref, send_sems, recv_sems, local_sem):
        x, y, c = _place()
        me = 4 * x + 2 * y + c
        mine = pltpu.make_async_copy(g_ref.at[me], out_ref.at[0], local_sem)
        mine.start()
        copies = []
        for k in range(1, NDEV):
            px = 1 - x if k & 4 else x
            py = 1 - y if k & 2 else y
            pc = 1 - c if k & 1 else c
            cp = pltpu.make_async_remote_copy(
                src_ref=g_ref.at[4 * px + 2 * py + pc], dst_ref=out_ref.at[k],
                send_sem=send_sems.at[k - 1], recv_sem=recv_sems.at[k - 1],
                device_id=(px, py, pc), device_id_type=MESH)
            cp.start()
            copies.append(cp)
        for cp in copies:
            cp.wait()
        mine.wait()

    return pl.pallas_call(
        body, name=name,
        out_shape=jax.ShapeDtypeStruct((NDEV, r, c_), g.dtype),
        in_specs=[pl.BlockSpec(memory_space=pl.ANY)],
        out_specs=pl.BlockSpec(memory_space=pl.ANY),
        scratch_shapes=[pltpu.SemaphoreType.DMA((7,)), pltpu.SemaphoreType.DMA((7,)), pltpu.SemaphoreType.DMA],
    )(g)


def _conv_fwd(x, g1, w1in, cw, w1out):
    s = x.shape[0]
    tm = min(TM_FWD, s)
    steps = s // tm

    def body(x_ref, g_ref, win_ref, cw_ref, wout_ref, x1_ref, h1_ref, p1_ref, yg_ref, tail_ref):
        @pl.when(pl.program_id(0) == 0)
        def _():
            tail_ref[...] = jnp.zeros_like(tail_ref)

        xv = x_ref[...]
        inv = lax.rsqrt(jnp.mean(xv * xv, axis=-1, keepdims=True) + RMS_EPS)
        h = (xv * inv * g_ref[...]).astype(BF16)
        h1_ref[...] = h
        row = lax.broadcasted_iota(jnp.int32, (tm, CH), 0)
        for ci in range(D // CH):
            lo, hi = ci * CH, (ci + 1) * CH
            parts = []
            for k in range(4):
                pk = _dot(h, win_ref[:, k * D + lo:k * D + hi]).astype(BF16)
                p1_ref[:, k * D + lo:k * D + hi] = pk
                parts.append(pk.astype(F32))
            b, c, xin, z = parts
            u = c * xin
            t6 = tail_ref[6:7, lo:hi]
            t7 = tail_ref[7:8, lo:hi]
            u1 = jnp.where(row == 0, t7, pltpu.roll(u, 1, 0))
            u2 = jnp.where(row == 0, t6, jnp.where(row == 1, t7, pltpu.roll(u, 2, 0)))
            tail_ref[:, lo:hi] = u[tm - 8:, :]
            y = cw_ref[2:3, lo:hi] * u + cw_ref[1:2, lo:hi] * u1 + cw_ref[0:1, lo:hi] * u2
            yg_ref[:, lo:hi] = (b * y * (z * _sigmoid(z))).astype(BF16)
        x1_ref[...] = xv + _dot(yg_ref[...], wout_ref[...])

    return pl.pallas_call(
        body, name="conv_fwd", grid=(steps,),
        in_specs=[_rows(tm, D), _const((1, D)), _const((D, 4 * D)), _const((8, D)), _const((D, D))],
        out_specs=[_rows(tm, D), _rows(tm, D), _rows(tm, 4 * D), _rows(tm, D)],
        out_shape=[jax.ShapeDtypeStruct((s, D), F32), jax.ShapeDtypeStruct((s, D), BF16),
                   jax.ShapeDtypeStruct((s, 4 * D), BF16), jax.ShapeDtypeStruct((s, D), BF16)],
        scratch_shapes=[pltpu.VMEM((8, D), F32)],
        compiler_params=_params("arbitrary"),
    )(x, g1, w1in, cw, w1out)


COL_BIAS = DH
ROW_BIAS = DH + 3


def _terms_cat(val):
    return jnp.concatenate(_split3(val), axis=1)


def _split_heads(x, aug, out_ref, lane, t_ref=None):
    for hp in range(H // 2):
        pair = x[:, hp * HA:(hp + 1) * HA]
        for k, feat in enumerate((pair, pltpu.roll(pair, DH, 1))):
            block = jnp.where(lane < DH, feat, aug(2 * hp + k))
            out_ref[2 * hp + k] = block.astype(BF16)
            if t_ref is not None:
                t_ref[2 * hp + k] = block.T[0:t_ref.shape[1], :].astype(BF16)


def _ones_at(lane, first):
    return jnp.where((lane >= first) & (lane < first + 3), 1.0, 0.0)


def _attn_proj_fwd(x1, g2, w2in, wf, bf, qg, kg, shift, e, et, tril, place_k, place_q, group):
    s = x1.shape[0]
    tm = min(TM_BWD, s)
    steps = s // tm
    per = group // tm

    def body(x_ref, g_ref, win_ref, wf_ref, bf_ref, qg_ref, kg_ref, sh_ref, e_ref, et_ref, tril_ref, pk_ref, pq_ref,
             h2_ref, qp_ref, kp_ref, z_ref, qa_ref, ka_ref, va_ref, vat_ref, iq_ref, ik_ref, c_ref, fl_ref, carry_ref):
        @pl.when(pl.program_id(0) == 0)
        def _():
            carry_ref[...] = jnp.zeros_like(carry_ref)

        xv = x_ref[...]
        inv = lax.rsqrt(jnp.mean(xv * xv, axis=-1, keepdims=True) + RMS_EPS)
        h = (xv * inv * g_ref[...]).astype(BF16)
        h2_ref[...] = h

        fl = _dot(h, wf_ref[...]) + bf_ref[...]
        fl_ref[...] = fl
        ex = jnp.exp(-jnp.abs(fl))
        up = 1.0 + ex
        log1p = jnp.where(up == 1.0, ex, jnp.log(up) * ex / (up - 1.0))
        lane = lax.broadcasted_iota(jnp.int32, (tm, LANES), 1)
        logf = jnp.where(lane < H, jnp.minimum(fl, 0.0) - log1p, 0.0)
        c_ref[...] = _tri_sum(tril_ref[...], logf) + carry_ref[0:1, :]
        carry_ref[0:1, :] = c_ref[tm - 1:tm, :]
        c = c_ref[...]

        def normed(col, pre_ref, inv_ref, gain_ref, scale):
            pre = _dot(h, win_ref[:, col * D:(col + 1) * D]).astype(BF16)
            pre_ref[...] = pre
            t = pre.astype(F32)
            invh = lax.rsqrt(_seg_sum(t * t, e_ref[...]) * (1.0 / DH) + RMS_EPS)
            inv_ref[...] = invh
            return t * _seg_bcast(invh, et_ref[...]) * (gain_ref[...] * scale)

        ones_col, ones_row = _ones_at(lane, COL_BIAS), _ones_at(lane, ROW_BIAS)
        q_bias = _dot(_terms_cat(c - sh_ref[...]), pq_ref[...])
        _split_heads(normed(0, qp_ref, iq_ref, qg_ref, 1.0 / math.sqrt(DH)),
                     lambda hh: q_bias[:, hh * HA:(hh + 1) * HA] + ones_col, qa_ref, lane)
        k_bias = _dot(_terms_cat(-c), pk_ref[...])
        _split_heads(normed(1, kp_ref, ik_ref, kg_ref, 1.0),
                     lambda hh: k_bias[:, hh * HA:(hh + 1) * HA] + ones_row, ka_ref, lane)
        v = _dot(h, win_ref[:, 2 * D:3 * D])
        _split_heads(v, lambda hh: ones_col, va_ref, lane, vat_ref)
        z_ref[...] = _dot(h, win_ref[:, 3 * D:4 * D]).astype(BF16)

    row_bf = lambda: _rows(tm, D)
    row_sm = lambda: _rows(tm, LANES)
    heads = lambda: pl.BlockSpec((H, tm, HA), lambda i: (0, i, 0))
    return pl.pallas_call(
        body, name="attn_proj_fwd", grid=(steps,),
        in_specs=[_rows(tm, D), _const((1, D)), _const((D, 4 * D)), _const((D, LANES)), _const((1, LANES)),
                  _const((1, D)), _const((1, D)), _const((1, LANES)), _const((D, LANES)), _const((LANES, D)),
                  _const((tm, tm)), _const((3 * LANES, H * HA)), _const((3 * LANES, H * HA))],
        out_specs=[row_bf() for _ in range(4)] + [heads() for _ in range(3)] + [
            pl.BlockSpec((H, None, AUG_ROWS, tm), lambda i: (0, i // per, 0, i % per))] + [row_sm() for _ in range(4)],
        out_shape=[jax.ShapeDtypeStruct((s, D), BF16)] * 4 + [jax.ShapeDtypeStruct((H, s, HA), BF16)] * 3 + [
            jax.ShapeDtypeStruct((H, s // group, AUG_ROWS, group), BF16)] + [jax.ShapeDtypeStruct((s, LANES), F32)] * 4,
        scratch_shapes=[pltpu.VMEM((8, LANES), F32)],
        compiler_params=_params("arbitrary"),
    )(x1, g2, w2in, wf, bf, qg, kg, shift, e, et, tril, place_k, place_q)


def _dot_nt(a, b):
    return lax.dot_general(a, b, (((1,), (1,)), ((), ())), preferred_element_type=F32)


def _dot_tn(a, b):
    return lax.dot_general(a, b, (((0,), (0,)), ((), ())), preferred_element_type=F32)


def _flash_fwd(bounded, qa, ka, vat):
    h_, nq, tq, _ = qa.shape
    s, tk = ka.shape[1], tq
    group = vat.shape[3]
    gt = group // tk
    shift = gt.bit_length() - 1
    bgt = min(BWD_TILES, nq)

    def body(flag_ref, q_ref, k_ref, vt_ref, o_ref, qb_ref, qbt_ref, m_ref, acct_ref):
        i = pl.program_id(1)
        lane = lax.broadcasted_iota(jnp.int32, (tq, HA), 1)
        in_bias = (lane >= ROW_BIAS) & (lane < ROW_BIAS + 3)
        acct_ref[...] = jnp.zeros_like(acct_ref)
        g = lax.shift_right_logical(i, shift)
        r = i & (gt - 1)

        def run(use_bound):
            q = q_ref[...]
            if not use_bound:
                q = jnp.where(in_bias, jnp.zeros_like(q), q)
                m_ref[...] = jnp.full_like(m_ref, MASKED)

            def step(first_key, vt, tiles, diagonal_at):
                keys = pl.ds(pl.multiple_of(first_key, tk), tiles * tk)
                zt = _dot_nt(k_ref[keys, :], q)
                if diagonal_at is not None:
                    key = lax.broadcasted_iota(jnp.int32, (tiles * tk, tq), 0)
                    qry = lax.broadcasted_iota(jnp.int32, (tiles * tk, tq), 1)
                    zt = jnp.where(key <= qry + diagonal_at * tk, zt, MASKED)
                if use_bound:
                    acct_ref[0:AUG_ROWS, :] += _dot(vt, jnp.exp(zt).astype(BF16))
                else:
                    m_old = m_ref[...]
                    m_new = jnp.maximum(m_old, jnp.max(zt, axis=0, keepdims=True))
                    pt = jnp.exp(zt - m_new)
                    acct_ref[0:AUG_ROWS, :] = jnp.exp(m_old - m_new) * acct_ref[0:AUG_ROWS, :] + _dot(
                        vt, pt.astype(BF16))
                    m_ref[...] = m_new

            def whole_group(jj, carry):
                step(jj * group, vt_ref[jj], gt, None)
                return carry

            lax.fori_loop(0, g, whole_group, 0)
            for rr in range(gt):
                pl.when(r == rr)(functools.partial(
                    lambda rr: step(g * group, vt_ref[g, :, 0:(rr + 1) * tk], rr + 1, rr), rr))

            if not use_bound:
                acct_ref[COL_BIAS + 1:COL_BIAS + 2, :] = m_ref[...]
            acc = acct_ref[...].T
            l = jnp.sum(jnp.where(lane == COL_BIAS, acc, 0.0), axis=1, keepdims=True)
            if use_bound:
                m = -jnp.sum(jnp.where(in_bias, q.astype(F32), 0.0), axis=1, keepdims=True)
            else:
                m = jnp.sum(jnp.where(lane == COL_BIAS + 1, acc, 0.0), axis=1, keepdims=True)
            lse = m + jnp.log(l)
            o_ref[...] = jnp.where(lane < DH, acc / l, lse)
            hi, mid, lo = _split3(-lse)
            qb = jnp.where(lane == ROW_BIAS, hi, jnp.where(lane == ROW_BIAS + 1, mid, jnp.where(
                lane == ROW_BIAS + 2, lo, q_ref[...])))
            qb_ref[...] = qb
            qbt_ref[...] = qb.astype(F32).T[0:AUG_ROWS, :].astype(BF16)

        use_bound = flag_ref[0, 0] > 0.5
        pl.when(use_bound)(lambda: run(True))
        pl.when(jnp.logical_not(use_bound))(lambda: run(False))

    tile = lambda: pl.BlockSpec((None, None, tq, HA), lambda h, i: (h, i, 0, 0))
    return pl.pallas_call(
        body, name="flash_fwd", grid=(h_, nq),
        in_specs=[pl.BlockSpec(memory_space=pltpu.SMEM), tile(), pl.BlockSpec((None, s, HA), lambda h, i: (h, 0, 0)),
                  pl.BlockSpec((None, s // group, AUG_ROWS, group), lambda h, i: (h, 0, 0, 0))],
        out_specs=[tile(), tile(), pl.BlockSpec((None, None, AUG_ROWS, tq), lambda h, i: (h, i // bgt, 0, i % bgt))],
        out_shape=[jax.ShapeDtypeStruct((h_, nq, tq, HA), F32), jax.ShapeDtypeStruct((h_, nq, tq, HA), BF16),
                   jax.ShapeDtypeStruct((h_, nq // bgt, AUG_ROWS, bgt * tq), BF16)],
        scratch_shapes=[pltpu.VMEM((1, tq), F32), pltpu.VMEM((HA, tq), F32)],
        compiler_params=_params("arbitrary", "arbitrary"),
    )(bounded, qa, ka, vat)


def _merge_heads(src, out_ref, lane):
    for hp in range(H // 2):
        out_ref[:, hp * HA:(hp + 1) * HA] = jnp.where(lane < DH, src(2 * hp), pltpu.roll(src(2 * hp + 1), DH, 1))


def _attn_out(ol, z, x1, tgt, w2out, w2out_t, e, place_k, group):
    s = z.shape[0]
    tm = min(TM_BWD, s)
    steps = s // tm
    per = group // tm

    def body(ol_ref, z_ref, x1_ref, t_ref, w_ref, wt_ref, e_ref, pk_ref, og_ref, dx2_ref, dz_ref, doa_ref, doat_ref,
             loss_ref, o_ref):
        @pl.when(pl.program_id(0) == 0)
        def _():
            loss_ref[...] = jnp.zeros_like(loss_ref)

        lane = lax.broadcasted_iota(jnp.int32, (tm, HA), 1)
        _merge_heads(lambda hh: ol_ref[hh], o_ref, lane)
        ov = o_ref[...]
        zv = z_ref[...].astype(F32)
        sg = _sigmoid(zv)
        sil = zv * sg
        og = (ov * sil).astype(BF16)
        og_ref[...] = og
        err = x1_ref[...] + _dot(og, w_ref[...]) - t_ref[...]
        loss_ref[...] += (0.5 / D) * jnp.sum(err * err, axis=0, keepdims=True)
        dx2 = err * (1.0 / D)
        dx2_ref[...] = dx2
        dog = _dot(dx2.astype(BF16), wt_ref[...])
        do = (dog * sil).astype(BF16).astype(F32)
        dz_ref[...] = (dog * ov * (sg * (1.0 + zv * (1.0 - sg)))).astype(BF16)
        delta = _seg_sum(do * ov, e_ref[...])
        d_bias = _dot(_terms_cat(-delta), pk_ref[...])
        _split_heads(do, lambda hh: d_bias[:, hh * HA:(hh + 1) * HA], doa_ref, lane, doat_ref)

    heads = lambda: pl.BlockSpec((H, tm, HA), lambda i: (0, i, 0))
    return pl.pallas_call(
        body, name="attn_out", grid=(steps,),
        in_specs=[heads(), _rows(tm, D), _rows(tm, D), _rows(tm, D), _const((D, D)), _const((D, D)),
                  _const((D, LANES)), _const((3 * LANES, H * HA))],
        out_specs=[_rows(tm, D), _rows(tm, D), _rows(tm, D), heads(),
                   pl.BlockSpec((H, None, AUG_ROWS, tm), lambda i: (0, i // per, 0, i % per)),
                   pl.BlockSpec((1, D), lambda i: (0, 0))],
        out_shape=[jax.ShapeDtypeStruct((s, D), BF16), jax.ShapeDtypeStruct((s, D), F32),
                   jax.ShapeDtypeStruct((s, D), BF16), jax.ShapeDtypeStruct((H, s, HA), BF16),
                   jax.ShapeDtypeStruct((H, s // group, AUG_ROWS, group), BF16), jax.ShapeDtypeStruct((1, D), F32)],
        scratch_shapes=[pltpu.VMEM((tm, D), F32)],
        compiler_params=_params("arbitrary"),
    )(ol, z, x1, tgt, w2out, w2out_t, e, place_k)


def _flash_bwd(qab, doa, qabt, doat, ka, va):
    h_, nq, tq, _ = qab.shape
    nk, tk = ka.shape[1], ka.shape[2]
    s = nk * tk
    gt = min(BWD_TILES, nq)
    group, groups, shift = gt * tq, nq // gt, gt.bit_length() - 1

    def body(qa_ref, da_ref, qat_ref, dat_ref, ka_ref, va_ref, dqt_hbm, dkt_ref, dvt_ref, dqt_acc, sem):
        hh = pl.program_id(0)
        j = pl.program_id(1)

        @pl.when(j == 0)
        def _():
            dqt_acc[...] = jnp.zeros_like(dqt_acc)

        dkt_ref[...] = jnp.zeros_like(dkt_ref)
        dvt_ref[...] = jnp.zeros_like(dvt_ref)
        ka_v = ka_ref[...]
        va_v = va_ref[...]
        kat = ka_v.astype(F32).T[0:AUG_ROWS, :].astype(BF16)
        g0 = lax.shift_right_logical(j, shift)
        r = j & (gt - 1)

        def step(first_row, tiles, masked, qat, dat, add_dqt):
            rows = pl.ds(pl.multiple_of(first_row, tq), tiles * tq)
            zz = _dot_nt(qa_ref[rows, :], ka_v)
            if masked:
                row = lax.broadcasted_iota(jnp.int32, (tiles * tq, tk), 0)
                col = lax.broadcasted_iota(jnp.int32, (tiles * tq, tk), 1)
                zz = jnp.where(col <= row, zz, MASKED)
            p = jnp.exp(zz)
            ds = (p * _dot_nt(da_ref[rows, :], va_v)).astype(BF16)
            add_dqt(_dot_nt(kat, ds))
            dvt_ref[0:AUG_ROWS, :] += _dot(dat, p.astype(BF16))
            dkt_ref[0:AUG_ROWS, :] += _dot(qat, ds)

        for rr in range(gt):
            def from_diagonal(rr=rr):
                def add(x):
                    dqt_acc[g0, 0:AUG_ROWS, rr * tq:group] += x

                step(j * tq, gt - rr, True, qat_ref[g0, :, rr * tq:group], dat_ref[g0, :, rr * tq:group], add)

            pl.when(r == rr)(from_diagonal)

        def whole_group(g, carry):
            def add(x):
                dqt_acc[g, 0:AUG_ROWS, :] += x

            step(g * group, gt, False, qat_ref[g], dat_ref[g], add)
            return carry

        lax.fori_loop(g0 + 1, groups, whole_group, 0)

        @pl.when(j == nk - 1)
        def _():
            cp = pltpu.make_async_copy(dqt_acc, dqt_hbm.at[hh], sem)
            cp.start()
            cp.wait()

    whole = lambda: pl.BlockSpec((None, s, HA), lambda h, j: (h, 0, 0))
    whole_t = lambda: pl.BlockSpec((None, groups, AUG_ROWS, group), lambda h, j: (h, 0, 0, 0))
    tile = lambda: pl.BlockSpec((None, None, tk, HA), lambda h, j: (h, j, 0, 0))
    return pl.pallas_call(
        body, name="flash_bwd", grid=(h_, nk),
        in_specs=[whole(), whole(), whole_t(), whole_t(), tile(), tile()],
        out_specs=[pl.BlockSpec(memory_space=pl.ANY),
                   pl.BlockSpec((None, HA, tk), lambda h, j: (h, 0, j)),
                   pl.BlockSpec((None, HA, tk), lambda h, j: (h, 0, j))],
        out_shape=[jax.ShapeDtypeStruct((h_, groups, HA, group), F32), jax.ShapeDtypeStruct((h_, HA, s), F32),
                   jax.ShapeDtypeStruct((h_, HA, s), F32)],
        scratch_shapes=[pltpu.VMEM((groups, HA, group), F32), pltpu.SemaphoreType.DMA],
        compiler_params=_params("arbitrary", "arbitrary"),
    )(qab.reshape(h_, s, HA), doa.reshape(h_, s, HA), qabt, doat, ka, va)


def _attn_proj_bwd(dq, dkt, dvt, dz, qpre, kpre, invq, invk, fl, x1, dx2, g2, qg, kg, w2in_t, wf_t, e, et,
                   triu, fold):
    s = x1.shape[0]
    tm = min(TM_BWD, s)
    steps = s // tm
    per = dq.shape[3] // tm

    def body(dq_ref, dkt_ref, dvt_ref, dz_ref, qp_ref, kp_ref, iq_ref, ik_ref, fl_ref, x1_ref, dx2_ref,
             g2_ref, qg_ref, kg_ref, wt_ref, wft_ref, e_ref, et_ref, triu_ref, fold_ref,
             dx1_ref, dp_ref, df_ref, dg2_ref, dqg_ref, dkg_ref, dbf_ref, carry_ref, rc_ref, qcol_ref, kcol_ref,
             dqs_ref, dkn_ref, dv_ref):
        step = pl.program_id(0)
        lane = lax.broadcasted_iota(jnp.int32, (tm, HA), 1)
        _merge_heads(lambda hh: dvt_ref[hh].T, dv_ref, lane)
        dc = jnp.zeros((tm, HA), F32)
        for hp in range(H // 2):
            dqh = [dq_ref[2 * hp].T, dq_ref[2 * hp + 1].T]
            dkh = [dkt_ref[2 * hp].T, dkt_ref[2 * hp + 1].T]
            dqs_ref[:, hp * HA:(hp + 1) * HA] = jnp.where(lane < DH, dqh[0], pltpu.roll(dqh[1], DH, 1))
            dkn_ref[:, hp * HA:(hp + 1) * HA] = jnp.where(lane < DH, dkh[0], pltpu.roll(dkh[1], DH, 1))
            for k in range(2):
                sums = jnp.where(lane == ROW_BIAS, dqh[k], 0.0) - pltpu.roll(
                    jnp.where(lane == COL_BIAS, dkh[k], 0.0), ROW_BIAS - COL_BIAS, 1)
                dc = dc + pltpu.roll(sums, (2 * hp + k - ROW_BIAS) % HA, 1)

        @pl.when(step == 0)
        def _():
            carry_ref[...] = jnp.zeros_like(carry_ref)
            dg2_ref[...] = jnp.zeros_like(dg2_ref)
            dbf_ref[...] = jnp.zeros_like(dbf_ref)
            qcol_ref[...] = jnp.zeros_like(qcol_ref)
            kcol_ref[...] = jnp.zeros_like(kcol_ref)
            dqg_ref[...] = jnp.zeros_like(dqg_ref)
            dkg_ref[...] = jnp.zeros_like(dkg_ref)

        rc_ref[...] = _tri_sum(triu_ref[...], dc) + carry_ref[0:1, :]
        carry_ref[0:1, :] = rc_ref[0:1, :]
        df = rc_ref[...] * _sigmoid(-fl_ref[...])
        dfb = df.astype(BF16)
        df_ref[...] = dfb
        dbf_ref[...] += jnp.sum(df, axis=0, keepdims=True)

        def norm_bwd(dn, pre_ref, inv_ref, gain_ref, col_ref):
            pre = pre_ref[...].astype(F32)
            invh = inv_ref[...]
            invb = _seg_bcast(invh, et_ref[...])
            col_ref[...] += jnp.sum(dn * pre * invb, axis=0, keepdims=True)
            gd = dn * gain_ref[...]
            mean = _seg_sum(gd * pre, e_ref[...]) * (1.0 / DH)
            return invb * gd - pre * _seg_bcast(mean * invh * invh * invh, et_ref[...])

        dq = norm_bwd(dqs_ref[...] * (1.0 / math.sqrt(DH)), qp_ref, iq_ref, qg_ref, qcol_ref).astype(BF16)
        dp_ref[:, 0:D] = dq
        dh = _dot(dq, wt_ref[0:D, :])
        dk = norm_bwd(dkn_ref[...], kp_ref, ik_ref, kg_ref, kcol_ref).astype(BF16)
        dp_ref[:, D:2 * D] = dk
        dh += _dot(dk, wt_ref[D:2 * D, :])
        dvb = dv_ref[...].astype(BF16)
        dp_ref[:, 2 * D:3 * D] = dvb
        dh += _dot(dvb, wt_ref[2 * D:3 * D, :])
        dzb = dz_ref[...]
        dp_ref[:, 3 * D:4 * D] = dzb
        dh += _dot(dzb, wt_ref[3 * D:4 * D, :])
        dh += _dot(dfb, wft_ref[...])

        xv = x1_ref[...]
        inv = lax.rsqrt(jnp.mean(xv * xv, axis=-1, keepdims=True) + RMS_EPS)
        dg2_ref[...] += jnp.sum(dh * xv * inv, axis=0, keepdims=True)
        gh = dh * g2_ref[...]
        dx1_ref[...] = dx2_ref[...] + inv * gh - xv * (inv * inv * inv * jnp.mean(gh * xv, axis=-1, keepdims=True))

        @pl.when(step == steps - 1)
        def _():
            dqg_ref[...] = _fold_heads(qcol_ref[...], fold_ref[...])
            dkg_ref[...] = _fold_heads(kcol_ref[...], fold_ref[...])

    rr = lambda n: _rows(tm, n, rev=True, steps=steps)
    acc = lambda n, r=1: pl.BlockSpec((r, n), lambda i: (0, 0))
    heads_t = lambda: pl.BlockSpec((H, HA, tm), lambda i: (0, 0, steps - 1 - i))
    return pl.pallas_call(
        body, name="attn_proj_bwd", grid=(steps,),
        in_specs=[pl.BlockSpec((H, None, HA, tm), lambda i: (0, (steps - 1 - i) // per, 0, (steps - 1 - i) % per)),
                  heads_t(), heads_t(),
                  rr(D), rr(D), rr(D), rr(LANES), rr(LANES), rr(LANES), rr(D), rr(D),
                  _const((1, D)), _const((1, D)), _const((1, D)), _const((4 * D, D)), _const((LANES, D)),
                  _const((D, LANES)), _const((LANES, D)), _const((tm, tm)), _const((D, LANES))],
        out_specs=[rr(D), rr(4 * D), rr(LANES), acc(D), acc(LANES, 8), acc(LANES, 8), acc(LANES)],
        out_shape=[jax.ShapeDtypeStruct((s, D), F32), jax.ShapeDtypeStruct((s, 4 * D), BF16),
                   jax.ShapeDtypeStruct((s, LANES), BF16), jax.ShapeDtypeStruct((1, D), F32),
                   jax.ShapeDtypeStruct((8, LANES), F32), jax.ShapeDtypeStruct((8, LANES), F32),
                   jax.ShapeDtypeStruct((1, LANES), F32)],
        scratch_shapes=[pltpu.VMEM((8, LANES), F32), pltpu.VMEM((tm, LANES), F32), pltpu.VMEM((1, D), F32),
                        pltpu.VMEM((1, D), F32), pltpu.VMEM((tm, D), F32), pltpu.VMEM((tm, D), F32),
                        pltpu.VMEM((tm, D), F32)],
        compiler_params=_params("arbitrary"),
    )(dq, dkt, dvt, dz, qpre, kpre, invq, invk, fl, x1, dx2, g2, qg, kg, w2in_t, wf_t, e, et, triu, fold)


def _fold_heads(col, fold):
    hi, mid, lo = _split3(jnp.broadcast_to(col, (8, D)))
    return _dot(hi, fold) + _dot(mid, fold) + _dot(lo, fold)


def _conv_bwd(dx1, p1, x, g1, cw, w1out_t, w1in_t):
    s = x.shape[0]
    tm = min(TM_BWD, s)
    steps = s // tm
    halo = tm // 8

    def body(dx1_ref, b_ref, c_ref, xi_ref, z_ref, ch_ref, xh_ref, x_ref, g_ref, cw_ref, wot_ref, wit_ref,
             gx_ref, dp_ref, dcw_ref, dg1_ref, head_ref):
        step = pl.program_id(0)

        @pl.when(step == 0)
        def _():
            head_ref[...] = jnp.zeros_like(head_ref)
            dcw_ref[...] = jnp.zeros_like(dcw_ref)
            dg1_ref[...] = jnp.zeros_like(dg1_ref)

        first_tile = step == steps - 1
        dx1 = dx1_ref[...]
        row = lax.broadcasted_iota(jnp.int32, (tm, CH), 0)
        dyb = dx1.astype(BF16)
        for ci in range(D // CH):
            lo, hi = ci * CH, (ci + 1) * CH
            dyg = _dot(dyb, wot_ref[:, lo:hi])
            b = b_ref[:, lo:hi].astype(F32)
            c = c_ref[:, lo:hi].astype(F32)
            xin = xi_ref[:, lo:hi].astype(F32)
            z = z_ref[:, lo:hi].astype(F32)
            u = c * xin
            t6 = jnp.where(first_tile, 0.0, ch_ref[6:7, lo:hi].astype(F32) * xh_ref[6:7, lo:hi].astype(F32))
            t7 = jnp.where(first_tile, 0.0, ch_ref[7:8, lo:hi].astype(F32) * xh_ref[7:8, lo:hi].astype(F32))
            u1 = jnp.where(row == 0, t7, pltpu.roll(u, 1, 0))
            u2 = jnp.where(row == 0, t6, jnp.where(row == 1, t7, pltpu.roll(u, 2, 0)))
            w0, w1, w2 = cw_ref[0:1, lo:hi], cw_ref[1:2, lo:hi], cw_ref[2:3, lo:hi]
            y = w2 * u + w1 * u1 + w0 * u2
            sg = _sigmoid(z)
            sil = z * sg
            dp_ref[:, lo:hi] = (dyg * y * sil).astype(BF16)
            dy = dyg * b * sil
            dp_ref[:, 3 * D + lo:3 * D + hi] = (dyg * b * y * (sg * (1.0 + z * (1.0 - sg)))).astype(BF16)
            dcw_ref[2:3, lo:hi] += jnp.sum(dy * u, axis=0, keepdims=True)
            dcw_ref[1:2, lo:hi] += jnp.sum(dy * u1, axis=0, keepdims=True)
            dcw_ref[0:1, lo:hi] += jnp.sum(dy * u2, axis=0, keepdims=True)
            n0 = head_ref[0:1, lo:hi]
            n1 = head_ref[1:2, lo:hi]
            dyn1 = jnp.where(row == tm - 1, n0, pltpu.roll(dy, tm - 1, 0))
            dyn2 = jnp.where(row == tm - 2, n0, jnp.where(row == tm - 1, n1, pltpu.roll(dy, tm - 2, 0)))
            head_ref[:, lo:hi] = dy[0:8, :]
            du = w2 * dy + w1 * dyn1 + w0 * dyn2
            dp_ref[:, D + lo:D + hi] = (du * xin).astype(BF16)
            dp_ref[:, 2 * D + lo:2 * D + hi] = (du * c).astype(BF16)
        dh = _dot(dp_ref[:, 0:D], wit_ref[0:D, :])
        for k in range(1, 4):
            dh += _dot(dp_ref[:, k * D:(k + 1) * D], wit_ref[k * D:(k + 1) * D, :])
        xv = x_ref[...]
        inv = lax.rsqrt(jnp.mean(xv * xv, axis=-1, keepdims=True) + RMS_EPS)
        dg1_ref[...] += jnp.sum(dh * xv * inv, axis=0, keepdims=True)
        gh = dh * g_ref[...]
        gx_ref[...] = dx1 + inv * gh - xv * (inv * inv * inv * jnp.mean(gh * xv, axis=-1, keepdims=True))

    rr = lambda n: _rows(tm, n, rev=True, steps=steps)
    part = lambda k: pl.BlockSpec((tm, D), lambda i: (steps - 1 - i, k))
    prev8 = lambda k: pl.BlockSpec((8, D), lambda i: (jnp.maximum((steps - 1 - i) * halo - 1, 0), k))
    return pl.pallas_call(
        body, name="conv_bwd", grid=(steps,),
        in_specs=[rr(D), part(0), part(1), part(2), part(3), prev8(1), prev8(2), rr(D), _const((1, D)),
                  _const((8, D)), _const((D, D)), _const((4 * D, D))],
        out_specs=[rr(D), rr(4 * D), pl.BlockSpec((8, D), lambda i: (0, 0)), pl.BlockSpec((1, D), lambda i: (0, 0))],
        out_shape=[jax.ShapeDtypeStruct((s, D), F32), jax.ShapeDtypeStruct((s, 4 * D), BF16),
                   jax.ShapeDtypeStruct((8, D), F32), jax.ShapeDtypeStruct((1, D), F32)],
        scratch_shapes=[pltpu.VMEM((8, D), F32)],
        compiler_params=_params("arbitrary"),
    )(dx1, p1, p1, p1, p1, p1, p1, x, g1, cw, w1out_t, w1in_t)


def _wgrad(a, g, name):
    s, k = a.shape
    n = g.shape[1]
    bn = min(n, 1024)
    ts = min(1024, s)

    def body(a_ref, g_ref, out_ref, acc_ref):
        t = pl.program_id(1)

        @pl.when(t == 0)
        def _():
            acc_ref[...] = jnp.zeros_like(acc_ref)

        acc_ref[...] += _dot_tn(a_ref[...], g_ref[...].astype(BF16))

        @pl.when(t == s // ts - 1)
        def _():
            out_ref[...] = acc_ref[...].astype(BF16)

    return pl.pallas_call(
        body, name=name, grid=(n // bn, s // ts),
        in_specs=[pl.BlockSpec((ts, k), lambda j, t: (t, 0)), pl.BlockSpec((ts, bn), lambda j, t: (t, j))],
        out_specs=pl.BlockSpec((k, bn), lambda j, t: (0, j)),
        out_shape=jax.ShapeDtypeStruct((k, n), BF16),
        scratch_shapes=[pltpu.VMEM((k, bn), F32)],
        compiler_params=_params("arbitrary", "arbitrary"),
    )(a, g)


def _adamw_math(w, g, m, v):
    m = ADAM_B1 * m + (1.0 - ADAM_B1) * g
    v = ADAM_B2 * v + (1.0 - ADAM_B2) * (g * g)
    m_hat = m / (1.0 - ADAM_B1 ** ADAM_STEP)
    v_hat = v / (1.0 - ADAM_B2 ** ADAM_STEP)
    delta = -ADAM_LR * (m_hat / (jnp.sqrt(v_hat) + ADAM_EPS) + ADAM_WD * w)
    return delta, m, v


def _adamw_big(parts, w, m, v):
    _, r, c_ = parts.shape
    rb = 368
    assert r % rb == 0

    def body(p_ref, w_ref, m_ref, v_ref, g_ref, d_ref, mo_ref, vo_ref):
        g = p_ref[0].astype(F32)
        for k in range(1, NDEV):
            g = g + p_ref[k].astype(F32)
        g_ref[...] = g
        d_ref[...], mo_ref[...], vo_ref[...] = _adamw_math(w_ref[...], g, m_ref[...], v_ref[...])

    blk = pl.BlockSpec((rb, c_), lambda i: (i, 0))
    return pl.pallas_call(
        body, name="adamw_big", grid=(r // rb,),
        in_specs=[pl.BlockSpec((NDEV, rb, c_), lambda i: (0, i, 0)), blk, blk, blk],
        out_specs=[blk, blk, blk, blk],
        out_shape=[jax.ShapeDtypeStruct((r, c_), F32)] * 4,
        compiler_params=_params("arbitrary"),
    )(parts, w, m, v)


def _sum_parts(parts, loss_row):
    _, r, c_ = parts.shape

    def body(p_ref, o_ref, l_ref):
        g = p_ref[0]
        for k in range(1, NDEV):
            g = g + p_ref[k]
        o_ref[...] = g
        per_row = jnp.sum(o_ref[loss_row:loss_row + 8, :], axis=1, keepdims=True)
        l_ref[...] = jnp.broadcast_to(jnp.sum(per_row, axis=0, keepdims=True), (8, c_))

    return pl.pallas_call(body, name="sum_small", out_shape=[jax.ShapeDtypeStruct((r, c_), F32),
                                                             jax.ShapeDtypeStruct((8, c_), F32)])(parts)


def _adamw_small(g, w, m, v):
    def body(g_ref, w_ref, m_ref, v_ref, d_ref, mo_ref, vo_ref):
        d_ref[...], mo_ref[...], vo_ref[...] = _adamw_math(w_ref[...], g_ref[...], m_ref[...], v_ref[...])

    return pl.pallas_call(body, name="adamw_small", out_shape=[jax.ShapeDtypeStruct(g.shape, F32)] * 3)(g, w, m, v)


def _pack_big(w1in, w2in, w1out, w2out, dtype):
    parts = [w1in.reshape(ROWS_W1IN, PACK_W), w2in.reshape(ROWS_W2IN, PACK_W), w1out.reshape(ROWS_WOUT, PACK_W),
             w2out.reshape(ROWS_WOUT, PACK_W)]
    used = ROWS_W1IN + ROWS_W2IN + 2 * ROWS_WOUT
    parts.append(jnp.zeros((PACK_ROWS - used, PACK_W), parts[0].dtype))
    return jnp.concatenate(parts, axis=0).astype(dtype)


def _unpack_big(p):
    a, b, c = ROWS_W1IN, ROWS_W1IN + ROWS_W2IN, ROWS_W1IN + ROWS_W2IN + ROWS_WOUT
    return (p[:a].reshape(1, D, 512), p[a:b].reshape(1, D, 514), p[b:c].reshape(1, 128, D),
            p[c:c + ROWS_WOUT].reshape(1, 128, D))


def _pad_lanes(a):
    return jnp.pad(a, ((0, 0), (0, LANES - a.shape[1])))


def _heads(a, s):
    return a.reshape(s, H, DH).transpose(1, 0, 2)


def _unheads(a, s):
    return a.transpose(1, 0, 2).reshape(s, D)


def _aug(cols, s):
    used = sum(c.shape[-1] for c in cols)
    return jnp.concatenate(cols + [jnp.zeros((H, s, HA - used), BF16)], axis=-1)


def _terms(v):
    return [t[..., None] for t in _split3(v)]


def _tiles(a, t):
    return a.reshape(H, a.shape[1] // t, t, HA)


def _tiles_t(a, t):
    return _tiles(a, t).transpose(0, 1, 3, 2)


def kernel(x, conv_norm_g, conv_w_in, conv_w, conv_w_out, attn_norm_g, attn_w_in, attn_b_f, attn_q_norm_g, attn_k_norm_g, attn_w_out, loss_target, m_conv_norm_g, m_conv_w_in, m_conv_w, m_conv_w_out, m_attn_norm_g, m_attn_w_in, m_attn_b_f, m_attn_q_norm_g, m_attn_k_norm_g, m_attn_w_out, v_conv_norm_g, v_conv_w_in, v_conv_w, v_conv_w_out, v_attn_norm_g, v_attn_w_in, v_attn_b_f, v_attn_q_norm_g, v_attn_k_norm_g, v_attn_w_out):
    s = x.shape[1]
    tq = min(TQ, s)
    tmf, tmb = min(TM_FWD, s), min(TM_BWD, s)
    me = 4 * lax.axis_index("x") + 2 * lax.axis_index("y") + lax.axis_index("c")
    xv, tgt = x[0], loss_target[0]

    wg = _all_gather(_pack_big(conv_w_in[0], attn_w_in[0], conv_w_out[0], attn_w_out[0], BF16), "gather_weights")
    small_w = jnp.concatenate([conv_w[0], attn_norm_g, jnp.zeros((4, 128), F32)], axis=0)
    sg_ = _all_gather(small_w, "gather_small_weights")
    a, b, c = ROWS_W1IN, ROWS_W1IN + ROWS_W2IN, ROWS_W1IN + ROWS_W2IN + ROWS_WOUT
    w1in = wg[:, :a].transpose(1, 0, 2).reshape(D, 4 * D)
    w2all = wg[:, a:b].reshape(NDEV, D, 514).transpose(1, 0, 2).reshape(D, 4 * D + H)
    w2in, wf = w2all[:, :4 * D], _pad_lanes(w2all[:, 4 * D:])
    w1out = wg[:, b:c].reshape(D, D)
    w2out = wg[:, c:c + ROWS_WOUT].reshape(D, D)
    cw = jnp.concatenate([sg_[:, 0:3, :].transpose(1, 0, 2).reshape(3, D), jnp.zeros((5, D), F32)], axis=0)
    g2 = sg_[:, 3, :].reshape(1, D)
    qg_t, kg_t = jnp.tile(attn_q_norm_g, (1, H)), jnp.tile(attn_k_norm_g, (1, H))
    bf = _pad_lanes(attn_b_f)

    e = (jnp.arange(D)[:, None] // DH == jnp.arange(LANES)[None, :]).astype(BF16)
    fold = (jnp.arange(D)[:, None] % DH == jnp.arange(LANES)[None, :]).astype(BF16)
    tril = (jnp.arange(tmb)[:, None] >= jnp.arange(tmb)[None, :]).astype(BF16)
    triu = tril.T
    src = jnp.arange(3 * LANES)
    dst = jnp.arange(H * HA)[None, :] - (HA * (src % LANES) + src // LANES)[:, None]
    place_k = ((dst == COL_BIAS) & (src % LANES < H)[:, None]).astype(BF16)
    place_q = ((dst == ROW_BIAS) & (src % LANES < H)[:, None]).astype(BF16)

    x1, h1, p1, yg = _conv_fwd(xv, conv_norm_g, w1in, cw, w1out)
    reach = 1.01 * math.sqrt(DH) * jnp.max(jnp.abs(attn_q_norm_g)) * jnp.max(jnp.abs(attn_k_norm_g))
    bounded = (2.0 * reach <= BOUNDED_SOFTMAX_REACH).astype(F32).reshape(1, 1)
    group = tq * min(FWD_TILES, s // tq)
    h2, qpre, kpre, z, qa, ka, va, vat, invq, invk, cc, fl = _attn_proj_fwd(
        x1, g2, w2in, wf, bf, qg_t, kg_t, jnp.broadcast_to(reach, (1, LANES)), e, e.T, tril, place_k, place_q, group)
    ol, qab, qabt = _flash_fwd(bounded, _tiles(qa, tq), ka, vat)
    og, dx2, dz, doa, doat, lossp = _attn_out(ol.reshape(H, s, HA), z, x1, tgt, w2out, w2out.T, e, place_k,
                                              tq * min(BWD_TILES, s // tq))

    dq, dkt, dvt = _flash_bwd(qab, _tiles(doa, tq), qabt, doat, _tiles(ka, tq), _tiles(va, tq))
    dx1, dp2, df, dg2, dqg, dkg, dbf = _attn_proj_bwd(
        dq, dkt, dvt, dz, qpre, kpre, invq, invk, fl, x1, dx2, g2, qg_t, kg_t, w2in.T, wf.T, e, e.T, triu, fold)
    gx, dp1, dcw, dg1 = _conv_bwd(dx1, p1, xv, conv_norm_g, cw, w1out.T, w1in.T)
    dw2out = _wgrad(og, dx2, "wgrad_attn_out")
    dw2in = _wgrad(h2, dp2, "wgrad_attn_in")
    dwf = _wgrad(h2, df, "wgrad_attn_forget")
    dw1out = _wgrad(yg, dx1, "wgrad_conv_out")
    dw1in = _wgrad(h1, dp1, "wgrad_conv_in")

    dw2all = jnp.concatenate([dw2in, dwf[:, :H]], axis=1)
    gfull = jnp.concatenate([
        dw1in.reshape(D, NDEV, 512).transpose(1, 0, 2),
        dw2all.reshape(D, NDEV, 514).transpose(1, 0, 2).reshape(NDEV, ROWS_W2IN, PACK_W),
        dw1out.reshape(NDEV, ROWS_WOUT, PACK_W), dw2out.reshape(NDEV, ROWS_WOUT, PACK_W),
        jnp.zeros((NDEV, PACK_ROWS - c - ROWS_WOUT, PACK_W), BF16)], axis=1)
    parts = _exchange_blocks(gfull, "scatter_grads")
    pack = lambda t1, t2, t3, t4: _pack_big(t1[0], t2[0], t3[0], t4[0], F32)
    gb, db, mb, vb = _adamw_big(parts, pack(conv_w_in, attn_w_in, conv_w_out, attn_w_out),
                                pack(m_conv_w_in, m_attn_w_in, m_conv_w_out, m_attn_w_out),
                                pack(v_conv_w_in, v_attn_w_in, v_conv_w_out, v_attn_w_out))

    shard_rows = jnp.concatenate([dcw[0:3].reshape(3, NDEV, 128).transpose(1, 0, 2), dg2.reshape(NDEV, 1, 128),
                                  jnp.zeros((NDEV, 4, 128), F32)], axis=1).reshape(64, 128)
    small_g = jnp.concatenate([shard_rows, dg1.reshape(8, 128), dbf, dqg[0:1], dkg[0:1], jnp.zeros((5, 128), F32),
                               lossp.reshape(8, 128)], axis=0)
    gs, loss8 = _sum_parts(_all_gather(small_g, "gather_small_grads"), 80)
    loss = loss8[0, 0]
    mine = lax.dynamic_slice(gs, (8 * me, 0), (8, 128))
    g_small = jnp.concatenate([mine, gs[64:75], jnp.zeros((5, 128), F32)], axis=0)

    def pack_small(cwk, ang, cng, bfk, qgk, kgk):
        return jnp.concatenate([cwk[0], ang, jnp.zeros((4, 128), F32), cng.reshape(8, 128), _pad_lanes(bfk),
                                _pad_lanes(qgk), _pad_lanes(kgk), jnp.zeros((5, 128), F32)], axis=0)

    ds_, ms_, vs_ = _adamw_small(
        g_small, pack_small(conv_w, attn_norm_g, conv_norm_g, attn_b_f, attn_q_norm_g, attn_k_norm_g),
        pack_small(m_conv_w, m_attn_norm_g, m_conv_norm_g, m_attn_b_f, m_attn_q_norm_g, m_attn_k_norm_g),
        pack_small(v_conv_w, v_attn_norm_g, v_conv_norm_g, v_attn_b_f, v_attn_q_norm_g, v_attn_k_norm_g))

    def leaves(big, small):
        w1i, w2i, w1o, w2o = _unpack_big(big)
        return (small[8:16].reshape(1, D), w1i, small[0:3].reshape(1, 3, 128), w1o, small[3:4], w2i,
                small[16:17, :H], small[17:18, :DH], small[18:19, :DH], w2o)

    return (loss, gx[None], *leaves(gb, g_small), *leaves(db, ds_), *leaves(mb, ms_), *leaves(vb, vs_))
```

```python
import functools
import math

import jax
import jax.numpy as jnp
from jax import lax
from jax.experimental import pallas as pl
from jax.experimental.pallas import tpu as pltpu

F32 = jnp.float32
BF16 = jnp.bfloat16

D = 1024
H = 16
DH = 64
NDEV = 8
RMS_EPS = 1e-6
LANES = 128
HA = 128
TM_FWD = 512
TM_BWD = 256
TQ = 512
TK = 512
AUG_ROWS = 80
FWD_TILES = 8
BWD_TILES = 4
CH = 256
PACK_W = 512
PACK_ROWS = 2576
ROWS_W1IN, ROWS_W2IN, ROWS_WOUT = 1024, 1028, 256
ADAM_LR, ADAM_B1, ADAM_B2, ADAM_EPS, ADAM_WD, ADAM_STEP = 0.001, 0.9, 0.999, 1e-08, 0.01, 10
VMEM_LIMIT = 56 * 1024 * 1024
MASKED = -1e30
BOUNDED_SOFTMAX_REACH = 60.0
MESH = pl.DeviceIdType.MESH


def _params(*sem, vmem=VMEM_LIMIT):
    return pltpu.CompilerParams(dimension_semantics=sem or None, vmem_limit_bytes=vmem)


def _const(shape):
    nd = len(shape)
    return pl.BlockSpec(shape, lambda *_: (0,) * nd, pipeline_mode=pl.Buffered(1))


def _rows(tm, n, rev=False, steps=None):
    if rev:
        return pl.BlockSpec((tm, n), lambda i: (steps - 1 - i, 0))
    return pl.BlockSpec((tm, n), lambda i: (i, 0))


def _dot(a, b):
    return jnp.dot(a, b, preferred_element_type=F32)


def _top16(x):
    bits = lax.bitcast_convert_type(x, jnp.uint32) & jnp.uint32(0xFFFF0000)
    return lax.bitcast_convert_type(bits, F32)


def _split2(x):
    hi = _top16(x)
    return hi.astype(BF16), (x - hi).astype(BF16)


def _split3(x):
    hi = _top16(x)
    r = x - hi
    mid = _top16(r)
    return hi.astype(BF16), mid.astype(BF16), (r - mid).astype(BF16)


def _seg_sum(a, e):
    hi, lo = _split2(a)
    return _dot(hi, e) + _dot(lo, e)


def _seg_bcast(s, et):
    hi, lo = _split2(s)
    return _dot(hi, et) + _dot(lo, et)


def _tri_sum(t, v):
    hi, mid, lo = _split3(v)
    return _dot(t, hi) + _dot(t, mid) + _dot(t, lo)


def _sigmoid(z):
    return 1.0 / (1.0 + jnp.exp(-z))


def _place():
    return lax.axis_index("x"), lax.axis_index("y"), lax.axis_index("c")


def _all_gather(xb, name):
    r, c_ = xb.shape

    def body(x_ref, out_ref, send_sems, recv_sems, local_sem):
        x, y, c = _place()
        me, sibling = (x, y, c), (x, y, 1 - c)
        chips = [(1 - x, y), (x, 1 - y), (1 - x, 1 - y)]

        def slab(px, py, pc):
            return out_ref.at[4 * px + 2 * py + pc]

        def copy(k, block, to, src=None):
            return pltpu.make_async_remote_copy(
                src_ref=slab(*block) if src is None else src, dst_ref=slab(*block),
                send_sem=send_sems.at[k], recv_sem=recv_sems.at[k], device_id=to, device_id_type=MESH)

        mine = pltpu.make_async_copy(x_ref, slab(*me), local_sem)
        mine.start()
        first = [copy(0, me, sibling, src=x_ref)]
        first += [copy(1 + j, me, (*chip, c), src=x_ref) for j, chip in enumerate(chips)]
        for cp in first:
            cp.start()
        passed = [copy(4 + j, (*chip, c), sibling) for j, chip in enumerate(chips)]
        for j, chip in enumerate(chips):
            copy(1 + j, (*chip, c), me).wait_recv()
            passed[j].start()
        copy(0, sibling, me).wait_recv()
        for j, chip in enumerate(chips):
            copy(4 + j, (*chip, 1 - c), me).wait_recv()
        for cp in first + passed:
            cp.wait_send()
        mine.wait()

    return pl.pallas_call(
        body, name=name,
        out_shape=jax.ShapeDtypeStruct((NDEV, r, c_), xb.dtype),
        in_specs=[pl.BlockSpec(memory_space=pl.ANY)],
        out_specs=pl.BlockSpec(memory_space=pl.ANY),
        scratch_shapes=[pltpu.SemaphoreType.DMA((7,)), pltpu.SemaphoreType.DMA((7,)), pltpu.SemaphoreType.DMA],
    )(xb)


def _exchange_blocks(g, name):
    _, r, c_ = g.shape

    def body(g_ref, out_ref, send_sems, recv_sems, local_sem):
        x, y, c = _place()
        me = 4 * x + 2 * y + c
        mine = pltpu.make_async_copy(g_ref.at[me], out_ref.at[0], local_sem)
        mine.start()
        copies = []
        for k in range(1, NDEV):
            px = 1 - x if k & 4 else x
            py = 1 - y if k & 2 else y
            pc = 1 - c if k & 1 else c
            cp = pltpu.make_async_remote_copy(
                src_ref=g_ref.at[4 * px + 2 * py + pc], dst_ref=out_ref.at[k],
                send_sem=send_sems.at[k - 1], recv_sem=recv_sems.at[k - 1],
                device_id=(px, py, pc), device_id_type=MESH)
            cp.start()
            copies.append(cp)
        for cp in copies:
            cp.wait()
        mine.wait()

    return pl.pallas_call(
        body, name=name,
        out_shape=jax.ShapeDtypeStruct((NDEV, r, c_), g.dtype),
        in_specs=[pl.BlockSpec(memory_space=pl.ANY)],
        out_specs=pl.BlockSpec(memory_space=pl.ANY),
        scratch_shapes=[pltpu.SemaphoreType.DMA((7,)), pltpu.SemaphoreType.DMA((7,)), pltpu.SemaphoreType.DMA],
    )(g)


def _conv_fwd(x, g1, w1in, cw, w1out):
    s = x.shape[0]
    tm = min(TM_FWD, s)
    steps = s // tm

    def body(x_ref, g_ref, win_ref, cw_ref, wout_ref, x1_ref, h1_ref, p1_ref, yg_ref, tail_ref):
        @pl.when(pl.program_id(0) == 0)
        def _():
            tail_ref[...] = jnp.zeros_like(tail_ref)

        xv = x_ref[...]
        inv = lax.rsqrt(jnp.mean(xv * xv, axis=-1, keepdims=True) + RMS_EPS)
        h = (xv * inv * g_ref[...]).astype(BF16)
        h1_ref[...] = h
        row = lax.broadcasted_iota(jnp.int32, (tm, CH), 0)
        for ci in range(D // CH):
            lo, hi = ci * CH, (ci + 1) * CH
            parts = []
            for k in range(4):
                pk = _dot(h, win_ref[:, k * D + lo:k * D + hi]).astype(BF16)
                p1_ref[:, k * D + lo:k * D + hi] = pk
                parts.append(pk.astype(F32))
            b, c, xin, z = parts
            u = c * xin
            t6 = tail_ref[6:7, lo:hi]
            t7 = tail_ref[7:8, lo:hi]
            u1 = jnp.where(row == 0, t7, pltpu.roll(u, 1, 0))
            u2 = jnp.where(row == 0, t6, jnp.where(row == 1, t7, pltpu.roll(u, 2, 0)))
            tail_ref[:, lo:hi] = u[tm - 8:, :]
            y = cw_ref[2:3, lo:hi] * u + cw_ref[1:2, lo:hi] * u1 + cw_ref[0:1, lo:hi] * u2
            yg_ref[:, lo:hi] = (b * y * (z * _sigmoid(z))).astype(BF16)
        x1_ref[...] = xv + _dot(yg_ref[...], wout_ref[...])

    return pl.pallas_call(
        body, name="conv_fwd", grid=(steps,),
        in_specs=[_rows(tm, D), _const((1, D)), _const((D, 4 * D)), _const((8, D)), _const((D, D))],
        out_specs=[_rows(tm, D), _rows(tm, D), _rows(tm, 4 * D), _rows(tm, D)],
        out_shape=[jax.ShapeDtypeStruct((s, D), F32), jax.ShapeDtypeStruct((s, D), BF16),
                   jax.ShapeDtypeStruct((s, 4 * D), BF16), jax.ShapeDtypeStruct((s, D), BF16)],
        scratch_shapes=[pltpu.VMEM((8, D), F32)],
        compiler_params=_params("arbitrary"),
    )(x, g1, w1in, cw, w1out)


COL_BIAS = DH
ROW_BIAS = DH + 3


def _terms_cat(val):
    return jnp.concatenate(_split3(val), axis=1)


def _split_heads(x, aug, out_ref, lane, t_ref=None):
    for hp in range(H // 2):
        pair = x[:, hp * HA:(hp + 1) * HA]
        for k, feat in enumerate((pair, pltpu.roll(pair, DH, 1))):
            block = jnp.where(lane < DH, feat, aug(2 * hp + k))
            out_ref[2 * hp + k] = block.astype(BF16)
            if t_ref is not None:
                t_ref[2 * hp + k] = block.T[0:t_ref.shape[1], :].astype(BF16)


def _ones_at(lane, first):
    return jnp.where((lane >= first) & (lane < first + 3), 1.0, 0.0)


def _attn_proj_fwd(x1, g2, w2in, wf, bf, qg, kg, shift, e, et, tril, place_k, place_q, group):
    s = x1.shape[0]
    tm = min(TM_BWD, s)
    steps = s // tm
    per = group // tm

    def body(x_ref, g_ref, win_ref, wf_ref, bf_ref, qg_ref, kg_ref, sh_ref, e_ref, et_ref, tril_ref, pk_ref, pq_ref,
             h2_ref, qp_ref, kp_ref, z_ref, qa_ref, ka_ref, va_ref, vat_ref, iq_ref, ik_ref, c_ref, fl_ref, carry_ref):
        @pl.when(pl.program_id(0) == 0)
        def _():
            carry_ref[...] = jnp.zeros_like(carry_ref)

        xv = x_ref[...]
        inv = lax.rsqrt(jnp.mean(xv * xv, axis=-1, keepdims=True) + RMS_EPS)
        h = (xv * inv * g_ref[...]).astype(BF16)
        h2_ref[...] = h

        fl = _dot(h, wf_ref[...]) + bf_ref[...]
        fl_ref[...] = fl
        ex = jnp.exp(-jnp.abs(fl))
        up = 1.0 + ex
        log1p = jnp.where(up == 1.0, ex, jnp.log(up) * ex / (up - 1.0))
        lane = lax.broadcasted_iota(jnp.int32, (tm, LANES), 1)
        logf = jnp.where(lane < H, jnp.minimum(fl, 0.0) - log1p, 0.0)
        c_ref[...] = _tri_sum(tril_ref[...], logf) + carry_ref[0:1, :]
        carry_ref[0:1, :] = c_ref[tm - 1:tm, :]
        c = c_ref[...]

        def normed(col, pre_ref, inv_ref, gain_ref, scale):
            pre = _dot(h, win_ref[:, col * D:(col + 1) * D]).astype(BF16)
            pre_ref[...] = pre
            t = pre.astype(F32)
            invh = lax.rsqrt(_seg_sum(t * t, e_ref[...]) * (1.0 / DH) + RMS_EPS)
            inv_ref[...] = invh
            return t * _seg_bcast(invh, et_ref[...]) * (gain_ref[...] * scale)

        ones_col, ones_row = _ones_at(lane, COL_BIAS), _ones_at(lane, ROW_BIAS)
        q_bias = _dot(_terms_cat(c - sh_ref[...]), pq_ref[...])
        _split_heads(normed(0, qp_ref, iq_ref, qg_ref, 1.0 / math.sqrt(DH)),
                     lambda hh: q_bias[:, hh * HA:(hh + 1) * HA] + ones_col, qa_ref, lane)
        k_bias = _dot(_terms_cat(-c), pk_ref[...])
        _split_heads(normed(1, kp_ref, ik_ref, kg_ref, 1.0),
                     lambda hh: k_bias[:, hh * HA:(hh + 1) * HA] + ones_row, ka_ref, lane)
        v = _dot(h, win_ref[:, 2 * D:3 * D])
        _split_heads(v, lambda hh: ones_col, va_ref, lane, vat_ref)
        z_ref[...] = _dot(h, win_ref[:, 3 * D:4 * D]).astype(BF16)

    row_bf = lambda: _rows(tm, D)
    row_sm = lambda: _rows(tm, LANES)
    heads = lambda: pl.BlockSpec((H, tm, HA), lambda i: (0, i, 0))
    return pl.pallas_call(
        body, name="attn_proj_fwd", grid=(steps,),
        in_specs=[_rows(tm, D), _const((1, D)), _const((D, 4 * D)), _const((D, LANES)), _const((1, LANES)),
                  _const((1, D)), _const((1, D)), _const((1, LANES)), _const((D, LANES)), _const((LANES, D)),
                  _const((tm, tm)), _const((3 * LANES, H * HA)), _const((3 * LANES, H * HA))],
        out_specs=[row_bf() for _ in range(4)] + [heads() for _ in range(3)] + [
            pl.BlockSpec((H, None, AUG_ROWS, tm), lambda i: (0, i // per, 0, i % per))] + [row_sm() for _ in range(4)],
        out_shape=[jax.ShapeDtypeStruct((s, D), BF16)] * 4 + [jax.ShapeDtypeStruct((H, s, HA), BF16)] * 3 + [
            jax.ShapeDtypeStruct((H, s // group, AUG_ROWS, group), BF16)] + [jax.ShapeDtypeStruct((s, LANES), F32)] * 4,
        scratch_shapes=[pltpu.VMEM((8, LANES), F32)],
        compiler_params=_params("arbitrary"),
    )(x1, g2, w2in, wf, bf, qg, kg, shift, e, et, tril, place_k, place_q)


def _dot_nt(a, b):
    return lax.dot_general(a, b, (((1,), (1,)), ((), ())), preferred_element_type=F32)


def _dot_tn(a, b):
    return lax.dot_general(a, b, (((0,), (0,)), ((), ())), preferred_element_type=F32)


def _flash_fwd(bounded, qa, ka, vat):
    h_, nq, tq, _ = qa.shape
    s, tk = ka.shape[1], tq
    group = vat.shape[3]
    gt = group // tk
    shift = gt.bit_length() - 1

    def body(flag_ref, q_ref, k_ref, vt_ref, o_ref, qb_ref, m_ref, acct_ref):
        i = pl.program_id(1)
        lane = lax.broadcasted_iota(jnp.int32, (tq, HA), 1)
        in_bias = (lane >= ROW_BIAS) & (lane < ROW_BIAS + 3)
        acct_ref[...] = jnp.zeros_like(acct_ref)
        g = lax.shift_right_logical(i, shift)
        r = i & (gt - 1)

        def run(use_bound):
            q = q_ref[...]
            if not use_bound:
                q = jnp.where(in_bias, jnp.zeros_like(q), q)
                m_ref[...] = jnp.full_like(m_ref, MASKED)

            def step(first_key, vt, tiles, diagonal_at):
                keys = pl.ds(pl.multiple_of(first_key, tk), tiles * tk)
                zt = _dot_nt(k_ref[keys, :], q)
                if diagonal_at is not None:
                    key = lax.broadcasted_iota(jnp.int32, (tiles * tk, tq), 0)
                    qry = lax.broadcasted_iota(jnp.int32, (tiles * tk, tq), 1)
                    zt = jnp.where(key <= qry + diagonal_at * tk, zt, MASKED)
                if use_bound:
                    acct_ref[0:AUG_ROWS, :] += _dot(vt, jnp.exp(zt).astype(BF16))
                else:
                    m_old = m_ref[...]
                    m_new = jnp.maximum(m_old, jnp.max(zt, axis=0, keepdims=True))
                    pt = jnp.exp(zt - m_new)
                    acct_ref[0:AUG_ROWS, :] = jnp.exp(m_old - m_new) * acct_ref[0:AUG_ROWS, :] + _dot(
                        vt, pt.astype(BF16))
                    m_ref[...] = m_new

            def whole_group(jj, carry):
                step(jj * group, vt_ref[jj], gt, None)
                return carry

            lax.fori_loop(0, g, whole_group, 0)
            for rr in range(gt):
                pl.when(r == rr)(functools.partial(
                    lambda rr: step(g * group, vt_ref[g, :, 0:(rr + 1) * tk], rr + 1, rr), rr))

            if not use_bound:
                acct_ref[COL_BIAS + 1:COL_BIAS + 2, :] = m_ref[...]
            acc = acct_ref[...].T
            l = jnp.sum(jnp.where(lane == COL_BIAS, acc, 0.0), axis=1, keepdims=True)
            if use_bound:
                m = -jnp.sum(jnp.where(in_bias, q.astype(F32), 0.0), axis=1, keepdims=True)
            else:
                m = jnp.sum(jnp.where(lane == COL_BIAS + 1, acc, 0.0), axis=1, keepdims=True)
            lse = m + jnp.log(l)
            o_ref[...] = jnp.where(lane < DH, acc / l, lse)
            hi, mid, lo = _split3(-lse)
            qb_ref[...] = jnp.where(lane == ROW_BIAS, hi, jnp.where(lane == ROW_BIAS + 1, mid, jnp.where(
                lane == ROW_BIAS + 2, lo, q_ref[...])))

        use_bound = flag_ref[0, 0] > 0.5
        pl.when(use_bound)(lambda: run(True))
        pl.when(jnp.logical_not(use_bound))(lambda: run(False))

    tile = lambda: pl.BlockSpec((None, None, tq, HA), lambda h, i: (h, i, 0, 0))
    return pl.pallas_call(
        body, name="flash_fwd", grid=(h_, nq),
        in_specs=[pl.BlockSpec(memory_space=pltpu.SMEM), tile(), pl.BlockSpec((None, s, HA), lambda h, i: (h, 0, 0)),
                  pl.BlockSpec((None, s // group, AUG_ROWS, group), lambda h, i: (h, 0, 0, 0))],
        out_specs=[tile(), tile()],
        out_shape=[jax.ShapeDtypeStruct((h_, nq, tq, HA), F32), jax.ShapeDtypeStruct((h_, nq, tq, HA), BF16)],
        scratch_shapes=[pltpu.VMEM((1, tq), F32), pltpu.VMEM((HA, tq), F32)],
        compiler_params=_params("arbitrary", "arbitrary"),
    )(bounded, qa, ka, vat)


def _merge_heads(src, out_ref, lane):
    for hp in range(H // 2):
        out_ref[:, hp * HA:(hp + 1) * HA] = jnp.where(lane < DH, src(2 * hp), pltpu.roll(src(2 * hp + 1), DH, 1))


def _attn_out(ol, z, x1, tgt, w2out, w2out_t, e, place_k):
    s = z.shape[0]
    tm = min(TM_BWD, s)
    steps = s // tm

    def body(ol_ref, z_ref, x1_ref, t_ref, w_ref, wt_ref, e_ref, pk_ref, og_ref, dx2_ref, dz_ref, doa_ref, loss_ref,
             o_ref):
        @pl.when(pl.program_id(0) == 0)
        def _():
            loss_ref[...] = jnp.zeros_like(loss_ref)

        lane = lax.broadcasted_iota(jnp.int32, (tm, HA), 1)
        _merge_heads(lambda hh: ol_ref[hh], o_ref, lane)
        ov = o_ref[...]
        zv = z_ref[...].astype(F32)
        sg = _sigmoid(zv)
        sil = zv * sg
        og = (ov * sil).astype(BF16)
        og_ref[...] = og
        err = x1_ref[...] + _dot(og, w_ref[...]) - t_ref[...]
        loss_ref[...] += (0.5 / D) * jnp.sum(err * err, axis=0, keepdims=True)
        dx2 = err * (1.0 / D)
        dx2_ref[...] = dx2
        dog = _dot(dx2.astype(BF16), wt_ref[...])
        do = (dog * sil).astype(BF16).astype(F32)
        dz_ref[...] = (dog * ov * (sg * (1.0 + zv * (1.0 - sg)))).astype(BF16)
        delta = _seg_sum(do * ov, e_ref[...])
        d_bias = _dot(_terms_cat(-delta), pk_ref[...])
        _split_heads(do, lambda hh: d_bias[:, hh * HA:(hh + 1) * HA], doa_ref, lane)

    heads = lambda: pl.BlockSpec((H, tm, HA), lambda i: (0, i, 0))
    return pl.pallas_call(
        body, name="attn_out", grid=(steps,),
        in_specs=[heads(), _rows(tm, D), _rows(tm, D), _rows(tm, D), _const((D, D)), _const((D, D)),
                  _const((D, LANES)), _const((3 * LANES, H * HA))],
        out_specs=[_rows(tm, D), _rows(tm, D), _rows(tm, D), heads(), pl.BlockSpec((1, D), lambda i: (0, 0))],
        out_shape=[jax.ShapeDtypeStruct((s, D), BF16), jax.ShapeDtypeStruct((s, D), F32),
                   jax.ShapeDtypeStruct((s, D), BF16), jax.ShapeDtypeStruct((H, s, HA), BF16),
                   jax.ShapeDtypeStruct((1, D), F32)],
        scratch_shapes=[pltpu.VMEM((tm, D), F32)],
        compiler_params=_params("arbitrary"),
    )(ol, z, x1, tgt, w2out, w2out_t, e, place_k)


def _flash_bwd(qab, doa, ka, va):
    h_, nq, tq, _ = qab.shape
    nk, tk = ka.shape[1], ka.shape[2]
    s = nk * tk

    def body(qa_ref, da_ref, ka_ref, va_ref, dq_hbm, dkt_ref, dvt_ref, dq_acc, sem):
        hh = pl.program_id(0)
        j = pl.program_id(1)

        @pl.when(j == 0)
        def _():
            dq_acc[...] = jnp.zeros_like(dq_acc)

        dkt_ref[...] = jnp.zeros_like(dkt_ref)
        dvt_ref[...] = jnp.zeros_like(dvt_ref)
        ka_v = ka_ref[...]
        va_v = va_ref[...]

        def step(i, tiles, masked):
            rows = pl.ds(pl.multiple_of(i * tq, tq), tiles * tq)
            qa_i, da_i = qa_ref[rows, :], da_ref[rows, :]
            zz = _dot_nt(qa_i, ka_v)
            if masked:
                row = lax.broadcasted_iota(jnp.int32, (tq, tk), 0)
                col = lax.broadcasted_iota(jnp.int32, (tq, tk), 1)
                zz = jnp.where(row >= col, zz, MASKED)
            p = jnp.exp(zz)
            ds = (p * _dot_nt(da_i, va_v)).astype(BF16)
            pb = p.astype(BF16)
            dq_acc[rows, :] += _dot(ds, ka_v)
            dvt_ref[...] += _dot_tn(da_i, pb)
            dkt_ref[...] += _dot_tn(qa_i, ds)

        step(j, 1, True)
        below = nq - 1 - j

        def wide(t, carry):
            step(j + 1 + BWD_TILES * t, BWD_TILES, False)
            return carry

        trips = below // BWD_TILES
        lax.fori_loop(0, trips, wide, 0)
        for k in range(BWD_TILES - 1):
            pl.when(trips * BWD_TILES + k < below)(
                functools.partial(step, j + 1 + trips * BWD_TILES + k, 1, False))

        @pl.when(j == nk - 1)
        def _():
            cp = pltpu.make_async_copy(dq_acc, dq_hbm.at[hh], sem)
            cp.start()
            cp.wait()

    whole = lambda: pl.BlockSpec((None, nq * tq, HA), lambda h, j: (h, 0, 0))
    tile = lambda: pl.BlockSpec((None, None, tk, HA), lambda h, j: (h, j, 0, 0))
    return pl.pallas_call(
        body, name="flash_bwd", grid=(h_, nk),
        in_specs=[whole(), whole(), tile(), tile()],
        out_specs=[pl.BlockSpec(memory_space=pl.ANY),
                   pl.BlockSpec((None, HA, tk), lambda h, j: (h, 0, j)),
                   pl.BlockSpec((None, HA, tk), lambda h, j: (h, 0, j))],
        out_shape=[jax.ShapeDtypeStruct((h_, s, HA), F32), jax.ShapeDtypeStruct((h_, HA, s), F32),
                   jax.ShapeDtypeStruct((h_, HA, s), F32)],
        scratch_shapes=[pltpu.VMEM((s, HA), F32), pltpu.SemaphoreType.DMA],
        compiler_params=_params("arbitrary", "arbitrary"),
    )(qab.reshape(h_, s, HA), doa.reshape(h_, s, HA), ka, va)


def _attn_proj_bwd(dq, dkt, dvt, dz, qpre, kpre, invq, invk, fl, x1, dx2, g2, qg, kg, w2in_t, wf_t, e, et,
                   triu, fold):
    s = x1.shape[0]
    tm = min(TM_BWD, s)
    steps = s // tm

    def body(dq_ref, dkt_ref, dvt_ref, dz_ref, qp_ref, kp_ref, iq_ref, ik_ref, fl_ref, x1_ref, dx2_ref,
             g2_ref, qg_ref, kg_ref, wt_ref, wft_ref, e_ref, et_ref, triu_ref, fold_ref,
             dx1_ref, dp_ref, df_ref, dg2_ref, dqg_ref, dkg_ref, dbf_ref, carry_ref, rc_ref, qcol_ref, kcol_ref,
             dqs_ref, dkn_ref, dv_ref):
        step = pl.program_id(0)
        lane = lax.broadcasted_iota(jnp.int32, (tm, HA), 1)
        _merge_heads(lambda hh: dvt_ref[hh].T, dv_ref, lane)
        dc = jnp.zeros((tm, HA), F32)
        for hp in range(H // 2):
            dqh = [dq_ref[2 * hp], dq_ref[2 * hp + 1]]
            dkh = [dkt_ref[2 * hp].T, dkt_ref[2 * hp + 1].T]
            dqs_ref[:, hp * HA:(hp + 1) * HA] = jnp.where(lane < DH, dqh[0], pltpu.roll(dqh[1], DH, 1))
            dkn_ref[:, hp * HA:(hp + 1) * HA] = jnp.where(lane < DH, dkh[0], pltpu.roll(dkh[1], DH, 1))
            for k in range(2):
                sums = jnp.where(lane == ROW_BIAS, dqh[k], 0.0) - pltpu.roll(
                    jnp.where(lane == COL_BIAS, dkh[k], 0.0), ROW_BIAS - COL_BIAS, 1)
                dc = dc + pltpu.roll(sums, (2 * hp + k - ROW_BIAS) % HA, 1)

        @pl.when(step == 0)
        def _():
            carry_ref[...] = jnp.zeros_like(carry_ref)
            dg2_ref[...] = jnp.zeros_like(dg2_ref)
            dbf_ref[...] = jnp.zeros_like(dbf_ref)
            qcol_ref[...] = jnp.zeros_like(qcol_ref)
            kcol_ref[...] = jnp.zeros_like(kcol_ref)
            dqg_ref[...] = jnp.zeros_like(dqg_ref)
            dkg_ref[...] = jnp.zeros_like(dkg_ref)

        rc_ref[...] = _tri_sum(triu_ref[...], dc) + carry_ref[0:1, :]
        carry_ref[0:1, :] = rc_ref[0:1, :]
        df = rc_ref[...] * _sigmoid(-fl_ref[...])
        dfb = df.astype(BF16)
        df_ref[...] = dfb
        dbf_ref[...] += jnp.sum(df, axis=0, keepdims=True)

        def norm_bwd(dn, pre_ref, inv_ref, gain_ref, col_ref):
            pre = pre_ref[...].astype(F32)
            invh = inv_ref[...]
            invb = _seg_bcast(invh, et_ref[...])
            col_ref[...] += jnp.sum(dn * pre * invb, axis=0, keepdims=True)
            gd = dn * gain_ref[...]
            mean = _seg_sum(gd * pre, e_ref[...]) * (1.0 / DH)
            return invb * gd - pre * _seg_bcast(mean * invh * invh * invh, et_ref[...])

        dq = norm_bwd(dqs_ref[...] * (1.0 / math.sqrt(DH)), qp_ref, iq_ref, qg_ref, qcol_ref).astype(BF16)
        dp_ref[:, 0:D] = dq
        dh = _dot(dq, wt_ref[0:D, :])
        dk = norm_bwd(dkn_ref[...], kp_ref, ik_ref, kg_ref, kcol_ref).astype(BF16)
        dp_ref[:, D:2 * D] = dk
        dh += _dot(dk, wt_ref[D:2 * D, :])
        dvb = dv_ref[...].astype(BF16)
        dp_ref[:, 2 * D:3 * D] = dvb
        dh += _dot(dvb, wt_ref[2 * D:3 * D, :])
        dzb = dz_ref[...]
        dp_ref[:, 3 * D:4 * D] = dzb
        dh += _dot(dzb, wt_ref[3 * D:4 * D, :])
        dh += _dot(dfb, wft_ref[...])

        xv = x1_ref[...]
        inv = lax.rsqrt(jnp.mean(xv * xv, axis=-1, keepdims=True) + RMS_EPS)
        dg2_ref[...] += jnp.sum(dh * xv * inv, axis=0, keepdims=True)
        gh = dh * g2_ref[...]
        dx1_ref[...] = dx2_ref[...] + inv * gh - xv * (inv * inv * inv * jnp.mean(gh * xv, axis=-1, keepdims=True))

        @pl.when(step == steps - 1)
        def _():
            dqg_ref[...] = _fold_heads(qcol_ref[...], fold_ref[...])
            dkg_ref[...] = _fold_heads(kcol_ref[...], fold_ref[...])

    rr = lambda n: _rows(tm, n, rev=True, steps=steps)
    acc = lambda n, r=1: pl.BlockSpec((r, n), lambda i: (0, 0))
    heads_t = lambda: pl.BlockSpec((H, HA, tm), lambda i: (0, 0, steps - 1 - i))
    return pl.pallas_call(
        body, name="attn_proj_bwd", grid=(steps,),
        in_specs=[pl.BlockSpec((H, tm, HA), lambda i: (0, steps - 1 - i, 0)), heads_t(), heads_t(),
                  rr(D), rr(D), rr(D), rr(LANES), rr(LANES), rr(LANES), rr(D), rr(D),
                  _const((1, D)), _const((1, D)), _const((1, D)), _const((4 * D, D)), _const((LANES, D)),
                  _const((D, LANES)), _const((LANES, D)), _const((tm, tm)), _const((D, LANES))],
        out_specs=[rr(D), rr(4 * D), rr(LANES), acc(D), acc(LANES, 8), acc(LANES, 8), acc(LANES)],
        out_shape=[jax.ShapeDtypeStruct((s, D), F32), jax.ShapeDtypeStruct((s, 4 * D), BF16),
                   jax.ShapeDtypeStruct((s, LANES), BF16), jax.ShapeDtypeStruct((1, D), F32),
                   jax.ShapeDtypeStruct((8, LANES), F32), jax.ShapeDtypeStruct((8, LANES), F32),
                   jax.ShapeDtypeStruct((1, LANES), F32)],
        scratch_shapes=[pltpu.VMEM((8, LANES), F32), pltpu.VMEM((tm, LANES), F32), pltpu.VMEM((1, D), F32),
                        pltpu.VMEM((1, D), F32), pltpu.VMEM((tm, D), F32), pltpu.VMEM((tm, D), F32),
                        pltpu.VMEM((tm, D), F32)],
        compiler_params=_params("arbitrary"),
    )(dq, dkt, dvt, dz, qpre, kpre, invq, invk, fl, x1, dx2, g2, qg, kg, w2in_t, wf_t, e, et, triu, fold)


def _fold_heads(col, fold):
    hi, mid, lo = _split3(jnp.broadcast_to(col, (8, D)))
    return _dot(hi, fold) + _dot(mid, fold) + _dot(lo, fold)


def _conv_bwd(dx1, p1, x, g1, cw, w1out_t, w1in_t):
    s = x.shape[0]
    tm = min(TM_BWD, s)
    steps = s // tm
    halo = tm // 8

    def body(dx1_ref, b_ref, c_ref, xi_ref, z_ref, ch_ref, xh_ref, x_ref, g_ref, cw_ref, wot_ref, wit_ref,
             gx_ref, dp_ref, dcw_ref, dg1_ref, head_ref):
        step = pl.program_id(0)

        @pl.when(step == 0)
        def _():
            head_ref[...] = jnp.zeros_like(head_ref)
            dcw_ref[...] = jnp.zeros_like(dcw_ref)
            dg1_ref[...] = jnp.zeros_like(dg1_ref)

        first_tile = step == steps - 1
        dx1 = dx1_ref[...]
        row = lax.broadcasted_iota(jnp.int32, (tm, CH), 0)
        dyb = dx1.astype(BF16)
        for ci in range(D // CH):
            lo, hi = ci * CH, (ci + 1) * CH
            dyg = _dot(dyb, wot_ref[:, lo:hi])
            b = b_ref[:, lo:hi].astype(F32)
            c = c_ref[:, lo:hi].astype(F32)
            xin = xi_ref[:, lo:hi].astype(F32)
            z = z_ref[:, lo:hi].astype(F32)
            u = c * xin
            t6 = jnp.where(first_tile, 0.0, ch_ref[6:7, lo:hi].astype(F32) * xh_ref[6:7, lo:hi].astype(F32))
            t7 = jnp.where(first_tile, 0.0, ch_ref[7:8, lo:hi].astype(F32) * xh_ref[7:8, lo:hi].astype(F32))
            u1 = jnp.where(row == 0, t7, pltpu.roll(u, 1, 0))
            u2 = jnp.where(row == 0, t6, jnp.where(row == 1, t7, pltpu.roll(u, 2, 0)))
            w0, w1, w2 = cw_ref[0:1, lo:hi], cw_ref[1:2, lo:hi], cw_ref[2:3, lo:hi]
            y = w2 * u + w1 * u1 + w0 * u2
            sg = _sigmoid(z)
            sil = z * sg
            dp_ref[:, lo:hi] = (dyg * y * sil).astype(BF16)
            dy = dyg * b * sil
            dp_ref[:, 3 * D + lo:3 * D + hi] = (dyg * b * y * (sg * (1.0 + z * (1.0 - sg)))).astype(BF16)
            dcw_ref[2:3, lo:hi] += jnp.sum(dy * u, axis=0, keepdims=True)
            dcw_ref[1:2, lo:hi] += jnp.sum(dy * u1, axis=0, keepdims=True)
            dcw_ref[0:1, lo:hi] += jnp.sum(dy * u2, axis=0, keepdims=True)
            n0 = head_ref[0:1, lo:hi]
            n1 = head_ref[1:2, lo:hi]
            dyn1 = jnp.where(row == tm - 1, n0, pltpu.roll(dy, tm - 1, 0))
            dyn2 = jnp.where(row == tm - 2, n0, jnp.where(row == tm - 1, n1, pltpu.roll(dy, tm - 2, 0)))
            head_ref[:, lo:hi] = dy[0:8, :]
            du = w2 * dy + w1 * dyn1 + w0 * dyn2
            dp_ref[:, D + lo:D + hi] = (du * xin).astype(BF16)
            dp_ref[:, 2 * D + lo:2 * D + hi] = (du * c).astype(BF16)
        dh = _dot(dp_ref[:, 0:D], wit_ref[0:D, :])
        for k in range(1, 4):
            dh += _dot(dp_ref[:, k * D:(k + 1) * D], wit_ref[k * D:(k + 1) * D, :])
        xv = x_ref[...]
        inv = lax.rsqrt(jnp.mean(xv * xv, axis=-1, keepdims=True) + RMS_EPS)
        dg1_ref[...] += jnp.sum(dh * xv * inv, axis=0, keepdims=True)
        gh = dh * g_ref[...]
        gx_ref[...] = dx1 + inv * gh - xv * (inv * inv * inv * jnp.mean(gh * xv, axis=-1, keepdims=True))

    rr = lambda n: _rows(tm, n, rev=True, steps=steps)
    part = lambda k: pl.BlockSpec((tm, D), lambda i: (steps - 1 - i, k))
    prev8 = lambda k: pl.BlockSpec((8, D), lambda i: (jnp.maximum((steps - 1 - i) * halo - 1, 0), k))
    return pl.pallas_call(
        body, name="conv_bwd", grid=(steps,),
        in_specs=[rr(D), part(0), part(1), part(2), part(3), prev8(1), prev8(2), rr(D), _const((1, D)),
                  _const((8, D)), _const((D, D)), _const((4 * D, D))],
        out_specs=[rr(D), rr(4 * D), pl.BlockSpec((8, D), lambda i: (0, 0)), pl.BlockSpec((1, D), lambda i: (0, 0))],
        out_shape=[jax.ShapeDtypeStruct((s, D), F32), jax.ShapeDtypeStruct((s, 4 * D), BF16),
                   jax.ShapeDtypeStruct((8, D), F32), jax.ShapeDtypeStruct((1, D), F32)],
        scratch_shapes=[pltpu.VMEM((8, D), F32)],
        compiler_params=_params("arbitrary"),
    )(dx1, p1, p1, p1, p1, p1, p1, x, g1, cw, w1out_t, w1in_t)


def _wgrad(a, g, name):
    s, k = a.shape
    n = g.shape[1]
    bn = min(n, 1024)
    ts = min(1024, s)

    def body(a_ref, g_ref, out_ref, acc_ref):
        t = pl.program_id(1)

        @pl.when(t == 0)
        def _():
            acc_ref[...] = jnp.zeros_like(acc_ref)

        acc_ref[...] += _dot_tn(a_ref[...], g_ref[...].astype(BF16))

        @pl.when(t == s // ts - 1)
        def _():
            out_ref[...] = acc_ref[...].astype(BF16)

    return pl.pallas_call(
        body, name=name, grid=(n // bn, s // ts),
        in_specs=[pl.BlockSpec((ts, k), lambda j, t: (t, 0)), pl.BlockSpec((ts, bn), lambda j, t: (t, j))],
        out_specs=pl.BlockSpec((k, bn), lambda j, t: (0, j)),
        out_shape=jax.ShapeDtypeStruct((k, n), BF16),
        scratch_shapes=[pltpu.VMEM((k, bn), F32)],
        compiler_params=_params("arbitrary", "arbitrary"),
    )(a, g)


def _adamw_math(w, g, m, v):
    m = ADAM_B1 * m + (1.0 - ADAM_B1) * g
    v = ADAM_B2 * v + (1.0 - ADAM_B2) * (g * g)
    m_hat = m / (1.0 - ADAM_B1 ** ADAM_STEP)
    v_hat = v / (1.0 - ADAM_B2 ** ADAM_STEP)
    delta = -ADAM_LR * (m_hat / (jnp.sqrt(v_hat) + ADAM_EPS) + ADAM_WD * w)
    return delta, m, v


def _adamw_big(parts, w, m, v):
    _, r, c_ = parts.shape
    rb = 368
    assert r % rb == 0

    def body(p_ref, w_ref, m_ref, v_ref, g_ref, d_ref, mo_ref, vo_ref):
        g = p_ref[0].astype(F32)
        for k in range(1, NDEV):
            g = g + p_ref[k].astype(F32)
        g_ref[...] = g
        d_ref[...], mo_ref[...], vo_ref[...] = _adamw_math(w_ref[...], g, m_ref[...], v_ref[...])

    blk = pl.BlockSpec((rb, c_), lambda i: (i, 0))
    return pl.pallas_call(
        body, name="adamw_big", grid=(r // rb,),
        in_specs=[pl.BlockSpec((NDEV, rb, c_), lambda i: (0, i, 0)), blk, blk, blk],
        out_specs=[blk, blk, blk, blk],
        out_shape=[jax.ShapeDtypeStruct((r, c_), F32)] * 4,
        compiler_params=_params("arbitrary"),
    )(parts, w, m, v)


def _sum_parts(parts, loss_row):
    _, r, c_ = parts.shape

    def body(p_ref, o_ref, l_ref):
        g = p_ref[0]
        for k in range(1, NDEV):
            g = g + p_ref[k]
        o_ref[...] = g
        per_row = jnp.sum(o_ref[loss_row:loss_row + 8, :], axis=1, keepdims=True)
        l_ref[...] = jnp.broadcast_to(jnp.sum(per_row, axis=0, keepdims=True), (8, c_))

    return pl.pallas_call(body, name="sum_small", out_shape=[jax.ShapeDtypeStruct((r, c_), F32),
                                                             jax.ShapeDtypeStruct((8, c_), F32)])(parts)


def _adamw_small(g, w, m, v):
    def body(g_ref, w_ref, m_ref, v_ref, d_ref, mo_ref, vo_ref):
        d_ref[...], mo_ref[...], vo_ref[...] = _adamw_math(w_ref[...], g_ref[...], m_ref[...], v_ref[...])

    return pl.pallas_call(body, name="adamw_small", out_shape=[jax.ShapeDtypeStruct(g.shape, F32)] * 3)(g, w, m, v)


def _pack_big(w1in, w2in, w1out, w2out, dtype):
    parts = [w1in.reshape(ROWS_W1IN, PACK_W), w2in.reshape(ROWS_W2IN, PACK_W), w1out.reshape(ROWS_WOUT, PACK_W),
             w2out.reshape(ROWS_WOUT, PACK_W)]
    used = ROWS_W1IN + ROWS_W2IN + 2 * ROWS_WOUT
    parts.append(jnp.zeros((PACK_ROWS - used, PACK_W), parts[0].dtype))
    return jnp.concatenate(parts, axis=0).astype(dtype)


def _unpack_big(p):
    a, b, c = ROWS_W1IN, ROWS_W1IN + ROWS_W2IN, ROWS_W1IN + ROWS_W2IN + ROWS_WOUT
    return (p[:a].reshape(1, D, 512), p[a:b].reshape(1, D, 514), p[b:c].reshape(1, 128, D),
            p[c:c + ROWS_WOUT].reshape(1, 128, D))


def _pad_lanes(a):
    return jnp.pad(a, ((0, 0), (0, LANES - a.shape[1])))


def _heads(a, s):
    return a.reshape(s, H, DH).transpose(1, 0, 2)


def _unheads(a, s):
    return a.transpose(1, 0, 2).reshape(s, D)


def _aug(cols, s):
    used = sum(c.shape[-1] for c in cols)
    return jnp.concatenate(cols + [jnp.zeros((H, s, HA - used), BF16)], axis=-1)


def _terms(v):
    return [t[..., None] for t in _split3(v)]


def _tiles(a, t):
    return a.reshape(H, a.shape[1] // t, t, HA)


def _tiles_t(a, t):
    return _tiles(a, t).transpose(0, 1, 3, 2)


def kernel(x, conv_norm_g, conv_w_in, conv_w, conv_w_out, attn_norm_g, attn_w_in, attn_b_f, attn_q_norm_g, attn_k_norm_g, attn_w_out, loss_target, m_conv_norm_g, m_conv_w_in, m_conv_w, m_conv_w_out, m_attn_norm_g, m_attn_w_in, m_attn_b_f, m_attn_q_norm_g, m_attn_k_norm_g, m_attn_w_out, v_conv_norm_g, v_conv_w_in, v_conv_w, v_conv_w_out, v_attn_norm_g, v_attn_w_in, v_attn_b_f, v_attn_q_norm_g, v_attn_k_norm_g, v_attn_w_out):
    s = x.shape[1]
    tq = min(TQ, s)
    tmf, tmb = min(TM_FWD, s), min(TM_BWD, s)
    me = 4 * lax.axis_index("x") + 2 * lax.axis_index("y") + lax.axis_index("c")
    xv, tgt = x[0], loss_target[0]

    wg = _all_gather(_pack_big(conv_w_in[0], attn_w_in[0], conv_w_out[0], attn_w_out[0], BF16), "gather_weights")
    small_w = jnp.concatenate([conv_w[0], attn_norm_g, jnp.zeros((4, 128), F32)], axis=0)
    sg_ = _all_gather(small_w, "gather_small_weights")
    a, b, c = ROWS_W1IN, ROWS_W1IN + ROWS_W2IN, ROWS_W1IN + ROWS_W2IN + ROWS_WOUT
    w1in = wg[:, :a].transpose(1, 0, 2).reshape(D, 4 * D)
    w2all = wg[:, a:b].reshape(NDEV, D, 514).transpose(1, 0, 2).reshape(D, 4 * D + H)
    w2in, wf = w2all[:, :4 * D], _pad_lanes(w2all[:, 4 * D:])
    w1out = wg[:, b:c].reshape(D, D)
    w2out = wg[:, c:c + ROWS_WOUT].reshape(D, D)
    cw = jnp.concatenate([sg_[:, 0:3, :].transpose(1, 0, 2).reshape(3, D), jnp.zeros((5, D), F32)], axis=0)
    g2 = sg_[:, 3, :].reshape(1, D)
    qg_t, kg_t = jnp.tile(attn_q_norm_g, (1, H)), jnp.tile(attn_k_norm_g, (1, H))
    bf = _pad_lanes(attn_b_f)

    e = (jnp.arange(D)[:, None] // DH == jnp.arange(LANES)[None, :]).astype(BF16)
    fold = (jnp.arange(D)[:, None] % DH == jnp.arange(LANES)[None, :]).astype(BF16)
    tril = (jnp.arange(tmb)[:, None] >= jnp.arange(tmb)[None, :]).astype(BF16)
    triu = tril.T
    src = jnp.arange(3 * LANES)
    dst = jnp.arange(H * HA)[None, :] - (HA * (src % LANES) + src // LANES)[:, None]
    place_k = ((dst == COL_BIAS) & (src % LANES < H)[:, None]).astype(BF16)
    place_q = ((dst == ROW_BIAS) & (src % LANES < H)[:, None]).astype(BF16)

    x1, h1, p1, yg = _conv_fwd(xv, conv_norm_g, w1in, cw, w1out)
    reach = 1.01 * math.sqrt(DH) * jnp.max(jnp.abs(attn_q_norm_g)) * jnp.max(jnp.abs(attn_k_norm_g))
    bounded = (2.0 * reach <= BOUNDED_SOFTMAX_REACH).astype(F32).reshape(1, 1)
    group = tq * min(FWD_TILES, s // tq)
    h2, qpre, kpre, z, qa, ka, va, vat, invq, invk, cc, fl = _attn_proj_fwd(
        x1, g2, w2in, wf, bf, qg_t, kg_t, jnp.broadcast_to(reach, (1, LANES)), e, e.T, tril, place_k, place_q, group)
    ol, qab = _flash_fwd(bounded, _tiles(qa, tq), ka, vat)
    og, dx2, dz, doa, lossp = _attn_out(ol.reshape(H, s, HA), z, x1, tgt, w2out, w2out.T, e, place_k)

    dq, dkt, dvt = _flash_bwd(qab, _tiles(doa, tq), _tiles(ka, tq), _tiles(va, tq))
    dx1, dp2, df, dg2, dqg, dkg, dbf = _attn_proj_bwd(
        dq, dkt, dvt, dz, qpre, kpre, invq, invk, fl, x1, dx2, g2, qg_t, kg_t, w2in.T, wf.T, e, e.T, triu, fold)
    gx, dp1, dcw, dg1 = _conv_bwd(dx1, p1, xv, conv_norm_g, cw, w1out.T, w1in.T)
    dw2out = _wgrad(og, dx2, "wgrad_attn_out")
    dw2in = _wgrad(h2, dp2, "wgrad_attn_in")
    dwf = _wgrad(h2, df, "wgrad_attn_forget")
    dw1out = _wgrad(yg, dx1, "wgrad_conv_out")
    dw1in = _wgrad(h1, dp1, "wgrad_conv_in")

    dw2all = jnp.concatenate([dw2in, dwf[:, :H]], axis=1)
    gfull = jnp.concatenate([
        dw1in.reshape(D, NDEV, 512).transpose(1, 0, 2),
        dw2all.reshape(D, NDEV, 514).transpose(1, 0, 2).reshape(NDEV, ROWS_W2IN, PACK_W),
        dw1out.reshape(NDEV, ROWS_WOUT, PACK_W), dw2out.reshape(NDEV, ROWS_WOUT, PACK_W),
        jnp.zeros((NDEV, PACK_ROWS - c - ROWS_WOUT, PACK_W), BF16)], axis=1)
    parts = _exchange_blocks(gfull, "scatter_grads")
    pack = lambda t1, t2, t3, t4: _pack_big(t1[0], t2[0], t3[0], t4[0], F32)
    gb, db, mb, vb = _adamw_big(parts, pack(conv_w_in, attn_w_in, conv_w_out, attn_w_out),
                                pack(m_conv_w_in, m_attn_w_in, m_conv_w_out, m_attn_w_out),
                                pack(v_conv_w_in, v_attn_w_in, v_conv_w_out, v_attn_w_out))

    shard_rows = jnp.concatenate([dcw[0:3].reshape(3, NDEV, 128).transpose(1, 0, 2), dg2.reshape(NDEV, 1, 128),
                                  jnp.zeros((NDEV, 4, 128), F32)], axis=1).reshape(64, 128)
    small_g = jnp.concatenate([shard_rows, dg1.reshape(8, 128), dbf, dqg[0:1], dkg[0:1], jnp.zeros((5, 128), F32),
                               lossp.reshape(8, 128)], axis=0)
    gs, loss8 = _sum_parts(_all_gather(small_g, "gather_small_grads"), 80)
    loss = loss8[0, 0]
    mine = lax.dynamic_slice(gs, (8 * me, 0), (8, 128))
    g_small = jnp.concatenate([mine, gs[64:75], jnp.zeros((5, 128), F32)], axis=0)

    def pack_small(cwk, ang, cng, bfk, qgk, kgk):
        return jnp.concatenate([cwk[0], ang, jnp.zeros((4, 128), F32), cng.reshape(8, 128), _pad_lanes(bfk),
                                _pad_lanes(qgk), _pad_lanes(kgk), jnp.zeros((5, 128), F32)], axis=0)

    ds_, ms_, vs_ = _adamw_small(
        g_small, pack_small(conv_w, attn_norm_g, conv_norm_g, attn_b_f, attn_q_norm_g, attn_k_norm_g),
        pack_small(m_conv_w, m_attn_norm_g, m_conv_norm_g, m_attn_b_f, m_attn_q_norm_g, m_attn_k_norm_g),
        pack_small(v_conv_w, v_attn_norm_g, v_conv_norm_g, v_attn_b_f, v_attn_q_norm_g, v_attn_k_norm_g))

    def leaves(big, small):
        w1i, w2i, w1o, w2o = _unpack_big(big)
        return (small[8:16].reshape(1, D), w1i, small[0:3].reshape(1, 3, 128), w1o, small[3:4], w2i,
                small[16:17, :H], small[17:18, :DH], small[18:19, :DH], w2o)

    return (loss, gx[None], *leaves(gb, g_small), *leaves(db, ds_), *leaves(mb, ms_), *leaves(vb, vs_))
```

```python
import functools
import math

import jax
import jax.numpy as jnp
from jax import lax
from jax.experimental import pallas as pl
from jax.experimental.pallas import tpu as pltpu

F32 = jnp.float32
BF16 = jnp.bfloat16

D = 1024
H = 16
DH = 64
NDEV = 8
RMS_EPS = 1e-6
LANES = 128
HA = 128
TM_FWD = 512
TM_BWD = 256
TQ = 512
TK = 512
AUG_ROWS = 80
FWD_TILES = 8
BWD_TILES = 4
CH = 256
PACK_W = 512
PACK_ROWS = 2576
ROWS_W1IN, ROWS_W2IN, ROWS_WOUT = 1024, 1028, 256
ADAM_LR, ADAM_B1, ADAM_B2, ADAM_EPS, ADAM_WD, ADAM_STEP = 0.001, 0.9, 0.999, 1e-08, 0.01, 10
VMEM_LIMIT = 56 * 1024 * 1024
MASKED = -1e30
BOUNDED_SOFTMAX_REACH = 60.0
MESH = pl.DeviceIdType.MESH


def _params(*sem, vmem=VMEM_LIMIT):
    return pltpu.CompilerParams(dimension_semantics=sem or None, vmem_limit_bytes=vmem)


def _const(shape):
    nd = len(shape)
    return pl.BlockSpec(shape, lambda *_: (0,) * nd, pipeline_mode=pl.Buffered(1))


def _rows(tm, n, rev=False, steps=None):
    if rev:
        return pl.BlockSpec((tm, n), lambda i: (steps - 1 - i, 0))
    return pl.BlockSpec((tm, n), lambda i: (i, 0))


def _dot(a, b):
    return jnp.dot(a, b, preferred_element_type=F32)


def _top16(x):
    bits = lax.bitcast_convert_type(x, jnp.uint32) & jnp.uint32(0xFFFF0000)
    return lax.bitcast_convert_type(bits, F32)


def _split2(x):
    hi = _top16(x)
    return hi.astype(BF16), (x - hi).astype(BF16)


def _split3(x):
    hi = _top16(x)
    r = x - hi
    mid = _top16(r)
    return hi.astype(BF16), mid.astype(BF16), (r - mid).astype(BF16)


def _seg_sum(a, e):
    hi, lo = _split2(a)
    return _dot(hi, e) + _dot(lo, e)


def _seg_bcast(s, et):
    hi, lo = _split2(s)
    return _dot(hi, et) + _dot(lo, et)


def _tri_sum(t, v):
    hi, mid, lo = _split3(v)
    return _dot(t, hi) + _dot(t, mid) + _dot(t, lo)


def _sigmoid(z):
    return 1.0 / (1.0 + jnp.exp(-z))


def _place():
    return lax.axis_index("x"), lax.axis_index("y"), lax.axis_index("c")


def _all_gather(xb, name):
    r, c_ = xb.shape

    def body(x_ref, out_ref, send_sems, recv_sems, local_sem):
        x, y, c = _place()
        me, sibling = (x, y, c), (x, y, 1 - c)
        chips = [(1 - x, y), (x, 1 - y), (1 - x, 1 - y)]

        def slab(px, py, pc):
            return out_ref.at[4 * px + 2 * py + pc]

        def copy(k, block, to, src=None):
            return pltpu.make_async_remote_copy(
                src_ref=slab(*block) if src is None else src, dst_ref=slab(*block),
                send_sem=send_sems.at[k], recv_sem=recv_sems.at[k], device_id=to, device_id_type=MESH)

        mine = pltpu.make_async_copy(x_ref, slab(*me), local_sem)
        mine.start()
        first = [copy(0, me, sibling, src=x_ref)]
        first += [copy(1 + j, me, (*chip, c), src=x_ref) for j, chip in enumerate(chips)]
        for cp in first:
            cp.start()
        passed = [copy(4 + j, (*chip, c), sibling) for j, chip in enumerate(chips)]
        for j, chip in enumerate(chips):
            copy(1 + j, (*chip, c), me).wait_recv()
            passed[j].start()
        copy(0, sibling, me).wait_recv()
        for j, chip in enumerate(chips):
            copy(4 + j, (*chip, 1 - c), me).wait_recv()
        for cp in first + passed:
            cp.wait_send()
        mine.wait()

    return pl.pallas_call(
        body, name=name,
        out_shape=jax.ShapeDtypeStruct((NDEV, r, c_), xb.dtype),
        in_specs=[pl.BlockSpec(memory_space=pl.ANY)],
        out_specs=pl.BlockSpec(memory_space=pl.ANY),
        scratch_shapes=[pltpu.SemaphoreType.DMA((7,)), pltpu.SemaphoreType.DMA((7,)), pltpu.SemaphoreType.DMA],
    )(xb)


def _exchange_blocks(g, name):
    _, r, c_ = g.shape

    def body(g_ref, out_ref, send_sems, recv_sems, local_sem):
        x, y, c = _place()
        me = 4 * x + 2 * y + c
        mine = pltpu.make_async_copy(g_ref.at[me], out_ref.at[0], local_sem)
        mine.start()
        copies = []
        for k in range(1, NDEV):
            px = 1 - x if k & 4 else x
            py = 1 - y if k & 2 else y
            pc = 1 - c if k & 1 else c
            cp = pltpu.make_async_remote_copy(
                src_ref=g_ref.at[4 * px + 2 * py + pc], dst_ref=out_ref.at[k],
                send_sem=send_sems.at[k - 1], recv_sem=recv_sems.at[k - 1],
                device_id=(px, py, pc), device_id_type=MESH)
            cp.start()
            copies.append(cp)
        for cp in copies:
            cp.wait()
        mine.wait()

    return pl.pallas_call(
        body, name=name,
        out_shape=jax.ShapeDtypeStruct((NDEV, r, c_), g.dtype),
        in_specs=[pl.BlockSpec(memory_space=pl.ANY)],
        out_specs=pl.BlockSpec(memory_space=pl.ANY),
        scratch_shapes=[pltpu.SemaphoreType.DMA((7,)), pltpu.SemaphoreType.DMA((7,)), pltpu.SemaphoreType.DMA],
    )(g)


def _conv_fwd(x, g1, w1in, cw, w1out):
    s = x.shape[0]
    tm = min(TM_FWD, s)
    steps = s // tm

    def body(x_ref, g_ref, win_ref, cw_ref, wout_ref, x1_ref, h1_ref, p1_ref, yg_ref, tail_ref):
        @pl.when(pl.program_id(0) == 0)
        def _():
            tail_ref[...] = jnp.zeros_like(tail_ref)

        xv = x_ref[...]
        inv = lax.rsqrt(jnp.mean(xv * xv, axis=-1, keepdims=True) + RMS_EPS)
        h = (xv * inv * g_ref[...]).astype(BF16)
        h1_ref[...] = h
        row = lax.broadcasted_iota(jnp.int32, (tm, CH), 0)
        for ci in range(D // CH):
            lo, hi = ci * CH, (ci + 1) * CH
            parts = []
            for k in range(4):
                pk = _dot(h, win_ref[:, k * D + lo:k * D + hi]).astype(BF16)
                p1_ref[:, k * D + lo:k * D + hi] = pk
                parts.append(pk.astype(F32))
            b, c, xin, z = parts
            u = c * xin
            t6 = tail_ref[6:7, lo:hi]
            t7 = tail_ref[7:8, lo:hi]
            u1 = jnp.where(row == 0, t7, pltpu.roll(u, 1, 0))
            u2 = jnp.where(row == 0, t6, jnp.where(row == 1, t7, pltpu.roll(u, 2, 0)))
            tail_ref[:, lo:hi] = u[tm - 8:, :]
            y = cw_ref[2:3, lo:hi] * u + cw_ref[1:2, lo:hi] * u1 + cw_ref[0:1, lo:hi] * u2
            yg_ref[:, lo:hi] = (b * y * (z * _sigmoid(z))).astype(BF16)
        x1_ref[...] = xv + _dot(yg_ref[...], wout_ref[...])

    return pl.pallas_call(
        body, name="conv_fwd", grid=(steps,),
        in_specs=[_rows(tm, D), _const((1, D)), _const((D, 4 * D)), _const((8, D)), _const((D, D))],
        out_specs=[_rows(tm, D), _rows(tm, D), _rows(tm, 4 * D), _rows(tm, D)],
        out_shape=[jax.ShapeDtypeStruct((s, D), F32), jax.ShapeDtypeStruct((s, D), BF16),
                   jax.ShapeDtypeStruct((s, 4 * D), BF16), jax.ShapeDtypeStruct((s, D), BF16)],
        scratch_shapes=[pltpu.VMEM((8, D), F32)],
        compiler_params=_params("arbitrary"),
    )(x, g1, w1in, cw, w1out)


COL_BIAS = DH
ROW_BIAS = DH + 3


def _terms_cat(val):
    return jnp.concatenate(_split3(val), axis=1)


def _split_heads(x, aug, out_ref, lane, t_ref=None):
    for hp in range(H // 2):
        pair = x[:, hp * HA:(hp + 1) * HA]
        for k, feat in enumerate((pair, pltpu.roll(pair, DH, 1))):
            block = jnp.where(lane < DH, feat, aug(2 * hp + k))
            out_ref[2 * hp + k] = block.astype(BF16)
            if t_ref is not None:
                t_ref[2 * hp + k] = block.T[0:t_ref.shape[1], :].astype(BF16)


def _ones_at(lane, first):
    return jnp.where((lane >= first) & (lane < first + 3), 1.0, 0.0)


def _attn_proj_fwd(x1, g2, w2in, wf, bf, qg, kg, shift, e, et, tril, place_k, place_q, group):
    s = x1.shape[0]
    tm = min(TM_BWD, s)
    steps = s // tm
    per = group // tm

    def body(x_ref, g_ref, win_ref, wf_ref, bf_ref, qg_ref, kg_ref, sh_ref, e_ref, et_ref, tril_ref, pk_ref, pq_ref,
             h2_ref, qp_ref, kp_ref, z_ref, qa_ref, ka_ref, va_ref, vat_ref, iq_ref, ik_ref, c_ref, fl_ref, carry_ref):
        @pl.when(pl.program_id(0) == 0)
        def _():
            carry_ref[...] = jnp.zeros_like(carry_ref)

        xv = x_ref[...]
        inv = lax.rsqrt(jnp.mean(xv * xv, axis=-1, keepdims=True) + RMS_EPS)
        h = (xv * inv * g_ref[...]).astype(BF16)
        h2_ref[...] = h

        fl = _dot(h, wf_ref[...]) + bf_ref[...]
        fl_ref[...] = fl
        ex = jnp.exp(-jnp.abs(fl))
        up = 1.0 + ex
        log1p = jnp.where(up == 1.0, ex, jnp.log(up) * ex / (up - 1.0))
        lane = lax.broadcasted_iota(jnp.int32, (tm, LANES), 1)
        logf = jnp.where(lane < H, jnp.minimum(fl, 0.0) - log1p, 0.0)
        c_ref[...] = _tri_sum(tril_ref[...], logf) + carry_ref[0:1, :]
        carry_ref[0:1, :] = c_ref[tm - 1:tm, :]
        c = c_ref[...]

        def normed(col, pre_ref, inv_ref, gain_ref, scale):
            pre = _dot(h, win_ref[:, col * D:(col + 1) * D]).astype(BF16)
            pre_ref[...] = pre
            t = pre.astype(F32)
            invh = lax.rsqrt(_seg_sum(t * t, e_ref[...]) * (1.0 / DH) + RMS_EPS)
            inv_ref[...] = invh
            return t * _seg_bcast(invh, et_ref[...]) * (gain_ref[...] * scale)

        ones_col, ones_row = _ones_at(lane, COL_BIAS), _ones_at(lane, ROW_BIAS)
        q_bias = _dot(_terms_cat(c - sh_ref[...]), pq_ref[...])
        _split_heads(normed(0, qp_ref, iq_ref, qg_ref, 1.0 / math.sqrt(DH)),
                     lambda hh: q_bias[:, hh * HA:(hh + 1) * HA] + ones_col, qa_ref, lane)
        k_bias = _dot(_terms_cat(-c), pk_ref[...])
        _split_heads(normed(1, kp_ref, ik_ref, kg_ref, 1.0),
                     lambda hh: k_bias[:, hh * HA:(hh + 1) * HA] + ones_row, ka_ref, lane)
        v = _dot(h, win_ref[:, 2 * D:3 * D])
        _split_heads(v, lambda hh: ones_col, va_ref, lane, vat_ref)
        z_ref[...] = _dot(h, win_ref[:, 3 * D:4 * D]).astype(BF16)

    row_bf = lambda: _rows(tm, D)
    row_sm = lambda: _rows(tm, LANES)
    heads = lambda: pl.BlockSpec((H, tm, HA), lambda i: (0, i, 0))
    return pl.pallas_call(
        body, name="attn_proj_fwd", grid=(steps,),
        in_specs=[_rows(tm, D), _const((1, D)), _const((D, 4 * D)), _const((D, LANES)), _const((1, LANES)),
                  _const((1, D)), _const((1, D)), _const((1, LANES)), _const((D, LANES)), _const((LANES, D)),
                  _const((tm, tm)), _const((3 * LANES, H * HA)), _const((3 * LANES, H * HA))],
        out_specs=[row_bf() for _ in range(4)] + [heads() for _ in range(3)] + [
            pl.BlockSpec((H, None, AUG_ROWS, tm), lambda i: (0, i // per, 0, i % per))] + [row_sm() for _ in range(4)],
        out_shape=[jax.ShapeDtypeStruct((s, D), BF16)] * 4 + [jax.ShapeDtypeStruct((H, s, HA), BF16)] * 3 + [
            jax.ShapeDtypeStruct((H, s // group, AUG_ROWS, group), BF16)] + [jax.ShapeDtypeStruct((s, LANES), F32)] * 4,
        scratch_shapes=[pltpu.VMEM((8, LANES), F32)],
        compiler_params=_params("arbitrary"),
    )(x1, g2, w2in, wf, bf, qg, kg, shift, e, et, tril, place_k, place_q)


def _dot_nt(a, b):
    return lax.dot_general(a, b, (((1,), (1,)), ((), ())), preferred_element_type=F32)


def _dot_tn(a, b):
    return lax.dot_general(a, b, (((0,), (0,)), ((), ())), preferred_element_type=F32)


def _flash_fwd(bounded, qa, ka, vat):
    h_, nq, tq, _ = qa.shape
    s, tk = ka.shape[1], tq
    group = vat.shape[3]
    gt = group // tk
    shift = gt.bit_length() - 1

    def body(flag_ref, q_ref, k_ref, vt_ref, o_ref, qb_ref, m_ref, acct_ref):
        i = pl.program_id(1)
        lane = lax.broadcasted_iota(jnp.int32, (tq, HA), 1)
        in_bias = (lane >= ROW_BIAS) & (lane < ROW_BIAS + 3)
        acct_ref[...] = jnp.zeros_like(acct_ref)
        g = lax.shift_right_logical(i, shift)
        r = i & (gt - 1)

        def run(use_bound):
            q = q_ref[...]
            if not use_bound:
                q = jnp.where(in_bias, jnp.zeros_like(q), q)
                m_ref[...] = jnp.full_like(m_ref, MASKED)

            def step(first_key, vt, tiles, diagonal_at):
                keys = pl.ds(pl.multiple_of(first_key, tk), tiles * tk)
                zt = _dot_nt(k_ref[keys, :], q)
                if diagonal_at is not None:
                    key = lax.broadcasted_iota(jnp.int32, (tiles * tk, tq), 0)
                    qry = lax.broadcasted_iota(jnp.int32, (tiles * tk, tq), 1)
                    zt = jnp.where(key <= qry + diagonal_at * tk, zt, MASKED)
                if use_bound:
                    acct_ref[0:AUG_ROWS, :] += _dot(vt, jnp.exp(zt).astype(BF16))
                else:
                    m_old = m_ref[...]
                    m_new = jnp.maximum(m_old, jnp.max(zt, axis=0, keepdims=True))
                    pt = jnp.exp(zt - m_new)
                    acct_ref[0:AUG_ROWS, :] = jnp.exp(m_old - m_new) * acct_ref[0:AUG_ROWS, :] + _dot(
                        vt, pt.astype(BF16))
                    m_ref[...] = m_new

            def whole_group(jj, carry):
                step(jj * group, vt_ref[jj], gt, None)
                return carry

            lax.fori_loop(0, g, whole_group, 0)
            for rr in range(gt):
                pl.when(r == rr)(functools.partial(
                    lambda rr: step(g * group, vt_ref[g, :, 0:(rr + 1) * tk], rr + 1, rr), rr))

            if not use_bound:
                acct_ref[COL_BIAS + 1:COL_BIAS + 2, :] = m_ref[...]
            acc = acct_ref[...].T
            l = jnp.sum(jnp.where(lane == COL_BIAS, acc, 0.0), axis=1, keepdims=True)
            if use_bound:
                m = -jnp.sum(jnp.where(in_bias, q.astype(F32), 0.0), axis=1, keepdims=True)
            else:
                m = jnp.sum(jnp.where(lane == COL_BIAS + 1, acc, 0.0), axis=1, keepdims=True)
            lse = m + jnp.log(l)
            o_ref[...] = jnp.where(lane < DH, acc / l, lse)
            hi, mid, lo = _split3(-lse)
            qb_ref[...] = jnp.where(lane == ROW_BIAS, hi, jnp.where(lane == ROW_BIAS + 1, mid, jnp.where(
                lane == ROW_BIAS + 2, lo, q_ref[...])))

        use_bound = flag_ref[0, 0] > 0.5
        pl.when(use_bound)(lambda: run(True))
        pl.when(jnp.logical_not(use_bound))(lambda: run(False))

    tile = lambda: pl.BlockSpec((None, None, tq, HA), lambda h, i: (h, i, 0, 0))
    return pl.pallas_call(
        body, name="flash_fwd", grid=(h_, nq),
        in_specs=[pl.BlockSpec(memory_space=pltpu.SMEM), tile(), pl.BlockSpec((None, s, HA), lambda h, i: (h, 0, 0)),
                  pl.BlockSpec((None, s // group, AUG_ROWS, group), lambda h, i: (h, 0, 0, 0))],
        out_specs=[tile(), tile()],
        out_shape=[jax.ShapeDtypeStruct((h_, nq, tq, HA), F32), jax.ShapeDtypeStruct((h_, nq, tq, HA), BF16)],
        scratch_shapes=[pltpu.VMEM((1, tq), F32), pltpu.VMEM((HA, tq), F32)],
        compiler_params=_params("arbitrary", "arbitrary"),
    )(bounded, qa, ka, vat)


def _merge_heads(src, out_ref, lane):
    for hp in range(H // 2):
        out_ref[:, hp * HA:(hp + 1) * HA] = jnp.where(lane < DH, src(2 * hp), pltpu.roll(src(2 * hp + 1), DH, 1))


def _attn_out(ol, z, x1, tgt, w2out, w2out_t, e, place_k):
    s = z.shape[0]
    tm = min(TM_BWD, s)
    steps = s // tm

    def body(ol_ref, z_ref, x1_ref, t_ref, w_ref, wt_ref, e_ref, pk_ref, og_ref, dx2_ref, dz_ref, doa_ref, loss_ref,
             o_ref):
        @pl.when(pl.program_id(0) == 0)
        def _():
            loss_ref[...] = jnp.zeros_like(loss_ref)

        lane = lax.broadcasted_iota(jnp.int32, (tm, HA), 1)
        _merge_heads(lambda hh: ol_ref[hh], o_ref, lane)
        ov = o_ref[...]
        zv = z_ref[...].astype(F32)
        sg = _sigmoid(zv)
        sil = zv * sg
        og = (ov * sil).astype(BF16)
        og_ref[...] = og
        err = x1_ref[...] + _dot(og, w_ref[...]) - t_ref[...]
        loss_ref[...] += (0.5 / D) * jnp.sum(err * err, axis=0, keepdims=True)
        dx2 = err * (1.0 / D)
        dx2_ref[...] = dx2
        dog = _dot(dx2.astype(BF16), wt_ref[...])
        do = (dog * sil).astype(BF16).astype(F32)
        dz_ref[...] = (dog * ov * (sg * (1.0 + zv * (1.0 - sg)))).astype(BF16)
        delta = _seg_sum(do * ov, e_ref[...])
        d_bias = _dot(_terms_cat(-delta), pk_ref[...])
        _split_heads(do, lambda hh: d_bias[:, hh * HA:(hh + 1) * HA], doa_ref, lane)

    heads = lambda: pl.BlockSpec((H, tm, HA), lambda i: (0, i, 0))
    return pl.pallas_call(
        body, name="attn_out", grid=(steps,),
        in_specs=[heads(), _rows(tm, D), _rows(tm, D), _rows(tm, D), _const((D, D)), _const((D, D)),
                  _const((D, LANES)), _const((3 * LANES, H * HA))],
        out_specs=[_rows(tm, D), _rows(tm, D), _rows(tm, D), heads(), pl.BlockSpec((1, D), lambda i: (0, 0))],
        out_shape=[jax.ShapeDtypeStruct((s, D), BF16), jax.ShapeDtypeStruct((s, D), F32),
                   jax.ShapeDtypeStruct((s, D), BF16), jax.ShapeDtypeStruct((H, s, HA), BF16),
                   jax.ShapeDtypeStruct((1, D), F32)],
        scratch_shapes=[pltpu.VMEM((tm, D), F32)],
        compiler_params=_params("arbitrary"),
    )(ol, z, x1, tgt, w2out, w2out_t, e, place_k)


def _flash_bwd(qab, doa, ka, va):
    h_, nq, tq, _ = qab.shape
    nk, tk = ka.shape[1], ka.shape[2]
    s = nk * tk
    gt = min(BWD_TILES, nq)
    shift = gt.bit_length() - 1

    def body(qa_ref, da_ref, ka_ref, va_ref, dq_hbm, dkt_ref, dvt_ref, dq_acc, sem):
        hh = pl.program_id(0)
        j = pl.program_id(1)

        @pl.when(j == 0)
        def _():
            dq_acc[...] = jnp.zeros_like(dq_acc)

        dkt_ref[...] = jnp.zeros_like(dkt_ref)
        dvt_ref[...] = jnp.zeros_like(dvt_ref)
        ka_v = ka_ref[...]
        va_v = va_ref[...]

        def step(i, tiles, masked):
            rows = pl.ds(pl.multiple_of(i * tq, tq), tiles * tq)
            qa_i, da_i = qa_ref[rows, :], da_ref[rows, :]
            zz = _dot_nt(qa_i, ka_v)
            if masked:
                row = lax.broadcasted_iota(jnp.int32, (tiles * tq, tk), 0)
                col = lax.broadcasted_iota(jnp.int32, (tiles * tq, tk), 1)
                zz = jnp.where(col <= row, zz, MASKED)
            p = jnp.exp(zz)
            ds = (p * _dot_nt(da_i, va_v)).astype(BF16)
            pb = p.astype(BF16)
            dq_acc[rows, :] += _dot(ds, ka_v)
            dvt_ref[...] += _dot_tn(da_i, pb)
            dkt_ref[...] += _dot_tn(qa_i, ds)

        g0 = lax.shift_right_logical(j, shift)
        for rr in range(gt):
            pl.when((j & (gt - 1)) == rr)(functools.partial(step, j, gt - rr, True))

        def whole_group(g, carry):
            step(g * gt, gt, False)
            return carry

        lax.fori_loop(g0 + 1, nq // gt, whole_group, 0)

        @pl.when(j == nk - 1)
        def _():
            cp = pltpu.make_async_copy(dq_acc, dq_hbm.at[hh], sem)
            cp.start()
            cp.wait()

    whole = lambda: pl.BlockSpec((None, nq * tq, HA), lambda h, j: (h, 0, 0))
    tile = lambda: pl.BlockSpec((None, None, tk, HA), lambda h, j: (h, j, 0, 0))
    return pl.pallas_call(
        body, name="flash_bwd", grid=(h_, nk),
        in_specs=[whole(), whole(), tile(), tile()],
        out_specs=[pl.BlockSpec(memory_space=pl.ANY),
                   pl.BlockSpec((None, HA, tk), lambda h, j: (h, 0, j)),
                   pl.BlockSpec((None, HA, tk), lambda h, j: (h, 0, j))],
        out_shape=[jax.ShapeDtypeStruct((h_, s, HA), F32), jax.ShapeDtypeStruct((h_, HA, s), F32),
                   jax.ShapeDtypeStruct((h_, HA, s), F32)],
        scratch_shapes=[pltpu.VMEM((s, HA), F32), pltpu.SemaphoreType.DMA],
        compiler_params=_params("arbitrary", "arbitrary"),
    )(qab.reshape(h_, s, HA), doa.reshape(h_, s, HA), ka, va)


def _attn_proj_bwd(dq, dkt, dvt, dz, qpre, kpre, invq, invk, fl, x1, dx2, g2, qg, kg, w2in_t, wf_t, e, et,
                   triu, fold):
    s = x1.shape[0]
    tm = min(TM_BWD, s)
    steps = s // tm

    def body(dq_ref, dkt_ref, dvt_ref, dz_ref, qp_ref, kp_ref, iq_ref, ik_ref, fl_ref, x1_ref, dx2_ref,
             g2_ref, qg_ref, kg_ref, wt_ref, wft_ref, e_ref, et_ref, triu_ref, fold_ref,
             dx1_ref, dp_ref, df_ref, dg2_ref, dqg_ref, dkg_ref, dbf_ref, carry_ref, rc_ref, qcol_ref, kcol_ref,
             dqs_ref, dkn_ref, dv_ref):
        step = pl.program_id(0)
        lane = lax.broadcasted_iota(jnp.int32, (tm, HA), 1)
        _merge_heads(lambda hh: dvt_ref[hh].T, dv_ref, lane)
        dc = jnp.zeros((tm, HA), F32)
        for hp in range(H // 2):
            dqh = [dq_ref[2 * hp], dq_ref[2 * hp + 1]]
            dkh = [dkt_ref[2 * hp].T, dkt_ref[2 * hp + 1].T]
            dqs_ref[:, hp * HA:(hp + 1) * HA] = jnp.where(lane < DH, dqh[0], pltpu.roll(dqh[1], DH, 1))
            dkn_ref[:, hp * HA:(hp + 1) * HA] = jnp.where(lane < DH, dkh[0], pltpu.roll(dkh[1], DH, 1))
            for k in range(2):
                sums = jnp.where(lane == ROW_BIAS, dqh[k], 0.0) - pltpu.roll(
                    jnp.where(lane == COL_BIAS, dkh[k], 0.0), ROW_BIAS - COL_BIAS, 1)
                dc = dc + pltpu.roll(sums, (2 * hp + k - ROW_BIAS) % HA, 1)

        @pl.when(step == 0)
        def _():
            carry_ref[...] = jnp.zeros_like(carry_ref)
            dg2_ref[...] = jnp.zeros_like(dg2_ref)
            dbf_ref[...] = jnp.zeros_like(dbf_ref)
            qcol_ref[...] = jnp.zeros_like(qcol_ref)
            kcol_ref[...] = jnp.zeros_like(kcol_ref)
            dqg_ref[...] = jnp.zeros_like(dqg_ref)
            dkg_ref[...] = jnp.zeros_like(dkg_ref)

        rc_ref[...] = _tri_sum(triu_ref[...], dc) + carry_ref[0:1, :]
        carry_ref[0:1, :] = rc_ref[0:1, :]
        df = rc_ref[...] * _sigmoid(-fl_ref[...])
        dfb = df.astype(BF16)
        df_ref[...] = dfb
        dbf_ref[...] += jnp.sum(df, axis=0, keepdims=True)

        def norm_bwd(dn, pre_ref, inv_ref, gain_ref, col_ref):
            pre = pre_ref[...].astype(F32)
            invh = inv_ref[...]
            invb = _seg_bcast(invh, et_ref[...])
            col_ref[...] += jnp.sum(dn * pre * invb, axis=0, keepdims=True)
            gd = dn * gain_ref[...]
            mean = _seg_sum(gd * pre, e_ref[...]) * (1.0 / DH)
            return invb * gd - pre * _seg_bcast(mean * invh * invh * invh, et_ref[...])

        dq = norm_bwd(dqs_ref[...] * (1.0 / math.sqrt(DH)), qp_ref, iq_ref, qg_ref, qcol_ref).astype(BF16)
        dp_ref[:, 0:D] = dq
        dh = _dot(dq, wt_ref[0:D, :])
        dk = norm_bwd(dkn_ref[...], kp_ref, ik_ref, kg_ref, kcol_ref).astype(BF16)
        dp_ref[:, D:2 * D] = dk
        dh += _dot(dk, wt_ref[D:2 * D, :])
        dvb = dv_ref[...].astype(BF16)
        dp_ref[:, 2 * D:3 * D] = dvb
        dh += _dot(dvb, wt_ref[2 * D:3 * D, :])
        dzb = dz_ref[...]
        dp_ref[:, 3 * D:4 * D] = dzb
        dh += _dot(dzb, wt_ref[3 * D:4 * D, :])
        dh += _dot(dfb, wft_ref[...])

        xv = x1_ref[...]
        inv = lax.rsqrt(jnp.mean(xv * xv, axis=-1, keepdims=True) + RMS_EPS)
        dg2_ref[...] += jnp.sum(dh * xv * inv, axis=0, keepdims=True)
        gh = dh * g2_ref[...]
        dx1_ref[...] = dx2_ref[...] + inv * gh - xv * (inv * inv * inv * jnp.mean(gh * xv, axis=-1, keepdims=True))

        @pl.when(step == steps - 1)
        def _():
            dqg_ref[...] = _fold_heads(qcol_ref[...], fold_ref[...])
            dkg_ref[...] = _fold_heads(kcol_ref[...], fold_ref[...])

    rr = lambda n: _rows(tm, n, rev=True, steps=steps)
    acc = lambda n, r=1: pl.BlockSpec((r, n), lambda i: (0, 0))
    heads_t = lambda: pl.BlockSpec((H, HA, tm), lambda i: (0, 0, steps - 1 - i))
    return pl.pallas_call(
        body, name="attn_proj_bwd", grid=(steps,),
        in_specs=[pl.BlockSpec((H, tm, HA), lambda i: (0, steps - 1 - i, 0)), heads_t(), heads_t(),
                  rr(D), rr(D), rr(D), rr(LANES), rr(LANES), rr(LANES), rr(D), rr(D),
                  _const((1, D)), _const((1, D)), _const((1, D)), _const((4 * D, D)), _const((LANES, D)),
                  _const((D, LANES)), _const((LANES, D)), _const((tm, tm)), _const((D, LANES))],
        out_specs=[rr(D), rr(4 * D), rr(LANES), acc(D), acc(LANES, 8), acc(LANES, 8), acc(LANES)],
        out_shape=[jax.ShapeDtypeStruct((s, D), F32), jax.ShapeDtypeStruct((s, 4 * D), BF16),
                   jax.ShapeDtypeStruct((s, LANES), BF16), jax.ShapeDtypeStruct((1, D), F32),
                   jax.ShapeDtypeStruct((8, LANES), F32), jax.ShapeDtypeStruct((8, LANES), F32),
                   jax.ShapeDtypeStruct((1, LANES), F32)],
        scratch_shapes=[pltpu.VMEM((8, LANES), F32), pltpu.VMEM((tm, LANES), F32), pltpu.VMEM((1, D), F32),
                        pltpu.VMEM((1, D), F32), pltpu.VMEM((tm, D), F32), pltpu.VMEM((tm, D), F32),
                        pltpu.VMEM((tm, D), F32)],
        compiler_params=_params("arbitrary"),
    )(dq, dkt, dvt, dz, qpre, kpre, invq, invk, fl, x1, dx2, g2, qg, kg, w2in_t, wf_t, e, et, triu, fold)


def _fold_heads(col, fold):
    hi, mid, lo = _split3(jnp.broadcast_to(col, (8, D)))
    return _dot(hi, fold) + _dot(mid, fold) + _dot(lo, fold)


def _conv_bwd(dx1, p1, x, g1, cw, w1out_t, w1in_t):
    s = x.shape[0]
    tm = min(TM_BWD, s)
    steps = s // tm
    halo = tm // 8

    def body(dx1_ref, b_ref, c_ref, xi_ref, z_ref, ch_ref, xh_ref, x_ref, g_ref, cw_ref, wot_ref, wit_ref,
             gx_ref, dp_ref, dcw_ref, dg1_ref, head_ref):
        step = pl.program_id(0)

        @pl.when(step == 0)
        def _():
            head_ref[...] = jnp.zeros_like(head_ref)
            dcw_ref[...] = jnp.zeros_like(dcw_ref)
            dg1_ref[...] = jnp.zeros_like(dg1_ref)

        first_tile = step == steps - 1
        dx1 = dx1_ref[...]
        row = lax.broadcasted_iota(jnp.int32, (tm, CH), 0)
        dyb = dx1.astype(BF16)
        for ci in range(D // CH):
            lo, hi = ci * CH, (ci + 1) * CH
            dyg = _dot(dyb, wot_ref[:, lo:hi])
            b = b_ref[:, lo:hi].astype(F32)
            c = c_ref[:, lo:hi].astype(F32)
            xin = xi_ref[:, lo:hi].astype(F32)
            z = z_ref[:, lo:hi].astype(F32)
            u = c * xin
            t6 = jnp.where(first_tile, 0.0, ch_ref[6:7, lo:hi].astype(F32) * xh_ref[6:7, lo:hi].astype(F32))
            t7 = jnp.where(first_tile, 0.0, ch_ref[7:8, lo:hi].astype(F32) * xh_ref[7:8, lo:hi].astype(F32))
            u1 = jnp.where(row == 0, t7, pltpu.roll(u, 1, 0))
            u2 = jnp.where(row == 0, t6, jnp.where(row == 1, t7, pltpu.roll(u, 2, 0)))
            w0, w1, w2 = cw_ref[0:1, lo:hi], cw_ref[1:2, lo:hi], cw_ref[2:3, lo:hi]
            y = w2 * u + w1 * u1 + w0 * u2
            sg = _sigmoid(z)
            sil = z * sg
            dp_ref[:, lo:hi] = (dyg * y * sil).astype(BF16)
            dy = dyg * b * sil
            dp_ref[:, 3 * D + lo:3 * D + hi] = (dyg * b * y * (sg * (1.0 + z * (1.0 - sg)))).astype(BF16)
            dcw_ref[2:3, lo:hi] += jnp.sum(dy * u, axis=0, keepdims=True)
            dcw_ref[1:2, lo:hi] += jnp.sum(dy * u1, axis=0, keepdims=True)
            dcw_ref[0:1, lo:hi] += jnp.sum(dy * u2, axis=0, keepdims=True)
            n0 = head_ref[0:1, lo:hi]
            n1 = head_ref[1:2, lo:hi]
            dyn1 = jnp.where(row == tm - 1, n0, pltpu.roll(dy, tm - 1, 0))
            dyn2 = jnp.where(row == tm - 2, n0, jnp.where(row == tm - 1, n1, pltpu.roll(dy, tm - 2, 0)))
            head_ref[:, lo:hi] = dy[0:8, :]
            du = w2 * dy + w1 * dyn1 + w0 * dyn2
            dp_ref[:, D + lo:D + hi] = (du * xin).astype(BF16)
            dp_ref[:, 2 * D + lo:2 * D + hi] = (du * c).astype(BF16)
        dh = _dot(dp_ref[:, 0:D], wit_ref[0:D, :])
        for k in range(1, 4):
            dh += _dot(dp_ref[:, k * D:(k + 1) * D], wit_ref[k * D:(k + 1) * D, :])
        xv = x_ref[...]
        inv = lax.rsqrt(jnp.mean(xv * xv, axis=-1, keepdims=True) + RMS_EPS)
        dg1_ref[...] += jnp.sum(dh * xv * inv, axis=0, keepdims=True)
        gh = dh * g_ref[...]
        gx_ref[...] = dx1 + inv * gh - xv * (inv * inv * inv * jnp.mean(gh * xv, axis=-1, keepdims=True))

    rr = lambda n: _rows(tm, n, rev=True, steps=steps)
    part = lambda k: pl.BlockSpec((tm, D), lambda i: (steps - 1 - i, k))
    prev8 = lambda k: pl.BlockSpec((8, D), lambda i: (jnp.maximum((steps - 1 - i) * halo - 1, 0), k))
    return pl.pallas_call(
        body, name="conv_bwd", grid=(steps,),
        in_specs=[rr(D), part(0), part(1), part(2), part(3), prev8(1), prev8(2), rr(D), _const((1, D)),
                  _const((8, D)), _const((D, D)), _const((4 * D, D))],
        out_specs=[rr(D), rr(4 * D), pl.BlockSpec((8, D), lambda i: (0, 0)), pl.BlockSpec((1, D), lambda i: (0, 0))],
        out_shape=[jax.ShapeDtypeStruct((s, D), F32), jax.ShapeDtypeStruct((s, 4 * D), BF16),
                   jax.ShapeDtypeStruct((8, D), F32), jax.ShapeDtypeStruct((1, D), F32)],
        scratch_shapes=[pltpu.VMEM((8, D), F32)],
        compiler_params=_params("arbitrary"),
    )(dx1, p1, p1, p1, p1, p1, p1, x, g1, cw, w1out_t, w1in_t)


def _wgrad(a, g, name):
    s, k = a.shape
    n = g.shape[1]
    bn = min(n, 1024)
    ts = min(1024, s)

    def body(a_ref, g_ref, out_ref, acc_ref):
        t = pl.program_id(1)

        @pl.when(t == 0)
        def _():
            acc_ref[...] = jnp.zeros_like(acc_ref)

        acc_ref[...] += _dot_tn(a_ref[...], g_ref[...].astype(BF16))

        @pl.when(t == s // ts - 1)
        def _():
            out_ref[...] = acc_ref[...].astype(BF16)

    return pl.pallas_call(
        body, name=name, grid=(n // bn, s // ts),
        in_specs=[pl.BlockSpec((ts, k), lambda j, t: (t, 0)), pl.BlockSpec((ts, bn), lambda j, t: (t, j))],
        out_specs=pl.BlockSpec((k, bn), lambda j, t: (0, j)),
        out_shape=jax.ShapeDtypeStruct((k, n), BF16),
        scratch_shapes=[pltpu.VMEM((k, bn), F32)],
        compiler_params=_params("arbitrary", "arbitrary"),
    )(a, g)


def _adamw_math(w, g, m, v):
    m = ADAM_B1 * m + (1.0 - ADAM_B1) * g
    v = ADAM_B2 * v + (1.0 - ADAM_B2) * (g * g)
    m_hat = m / (1.0 - ADAM_B1 ** ADAM_STEP)
    v_hat = v / (1.0 - ADAM_B2 ** ADAM_STEP)
    delta = -ADAM_LR * (m_hat / (jnp.sqrt(v_hat) + ADAM_EPS) + ADAM_WD * w)
    return delta, m, v


def _adamw_big(parts, w, m, v):
    _, r, c_ = parts.shape
    rb = 368
    assert r % rb == 0

    def body(p_ref, w_ref, m_ref, v_ref, g_ref, d_ref, mo_ref, vo_ref):
        g = p_ref[0].astype(F32)
        for k in range(1, NDEV):
            g = g + p_ref[k].astype(F32)
        g_ref[...] = g
        d_ref[...], mo_ref[...], vo_ref[...] = _adamw_math(w_ref[...], g, m_ref[...], v_ref[...])

    blk = pl.BlockSpec((rb, c_), lambda i: (i, 0))
    return pl.pallas_call(
        body, name="adamw_big", grid=(r // rb,),
        in_specs=[pl.BlockSpec((NDEV, rb, c_), lambda i: (0, i, 0)), blk, blk, blk],
        out_specs=[blk, blk, blk, blk],
        out_shape=[jax.ShapeDtypeStruct((r, c_), F32)] * 4,
        compiler_params=_params("arbitrary"),
    )(parts, w, m, v)


def _sum_parts(parts, loss_row):
    _, r, c_ = parts.shape

    def body(p_ref, o_ref, l_ref):
        g = p_ref[0]
        for k in range(1, NDEV):
            g = g + p_ref[k]
        o_ref[...] = g
        per_row = jnp.sum(o_ref[loss_row:loss_row + 8, :], axis=1, keepdims=True)
        l_ref[...] = jnp.broadcast_to(jnp.sum(per_row, axis=0, keepdims=True), (8, c_))

    return pl.pallas_call(body, name="sum_small", out_shape=[jax.ShapeDtypeStruct((r, c_), F32),
                                                             jax.ShapeDtypeStruct((8, c_), F32)])(parts)


def _adamw_small(g, w, m, v):
    def body(g_ref, w_ref, m_ref, v_ref, d_ref, mo_ref, vo_ref):
        d_ref[...], mo_ref[...], vo_ref[...] = _adamw_math(w_ref[...], g_ref[...], m_ref[...], v_ref[...])

    return pl.pallas_call(body, name="adamw_small", out_shape=[jax.ShapeDtypeStruct(g.shape, F32)] * 3)(g, w, m, v)


def _pack_big(w1in, w2in, w1out, w2out, dtype):
    parts = [w1in.reshape(ROWS_W1IN, PACK_W), w2in.reshape(ROWS_W2IN, PACK_W), w1out.reshape(ROWS_WOUT, PACK_W),
             w2out.reshape(ROWS_WOUT, PACK_W)]
    used = ROWS_W1IN + ROWS_W2IN + 2 * ROWS_WOUT
    parts.append(jnp.zeros((PACK_ROWS - used, PACK_W), parts[0].dtype))
    return jnp.concatenate(parts, axis=0).astype(dtype)


def _unpack_big(p):
    a, b, c = ROWS_W1IN, ROWS_W1IN + ROWS_W2IN, ROWS_W1IN + ROWS_W2IN + ROWS_WOUT
    return (p[:a].reshape(1, D, 512), p[a:b].reshape(1, D, 514), p[b:c].reshape(1, 128, D),
            p[c:c + ROWS_WOUT].reshape(1, 128, D))


def _pad_lanes(a):
    return jnp.pad(a, ((0, 0), (0, LANES - a.shape[1])))


def _heads(a, s):
    return a.reshape(s, H, DH).transpose(1, 0, 2)


def _unheads(a, s):
    return a.transpose(1, 0, 2).reshape(s, D)


def _aug(cols, s):
    used = sum(c.shape[-1] for c in cols)
    return jnp.concatenate(cols + [jnp.zeros((H, s, HA - used), BF16)], axis=-1)


def _terms(v):
    return [t[..., None] for t in _split3(v)]


def _tiles(a, t):
    return a.reshape(H, a.shape[1] // t, t, HA)


def _tiles_t(a, t):
    return _tiles(a, t).transpose(0, 1, 3, 2)


def kernel(x, conv_norm_g, conv_w_in, conv_w, conv_w_out, attn_norm_g, attn_w_in, attn_b_f, attn_q_norm_g, attn_k_norm_g, attn_w_out, loss_target, m_conv_norm_g, m_conv_w_in, m_conv_w, m_conv_w_out, m_attn_norm_g, m_attn_w_in, m_attn_b_f, m_attn_q_norm_g, m_attn_k_norm_g, m_attn_w_out, v_conv_norm_g, v_conv_w_in, v_conv_w, v_conv_w_out, v_attn_norm_g, v_attn_w_in, v_attn_b_f, v_attn_q_norm_g, v_attn_k_norm_g, v_attn_w_out):
    s = x.shape[1]
    tq = min(TQ, s)
    tmf, tmb = min(TM_FWD, s), min(TM_BWD, s)
    me = 4 * lax.axis_index("x") + 2 * lax.axis_index("y") + lax.axis_index("c")
    xv, tgt = x[0], loss_target[0]

    wg = _all_gather(_pack_big(conv_w_in[0], attn_w_in[0], conv_w_out[0], attn_w_out[0], BF16), "gather_weights")
    small_w = jnp.concatenate([conv_w[0], attn_norm_g, jnp.zeros((4, 128), F32)], axis=0)
    sg_ = _all_gather(small_w, "gather_small_weights")
    a, b, c = ROWS_W1IN, ROWS_W1IN + ROWS_W2IN, ROWS_W1IN + ROWS_W2IN + ROWS_WOUT
    w1in = wg[:, :a].transpose(1, 0, 2).reshape(D, 4 * D)
    w2all = wg[:, a:b].reshape(NDEV, D, 514).transpose(1, 0, 2).reshape(D, 4 * D + H)
    w2in, wf = w2all[:, :4 * D], _pad_lanes(w2all[:, 4 * D:])
    w1out = wg[:, b:c].reshape(D, D)
    w2out = wg[:, c:c + ROWS_WOUT].reshape(D, D)
    cw = jnp.concatenate([sg_[:, 0:3, :].transpose(1, 0, 2).reshape(3, D), jnp.zeros((5, D), F32)], axis=0)
    g2 = sg_[:, 3, :].reshape(1, D)
    qg_t, kg_t = jnp.tile(attn_q_norm_g, (1, H)), jnp.tile(attn_k_norm_g, (1, H))
    bf = _pad_lanes(attn_b_f)

    e = (jnp.arange(D)[:, None] // DH == jnp.arange(LANES)[None, :]).astype(BF16)
    fold = (jnp.arange(D)[:, None] % DH == jnp.arange(LANES)[None, :]).astype(BF16)
    tril = (jnp.arange(tmb)[:, None] >= jnp.arange(tmb)[None, :]).astype(BF16)
    triu = tril.T
    src = jnp.arange(3 * LANES)
    dst = jnp.arange(H * HA)[None, :] - (HA * (src % LANES) + src // LANES)[:, None]
    place_k = ((dst == COL_BIAS) & (src % LANES < H)[:, None]).astype(BF16)
    place_q = ((dst == ROW_BIAS) & (src % LANES < H)[:, None]).astype(BF16)

    x1, h1, p1, yg = _conv_fwd(xv, conv_norm_g, w1in, cw, w1out)
    reach = 1.01 * math.sqrt(DH) * jnp.max(jnp.abs(attn_q_norm_g)) * jnp.max(jnp.abs(attn_k_norm_g))
    bounded = (2.0 * reach <= BOUNDED_SOFTMAX_REACH).astype(F32).reshape(1, 1)
    group = tq * min(FWD_TILES, s // tq)
    h2, qpre, kpre, z, qa, ka, va, vat, invq, invk, cc, fl = _attn_proj_fwd(
        x1, g2, w2in, wf, bf, qg_t, kg_t, jnp.broadcast_to(reach, (1, LANES)), e, e.T, tril, place_k, place_q, group)
    ol, qab = _flash_fwd(bounded, _tiles(qa, tq), ka, vat)
    og, dx2, dz, doa, lossp = _attn_out(ol.reshape(H, s, HA), z, x1, tgt, w2out, w2out.T, e, place_k)

    dq, dkt, dvt = _flash_bwd(qab, _tiles(doa, tq), _tiles(ka, tq), _tiles(va, tq))
    dx1, dp2, df, dg2, dqg, dkg, dbf = _attn_proj_bwd(
        dq, dkt, dvt, dz, qpre, kpre, invq, invk, fl, x1, dx2, g2, qg_t, kg_t, w2in.T, wf.T, e, e.T, triu, fold)
    gx, dp1, dcw, dg1 = _conv_bwd(dx1, p1, xv, conv_norm_g, cw, w1out.T, w1in.T)
    dw2out = _wgrad(og, dx2, "wgrad_attn_out")
    dw2in = _wgrad(h2, dp2, "wgrad_attn_in")
    dwf = _wgrad(h2, df, "wgrad_attn_forget")
    dw1out = _wgrad(yg, dx1, "wgrad_conv_out")
    dw1in = _wgrad(h1, dp1, "wgrad_conv_in")

    dw2all = jnp.concatenate([dw2in, dwf[:, :H]], axis=1)
    gfull = jnp.concatenate([
        dw1in.reshape(D, NDEV, 512).transpose(1, 0, 2),
        dw2all.reshape(D, NDEV, 514).transpose(1, 0, 2).reshape(NDEV, ROWS_W2IN, PACK_W),
        dw1out.reshape(NDEV, ROWS_WOUT, PACK_W), dw2out.reshape(NDEV, ROWS_WOUT, PACK_W),
        jnp.zeros((NDEV, PACK_ROWS - c - ROWS_WOUT, PACK_W), BF16)], axis=1)
    parts = _exchange_blocks(gfull, "scatter_grads")
    pack = lambda t1, t2, t3, t4: _pack_big(t1[0], t2[0], t3[0], t4[0], F32)
    gb, db, mb, vb = _adamw_big(parts, pack(conv_w_in, attn_w_in, conv_w_out, attn_w_out),
                                pack(m_conv_w_in, m_attn_w_in, m_conv_w_out, m_attn_w_out),
                                pack(v_conv_w_in, v_attn_w_in, v_conv_w_out, v_attn_w_out))

    shard_rows = jnp.concatenate([dcw[0:3].reshape(3, NDEV, 128).transpose(1, 0, 2), dg2.reshape(NDEV, 1, 128),
                                  jnp.zeros((NDEV, 4, 128), F32)], axis=1).reshape(64, 128)
    small_g = jnp.concatenate([shard_rows, dg1.reshape(8, 128), dbf, dqg[0:1], dkg[0:1], jnp.zeros((5, 128), F32),
                               lossp.reshape(8, 128)], axis=0)
    gs, loss8 = _sum_parts(_all_gather(small_g, "gather_small_grads"), 80)
    loss = loss8[0, 0]
    mine = lax.dynamic_slice(gs, (8 * me, 0), (8, 128))
    g_small = jnp.concatenate([mine, gs[64:75], jnp.zeros((5, 128), F32)], axis=0)

    def pack_small(cwk, ang, cng, bfk, qgk, kgk):
        return jnp.concatenate([cwk[0], ang, jnp.zeros((4, 128), F32), cng.reshape(8, 128), _pad_lanes(bfk),
                                _pad_lanes(qgk), _pad_lanes(kgk), jnp.zeros((5, 128), F32)], axis=0)

    ds_, ms_, vs_ = _adamw_small(
        g_small, pack_small(conv_w, attn_norm_g, conv_norm_g, attn_b_f, attn_q_norm_g, attn_k_norm_g),
        pack_small(m_conv_w, m_attn_norm_g, m_conv_norm_g, m_attn_b_f, m_attn_q_norm_g, m_attn_k_norm_g),
        pack_small(v_conv_w, v_attn_norm_g, v_conv_norm_g, v_attn_b_f, v_attn_q_norm_g, v_attn_k_norm_g))

    def leaves(big, small):
        w1i, w2i, w1o, w2o = _unpack_big(big)
        return (small[8:16].reshape(1, D), w1i, small[0:3].reshape(1, 3, 128), w1o, small[3:4], w2i,
                small[16:17, :H], small[17:18, :DH], small[18:19, :DH], w2o)

    return (loss, gx[None], *leaves(gb, g_small), *leaves(db, ds_), *leaves(mb, ms_), *leaves(vb, vs_))
```

```python
import functools
import math

import jax
import jax.numpy as jnp
from jax import lax
from jax.experimental import pallas as pl
from jax.experimental.pallas import tpu as pltpu

F32 = jnp.float32
BF16 = jnp.bfloat16

D = 1024
H = 16
DH = 64
NDEV = 8
RMS_EPS = 1e-6
LANES = 128
HA = 128
TM_FWD = 512
TM_BWD = 256
TQ = 512
TK = 512
AUG_ROWS = 80
FWD_TILES = 8
BWD_TILES = 4
CH = 256
PACK_W = 512
PACK_ROWS = 2576
ROWS_W1IN, ROWS_W2IN, ROWS_WOUT = 1024, 1028, 256
ADAM_LR, ADAM_B1, ADAM_B2, ADAM_EPS, ADAM_WD, ADAM_STEP = 0.001, 0.9, 0.999, 1e-08, 0.01, 10
VMEM_LIMIT = 56 * 1024 * 1024
MASKED = -1e30
BOUNDED_SOFTMAX_REACH = 60.0
MESH = pl.DeviceIdType.MESH


def _params(*sem, vmem=VMEM_LIMIT):
    return pltpu.CompilerParams(dimension_semantics=sem or None, vmem_limit_bytes=vmem)


def _const(shape):
    nd = len(shape)
    return pl.BlockSpec(shape, lambda *_: (0,) * nd, pipeline_mode=pl.Buffered(1))


def _rows(tm, n, rev=False, steps=None):
    if rev:
        return pl.BlockSpec((tm, n), lambda i: (steps - 1 - i, 0))
    return pl.BlockSpec((tm, n), lambda i: (i, 0))


def _dot(a, b):
    return jnp.dot(a, b, preferred_element_type=F32)


def _top16(x):
    bits = lax.bitcast_convert_type(x, jnp.uint32) & jnp.uint32(0xFFFF0000)
    return lax.bitcast_convert_type(bits, F32)


def _split2(x):
    hi = _top16(x)
    return hi.astype(BF16), (x - hi).astype(BF16)


def _split3(x):
    hi = _top16(x)
    r = x - hi
    mid = _top16(r)
    return hi.astype(BF16), mid.astype(BF16), (r - mid).astype(BF16)


def _seg_sum(a, e):
    hi, lo = _split2(a)
    return _dot(hi, e) + _dot(lo, e)


def _seg_bcast(s, et2):
    return _dot(jnp.concatenate(_split2(s), axis=1), et2)


def _tri_sum(t, v):
    hi, mid, lo = _split3(v)
    return _dot(t, hi) + _dot(t, mid) + _dot(t, lo)


def _sigmoid(z):
    return 1.0 / (1.0 + jnp.exp(-z))


def _place():
    return lax.axis_index("x"), lax.axis_index("y"), lax.axis_index("c")


def _all_gather(xb, name):
    r, c_ = xb.shape

    def body(x_ref, out_ref, send_sems, recv_sems, local_sem):
        x, y, c = _place()
        me, sibling = (x, y, c), (x, y, 1 - c)
        chips = [(1 - x, y), (x, 1 - y), (1 - x, 1 - y)]

        def slab(px, py, pc):
            return out_ref.at[4 * px + 2 * py + pc]

        def copy(k, block, to, src=None):
            return pltpu.make_async_remote_copy(
                src_ref=slab(*block) if src is None else src, dst_ref=slab(*block),
                send_sem=send_sems.at[k], recv_sem=recv_sems.at[k], device_id=to, device_id_type=MESH)

        mine = pltpu.make_async_copy(x_ref, slab(*me), local_sem)
        mine.start()
        first = [copy(0, me, sibling, src=x_ref)]
        first += [copy(1 + j, me, (*chip, c), src=x_ref) for j, chip in enumerate(chips)]
        for cp in first:
            cp.start()
        passed = [copy(4 + j, (*chip, c), sibling) for j, chip in enumerate(chips)]
        for j, chip in enumerate(chips):
            copy(1 + j, (*chip, c), me).wait_recv()
            passed[j].start()
        copy(0, sibling, me).wait_recv()
        for j, chip in enumerate(chips):
            copy(4 + j, (*chip, 1 - c), me).wait_recv()
        for cp in first + passed:
            cp.wait_send()
        mine.wait()

    return pl.pallas_call(
        body, name=name,
        out_shape=jax.ShapeDtypeStruct((NDEV, r, c_), xb.dtype),
        in_specs=[pl.BlockSpec(memory_space=pl.ANY)],
        out_specs=pl.BlockSpec(memory_space=pl.ANY),
        scratch_shapes=[pltpu.SemaphoreType.DMA((7,)), pltpu.SemaphoreType.DMA((7,)), pltpu.SemaphoreType.DMA],
    )(xb)


def _exchange_blocks(g, name):
    _, r, c_ = g.shape

    def body(g_ref, out_ref, send_sems, recv_sems, local_sem):
        x, y, c = _place()
        me = 4 * x + 2 * y + c
        mine = pltpu.make_async_copy(g_ref.at[me], out_ref.at[0], local_sem)
        mine.start()
        copies = []
        for k in range(1, NDEV):
            px = 1 - x if k & 4 else x
            py = 1 - y if k & 2 else y
            pc = 1 - c if k & 1 else c
            cp = pltpu.make_async_remote_copy(
                src_ref=g_ref.at[4 * px + 2 * py + pc], dst_ref=out_ref.at[k],
                send_sem=send_sems.at[k - 1], recv_sem=recv_sems.at[k - 1],
                device_id=(px, py, pc), device_id_type=MESH)
            cp.start()
            copies.append(cp)
        for cp in copies:
            cp.wait()
        mine.wait()

    return pl.pallas_call(
        body, name=name,
        out_shape=jax.ShapeDtypeStruct((NDEV, r, c_), g.dtype),
        in_specs=[pl.BlockSpec(memory_space=pl.ANY)],
        out_specs=pl.BlockSpec(memory_space=pl.ANY),
        scratch_shapes=[pltpu.SemaphoreType.DMA((7,)), pltpu.SemaphoreType.DMA((7,)), pltpu.SemaphoreType.DMA],
    )(g)


def _conv_fwd(x, g1, w1in, cw, w1out):
    s = x.shape[0]
    tm = min(TM_FWD, s)
    steps = s // tm

    def body(x_ref, g_ref, win_ref, cw_ref, wout_ref, x1_ref, h1_ref, p1_ref, yg_ref, tail_ref):
        @pl.when(pl.program_id(0) == 0)
        def _():
            tail_ref[...] = jnp.zeros_like(tail_ref)

        xv = x_ref[...]
        inv = lax.rsqrt(jnp.mean(xv * xv, axis=-1, keepdims=True) + RMS_EPS)
        h = (xv * inv * g_ref[...]).astype(BF16)
        h1_ref[...] = h
        row = lax.broadcasted_iota(jnp.int32, (tm, CH), 0)
        for ci in range(D // CH):
            lo, hi = ci * CH, (ci + 1) * CH
            parts = []
            for k in range(4):
                pk = _dot(h, win_ref[:, k * D + lo:k * D + hi]).astype(BF16)
                p1_ref[:, k * D + lo:k * D + hi] = pk
                parts.append(pk.astype(F32))
            b, c, xin, z = parts
            u = c * xin
            t6 = tail_ref[6:7, lo:hi]
            t7 = tail_ref[7:8, lo:hi]
            u1 = jnp.where(row == 0, t7, pltpu.roll(u, 1, 0))
            u2 = jnp.where(row == 0, t6, jnp.where(row == 1, t7, pltpu.roll(u, 2, 0)))
            tail_ref[:, lo:hi] = u[tm - 8:, :]
            y = cw_ref[2:3, lo:hi] * u + cw_ref[1:2, lo:hi] * u1 + cw_ref[0:1, lo:hi] * u2
            yg_ref[:, lo:hi] = (b * y * (z * _sigmoid(z))).astype(BF16)
        x1_ref[...] = xv + _dot(yg_ref[...], wout_ref[...])

    return pl.pallas_call(
        body, name="conv_fwd", grid=(steps,),
        in_specs=[_rows(tm, D), _const((1, D)), _const((D, 4 * D)), _const((8, D)), _const((D, D))],
        out_specs=[_rows(tm, D), _rows(tm, D), _rows(tm, 4 * D), _rows(tm, D)],
        out_shape=[jax.ShapeDtypeStruct((s, D), F32), jax.ShapeDtypeStruct((s, D), BF16),
                   jax.ShapeDtypeStruct((s, 4 * D), BF16), jax.ShapeDtypeStruct((s, D), BF16)],
        scratch_shapes=[pltpu.VMEM((8, D), F32)],
        compiler_params=_params("arbitrary"),
    )(x, g1, w1in, cw, w1out)


COL_BIAS = DH
ROW_BIAS = DH + 3


def _terms_cat(val):
    return jnp.concatenate(_split3(val), axis=1)


def _split_heads(x, aug, out_ref, lane, t_ref=None):
    for hp in range(H // 2):
        pair = x[:, hp * HA:(hp + 1) * HA]
        for k, feat in enumerate((pair, pltpu.roll(pair, DH, 1))):
            block = jnp.where(lane < DH, feat, aug(2 * hp + k))
            out_ref[2 * hp + k] = block.astype(BF16)
            if t_ref is not None:
                t_ref[2 * hp + k] = block.T[0:t_ref.shape[1], :].astype(BF16)


def _ones_at(lane, first):
    return jnp.where((lane >= first) & (lane < first + 3), 1.0, 0.0)


def _attn_proj_fwd(x1, g2, w2in, wf, bf, qg, kg, shift, e, et, tril, place_k, place_q, group):
    s = x1.shape[0]
    tm = min(TM_BWD, s)
    steps = s // tm
    per = group // tm

    def body(x_ref, g_ref, win_ref, wf_ref, bf_ref, qg_ref, kg_ref, sh_ref, e_ref, et_ref, tril_ref, pk_ref, pq_ref,
             h2_ref, qp_ref, kp_ref, z_ref, qa_ref, ka_ref, va_ref, vat_ref, iq_ref, ik_ref, c_ref, fl_ref, carry_ref):
        @pl.when(pl.program_id(0) == 0)
        def _():
            carry_ref[...] = jnp.zeros_like(carry_ref)

        xv = x_ref[...]
        inv = lax.rsqrt(jnp.mean(xv * xv, axis=-1, keepdims=True) + RMS_EPS)
        h = (xv * inv * g_ref[...]).astype(BF16)
        h2_ref[...] = h

        fl = _dot(h, wf_ref[...]) + bf_ref[...]
        fl_ref[...] = fl
        ex = jnp.exp(-jnp.abs(fl))
        up = 1.0 + ex
        log1p = jnp.where(up == 1.0, ex, jnp.log(up) * ex / (up - 1.0))
        lane = lax.broadcasted_iota(jnp.int32, (tm, LANES), 1)
        logf = jnp.where(lane < H, jnp.minimum(fl, 0.0) - log1p, 0.0)
        c_ref[...] = _tri_sum(tril_ref[...], logf) + carry_ref[0:1, :]
        carry_ref[0:1, :] = c_ref[tm - 1:tm, :]
        c = c_ref[...]

        def normed(col, pre_ref, inv_ref, gain_ref, scale):
            pre = _dot(h, win_ref[:, col * D:(col + 1) * D]).astype(BF16)
            pre_ref[...] = pre
            t = pre.astype(F32)
            invh = lax.rsqrt(_seg_sum(t * t, e_ref[...]) * (1.0 / DH) + RMS_EPS)
            inv_ref[...] = invh
            return t * _seg_bcast(invh, et_ref[...]) * (gain_ref[...] * scale)

        ones_col, ones_row = _ones_at(lane, COL_BIAS), _ones_at(lane, ROW_BIAS)
        q_bias = _dot(_terms_cat(c - sh_ref[...]), pq_ref[...])
        _split_heads(normed(0, qp_ref, iq_ref, qg_ref, 1.0 / math.sqrt(DH)),
                     lambda hh: q_bias[:, hh * HA:(hh + 1) * HA] + ones_col, qa_ref, lane)
        k_bias = _dot(_terms_cat(-c), pk_ref[...])
        _split_heads(normed(1, kp_ref, ik_ref, kg_ref, 1.0),
                     lambda hh: k_bias[:, hh * HA:(hh + 1) * HA] + ones_row, ka_ref, lane)
        v = _dot(h, win_ref[:, 2 * D:3 * D])
        _split_heads(v, lambda hh: ones_col, va_ref, lane, vat_ref)
        z_ref[...] = _dot(h, win_ref[:, 3 * D:4 * D]).astype(BF16)

    row_bf = lambda: _rows(tm, D)
    row_sm = lambda: _rows(tm, LANES)
    heads = lambda: pl.BlockSpec((H, tm, HA), lambda i: (0, i, 0))
    return pl.pallas_call(
        body, name="attn_proj_fwd", grid=(steps,),
        in_specs=[_rows(tm, D), _const((1, D)), _const((D, 4 * D)), _const((D, LANES)), _const((1, LANES)),
                  _const((1, D)), _const((1, D)), _const((1, LANES)), _const((D, LANES)), _const((2 * LANES, D)),
                  _const((tm, tm)), _const((3 * LANES, H * HA)), _const((3 * LANES, H * HA))],
        out_specs=[row_bf() for _ in range(4)] + [heads() for _ in range(3)] + [
            pl.BlockSpec((H, None, AUG_ROWS, tm), lambda i: (0, i // per, 0, i % per))] + [row_sm() for _ in range(4)],
        out_shape=[jax.ShapeDtypeStruct((s, D), BF16)] * 4 + [jax.ShapeDtypeStruct((H, s, HA), BF16)] * 3 + [
            jax.ShapeDtypeStruct((H, s // group, AUG_ROWS, group), BF16)] + [jax.ShapeDtypeStruct((s, LANES), F32)] * 4,
        scratch_shapes=[pltpu.VMEM((8, LANES), F32)],
        compiler_params=_params("arbitrary"),
    )(x1, g2, w2in, wf, bf, qg, kg, shift, e, et, tril, place_k, place_q)


def _dot_nt(a, b):
    return lax.dot_general(a, b, (((1,), (1,)), ((), ())), preferred_element_type=F32)


def _dot_tn(a, b):
    return lax.dot_general(a, b, (((0,), (0,)), ((), ())), preferred_element_type=F32)


def _flash_fwd(bounded, qa, ka, vat):
    h_, nq, tq, _ = qa.shape
    s, tk = ka.shape[1], tq
    group = vat.shape[3]
    gt = group // tk
    shift = gt.bit_length() - 1

    def body(flag_ref, q_ref, k_ref, vt_ref, o_ref, qb_ref, m_ref, acct_ref):
        i = pl.program_id(1)
        lane = lax.broadcasted_iota(jnp.int32, (tq, HA), 1)
        in_bias = (lane >= ROW_BIAS) & (lane < ROW_BIAS + 3)
        acct_ref[...] = jnp.zeros_like(acct_ref)
        g = lax.shift_right_logical(i, shift)
        r = i & (gt - 1)

        def run(use_bound):
            q = q_ref[...]
            if not use_bound:
                q = jnp.where(in_bias, jnp.zeros_like(q), q)
                m_ref[...] = jnp.full_like(m_ref, MASKED)

            def step(first_key, vt, tiles, diagonal_at):
                keys = pl.ds(pl.multiple_of(first_key, tk), tiles * tk)
                zt = _dot_nt(k_ref[keys, :], q)
                if diagonal_at is not None:
                    key = lax.broadcasted_iota(jnp.int32, (tiles * tk, tq), 0)
                    qry = lax.broadcasted_iota(jnp.int32, (tiles * tk, tq), 1)
                    zt = jnp.where(key <= qry + diagonal_at * tk, zt, MASKED)
                if use_bound:
                    acct_ref[0:AUG_ROWS, :] += _dot(vt, jnp.exp(zt).astype(BF16))
                else:
                    m_old = m_ref[...]
                    m_new = jnp.maximum(m_old, jnp.max(zt, axis=0, keepdims=True))
                    pt = jnp.exp(zt - m_new)
                    acct_ref[0:AUG_ROWS, :] = jnp.exp(m_old - m_new) * acct_ref[0:AUG_ROWS, :] + _dot(
                        vt, pt.astype(BF16))
                    m_ref[...] = m_new

            def whole_group(jj, carry):
                step(jj * group, vt_ref[jj], gt, None)
                return carry

            lax.fori_loop(0, g, whole_group, 0)
            for rr in range(gt):
                pl.when(r == rr)(functools.partial(
                    lambda rr: step(g * group, vt_ref[g, :, 0:(rr + 1) * tk], rr + 1, rr), rr))

            if not use_bound:
                acct_ref[COL_BIAS + 1:COL_BIAS + 2, :] = m_ref[...]
            acc = acct_ref[...].T
            l = jnp.sum(jnp.where(lane == COL_BIAS, acc, 0.0), axis=1, keepdims=True)
            if use_bound:
                m = -jnp.sum(jnp.where(in_bias, q.astype(F32), 0.0), axis=1, keepdims=True)
            else:
                m = jnp.sum(jnp.where(lane == COL_BIAS + 1, acc, 0.0), axis=1, keepdims=True)
            lse = m + jnp.log(l)
            o_ref[...] = jnp.where(lane < DH, acc / l, lse)
            hi, mid, lo = _split3(-lse)
            qb_ref[...] = jnp.where(lane == ROW_BIAS, hi, jnp.where(lane == ROW_BIAS + 1, mid, jnp.where(
                lane == ROW_BIAS + 2, lo, q_ref[...])))

        use_bound = flag_ref[0, 0] > 0.5
        pl.when(use_bound)(lambda: run(True))
        pl.when(jnp.logical_not(use_bound))(lambda: run(False))

    tile = lambda: pl.BlockSpec((None, None, tq, HA), lambda h, i: (h, i, 0, 0))
    return pl.pallas_call(
        body, name="flash_fwd", grid=(h_, nq),
        in_specs=[pl.BlockSpec(memory_space=pltpu.SMEM), tile(), pl.BlockSpec((None, s, HA), lambda h, i: (h, 0, 0)),
                  pl.BlockSpec((None, s // group, AUG_ROWS, group), lambda h, i: (h, 0, 0, 0))],
        out_specs=[tile(), tile()],
        out_shape=[jax.ShapeDtypeStruct((h_, nq, tq, HA), F32), jax.ShapeDtypeStruct((h_, nq, tq, HA), BF16)],
        scratch_shapes=[pltpu.VMEM((1, tq), F32), pltpu.VMEM((HA, tq), F32)],
        compiler_params=_params("arbitrary", "arbitrary"),
    )(bounded, qa, ka, vat)


def _merge_heads(src, out_ref, lane):
    for hp in range(H // 2):
        out_ref[:, hp * HA:(hp + 1) * HA] = jnp.where(lane < DH, src(2 * hp), pltpu.roll(src(2 * hp + 1), DH, 1))


def _attn_out(ol, z, x1, tgt, w2out, w2out_t, e, place_k):
    s = z.shape[0]
    tm = min(TM_BWD, s)
    steps = s // tm

    def body(ol_ref, z_ref, x1_ref, t_ref, w_ref, wt_ref, e_ref, pk_ref, og_ref, dx2_ref, dz_ref, doa_ref, loss_ref,
             o_ref):
        @pl.when(pl.program_id(0) == 0)
        def _():
            loss_ref[...] = jnp.zeros_like(loss_ref)

        lane = lax.broadcasted_iota(jnp.int32, (tm, HA), 1)
        _merge_heads(lambda hh: ol_ref[hh], o_ref, lane)
        ov = o_ref[...]
        zv = z_ref[...].astype(F32)
        sg = _sigmoid(zv)
        sil = zv * sg
        og = (ov * sil).astype(BF16)
        og_ref[...] = og
        err = x1_ref[...] + _dot(og, w_ref[...]) - t_ref[...]
        loss_ref[...] += (0.5 / D) * jnp.sum(err * err, axis=0, keepdims=True)
        dx2 = err * (1.0 / D)
        dx2_ref[...] = dx2
        dog = _dot(dx2.astype(BF16), wt_ref[...])
        do = (dog * sil).astype(BF16).astype(F32)
        dz_ref[...] = (dog * ov * (sg * (1.0 + zv * (1.0 - sg)))).astype(BF16)
        delta = _seg_sum(do * ov, e_ref[...])
        d_bias = _dot(_terms_cat(-delta), pk_ref[...])
        _split_heads(do, lambda hh: d_bias[:, hh * HA:(hh + 1) * HA], doa_ref, lane)

    heads = lambda: pl.BlockSpec((H, tm, HA), lambda i: (0, i, 0))
    return pl.pallas_call(
        body, name="attn_out", grid=(steps,),
        in_specs=[heads(), _rows(tm, D), _rows(tm, D), _rows(tm, D), _const((D, D)), _const((D, D)),
                  _const((D, LANES)), _const((3 * LANES, H * HA))],
        out_specs=[_rows(tm, D), _rows(tm, D), _rows(tm, D), heads(), pl.BlockSpec((1, D), lambda i: (0, 0))],
        out_shape=[jax.ShapeDtypeStruct((s, D), BF16), jax.ShapeDtypeStruct((s, D), F32),
                   jax.ShapeDtypeStruct((s, D), BF16), jax.ShapeDtypeStruct((H, s, HA), BF16),
                   jax.ShapeDtypeStruct((1, D), F32)],
        scratch_shapes=[pltpu.VMEM((tm, D), F32)],
        compiler_params=_params("arbitrary"),
    )(ol, z, x1, tgt, w2out, w2out_t, e, place_k)


def _flash_bwd(qab, doa, ka, va):
    h_, nq, tq, _ = qab.shape
    nk, tk = ka.shape[1], ka.shape[2]
    s = nk * tk
    gt = min(BWD_TILES, nq)
    shift = gt.bit_length() - 1

    def body(qa_ref, da_ref, ka_ref, va_ref, dq_hbm, dk_ref, dv_ref, dq_acc, dkt_ref, dvt_ref, sem):
        hh = pl.program_id(0)
        j = pl.program_id(1)

        @pl.when(j == 0)
        def _():
            dq_acc[...] = jnp.zeros_like(dq_acc)

        dkt_ref[...] = jnp.zeros_like(dkt_ref)
        dvt_ref[...] = jnp.zeros_like(dvt_ref)
        ka_v = ka_ref[...]
        va_v = va_ref[...]

        def step(i, tiles, masked):
            rows = pl.ds(pl.multiple_of(i * tq, tq), tiles * tq)
            qa_i, da_i = qa_ref[rows, :], da_ref[rows, :]
            zz = _dot_nt(qa_i, ka_v)
            if masked:
                row = lax.broadcasted_iota(jnp.int32, (tiles * tq, tk), 0)
                col = lax.broadcasted_iota(jnp.int32, (tiles * tq, tk), 1)
                zz = jnp.where(col <= row, zz, MASKED)
            p = jnp.exp(zz)
            ds = (p * _dot_nt(da_i, va_v)).astype(BF16)
            pb = p.astype(BF16)
            dq_acc[rows, :] += _dot(ds, ka_v)
            dvt_ref[...] += _dot_tn(da_i, pb)
            dkt_ref[...] += _dot_tn(qa_i, ds)

        g0 = lax.shift_right_logical(j, shift)
        for rr in range(gt):
            pl.when((j & (gt - 1)) == rr)(functools.partial(step, j, gt - rr, True))

        def whole_group(g, carry):
            step(g * gt, gt, False)
            return carry

        lax.fori_loop(g0 + 1, nq // gt, whole_group, 0)
        dk_ref[...] = dkt_ref[...].T
        dv_ref[...] = dvt_ref[...].T

        @pl.when(j == nk - 1)
        def _():
            cp = pltpu.make_async_copy(dq_acc, dq_hbm.at[hh], sem)
            cp.start()
            cp.wait()

    whole = lambda: pl.BlockSpec((None, nq * tq, HA), lambda h, j: (h, 0, 0))
    tile = lambda: pl.BlockSpec((None, None, tk, HA), lambda h, j: (h, j, 0, 0))
    rows = lambda: pl.BlockSpec((None, tk, HA), lambda h, j: (h, j, 0))
    return pl.pallas_call(
        body, name="flash_bwd", grid=(h_, nk),
        in_specs=[whole(), whole(), tile(), tile()],
        out_specs=[pl.BlockSpec(memory_space=pl.ANY), rows(), rows()],
        out_shape=[jax.ShapeDtypeStruct((h_, s, HA), F32)] * 3,
        scratch_shapes=[pltpu.VMEM((s, HA), F32), pltpu.VMEM((HA, tk), F32), pltpu.VMEM((HA, tk), F32),
                        pltpu.SemaphoreType.DMA],
        compiler_params=_params("arbitrary", "arbitrary"),
    )(qab.reshape(h_, s, HA), doa.reshape(h_, s, HA), ka, va)


def _attn_proj_bwd(dq, dk, dv, dz, qpre, kpre, invq, invk, fl, x1, dx2, g2, qg, kg, w2in_t, wf_t, e, et,
                   triu, fold):
    s = x1.shape[0]
    tm = min(TM_BWD, s)
    steps = s // tm

    def body(dq_ref, dk_ref, dvh_ref, dz_ref, qp_ref, kp_ref, iq_ref, ik_ref, fl_ref, x1_ref, dx2_ref,
             g2_ref, qg_ref, kg_ref, wt_ref, wft_ref, e_ref, et_ref, triu_ref, fold_ref,
             dx1_ref, dp_ref, df_ref, dg2_ref, dqg_ref, dkg_ref, dbf_ref, carry_ref, rc_ref, qcol_ref, kcol_ref,
             dqs_ref, dkn_ref, dv_ref):
        step = pl.program_id(0)
        lane = lax.broadcasted_iota(jnp.int32, (tm, HA), 1)
        _merge_heads(lambda hh: dvh_ref[hh], dv_ref, lane)
        dc = jnp.zeros((tm, HA), F32)
        for hp in range(H // 2):
            dqh = [dq_ref[2 * hp], dq_ref[2 * hp + 1]]
            dkh = [dk_ref[2 * hp], dk_ref[2 * hp + 1]]
            dqs_ref[:, hp * HA:(hp + 1) * HA] = jnp.where(lane < DH, dqh[0], pltpu.roll(dqh[1], DH, 1))
            dkn_ref[:, hp * HA:(hp + 1) * HA] = jnp.where(lane < DH, dkh[0], pltpu.roll(dkh[1], DH, 1))
            for k in range(2):
                sums = jnp.where(lane == ROW_BIAS, dqh[k], 0.0) - pltpu.roll(
                    jnp.where(lane == COL_BIAS, dkh[k], 0.0), ROW_BIAS - COL_BIAS, 1)
                dc = dc + pltpu.roll(sums, (2 * hp + k - ROW_BIAS) % HA, 1)

        @pl.when(step == 0)
        def _():
            carry_ref[...] = jnp.zeros_like(carry_ref)
            dg2_ref[...] = jnp.zeros_like(dg2_ref)
            dbf_ref[...] = jnp.zeros_like(dbf_ref)
            qcol_ref[...] = jnp.zeros_like(qcol_ref)
            kcol_ref[...] = jnp.zeros_like(kcol_ref)
            dqg_ref[...] = jnp.zeros_like(dqg_ref)
            dkg_ref[...] = jnp.zeros_like(dkg_ref)

        rc_ref[...] = _tri_sum(triu_ref[...], dc) + carry_ref[0:1, :]
        carry_ref[0:1, :] = rc_ref[0:1, :]
        df = rc_ref[...] * _sigmoid(-fl_ref[...])
        dfb = df.astype(BF16)
        df_ref[...] = dfb
        dbf_ref[...] += jnp.sum(df, axis=0, keepdims=True)

        def norm_bwd(dn, pre_ref, inv_ref, gain_ref, col_ref):
            pre = pre_ref[...].astype(F32)
            invh = inv_ref[...]
            invb = _seg_bcast(invh, et_ref[...])
            col_ref[...] += jnp.sum(dn * pre * invb, axis=0, keepdims=True)
            gd = dn * gain_ref[...]
            mean = _seg_sum(gd * pre, e_ref[...]) * (1.0 / DH)
            return invb * gd - pre * _seg_bcast(mean * invh * invh * invh, et_ref[...])

        dq = norm_bwd(dqs_ref[...] * (1.0 / math.sqrt(DH)), qp_ref, iq_ref, qg_ref, qcol_ref).astype(BF16)
        dp_ref[:, 0:D] = dq
        dh = _dot(dq, wt_ref[0:D, :])
        dk = norm_bwd(dkn_ref[...], kp_ref, ik_ref, kg_ref, kcol_ref).astype(BF16)
        dp_ref[:, D:2 * D] = dk
        dh += _dot(dk, wt_ref[D:2 * D, :])
        dvb = dv_ref[...].astype(BF16)
        dp_ref[:, 2 * D:3 * D] = dvb
        dh += _dot(dvb, wt_ref[2 * D:3 * D, :])
        dzb = dz_ref[...]
        dp_ref[:, 3 * D:4 * D] = dzb
        dh += _dot(dzb, wt_ref[3 * D:4 * D, :])
        dh += _dot(dfb, wft_ref[...])

        xv = x1_ref[...]
        inv = lax.rsqrt(jnp.mean(xv * xv, axis=-1, keepdims=True) + RMS_EPS)
        dg2_ref[...] += jnp.sum(dh * xv * inv, axis=0, keepdims=True)
        gh = dh * g2_ref[...]
        dx1_ref[...] = dx2_ref[...] + inv * gh - xv * (inv * inv * inv * jnp.mean(gh * xv, axis=-1, keepdims=True))

        @pl.when(step == steps - 1)
        def _():
            dqg_ref[...] = _fold_heads(qcol_ref[...], fold_ref[...])
            dkg_ref[...] = _fold_heads(kcol_ref[...], fold_ref[...])

    rr = lambda n: _rows(tm, n, rev=True, steps=steps)
    acc = lambda n, r=1: pl.BlockSpec((r, n), lambda i: (0, 0))
    heads = lambda: pl.BlockSpec((H, tm, HA), lambda i: (0, steps - 1 - i, 0))
    return pl.pallas_call(
        body, name="attn_proj_bwd", grid=(steps,),
        in_specs=[heads(), heads(), heads(),
                  rr(D), rr(D), rr(D), rr(LANES), rr(LANES), rr(LANES), rr(D), rr(D),
                  _const((1, D)), _const((1, D)), _const((1, D)), _const((4 * D, D)), _const((LANES, D)),
                  _const((D, LANES)), _const((2 * LANES, D)), _const((tm, tm)), _const((D, LANES))],
        out_specs=[rr(D), rr(4 * D), rr(LANES), acc(D), acc(LANES, 8), acc(LANES, 8), acc(LANES)],
        out_shape=[jax.ShapeDtypeStruct((s, D), F32), jax.ShapeDtypeStruct((s, 4 * D), BF16),
                   jax.ShapeDtypeStruct((s, LANES), BF16), jax.ShapeDtypeStruct((1, D), F32),
                   jax.ShapeDtypeStruct((8, LANES), F32), jax.ShapeDtypeStruct((8, LANES), F32),
                   jax.ShapeDtypeStruct((1, LANES), F32)],
        scratch_shapes=[pltpu.VMEM((8, LANES), F32), pltpu.VMEM((tm, LANES), F32), pltpu.VMEM((1, D), F32),
                        pltpu.VMEM((1, D), F32), pltpu.VMEM((tm, D), F32), pltpu.VMEM((tm, D), F32),
                        pltpu.VMEM((tm, D), F32)],
        compiler_params=_params("arbitrary"),
    )(dq, dk, dv, dz, qpre, kpre, invq, invk, fl, x1, dx2, g2, qg, kg, w2in_t, wf_t, e, et, triu, fold)


def _fold_heads(col, fold):
    hi, mid, lo = _split3(jnp.broadcast_to(col, (8, D)))
    return _dot(hi, fold) + _dot(mid, fold) + _dot(lo, fold)


def _conv_bwd(dx1, p1, x, g1, cw, w1out_t, w1in_t):
    s = x.shape[0]
    tm = min(TM_BWD, s)
    steps = s // tm
    halo = tm // 8

    def body(dx1_ref, b_ref, c_ref, xi_ref, z_ref, ch_ref, xh_ref, x_ref, g_ref, cw_ref, wot_ref, wit_ref,
             gx_ref, dp_ref, dcw_ref, dg1_ref, head_ref):
        step = pl.program_id(0)

        @pl.when(step == 0)
        def _():
            head_ref[...] = jnp.zeros_like(head_ref)
            dcw_ref[...] = jnp.zeros_like(dcw_ref)
            dg1_ref[...] = jnp.zeros_like(dg1_ref)

        first_tile = step == steps - 1
        dx1 = dx1_ref[...]
        row = lax.broadcasted_iota(jnp.int32, (tm, CH), 0)
        dyb = dx1.astype(BF16)
        for ci in range(D // CH):
            lo, hi = ci * CH, (ci + 1) * CH
            dyg = _dot(dyb, wot_ref[:, lo:hi])
            b = b_ref[:, lo:hi].astype(F32)
            c = c_ref[:, lo:hi].astype(F32)
            xin = xi_ref[:, lo:hi].astype(F32)
            z = z_ref[:, lo:hi].astype(F32)
            u = c * xin
            t6 = jnp.where(first_tile, 0.0, ch_ref[6:7, lo:hi].astype(F32) * xh_ref[6:7, lo:hi].astype(F32))
            t7 = jnp.where(first_tile, 0.0, ch_ref[7:8, lo:hi].astype(F32) * xh_ref[7:8, lo:hi].astype(F32))
            u1 = jnp.where(row == 0, t7, pltpu.roll(u, 1, 0))
            u2 = jnp.where(row == 0, t6, jnp.where(row == 1, t7, pltpu.roll(u, 2, 0)))
            w0, w1, w2 = cw_ref[0:1, lo:hi], cw_ref[1:2, lo:hi], cw_ref[2:3, lo:hi]
            y = w2 * u + w1 * u1 + w0 * u2
            sg = _sigmoid(z)
            sil = z * sg
            dp_ref[:, lo:hi] = (dyg * y * sil).astype(BF16)
            dy = dyg * b * sil
            dp_ref[:, 3 * D + lo:3 * D + hi] = (dyg * b * y * (sg * (1.0 + z * (1.0 - sg)))).astype(BF16)
            dcw_ref[2:3, lo:hi] += jnp.sum(dy * u, axis=0, keepdims=True)
            dcw_ref[1:2, lo:hi] += jnp.sum(dy * u1, axis=0, keepdims=True)
            dcw_ref[0:1, lo:hi] += jnp.sum(dy * u2, axis=0, keepdims=True)
            n0 = head_ref[0:1, lo:hi]
            n1 = head_ref[1:2, lo:hi]
            dyn1 = jnp.where(row == tm - 1, n0, pltpu.roll(dy, tm - 1, 0))
            dyn2 = jnp.where(row == tm - 2, n0, jnp.where(row == tm - 1, n1, pltpu.roll(dy, tm - 2, 0)))
            head_ref[:, lo:hi] = dy[0:8, :]
            du = w2 * dy + w1 * dyn1 + w0 * dyn2
            dp_ref[:, D + lo:D + hi] = (du * xin).astype(BF16)
            dp_ref[:, 2 * D + lo:2 * D + hi] = (du * c).astype(BF16)
        dh = _dot(dp_ref[:, 0:D], wit_ref[0:D, :])
        for k in range(1, 4):
            dh += _dot(dp_ref[:, k * D:(k + 1) * D], wit_ref[k * D:(k + 1) * D, :])
        xv = x_ref[...]
        inv = lax.rsqrt(jnp.mean(xv * xv, axis=-1, keepdims=True) + RMS_EPS)
        dg1_ref[...] += jnp.sum(dh * xv * inv, axis=0, keepdims=True)
        gh = dh * g_ref[...]
        gx_ref[...] = dx1 + inv * gh - xv * (inv * inv * inv * jnp.mean(gh * xv, axis=-1, keepdims=True))

    rr = lambda n: _rows(tm, n, rev=True, steps=steps)
    part = lambda k: pl.BlockSpec((tm, D), lambda i: (steps - 1 - i, k))
    prev8 = lambda k: pl.BlockSpec((8, D), lambda i: (jnp.maximum((steps - 1 - i) * halo - 1, 0), k))
    return pl.pallas_call(
        body, name="conv_bwd", grid=(steps,),
        in_specs=[rr(D), part(0), part(1), part(2), part(3), prev8(1), prev8(2), rr(D), _const((1, D)),
                  _const((8, D)), _const((D, D)), _const((4 * D, D))],
        out_specs=[rr(D), rr(4 * D), pl.BlockSpec((8, D), lambda i: (0, 0)), pl.BlockSpec((1, D), lambda i: (0, 0))],
        out_shape=[jax.ShapeDtypeStruct((s, D), F32), jax.ShapeDtypeStruct((s, 4 * D), BF16),
                   jax.ShapeDtypeStruct((8, D), F32), jax.ShapeDtypeStruct((1, D), F32)],
        scratch_shapes=[pltpu.VMEM((8, D), F32)],
        compiler_params=_params("arbitrary"),
    )(dx1, p1, p1, p1, p1, p1, p1, x, g1, cw, w1out_t, w1in_t)


def _wgrad(a, g, name):
    s, k = a.shape
    n = g.shape[1]
    bn = min(n, 1024)
    ts = min(1024, s)

    def body(a_ref, g_ref, out_ref, acc_ref):
        t = pl.program_id(1)

        @pl.when(t == 0)
        def _():
            acc_ref[...] = jnp.zeros_like(acc_ref)

        acc_ref[...] += _dot_tn(a_ref[...], g_ref[...].astype(BF16))

        @pl.when(t == s // ts - 1)
        def _():
            out_ref[...] = acc_ref[...].astype(BF16)

    return pl.pallas_call(
        body, name=name, grid=(n // bn, s // ts),
        in_specs=[pl.BlockSpec((ts, k), lambda j, t: (t, 0)), pl.BlockSpec((ts, bn), lambda j, t: (t, j))],
        out_specs=pl.BlockSpec((k, bn), lambda j, t: (0, j)),
        out_shape=jax.ShapeDtypeStruct((k, n), BF16),
        scratch_shapes=[pltpu.VMEM((k, bn), F32)],
        compiler_params=_params("arbitrary", "arbitrary"),
    )(a, g)


def _adamw_math(w, g, m, v):
    m = ADAM_B1 * m + (1.0 - ADAM_B1) * g
    v = ADAM_B2 * v + (1.0 - ADAM_B2) * (g * g)
    m_hat = m / (1.0 - ADAM_B1 ** ADAM_STEP)
    v_hat = v / (1.0 - ADAM_B2 ** ADAM_STEP)
    delta = -ADAM_LR * (m_hat / (jnp.sqrt(v_hat) + ADAM_EPS) + ADAM_WD * w)
    return delta, m, v


def _adamw_big(parts, w, m, v):
    _, r, c_ = parts.shape
    rb = 368
    assert r % rb == 0

    def body(p_ref, w_ref, m_ref, v_ref, g_ref, d_ref, mo_ref, vo_ref):
        g = p_ref[0].astype(F32)
        for k in range(1, NDEV):
            g = g + p_ref[k].astype(F32)
        g_ref[...] = g
        d_ref[...], mo_ref[...], vo_ref[...] = _adamw_math(w_ref[...], g, m_ref[...], v_ref[...])

    blk = pl.BlockSpec((rb, c_), lambda i: (i, 0))
    return pl.pallas_call(
        body, name="adamw_big", grid=(r // rb,),
        in_specs=[pl.BlockSpec((NDEV, rb, c_), lambda i: (0, i, 0)), blk, blk, blk],
        out_specs=[blk, blk, blk, blk],
        out_shape=[jax.ShapeDtypeStruct((r, c_), F32)] * 4,
        compiler_params=_params("arbitrary"),
    )(parts, w, m, v)


def _sum_parts(parts, loss_row):
    _, r, c_ = parts.shape

    def body(p_ref, o_ref, l_ref):
        g = p_ref[0]
        for k in range(1, NDEV):
            g = g + p_ref[k]
        o_ref[...] = g
        per_row = jnp.sum(o_ref[loss_row:loss_row + 8, :], axis=1, keepdims=True)
        l_ref[...] = jnp.broadcast_to(jnp.sum(per_row, axis=0, keepdims=True), (8, c_))

    return pl.pallas_call(body, name="sum_small", out_shape=[jax.ShapeDtypeStruct((r, c_), F32),
                                                             jax.ShapeDtypeStruct((8, c_), F32)])(parts)


def _adamw_small(g, w, m, v):
    def body(g_ref, w_ref, m_ref, v_ref, d_ref, mo_ref, vo_ref):
        d_ref[...], mo_ref[...], vo_ref[...] = _adamw_math(w_ref[...], g_ref[...], m_ref[...], v_ref[...])

    return pl.pallas_call(body, name="adamw_small", out_shape=[jax.ShapeDtypeStruct(g.shape, F32)] * 3)(g, w, m, v)


def _pack_big(w1in, w2in, w1out, w2out, dtype):
    parts = [w1in.reshape(ROWS_W1IN, PACK_W), w2in.reshape(ROWS_W2IN, PACK_W), w1out.reshape(ROWS_WOUT, PACK_W),
             w2out.reshape(ROWS_WOUT, PACK_W)]
    used = ROWS_W1IN + ROWS_W2IN + 2 * ROWS_WOUT
    parts.append(jnp.zeros((PACK_ROWS - used, PACK_W), parts[0].dtype))
    return jnp.concatenate(parts, axis=0).astype(dtype)


def _unpack_big(p):
    a, b, c = ROWS_W1IN, ROWS_W1IN + ROWS_W2IN, ROWS_W1IN + ROWS_W2IN + ROWS_WOUT
    return (p[:a].reshape(1, D, 512), p[a:b].reshape(1, D, 514), p[b:c].reshape(1, 128, D),
            p[c:c + ROWS_WOUT].reshape(1, 128, D))


def _pad_lanes(a):
    return jnp.pad(a, ((0, 0), (0, LANES - a.shape[1])))


def _heads(a, s):
    return a.reshape(s, H, DH).transpose(1, 0, 2)


def _unheads(a, s):
    return a.transpose(1, 0, 2).reshape(s, D)


def _aug(cols, s):
    used = sum(c.shape[-1] for c in cols)
    return jnp.concatenate(cols + [jnp.zeros((H, s, HA - used), BF16)], axis=-1)


def _terms(v):
    return [t[..., None] for t in _split3(v)]


def _tiles(a, t):
    return a.reshape(H, a.shape[1] // t, t, HA)


def _tiles_t(a, t):
    return _tiles(a, t).transpose(0, 1, 3, 2)


def kernel(x, conv_norm_g, conv_w_in, conv_w, conv_w_out, attn_norm_g, attn_w_in, attn_b_f, attn_q_norm_g, attn_k_norm_g, attn_w_out, loss_target, m_conv_norm_g, m_conv_w_in, m_conv_w, m_conv_w_out, m_attn_norm_g, m_attn_w_in, m_attn_b_f, m_attn_q_norm_g, m_attn_k_norm_g, m_attn_w_out, v_conv_norm_g, v_conv_w_in, v_conv_w, v_conv_w_out, v_attn_norm_g, v_attn_w_in, v_attn_b_f, v_attn_q_norm_g, v_attn_k_norm_g, v_attn_w_out):
    s = x.shape[1]
    tq = min(TQ, s)
    tmf, tmb = min(TM_FWD, s), min(TM_BWD, s)
    me = 4 * lax.axis_index("x") + 2 * lax.axis_index("y") + lax.axis_index("c")
    xv, tgt = x[0], loss_target[0]

    wg = _all_gather(_pack_big(conv_w_in[0], attn_w_in[0], conv_w_out[0], attn_w_out[0], BF16), "gather_weights")
    small_w = jnp.concatenate([conv_w[0], attn_norm_g, jnp.zeros((4, 128), F32)], axis=0)
    sg_ = _all_gather(small_w, "gather_small_weights")
    a, b, c = ROWS_W1IN, ROWS_W1IN + ROWS_W2IN, ROWS_W1IN + ROWS_W2IN + ROWS_WOUT
    w1in = wg[:, :a].transpose(1, 0, 2).reshape(D, 4 * D)
    w2all = wg[:, a:b].reshape(NDEV, D, 514).transpose(1, 0, 2).reshape(D, 4 * D + H)
    w2in, wf = w2all[:, :4 * D], _pad_lanes(w2all[:, 4 * D:])
    w1out = wg[:, b:c].reshape(D, D)
    w2out = wg[:, c:c + ROWS_WOUT].reshape(D, D)
    cw = jnp.concatenate([sg_[:, 0:3, :].transpose(1, 0, 2).reshape(3, D), jnp.zeros((5, D), F32)], axis=0)
    g2 = sg_[:, 3, :].reshape(1, D)
    qg_t, kg_t = jnp.tile(attn_q_norm_g, (1, H)), jnp.tile(attn_k_norm_g, (1, H))
    bf = _pad_lanes(attn_b_f)

    e = (jnp.arange(D)[:, None] // DH == jnp.arange(LANES)[None, :]).astype(BF16)
    et2 = jnp.concatenate([e.T, e.T], axis=0)
    fold = (jnp.arange(D)[:, None] % DH == jnp.arange(LANES)[None, :]).astype(BF16)
    tril = (jnp.arange(tmb)[:, None] >= jnp.arange(tmb)[None, :]).astype(BF16)
    triu = tril.T
    src = jnp.arange(3 * LANES)
    dst = jnp.arange(H * HA)[None, :] - (HA * (src % LANES) + src // LANES)[:, None]
    place_k = ((dst == COL_BIAS) & (src % LANES < H)[:, None]).astype(BF16)
    place_q = ((dst == ROW_BIAS) & (src % LANES < H)[:, None]).astype(BF16)

    x1, h1, p1, yg = _conv_fwd(xv, conv_norm_g, w1in, cw, w1out)
    reach = 1.01 * math.sqrt(DH) * jnp.max(jnp.abs(attn_q_norm_g)) * jnp.max(jnp.abs(attn_k_norm_g))
    bounded = (2.0 * reach <= BOUNDED_SOFTMAX_REACH).astype(F32).reshape(1, 1)
    group = tq * min(FWD_TILES, s // tq)
    h2, qpre, kpre, z, qa, ka, va, vat, invq, invk, cc, fl = _attn_proj_fwd(
        x1, g2, w2in, wf, bf, qg_t, kg_t, jnp.broadcast_to(reach, (1, LANES)), e, et2, tril, place_k, place_q, group)
    ol, qab = _flash_fwd(bounded, _tiles(qa, tq), ka, vat)
    og, dx2, dz, doa, lossp = _attn_out(ol.reshape(H, s, HA), z, x1, tgt, w2out, w2out.T, e, place_k)

    dq, dk, dv = _flash_bwd(qab, _tiles(doa, tq), _tiles(ka, tq), _tiles(va, tq))
    dx1, dp2, df, dg2, dqg, dkg, dbf = _attn_proj_bwd(
        dq, dk, dv, dz, qpre, kpre, invq, invk, fl, x1, dx2, g2, qg_t, kg_t, w2in.T, wf.T, e, et2, triu, fold)
    gx, dp1, dcw, dg1 = _conv_bwd(dx1, p1, xv, conv_norm_g, cw, w1out.T, w1in.T)
    dw2out = _wgrad(og, dx2, "wgrad_attn_out")
    dw2in = _wgrad(h2, dp2, "wgrad_attn_in")
    dwf = _wgrad(h2, df, "wgrad_attn_forget")
    dw1out = _wgrad(yg, dx1, "wgrad_conv_out")
    dw1in = _wgrad(h1, dp1, "wgrad_conv_in")

    dw2all = jnp.concatenate([dw2in, dwf[:, :H]], axis=1)
    gfull = jnp.concatenate([
        dw1in.reshape(D, NDEV, 512).transpose(1, 0, 2),
        dw2all.reshape(D, NDEV, 514).transpose(1, 0, 2).reshape(NDEV, ROWS_W2IN, PACK_W),
        dw1out.reshape(NDEV, ROWS_WOUT, PACK_W), dw2out.reshape(NDEV, ROWS_WOUT, PACK_W),
        jnp.zeros((NDEV, PACK_ROWS - c - ROWS_WOUT, PACK_W), BF16)], axis=1)
    parts = _exchange_blocks(gfull, "scatter_grads")
    pack = lambda t1, t2, t3, t4: _pack_big(t1[0], t2[0], t3[0], t4[0], F32)
    gb, db, mb, vb = _adamw_big(parts, pack(conv_w_in, attn_w_in, conv_w_out, attn_w_out),
                                pack(m_conv_w_in, m_attn_w_in, m_conv_w_out, m_attn_w_out),
                                pack(v_conv_w_in, v_attn_w_in, v_conv_w_out, v_attn_w_out))

    shard_rows = jnp.concatenate([dcw[0:3].reshape(3, NDEV, 128).transpose(1, 0, 2), dg2.reshape(NDEV, 1, 128),
                                  jnp.zeros((NDEV, 4, 128), F32)], axis=1).reshape(64, 128)
    small_g = jnp.concatenate([shard_rows, dg1.reshape(8, 128), dbf, dqg[0:1], dkg[0:1], jnp.zeros((5, 128), F32),
                               lossp.reshape(8, 128)], axis=0)
    gs, loss8 = _sum_parts(_all_gather(small_g, "gather_small_grads"), 80)
    loss = loss8[0, 0]
    mine = lax.dynamic_slice(gs, (8 * me, 0), (8, 128))
    g_small = jnp.concatenate([mine, gs[64:75], jnp.zeros((5, 128), F32)], axis=0)

    def pack_small(cwk, ang, cng, bfk, qgk, kgk):
        return jnp.concatenate([cwk[0], ang, jnp.zeros((4, 128), F32), cng.reshape(8, 128), _pad_lanes(bfk),
                                _pad_lanes(qgk), _pad_lanes(kgk), jnp.zeros((5, 128), F32)], axis=0)

    ds_, ms_, vs_ = _adamw_small(
        g_small, pack_small(conv_w, attn_norm_g, conv_norm_g, attn_b_f, attn_q_norm_g, attn_k_norm_g),
        pack_small(m_conv_w, m_attn_norm_g, m_conv_norm_g, m_attn_b_f, m_attn_q_norm_g, m_attn_k_norm_g),
        pack_small(v_conv_w, v_attn_norm_g, v_conv_norm_g, v_attn_b_f, v_attn_q_norm_g, v_attn_k_norm_g))

    def leaves(big, small):
        w1i, w2i, w1o, w2o = _unpack_big(big)
        return (small[8:16].reshape(1, D), w1i, small[0:3].reshape(1, 3, 128), w1o, small[3:4], w2i,
                small[16:17, :H], small[17:18, :DH], small[18:19, :DH], w2o)

    return (loss, gx[None], *leaves(gb, g_small), *leaves(db, ds_), *leaves(mb, ms_), *leaves(vb, vs_))
```

```python
import functools
import math

import jax
import jax.numpy as jnp
from jax import lax
from jax.experimental import pallas as pl
from jax.experimental.pallas import tpu as pltpu

F32 = jnp.float32
BF16 = jnp.bfloat16

D = 1024
H = 16
DH = 64
NDEV = 8
RMS_EPS = 1e-6
LANES = 128
HA = 128
TM_FWD = 512
TM_BWD = 256
TQ = 512
AUG_ROWS = 80
FWD_TILES = 8
BWD_TILES = 4
CH = 256
PACK_W = 512
PACK_ROWS = 2576
ROWS_W1IN, ROWS_W2IN, ROWS_WOUT = 1024, 1028, 256
ADAM_LR, ADAM_B1, ADAM_B2, ADAM_EPS, ADAM_WD, ADAM_STEP = 0.001, 0.9, 0.999, 1e-08, 0.01, 10
VMEM_LIMIT = 56 * 1024 * 1024
MASKED = -1e30
BOUNDED_SOFTMAX_REACH = 60.0
MESH = pl.DeviceIdType.MESH


def _params(*sem, vmem=VMEM_LIMIT):
    return pltpu.CompilerParams(dimension_semantics=sem or None, vmem_limit_bytes=vmem)


def _const(shape):
    nd = len(shape)
    return pl.BlockSpec(shape, lambda *_: (0,) * nd, pipeline_mode=pl.Buffered(1))


def _rows(tm, n, rev=False, steps=None):
    if rev:
        return pl.BlockSpec((tm, n), lambda i: (steps - 1 - i, 0))
    return pl.BlockSpec((tm, n), lambda i: (i, 0))


def _dot(a, b):
    return jnp.dot(a, b, preferred_element_type=F32)


def _top16(x):
    bits = lax.bitcast_convert_type(x, jnp.uint32) & jnp.uint32(0xFFFF0000)
    return lax.bitcast_convert_type(bits, F32)


def _split2(x):
    hi = _top16(x)
    return hi.astype(BF16), (x - hi).astype(BF16)


def _split3(x):
    hi = _top16(x)
    r = x - hi
    mid = _top16(r)
    return hi.astype(BF16), mid.astype(BF16), (r - mid).astype(BF16)


def _seg_sum(a, e):
    hi, lo = _split2(a)
    return _dot(hi, e) + _dot(lo, e)


def _seg_bcast(s, et2):
    return _dot(jnp.concatenate(_split2(s), axis=1), et2)


def _tri_sum(t, v):
    hi, mid, lo = _split3(v)
    return _dot(t, hi) + _dot(t, mid) + _dot(t, lo)


def _sigmoid(z):
    return 1.0 / (1.0 + jnp.exp(-z))


def _place():
    return lax.axis_index("x"), lax.axis_index("y"), lax.axis_index("c")


def _all_gather(xb, name):
    r, c_ = xb.shape

    def body(x_ref, out_ref, send_sems, recv_sems, local_sem):
        x, y, c = _place()
        me, sibling = (x, y, c), (x, y, 1 - c)
        chips = [(1 - x, y), (x, 1 - y), (1 - x, 1 - y)]

        def slab(px, py, pc):
            return out_ref.at[4 * px + 2 * py + pc]

        def copy(k, block, to, src=None):
            return pltpu.make_async_remote_copy(
                src_ref=slab(*block) if src is None else src, dst_ref=slab(*block),
                send_sem=send_sems.at[k], recv_sem=recv_sems.at[k], device_id=to, device_id_type=MESH)

        mine = pltpu.make_async_copy(x_ref, slab(*me), local_sem)
        mine.start()
        first = [copy(0, me, sibling, src=x_ref)]
        first += [copy(1 + j, me, (*chip, c), src=x_ref) for j, chip in enumerate(chips)]
        for cp in first:
            cp.start()
        passed = [copy(4 + j, (*chip, c), sibling) for j, chip in enumerate(chips)]
        for j, chip in enumerate(chips):
            copy(1 + j, (*chip, c), me).wait_recv()
            passed[j].start()
        copy(0, sibling, me).wait_recv()
        for j, chip in enumerate(chips):
            copy(4 + j, (*chip, 1 - c), me).wait_recv()
        for cp in first + passed:
            cp.wait_send()
        mine.wait()

    return pl.pallas_call(
        body, name=name,
        out_shape=jax.ShapeDtypeStruct((NDEV, r, c_), xb.dtype),
        in_specs=[pl.BlockSpec(memory_space=pl.ANY)],
        out_specs=pl.BlockSpec(memory_space=pl.ANY),
        scratch_shapes=[pltpu.SemaphoreType.DMA((7,)), pltpu.SemaphoreType.DMA((7,)), pltpu.SemaphoreType.DMA],
    )(xb)


def _exchange_blocks(g, name):
    _, r, c_ = g.shape

    def body(g_ref, out_ref, send_sems, recv_sems, local_sem):
        x, y, c = _place()
        me = 4 * x + 2 * y + c
        mine = pltpu.make_async_copy(g_ref.at[me], out_ref.at[0], local_sem)
        mine.start()
        copies = []
        for k in range(1, NDEV):
            px = 1 - x if k & 4 else x
            py = 1 - y if k & 2 else y
            pc = 1 - c if k & 1 else c
            cp = pltpu.make_async_remote_copy(
                src_ref=g_ref.at[4 * px + 2 * py + pc], dst_ref=out_ref.at[k],
                send_sem=send_sems.at[k - 1], recv_sem=recv_sems.at[k - 1],
                device_id=(px, py, pc), device_id_type=MESH)
            cp.start()
            copies.append(cp)
        for cp in copies:
            cp.wait()
        mine.wait()

    return pl.pallas_call(
        body, name=name,
        out_shape=jax.ShapeDtypeStruct((NDEV, r, c_), g.dtype),
        in_specs=[pl.BlockSpec(memory_space=pl.ANY)],
        out_specs=pl.BlockSpec(memory_space=pl.ANY),
        scratch_shapes=[pltpu.SemaphoreType.DMA((7,)), pltpu.SemaphoreType.DMA((7,)), pltpu.SemaphoreType.DMA],
    )(g)


def _conv_fwd(x, g1, w1in, cw, w1out):
    s = x.shape[0]
    tm = min(TM_FWD, s)
    steps = s // tm

    def body(x_ref, g_ref, win_ref, cw_ref, wout_ref, x1_ref, h1_ref, p1_ref, yg_ref, tail_ref):
        @pl.when(pl.program_id(0) == 0)
        def _():
            tail_ref[...] = jnp.zeros_like(tail_ref)

        xv = x_ref[...]
        inv = lax.rsqrt(jnp.mean(xv * xv, axis=-1, keepdims=True) + RMS_EPS)
        h = (xv * inv * g_ref[...]).astype(BF16)
        h1_ref[...] = h
        row = lax.broadcasted_iota(jnp.int32, (tm, CH), 0)
        for ci in range(D // CH):
            lo, hi = ci * CH, (ci + 1) * CH
            parts = []
            for k in range(4):
                pk = _dot(h, win_ref[:, k * D + lo:k * D + hi]).astype(BF16)
                p1_ref[:, k * D + lo:k * D + hi] = pk
                parts.append(pk.astype(F32))
            b, c, xin, z = parts
            u = c * xin
            t6 = tail_ref[6:7, lo:hi]
            t7 = tail_ref[7:8, lo:hi]
            u1 = jnp.where(row == 0, t7, pltpu.roll(u, 1, 0))
            u2 = jnp.where(row == 0, t6, jnp.where(row == 1, t7, pltpu.roll(u, 2, 0)))
            tail_ref[:, lo:hi] = u[tm - 8:, :]
            y = cw_ref[2:3, lo:hi] * u + cw_ref[1:2, lo:hi] * u1 + cw_ref[0:1, lo:hi] * u2
            yg_ref[:, lo:hi] = (b * y * (z * _sigmoid(z))).astype(BF16)
        x1_ref[...] = xv + _dot(yg_ref[...], wout_ref[...])

    return pl.pallas_call(
        body, name="conv_fwd", grid=(steps,),
        in_specs=[_rows(tm, D), _const((1, D)), _const((D, 4 * D)), _const((8, D)), _const((D, D))],
        out_specs=[_rows(tm, D), _rows(tm, D), _rows(tm, 4 * D), _rows(tm, D)],
        out_shape=[jax.ShapeDtypeStruct((s, D), F32), jax.ShapeDtypeStruct((s, D), BF16),
                   jax.ShapeDtypeStruct((s, 4 * D), BF16), jax.ShapeDtypeStruct((s, D), BF16)],
        scratch_shapes=[pltpu.VMEM((8, D), F32)],
        compiler_params=_params("arbitrary"),
    )(x, g1, w1in, cw, w1out)


COL_BIAS = DH
ROW_BIAS = DH + 3


def _terms_cat(val):
    return jnp.concatenate(_split3(val), axis=1)


def _split_heads(x, aug, out_ref, lane, t_ref=None):
    for hp in range(H // 2):
        pair = x[:, hp * HA:(hp + 1) * HA]
        for k, feat in enumerate((pair, pltpu.roll(pair, DH, 1))):
            block = jnp.where(lane < DH, feat, aug(2 * hp + k))
            out_ref[2 * hp + k] = block.astype(BF16)
            if t_ref is not None:
                t_ref[2 * hp + k] = block.T[0:t_ref.shape[1], :].astype(BF16)


def _ones_at(lane, first):
    return jnp.where((lane >= first) & (lane < first + 3), 1.0, 0.0)


def _attn_proj_fwd(x1, g2, w2in, wf, bf, qg, kg, shift, e, et, tril, place_k, place_q, group):
    s = x1.shape[0]
    tm = min(TM_BWD, s)
    steps = s // tm
    per = group // tm

    def body(x_ref, g_ref, win_ref, wf_ref, bf_ref, qg_ref, kg_ref, sh_ref, e_ref, et_ref, tril_ref, pk_ref, pq_ref,
             h2_ref, qp_ref, kp_ref, z_ref, qa_ref, ka_ref, va_ref, vat_ref, iq_ref, ik_ref, c_ref, fl_ref, carry_ref):
        @pl.when(pl.program_id(0) == 0)
        def _():
            carry_ref[...] = jnp.zeros_like(carry_ref)

        xv = x_ref[...]
        inv = lax.rsqrt(jnp.mean(xv * xv, axis=-1, keepdims=True) + RMS_EPS)
        h = (xv * inv * g_ref[...]).astype(BF16)
        h2_ref[...] = h

        fl = _dot(h, wf_ref[...]) + bf_ref[...]
        fl_ref[...] = fl
        ex = jnp.exp(-jnp.abs(fl))
        up = 1.0 + ex
        log1p = jnp.where(up == 1.0, ex, jnp.log(up) * ex / (up - 1.0))
        lane = lax.broadcasted_iota(jnp.int32, (tm, LANES), 1)
        logf = jnp.where(lane < H, jnp.minimum(fl, 0.0) - log1p, 0.0)
        c_ref[...] = _tri_sum(tril_ref[...], logf) + carry_ref[0:1, :]
        carry_ref[0:1, :] = c_ref[tm - 1:tm, :]
        c = c_ref[...]

        def normed(col, pre_ref, inv_ref, gain_ref, scale):
            pre = _dot(h, win_ref[:, col * D:(col + 1) * D]).astype(BF16)
            pre_ref[...] = pre
            t = pre.astype(F32)
            invh = lax.rsqrt(_seg_sum(t * t, e_ref[...]) * (1.0 / DH) + RMS_EPS)
            inv_ref[...] = invh
            return t * _seg_bcast(invh, et_ref[...]) * (gain_ref[...] * scale)

        ones_col, ones_row = _ones_at(lane, COL_BIAS), _ones_at(lane, ROW_BIAS)
        q_bias = _dot(_terms_cat(c - sh_ref[...]), pq_ref[...])
        _split_heads(normed(0, qp_ref, iq_ref, qg_ref, 1.0 / math.sqrt(DH)),
                     lambda hh: q_bias[:, hh * HA:(hh + 1) * HA] + ones_col, qa_ref, lane)
        k_bias = _dot(_terms_cat(-c), pk_ref[...])
        _split_heads(normed(1, kp_ref, ik_ref, kg_ref, 1.0),
                     lambda hh: k_bias[:, hh * HA:(hh + 1) * HA] + ones_row, ka_ref, lane)
        v = _dot(h, win_ref[:, 2 * D:3 * D])
        _split_heads(v, lambda hh: ones_col, va_ref, lane, vat_ref)
        z_ref[...] = _dot(h, win_ref[:, 3 * D:4 * D]).astype(BF16)

    row_bf = lambda: _rows(tm, D)
    row_sm = lambda: _rows(tm, LANES)
    heads = lambda: pl.BlockSpec((H, tm, HA), lambda i: (0, i, 0))
    return pl.pallas_call(
        body, name="attn_proj_fwd", grid=(steps,),
        in_specs=[_rows(tm, D), _const((1, D)), _const((D, 4 * D)), _const((D, LANES)), _const((1, LANES)),
                  _const((1, D)), _const((1, D)), _const((1, LANES)), _const((D, LANES)), _const((2 * LANES, D)),
                  _const((tm, tm)), _const((3 * LANES, H * HA)), _const((3 * LANES, H * HA))],
        out_specs=[row_bf() for _ in range(4)] + [heads() for _ in range(3)] + [
            pl.BlockSpec((H, None, AUG_ROWS, tm), lambda i: (0, i // per, 0, i % per))] + [row_sm() for _ in range(4)],
        out_shape=[jax.ShapeDtypeStruct((s, D), BF16)] * 4 + [jax.ShapeDtypeStruct((H, s, HA), BF16)] * 3 + [
            jax.ShapeDtypeStruct((H, s // group, AUG_ROWS, group), BF16)] + [jax.ShapeDtypeStruct((s, LANES), F32)] * 4,
        scratch_shapes=[pltpu.VMEM((8, LANES), F32)],
        compiler_params=_params("arbitrary"),
    )(x1, g2, w2in, wf, bf, qg, kg, shift, e, et, tril, place_k, place_q)


def _dot_nt(a, b):
    return lax.dot_general(a, b, (((1,), (1,)), ((), ())), preferred_element_type=F32)


def _dot_tn(a, b):
    return lax.dot_general(a, b, (((0,), (0,)), ((), ())), preferred_element_type=F32)


def _flash_fwd(bounded, qa, ka, vat):
    h_, nq, tq, _ = qa.shape
    s, tk = ka.shape[1], tq
    group = vat.shape[3]
    gt = group // tk
    shift = gt.bit_length() - 1

    def body(flag_ref, q_ref, k_ref, vt_ref, o_ref, qb_ref, m_ref, acct_ref):
        i = pl.program_id(1)
        lane = lax.broadcasted_iota(jnp.int32, (tq, HA), 1)
        in_bias = (lane >= ROW_BIAS) & (lane < ROW_BIAS + 3)
        acct_ref[...] = jnp.zeros_like(acct_ref)
        g = lax.shift_right_logical(i, shift)
        r = i & (gt - 1)

        def run(use_bound):
            q = q_ref[...]
            if not use_bound:
                q = jnp.where(in_bias, jnp.zeros_like(q), q)
                m_ref[...] = jnp.full_like(m_ref, MASKED)

            def step(first_key, vt, tiles, diagonal_at):
                keys = pl.ds(pl.multiple_of(first_key, tk), tiles * tk)
                zt = _dot_nt(k_ref[keys, :], q)
                if diagonal_at is not None:
                    key = lax.broadcasted_iota(jnp.int32, (tiles * tk, tq), 0)
                    qry = lax.broadcasted_iota(jnp.int32, (tiles * tk, tq), 1)
                    zt = jnp.where(key <= qry + diagonal_at * tk, zt, MASKED)
                if use_bound:
                    acct_ref[0:AUG_ROWS, :] += _dot(vt, jnp.exp(zt).astype(BF16))
                else:
                    m_old = m_ref[...]
                    m_new = jnp.maximum(m_old, jnp.max(zt, axis=0, keepdims=True))
                    pt = jnp.exp(zt - m_new)
                    acct_ref[0:AUG_ROWS, :] = jnp.exp(m_old - m_new) * acct_ref[0:AUG_ROWS, :] + _dot(
                        vt, pt.astype(BF16))
                    m_ref[...] = m_new

            def whole_group(jj, carry):
                step(jj * group, vt_ref[jj], gt, None)
                return carry

            lax.fori_loop(0, g, whole_group, 0)
            for rr in range(gt):
                pl.when(r == rr)(functools.partial(
                    lambda rr: step(g * group, vt_ref[g, :, 0:(rr + 1) * tk], rr + 1, rr), rr))

            if not use_bound:
                acct_ref[COL_BIAS + 1:COL_BIAS + 2, :] = m_ref[...]
            acc = acct_ref[...].T
            l = jnp.sum(jnp.where(lane == COL_BIAS, acc, 0.0), axis=1, keepdims=True)
            if use_bound:
                m = -jnp.sum(jnp.where(in_bias, q.astype(F32), 0.0), axis=1, keepdims=True)
            else:
                m = jnp.sum(jnp.where(lane == COL_BIAS + 1, acc, 0.0), axis=1, keepdims=True)
            lse = m + jnp.log(l)
            o_ref[...] = jnp.where(lane < DH, acc / l, lse)
            hi, mid, lo = _split3(-lse)
            qb_ref[...] = jnp.where(lane == ROW_BIAS, hi, jnp.where(lane == ROW_BIAS + 1, mid, jnp.where(
                lane == ROW_BIAS + 2, lo, q_ref[...])))

        use_bound = flag_ref[0, 0] > 0.5
        pl.when(use_bound)(lambda: run(True))
        pl.when(jnp.logical_not(use_bound))(lambda: run(False))

    tile = lambda: pl.BlockSpec((None, None, tq, HA), lambda h, i: (h, i, 0, 0))
    return pl.pallas_call(
        body, name="flash_fwd", grid=(h_, nq),
        in_specs=[pl.BlockSpec(memory_space=pltpu.SMEM), tile(), pl.BlockSpec((None, s, HA), lambda h, i: (h, 0, 0)),
                  pl.BlockSpec((None, s // group, AUG_ROWS, group), lambda h, i: (h, 0, 0, 0))],
        out_specs=[tile(), tile()],
        out_shape=[jax.ShapeDtypeStruct((h_, nq, tq, HA), F32), jax.ShapeDtypeStruct((h_, nq, tq, HA), BF16)],
        scratch_shapes=[pltpu.VMEM((1, tq), F32), pltpu.VMEM((HA, tq), F32)],
        compiler_params=_params("arbitrary", "arbitrary"),
    )(bounded, qa, ka, vat)


def _merge_heads(src, out_ref, lane):
    for hp in range(H // 2):
        out_ref[:, hp * HA:(hp + 1) * HA] = jnp.where(lane < DH, src(2 * hp), pltpu.roll(src(2 * hp + 1), DH, 1))


def _attn_out(ol, z, x1, tgt, w2out, w2out_t, e, place_k):
    s = z.shape[0]
    tm = min(TM_BWD, s)
    steps = s // tm

    def body(ol_ref, z_ref, x1_ref, t_ref, w_ref, wt_ref, e_ref, pk_ref, og_ref, dx2_ref, dz_ref, doa_ref, loss_ref,
             o_ref):
        @pl.when(pl.program_id(0) == 0)
        def _():
            loss_ref[...] = jnp.zeros_like(loss_ref)

        lane = lax.broadcasted_iota(jnp.int32, (tm, HA), 1)
        _merge_heads(lambda hh: ol_ref[hh], o_ref, lane)
        ov = o_ref[...]
        zv = z_ref[...].astype(F32)
        sg = _sigmoid(zv)
        sil = zv * sg
        og = (ov * sil).astype(BF16)
        og_ref[...] = og
        err = x1_ref[...] + _dot(og, w_ref[...]) - t_ref[...]
        loss_ref[...] += (0.5 / D) * jnp.sum(err * err, axis=0, keepdims=True)
        dx2 = err * (1.0 / D)
        dx2_ref[...] = dx2
        dog = _dot(dx2.astype(BF16), wt_ref[...])
        do = (dog * sil).astype(BF16).astype(F32)
        dz_ref[...] = (dog * ov * (sg * (1.0 + zv * (1.0 - sg)))).astype(BF16)
        delta = _seg_sum(do * ov, e_ref[...])
        d_bias = _dot(_terms_cat(-delta), pk_ref[...])
        _split_heads(do, lambda hh: d_bias[:, hh * HA:(hh + 1) * HA], doa_ref, lane)

    heads = lambda: pl.BlockSpec((H, tm, HA), lambda i: (0, i, 0))
    return pl.pallas_call(
        body, name="attn_out", grid=(steps,),
        in_specs=[heads(), _rows(tm, D), _rows(tm, D), _rows(tm, D), _const((D, D)), _const((D, D)),
                  _const((D, LANES)), _const((3 * LANES, H * HA))],
        out_specs=[_rows(tm, D), _rows(tm, D), _rows(tm, D), heads(), pl.BlockSpec((1, D), lambda i: (0, 0))],
        out_shape=[jax.ShapeDtypeStruct((s, D), BF16), jax.ShapeDtypeStruct((s, D), F32),
                   jax.ShapeDtypeStruct((s, D), BF16), jax.ShapeDtypeStruct((H, s, HA), BF16),
                   jax.ShapeDtypeStruct((1, D), F32)],
        scratch_shapes=[pltpu.VMEM((tm, D), F32)],
        compiler_params=_params("arbitrary"),
    )(ol, z, x1, tgt, w2out, w2out_t, e, place_k)


def _flash_bwd(qab, doa, ka, va):
    h_, nq, tq, _ = qab.shape
    nk, tk = ka.shape[1], ka.shape[2]
    s = nk * tk
    gt = min(BWD_TILES, nq)
    shift = gt.bit_length() - 1

    def body(qa_ref, da_ref, ka_ref, va_ref, dq_hbm, dk_ref, dv_ref, dq_acc, dkt_ref, dvt_ref, sem):
        hh = pl.program_id(0)
        j = pl.program_id(1)

        @pl.when(j == 0)
        def _():
            dq_acc[...] = jnp.zeros_like(dq_acc)

        dkt_ref[...] = jnp.zeros_like(dkt_ref)
        dvt_ref[...] = jnp.zeros_like(dvt_ref)
        ka_v = ka_ref[...]
        va_v = va_ref[...]

        def step(i, tiles, masked):
            rows = pl.ds(pl.multiple_of(i * tq, tq), tiles * tq)
            qa_i, da_i = qa_ref[rows, :], da_ref[rows, :]
            zz = _dot_nt(qa_i, ka_v)
            if masked:
                row = lax.broadcasted_iota(jnp.int32, (tiles * tq, tk), 0)
                col = lax.broadcasted_iota(jnp.int32, (tiles * tq, tk), 1)
                zz = jnp.where(col <= row, zz, MASKED)
            p = jnp.exp(zz)
            ds = (p * _dot_nt(da_i, va_v)).astype(BF16)
            pb = p.astype(BF16)
            dq_acc[rows, :] += _dot(ds, ka_v)
            dvt_ref[...] += _dot_tn(da_i, pb)
            dkt_ref[...] += _dot_tn(qa_i, ds)

        g0 = lax.shift_right_logical(j, shift)
        for rr in range(gt):
            pl.when((j & (gt - 1)) == rr)(functools.partial(step, j, gt - rr, True))

        def whole_group(g, carry):
            step(g * gt, gt, False)
            return carry

        lax.fori_loop(g0 + 1, nq // gt, whole_group, 0)
        dk_ref[...] = dkt_ref[...].T
        dv_ref[...] = dvt_ref[...].T.astype(BF16)

        @pl.when(j == nk - 1)
        def _():
            cp = pltpu.make_async_copy(dq_acc, dq_hbm.at[hh], sem)
            cp.start()
            cp.wait()

    whole = lambda: pl.BlockSpec((None, nq * tq, HA), lambda h, j: (h, 0, 0))
    tile = lambda: pl.BlockSpec((None, None, tk, HA), lambda h, j: (h, j, 0, 0))
    rows = lambda: pl.BlockSpec((None, tk, HA), lambda h, j: (h, j, 0))
    return pl.pallas_call(
        body, name="flash_bwd", grid=(h_, nk),
        in_specs=[whole(), whole(), tile(), tile()],
        out_specs=[pl.BlockSpec(memory_space=pl.ANY), rows(), rows()],
        out_shape=[jax.ShapeDtypeStruct((h_, s, HA), F32)] * 2 + [jax.ShapeDtypeStruct((h_, s, HA), BF16)],
        scratch_shapes=[pltpu.VMEM((s, HA), F32), pltpu.VMEM((HA, tk), F32), pltpu.VMEM((HA, tk), F32),
                        pltpu.SemaphoreType.DMA],
        compiler_params=_params("arbitrary", "arbitrary"),
    )(qab.reshape(h_, s, HA), doa.reshape(h_, s, HA), ka, va)


def _attn_proj_bwd(dq, dk, dv, dz, qpre, kpre, invq, invk, fl, x1, dx2, g2, qg, kg, w2in_t, wf_t, e, et,
                   triu, fold):
    s = x1.shape[0]
    tm = min(TM_BWD, s)
    steps = s // tm

    def body(dq_ref, dk_ref, dvh_ref, dz_ref, qp_ref, kp_ref, iq_ref, ik_ref, fl_ref, x1_ref, dx2_ref,
             g2_ref, qg_ref, kg_ref, wt_ref, wft_ref, e_ref, et_ref, triu_ref, fold_ref,
             dx1_ref, dp_ref, df_ref, dg2_ref, dqg_ref, dkg_ref, dbf_ref, carry_ref, rc_ref, qcol_ref, kcol_ref,
             dqs_ref, dkn_ref, dv_ref):
        step = pl.program_id(0)
        lane = lax.broadcasted_iota(jnp.int32, (tm, HA), 1)
        _merge_heads(lambda hh: dvh_ref[hh].astype(F32), dv_ref, lane)
        dc = jnp.zeros((tm, HA), F32)
        for hp in range(H // 2):
            dqh = [dq_ref[2 * hp], dq_ref[2 * hp + 1]]
            dkh = [dk_ref[2 * hp], dk_ref[2 * hp + 1]]
            dqs_ref[:, hp * HA:(hp + 1) * HA] = jnp.where(lane < DH, dqh[0], pltpu.roll(dqh[1], DH, 1))
            dkn_ref[:, hp * HA:(hp + 1) * HA] = jnp.where(lane < DH, dkh[0], pltpu.roll(dkh[1], DH, 1))
            for k in range(2):
                sums = jnp.where(lane == ROW_BIAS, dqh[k], 0.0) - pltpu.roll(
                    jnp.where(lane == COL_BIAS, dkh[k], 0.0), ROW_BIAS - COL_BIAS, 1)
                dc = dc + pltpu.roll(sums, (2 * hp + k - ROW_BIAS) % HA, 1)

        @pl.when(step == 0)
        def _():
            carry_ref[...] = jnp.zeros_like(carry_ref)
            dg2_ref[...] = jnp.zeros_like(dg2_ref)
            dbf_ref[...] = jnp.zeros_like(dbf_ref)
            qcol_ref[...] = jnp.zeros_like(qcol_ref)
            kcol_ref[...] = jnp.zeros_like(kcol_ref)
            dqg_ref[...] = jnp.zeros_like(dqg_ref)
            dkg_ref[...] = jnp.zeros_like(dkg_ref)

        rc_ref[...] = _tri_sum(triu_ref[...], dc) + carry_ref[0:1, :]
        carry_ref[0:1, :] = rc_ref[0:1, :]
        df = rc_ref[...] * _sigmoid(-fl_ref[...])
        dfb = df.astype(BF16)
        df_ref[...] = dfb
        dbf_ref[...] += jnp.sum(df, axis=0, keepdims=True)

        def norm_bwd(dn, pre_ref, inv_ref, gain_ref, col_ref):
            pre = pre_ref[...].astype(F32)
            invh = inv_ref[...]
            invb = _seg_bcast(invh, et_ref[...])
            col_ref[...] += jnp.sum(dn * pre * invb, axis=0, keepdims=True)
            gd = dn * gain_ref[...]
            mean = _seg_sum(gd * pre, e_ref[...]) * (1.0 / DH)
            return invb * gd - pre * _seg_bcast(mean * invh * invh * invh, et_ref[...])

        dq = norm_bwd(dqs_ref[...] * (1.0 / math.sqrt(DH)), qp_ref, iq_ref, qg_ref, qcol_ref).astype(BF16)
        dp_ref[:, 0:D] = dq
        dh = _dot(dq, wt_ref[0:D, :])
        dk = norm_bwd(dkn_ref[...], kp_ref, ik_ref, kg_ref, kcol_ref).astype(BF16)
        dp_ref[:, D:2 * D] = dk
        dh += _dot(dk, wt_ref[D:2 * D, :])
        dvb = dv_ref[...].astype(BF16)
        dp_ref[:, 2 * D:3 * D] = dvb
        dh += _dot(dvb, wt_ref[2 * D:3 * D, :])
        dzb = dz_ref[...]
        dp_ref[:, 3 * D:4 * D] = dzb
        dh += _dot(dzb, wt_ref[3 * D:4 * D, :])
        dh += _dot(dfb, wft_ref[...])

        xv = x1_ref[...]
        inv = lax.rsqrt(jnp.mean(xv * xv, axis=-1, keepdims=True) + RMS_EPS)
        dg2_ref[...] += jnp.sum(dh * xv * inv, axis=0, keepdims=True)
        gh = dh * g2_ref[...]
        dx1_ref[...] = dx2_ref[...] + inv * gh - xv * (inv * inv * inv * jnp.mean(gh * xv, axis=-1, keepdims=True))

        @pl.when(step == steps - 1)
        def _():
            dqg_ref[...] = _fold_heads(qcol_ref[...], fold_ref[...])
            dkg_ref[...] = _fold_heads(kcol_ref[...], fold_ref[...])

    rr = lambda n: _rows(tm, n, rev=True, steps=steps)
    acc = lambda n, r=1: pl.BlockSpec((r, n), lambda i: (0, 0))
    heads = lambda: pl.BlockSpec((H, tm, HA), lambda i: (0, steps - 1 - i, 0))
    return pl.pallas_call(
        body, name="attn_proj_bwd", grid=(steps,),
        in_specs=[heads(), heads(), heads(),
                  rr(D), rr(D), rr(D), rr(LANES), rr(LANES), rr(LANES), rr(D), rr(D),
                  _const((1, D)), _const((1, D)), _const((1, D)), _const((4 * D, D)), _const((LANES, D)),
                  _const((D, LANES)), _const((2 * LANES, D)), _const((tm, tm)), _const((D, LANES))],
        out_specs=[rr(D), rr(4 * D), rr(LANES), acc(D), acc(LANES, 8), acc(LANES, 8), acc(LANES)],
        out_shape=[jax.ShapeDtypeStruct((s, D), F32), jax.ShapeDtypeStruct((s, 4 * D), BF16),
                   jax.ShapeDtypeStruct((s, LANES), BF16), jax.ShapeDtypeStruct((1, D), F32),
                   jax.ShapeDtypeStruct((8, LANES), F32), jax.ShapeDtypeStruct((8, LANES), F32),
                   jax.ShapeDtypeStruct((1, LANES), F32)],
        scratch_shapes=[pltpu.VMEM((8, LANES), F32), pltpu.VMEM((tm, LANES), F32), pltpu.VMEM((1, D), F32),
                        pltpu.VMEM((1, D), F32), pltpu.VMEM((tm, D), F32), pltpu.VMEM((tm, D), F32),
                        pltpu.VMEM((tm, D), F32)],
        compiler_params=_params("arbitrary"),
    )(dq, dk, dv, dz, qpre, kpre, invq, invk, fl, x1, dx2, g2, qg, kg, w2in_t, wf_t, e, et, triu, fold)


def _fold_heads(col, fold):
    hi, mid, lo = _split3(jnp.broadcast_to(col, (8, D)))
    return _dot(hi, fold) + _dot(mid, fold) + _dot(lo, fold)


def _conv_bwd(dx1, p1, x, g1, cw, w1out_t, w1in_t):
    s = x.shape[0]
    tm = min(TM_BWD, s)
    steps = s // tm
    halo = tm // 8

    def body(dx1_ref, b_ref, c_ref, xi_ref, z_ref, ch_ref, xh_ref, x_ref, g_ref, cw_ref, wot_ref, wit_ref,
             gx_ref, dp_ref, dcw_ref, dg1_ref, head_ref):
        step = pl.program_id(0)

        @pl.when(step == 0)
        def _():
            head_ref[...] = jnp.zeros_like(head_ref)
            dcw_ref[...] = jnp.zeros_like(dcw_ref)
            dg1_ref[...] = jnp.zeros_like(dg1_ref)

        first_tile = step == steps - 1
        dx1 = dx1_ref[...]
        row = lax.broadcasted_iota(jnp.int32, (tm, CH), 0)
        dyb = dx1.astype(BF16)
        for ci in range(D // CH):
            lo, hi = ci * CH, (ci + 1) * CH
            dyg = _dot(dyb, wot_ref[:, lo:hi])
            b = b_ref[:, lo:hi].astype(F32)
            c = c_ref[:, lo:hi].astype(F32)
            xin = xi_ref[:, lo:hi].astype(F32)
            z = z_ref[:, lo:hi].astype(F32)
            u = c * xin
            t6 = jnp.where(first_tile, 0.0, ch_ref[6:7, lo:hi].astype(F32) * xh_ref[6:7, lo:hi].astype(F32))
            t7 = jnp.where(first_tile, 0.0, ch_ref[7:8, lo:hi].astype(F32) * xh_ref[7:8, lo:hi].astype(F32))
            u1 = jnp.where(row == 0, t7, pltpu.roll(u, 1, 0))
            u2 = jnp.where(row == 0, t6, jnp.where(row == 1, t7, pltpu.roll(u, 2, 0)))
            w0, w1, w2 = cw_ref[0:1, lo:hi], cw_ref[1:2, lo:hi], cw_ref[2:3, lo:hi]
            y = w2 * u + w1 * u1 + w0 * u2
            sg = _sigmoid(z)
            sil = z * sg
            dp_ref[:, lo:hi] = (dyg * y * sil).astype(BF16)
            dy = dyg * b * sil
            dp_ref[:, 3 * D + lo:3 * D + hi] = (dyg * b * y * (sg * (1.0 + z * (1.0 - sg)))).astype(BF16)
            dcw_ref[2:3, lo:hi] += jnp.sum(dy * u, axis=0, keepdims=True)
            dcw_ref[1:2, lo:hi] += jnp.sum(dy * u1, axis=0, keepdims=True)
            dcw_ref[0:1, lo:hi] += jnp.sum(dy * u2, axis=0, keepdims=True)
            n0 = head_ref[0:1, lo:hi]
            n1 = head_ref[1:2, lo:hi]
            dyn1 = jnp.where(row == tm - 1, n0, pltpu.roll(dy, tm - 1, 0))
            dyn2 = jnp.where(row == tm - 2, n0, jnp.where(row == tm - 1, n1, pltpu.roll(dy, tm - 2, 0)))
            head_ref[:, lo:hi] = dy[0:8, :]
            du = w2 * dy + w1 * dyn1 + w0 * dyn2
            dp_ref[:, D + lo:D + hi] = (du * xin).astype(BF16)
            dp_ref[:, 2 * D + lo:2 * D + hi] = (du * c).astype(BF16)
        dh = _dot(dp_ref[:, 0:D], wit_ref[0:D, :])
        for k in range(1, 4):
            dh += _dot(dp_ref[:, k * D:(k + 1) * D], wit_ref[k * D:(k + 1) * D, :])
        xv = x_ref[...]
        inv = lax.rsqrt(jnp.mean(xv * xv, axis=-1, keepdims=True) + RMS_EPS)
        dg1_ref[...] += jnp.sum(dh * xv * inv, axis=0, keepdims=True)
        gh = dh * g_ref[...]
        gx_ref[...] = dx1 + inv * gh - xv * (inv * inv * inv * jnp.mean(gh * xv, axis=-1, keepdims=True))

    rr = lambda n: _rows(tm, n, rev=True, steps=steps)
    part = lambda k: pl.BlockSpec((tm, D), lambda i: (steps - 1 - i, k))
    prev8 = lambda k: pl.BlockSpec((8, D), lambda i: (jnp.maximum((steps - 1 - i) * halo - 1, 0), k))
    return pl.pallas_call(
        body, name="conv_bwd", grid=(steps,),
        in_specs=[rr(D), part(0), part(1), part(2), part(3), prev8(1), prev8(2), rr(D), _const((1, D)),
                  _const((8, D)), _const((D, D)), _const((4 * D, D))],
        out_specs=[rr(D), rr(4 * D), pl.BlockSpec((8, D), lambda i: (0, 0)), pl.BlockSpec((1, D), lambda i: (0, 0))],
        out_shape=[jax.ShapeDtypeStruct((s, D), F32), jax.ShapeDtypeStruct((s, 4 * D), BF16),
                   jax.ShapeDtypeStruct((8, D), F32), jax.ShapeDtypeStruct((1, D), F32)],
        scratch_shapes=[pltpu.VMEM((8, D), F32)],
        compiler_params=_params("arbitrary"),
    )(dx1, p1, p1, p1, p1, p1, p1, x, g1, cw, w1out_t, w1in_t)


def _wgrad(a, g, name):
    s, k = a.shape
    n = g.shape[1]
    bn = min(n, 1024)
    ts = min(1024, s)

    def body(a_ref, g_ref, out_ref, acc_ref):
        t = pl.program_id(1)

        @pl.when(t == 0)
        def _():
            acc_ref[...] = jnp.zeros_like(acc_ref)

        acc_ref[...] += _dot_tn(a_ref[...], g_ref[...].astype(BF16))

        @pl.when(t == s // ts - 1)
        def _():
            out_ref[...] = acc_ref[...].astype(BF16)

    return pl.pallas_call(
        body, name=name, grid=(n // bn, s // ts),
        in_specs=[pl.BlockSpec((ts, k), lambda j, t: (t, 0)), pl.BlockSpec((ts, bn), lambda j, t: (t, j))],
        out_specs=pl.BlockSpec((k, bn), lambda j, t: (0, j)),
        out_shape=jax.ShapeDtypeStruct((k, n), BF16),
        scratch_shapes=[pltpu.VMEM((k, bn), F32)],
        compiler_params=_params("arbitrary", "arbitrary"),
    )(a, g)


def _adamw_math(w, g, m, v):
    m = ADAM_B1 * m + (1.0 - ADAM_B1) * g
    v = ADAM_B2 * v + (1.0 - ADAM_B2) * (g * g)
    m_hat = m / (1.0 - ADAM_B1 ** ADAM_STEP)
    v_hat = v / (1.0 - ADAM_B2 ** ADAM_STEP)
    delta = -ADAM_LR * (m_hat / (jnp.sqrt(v_hat) + ADAM_EPS) + ADAM_WD * w)
    return delta, m, v


def _adamw_big(parts, w, m, v):
    _, r, c_ = parts.shape
    rb = 368
    assert r % rb == 0

    def body(p_ref, w_ref, m_ref, v_ref, g_ref, d_ref, mo_ref, vo_ref):
        g = p_ref[0].astype(F32)
        for k in range(1, NDEV):
            g = g + p_ref[k].astype(F32)
        g_ref[...] = g
        d_ref[...], mo_ref[...], vo_ref[...] = _adamw_math(w_ref[...], g, m_ref[...], v_ref[...])

    blk = pl.BlockSpec((rb, c_), lambda i: (i, 0))
    return pl.pallas_call(
        body, name="adamw_big", grid=(r // rb,),
        in_specs=[pl.BlockSpec((NDEV, rb, c_), lambda i: (0, i, 0)), blk, blk, blk],
        out_specs=[blk, blk, blk, blk],
        out_shape=[jax.ShapeDtypeStruct((r, c_), F32)] * 4,
        compiler_params=_params("arbitrary"),
    )(parts, w, m, v)


def _sum_parts(parts, loss_row):
    _, r, c_ = parts.shape

    def body(p_ref, o_ref, l_ref):
        g = p_ref[0]
        for k in range(1, NDEV):
            g = g + p_ref[k]
        o_ref[...] = g
        per_row = jnp.sum(o_ref[loss_row:loss_row + 8, :], axis=1, keepdims=True)
        l_ref[...] = jnp.broadcast_to(jnp.sum(per_row, axis=0, keepdims=True), (8, c_))

    return pl.pallas_call(body, name="sum_small", out_shape=[jax.ShapeDtypeStruct((r, c_), F32),
                                                             jax.ShapeDtypeStruct((8, c_), F32)])(parts)


def _adamw_small(g, w, m, v):
    def body(g_ref, w_ref, m_ref, v_ref, d_ref, mo_ref, vo_ref):
        d_ref[...], mo_ref[...], vo_ref[...] = _adamw_math(w_ref[...], g_ref[...], m_ref[...], v_ref[...])

    return pl.pallas_call(body, name="adamw_small", out_shape=[jax.ShapeDtypeStruct(g.shape, F32)] * 3)(g, w, m, v)


def _pack_big(w1in, w2in, w1out, w2out, dtype):
    parts = [w1in.reshape(ROWS_W1IN, PACK_W), w2in.reshape(ROWS_W2IN, PACK_W), w1out.reshape(ROWS_WOUT, PACK_W),
             w2out.reshape(ROWS_WOUT, PACK_W)]
    used = ROWS_W1IN + ROWS_W2IN + 2 * ROWS_WOUT
    parts.append(jnp.zeros((PACK_ROWS - used, PACK_W), parts[0].dtype))
    return jnp.concatenate(parts, axis=0).astype(dtype)


def _unpack_big(p):
    a, b, c = ROWS_W1IN, ROWS_W1IN + ROWS_W2IN, ROWS_W1IN + ROWS_W2IN + ROWS_WOUT
    return (p[:a].reshape(1, D, 512), p[a:b].reshape(1, D, 514), p[b:c].reshape(1, 128, D),
            p[c:c + ROWS_WOUT].reshape(1, 128, D))


def _pad_lanes(a):
    return jnp.pad(a, ((0, 0), (0, LANES - a.shape[1])))


def _tiles(a, t):
    return a.reshape(H, a.shape[1] // t, t, HA)


def kernel(x, conv_norm_g, conv_w_in, conv_w, conv_w_out, attn_norm_g, attn_w_in, attn_b_f, attn_q_norm_g, attn_k_norm_g, attn_w_out, loss_target, m_conv_norm_g, m_conv_w_in, m_conv_w, m_conv_w_out, m_attn_norm_g, m_attn_w_in, m_attn_b_f, m_attn_q_norm_g, m_attn_k_norm_g, m_attn_w_out, v_conv_norm_g, v_conv_w_in, v_conv_w, v_conv_w_out, v_attn_norm_g, v_attn_w_in, v_attn_b_f, v_attn_q_norm_g, v_attn_k_norm_g, v_attn_w_out):
    s = x.shape[1]
    tq = min(TQ, s)
    tmb = min(TM_BWD, s)
    me = 4 * lax.axis_index("x") + 2 * lax.axis_index("y") + lax.axis_index("c")
    xv, tgt = x[0], loss_target[0]

    wg = _all_gather(_pack_big(conv_w_in[0], attn_w_in[0], conv_w_out[0], attn_w_out[0], BF16), "gather_weights")
    small_w = jnp.concatenate([conv_w[0], attn_norm_g, jnp.zeros((4, 128), F32)], axis=0)
    sg_ = _all_gather(small_w, "gather_small_weights")
    a, b, c = ROWS_W1IN, ROWS_W1IN + ROWS_W2IN, ROWS_W1IN + ROWS_W2IN + ROWS_WOUT
    w1in = wg[:, :a].transpose(1, 0, 2).reshape(D, 4 * D)
    w2all = wg[:, a:b].reshape(NDEV, D, 514).transpose(1, 0, 2).reshape(D, 4 * D + H)
    w2in, wf = w2all[:, :4 * D], _pad_lanes(w2all[:, 4 * D:])
    w1out = wg[:, b:c].reshape(D, D)
    w2out = wg[:, c:c + ROWS_WOUT].reshape(D, D)
    cw = jnp.concatenate([sg_[:, 0:3, :].transpose(1, 0, 2).reshape(3, D), jnp.zeros((5, D), F32)], axis=0)
    g2 = sg_[:, 3, :].reshape(1, D)
    qg_t, kg_t = jnp.tile(attn_q_norm_g, (1, H)), jnp.tile(attn_k_norm_g, (1, H))
    bf = _pad_lanes(attn_b_f)

    e = (jnp.arange(D)[:, None] // DH == jnp.arange(LANES)[None, :]).astype(BF16)
    et2 = jnp.concatenate([e.T, e.T], axis=0)
    fold = (jnp.arange(D)[:, None] % DH == jnp.arange(LANES)[None, :]).astype(BF16)
    tril = (jnp.arange(tmb)[:, None] >= jnp.arange(tmb)[None, :]).astype(BF16)
    triu = tril.T
    src = jnp.arange(3 * LANES)
    dst = jnp.arange(H * HA)[None, :] - (HA * (src % LANES) + src // LANES)[:, None]
    place_k = ((dst == COL_BIAS) & (src % LANES < H)[:, None]).astype(BF16)
    place_q = ((dst == ROW_BIAS) & (src % LANES < H)[:, None]).astype(BF16)

    x1, h1, p1, yg = _conv_fwd(xv, conv_norm_g, w1in, cw, w1out)
    reach = 1.01 * math.sqrt(DH) * jnp.max(jnp.abs(attn_q_norm_g)) * jnp.max(jnp.abs(attn_k_norm_g))
    bounded = (2.0 * reach <= BOUNDED_SOFTMAX_REACH).astype(F32).reshape(1, 1)
    group = tq * min(FWD_TILES, s // tq)
    h2, qpre, kpre, z, qa, ka, va, vat, invq, invk, cc, fl = _attn_proj_fwd(
        x1, g2, w2in, wf, bf, qg_t, kg_t, jnp.broadcast_to(reach, (1, LANES)), e, et2, tril, place_k, place_q, group)
    ol, qab = _flash_fwd(bounded, _tiles(qa, tq), ka, vat)
    og, dx2, dz, doa, lossp = _attn_out(ol.reshape(H, s, HA), z, x1, tgt, w2out, w2out.T, e, place_k)

    dq, dk, dv = _flash_bwd(qab, _tiles(doa, tq), _tiles(ka, tq), _tiles(va, tq))
    dx1, dp2, df, dg2, dqg, dkg, dbf = _attn_proj_bwd(
        dq, dk, dv, dz, qpre, kpre, invq, invk, fl, x1, dx2, g2, qg_t, kg_t, w2in.T, wf.T, e, et2, triu, fold)
    gx, dp1, dcw, dg1 = _conv_bwd(dx1, p1, xv, conv_norm_g, cw, w1out.T, w1in.T)
    dw2out = _wgrad(og, dx2, "wgrad_attn_out")
    dw2in = _wgrad(h2, dp2, "wgrad_attn_in")
    dwf = _wgrad(h2, df, "wgrad_attn_forget")
    dw1out = _wgrad(yg, dx1, "wgrad_conv_out")
    dw1in = _wgrad(h1, dp1, "wgrad_conv_in")

    dw2all = jnp.concatenate([dw2in, dwf[:, :H]], axis=1)
    gfull = jnp.concatenate([
        dw1in.reshape(D, NDEV, 512).transpose(1, 0, 2),
        dw2all.reshape(D, NDEV, 514).transpose(1, 0, 2).reshape(NDEV, ROWS_W2IN, PACK_W),
        dw1out.reshape(NDEV, ROWS_WOUT, PACK_W), dw2out.reshape(NDEV, ROWS_WOUT, PACK_W),
        jnp.zeros((NDEV, PACK_ROWS - c - ROWS_WOUT, PACK_W), BF16)], axis=1)
    parts = _exchange_blocks(gfull, "scatter_grads")
    pack = lambda t1, t2, t3, t4: _pack_big(t1[0], t2[0], t3[0], t4[0], F32)
    gb, db, mb, vb = _adamw_big(parts, pack(conv_w_in, attn_w_in, conv_w_out, attn_w_out),
                                pack(m_conv_w_in, m_attn_w_in, m_conv_w_out, m_attn_w_out),
                                pack(v_conv_w_in, v_attn_w_in, v_conv_w_out, v_attn_w_out))

    shard_rows = jnp.concatenate([dcw[0:3].reshape(3, NDEV, 128).transpose(1, 0, 2), dg2.reshape(NDEV, 1, 128),
                                  jnp.zeros((NDEV, 4, 128), F32)], axis=1).reshape(64, 128)
    small_g = jnp.concatenate([shard_rows, dg1.reshape(8, 128), dbf, dqg[0:1], dkg[0:1], jnp.zeros((5, 128), F32),
                               lossp.reshape(8, 128)], axis=0)
    gs, loss8 = _sum_parts(_all_gather(small_g, "gather_small_grads"), 80)
    loss = loss8[0, 0]
    mine = lax.dynamic_slice(gs, (8 * me, 0), (8, 128))
    g_small = jnp.concatenate([mine, gs[64:75], jnp.zeros((5, 128), F32)], axis=0)

    def pack_small(cwk, ang, cng, bfk, qgk, kgk):
        return jnp.concatenate([cwk[0], ang, jnp.zeros((4, 128), F32), cng.reshape(8, 128), _pad_lanes(bfk),
                                _pad_lanes(qgk), _pad_lanes(kgk), jnp.zeros((5, 128), F32)], axis=0)

    ds_, ms_, vs_ = _adamw_small(
        g_small, pack_small(conv_w, attn_norm_g, conv_norm_g, attn_b_f, attn_q_norm_g, attn_k_norm_g),
        pack_small(m_conv_w, m_attn_norm_g, m_conv_norm_g, m_attn_b_f, m_attn_q_norm_g, m_attn_k_norm_g),
        pack_small(v_conv_w, v_attn_norm_g, v_conv_norm_g, v_attn_b_f, v_attn_q_norm_g, v_attn_k_norm_g))

    def leaves(big, small):
        w1i, w2i, w1o, w2o = _unpack_big(big)
        return (small[8:16].reshape(1, D), w1i, small[0:3].reshape(1, 3, 128), w1o, small[3:4], w2i,
                small[16:17, :H], small[17:18, :DH], small[18:19, :DH], w2o)

    return (loss, gx[None], *leaves(gb, g_small), *leaves(db, ds_), *leaves(mb, ms_), *leaves(vb, vs_))
```

```python
import functools
import math

import jax
import jax.numpy as jnp
from jax import lax
from jax.experimental import pallas as pl
from jax.experimental.pallas import tpu as pltpu

F32 = jnp.float32
BF16 = jnp.bfloat16

D = 1024
H = 16
DH = 64
NDEV = 8
RMS_EPS = 1e-6
LANES = 128
HA = 128
TM_FWD = 512
TM_BWD = 256
TQ = 512
AUG_ROWS = 80
FWD_TILES = 8
BWD_TILES = 4
CH = 256
PACK_W = 512
PACK_ROWS = 2576
ROWS_W1IN, ROWS_W2IN, ROWS_WOUT = 1024, 1028, 256
ADAM_BLOCKS = 4
PAD_ROWS2 = 60
ADAM_LR, ADAM_B1, ADAM_B2, ADAM_EPS, ADAM_WD, ADAM_STEP = 0.001, 0.9, 0.999, 1e-08, 0.01, 10
VMEM_LIMIT = 56 * 1024 * 1024
MASKED = -1e30
BOUNDED_SOFTMAX_REACH = 60.0
MESH = pl.DeviceIdType.MESH


def _params(*sem, vmem=VMEM_LIMIT):
    return pltpu.CompilerParams(dimension_semantics=sem or None, vmem_limit_bytes=vmem)


def _const(shape):
    nd = len(shape)
    return pl.BlockSpec(shape, lambda *_: (0,) * nd, pipeline_mode=pl.Buffered(1))


def _rows(tm, n, rev=False, steps=None):
    if rev:
        return pl.BlockSpec((tm, n), lambda i: (steps - 1 - i, 0))
    return pl.BlockSpec((tm, n), lambda i: (i, 0))


def _dot(a, b):
    return jnp.dot(a, b, preferred_element_type=F32)


def _top16(x):
    bits = lax.bitcast_convert_type(x, jnp.uint32) & jnp.uint32(0xFFFF0000)
    return lax.bitcast_convert_type(bits, F32)


def _split2(x):
    hi = _top16(x)
    return hi.astype(BF16), (x - hi).astype(BF16)


def _split3(x):
    hi = _top16(x)
    r = x - hi
    mid = _top16(r)
    return hi.astype(BF16), mid.astype(BF16), (r - mid).astype(BF16)


def _seg_sum(a, e):
    hi, lo = _split2(a)
    return _dot(hi, e) + _dot(lo, e)


def _seg_bcast(s, et2):
    return _dot(jnp.concatenate(_split2(s), axis=1), et2)


def _tri_sum(t, v):
    hi, mid, lo = _split3(v)
    return _dot(t, hi) + _dot(t, mid) + _dot(t, lo)


def _sigmoid(z):
    return 1.0 / (1.0 + jnp.exp(-z))


def _place():
    return lax.axis_index("x"), lax.axis_index("y"), lax.axis_index("c")


def _all_gather(xb, name):
    r, c_ = xb.shape

    def body(x_ref, out_ref, send_sems, recv_sems, local_sem):
        x, y, c = _place()
        me, sibling = (x, y, c), (x, y, 1 - c)
        chips = [(1 - x, y), (x, 1 - y), (1 - x, 1 - y)]

        def slab(px, py, pc):
            return out_ref.at[4 * px + 2 * py + pc]

        def copy(k, block, to, src=None):
            return pltpu.make_async_remote_copy(
                src_ref=slab(*block) if src is None else src, dst_ref=slab(*block),
                send_sem=send_sems.at[k], recv_sem=recv_sems.at[k], device_id=to, device_id_type=MESH)

        mine = pltpu.make_async_copy(x_ref, slab(*me), local_sem)
        mine.start()
        first = [copy(0, me, sibling, src=x_ref)]
        first += [copy(1 + j, me, (*chip, c), src=x_ref) for j, chip in enumerate(chips)]
        for cp in first:
            cp.start()
        passed = [copy(4 + j, (*chip, c), sibling) for j, chip in enumerate(chips)]
        for j, chip in enumerate(chips):
            copy(1 + j, (*chip, c), me).wait_recv()
            passed[j].start()
        copy(0, sibling, me).wait_recv()
        for j, chip in enumerate(chips):
            copy(4 + j, (*chip, 1 - c), me).wait_recv()
        for cp in first + passed:
            cp.wait_send()
        mine.wait()

    return pl.pallas_call(
        body, name=name,
        out_shape=jax.ShapeDtypeStruct((NDEV, r, c_), xb.dtype),
        in_specs=[pl.BlockSpec(memory_space=pl.ANY)],
        out_specs=pl.BlockSpec(memory_space=pl.ANY),
        scratch_shapes=[pltpu.SemaphoreType.DMA((7,)), pltpu.SemaphoreType.DMA((7,)), pltpu.SemaphoreType.DMA],
    )(xb)


def _block_copies(g_ref, out_ref, send_sems, recv_sems, local_sem):
    x, y, c = _place()
    copies = [pltpu.make_async_copy(g_ref.at[4 * x + 2 * y + c], out_ref.at[0], local_sem)]
    for k in range(1, NDEV):
        px = 1 - x if k & 4 else x
        py = 1 - y if k & 2 else y
        pc = 1 - c if k & 1 else c
        copies.append(pltpu.make_async_remote_copy(
            src_ref=g_ref.at[4 * px + 2 * py + pc], dst_ref=out_ref.at[k],
            send_sem=send_sems.at[k - 1], recv_sem=recv_sems.at[k - 1],
            device_id=(px, py, pc), device_id_type=MESH))
    return copies


EXCHANGE_SEMS = [pltpu.SemaphoreType.DMA((7,)), pltpu.SemaphoreType.DMA((7,)), pltpu.SemaphoreType.DMA]


def _exchange_blocks(g, name):
    _, r, c_ = g.shape

    def body(g_ref, out_ref, send_sems, recv_sems, local_sem):
        copies = _block_copies(g_ref, out_ref, send_sems, recv_sems, local_sem)
        for cp in copies:
            cp.start()
        for cp in copies:
            cp.wait()

    return pl.pallas_call(
        body, name=name,
        out_shape=jax.ShapeDtypeStruct((NDEV, r, c_), g.dtype),
        in_specs=[pl.BlockSpec(memory_space=pl.ANY)],
        out_specs=pl.BlockSpec(memory_space=pl.ANY),
        scratch_shapes=list(EXCHANGE_SEMS),
    )(g)


def _conv_fwd(x, g1, w1in, cw, w1out):
    s = x.shape[0]
    tm = min(TM_FWD, s)
    steps = s // tm

    def body(x_ref, g_ref, win_ref, cw_ref, wout_ref, x1_ref, h1_ref, p1_ref, yg_ref, tail_ref):
        @pl.when(pl.program_id(0) == 0)
        def _():
            tail_ref[...] = jnp.zeros_like(tail_ref)

        xv = x_ref[...]
        inv = lax.rsqrt(jnp.mean(xv * xv, axis=-1, keepdims=True) + RMS_EPS)
        h = (xv * inv * g_ref[...]).astype(BF16)
        h1_ref[...] = h
        row = lax.broadcasted_iota(jnp.int32, (tm, CH), 0)
        for ci in range(D // CH):
            lo, hi = ci * CH, (ci + 1) * CH
            parts = []
            for k in range(4):
                pk = _dot(h, win_ref[:, k * D + lo:k * D + hi]).astype(BF16)
                p1_ref[:, k * D + lo:k * D + hi] = pk
                parts.append(pk.astype(F32))
            b, c, xin, z = parts
            u = c * xin
            t6 = tail_ref[6:7, lo:hi]
            t7 = tail_ref[7:8, lo:hi]
            u1 = jnp.where(row == 0, t7, pltpu.roll(u, 1, 0))
            u2 = jnp.where(row == 0, t6, jnp.where(row == 1, t7, pltpu.roll(u, 2, 0)))
            tail_ref[:, lo:hi] = u[tm - 8:, :]
            y = cw_ref[2:3, lo:hi] * u + cw_ref[1:2, lo:hi] * u1 + cw_ref[0:1, lo:hi] * u2
            yg_ref[:, lo:hi] = (b * y * (z * _sigmoid(z))).astype(BF16)
        x1_ref[...] = xv + _dot(yg_ref[...], wout_ref[...])

    return pl.pallas_call(
        body, name="conv_fwd", grid=(steps,),
        in_specs=[_rows(tm, D), _const((1, D)), _const((D, 4 * D)), _const((8, D)), _const((D, D))],
        out_specs=[_rows(tm, D), _rows(tm, D), _rows(tm, 4 * D), _rows(tm, D)],
        out_shape=[jax.ShapeDtypeStruct((s, D), F32), jax.ShapeDtypeStruct((s, D), BF16),
                   jax.ShapeDtypeStruct((s, 4 * D), BF16), jax.ShapeDtypeStruct((s, D), BF16)],
        scratch_shapes=[pltpu.VMEM((8, D), F32)],
        compiler_params=_params("arbitrary"),
    )(x, g1, w1in, cw, w1out)


COL_BIAS = DH
ROW_BIAS = DH + 3


def _terms_cat(val):
    return jnp.concatenate(_split3(val), axis=1)


def _split_heads(x, aug, out_ref, lane, t_ref=None):
    for hp in range(H // 2):
        pair = x[:, hp * HA:(hp + 1) * HA]
        for k, feat in enumerate((pair, pltpu.roll(pair, DH, 1))):
            block = jnp.where(lane < DH, feat, aug(2 * hp + k))
            out_ref[2 * hp + k] = block.astype(BF16)
            if t_ref is not None:
                t_ref[2 * hp + k] = block.T[0:t_ref.shape[1], :].astype(BF16)


def _ones_at(lane, first):
    return jnp.where((lane >= first) & (lane < first + 3), 1.0, 0.0)


def _attn_proj_fwd(x1, g2, w2in, wf, bf, qg, kg, shift, e, et, tril, place_k, place_q, group):
    s = x1.shape[0]
    tm = min(TM_BWD, s)
    steps = s // tm
    per = group // tm

    def body(x_ref, g_ref, win_ref, wf_ref, bf_ref, qg_ref, kg_ref, sh_ref, e_ref, et_ref, tril_ref, pk_ref, pq_ref,
             h2_ref, qp_ref, kp_ref, z_ref, qa_ref, ka_ref, va_ref, vat_ref, iq_ref, ik_ref, c_ref, fl_ref, carry_ref):
        @pl.when(pl.program_id(0) == 0)
        def _():
            carry_ref[...] = jnp.zeros_like(carry_ref)

        xv = x_ref[...]
        inv = lax.rsqrt(jnp.mean(xv * xv, axis=-1, keepdims=True) + RMS_EPS)
        h = (xv * inv * g_ref[...]).astype(BF16)
        h2_ref[...] = h

        fl = _dot(h, wf_ref[...]) + bf_ref[...]
        fl_ref[...] = fl
        ex = jnp.exp(-jnp.abs(fl))
        up = 1.0 + ex
        log1p = jnp.where(up == 1.0, ex, jnp.log(up) * ex / (up - 1.0))
        lane = lax.broadcasted_iota(jnp.int32, (tm, LANES), 1)
        logf = jnp.where(lane < H, jnp.minimum(fl, 0.0) - log1p, 0.0)
        c_ref[...] = _tri_sum(tril_ref[...], logf) + carry_ref[0:1, :]
        carry_ref[0:1, :] = c_ref[tm - 1:tm, :]
        c = c_ref[...]

        def normed(col, pre_ref, inv_ref, gain_ref, scale):
            pre = _dot(h, win_ref[:, col * D:(col + 1) * D]).astype(BF16)
            pre_ref[...] = pre
            t = pre.astype(F32)
            invh = lax.rsqrt(_seg_sum(t * t, e_ref[...]) * (1.0 / DH) + RMS_EPS)
            inv_ref[...] = invh
            return t * _seg_bcast(invh, et_ref[...]) * (gain_ref[...] * scale)

        ones_col, ones_row = _ones_at(lane, COL_BIAS), _ones_at(lane, ROW_BIAS)
        q_bias = _dot(_terms_cat(c - sh_ref[...]), pq_ref[...])
        _split_heads(normed(0, qp_ref, iq_ref, qg_ref, 1.0 / math.sqrt(DH)),
                     lambda hh: q_bias[:, hh * HA:(hh + 1) * HA] + ones_col, qa_ref, lane)
        k_bias = _dot(_terms_cat(-c), pk_ref[...])
        _split_heads(normed(1, kp_ref, ik_ref, kg_ref, 1.0),
                     lambda hh: k_bias[:, hh * HA:(hh + 1) * HA] + ones_row, ka_ref, lane)
        v = _dot(h, win_ref[:, 2 * D:3 * D])
        _split_heads(v, lambda hh: ones_col, va_ref, lane, vat_ref)
        z_ref[...] = _dot(h, win_ref[:, 3 * D:4 * D]).astype(BF16)

    row_bf = lambda: _rows(tm, D)
    row_sm = lambda: _rows(tm, LANES)
    heads = lambda: pl.BlockSpec((H, tm, HA), lambda i: (0, i, 0))
    return pl.pallas_call(
        body, name="attn_proj_fwd", grid=(steps,),
        in_specs=[_rows(tm, D), _const((1, D)), _const((D, 4 * D)), _const((D, LANES)), _const((1, LANES)),
                  _const((1, D)), _const((1, D)), _const((1, LANES)), _const((D, LANES)), _const((2 * LANES, D)),
                  _const((tm, tm)), _const((3 * LANES, H * HA)), _const((3 * LANES, H * HA))],
        out_specs=[row_bf() for _ in range(4)] + [heads() for _ in range(3)] + [
            pl.BlockSpec((H, None, AUG_ROWS, tm), lambda i: (0, i // per, 0, i % per))] + [row_sm() for _ in range(4)],
        out_shape=[jax.ShapeDtypeStruct((s, D), BF16)] * 4 + [jax.ShapeDtypeStruct((H, s, HA), BF16)] * 3 + [
            jax.ShapeDtypeStruct((H, s // group, AUG_ROWS, group), BF16)] + [jax.ShapeDtypeStruct((s, LANES), F32)] * 4,
        scratch_shapes=[pltpu.VMEM((8, LANES), F32)],
        compiler_params=_params("arbitrary"),
    )(x1, g2, w2in, wf, bf, qg, kg, shift, e, et, tril, place_k, place_q)


def _dot_nt(a, b):
    return lax.dot_general(a, b, (((1,), (1,)), ((), ())), preferred_element_type=F32)


def _dot_tn(a, b):
    return lax.dot_general(a, b, (((0,), (0,)), ((), ())), preferred_element_type=F32)


def _flash_fwd(bounded, qa, ka, vat):
    h_, nq, tq, _ = qa.shape
    s, tk = ka.shape[1], tq
    group = vat.shape[3]
    gt = group // tk
    shift = gt.bit_length() - 1

    def body(flag_ref, q_ref, k_ref, vt_ref, o_ref, qb_ref, m_ref, acct_ref):
        i = pl.program_id(1)
        lane = lax.broadcasted_iota(jnp.int32, (tq, HA), 1)
        in_bias = (lane >= ROW_BIAS) & (lane < ROW_BIAS + 3)
        acct_ref[...] = jnp.zeros_like(acct_ref)
        g = lax.shift_right_logical(i, shift)
        r = i & (gt - 1)

        def run(use_bound):
            q = q_ref[...]
            if not use_bound:
                q = jnp.where(in_bias, jnp.zeros_like(q), q)
                m_ref[...] = jnp.full_like(m_ref, MASKED)

            def step(first_key, vt, tiles, diagonal_at):
                keys = pl.ds(pl.multiple_of(first_key, tk), tiles * tk)
                zt = _dot_nt(k_ref[keys, :], q)
                if diagonal_at is not None:
                    key = lax.broadcasted_iota(jnp.int32, (tiles * tk, tq), 0)
                    qry = lax.broadcasted_iota(jnp.int32, (tiles * tk, tq), 1)
                    zt = jnp.where(key <= qry + diagonal_at * tk, zt, MASKED)
                if use_bound:
                    acct_ref[0:AUG_ROWS, :] += _dot(vt, jnp.exp(zt).astype(BF16))
                else:
                    m_old = m_ref[...]
                    m_new = jnp.maximum(m_old, jnp.max(zt, axis=0, keepdims=True))
                    pt = jnp.exp(zt - m_new)
                    acct_ref[0:AUG_ROWS, :] = jnp.exp(m_old - m_new) * acct_ref[0:AUG_ROWS, :] + _dot(
                        vt, pt.astype(BF16))
                    m_ref[...] = m_new

            def whole_group(jj, carry):
                step(jj * group, vt_ref[jj], gt, None)
                return carry

            lax.fori_loop(0, g, whole_group, 0)
            for rr in range(gt):
                pl.when(r == rr)(functools.partial(
                    lambda rr: step(g * group, vt_ref[g, :, 0:(rr + 1) * tk], rr + 1, rr), rr))

            if not use_bound:
                acct_ref[COL_BIAS + 1:COL_BIAS + 2, :] = m_ref[...]
            acc = acct_ref[...].T
            l = jnp.sum(jnp.where(lane == COL_BIAS, acc, 0.0), axis=1, keepdims=True)
            if use_bound:
                m = -jnp.sum(jnp.where(in_bias, q.astype(F32), 0.0), axis=1, keepdims=True)
            else:
                m = jnp.sum(jnp.where(lane == COL_BIAS + 1, acc, 0.0), axis=1, keepdims=True)
            lse = m + jnp.log(l)
            o_ref[...] = jnp.where(lane < DH, acc / l, lse)
            hi, mid, lo = _split3(-lse)
            qb_ref[...] = jnp.where(lane == ROW_BIAS, hi, jnp.where(lane == ROW_BIAS + 1, mid, jnp.where(
                lane == ROW_BIAS + 2, lo, q_ref[...])))

        use_bound = flag_ref[0, 0] > 0.5
        pl.when(use_bound)(lambda: run(True))
        pl.when(jnp.logical_not(use_bound))(lambda: run(False))

    tile = lambda: pl.BlockSpec((None, None, tq, HA), lambda h, i: (h, i, 0, 0))
    return pl.pallas_call(
        body, name="flash_fwd", grid=(h_, nq),
        in_specs=[pl.BlockSpec(memory_space=pltpu.SMEM), tile(), pl.BlockSpec((None, s, HA), lambda h, i: (h, 0, 0)),
                  pl.BlockSpec((None, s // group, AUG_ROWS, group), lambda h, i: (h, 0, 0, 0))],
        out_specs=[tile(), tile()],
        out_shape=[jax.ShapeDtypeStruct((h_, nq, tq, HA), F32), jax.ShapeDtypeStruct((h_, nq, tq, HA), BF16)],
        scratch_shapes=[pltpu.VMEM((1, tq), F32), pltpu.VMEM((HA, tq), F32)],
        compiler_params=_params("arbitrary", "arbitrary"),
    )(bounded, qa, ka, vat)


def _merge_heads(src, out_ref, lane):
    for hp in range(H // 2):
        out_ref[:, hp * HA:(hp + 1) * HA] = jnp.where(lane < DH, src(2 * hp), pltpu.roll(src(2 * hp + 1), DH, 1))


def _attn_out(ol, z, x1, tgt, w2out, w2out_t, e, place_k):
    s = z.shape[0]
    tm = min(TM_BWD, s)
    steps = s // tm

    def body(ol_ref, z_ref, x1_ref, t_ref, w_ref, wt_ref, e_ref, pk_ref, og_ref, dx2_ref, dz_ref, doa_ref, loss_ref,
             o_ref):
        @pl.when(pl.program_id(0) == 0)
        def _():
            loss_ref[...] = jnp.zeros_like(loss_ref)

        lane = lax.broadcasted_iota(jnp.int32, (tm, HA), 1)
        _merge_heads(lambda hh: ol_ref[hh], o_ref, lane)
        ov = o_ref[...]
        zv = z_ref[...].astype(F32)
        sg = _sigmoid(zv)
        sil = zv * sg
        og = (ov * sil).astype(BF16)
        og_ref[...] = og
        err = x1_ref[...] + _dot(og, w_ref[...]) - t_ref[...]
        loss_ref[...] += (0.5 / D) * jnp.sum(err * err, axis=0, keepdims=True)
        dx2 = err * (1.0 / D)
        dx2_ref[...] = dx2
        dog = _dot(dx2.astype(BF16), wt_ref[...])
        do = (dog * sil).astype(BF16).astype(F32)
        dz_ref[...] = (dog * ov * (sg * (1.0 + zv * (1.0 - sg)))).astype(BF16)
        delta = _seg_sum(do * ov, e_ref[...])
        d_bias = _dot(_terms_cat(-delta), pk_ref[...])
        _split_heads(do, lambda hh: d_bias[:, hh * HA:(hh + 1) * HA], doa_ref, lane)

    heads = lambda: pl.BlockSpec((H, tm, HA), lambda i: (0, i, 0))
    return pl.pallas_call(
        body, name="attn_out", grid=(steps,),
        in_specs=[heads(), _rows(tm, D), _rows(tm, D), _rows(tm, D), _const((D, D)), _const((D, D)),
                  _const((D, LANES)), _const((3 * LANES, H * HA))],
        out_specs=[_rows(tm, D), _rows(tm, D), _rows(tm, D), heads(), pl.BlockSpec((1, D), lambda i: (0, 0))],
        out_shape=[jax.ShapeDtypeStruct((s, D), BF16), jax.ShapeDtypeStruct((s, D), F32),
                   jax.ShapeDtypeStruct((s, D), BF16), jax.ShapeDtypeStruct((H, s, HA), BF16),
                   jax.ShapeDtypeStruct((1, D), F32)],
        scratch_shapes=[pltpu.VMEM((tm, D), F32)],
        compiler_params=_params("arbitrary"),
    )(ol, z, x1, tgt, w2out, w2out_t, e, place_k)


def _flash_bwd(qab, doa, ka, va):
    h_, nq, tq, _ = qab.shape
    nk, tk = ka.shape[1], ka.shape[2]
    s = nk * tk
    gt = min(BWD_TILES, nq)
    shift = gt.bit_length() - 1

    def body(qa_ref, da_ref, ka_ref, va_ref, dq_hbm, dk_ref, dv_ref, dq_acc, dkt_ref, dvt_ref, sem):
        hh = pl.program_id(0)
        j = pl.program_id(1)

        @pl.when(j == 0)
        def _():
            dq_acc[...] = jnp.zeros_like(dq_acc)

        dkt_ref[...] = jnp.zeros_like(dkt_ref)
        dvt_ref[...] = jnp.zeros_like(dvt_ref)
        ka_v = ka_ref[...]
        va_v = va_ref[...]

        def step(i, tiles, masked):
            rows = pl.ds(pl.multiple_of(i * tq, tq), tiles * tq)
            qa_i, da_i = qa_ref[rows, :], da_ref[rows, :]
            zz = _dot_nt(qa_i, ka_v)
            if masked:
                row = lax.broadcasted_iota(jnp.int32, (tiles * tq, tk), 0)
                col = lax.broadcasted_iota(jnp.int32, (tiles * tq, tk), 1)
                zz = jnp.where(col <= row, zz, MASKED)
            p = jnp.exp(zz)
            ds = (p * _dot_nt(da_i, va_v)).astype(BF16)
            pb = p.astype(BF16)
            dq_acc[rows, :] += _dot(ds, ka_v)
            dvt_ref[...] += _dot_tn(da_i, pb)
            dkt_ref[...] += _dot_tn(qa_i, ds)

        g0 = lax.shift_right_logical(j, shift)
        for rr in range(gt):
            pl.when((j & (gt - 1)) == rr)(functools.partial(step, j, gt - rr, True))

        def whole_group(g, carry):
            step(g * gt, gt, False)
            return carry

        lax.fori_loop(g0 + 1, nq // gt, whole_group, 0)
        dk_ref[...] = dkt_ref[...].T
        dv_ref[...] = dvt_ref[...].T.astype(BF16)

        @pl.when(j == nk - 1)
        def _():
            cp = pltpu.make_async_copy(dq_acc, dq_hbm.at[hh], sem)
            cp.start()
            cp.wait()

    whole = lambda: pl.BlockSpec((None, nq * tq, HA), lambda h, j: (h, 0, 0))
    tile = lambda: pl.BlockSpec((None, None, tk, HA), lambda h, j: (h, j, 0, 0))
    rows = lambda: pl.BlockSpec((None, tk, HA), lambda h, j: (h, j, 0))
    return pl.pallas_call(
        body, name="flash_bwd", grid=(h_, nk),
        in_specs=[whole(), whole(), tile(), tile()],
        out_specs=[pl.BlockSpec(memory_space=pl.ANY), rows(), rows()],
        out_shape=[jax.ShapeDtypeStruct((h_, s, HA), F32)] * 2 + [jax.ShapeDtypeStruct((h_, s, HA), BF16)],
        scratch_shapes=[pltpu.VMEM((s, HA), F32), pltpu.VMEM((HA, tk), F32), pltpu.VMEM((HA, tk), F32),
                        pltpu.SemaphoreType.DMA],
        compiler_params=_params("arbitrary", "arbitrary"),
    )(qab.reshape(h_, s, HA), doa.reshape(h_, s, HA), ka, va)


def _attn_proj_bwd(dq, dk, dv, dz, qpre, kpre, invq, invk, fl, x1, dx2, g2, qg, kg, w2in_t, wf_t, e, et,
                   triu, fold):
    s = x1.shape[0]
    tm = min(TM_BWD, s)
    steps = s // tm

    def body(dq_ref, dk_ref, dvh_ref, dz_ref, qp_ref, kp_ref, iq_ref, ik_ref, fl_ref, x1_ref, dx2_ref,
             g2_ref, qg_ref, kg_ref, wt_ref, wft_ref, e_ref, et_ref, triu_ref, fold_ref,
             dx1_ref, dp_ref, df_ref, dg2_ref, dqg_ref, dkg_ref, dbf_ref, carry_ref, rc_ref, qcol_ref, kcol_ref,
             dqs_ref, dkn_ref, dv_ref):
        step = pl.program_id(0)
        lane = lax.broadcasted_iota(jnp.int32, (tm, HA), 1)
        _merge_heads(lambda hh: dvh_ref[hh].astype(F32), dv_ref, lane)
        dc = jnp.zeros((tm, HA), F32)
        for hp in range(H // 2):
            dqh = [dq_ref[2 * hp], dq_ref[2 * hp + 1]]
            dkh = [dk_ref[2 * hp], dk_ref[2 * hp + 1]]
            dqs_ref[:, hp * HA:(hp + 1) * HA] = jnp.where(lane < DH, dqh[0], pltpu.roll(dqh[1], DH, 1))
            dkn_ref[:, hp * HA:(hp + 1) * HA] = jnp.where(lane < DH, dkh[0], pltpu.roll(dkh[1], DH, 1))
            for k in range(2):
                sums = jnp.where(lane == ROW_BIAS, dqh[k], 0.0) - pltpu.roll(
                    jnp.where(lane == COL_BIAS, dkh[k], 0.0), ROW_BIAS - COL_BIAS, 1)
                dc = dc + pltpu.roll(sums, (2 * hp + k - ROW_BIAS) % HA, 1)

        @pl.when(step == 0)
        def _():
            carry_ref[...] = jnp.zeros_like(carry_ref)
            dg2_ref[...] = jnp.zeros_like(dg2_ref)
            dbf_ref[...] = jnp.zeros_like(dbf_ref)
            qcol_ref[...] = jnp.zeros_like(qcol_ref)
            kcol_ref[...] = jnp.zeros_like(kcol_ref)
            dqg_ref[...] = jnp.zeros_like(dqg_ref)
            dkg_ref[...] = jnp.zeros_like(dkg_ref)

        rc_ref[...] = _tri_sum(triu_ref[...], dc) + carry_ref[0:1, :]
        carry_ref[0:1, :] = rc_ref[0:1, :]
        df = rc_ref[...] * _sigmoid(-fl_ref[...])
        dfb = df.astype(BF16)
        df_ref[...] = dfb
        dbf_ref[...] += jnp.sum(df, axis=0, keepdims=True)

        def norm_bwd(dn, pre_ref, inv_ref, gain_ref, col_ref):
            pre = pre_ref[...].astype(F32)
            invh = inv_ref[...]
            invb = _seg_bcast(invh, et_ref[...])
            col_ref[...] += jnp.sum(dn * pre * invb, axis=0, keepdims=True)
            gd = dn * gain_ref[...]
            mean = _seg_sum(gd * pre, e_ref[...]) * (1.0 / DH)
            return invb * gd - pre * _seg_bcast(mean * invh * invh * invh, et_ref[...])

        dq = norm_bwd(dqs_ref[...] * (1.0 / math.sqrt(DH)), qp_ref, iq_ref, qg_ref, qcol_ref).astype(BF16)
        dp_ref[:, 0:D] = dq
        dh = _dot(dq, wt_ref[0:D, :])
        dk = norm_bwd(dkn_ref[...], kp_ref, ik_ref, kg_ref, kcol_ref).astype(BF16)
        dp_ref[:, D:2 * D] = dk
        dh += _dot(dk, wt_ref[D:2 * D, :])
        dvb = dv_ref[...].astype(BF16)
        dp_ref[:, 2 * D:3 * D] = dvb
        dh += _dot(dvb, wt_ref[2 * D:3 * D, :])
        dzb = dz_ref[...]
        dp_ref[:, 3 * D:4 * D] = dzb
        dh += _dot(dzb, wt_ref[3 * D:4 * D, :])
        dh += _dot(dfb, wft_ref[...])

        xv = x1_ref[...]
        inv = lax.rsqrt(jnp.mean(xv * xv, axis=-1, keepdims=True) + RMS_EPS)
        dg2_ref[...] += jnp.sum(dh * xv * inv, axis=0, keepdims=True)
        gh = dh * g2_ref[...]
        dx1_ref[...] = dx2_ref[...] + inv * gh - xv * (inv * inv * inv * jnp.mean(gh * xv, axis=-1, keepdims=True))

        @pl.when(step == steps - 1)
        def _():
            dqg_ref[...] = _fold_heads(qcol_ref[...], fold_ref[...])
            dkg_ref[...] = _fold_heads(kcol_ref[...], fold_ref[...])

    rr = lambda n: _rows(tm, n, rev=True, steps=steps)
    acc = lambda n, r=1: pl.BlockSpec((r, n), lambda i: (0, 0))
    heads = lambda: pl.BlockSpec((H, tm, HA), lambda i: (0, steps - 1 - i, 0))
    return pl.pallas_call(
        body, name="attn_proj_bwd", grid=(steps,),
        in_specs=[heads(), heads(), heads(),
                  rr(D), rr(D), rr(D), rr(LANES), rr(LANES), rr(LANES), rr(D), rr(D),
                  _const((1, D)), _const((1, D)), _const((1, D)), _const((4 * D, D)), _const((LANES, D)),
                  _const((D, LANES)), _const((2 * LANES, D)), _const((tm, tm)), _const((D, LANES))],
        out_specs=[rr(D), rr(4 * D), rr(LANES), acc(D), acc(LANES, 8), acc(LANES, 8), acc(LANES)],
        out_shape=[jax.ShapeDtypeStruct((s, D), F32), jax.ShapeDtypeStruct((s, 4 * D), BF16),
                   jax.ShapeDtypeStruct((s, LANES), BF16), jax.ShapeDtypeStruct((1, D), F32),
                   jax.ShapeDtypeStruct((8, LANES), F32), jax.ShapeDtypeStruct((8, LANES), F32),
                   jax.ShapeDtypeStruct((1, LANES), F32)],
        scratch_shapes=[pltpu.VMEM((8, LANES), F32), pltpu.VMEM((tm, LANES), F32), pltpu.VMEM((1, D), F32),
                        pltpu.VMEM((1, D), F32), pltpu.VMEM((tm, D), F32), pltpu.VMEM((tm, D), F32),
                        pltpu.VMEM((tm, D), F32)],
        compiler_params=_params("arbitrary"),
    )(dq, dk, dv, dz, qpre, kpre, invq, invk, fl, x1, dx2, g2, qg, kg, w2in_t, wf_t, e, et, triu, fold)


def _fold_heads(col, fold):
    hi, mid, lo = _split3(jnp.broadcast_to(col, (8, D)))
    return _dot(hi, fold) + _dot(mid, fold) + _dot(lo, fold)


def _conv_bwd(dx1, p1, x, g1, cw, w1out_t, w1in_t, grads2):
    s = x.shape[0]
    tm = min(TM_BWD, s)
    steps = s // tm
    halo = tm // 8

    def body(dx1_ref, b_ref, c_ref, xi_ref, z_ref, ch_ref, xh_ref, x_ref, g_ref, cw_ref, wot_ref, wit_ref, g2_ref,
             gx_ref, dp_ref, dcw_ref, dg1_ref, parts_ref, head_ref, send_sems, recv_sems, local_sem):
        step = pl.program_id(0)

        @pl.when(step == 0)
        def _():
            for cp in _block_copies(g2_ref, parts_ref, send_sems, recv_sems, local_sem):
                cp.start()

        @pl.when(step == 0)
        def _():
            head_ref[...] = jnp.zeros_like(head_ref)
            dcw_ref[...] = jnp.zeros_like(dcw_ref)
            dg1_ref[...] = jnp.zeros_like(dg1_ref)

        first_tile = step == steps - 1
        dx1 = dx1_ref[...]
        row = lax.broadcasted_iota(jnp.int32, (tm, CH), 0)
        dyb = dx1.astype(BF16)
        for ci in range(D // CH):
            lo, hi = ci * CH, (ci + 1) * CH
            dyg = _dot(dyb, wot_ref[:, lo:hi])
            b = b_ref[:, lo:hi].astype(F32)
            c = c_ref[:, lo:hi].astype(F32)
            xin = xi_ref[:, lo:hi].astype(F32)
            z = z_ref[:, lo:hi].astype(F32)
            u = c * xin
            t6 = jnp.where(first_tile, 0.0, ch_ref[6:7, lo:hi].astype(F32) * xh_ref[6:7, lo:hi].astype(F32))
            t7 = jnp.where(first_tile, 0.0, ch_ref[7:8, lo:hi].astype(F32) * xh_ref[7:8, lo:hi].astype(F32))
            u1 = jnp.where(row == 0, t7, pltpu.roll(u, 1, 0))
            u2 = jnp.where(row == 0, t6, jnp.where(row == 1, t7, pltpu.roll(u, 2, 0)))
            w0, w1, w2 = cw_ref[0:1, lo:hi], cw_ref[1:2, lo:hi], cw_ref[2:3, lo:hi]
            y = w2 * u + w1 * u1 + w0 * u2
            sg = _sigmoid(z)
            sil = z * sg
            dp_ref[:, lo:hi] = (dyg * y * sil).astype(BF16)
            dy = dyg * b * sil
            dp_ref[:, 3 * D + lo:3 * D + hi] = (dyg * b * y * (sg * (1.0 + z * (1.0 - sg)))).astype(BF16)
            dcw_ref[2:3, lo:hi] += jnp.sum(dy * u, axis=0, keepdims=True)
            dcw_ref[1:2, lo:hi] += jnp.sum(dy * u1, axis=0, keepdims=True)
            dcw_ref[0:1, lo:hi] += jnp.sum(dy * u2, axis=0, keepdims=True)
            n0 = head_ref[0:1, lo:hi]
            n1 = head_ref[1:2, lo:hi]
            dyn1 = jnp.where(row == tm - 1, n0, pltpu.roll(dy, tm - 1, 0))
            dyn2 = jnp.where(row == tm - 2, n0, jnp.where(row == tm - 1, n1, pltpu.roll(dy, tm - 2, 0)))
            head_ref[:, lo:hi] = dy[0:8, :]
            du = w2 * dy + w1 * dyn1 + w0 * dyn2
            dp_ref[:, D + lo:D + hi] = (du * xin).astype(BF16)
            dp_ref[:, 2 * D + lo:2 * D + hi] = (du * c).astype(BF16)
        dh = _dot(dp_ref[:, 0:D], wit_ref[0:D, :])
        for k in range(1, 4):
            dh += _dot(dp_ref[:, k * D:(k + 1) * D], wit_ref[k * D:(k + 1) * D, :])
        xv = x_ref[...]
        inv = lax.rsqrt(jnp.mean(xv * xv, axis=-1, keepdims=True) + RMS_EPS)
        dg1_ref[...] += jnp.sum(dh * xv * inv, axis=0, keepdims=True)
        gh = dh * g_ref[...]
        gx_ref[...] = dx1 + inv * gh - xv * (inv * inv * inv * jnp.mean(gh * xv, axis=-1, keepdims=True))

        @pl.when(step == steps - 1)
        def _():
            for cp in _block_copies(g2_ref, parts_ref, send_sems, recv_sems, local_sem):
                cp.wait()

    rr = lambda n: _rows(tm, n, rev=True, steps=steps)
    part = lambda k: pl.BlockSpec((tm, D), lambda i: (steps - 1 - i, k))
    prev8 = lambda k: pl.BlockSpec((8, D), lambda i: (jnp.maximum((steps - 1 - i) * halo - 1, 0), k))
    return pl.pallas_call(
        body, name="conv_bwd", grid=(steps,),
        in_specs=[rr(D), part(0), part(1), part(2), part(3), prev8(1), prev8(2), rr(D), _const((1, D)),
                  _const((8, D)), _const((D, D)), _const((4 * D, D)), pl.BlockSpec(memory_space=pl.ANY)],
        out_specs=[rr(D), rr(4 * D), pl.BlockSpec((8, D), lambda i: (0, 0)), pl.BlockSpec((1, D), lambda i: (0, 0)),
                   pl.BlockSpec(memory_space=pl.ANY)],
        out_shape=[jax.ShapeDtypeStruct((s, D), F32), jax.ShapeDtypeStruct((s, 4 * D), BF16),
                   jax.ShapeDtypeStruct((8, D), F32), jax.ShapeDtypeStruct((1, D), F32),
                   jax.ShapeDtypeStruct(grads2.shape, grads2.dtype)],
        scratch_shapes=[pltpu.VMEM((8, D), F32)] + list(EXCHANGE_SEMS),
        compiler_params=_params("arbitrary"),
    )(dx1, p1, p1, p1, p1, p1, p1, x, g1, cw, w1out_t, w1in_t, grads2)


def _wgrad(a, g, name):
    s, k = a.shape
    n = g.shape[1]
    bn = min(n, 1024)
    ts = min(1024, s)

    def body(a_ref, g_ref, out_ref, acc_ref):
        t = pl.program_id(1)

        @pl.when(t == 0)
        def _():
            acc_ref[...] = jnp.zeros_like(acc_ref)

        acc_ref[...] += _dot_tn(a_ref[...], g_ref[...].astype(BF16))

        @pl.when(t == s // ts - 1)
        def _():
            out_ref[...] = acc_ref[...].astype(BF16)

    return pl.pallas_call(
        body, name=name, grid=(n // bn, s // ts),
        in_specs=[pl.BlockSpec((ts, k), lambda j, t: (t, 0)), pl.BlockSpec((ts, bn), lambda j, t: (t, j))],
        out_specs=pl.BlockSpec((k, bn), lambda j, t: (0, j)),
        out_shape=jax.ShapeDtypeStruct((k, n), BF16),
        scratch_shapes=[pltpu.VMEM((k, bn), F32)],
        compiler_params=_params("arbitrary", "arbitrary"),
    )(a, g)


def _adamw_math(w, g, m, v):
    m = ADAM_B1 * m + (1.0 - ADAM_B1) * g
    v = ADAM_B2 * v + (1.0 - ADAM_B2) * (g * g)
    m_hat = m / (1.0 - ADAM_B1 ** ADAM_STEP)
    v_hat = v / (1.0 - ADAM_B2 ** ADAM_STEP)
    delta = -ADAM_LR * (m_hat / (jnp.sqrt(v_hat) + ADAM_EPS) + ADAM_WD * w)
    return delta, m, v


def _adamw_big(parts, w, m, v, name):
    _, r, c_ = parts.shape
    rb = r // ADAM_BLOCKS
    assert r % (16 * ADAM_BLOCKS) == 0

    def body(p_ref, w_ref, m_ref, v_ref, g_ref, d_ref, mo_ref, vo_ref):
        g = p_ref[0].astype(F32)
        for k in range(1, NDEV):
            g = g + p_ref[k].astype(F32)
        g_ref[...] = g
        d_ref[...], mo_ref[...], vo_ref[...] = _adamw_math(w_ref[...], g, m_ref[...], v_ref[...])

    blk = pl.BlockSpec((rb, c_), lambda i: (i, 0))
    return pl.pallas_call(
        body, name=name, grid=(r // rb,),
        in_specs=[pl.BlockSpec((NDEV, rb, c_), lambda i: (0, i, 0)), blk, blk, blk],
        out_specs=[blk, blk, blk, blk],
        out_shape=[jax.ShapeDtypeStruct((r, c_), F32)] * 4,
        compiler_params=_params("arbitrary"),
    )(parts, w, m, v)


def _sum_parts(parts, loss_row):
    _, r, c_ = parts.shape

    def body(p_ref, o_ref, l_ref):
        g = p_ref[0]
        for k in range(1, NDEV):
            g = g + p_ref[k]
        o_ref[...] = g
        per_row = jnp.sum(o_ref[loss_row:loss_row + 8, :], axis=1, keepdims=True)
        l_ref[...] = jnp.broadcast_to(jnp.sum(per_row, axis=0, keepdims=True), (8, c_))

    return pl.pallas_call(body, name="sum_small", out_shape=[jax.ShapeDtypeStruct((r, c_), F32),
                                                             jax.ShapeDtypeStruct((8, c_), F32)])(parts)


def _adamw_small(g, w, m, v):
    def body(g_ref, w_ref, m_ref, v_ref, d_ref, mo_ref, vo_ref):
        d_ref[...], mo_ref[...], vo_ref[...] = _adamw_math(w_ref[...], g_ref[...], m_ref[...], v_ref[...])

    return pl.pallas_call(body, name="adamw_small", out_shape=[jax.ShapeDtypeStruct(g.shape, F32)] * 3)(g, w, m, v)


def _pack_big(w1in, w2in, w1out, w2out, dtype):
    parts = [w1in.reshape(ROWS_W1IN, PACK_W), w2in.reshape(ROWS_W2IN, PACK_W), w1out.reshape(ROWS_WOUT, PACK_W),
             w2out.reshape(ROWS_WOUT, PACK_W)]
    used = ROWS_W1IN + ROWS_W2IN + 2 * ROWS_WOUT
    parts.append(jnp.zeros((PACK_ROWS - used, PACK_W), parts[0].dtype))
    return jnp.concatenate(parts, axis=0).astype(dtype)


def _pad_lanes(a):
    return jnp.pad(a, ((0, 0), (0, LANES - a.shape[1])))


def _tiles(a, t):
    return a.reshape(H, a.shape[1] // t, t, HA)


def kernel(x, conv_norm_g, conv_w_in, conv_w, conv_w_out, attn_norm_g, attn_w_in, attn_b_f, attn_q_norm_g, attn_k_norm_g, attn_w_out, loss_target, m_conv_norm_g, m_conv_w_in, m_conv_w, m_conv_w_out, m_attn_norm_g, m_attn_w_in, m_attn_b_f, m_attn_q_norm_g, m_attn_k_norm_g, m_attn_w_out, v_conv_norm_g, v_conv_w_in, v_conv_w, v_conv_w_out, v_attn_norm_g, v_attn_w_in, v_attn_b_f, v_attn_q_norm_g, v_attn_k_norm_g, v_attn_w_out):
    s = x.shape[1]
    tq = min(TQ, s)
    tmb = min(TM_BWD, s)
    me = 4 * lax.axis_index("x") + 2 * lax.axis_index("y") + lax.axis_index("c")
    xv, tgt = x[0], loss_target[0]

    wg = _all_gather(_pack_big(conv_w_in[0], attn_w_in[0], conv_w_out[0], attn_w_out[0], BF16), "gather_weights")
    small_w = jnp.concatenate([conv_w[0], attn_norm_g, jnp.zeros((4, 128), F32)], axis=0)
    sg_ = _all_gather(small_w, "gather_small_weights")
    a, b, c = ROWS_W1IN, ROWS_W1IN + ROWS_W2IN, ROWS_W1IN + ROWS_W2IN + ROWS_WOUT
    w1in = wg[:, :a].transpose(1, 0, 2).reshape(D, 4 * D)
    w2all = wg[:, a:b].reshape(NDEV, D, 514).transpose(1, 0, 2).reshape(D, 4 * D + H)
    w2in, wf = w2all[:, :4 * D], _pad_lanes(w2all[:, 4 * D:])
    w1out = wg[:, b:c].reshape(D, D)
    w2out = wg[:, c:c + ROWS_WOUT].reshape(D, D)
    cw = jnp.concatenate([sg_[:, 0:3, :].transpose(1, 0, 2).reshape(3, D), jnp.zeros((5, D), F32)], axis=0)
    g2 = sg_[:, 3, :].reshape(1, D)
    qg_t, kg_t = jnp.tile(attn_q_norm_g, (1, H)), jnp.tile(attn_k_norm_g, (1, H))
    bf = _pad_lanes(attn_b_f)

    e = (jnp.arange(D)[:, None] // DH == jnp.arange(LANES)[None, :]).astype(BF16)
    et2 = jnp.concatenate([e.T, e.T], axis=0)
    fold = (jnp.arange(D)[:, None] % DH == jnp.arange(LANES)[None, :]).astype(BF16)
    tril = (jnp.arange(tmb)[:, None] >= jnp.arange(tmb)[None, :]).astype(BF16)
    triu = tril.T
    src = jnp.arange(3 * LANES)
    dst = jnp.arange(H * HA)[None, :] - (HA * (src % LANES) + src // LANES)[:, None]
    place_k = ((dst == COL_BIAS) & (src % LANES < H)[:, None]).astype(BF16)
    place_q = ((dst == ROW_BIAS) & (src % LANES < H)[:, None]).astype(BF16)

    x1, h1, p1, yg = _conv_fwd(xv, conv_norm_g, w1in, cw, w1out)
    reach = 1.01 * math.sqrt(DH) * jnp.max(jnp.abs(attn_q_norm_g)) * jnp.max(jnp.abs(attn_k_norm_g))
    bounded = (2.0 * reach <= BOUNDED_SOFTMAX_REACH).astype(F32).reshape(1, 1)
    group = tq * min(FWD_TILES, s // tq)
    h2, qpre, kpre, z, qa, ka, va, vat, invq, invk, cc, fl = _attn_proj_fwd(
        x1, g2, w2in, wf, bf, qg_t, kg_t, jnp.broadcast_to(reach, (1, LANES)), e, et2, tril, place_k, place_q, group)
    ol, qab = _flash_fwd(bounded, _tiles(qa, tq), ka, vat)
    og, dx2, dz, doa, lossp = _attn_out(ol.reshape(H, s, HA), z, x1, tgt, w2out, w2out.T, e, place_k)

    dq, dk, dv = _flash_bwd(qab, _tiles(doa, tq), _tiles(ka, tq), _tiles(va, tq))
    dx1, dp2, df, dg2, dqg, dkg, dbf = _attn_proj_bwd(
        dq, dk, dv, dz, qpre, kpre, invq, invk, fl, x1, dx2, g2, qg_t, kg_t, w2in.T, wf.T, e, et2, triu, fold)
    dw2out = _wgrad(og, dx2, "wgrad_attn_out")
    dw2in = _wgrad(h2, dp2, "wgrad_attn_in")
    dwf = _wgrad(h2, df, "wgrad_attn_forget")
    dw2all = jnp.concatenate([dw2in, dwf[:, :H]], axis=1)
    grads2 = jnp.concatenate([
        dw2all.reshape(D, NDEV, 514).transpose(1, 0, 2).reshape(NDEV, ROWS_W2IN, PACK_W),
        dw2out.reshape(NDEV, ROWS_WOUT, PACK_W), jnp.zeros((NDEV, PAD_ROWS2, PACK_W), BF16)], axis=1)
    gx, dp1, dcw, dg1, parts2 = _conv_bwd(dx1, p1, xv, conv_norm_g, cw, w1out.T, w1in.T, grads2)
    dw1out = _wgrad(yg, dx1, "wgrad_conv_out")
    dw1in = _wgrad(h1, dp1, "wgrad_conv_in")
    grads1 = jnp.concatenate([dw1in.reshape(D, NDEV, 512).transpose(1, 0, 2),
                              dw1out.reshape(NDEV, ROWS_WOUT, PACK_W)], axis=1)
    parts1 = _exchange_blocks(grads1, "scatter_grads")

    def pack(t_in, t_out, pad):
        rows = [t_in[0].reshape(-1, PACK_W), t_out[0].reshape(ROWS_WOUT, PACK_W)]
        return jnp.concatenate(rows + ([jnp.zeros((pad, PACK_W), F32)] if pad else []), axis=0)

    big1 = _adamw_big(parts1, pack(conv_w_in, conv_w_out, 0), pack(m_conv_w_in, m_conv_w_out, 0),
                      pack(v_conv_w_in, v_conv_w_out, 0), "adamw_conv")
    big2 = _adamw_big(parts2, pack(attn_w_in, attn_w_out, PAD_ROWS2), pack(m_attn_w_in, m_attn_w_out, PAD_ROWS2),
                      pack(v_attn_w_in, v_attn_w_out, PAD_ROWS2), "adamw_attn")

    shard_rows = jnp.concatenate([dcw[0:3].reshape(3, NDEV, 128).transpose(1, 0, 2), dg2.reshape(NDEV, 1, 128),
                                  jnp.zeros((NDEV, 4, 128), F32)], axis=1).reshape(64, 128)
    small_g = jnp.concatenate([shard_rows, dg1.reshape(8, 128), dbf, dqg[0:1], dkg[0:1], jnp.zeros((5, 128), F32),
                               lossp.reshape(8, 128)], axis=0)
    gs, loss8 = _sum_parts(_all_gather(small_g, "gather_small_grads"), 80)
    loss = loss8[0, 0]
    mine = lax.dynamic_slice(gs, (8 * me, 0), (8, 128))
    g_small = jnp.concatenate([mine, gs[64:75], jnp.zeros((5, 128), F32)], axis=0)

    def pack_small(cwk, ang, cng, bfk, qgk, kgk):
        return jnp.concatenate([cwk[0], ang, jnp.zeros((4, 128), F32), cng.reshape(8, 128), _pad_lanes(bfk),
                                _pad_lanes(qgk), _pad_lanes(kgk), jnp.zeros((5, 128), F32)], axis=0)

    ds_, ms_, vs_ = _adamw_small(
        g_small, pack_small(conv_w, attn_norm_g, conv_norm_g, attn_b_f, attn_q_norm_g, attn_k_norm_g),
        pack_small(m_conv_w, m_attn_norm_g, m_conv_norm_g, m_attn_b_f, m_attn_q_norm_g, m_attn_k_norm_g),
        pack_small(v_conv_w, v_attn_norm_g, v_conv_norm_g, v_attn_b_f, v_attn_q_norm_g, v_attn_k_norm_g))

    def leaves(i, small):
        b1, b2 = big1[i], big2[i]
        return (small[8:16].reshape(1, D), b1[:ROWS_W1IN].reshape(1, D, 512), small[0:3].reshape(1, 3, 128),
                b1[ROWS_W1IN:].reshape(1, 128, D), small[3:4], b2[:ROWS_W2IN].reshape(1, D, 514),
                small[16:17, :H], small[17:18, :DH], small[18:19, :DH],
                b2[ROWS_W2IN:ROWS_W2IN + ROWS_WOUT].reshape(1, 128, D))

    return (loss, gx[None], *leaves(0, g_small), *leaves(1, ds_), *leaves(2, ms_), *leaves(3, vs_))
```

```python
import functools
import math

import jax
import jax.numpy as jnp
from jax import lax
from jax.experimental import pallas as pl
from jax.experimental.pallas import tpu as pltpu

F32 = jnp.float32
BF16 = jnp.bfloat16

D = 1024
H = 16
DH = 64
NDEV = 8
RMS_EPS = 1e-6
LANES = 128
HA = 128
TM_FWD = 512
TM_BWD = 256
TQ = 512
AUG_ROWS = 80
FWD_TILES = 8
BWD_TILES = 4
CH = 256
PACK_W = 512
ROWS_W1IN, ROWS_W2IN, ROWS_WOUT = 1024, 1028, 256
ADAM_BLOCKS = 4
PAD_ROWS2 = 60
ADAM_LR, ADAM_B1, ADAM_B2, ADAM_EPS, ADAM_WD, ADAM_STEP = 0.001, 0.9, 0.999, 1e-08, 0.01, 10
VMEM_LIMIT = 56 * 1024 * 1024
MASKED = -1e30
BOUNDED_SOFTMAX_REACH = 60.0
MESH = pl.DeviceIdType.MESH


def _params(*sem, vmem=VMEM_LIMIT):
    return pltpu.CompilerParams(dimension_semantics=sem or None, vmem_limit_bytes=vmem)


def _const(shape):
    nd = len(shape)
    return pl.BlockSpec(shape, lambda *_: (0,) * nd, pipeline_mode=pl.Buffered(1))


def _rows(tm, n, rev=False, steps=None):
    if rev:
        return pl.BlockSpec((tm, n), lambda i: (steps - 1 - i, 0))
    return pl.BlockSpec((tm, n), lambda i: (i, 0))


def _dot(a, b):
    return jnp.dot(a, b, preferred_element_type=F32)


def _top16(x):
    bits = lax.bitcast_convert_type(x, jnp.uint32) & jnp.uint32(0xFFFF0000)
    return lax.bitcast_convert_type(bits, F32)


def _split2(x):
    hi = _top16(x)
    return hi.astype(BF16), (x - hi).astype(BF16)


def _split3(x):
    hi = _top16(x)
    r = x - hi
    mid = _top16(r)
    return hi.astype(BF16), mid.astype(BF16), (r - mid).astype(BF16)


def _seg_sum(a, e):
    hi, lo = _split2(a)
    return _dot(hi, e) + _dot(lo, e)


def _seg_bcast(s, et2):
    return _dot(jnp.concatenate(_split2(s), axis=1), et2)


def _tri_sum(t, v):
    hi, mid, lo = _split3(v)
    return _dot(t, hi) + _dot(t, mid) + _dot(t, lo)


def _sigmoid(z):
    return 1.0 / (1.0 + jnp.exp(-z))


def _place():
    return lax.axis_index("x"), lax.axis_index("y"), lax.axis_index("c")


def _all_gather(xb, name):
    r, c_ = xb.shape

    def body(x_ref, out_ref, send_sems, recv_sems, local_sem):
        x, y, c = _place()
        me, sibling = (x, y, c), (x, y, 1 - c)
        chips = [(1 - x, y), (x, 1 - y), (1 - x, 1 - y)]

        def slab(px, py, pc):
            return out_ref.at[4 * px + 2 * py + pc]

        def copy(k, block, to, src=None):
            return pltpu.make_async_remote_copy(
                src_ref=slab(*block) if src is None else src, dst_ref=slab(*block),
                send_sem=send_sems.at[k], recv_sem=recv_sems.at[k], device_id=to, device_id_type=MESH)

        mine = pltpu.make_async_copy(x_ref, slab(*me), local_sem)
        mine.start()
        first = [copy(0, me, sibling, src=x_ref)]
        first += [copy(1 + j, me, (*chip, c), src=x_ref) for j, chip in enumerate(chips)]
        for cp in first:
            cp.start()
        passed = [copy(4 + j, (*chip, c), sibling) for j, chip in enumerate(chips)]
        for j, chip in enumerate(chips):
            copy(1 + j, (*chip, c), me).wait_recv()
            passed[j].start()
        copy(0, sibling, me).wait_recv()
        for j, chip in enumerate(chips):
            copy(4 + j, (*chip, 1 - c), me).wait_recv()
        for cp in first + passed:
            cp.wait_send()
        mine.wait()

    return pl.pallas_call(
        body, name=name,
        out_shape=jax.ShapeDtypeStruct((NDEV, r, c_), xb.dtype),
        in_specs=[pl.BlockSpec(memory_space=pl.ANY)],
        out_specs=pl.BlockSpec(memory_space=pl.ANY),
        scratch_shapes=[pltpu.SemaphoreType.DMA((7,)), pltpu.SemaphoreType.DMA((7,)), pltpu.SemaphoreType.DMA],
    )(xb)


def _block_copies(g_ref, out_ref, send_sems, recv_sems, local_sem):
    x, y, c = _place()
    copies = [pltpu.make_async_copy(g_ref.at[4 * x + 2 * y + c], out_ref.at[0], local_sem)]
    for k in range(1, NDEV):
        px = 1 - x if k & 4 else x
        py = 1 - y if k & 2 else y
        pc = 1 - c if k & 1 else c
        copies.append(pltpu.make_async_remote_copy(
            src_ref=g_ref.at[4 * px + 2 * py + pc], dst_ref=out_ref.at[k],
            send_sem=send_sems.at[k - 1], recv_sem=recv_sems.at[k - 1],
            device_id=(px, py, pc), device_id_type=MESH))
    return copies


def _gather_copies(x_ref, out_ref, send_sems, recv_sems, local_sem):
    x, y, c = _place()
    mine = out_ref.at[4 * x + 2 * y + c]
    copies = [pltpu.make_async_copy(x_ref, mine, local_sem)]
    for k in range(1, NDEV):
        peer = (1 - x if k & 4 else x, 1 - y if k & 2 else y, 1 - c if k & 1 else c)
        copies.append(pltpu.make_async_remote_copy(
            src_ref=x_ref, dst_ref=mine, send_sem=send_sems.at[k - 1], recv_sem=recv_sems.at[k - 1],
            device_id=peer, device_id_type=MESH))
    return copies


EXCHANGE_SEMS = [pltpu.SemaphoreType.DMA((7,)), pltpu.SemaphoreType.DMA((7,)), pltpu.SemaphoreType.DMA]


def _exchange_blocks(g, name):
    _, r, c_ = g.shape

    def body(g_ref, out_ref, send_sems, recv_sems, local_sem):
        copies = _block_copies(g_ref, out_ref, send_sems, recv_sems, local_sem)
        for cp in copies:
            cp.start()
        for cp in copies:
            cp.wait()

    return pl.pallas_call(
        body, name=name,
        out_shape=jax.ShapeDtypeStruct((NDEV, r, c_), g.dtype),
        in_specs=[pl.BlockSpec(memory_space=pl.ANY)],
        out_specs=pl.BlockSpec(memory_space=pl.ANY),
        scratch_shapes=list(EXCHANGE_SEMS),
    )(g)


def _conv_fwd(x, g1, w1in, cw, w1out, shard2):
    s = x.shape[0]
    tm = min(TM_FWD, s)
    steps = s // tm

    def body(x_ref, g_ref, win_ref, cw_ref, wout_ref, sh_ref, x1_ref, h1_ref, p1_ref, yg_ref, all_ref, tail_ref,
             send_sems, recv_sems, local_sem):
        @pl.when(pl.program_id(0) == 0)
        def _():
            tail_ref[...] = jnp.zeros_like(tail_ref)
            for cp in _gather_copies(sh_ref, all_ref, send_sems, recv_sems, local_sem):
                cp.start()

        xv = x_ref[...]
        inv = lax.rsqrt(jnp.mean(xv * xv, axis=-1, keepdims=True) + RMS_EPS)
        h = (xv * inv * g_ref[...]).astype(BF16)
        h1_ref[...] = h
        row = lax.broadcasted_iota(jnp.int32, (tm, CH), 0)
        for ci in range(D // CH):
            lo, hi = ci * CH, (ci + 1) * CH
            parts = []
            for k in range(4):
                pk = _dot(h, win_ref[:, k * D + lo:k * D + hi]).astype(BF16)
                p1_ref[:, k * D + lo:k * D + hi] = pk
                parts.append(pk.astype(F32))
            b, c, xin, z = parts
            u = c * xin
            t6 = tail_ref[6:7, lo:hi]
            t7 = tail_ref[7:8, lo:hi]
            u1 = jnp.where(row == 0, t7, pltpu.roll(u, 1, 0))
            u2 = jnp.where(row == 0, t6, jnp.where(row == 1, t7, pltpu.roll(u, 2, 0)))
            tail_ref[:, lo:hi] = u[tm - 8:, :]
            y = cw_ref[2:3, lo:hi] * u + cw_ref[1:2, lo:hi] * u1 + cw_ref[0:1, lo:hi] * u2
            yg_ref[:, lo:hi] = (b * y * (z * _sigmoid(z))).astype(BF16)
        x1_ref[...] = xv + _dot(yg_ref[...], wout_ref[...])

        @pl.when(pl.program_id(0) == steps - 1)
        def _():
            for cp in _gather_copies(sh_ref, all_ref, send_sems, recv_sems, local_sem):
                cp.wait()

    return pl.pallas_call(
        body, name="conv_fwd", grid=(steps,),
        in_specs=[_rows(tm, D), _const((1, D)), _const((D, 4 * D)), _const((8, D)), _const((D, D)),
                  pl.BlockSpec(memory_space=pl.ANY)],
        out_specs=[_rows(tm, D), _rows(tm, D), _rows(tm, 4 * D), _rows(tm, D), pl.BlockSpec(memory_space=pl.ANY)],
        out_shape=[jax.ShapeDtypeStruct((s, D), F32), jax.ShapeDtypeStruct((s, D), BF16),
                   jax.ShapeDtypeStruct((s, 4 * D), BF16), jax.ShapeDtypeStruct((s, D), BF16),
                   jax.ShapeDtypeStruct((NDEV,) + shard2.shape, shard2.dtype)],
        scratch_shapes=[pltpu.VMEM((8, D), F32)] + list(EXCHANGE_SEMS),
        compiler_params=_params("arbitrary"),
    )(x, g1, w1in, cw, w1out, shard2)


COL_BIAS = DH
ROW_BIAS = DH + 3


def _terms_cat(val):
    return jnp.concatenate(_split3(val), axis=1)


def _split_heads(x, aug, out_ref, lane, t_ref=None):
    for hp in range(H // 2):
        pair = x[:, hp * HA:(hp + 1) * HA]
        for k, feat in enumerate((pair, pltpu.roll(pair, DH, 1))):
            block = jnp.where(lane < DH, feat, aug(2 * hp + k))
            out_ref[2 * hp + k] = block.astype(BF16)
            if t_ref is not None:
                t_ref[2 * hp + k] = block.T[0:t_ref.shape[1], :].astype(BF16)


def _ones_at(lane, first):
    return jnp.where((lane >= first) & (lane < first + 3), 1.0, 0.0)


def _attn_proj_fwd(x1, g2, w2in, wf, bf, qg, kg, shift, e, et, tril, place_k, place_q, group):
    s = x1.shape[0]
    tm = min(TM_BWD, s)
    steps = s // tm
    per = group // tm

    def body(x_ref, g_ref, win_ref, wf_ref, bf_ref, qg_ref, kg_ref, sh_ref, e_ref, et_ref, tril_ref, pk_ref, pq_ref,
             h2_ref, qp_ref, kp_ref, z_ref, qa_ref, ka_ref, va_ref, vat_ref, iq_ref, ik_ref, c_ref, fl_ref, carry_ref):
        @pl.when(pl.program_id(0) == 0)
        def _():
            carry_ref[...] = jnp.zeros_like(carry_ref)

        xv = x_ref[...]
        inv = lax.rsqrt(jnp.mean(xv * xv, axis=-1, keepdims=True) + RMS_EPS)
        h = (xv * inv * g_ref[...]).astype(BF16)
        h2_ref[...] = h

        fl = _dot(h, wf_ref[...]) + bf_ref[...]
        fl_ref[...] = fl
        ex = jnp.exp(-jnp.abs(fl))
        up = 1.0 + ex
        log1p = jnp.where(up == 1.0, ex, jnp.log(up) * ex / (up - 1.0))
        lane = lax.broadcasted_iota(jnp.int32, (tm, LANES), 1)
        logf = jnp.where(lane < H, jnp.minimum(fl, 0.0) - log1p, 0.0)
        c_ref[...] = _tri_sum(tril_ref[...], logf) + carry_ref[0:1, :]
        carry_ref[0:1, :] = c_ref[tm - 1:tm, :]
        c = c_ref[...]

        def normed(col, pre_ref, inv_ref, gain_ref, scale):
            pre = _dot(h, win_ref[:, col * D:(col + 1) * D]).astype(BF16)
            pre_ref[...] = pre
            t = pre.astype(F32)
            invh = lax.rsqrt(_seg_sum(t * t, e_ref[...]) * (1.0 / DH) + RMS_EPS)
            inv_ref[...] = invh
            return t * _seg_bcast(invh, et_ref[...]) * (gain_ref[...] * scale)

        ones_col, ones_row = _ones_at(lane, COL_BIAS), _ones_at(lane, ROW_BIAS)
        q_bias = _dot(_terms_cat(c - sh_ref[...]), pq_ref[...])
        _split_heads(normed(0, qp_ref, iq_ref, qg_ref, 1.0 / math.sqrt(DH)),
                     lambda hh: q_bias[:, hh * HA:(hh + 1) * HA] + ones_col, qa_ref, lane)
        k_bias = _dot(_terms_cat(-c), pk_ref[...])
        _split_heads(normed(1, kp_ref, ik_ref, kg_ref, 1.0),
                     lambda hh: k_bias[:, hh * HA:(hh + 1) * HA] + ones_row, ka_ref, lane)
        v = _dot(h, win_ref[:, 2 * D:3 * D])
        _split_heads(v, lambda hh: ones_col, va_ref, lane, vat_ref)
        z_ref[...] = _dot(h, win_ref[:, 3 * D:4 * D]).astype(BF16)

    row_bf = lambda: _rows(tm, D)
    row_sm = lambda: _rows(tm, LANES)
    heads = lambda: pl.BlockSpec((H, tm, HA), lambda i: (0, i, 0))
    return pl.pallas_call(
        body, name="attn_proj_fwd", grid=(steps,),
        in_specs=[_rows(tm, D), _const((1, D)), _const((D, 4 * D)), _const((D, LANES)), _const((1, LANES)),
                  _const((1, D)), _const((1, D)), _const((1, LANES)), _const((D, LANES)), _const((2 * LANES, D)),
                  _const((tm, tm)), _const((3 * LANES, H * HA)), _const((3 * LANES, H * HA))],
        out_specs=[row_bf() for _ in range(4)] + [heads() for _ in range(3)] + [
            pl.BlockSpec((H, None, AUG_ROWS, tm), lambda i: (0, i // per, 0, i % per))] + [row_sm() for _ in range(4)],
        out_shape=[jax.ShapeDtypeStruct((s, D), BF16)] * 4 + [jax.ShapeDtypeStruct((H, s, HA), BF16)] * 3 + [
            jax.ShapeDtypeStruct((H, s // group, AUG_ROWS, group), BF16)] + [jax.ShapeDtypeStruct((s, LANES), F32)] * 4,
        scratch_shapes=[pltpu.VMEM((8, LANES), F32)],
        compiler_params=_params("arbitrary"),
    )(x1, g2, w2in, wf, bf, qg, kg, shift, e, et, tril, place_k, place_q)


def _dot_nt(a, b):
    return lax.dot_general(a, b, (((1,), (1,)), ((), ())), preferred_element_type=F32)


def _dot_tn(a, b):
    return lax.dot_general(a, b, (((0,), (0,)), ((), ())), preferred_element_type=F32)


def _flash_fwd(bounded, qa, ka, vat):
    h_, nq, tq, _ = qa.shape
    s, tk = ka.shape[1], tq
    group = vat.shape[3]
    gt = group // tk
    shift = gt.bit_length() - 1

    def body(flag_ref, q_ref, k_ref, vt_ref, o_ref, qb_ref, m_ref, acct_ref):
        i = pl.program_id(1)
        lane = lax.broadcasted_iota(jnp.int32, (tq, HA), 1)
        in_bias = (lane >= ROW_BIAS) & (lane < ROW_BIAS + 3)
        acct_ref[...] = jnp.zeros_like(acct_ref)
        g = lax.shift_right_logical(i, shift)
        r = i & (gt - 1)

        def run(use_bound):
            q = q_ref[...]
            if not use_bound:
                q = jnp.where(in_bias, jnp.zeros_like(q), q)
                m_ref[...] = jnp.full_like(m_ref, MASKED)

            def step(first_key, vt, tiles, diagonal_at):
                keys = pl.ds(pl.multiple_of(first_key, tk), tiles * tk)
                zt = _dot_nt(k_ref[keys, :], q)
                if diagonal_at is not None:
                    key = lax.broadcasted_iota(jnp.int32, (tiles * tk, tq), 0)
                    qry = lax.broadcasted_iota(jnp.int32, (tiles * tk, tq), 1)
                    zt = jnp.where(key <= qry + diagonal_at * tk, zt, MASKED)
                if use_bound:
                    acct_ref[0:AUG_ROWS, :] += _dot(vt, jnp.exp(zt).astype(BF16))
                else:
                    m_old = m_ref[...]
                    m_new = jnp.maximum(m_old, jnp.max(zt, axis=0, keepdims=True))
                    pt = jnp.exp(zt - m_new)
                    acct_ref[0:AUG_ROWS, :] = jnp.exp(m_old - m_new) * acct_ref[0:AUG_ROWS, :] + _dot(
                        vt, pt.astype(BF16))
                    m_ref[...] = m_new

            def whole_group(jj, carry):
                step(jj * group, vt_ref[jj], gt, None)
                return carry

            lax.fori_loop(0, g, whole_group, 0)
            for rr in range(gt):
                pl.when(r == rr)(functools.partial(
                    lambda rr: step(g * group, vt_ref[g, :, 0:(rr + 1) * tk], rr + 1, rr), rr))

            if not use_bound:
                acct_ref[COL_BIAS + 1:COL_BIAS + 2, :] = m_ref[...]
            acc = acct_ref[...].T
            l = jnp.sum(jnp.where(lane == COL_BIAS, acc, 0.0), axis=1, keepdims=True)
            if use_bound:
                m = -jnp.sum(jnp.where(in_bias, q.astype(F32), 0.0), axis=1, keepdims=True)
            else:
                m = jnp.sum(jnp.where(lane == COL_BIAS + 1, acc, 0.0), axis=1, keepdims=True)
            lse = m + jnp.log(l)
            o_ref[...] = jnp.where(lane < DH, acc / l, lse)
            hi, mid, lo = _split3(-lse)
            qb_ref[...] = jnp.where(lane == ROW_BIAS, hi, jnp.where(lane == ROW_BIAS + 1, mid, jnp.where(
                lane == ROW_BIAS + 2, lo, q_ref[...])))

        use_bound = flag_ref[0, 0] > 0.5
        pl.when(use_bound)(lambda: run(True))
        pl.when(jnp.logical_not(use_bound))(lambda: run(False))

    tile = lambda: pl.BlockSpec((None, None, tq, HA), lambda h, i: (h, i, 0, 0))
    return pl.pallas_call(
        body, name="flash_fwd", grid=(h_, nq),
        in_specs=[pl.BlockSpec(memory_space=pltpu.SMEM), tile(), pl.BlockSpec((None, s, HA), lambda h, i: (h, 0, 0)),
                  pl.BlockSpec((None, s // group, AUG_ROWS, group), lambda h, i: (h, 0, 0, 0))],
        out_specs=[tile(), tile()],
        out_shape=[jax.ShapeDtypeStruct((h_, nq, tq, HA), F32), jax.ShapeDtypeStruct((h_, nq, tq, HA), BF16)],
        scratch_shapes=[pltpu.VMEM((1, tq), F32), pltpu.VMEM((HA, tq), F32)],
        compiler_params=_params("arbitrary", "arbitrary"),
    )(bounded, qa, ka, vat)


def _merge_heads(src, out_ref, lane):
    for hp in range(H // 2):
        out_ref[:, hp * HA:(hp + 1) * HA] = jnp.where(lane < DH, src(2 * hp), pltpu.roll(src(2 * hp + 1), DH, 1))


def _attn_out(ol, z, x1, tgt, w2out, w2out_t, e, place_k):
    s = z.shape[0]
    tm = min(TM_BWD, s)
    steps = s // tm

    def body(ol_ref, z_ref, x1_ref, t_ref, w_ref, wt_ref, e_ref, pk_ref, og_ref, dx2_ref, dz_ref, doa_ref, loss_ref,
             o_ref):
        @pl.when(pl.program_id(0) == 0)
        def _():
            loss_ref[...] = jnp.zeros_like(loss_ref)

        lane = lax.broadcasted_iota(jnp.int32, (tm, HA), 1)
        _merge_heads(lambda hh: ol_ref[hh], o_ref, lane)
        ov = o_ref[...]
        zv = z_ref[...].astype(F32)
        sg = _sigmoid(zv)
        sil = zv * sg
        og = (ov * sil).astype(BF16)
        og_ref[...] = og
        err = x1_ref[...] + _dot(og, w_ref[...]) - t_ref[...]
        loss_ref[...] += (0.5 / D) * jnp.sum(err * err, axis=0, keepdims=True)
        dx2 = err * (1.0 / D)
        dx2_ref[...] = dx2
        dog = _dot(dx2.astype(BF16), wt_ref[...])
        do = (dog * sil).astype(BF16).astype(F32)
        dz_ref[...] = (dog * ov * (sg * (1.0 + zv * (1.0 - sg)))).astype(BF16)
        delta = _seg_sum(do * ov, e_ref[...])
        d_bias = _dot(_terms_cat(-delta), pk_ref[...])
        _split_heads(do, lambda hh: d_bias[:, hh * HA:(hh + 1) * HA], doa_ref, lane)

    heads = lambda: pl.BlockSpec((H, tm, HA), lambda i: (0, i, 0))
    return pl.pallas_call(
        body, name="attn_out", grid=(steps,),
        in_specs=[heads(), _rows(tm, D), _rows(tm, D), _rows(tm, D), _const((D, D)), _const((D, D)),
                  _const((D, LANES)), _const((3 * LANES, H * HA))],
        out_specs=[_rows(tm, D), _rows(tm, D), _rows(tm, D), heads(), pl.BlockSpec((1, D), lambda i: (0, 0))],
        out_shape=[jax.ShapeDtypeStruct((s, D), BF16), jax.ShapeDtypeStruct((s, D), F32),
                   jax.ShapeDtypeStruct((s, D), BF16), jax.ShapeDtypeStruct((H, s, HA), BF16),
                   jax.ShapeDtypeStruct((1, D), F32)],
        scratch_shapes=[pltpu.VMEM((tm, D), F32)],
        compiler_params=_params("arbitrary"),
    )(ol, z, x1, tgt, w2out, w2out_t, e, place_k)


def _flash_bwd(qab, doa, ka, va):
    h_, nq, tq, _ = qab.shape
    nk, tk = ka.shape[1], ka.shape[2]
    s = nk * tk
    gt = min(BWD_TILES, nq)
    shift = gt.bit_length() - 1

    def body(qa_ref, da_ref, ka_ref, va_ref, dq_hbm, dk_ref, dv_ref, dq_acc, dkt_ref, dvt_ref, sem):
        hh = pl.program_id(0)
        j = pl.program_id(1)

        @pl.when(j == 0)
        def _():
            dq_acc[...] = jnp.zeros_like(dq_acc)

        dkt_ref[...] = jnp.zeros_like(dkt_ref)
        dvt_ref[...] = jnp.zeros_like(dvt_ref)
        ka_v = ka_ref[...]
        va_v = va_ref[...]

        def step(i, tiles, masked):
            rows = pl.ds(pl.multiple_of(i * tq, tq), tiles * tq)
            qa_i, da_i = qa_ref[rows, :], da_ref[rows, :]
            zz = _dot_nt(qa_i, ka_v)
            if masked:
                row = lax.broadcasted_iota(jnp.int32, (tiles * tq, tk), 0)
                col = lax.broadcasted_iota(jnp.int32, (tiles * tq, tk), 1)
                zz = jnp.where(col <= row, zz, MASKED)
            p = jnp.exp(zz)
            ds = (p * _dot_nt(da_i, va_v)).astype(BF16)
            pb = p.astype(BF16)
            dq_acc[rows, :] += _dot(ds, ka_v)
            dvt_ref[...] += _dot_tn(da_i, pb)
            dkt_ref[...] += _dot_tn(qa_i, ds)

        g0 = lax.shift_right_logical(j, shift)
        for rr in range(gt):
            pl.when((j & (gt - 1)) == rr)(functools.partial(step, j, gt - rr, True))

        def whole_group(g, carry):
            step(g * gt, gt, False)
            return carry

        lax.fori_loop(g0 + 1, nq // gt, whole_group, 0)
        dk_ref[...] = dkt_ref[...].T
        dv_ref[...] = dvt_ref[...].T.astype(BF16)

        @pl.when(j == nk - 1)
        def _():
            cp = pltpu.make_async_copy(dq_acc, dq_hbm.at[hh], sem)
            cp.start()
            cp.wait()

    whole = lambda: pl.BlockSpec((None, nq * tq, HA), lambda h, j: (h, 0, 0))
    tile = lambda: pl.BlockSpec((None, None, tk, HA), lambda h, j: (h, j, 0, 0))
    rows = lambda: pl.BlockSpec((None, tk, HA), lambda h, j: (h, j, 0))
    return pl.pallas_call(
        body, name="flash_bwd", grid=(h_, nk),
        in_specs=[whole(), whole(), tile(), tile()],
        out_specs=[pl.BlockSpec(memory_space=pl.ANY), rows(), rows()],
        out_shape=[jax.ShapeDtypeStruct((h_, s, HA), F32)] * 2 + [jax.ShapeDtypeStruct((h_, s, HA), BF16)],
        scratch_shapes=[pltpu.VMEM((s, HA), F32), pltpu.VMEM((HA, tk), F32), pltpu.VMEM((HA, tk), F32),
                        pltpu.SemaphoreType.DMA],
        compiler_params=_params("arbitrary", "arbitrary"),
    )(qab.reshape(h_, s, HA), doa.reshape(h_, s, HA), ka, va)


def _attn_proj_bwd(dq, dk, dv, dz, qpre, kpre, invq, invk, fl, x1, dx2, g2, qg, kg, w2in_t, wf_t, e, et,
                   triu, fold):
    s = x1.shape[0]
    tm = min(TM_BWD, s)
    steps = s // tm

    def body(dq_ref, dk_ref, dvh_ref, dz_ref, qp_ref, kp_ref, iq_ref, ik_ref, fl_ref, x1_ref, dx2_ref,
             g2_ref, qg_ref, kg_ref, wt_ref, wft_ref, e_ref, et_ref, triu_ref, fold_ref,
             dx1_ref, dp_ref, df_ref, dg2_ref, dqg_ref, dkg_ref, dbf_ref, carry_ref, rc_ref, qcol_ref, kcol_ref,
             dqs_ref, dkn_ref, dv_ref):
        step = pl.program_id(0)
        lane = lax.broadcasted_iota(jnp.int32, (tm, HA), 1)
        _merge_heads(lambda hh: dvh_ref[hh].astype(F32), dv_ref, lane)
        dc = jnp.zeros((tm, HA), F32)
        for hp in range(H // 2):
            dqh = [dq_ref[2 * hp], dq_ref[2 * hp + 1]]
            dkh = [dk_ref[2 * hp], dk_ref[2 * hp + 1]]
            dqs_ref[:, hp * HA:(hp + 1) * HA] = jnp.where(lane < DH, dqh[0], pltpu.roll(dqh[1], DH, 1))
            dkn_ref[:, hp * HA:(hp + 1) * HA] = jnp.where(lane < DH, dkh[0], pltpu.roll(dkh[1], DH, 1))
            for k in range(2):
                sums = jnp.where(lane == ROW_BIAS, dqh[k], 0.0) - pltpu.roll(
                    jnp.where(lane == COL_BIAS, dkh[k], 0.0), ROW_BIAS - COL_BIAS, 1)
                dc = dc + pltpu.roll(sums, (2 * hp + k - ROW_BIAS) % HA, 1)

        @pl.when(step == 0)
        def _():
            carry_ref[...] = jnp.zeros_like(carry_ref)
            dg2_ref[...] = jnp.zeros_like(dg2_ref)
            dbf_ref[...] = jnp.zeros_like(dbf_ref)
            qcol_ref[...] = jnp.zeros_like(qcol_ref)
            kcol_ref[...] = jnp.zeros_like(kcol_ref)
            dqg_ref[...] = jnp.zeros_like(dqg_ref)
            dkg_ref[...] = jnp.zeros_like(dkg_ref)

        rc_ref[...] = _tri_sum(triu_ref[...], dc) + carry_ref[0:1, :]
        carry_ref[0:1, :] = rc_ref[0:1, :]
        df = rc_ref[...] * _sigmoid(-fl_ref[...])
        dfb = df.astype(BF16)
        df_ref[...] = dfb
        dbf_ref[...] += jnp.sum(df, axis=0, keepdims=True)

        def norm_bwd(dn, pre_ref, inv_ref, gain_ref, col_ref):
            pre = pre_ref[...].astype(F32)
            invh = inv_ref[...]
            invb = _seg_bcast(invh, et_ref[...])
            col_ref[...] += jnp.sum(dn * pre * invb, axis=0, keepdims=True)
            gd = dn * gain_ref[...]
            mean = _seg_sum(gd * pre, e_ref[...]) * (1.0 / DH)
            return invb * gd - pre * _seg_bcast(mean * invh * invh * invh, et_ref[...])

        dq = norm_bwd(dqs_ref[...] * (1.0 / math.sqrt(DH)), qp_ref, iq_ref, qg_ref, qcol_ref).astype(BF16)
        dp_ref[:, 0:D] = dq
        dh = _dot(dq, wt_ref[0:D, :])
        dk = norm_bwd(dkn_ref[...], kp_ref, ik_ref, kg_ref, kcol_ref).astype(BF16)
        dp_ref[:, D:2 * D] = dk
        dh += _dot(dk, wt_ref[D:2 * D, :])
        dvb = dv_ref[...].astype(BF16)
        dp_ref[:, 2 * D:3 * D] = dvb
        dh += _dot(dvb, wt_ref[2 * D:3 * D, :])
        dzb = dz_ref[...]
        dp_ref[:, 3 * D:4 * D] = dzb
        dh += _dot(dzb, wt_ref[3 * D:4 * D, :])
        dh += _dot(dfb, wft_ref[...])

        xv = x1_ref[...]
        inv = lax.rsqrt(jnp.mean(xv * xv, axis=-1, keepdims=True) + RMS_EPS)
        dg2_ref[...] += jnp.sum(dh * xv * inv, axis=0, keepdims=True)
        gh = dh * g2_ref[...]
        dx1_ref[...] = dx2_ref[...] + inv * gh - xv * (inv * inv * inv * jnp.mean(gh * xv, axis=-1, keepdims=True))

        @pl.when(step == steps - 1)
        def _():
            dqg_ref[...] = _fold_heads(qcol_ref[...], fold_ref[...])
            dkg_ref[...] = _fold_heads(kcol_ref[...], fold_ref[...])

    rr = lambda n: _rows(tm, n, rev=True, steps=steps)
    acc = lambda n, r=1: pl.BlockSpec((r, n), lambda i: (0, 0))
    heads = lambda: pl.BlockSpec((H, tm, HA), lambda i: (0, steps - 1 - i, 0))
    return pl.pallas_call(
        body, name="attn_proj_bwd", grid=(steps,),
        in_specs=[heads(), heads(), heads(),
                  rr(D), rr(D), rr(D), rr(LANES), rr(LANES), rr(LANES), rr(D), rr(D),
                  _const((1, D)), _const((1, D)), _const((1, D)), _const((4 * D, D)), _const((LANES, D)),
                  _const((D, LANES)), _const((2 * LANES, D)), _const((tm, tm)), _const((D, LANES))],
        out_specs=[rr(D), rr(4 * D), rr(LANES), acc(D), acc(LANES, 8), acc(LANES, 8), acc(LANES)],
        out_shape=[jax.ShapeDtypeStruct((s, D), F32), jax.ShapeDtypeStruct((s, 4 * D), BF16),
                   jax.ShapeDtypeStruct((s, LANES), BF16), jax.ShapeDtypeStruct((1, D), F32),
                   jax.ShapeDtypeStruct((8, LANES), F32), jax.ShapeDtypeStruct((8, LANES), F32),
                   jax.ShapeDtypeStruct((1, LANES), F32)],
        scratch_shapes=[pltpu.VMEM((8, LANES), F32), pltpu.VMEM((tm, LANES), F32), pltpu.VMEM((1, D), F32),
                        pltpu.VMEM((1, D), F32), pltpu.VMEM((tm, D), F32), pltpu.VMEM((tm, D), F32),
                        pltpu.VMEM((tm, D), F32)],
        compiler_params=_params("arbitrary"),
    )(dq, dk, dv, dz, qpre, kpre, invq, invk, fl, x1, dx2, g2, qg, kg, w2in_t, wf_t, e, et, triu, fold)


def _fold_heads(col, fold):
    hi, mid, lo = _split3(jnp.broadcast_to(col, (8, D)))
    return _dot(hi, fold) + _dot(mid, fold) + _dot(lo, fold)


def _conv_bwd(dx1, p1, x, g1, cw, w1out_t, w1in_t, grads2):
    s = x.shape[0]
    tm = min(TM_BWD, s)
    steps = s // tm
    halo = tm // 8

    def body(dx1_ref, b_ref, c_ref, xi_ref, z_ref, ch_ref, xh_ref, x_ref, g_ref, cw_ref, wot_ref, wit_ref, g2_ref,
             gx_ref, dp_ref, dcw_ref, dg1_ref, parts_ref, head_ref, send_sems, recv_sems, local_sem):
        step = pl.program_id(0)

        @pl.when(step == 0)
        def _():
            for cp in _block_copies(g2_ref, parts_ref, send_sems, recv_sems, local_sem):
                cp.start()

        @pl.when(step == 0)
        def _():
            head_ref[...] = jnp.zeros_like(head_ref)
            dcw_ref[...] = jnp.zeros_like(dcw_ref)
            dg1_ref[...] = jnp.zeros_like(dg1_ref)

        first_tile = step == steps - 1
        dx1 = dx1_ref[...]
        row = lax.broadcasted_iota(jnp.int32, (tm, CH), 0)
        dyb = dx1.astype(BF16)
        for ci in range(D // CH):
            lo, hi = ci * CH, (ci + 1) * CH
            dyg = _dot(dyb, wot_ref[:, lo:hi])
            b = b_ref[:, lo:hi].astype(F32)
            c = c_ref[:, lo:hi].astype(F32)
            xin = xi_ref[:, lo:hi].astype(F32)
            z = z_ref[:, lo:hi].astype(F32)
            u = c * xin
            t6 = jnp.where(first_tile, 0.0, ch_ref[6:7, lo:hi].astype(F32) * xh_ref[6:7, lo:hi].astype(F32))
            t7 = jnp.where(first_tile, 0.0, ch_ref[7:8, lo:hi].astype(F32) * xh_ref[7:8, lo:hi].astype(F32))
            u1 = jnp.where(row == 0, t7, pltpu.roll(u, 1, 0))
            u2 = jnp.where(row == 0, t6, jnp.where(row == 1, t7, pltpu.roll(u, 2, 0)))
            w0, w1, w2 = cw_ref[0:1, lo:hi], cw_ref[1:2, lo:hi], cw_ref[2:3, lo:hi]
            y = w2 * u + w1 * u1 + w0 * u2
            sg = _sigmoid(z)
            sil = z * sg
            dp_ref[:, lo:hi] = (dyg * y * sil).astype(BF16)
            dy = dyg * b * sil
            dp_ref[:, 3 * D + lo:3 * D + hi] = (dyg * b * y * (sg * (1.0 + z * (1.0 - sg)))).astype(BF16)
            dcw_ref[2:3, lo:hi] += jnp.sum(dy * u, axis=0, keepdims=True)
            dcw_ref[1:2, lo:hi] += jnp.sum(dy * u1, axis=0, keepdims=True)
            dcw_ref[0:1, lo:hi] += jnp.sum(dy * u2, axis=0, keepdims=True)
            n0 = head_ref[0:1, lo:hi]
            n1 = head_ref[1:2, lo:hi]
            dyn1 = jnp.where(row == tm - 1, n0, pltpu.roll(dy, tm - 1, 0))
            dyn2 = jnp.where(row == tm - 2, n0, jnp.where(row == tm - 1, n1, pltpu.roll(dy, tm - 2, 0)))
            head_ref[:, lo:hi] = dy[0:8, :]
            du = w2 * dy + w1 * dyn1 + w0 * dyn2
            dp_ref[:, D + lo:D + hi] = (du * xin).astype(BF16)
            dp_ref[:, 2 * D + lo:2 * D + hi] = (du * c).astype(BF16)
        dh = _dot(dp_ref[:, 0:D], wit_ref[0:D, :])
        for k in range(1, 4):
            dh += _dot(dp_ref[:, k * D:(k + 1) * D], wit_ref[k * D:(k + 1) * D, :])
        xv = x_ref[...]
        inv = lax.rsqrt(jnp.mean(xv * xv, axis=-1, keepdims=True) + RMS_EPS)
        dg1_ref[...] += jnp.sum(dh * xv * inv, axis=0, keepdims=True)
        gh = dh * g_ref[...]
        gx_ref[...] = dx1 + inv * gh - xv * (inv * inv * inv * jnp.mean(gh * xv, axis=-1, keepdims=True))

        @pl.when(step == steps - 1)
        def _():
            for cp in _block_copies(g2_ref, parts_ref, send_sems, recv_sems, local_sem):
                cp.wait()

    rr = lambda n: _rows(tm, n, rev=True, steps=steps)
    part = lambda k: pl.BlockSpec((tm, D), lambda i: (steps - 1 - i, k))
    prev8 = lambda k: pl.BlockSpec((8, D), lambda i: (jnp.maximum((steps - 1 - i) * halo - 1, 0), k))
    return pl.pallas_call(
        body, name="conv_bwd", grid=(steps,),
        in_specs=[rr(D), part(0), part(1), part(2), part(3), prev8(1), prev8(2), rr(D), _const((1, D)),
                  _const((8, D)), _const((D, D)), _const((4 * D, D)), pl.BlockSpec(memory_space=pl.ANY)],
        out_specs=[rr(D), rr(4 * D), pl.BlockSpec((8, D), lambda i: (0, 0)), pl.BlockSpec((1, D), lambda i: (0, 0)),
                   pl.BlockSpec(memory_space=pl.ANY)],
        out_shape=[jax.ShapeDtypeStruct((s, D), F32), jax.ShapeDtypeStruct((s, 4 * D), BF16),
                   jax.ShapeDtypeStruct((8, D), F32), jax.ShapeDtypeStruct((1, D), F32),
                   jax.ShapeDtypeStruct(grads2.shape, grads2.dtype)],
        scratch_shapes=[pltpu.VMEM((8, D), F32)] + list(EXCHANGE_SEMS),
        compiler_params=_params("arbitrary"),
    )(dx1, p1, p1, p1, p1, p1, p1, x, g1, cw, w1out_t, w1in_t, grads2)


def _wgrad(a, g, name):
    s, k = a.shape
    n = g.shape[1]
    bn = min(n, 1024)
    ts = min(1024, s)

    def body(a_ref, g_ref, out_ref, acc_ref):
        t = pl.program_id(1)

        @pl.when(t == 0)
        def _():
            acc_ref[...] = jnp.zeros_like(acc_ref)

        acc_ref[...] += _dot_tn(a_ref[...], g_ref[...].astype(BF16))

        @pl.when(t == s // ts - 1)
        def _():
            out_ref[...] = acc_ref[...].astype(BF16)

    return pl.pallas_call(
        body, name=name, grid=(n // bn, s // ts),
        in_specs=[pl.BlockSpec((ts, k), lambda j, t: (t, 0)), pl.BlockSpec((ts, bn), lambda j, t: (t, j))],
        out_specs=pl.BlockSpec((k, bn), lambda j, t: (0, j)),
        out_shape=jax.ShapeDtypeStruct((k, n), BF16),
        scratch_shapes=[pltpu.VMEM((k, bn), F32)],
        compiler_params=_params("arbitrary", "arbitrary"),
    )(a, g)


def _adamw_math(w, g, m, v):
    m = ADAM_B1 * m + (1.0 - ADAM_B1) * g
    v = ADAM_B2 * v + (1.0 - ADAM_B2) * (g * g)
    m_hat = m / (1.0 - ADAM_B1 ** ADAM_STEP)
    v_hat = v / (1.0 - ADAM_B2 ** ADAM_STEP)
    delta = -ADAM_LR * (m_hat / (jnp.sqrt(v_hat) + ADAM_EPS) + ADAM_WD * w)
    return delta, m, v


def _adamw_big(parts, w, m, v, name):
    _, r, c_ = parts.shape
    rb = r // ADAM_BLOCKS
    assert r % (16 * ADAM_BLOCKS) == 0

    def body(p_ref, w_ref, m_ref, v_ref, g_ref, d_ref, mo_ref, vo_ref):
        g = p_ref[0].astype(F32)
        for k in range(1, NDEV):
            g = g + p_ref[k].astype(F32)
        g_ref[...] = g
        d_ref[...], mo_ref[...], vo_ref[...] = _adamw_math(w_ref[...], g, m_ref[...], v_ref[...])

    blk = pl.BlockSpec((rb, c_), lambda i: (i, 0))
    return pl.pallas_call(
        body, name=name, grid=(r // rb,),
        in_specs=[pl.BlockSpec((NDEV, rb, c_), lambda i: (0, i, 0)), blk, blk, blk],
        out_specs=[blk, blk, blk, blk],
        out_shape=[jax.ShapeDtypeStruct((r, c_), F32)] * 4,
        compiler_params=_params("arbitrary"),
    )(parts, w, m, v)


def _sum_parts(parts, loss_row):
    _, r, c_ = parts.shape

    def body(p_ref, o_ref, l_ref):
        g = p_ref[0]
        for k in range(1, NDEV):
            g = g + p_ref[k]
        o_ref[...] = g
        per_row = jnp.sum(o_ref[loss_row:loss_row + 8, :], axis=1, keepdims=True)
        l_ref[...] = jnp.broadcast_to(jnp.sum(per_row, axis=0, keepdims=True), (8, c_))

    return pl.pallas_call(body, name="sum_small", out_shape=[jax.ShapeDtypeStruct((r, c_), F32),
                                                             jax.ShapeDtypeStruct((8, c_), F32)])(parts)


def _adamw_small(g, w, m, v):
    def body(g_ref, w_ref, m_ref, v_ref, d_ref, mo_ref, vo_ref):
        d_ref[...], mo_ref[...], vo_ref[...] = _adamw_math(w_ref[...], g_ref[...], m_ref[...], v_ref[...])

    return pl.pallas_call(body, name="adamw_small", out_shape=[jax.ShapeDtypeStruct(g.shape, F32)] * 3)(g, w, m, v)


def _pad_lanes(a):
    return jnp.pad(a, ((0, 0), (0, LANES - a.shape[1])))


def _tiles(a, t):
    return a.reshape(H, a.shape[1] // t, t, HA)


def kernel(x, conv_norm_g, conv_w_in, conv_w, conv_w_out, attn_norm_g, attn_w_in, attn_b_f, attn_q_norm_g, attn_k_norm_g, attn_w_out, loss_target, m_conv_norm_g, m_conv_w_in, m_conv_w, m_conv_w_out, m_attn_norm_g, m_attn_w_in, m_attn_b_f, m_attn_q_norm_g, m_attn_k_norm_g, m_attn_w_out, v_conv_norm_g, v_conv_w_in, v_conv_w, v_conv_w_out, v_attn_norm_g, v_attn_w_in, v_attn_b_f, v_attn_q_norm_g, v_attn_k_norm_g, v_attn_w_out):
    s = x.shape[1]
    tq = min(TQ, s)
    tmb = min(TM_BWD, s)
    me = 4 * lax.axis_index("x") + 2 * lax.axis_index("y") + lax.axis_index("c")
    xv, tgt = x[0], loss_target[0]

    def slab(t_in, t_out, pad, dtype):
        rows = [t_in[0].reshape(-1, PACK_W), t_out[0].reshape(ROWS_WOUT, PACK_W)]
        return jnp.concatenate(rows + ([jnp.zeros((pad, PACK_W), F32)] if pad else []), axis=0).astype(dtype)

    wg1 = _all_gather(slab(conv_w_in, conv_w_out, 0, BF16), "gather_weights")
    small_w = jnp.concatenate([conv_w[0], attn_norm_g, jnp.zeros((4, 128), F32)], axis=0)
    sg_ = _all_gather(small_w, "gather_small_weights")
    w1in = wg1[:, :ROWS_W1IN].transpose(1, 0, 2).reshape(D, 4 * D)
    w1out = wg1[:, ROWS_W1IN:].reshape(D, D)
    cw = jnp.concatenate([sg_[:, 0:3, :].transpose(1, 0, 2).reshape(3, D), jnp.zeros((5, D), F32)], axis=0)
    g2 = sg_[:, 3, :].reshape(1, D)
    qg_t, kg_t = jnp.tile(attn_q_norm_g, (1, H)), jnp.tile(attn_k_norm_g, (1, H))
    bf = _pad_lanes(attn_b_f)

    e = (jnp.arange(D)[:, None] // DH == jnp.arange(LANES)[None, :]).astype(BF16)
    et2 = jnp.concatenate([e.T, e.T], axis=0)
    fold = (jnp.arange(D)[:, None] % DH == jnp.arange(LANES)[None, :]).astype(BF16)
    tril = (jnp.arange(tmb)[:, None] >= jnp.arange(tmb)[None, :]).astype(BF16)
    triu = tril.T
    src = jnp.arange(3 * LANES)
    dst = jnp.arange(H * HA)[None, :] - (HA * (src % LANES) + src // LANES)[:, None]
    place_k = ((dst == COL_BIAS) & (src % LANES < H)[:, None]).astype(BF16)
    place_q = ((dst == ROW_BIAS) & (src % LANES < H)[:, None]).astype(BF16)

    x1, h1, p1, yg, wg2 = _conv_fwd(xv, conv_norm_g, w1in, cw, w1out, slab(attn_w_in, attn_w_out, PAD_ROWS2, BF16))
    w2all = wg2[:, :ROWS_W2IN].reshape(NDEV, D, 514).transpose(1, 0, 2).reshape(D, 4 * D + H)
    w2in, wf = w2all[:, :4 * D], _pad_lanes(w2all[:, 4 * D:])
    w2out = wg2[:, ROWS_W2IN:ROWS_W2IN + ROWS_WOUT].reshape(D, D)
    reach = 1.01 * math.sqrt(DH) * jnp.max(jnp.abs(attn_q_norm_g)) * jnp.max(jnp.abs(attn_k_norm_g))
    bounded = (2.0 * reach <= BOUNDED_SOFTMAX_REACH).astype(F32).reshape(1, 1)
    group = tq * min(FWD_TILES, s // tq)
    h2, qpre, kpre, z, qa, ka, va, vat, invq, invk, cc, fl = _attn_proj_fwd(
        x1, g2, w2in, wf, bf, qg_t, kg_t, jnp.broadcast_to(reach, (1, LANES)), e, et2, tril, place_k, place_q, group)
    ol, qab = _flash_fwd(bounded, _tiles(qa, tq), ka, vat)
    og, dx2, dz, doa, lossp = _attn_out(ol.reshape(H, s, HA), z, x1, tgt, w2out, w2out.T, e, place_k)

    dq, dk, dv = _flash_bwd(qab, _tiles(doa, tq), _tiles(ka, tq), _tiles(va, tq))
    dx1, dp2, df, dg2, dqg, dkg, dbf = _attn_proj_bwd(
        dq, dk, dv, dz, qpre, kpre, invq, invk, fl, x1, dx2, g2, qg_t, kg_t, w2in.T, wf.T, e, et2, triu, fold)
    dw2out = _wgrad(og, dx2, "wgrad_attn_out")
    dw2in = _wgrad(h2, dp2, "wgrad_attn_in")
    dwf = _wgrad(h2, df, "wgrad_attn_forget")
    dw2all = jnp.concatenate([dw2in, dwf[:, :H]], axis=1)
    grads2 = jnp.concatenate([
        dw2all.reshape(D, NDEV, 514).transpose(1, 0, 2).reshape(NDEV, ROWS_W2IN, PACK_W),
        dw2out.reshape(NDEV, ROWS_WOUT, PACK_W), jnp.zeros((NDEV, PAD_ROWS2, PACK_W), BF16)], axis=1)
    gx, dp1, dcw, dg1, parts2 = _conv_bwd(dx1, p1, xv, conv_norm_g, cw, w1out.T, w1in.T, grads2)
    dw1out = _wgrad(yg, dx1, "wgrad_conv_out")
    dw1in = _wgrad(h1, dp1, "wgrad_conv_in")
    grads1 = jnp.concatenate([dw1in.reshape(D, NDEV, 512).transpose(1, 0, 2),
                              dw1out.reshape(NDEV, ROWS_WOUT, PACK_W)], axis=1)
    parts1 = _exchange_blocks(grads1, "scatter_grads")

    big1 = _adamw_big(parts1, slab(conv_w_in, conv_w_out, 0, F32), slab(m_conv_w_in, m_conv_w_out, 0, F32),
                      slab(v_conv_w_in, v_conv_w_out, 0, F32), "adamw_conv")
    big2 = _adamw_big(parts2, slab(attn_w_in, attn_w_out, PAD_ROWS2, F32),
                      slab(m_attn_w_in, m_attn_w_out, PAD_ROWS2, F32),
                      slab(v_attn_w_in, v_attn_w_out, PAD_ROWS2, F32), "adamw_attn")

    shard_rows = jnp.concatenate([dcw[0:3].reshape(3, NDEV, 128).transpose(1, 0, 2), dg2.reshape(NDEV, 1, 128),
                                  jnp.zeros((NDEV, 4, 128), F32)], axis=1).reshape(64, 128)
    small_g = jnp.concatenate([shard_rows, dg1.reshape(8, 128), dbf, dqg[0:1], dkg[0:1], jnp.zeros((5, 128), F32),
                               lossp.reshape(8, 128)], axis=0)
    gs, loss8 = _sum_parts(_all_gather(small_g, "gather_small_grads"), 80)
    loss = loss8[0, 0]
    mine = lax.dynamic_slice(gs, (8 * me, 0), (8, 128))
    g_small = jnp.concatenate([mine, gs[64:75], jnp.zeros((5, 128), F32)], axis=0)

    def pack_small(cwk, ang, cng, bfk, qgk, kgk):
        return jnp.concatenate([cwk[0], ang, jnp.zeros((4, 128), F32), cng.reshape(8, 128), _pad_lanes(bfk),
                                _pad_lanes(qgk), _pad_lanes(kgk), jnp.zeros((5, 128), F32)], axis=0)

    ds_, ms_, vs_ = _adamw_small(
        g_small, pack_small(conv_w, attn_norm_g, conv_norm_g, attn_b_f, attn_q_norm_g, attn_k_norm_g),
        pack_small(m_conv_w, m_attn_norm_g, m_conv_norm_g, m_attn_b_f, m_attn_q_norm_g, m_attn_k_norm_g),
        pack_small(v_conv_w, v_attn_norm_g, v_conv_norm_g, v_attn_b_f, v_attn_q_norm_g, v_attn_k_norm_g))

    def leaves(i, small):
        b1, b2 = big1[i], big2[i]
        return (small[8:16].reshape(1, D), b1[:ROWS_W1IN].reshape(1, D, 512), small[0:3].reshape(1, 3, 128),
                b1[ROWS_W1IN:].reshape(1, 128, D), small[3:4], b2[:ROWS_W2IN].reshape(1, D, 514),
                small[16:17, :H], small[17:18, :DH], small[18:19, :DH],
                b2[ROWS_W2IN:ROWS_W2IN + ROWS_WOUT].reshape(1, 128, D))

    return (loss, gx[None], *leaves(0, g_small), *leaves(1, ds_), *leaves(2, ms_), *leaves(3, vs_))
```

```python
import functools
import math

import jax
import jax.numpy as jnp
from jax import lax
from jax.experimental import pallas as pl
from jax.experimental.pallas import tpu as pltpu

F32 = jnp.float32
BF16 = jnp.bfloat16

D = 1024
H = 16
DH = 64
NDEV = 8
RMS_EPS = 1e-6
LANES = 128
HA = 128
TM_FWD = 512
TM_BWD = 256
TQ = 512
AUG_ROWS = 80
FWD_TILES = 8
BWD_TILES = 4
CH = 256
PACK_W = 512
ROWS_W1IN, ROWS_W2IN, ROWS_WOUT = 1024, 1028, 256
ADAM_BLOCKS = 4
PAD_ROWS2 = 60
PAD_ROWS_W2 = 12
ADAM_LR, ADAM_B1, ADAM_B2, ADAM_EPS, ADAM_WD, ADAM_STEP = 0.001, 0.9, 0.999, 1e-08, 0.01, 10
VMEM_LIMIT = 56 * 1024 * 1024
MASKED = -1e30
BOUNDED_SOFTMAX_REACH = 60.0
MESH = pl.DeviceIdType.MESH


def _params(*sem, vmem=VMEM_LIMIT):
    return pltpu.CompilerParams(dimension_semantics=sem or None, vmem_limit_bytes=vmem)


def _const(shape):
    nd = len(shape)
    return pl.BlockSpec(shape, lambda *_: (0,) * nd, pipeline_mode=pl.Buffered(1))


def _rows(tm, n, rev=False, steps=None):
    if rev:
        return pl.BlockSpec((tm, n), lambda i: (steps - 1 - i, 0))
    return pl.BlockSpec((tm, n), lambda i: (i, 0))


def _dot(a, b):
    return jnp.dot(a, b, preferred_element_type=F32)


def _top16(x):
    bits = lax.bitcast_convert_type(x, jnp.uint32) & jnp.uint32(0xFFFF0000)
    return lax.bitcast_convert_type(bits, F32)


def _split2(x):
    hi = _top16(x)
    return hi.astype(BF16), (x - hi).astype(BF16)


def _split3(x):
    hi = _top16(x)
    r = x - hi
    mid = _top16(r)
    return hi.astype(BF16), mid.astype(BF16), (r - mid).astype(BF16)


def _seg_sum(a, e):
    hi, lo = _split2(a)
    return _dot(hi, e) + _dot(lo, e)


def _seg_bcast(s, et2):
    return _dot(jnp.concatenate(_split2(s), axis=1), et2)


def _tri_sum(t, v):
    hi, mid, lo = _split3(v)
    return _dot(t, hi) + _dot(t, mid) + _dot(t, lo)


def _sigmoid(z):
    return 1.0 / (1.0 + jnp.exp(-z))


def _place():
    return lax.axis_index("x"), lax.axis_index("y"), lax.axis_index("c")


def _all_gather(xb, name):
    r, c_ = xb.shape

    def body(x_ref, out_ref, send_sems, recv_sems, local_sem):
        x, y, c = _place()
        me, sibling = (x, y, c), (x, y, 1 - c)
        chips = [(1 - x, y), (x, 1 - y), (1 - x, 1 - y)]

        def slab(px, py, pc):
            return out_ref.at[4 * px + 2 * py + pc]

        def copy(k, block, to, src=None):
            return pltpu.make_async_remote_copy(
                src_ref=slab(*block) if src is None else src, dst_ref=slab(*block),
                send_sem=send_sems.at[k], recv_sem=recv_sems.at[k], device_id=to, device_id_type=MESH)

        mine = pltpu.make_async_copy(x_ref, slab(*me), local_sem)
        mine.start()
        first = [copy(0, me, sibling, src=x_ref)]
        first += [copy(1 + j, me, (*chip, c), src=x_ref) for j, chip in enumerate(chips)]
        for cp in first:
            cp.start()
        passed = [copy(4 + j, (*chip, c), sibling) for j, chip in enumerate(chips)]
        for j, chip in enumerate(chips):
            copy(1 + j, (*chip, c), me).wait_recv()
            passed[j].start()
        copy(0, sibling, me).wait_recv()
        for j, chip in enumerate(chips):
            copy(4 + j, (*chip, 1 - c), me).wait_recv()
        for cp in first + passed:
            cp.wait_send()
        mine.wait()

    return pl.pallas_call(
        body, name=name,
        out_shape=jax.ShapeDtypeStruct((NDEV, r, c_), xb.dtype),
        in_specs=[pl.BlockSpec(memory_space=pl.ANY)],
        out_specs=pl.BlockSpec(memory_space=pl.ANY),
        scratch_shapes=[pltpu.SemaphoreType.DMA((7,)), pltpu.SemaphoreType.DMA((7,)), pltpu.SemaphoreType.DMA],
    )(xb)


def _block_copies(g_ref, out_ref, send_sems, recv_sems, local_sem):
    x, y, c = _place()
    copies = [pltpu.make_async_copy(g_ref.at[4 * x + 2 * y + c], out_ref.at[0], local_sem)]
    for k in range(1, NDEV):
        px = 1 - x if k & 4 else x
        py = 1 - y if k & 2 else y
        pc = 1 - c if k & 1 else c
        copies.append(pltpu.make_async_remote_copy(
            src_ref=g_ref.at[4 * px + 2 * py + pc], dst_ref=out_ref.at[k],
            send_sem=send_sems.at[k - 1], recv_sem=recv_sems.at[k - 1],
            device_id=(px, py, pc), device_id_type=MESH))
    return copies


def _gather_copies(x_ref, out_ref, send_sems, recv_sems, local_sem):
    x, y, c = _place()
    mine = out_ref.at[4 * x + 2 * y + c]
    copies = [pltpu.make_async_copy(x_ref, mine, local_sem)]
    for k in range(1, NDEV):
        peer = (1 - x if k & 4 else x, 1 - y if k & 2 else y, 1 - c if k & 1 else c)
        copies.append(pltpu.make_async_remote_copy(
            src_ref=x_ref, dst_ref=mine, send_sem=send_sems.at[k - 1], recv_sem=recv_sems.at[k - 1],
            device_id=peer, device_id_type=MESH))
    return copies


EXCHANGE_SEMS = [pltpu.SemaphoreType.DMA((7,)), pltpu.SemaphoreType.DMA((7,)), pltpu.SemaphoreType.DMA]


def _exchange_blocks(g, name):
    _, r, c_ = g.shape

    def body(g_ref, out_ref, send_sems, recv_sems, local_sem):
        copies = _block_copies(g_ref, out_ref, send_sems, recv_sems, local_sem)
        for cp in copies:
            cp.start()
        for cp in copies:
            cp.wait()

    return pl.pallas_call(
        body, name=name,
        out_shape=jax.ShapeDtypeStruct((NDEV, r, c_), g.dtype),
        in_specs=[pl.BlockSpec(memory_space=pl.ANY)],
        out_specs=pl.BlockSpec(memory_space=pl.ANY),
        scratch_shapes=list(EXCHANGE_SEMS),
    )(g)


def _conv_fwd(x, g1, w1in, cw, w1out, shard2):
    s = x.shape[0]
    tm = min(TM_FWD, s)
    steps = s // tm

    def body(x_ref, g_ref, win_ref, cw_ref, wout_ref, sh_ref, x1_ref, h1_ref, p1_ref, yg_ref, all_ref, tail_ref,
             send_sems, recv_sems, local_sem):
        @pl.when(pl.program_id(0) == 0)
        def _():
            tail_ref[...] = jnp.zeros_like(tail_ref)
            for cp in _gather_copies(sh_ref, all_ref, send_sems, recv_sems, local_sem):
                cp.start()

        xv = x_ref[...]
        inv = lax.rsqrt(jnp.mean(xv * xv, axis=-1, keepdims=True) + RMS_EPS)
        h = (xv * inv * g_ref[...]).astype(BF16)
        h1_ref[...] = h
        row = lax.broadcasted_iota(jnp.int32, (tm, CH), 0)
        for ci in range(D // CH):
            lo, hi = ci * CH, (ci + 1) * CH
            parts = []
            for k in range(4):
                pk = _dot(h, win_ref[:, k * D + lo:k * D + hi]).astype(BF16)
                p1_ref[:, k * D + lo:k * D + hi] = pk
                parts.append(pk.astype(F32))
            b, c, xin, z = parts
            u = c * xin
            t6 = tail_ref[6:7, lo:hi]
            t7 = tail_ref[7:8, lo:hi]
            u1 = jnp.where(row == 0, t7, pltpu.roll(u, 1, 0))
            u2 = jnp.where(row == 0, t6, jnp.where(row == 1, t7, pltpu.roll(u, 2, 0)))
            tail_ref[:, lo:hi] = u[tm - 8:, :]
            y = cw_ref[2:3, lo:hi] * u + cw_ref[1:2, lo:hi] * u1 + cw_ref[0:1, lo:hi] * u2
            yg_ref[:, lo:hi] = (b * y * (z * _sigmoid(z))).astype(BF16)
        x1_ref[...] = xv + _dot(yg_ref[...], wout_ref[...])

        @pl.when(pl.program_id(0) == steps - 1)
        def _():
            for cp in _gather_copies(sh_ref, all_ref, send_sems, recv_sems, local_sem):
                cp.wait()

    return pl.pallas_call(
        body, name="conv_fwd", grid=(steps,),
        in_specs=[_rows(tm, D), _const((1, D)), _const((D, 4 * D)), _const((8, D)), _const((D, D)),
                  pl.BlockSpec(memory_space=pl.ANY)],
        out_specs=[_rows(tm, D), _rows(tm, D), _rows(tm, 4 * D), _rows(tm, D), pl.BlockSpec(memory_space=pl.ANY)],
        out_shape=[jax.ShapeDtypeStruct((s, D), F32), jax.ShapeDtypeStruct((s, D), BF16),
                   jax.ShapeDtypeStruct((s, 4 * D), BF16), jax.ShapeDtypeStruct((s, D), BF16),
                   jax.ShapeDtypeStruct((NDEV,) + shard2.shape, shard2.dtype)],
        scratch_shapes=[pltpu.VMEM((8, D), F32)] + list(EXCHANGE_SEMS),
        compiler_params=_params("arbitrary"),
    )(x, g1, w1in, cw, w1out, shard2)


COL_BIAS = DH
ROW_BIAS = DH + 3


def _terms_cat(val):
    return jnp.concatenate(_split3(val), axis=1)


def _split_heads(x, aug, out_ref, lane, t_ref=None):
    for hp in range(H // 2):
        pair = x[:, hp * HA:(hp + 1) * HA]
        for k, feat in enumerate((pair, pltpu.roll(pair, DH, 1))):
            block = jnp.where(lane < DH, feat, aug(2 * hp + k))
            out_ref[2 * hp + k] = block.astype(BF16)
            if t_ref is not None:
                t_ref[2 * hp + k] = block.T[0:t_ref.shape[1], :].astype(BF16)


def _ones_at(lane, first):
    return jnp.where((lane >= first) & (lane < first + 3), 1.0, 0.0)


def _attn_proj_fwd(x1, g2, w2in, wf, bf, qg, kg, shift, e, et, tril, place_k, place_q, group):
    s = x1.shape[0]
    tm = min(TM_BWD, s)
    steps = s // tm
    per = group // tm

    def body(x_ref, g_ref, win_ref, wf_ref, bf_ref, qg_ref, kg_ref, sh_ref, e_ref, et_ref, tril_ref, pk_ref, pq_ref,
             h2_ref, qp_ref, kp_ref, z_ref, qa_ref, ka_ref, va_ref, vat_ref, iq_ref, ik_ref, c_ref, fl_ref, carry_ref):
        @pl.when(pl.program_id(0) == 0)
        def _():
            carry_ref[...] = jnp.zeros_like(carry_ref)

        xv = x_ref[...]
        inv = lax.rsqrt(jnp.mean(xv * xv, axis=-1, keepdims=True) + RMS_EPS)
        h = (xv * inv * g_ref[...]).astype(BF16)
        h2_ref[...] = h

        fl = _dot(h, wf_ref[...]) + bf_ref[...]
        fl_ref[...] = fl
        ex = jnp.exp(-jnp.abs(fl))
        up = 1.0 + ex
        log1p = jnp.where(up == 1.0, ex, jnp.log(up) * ex / (up - 1.0))
        lane = lax.broadcasted_iota(jnp.int32, (tm, LANES), 1)
        logf = jnp.where(lane < H, jnp.minimum(fl, 0.0) - log1p, 0.0)
        c_ref[...] = _tri_sum(tril_ref[...], logf) + carry_ref[0:1, :]
        carry_ref[0:1, :] = c_ref[tm - 1:tm, :]
        c = c_ref[...]

        def normed(col, pre_ref, inv_ref, gain_ref, scale):
            pre = _dot(h, win_ref[:, col * D:(col + 1) * D]).astype(BF16)
            pre_ref[...] = pre
            t = pre.astype(F32)
            invh = lax.rsqrt(_seg_sum(t * t, e_ref[...]) * (1.0 / DH) + RMS_EPS)
            inv_ref[...] = invh
            return t * _seg_bcast(invh, et_ref[...]) * (gain_ref[...] * scale)

        ones_col, ones_row = _ones_at(lane, COL_BIAS), _ones_at(lane, ROW_BIAS)
        q_bias = _dot(_terms_cat(c - sh_ref[...]), pq_ref[...])
        _split_heads(normed(0, qp_ref, iq_ref, qg_ref, 1.0 / math.sqrt(DH)),
                     lambda hh: q_bias[:, hh * HA:(hh + 1) * HA] + ones_col, qa_ref, lane)
        k_bias = _dot(_terms_cat(-c), pk_ref[...])
        _split_heads(normed(1, kp_ref, ik_ref, kg_ref, 1.0),
                     lambda hh: k_bias[:, hh * HA:(hh + 1) * HA] + ones_row, ka_ref, lane)
        v = _dot(h, win_ref[:, 2 * D:3 * D])
        _split_heads(v, lambda hh: ones_col, va_ref, lane, vat_ref)
        z_ref[...] = _dot(h, win_ref[:, 3 * D:4 * D]).astype(BF16)

    row_bf = lambda: _rows(tm, D)
    row_sm = lambda: _rows(tm, LANES)
    heads = lambda: pl.BlockSpec((H, tm, HA), lambda i: (0, i, 0))
    return pl.pallas_call(
        body, name="attn_proj_fwd", grid=(steps,),
        in_specs=[_rows(tm, D), _const((1, D)), _const((D, 4 * D)), _const((D, LANES)), _const((1, LANES)),
                  _const((1, D)), _const((1, D)), _const((1, LANES)), _const((D, LANES)), _const((2 * LANES, D)),
                  _const((tm, tm)), _const((3 * LANES, H * HA)), _const((3 * LANES, H * HA))],
        out_specs=[row_bf() for _ in range(4)] + [heads() for _ in range(3)] + [
            pl.BlockSpec((H, None, AUG_ROWS, tm), lambda i: (0, i // per, 0, i % per))] + [row_sm() for _ in range(4)],
        out_shape=[jax.ShapeDtypeStruct((s, D), BF16)] * 4 + [jax.ShapeDtypeStruct((H, s, HA), BF16)] * 3 + [
            jax.ShapeDtypeStruct((H, s // group, AUG_ROWS, group), BF16)] + [jax.ShapeDtypeStruct((s, LANES), F32)] * 4,
        scratch_shapes=[pltpu.VMEM((8, LANES), F32)],
        compiler_params=_params("arbitrary"),
    )(x1, g2, w2in, wf, bf, qg, kg, shift, e, et, tril, place_k, place_q)


def _dot_nt(a, b):
    return lax.dot_general(a, b, (((1,), (1,)), ((), ())), preferred_element_type=F32)


def _dot_tn(a, b):
    return lax.dot_general(a, b, (((0,), (0,)), ((), ())), preferred_element_type=F32)


def _flash_fwd(bounded, qa, ka, vat):
    h_, nq, tq, _ = qa.shape
    s, tk = ka.shape[1], tq
    group = vat.shape[3]
    gt = group // tk
    shift = gt.bit_length() - 1

    def body(flag_ref, q_ref, k_ref, vt_ref, o_ref, qb_ref, m_ref, acct_ref):
        i = pl.program_id(1)
        lane = lax.broadcasted_iota(jnp.int32, (tq, HA), 1)
        in_bias = (lane >= ROW_BIAS) & (lane < ROW_BIAS + 3)
        acct_ref[...] = jnp.zeros_like(acct_ref)
        g = lax.shift_right_logical(i, shift)
        r = i & (gt - 1)

        def run(use_bound):
            q = q_ref[...]
            if not use_bound:
                q = jnp.where(in_bias, jnp.zeros_like(q), q)
                m_ref[...] = jnp.full_like(m_ref, MASKED)

            def step(first_key, vt, tiles, diagonal_at):
                keys = pl.ds(pl.multiple_of(first_key, tk), tiles * tk)
                zt = _dot_nt(k_ref[keys, :], q)
                if diagonal_at is not None:
                    key = lax.broadcasted_iota(jnp.int32, (tiles * tk, tq), 0)
                    qry = lax.broadcasted_iota(jnp.int32, (tiles * tk, tq), 1)
                    zt = jnp.where(key <= qry + diagonal_at * tk, zt, MASKED)
                if use_bound:
                    acct_ref[0:AUG_ROWS, :] += _dot(vt, jnp.exp(zt).astype(BF16))
                else:
                    m_old = m_ref[...]
                    m_new = jnp.maximum(m_old, jnp.max(zt, axis=0, keepdims=True))
                    pt = jnp.exp(zt - m_new)
                    acct_ref[0:AUG_ROWS, :] = jnp.exp(m_old - m_new) * acct_ref[0:AUG_ROWS, :] + _dot(
                        vt, pt.astype(BF16))
                    m_ref[...] = m_new

            def whole_group(jj, carry):
                step(jj * group, vt_ref[jj], gt, None)
                return carry

            lax.fori_loop(0, g, whole_group, 0)
            for rr in range(gt):
                pl.when(r == rr)(functools.partial(
                    lambda rr: step(g * group, vt_ref[g, :, 0:(rr + 1) * tk], rr + 1, rr), rr))

            if not use_bound:
                acct_ref[COL_BIAS + 1:COL_BIAS + 2, :] = m_ref[...]
            acc = acct_ref[...].T
            l = jnp.sum(jnp.where(lane == COL_BIAS, acc, 0.0), axis=1, keepdims=True)
            if use_bound:
                m = -jnp.sum(jnp.where(in_bias, q.astype(F32), 0.0), axis=1, keepdims=True)
            else:
                m = jnp.sum(jnp.where(lane == COL_BIAS + 1, acc, 0.0), axis=1, keepdims=True)
            lse = m + jnp.log(l)
            o_ref[...] = jnp.where(lane < DH, acc / l, lse)
            hi, mid, lo = _split3(-lse)
            qb_ref[...] = jnp.where(lane == ROW_BIAS, hi, jnp.where(lane == ROW_BIAS + 1, mid, jnp.where(
                lane == ROW_BIAS + 2, lo, q_ref[...])))

        use_bound = flag_ref[0, 0] > 0.5
        pl.when(use_bound)(lambda: run(True))
        pl.when(jnp.logical_not(use_bound))(lambda: run(False))

    tile = lambda: pl.BlockSpec((None, None, tq, HA), lambda h, i: (h, i, 0, 0))
    return pl.pallas_call(
        body, name="flash_fwd", grid=(h_, nq),
        in_specs=[pl.BlockSpec(memory_space=pltpu.SMEM), tile(), pl.BlockSpec((None, s, HA), lambda h, i: (h, 0, 0)),
                  pl.BlockSpec((None, s // group, AUG_ROWS, group), lambda h, i: (h, 0, 0, 0))],
        out_specs=[tile(), tile()],
        out_shape=[jax.ShapeDtypeStruct((h_, nq, tq, HA), F32), jax.ShapeDtypeStruct((h_, nq, tq, HA), BF16)],
        scratch_shapes=[pltpu.VMEM((1, tq), F32), pltpu.VMEM((HA, tq), F32)],
        compiler_params=_params("arbitrary", "arbitrary"),
    )(bounded, qa, ka, vat)


def _merge_heads(src, out_ref, lane):
    for hp in range(H // 2):
        out_ref[:, hp * HA:(hp + 1) * HA] = jnp.where(lane < DH, src(2 * hp), pltpu.roll(src(2 * hp + 1), DH, 1))


def _attn_out(ol, z, x1, tgt, w2out, w2out_t, e, place_k):
    s = z.shape[0]
    tm = min(TM_BWD, s)
    steps = s // tm

    def body(ol_ref, z_ref, x1_ref, t_ref, w_ref, wt_ref, e_ref, pk_ref, og_ref, dx2_ref, dz_ref, doa_ref, loss_ref,
             o_ref):
        @pl.when(pl.program_id(0) == 0)
        def _():
            loss_ref[...] = jnp.zeros_like(loss_ref)

        lane = lax.broadcasted_iota(jnp.int32, (tm, HA), 1)
        _merge_heads(lambda hh: ol_ref[hh], o_ref, lane)
        ov = o_ref[...]
        zv = z_ref[...].astype(F32)
        sg = _sigmoid(zv)
        sil = zv * sg
        og = (ov * sil).astype(BF16)
        og_ref[...] = og
        err = x1_ref[...] + _dot(og, w_ref[...]) - t_ref[...]
        loss_ref[...] += (0.5 / D) * jnp.sum(err * err, axis=0, keepdims=True)
        dx2 = err * (1.0 / D)
        dx2_ref[...] = dx2
        dog = _dot(dx2.astype(BF16), wt_ref[...])
        do = (dog * sil).astype(BF16).astype(F32)
        dz_ref[...] = (dog * ov * (sg * (1.0 + zv * (1.0 - sg)))).astype(BF16)
        delta = _seg_sum(do * ov, e_ref[...])
        d_bias = _dot(_terms_cat(-delta), pk_ref[...])
        _split_heads(do, lambda hh: d_bias[:, hh * HA:(hh + 1) * HA], doa_ref, lane)

    heads = lambda: pl.BlockSpec((H, tm, HA), lambda i: (0, i, 0))
    return pl.pallas_call(
        body, name="attn_out", grid=(steps,),
        in_specs=[heads(), _rows(tm, D), _rows(tm, D), _rows(tm, D), _const((D, D)), _const((D, D)),
                  _const((D, LANES)), _const((3 * LANES, H * HA))],
        out_specs=[_rows(tm, D), _rows(tm, D), _rows(tm, D), heads(), pl.BlockSpec((1, D), lambda i: (0, 0))],
        out_shape=[jax.ShapeDtypeStruct((s, D), BF16), jax.ShapeDtypeStruct((s, D), F32),
                   jax.ShapeDtypeStruct((s, D), BF16), jax.ShapeDtypeStruct((H, s, HA), BF16),
                   jax.ShapeDtypeStruct((1, D), F32)],
        scratch_shapes=[pltpu.VMEM((tm, D), F32)],
        compiler_params=_params("arbitrary"),
    )(ol, z, x1, tgt, w2out, w2out_t, e, place_k)


def _flash_bwd(qab, doa, ka, va):
    h_, nq, tq, _ = qab.shape
    nk, tk = ka.shape[1], ka.shape[2]
    s = nk * tk
    gt = min(BWD_TILES, nq)
    shift = gt.bit_length() - 1

    def body(qa_ref, da_ref, ka_ref, va_ref, dq_hbm, dk_ref, dv_ref, dq_acc, dkt_ref, dvt_ref, sem):
        hh = pl.program_id(0)
        j = pl.program_id(1)

        @pl.when(j == 0)
        def _():
            dq_acc[...] = jnp.zeros_like(dq_acc)

        dkt_ref[...] = jnp.zeros_like(dkt_ref)
        dvt_ref[...] = jnp.zeros_like(dvt_ref)
        ka_v = ka_ref[...]
        va_v = va_ref[...]

        def step(i, tiles, masked):
            rows = pl.ds(pl.multiple_of(i * tq, tq), tiles * tq)
            qa_i, da_i = qa_ref[rows, :], da_ref[rows, :]
            zz = _dot_nt(qa_i, ka_v)
            if masked:
                row = lax.broadcasted_iota(jnp.int32, (tiles * tq, tk), 0)
                col = lax.broadcasted_iota(jnp.int32, (tiles * tq, tk), 1)
                zz = jnp.where(col <= row, zz, MASKED)
            p = jnp.exp(zz)
            ds = (p * _dot_nt(da_i, va_v)).astype(BF16)
            pb = p.astype(BF16)
            dq_acc[rows, :] += _dot(ds, ka_v)
            dvt_ref[...] += _dot_tn(da_i, pb)
            dkt_ref[...] += _dot_tn(qa_i, ds)

        g0 = lax.shift_right_logical(j, shift)
        for rr in range(gt):
            pl.when((j & (gt - 1)) == rr)(functools.partial(step, j, gt - rr, True))

        def whole_group(g, carry):
            step(g * gt, gt, False)
            return carry

        lax.fori_loop(g0 + 1, nq // gt, whole_group, 0)
        dk_ref[...] = dkt_ref[...].T
        dv_ref[...] = dvt_ref[...].T.astype(BF16)

        @pl.when(j == nk - 1)
        def _():
            cp = pltpu.make_async_copy(dq_acc, dq_hbm.at[hh], sem)
            cp.start()
            cp.wait()

    whole = lambda: pl.BlockSpec((None, nq * tq, HA), lambda h, j: (h, 0, 0))
    tile = lambda: pl.BlockSpec((None, None, tk, HA), lambda h, j: (h, j, 0, 0))
    rows = lambda: pl.BlockSpec((None, tk, HA), lambda h, j: (h, j, 0))
    return pl.pallas_call(
        body, name="flash_bwd", grid=(h_, nk),
        in_specs=[whole(), whole(), tile(), tile()],
        out_specs=[pl.BlockSpec(memory_space=pl.ANY), rows(), rows()],
        out_shape=[jax.ShapeDtypeStruct((h_, s, HA), F32)] * 2 + [jax.ShapeDtypeStruct((h_, s, HA), BF16)],
        scratch_shapes=[pltpu.VMEM((s, HA), F32), pltpu.VMEM((HA, tk), F32), pltpu.VMEM((HA, tk), F32),
                        pltpu.SemaphoreType.DMA],
        compiler_params=_params("arbitrary", "arbitrary"),
    )(qab.reshape(h_, s, HA), doa.reshape(h_, s, HA), ka, va)


def _attn_proj_bwd(dq, dk, dv, dz, qpre, kpre, invq, invk, fl, x1, dx2, g2, qg, kg, w2in_t, wf_t, e, et,
                   triu, fold):
    s = x1.shape[0]
    tm = min(TM_BWD, s)
    steps = s // tm

    def body(dq_ref, dk_ref, dvh_ref, dz_ref, qp_ref, kp_ref, iq_ref, ik_ref, fl_ref, x1_ref, dx2_ref,
             g2_ref, qg_ref, kg_ref, wt_ref, wft_ref, e_ref, et_ref, triu_ref, fold_ref,
             dx1_ref, dp_ref, df_ref, dg2_ref, dqg_ref, dkg_ref, dbf_ref, carry_ref, rc_ref, qcol_ref, kcol_ref,
             dqs_ref, dkn_ref, dv_ref):
        step = pl.program_id(0)
        lane = lax.broadcasted_iota(jnp.int32, (tm, HA), 1)
        _merge_heads(lambda hh: dvh_ref[hh].astype(F32), dv_ref, lane)
        dc = jnp.zeros((tm, HA), F32)
        for hp in range(H // 2):
            dqh = [dq_ref[2 * hp], dq_ref[2 * hp + 1]]
            dkh = [dk_ref[2 * hp], dk_ref[2 * hp + 1]]
            dqs_ref[:, hp * HA:(hp + 1) * HA] = jnp.where(lane < DH, dqh[0], pltpu.roll(dqh[1], DH, 1))
            dkn_ref[:, hp * HA:(hp + 1) * HA] = jnp.where(lane < DH, dkh[0], pltpu.roll(dkh[1], DH, 1))
            for k in range(2):
                sums = jnp.where(lane == ROW_BIAS, dqh[k], 0.0) - pltpu.roll(
                    jnp.where(lane == COL_BIAS, dkh[k], 0.0), ROW_BIAS - COL_BIAS, 1)
                dc = dc + pltpu.roll(sums, (2 * hp + k - ROW_BIAS) % HA, 1)

        @pl.when(step == 0)
        def _():
            carry_ref[...] = jnp.zeros_like(carry_ref)
            dg2_ref[...] = jnp.zeros_like(dg2_ref)
            dbf_ref[...] = jnp.zeros_like(dbf_ref)
            qcol_ref[...] = jnp.zeros_like(qcol_ref)
            kcol_ref[...] = jnp.zeros_like(kcol_ref)
            dqg_ref[...] = jnp.zeros_like(dqg_ref)
            dkg_ref[...] = jnp.zeros_like(dkg_ref)

        rc_ref[...] = _tri_sum(triu_ref[...], dc) + carry_ref[0:1, :]
        carry_ref[0:1, :] = rc_ref[0:1, :]
        df = rc_ref[...] * _sigmoid(-fl_ref[...])
        dfb = df.astype(BF16)
        df_ref[...] = dfb
        dbf_ref[...] += jnp.sum(df, axis=0, keepdims=True)

        def norm_bwd(dn, pre_ref, inv_ref, gain_ref, col_ref):
            pre = pre_ref[...].astype(F32)
            invh = inv_ref[...]
            invb = _seg_bcast(invh, et_ref[...])
            col_ref[...] += jnp.sum(dn * pre * invb, axis=0, keepdims=True)
            gd = dn * gain_ref[...]
            mean = _seg_sum(gd * pre, e_ref[...]) * (1.0 / DH)
            return invb * gd - pre * _seg_bcast(mean * invh * invh * invh, et_ref[...])

        dq = norm_bwd(dqs_ref[...] * (1.0 / math.sqrt(DH)), qp_ref, iq_ref, qg_ref, qcol_ref).astype(BF16)
        dp_ref[:, 0:D] = dq
        dh = _dot(dq, wt_ref[0:D, :])
        dk = norm_bwd(dkn_ref[...], kp_ref, ik_ref, kg_ref, kcol_ref).astype(BF16)
        dp_ref[:, D:2 * D] = dk
        dh += _dot(dk, wt_ref[D:2 * D, :])
        dvb = dv_ref[...].astype(BF16)
        dp_ref[:, 2 * D:3 * D] = dvb
        dh += _dot(dvb, wt_ref[2 * D:3 * D, :])
        dzb = dz_ref[...]
        dp_ref[:, 3 * D:4 * D] = dzb
        dh += _dot(dzb, wt_ref[3 * D:4 * D, :])
        dh += _dot(dfb, wft_ref[...])

        xv = x1_ref[...]
        inv = lax.rsqrt(jnp.mean(xv * xv, axis=-1, keepdims=True) + RMS_EPS)
        dg2_ref[...] += jnp.sum(dh * xv * inv, axis=0, keepdims=True)
        gh = dh * g2_ref[...]
        dx1_ref[...] = dx2_ref[...] + inv * gh - xv * (inv * inv * inv * jnp.mean(gh * xv, axis=-1, keepdims=True))

        @pl.when(step == steps - 1)
        def _():
            dqg_ref[...] = _fold_heads(qcol_ref[...], fold_ref[...])
            dkg_ref[...] = _fold_heads(kcol_ref[...], fold_ref[...])

    rr = lambda n: _rows(tm, n, rev=True, steps=steps)
    acc = lambda n, r=1: pl.BlockSpec((r, n), lambda i: (0, 0))
    heads = lambda: pl.BlockSpec((H, tm, HA), lambda i: (0, steps - 1 - i, 0))
    return pl.pallas_call(
        body, name="attn_proj_bwd", grid=(steps,),
        in_specs=[heads(), heads(), heads(),
                  rr(D), rr(D), rr(D), rr(LANES), rr(LANES), rr(LANES), rr(D), rr(D),
                  _const((1, D)), _const((1, D)), _const((1, D)), _const((4 * D, D)), _const((LANES, D)),
                  _const((D, LANES)), _const((2 * LANES, D)), _const((tm, tm)), _const((D, LANES))],
        out_specs=[rr(D), rr(4 * D), rr(LANES), acc(D), acc(LANES, 8), acc(LANES, 8), acc(LANES)],
        out_shape=[jax.ShapeDtypeStruct((s, D), F32), jax.ShapeDtypeStruct((s, 4 * D), BF16),
                   jax.ShapeDtypeStruct((s, LANES), BF16), jax.ShapeDtypeStruct((1, D), F32),
                   jax.ShapeDtypeStruct((8, LANES), F32), jax.ShapeDtypeStruct((8, LANES), F32),
                   jax.ShapeDtypeStruct((1, LANES), F32)],
        scratch_shapes=[pltpu.VMEM((8, LANES), F32), pltpu.VMEM((tm, LANES), F32), pltpu.VMEM((1, D), F32),
                        pltpu.VMEM((1, D), F32), pltpu.VMEM((tm, D), F32), pltpu.VMEM((tm, D), F32),
                        pltpu.VMEM((tm, D), F32)],
        compiler_params=_params("arbitrary"),
    )(dq, dk, dv, dz, qpre, kpre, invq, invk, fl, x1, dx2, g2, qg, kg, w2in_t, wf_t, e, et, triu, fold)


def _fold_heads(col, fold):
    hi, mid, lo = _split3(jnp.broadcast_to(col, (8, D)))
    return _dot(hi, fold) + _dot(mid, fold) + _dot(lo, fold)


def _conv_bwd(dx1, p1, x, g1, cw, w1out_t, w1in_t, grads2):
    s = x.shape[0]
    tm = min(TM_BWD, s)
    steps = s // tm
    halo = tm // 8

    def body(dx1_ref, b_ref, c_ref, xi_ref, z_ref, ch_ref, xh_ref, x_ref, g_ref, cw_ref, wot_ref, wit_ref, g2_ref,
             gx_ref, dp_ref, dcw_ref, dg1_ref, parts_ref, head_ref, send_sems, recv_sems, local_sem):
        step = pl.program_id(0)

        @pl.when(step == 0)
        def _():
            for cp in _block_copies(g2_ref, parts_ref, send_sems, recv_sems, local_sem):
                cp.start()

        @pl.when(step == 0)
        def _():
            head_ref[...] = jnp.zeros_like(head_ref)
            dcw_ref[...] = jnp.zeros_like(dcw_ref)
            dg1_ref[...] = jnp.zeros_like(dg1_ref)

        first_tile = step == steps - 1
        dx1 = dx1_ref[...]
        row = lax.broadcasted_iota(jnp.int32, (tm, CH), 0)
        dyb = dx1.astype(BF16)
        for ci in range(D // CH):
            lo, hi = ci * CH, (ci + 1) * CH
            dyg = _dot(dyb, wot_ref[:, lo:hi])
            b = b_ref[:, lo:hi].astype(F32)
            c = c_ref[:, lo:hi].astype(F32)
            xin = xi_ref[:, lo:hi].astype(F32)
            z = z_ref[:, lo:hi].astype(F32)
            u = c * xin
            t6 = jnp.where(first_tile, 0.0, ch_ref[6:7, lo:hi].astype(F32) * xh_ref[6:7, lo:hi].astype(F32))
            t7 = jnp.where(first_tile, 0.0, ch_ref[7:8, lo:hi].astype(F32) * xh_ref[7:8, lo:hi].astype(F32))
            u1 = jnp.where(row == 0, t7, pltpu.roll(u, 1, 0))
            u2 = jnp.where(row == 0, t6, jnp.where(row == 1, t7, pltpu.roll(u, 2, 0)))
            w0, w1, w2 = cw_ref[0:1, lo:hi], cw_ref[1:2, lo:hi], cw_ref[2:3, lo:hi]
            y = w2 * u + w1 * u1 + w0 * u2
            sg = _sigmoid(z)
            sil = z * sg
            dp_ref[:, lo:hi] = (dyg * y * sil).astype(BF16)
            dy = dyg * b * sil
            dp_ref[:, 3 * D + lo:3 * D + hi] = (dyg * b * y * (sg * (1.0 + z * (1.0 - sg)))).astype(BF16)
            dcw_ref[2:3, lo:hi] += jnp.sum(dy * u, axis=0, keepdims=True)
            dcw_ref[1:2, lo:hi] += jnp.sum(dy * u1, axis=0, keepdims=True)
            dcw_ref[0:1, lo:hi] += jnp.sum(dy * u2, axis=0, keepdims=True)
            n0 = head_ref[0:1, lo:hi]
            n1 = head_ref[1:2, lo:hi]
            dyn1 = jnp.where(row == tm - 1, n0, pltpu.roll(dy, tm - 1, 0))
            dyn2 = jnp.where(row == tm - 2, n0, jnp.where(row == tm - 1, n1, pltpu.roll(dy, tm - 2, 0)))
            head_ref[:, lo:hi] = dy[0:8, :]
            du = w2 * dy + w1 * dyn1 + w0 * dyn2
            dp_ref[:, D + lo:D + hi] = (du * xin).astype(BF16)
            dp_ref[:, 2 * D + lo:2 * D + hi] = (du * c).astype(BF16)
        dh = _dot(dp_ref[:, 0:D], wit_ref[0:D, :])
        for k in range(1, 4):
            dh += _dot(dp_ref[:, k * D:(k + 1) * D], wit_ref[k * D:(k + 1) * D, :])
        xv = x_ref[...]
        inv = lax.rsqrt(jnp.mean(xv * xv, axis=-1, keepdims=True) + RMS_EPS)
        dg1_ref[...] += jnp.sum(dh * xv * inv, axis=0, keepdims=True)
        gh = dh * g_ref[...]
        gx_ref[...] = dx1 + inv * gh - xv * (inv * inv * inv * jnp.mean(gh * xv, axis=-1, keepdims=True))

        @pl.when(step == steps - 1)
        def _():
            for cp in _block_copies(g2_ref, parts_ref, send_sems, recv_sems, local_sem):
                cp.wait()

    rr = lambda n: _rows(tm, n, rev=True, steps=steps)
    part = lambda k: pl.BlockSpec((tm, D), lambda i: (steps - 1 - i, k))
    prev8 = lambda k: pl.BlockSpec((8, D), lambda i: (jnp.maximum((steps - 1 - i) * halo - 1, 0), k))
    return pl.pallas_call(
        body, name="conv_bwd", grid=(steps,),
        in_specs=[rr(D), part(0), part(1), part(2), part(3), prev8(1), prev8(2), rr(D), _const((1, D)),
                  _const((8, D)), _const((D, D)), _const((4 * D, D)), pl.BlockSpec(memory_space=pl.ANY)],
        out_specs=[rr(D), rr(4 * D), pl.BlockSpec((8, D), lambda i: (0, 0)), pl.BlockSpec((1, D), lambda i: (0, 0)),
                   pl.BlockSpec(memory_space=pl.ANY)],
        out_shape=[jax.ShapeDtypeStruct((s, D), F32), jax.ShapeDtypeStruct((s, 4 * D), BF16),
                   jax.ShapeDtypeStruct((8, D), F32), jax.ShapeDtypeStruct((1, D), F32),
                   jax.ShapeDtypeStruct(grads2.shape, grads2.dtype)],
        scratch_shapes=[pltpu.VMEM((8, D), F32)] + list(EXCHANGE_SEMS),
        compiler_params=_params("arbitrary"),
    )(dx1, p1, p1, p1, p1, p1, p1, x, g1, cw, w1out_t, w1in_t, grads2)


def _wgrad(a, g, name):
    s, k = a.shape
    n = g.shape[1]
    bn = min(n, 1024)
    ts = min(1024, s)

    def body(a_ref, g_ref, out_ref, acc_ref):
        t = pl.program_id(1)

        @pl.when(t == 0)
        def _():
            acc_ref[...] = jnp.zeros_like(acc_ref)

        acc_ref[...] += _dot_tn(a_ref[...], g_ref[...].astype(BF16))

        @pl.when(t == s // ts - 1)
        def _():
            out_ref[...] = acc_ref[...].astype(BF16)

    return pl.pallas_call(
        body, name=name, grid=(n // bn, s // ts),
        in_specs=[pl.BlockSpec((ts, k), lambda j, t: (t, 0)), pl.BlockSpec((ts, bn), lambda j, t: (t, j))],
        out_specs=pl.BlockSpec((k, bn), lambda j, t: (0, j)),
        out_shape=jax.ShapeDtypeStruct((k, n), BF16),
        scratch_shapes=[pltpu.VMEM((k, bn), F32)],
        compiler_params=_params("arbitrary", "arbitrary"),
    )(a, g)


def _adamw_math(w, g, m, v):
    m = ADAM_B1 * m + (1.0 - ADAM_B1) * g
    v = ADAM_B2 * v + (1.0 - ADAM_B2) * (g * g)
    m_hat = m / (1.0 - ADAM_B1 ** ADAM_STEP)
    v_hat = v / (1.0 - ADAM_B2 ** ADAM_STEP)
    delta = -ADAM_LR * (m_hat / (jnp.sqrt(v_hat) + ADAM_EPS) + ADAM_WD * w)
    return delta, m, v


def _adamw_big(parts, w, m, v, name):
    _, r, c_ = parts.shape
    rb = r // ADAM_BLOCKS
    assert r % (16 * ADAM_BLOCKS) == 0

    def body(p_ref, w_ref, m_ref, v_ref, g_ref, d_ref, mo_ref, vo_ref):
        g = p_ref[0].astype(F32)
        for k in range(1, NDEV):
            g = g + p_ref[k].astype(F32)
        g_ref[...] = g
        d_ref[...], mo_ref[...], vo_ref[...] = _adamw_math(w_ref[...], g, m_ref[...], v_ref[...])

    blk = pl.BlockSpec((rb, c_), lambda i: (i, 0))
    return pl.pallas_call(
        body, name=name, grid=(r // rb,),
        in_specs=[pl.BlockSpec((NDEV, rb, c_), lambda i: (0, i, 0)), blk, blk, blk],
        out_specs=[blk, blk, blk, blk],
        out_shape=[jax.ShapeDtypeStruct((r, c_), F32)] * 4,
        compiler_params=_params("arbitrary"),
    )(parts, w, m, v)


def _sum_parts(parts, loss_row):
    _, r, c_ = parts.shape

    def body(p_ref, o_ref, l_ref):
        g = p_ref[0]
        for k in range(1, NDEV):
            g = g + p_ref[k]
        o_ref[...] = g
        per_row = jnp.sum(o_ref[loss_row:loss_row + 8, :], axis=1, keepdims=True)
        l_ref[...] = jnp.broadcast_to(jnp.sum(per_row, axis=0, keepdims=True), (8, c_))

    return pl.pallas_call(body, name="sum_small", out_shape=[jax.ShapeDtypeStruct((r, c_), F32),
                                                             jax.ShapeDtypeStruct((8, c_), F32)])(parts)


def _adamw_small(g, w, m, v):
    def body(g_ref, w_ref, m_ref, v_ref, d_ref, mo_ref, vo_ref):
        d_ref[...], mo_ref[...], vo_ref[...] = _adamw_math(w_ref[...], g_ref[...], m_ref[...], v_ref[...])

    return pl.pallas_call(body, name="adamw_small", out_shape=[jax.ShapeDtypeStruct(g.shape, F32)] * 3)(g, w, m, v)


def _pad_lanes(a):
    return jnp.pad(a, ((0, 0), (0, LANES - a.shape[1])))


def _tiles(a, t):
    return a.reshape(H, a.shape[1] // t, t, HA)


def kernel(x, conv_norm_g, conv_w_in, conv_w, conv_w_out, attn_norm_g, attn_w_in, attn_b_f, attn_q_norm_g, attn_k_norm_g, attn_w_out, loss_target, m_conv_norm_g, m_conv_w_in, m_conv_w, m_conv_w_out, m_attn_norm_g, m_attn_w_in, m_attn_b_f, m_attn_q_norm_g, m_attn_k_norm_g, m_attn_w_out, v_conv_norm_g, v_conv_w_in, v_conv_w, v_conv_w_out, v_attn_norm_g, v_attn_w_in, v_attn_b_f, v_attn_q_norm_g, v_attn_k_norm_g, v_attn_w_out):
    s = x.shape[1]
    tq = min(TQ, s)
    tmb = min(TM_BWD, s)
    me = 4 * lax.axis_index("x") + 2 * lax.axis_index("y") + lax.axis_index("c")
    xv, tgt = x[0], loss_target[0]

    def slab(shards, pad, dtype):
        rows = [t[0].reshape(-1, PACK_W) for t in shards]
        return jnp.concatenate(rows + ([jnp.zeros((pad, PACK_W), F32)] if pad else []), axis=0).astype(dtype)

    wg1 = _all_gather(slab([conv_w_in, conv_w_out], 0, BF16), "gather_weights")
    small_w = jnp.concatenate([conv_w[0], attn_norm_g, jnp.zeros((4, 128), F32)], axis=0)
    sg_ = _all_gather(small_w, "gather_small_weights")
    w1in = wg1[:, :ROWS_W1IN].transpose(1, 0, 2).reshape(D, 4 * D)
    w1out = wg1[:, ROWS_W1IN:].reshape(D, D)
    cw = jnp.concatenate([sg_[:, 0:3, :].transpose(1, 0, 2).reshape(3, D), jnp.zeros((5, D), F32)], axis=0)
    g2 = sg_[:, 3, :].reshape(1, D)
    qg_t, kg_t = jnp.tile(attn_q_norm_g, (1, H)), jnp.tile(attn_k_norm_g, (1, H))
    bf = _pad_lanes(attn_b_f)

    e = (jnp.arange(D)[:, None] // DH == jnp.arange(LANES)[None, :]).astype(BF16)
    et2 = jnp.concatenate([e.T, e.T], axis=0)
    fold = (jnp.arange(D)[:, None] % DH == jnp.arange(LANES)[None, :]).astype(BF16)
    tril = (jnp.arange(tmb)[:, None] >= jnp.arange(tmb)[None, :]).astype(BF16)
    triu = tril.T
    src = jnp.arange(3 * LANES)
    dst = jnp.arange(H * HA)[None, :] - (HA * (src % LANES) + src // LANES)[:, None]
    place_k = ((dst == COL_BIAS) & (src % LANES < H)[:, None]).astype(BF16)
    place_q = ((dst == ROW_BIAS) & (src % LANES < H)[:, None]).astype(BF16)

    x1, h1, p1, yg, wg2 = _conv_fwd(xv, conv_norm_g, w1in, cw, w1out, slab([attn_w_in, attn_w_out], PAD_ROWS_W2, BF16))
    w2all = wg2[:, :ROWS_W2IN].reshape(NDEV, D, 514).transpose(1, 0, 2).reshape(D, 4 * D + H)
    w2in, wf = w2all[:, :4 * D], _pad_lanes(w2all[:, 4 * D:])
    w2out = wg2[:, ROWS_W2IN:ROWS_W2IN + ROWS_WOUT].reshape(D, D)
    reach = 1.01 * math.sqrt(DH) * jnp.max(jnp.abs(attn_q_norm_g)) * jnp.max(jnp.abs(attn_k_norm_g))
    bounded = (2.0 * reach <= BOUNDED_SOFTMAX_REACH).astype(F32).reshape(1, 1)
    group = tq * min(FWD_TILES, s // tq)
    h2, qpre, kpre, z, qa, ka, va, vat, invq, invk, cc, fl = _attn_proj_fwd(
        x1, g2, w2in, wf, bf, qg_t, kg_t, jnp.broadcast_to(reach, (1, LANES)), e, et2, tril, place_k, place_q, group)
    ol, qab = _flash_fwd(bounded, _tiles(qa, tq), ka, vat)
    og, dx2, dz, doa, lossp = _attn_out(ol.reshape(H, s, HA), z, x1, tgt, w2out, w2out.T, e, place_k)

    dq, dk, dv = _flash_bwd(qab, _tiles(doa, tq), _tiles(ka, tq), _tiles(va, tq))
    dx1, dp2, df, dg2, dqg, dkg, dbf = _attn_proj_bwd(
        dq, dk, dv, dz, qpre, kpre, invq, invk, fl, x1, dx2, g2, qg_t, kg_t, w2in.T, wf.T, e, et2, triu, fold)
    dw2out = _wgrad(og, dx2, "wgrad_attn_out")
    dw2in = _wgrad(h2, dp2, "wgrad_attn_in")
    dwf = _wgrad(h2, df, "wgrad_attn_forget")
    dw1out = _wgrad(yg, dx1, "wgrad_conv_out")
    dw2all = jnp.concatenate([dw2in, dwf[:, :H]], axis=1)
    grads2 = jnp.concatenate([
        dw2all.reshape(D, NDEV, 514).transpose(1, 0, 2).reshape(NDEV, ROWS_W2IN, PACK_W),
        dw2out.reshape(NDEV, ROWS_WOUT, PACK_W), dw1out.reshape(NDEV, ROWS_WOUT, PACK_W),
        jnp.zeros((NDEV, PAD_ROWS2, PACK_W), BF16)], axis=1)
    gx, dp1, dcw, dg1, parts2 = _conv_bwd(dx1, p1, xv, conv_norm_g, cw, w1out.T, w1in.T, grads2)
    dw1in = _wgrad(h1, dp1, "wgrad_conv_in")
    parts1 = _exchange_blocks(dw1in.reshape(D, NDEV, 512).transpose(1, 0, 2), "scatter_grads")

    big1 = _adamw_big(parts1, slab([conv_w_in], 0, F32), slab([m_conv_w_in], 0, F32), slab([v_conv_w_in], 0, F32),
                      "adamw_conv_in")
    big2 = _adamw_big(parts2, slab([attn_w_in, attn_w_out, conv_w_out], PAD_ROWS2, F32),
                      slab([m_attn_w_in, m_attn_w_out, m_conv_w_out], PAD_ROWS2, F32),
                      slab([v_attn_w_in, v_attn_w_out, v_conv_w_out], PAD_ROWS2, F32), "adamw_rest")

    shard_rows = jnp.concatenate([dcw[0:3].reshape(3, NDEV, 128).transpose(1, 0, 2), dg2.reshape(NDEV, 1, 128),
                                  jnp.zeros((NDEV, 4, 128), F32)], axis=1).reshape(64, 128)
    small_g = jnp.concatenate([shard_rows, dg1.reshape(8, 128), dbf, dqg[0:1], dkg[0:1], jnp.zeros((5, 128), F32),
                               lossp.reshape(8, 128)], axis=0)
    gs, loss8 = _sum_parts(_all_gather(small_g, "gather_small_grads"), 80)
    loss = loss8[0, 0]
    mine = lax.dynamic_slice(gs, (8 * me, 0), (8, 128))
    g_small = jnp.concatenate([mine, gs[64:75], jnp.zeros((5, 128), F32)], axis=0)

    def pack_small(cwk, ang, cng, bfk, qgk, kgk):
        return jnp.concatenate([cwk[0], ang, jnp.zeros((4, 128), F32), cng.reshape(8, 128), _pad_lanes(bfk),
                                _pad_lanes(qgk), _pad_lanes(kgk), jnp.zeros((5, 128), F32)], axis=0)

    ds_, ms_, vs_ = _adamw_small(
        g_small, pack_small(conv_w, attn_norm_g, conv_norm_g, attn_b_f, attn_q_norm_g, attn_k_norm_g),
        pack_small(m_conv_w, m_attn_norm_g, m_conv_norm_g, m_attn_b_f, m_attn_q_norm_g, m_attn_k_norm_g),
        pack_small(v_conv_w, v_attn_norm_g, v_conv_norm_g, v_attn_b_f, v_attn_q_norm_g, v_attn_k_norm_g))

    def leaves(i, small):
        b1, b2 = big1[i], big2[i]
        w2o, w1o = ROWS_W2IN, ROWS_W2IN + ROWS_WOUT
        return (small[8:16].reshape(1, D), b1.reshape(1, D, 512), small[0:3].reshape(1, 3, 128),
                b2[w1o:w1o + ROWS_WOUT].reshape(1, 128, D), small[3:4], b2[:ROWS_W2IN].reshape(1, D, 514),
                small[16:17, :H], small[17:18, :DH], small[18:19, :DH], b2[w2o:w1o].reshape(1, 128, D))

    return (loss, gx[None], *leaves(0, g_small), *leaves(1, ds_), *leaves(2, ms_), *leaves(3, vs_))
```

```python
import functools
import math

import jax
import jax.numpy as jnp
from jax import lax
from jax.experimental import pallas as pl
from jax.experimental.pallas import tpu as pltpu

F32 = jnp.float32
BF16 = jnp.bfloat16

D = 1024
H = 16
DH = 64
NDEV = 8
RMS_EPS = 1e-6
LANES = 128
HA = 128
TM_FWD = 512
TM_BWD = 256
TQ = 512
AUG_ROWS = 80
FWD_TILES = 8
BWD_TILES = 4
CH = 256
PACK_W = 512
ROWS_W1IN, ROWS_W2IN, ROWS_WOUT = 1024, 1028, 256
ADAM_BLOCKS = 4
PAD_ROWS_W2 = 60
ADAM_LR, ADAM_B1, ADAM_B2, ADAM_EPS, ADAM_WD, ADAM_STEP = 0.001, 0.9, 0.999, 1e-08, 0.01, 10
VMEM_LIMIT = 56 * 1024 * 1024
MASKED = -1e30
BOUNDED_SOFTMAX_REACH = 60.0
MESH = pl.DeviceIdType.MESH


def _params(*sem, vmem=VMEM_LIMIT):
    return pltpu.CompilerParams(dimension_semantics=sem or None, vmem_limit_bytes=vmem)


def _const(shape):
    nd = len(shape)
    return pl.BlockSpec(shape, lambda *_: (0,) * nd, pipeline_mode=pl.Buffered(1))


def _rows(tm, n, rev=False, steps=None):
    if rev:
        return pl.BlockSpec((tm, n), lambda i: (steps - 1 - i, 0))
    return pl.BlockSpec((tm, n), lambda i: (i, 0))


def _dot(a, b):
    return jnp.dot(a, b, preferred_element_type=F32)


def _top16(x):
    bits = lax.bitcast_convert_type(x, jnp.uint32) & jnp.uint32(0xFFFF0000)
    return lax.bitcast_convert_type(bits, F32)


def _split2(x):
    hi = _top16(x)
    return hi.astype(BF16), (x - hi).astype(BF16)


def _split3(x):
    hi = _top16(x)
    r = x - hi
    mid = _top16(r)
    return hi.astype(BF16), mid.astype(BF16), (r - mid).astype(BF16)


def _seg_sum(a, e):
    hi, lo = _split2(a)
    return _dot(hi, e) + _dot(lo, e)


def _seg_bcast(s, et2):
    return _dot(jnp.concatenate(_split2(s), axis=1), et2)


def _tri_sum(t, v):
    hi, mid, lo = _split3(v)
    return _dot(t, hi) + _dot(t, mid) + _dot(t, lo)


def _sigmoid(z):
    return 1.0 / (1.0 + jnp.exp(-z))


def _place():
    return lax.axis_index("x"), lax.axis_index("y"), lax.axis_index("c")


def _all_gather(xb, name):
    r, c_ = xb.shape

    def body(x_ref, out_ref, send_sems, recv_sems, local_sem):
        x, y, c = _place()
        me, sibling = (x, y, c), (x, y, 1 - c)
        chips = [(1 - x, y), (x, 1 - y), (1 - x, 1 - y)]

        def slab(px, py, pc):
            return out_ref.at[4 * px + 2 * py + pc]

        def copy(k, block, to, src=None):
            return pltpu.make_async_remote_copy(
                src_ref=slab(*block) if src is None else src, dst_ref=slab(*block),
                send_sem=send_sems.at[k], recv_sem=recv_sems.at[k], device_id=to, device_id_type=MESH)

        mine = pltpu.make_async_copy(x_ref, slab(*me), local_sem)
        mine.start()
        first = [copy(0, me, sibling, src=x_ref)]
        first += [copy(1 + j, me, (*chip, c), src=x_ref) for j, chip in enumerate(chips)]
        for cp in first:
            cp.start()
        passed = [copy(4 + j, (*chip, c), sibling) for j, chip in enumerate(chips)]
        for j, chip in enumerate(chips):
            copy(1 + j, (*chip, c), me).wait_recv()
            passed[j].start()
        copy(0, sibling, me).wait_recv()
        for j, chip in enumerate(chips):
            copy(4 + j, (*chip, 1 - c), me).wait_recv()
        for cp in first + passed:
            cp.wait_send()
        mine.wait()

    return pl.pallas_call(
        body, name=name,
        out_shape=jax.ShapeDtypeStruct((NDEV, r, c_), xb.dtype),
        in_specs=[pl.BlockSpec(memory_space=pl.ANY)],
        out_specs=pl.BlockSpec(memory_space=pl.ANY),
        scratch_shapes=[pltpu.SemaphoreType.DMA((7,)), pltpu.SemaphoreType.DMA((7,)), pltpu.SemaphoreType.DMA],
    )(xb)


def _block_copies(g_ref, out_ref, send_sems, recv_sems, local_sem):
    x, y, c = _place()
    copies = [pltpu.make_async_copy(g_ref.at[4 * x + 2 * y + c], out_ref.at[0], local_sem)]
    for k in range(1, NDEV):
        px = 1 - x if k & 4 else x
        py = 1 - y if k & 2 else y
        pc = 1 - c if k & 1 else c
        copies.append(pltpu.make_async_remote_copy(
            src_ref=g_ref.at[4 * px + 2 * py + pc], dst_ref=out_ref.at[k],
            send_sem=send_sems.at[k - 1], recv_sem=recv_sems.at[k - 1],
            device_id=(px, py, pc), device_id_type=MESH))
    return copies


def _gather_copies(x_ref, out_ref, send_sems, recv_sems, local_sem):
    x, y, c = _place()
    mine = out_ref.at[4 * x + 2 * y + c]
    copies = [pltpu.make_async_copy(x_ref, mine, local_sem)]
    for k in range(1, NDEV):
        peer = (1 - x if k & 4 else x, 1 - y if k & 2 else y, 1 - c if k & 1 else c)
        copies.append(pltpu.make_async_remote_copy(
            src_ref=x_ref, dst_ref=mine, send_sem=send_sems.at[k - 1], recv_sem=recv_sems.at[k - 1],
            device_id=peer, device_id_type=MESH))
    return copies


EXCHANGE_SEMS = [pltpu.SemaphoreType.DMA((7,)), pltpu.SemaphoreType.DMA((7,)), pltpu.SemaphoreType.DMA]


def _exchange_blocks(g, name):
    _, r, c_ = g.shape

    def body(g_ref, out_ref, send_sems, recv_sems, local_sem):
        copies = _block_copies(g_ref, out_ref, send_sems, recv_sems, local_sem)
        for cp in copies:
            cp.start()
        for cp in copies:
            cp.wait()

    return pl.pallas_call(
        body, name=name,
        out_shape=jax.ShapeDtypeStruct((NDEV, r, c_), g.dtype),
        in_specs=[pl.BlockSpec(memory_space=pl.ANY)],
        out_specs=pl.BlockSpec(memory_space=pl.ANY),
        scratch_shapes=list(EXCHANGE_SEMS),
    )(g)


def _conv_fwd(x, g1, w1in, cw, w1out, shard2):
    s = x.shape[0]
    tm = min(TM_FWD, s)
    steps = s // tm

    def body(x_ref, g_ref, win_ref, cw_ref, wout_ref, sh_ref, x1_ref, h1_ref, p1_ref, yg_ref, all_ref, tail_ref,
             send_sems, recv_sems, local_sem):
        @pl.when(pl.program_id(0) == 0)
        def _():
            tail_ref[...] = jnp.zeros_like(tail_ref)
            for cp in _gather_copies(sh_ref, all_ref, send_sems, recv_sems, local_sem):
                cp.start()

        xv = x_ref[...]
        inv = lax.rsqrt(jnp.mean(xv * xv, axis=-1, keepdims=True) + RMS_EPS)
        h = (xv * inv * g_ref[...]).astype(BF16)
        h1_ref[...] = h
        row = lax.broadcasted_iota(jnp.int32, (tm, CH), 0)
        for ci in range(D // CH):
            lo, hi = ci * CH, (ci + 1) * CH
            parts = []
            for k in range(4):
                pk = _dot(h, win_ref[:, k * D + lo:k * D + hi]).astype(BF16)
                p1_ref[:, k * D + lo:k * D + hi] = pk
                parts.append(pk.astype(F32))
            b, c, xin, z = parts
            u = c * xin
            t6 = tail_ref[6:7, lo:hi]
            t7 = tail_ref[7:8, lo:hi]
            u1 = jnp.where(row == 0, t7, pltpu.roll(u, 1, 0))
            u2 = jnp.where(row == 0, t6, jnp.where(row == 1, t7, pltpu.roll(u, 2, 0)))
            tail_ref[:, lo:hi] = u[tm - 8:, :]
            y = cw_ref[2:3, lo:hi] * u + cw_ref[1:2, lo:hi] * u1 + cw_ref[0:1, lo:hi] * u2
            yg_ref[:, lo:hi] = (b * y * (z * _sigmoid(z))).astype(BF16)
        x1_ref[...] = xv + _dot(yg_ref[...], wout_ref[...])

        @pl.when(pl.program_id(0) == steps - 1)
        def _():
            for cp in _gather_copies(sh_ref, all_ref, send_sems, recv_sems, local_sem):
                cp.wait()

    return pl.pallas_call(
        body, name="conv_fwd", grid=(steps,),
        in_specs=[_rows(tm, D), _const((1, D)), _const((D, 4 * D)), _const((8, D)), _const((D, D)),
                  pl.BlockSpec(memory_space=pl.ANY)],
        out_specs=[_rows(tm, D), _rows(tm, D), _rows(tm, 4 * D), _rows(tm, D), pl.BlockSpec(memory_space=pl.ANY)],
        out_shape=[jax.ShapeDtypeStruct((s, D), F32), jax.ShapeDtypeStruct((s, D), BF16),
                   jax.ShapeDtypeStruct((s, 4 * D), BF16), jax.ShapeDtypeStruct((s, D), BF16),
                   jax.ShapeDtypeStruct((NDEV,) + shard2.shape, shard2.dtype)],
        scratch_shapes=[pltpu.VMEM((8, D), F32)] + list(EXCHANGE_SEMS),
        compiler_params=_params("arbitrary"),
    )(x, g1, w1in, cw, w1out, shard2)


COL_BIAS = DH
ROW_BIAS = DH + 3


def _terms_cat(val):
    return jnp.concatenate(_split3(val), axis=1)


def _split_heads(x, aug, out_ref, lane, t_ref=None):
    for hp in range(H // 2):
        pair = x[:, hp * HA:(hp + 1) * HA]
        for k, feat in enumerate((pair, pltpu.roll(pair, DH, 1))):
            block = jnp.where(lane < DH, feat, aug(2 * hp + k))
            out_ref[2 * hp + k] = block.astype(BF16)
            if t_ref is not None:
                t_ref[2 * hp + k] = block.T[0:t_ref.shape[1], :].astype(BF16)


def _ones_at(lane, first):
    return jnp.where((lane >= first) & (lane < first + 3), 1.0, 0.0)


def _attn_proj_fwd(x1, g2, w2in, wf, bf, qg, kg, shift, e, et, tril, place_k, place_q, group):
    s = x1.shape[0]
    tm = min(TM_BWD, s)
    steps = s // tm
    per = group // tm

    def body(x_ref, g_ref, win_ref, wf_ref, bf_ref, qg_ref, kg_ref, sh_ref, e_ref, et_ref, tril_ref, pk_ref, pq_ref,
             h2_ref, qp_ref, kp_ref, z_ref, qa_ref, ka_ref, va_ref, vat_ref, iq_ref, ik_ref, c_ref, fl_ref, carry_ref):
        @pl.when(pl.program_id(0) == 0)
        def _():
            carry_ref[...] = jnp.zeros_like(carry_ref)

        xv = x_ref[...]
        inv = lax.rsqrt(jnp.mean(xv * xv, axis=-1, keepdims=True) + RMS_EPS)
        h = (xv * inv * g_ref[...]).astype(BF16)
        h2_ref[...] = h

        fl = _dot(h, wf_ref[...]) + bf_ref[...]
        fl_ref[...] = fl
        ex = jnp.exp(-jnp.abs(fl))
        up = 1.0 + ex
        log1p = jnp.where(up == 1.0, ex, jnp.log(up) * ex / (up - 1.0))
        lane = lax.broadcasted_iota(jnp.int32, (tm, LANES), 1)
        logf = jnp.where(lane < H, jnp.minimum(fl, 0.0) - log1p, 0.0)
        c_ref[...] = _tri_sum(tril_ref[...], logf) + carry_ref[0:1, :]
        carry_ref[0:1, :] = c_ref[tm - 1:tm, :]
        c = c_ref[...]

        def normed(col, pre_ref, inv_ref, gain_ref, scale):
            pre = _dot(h, win_ref[:, col * D:(col + 1) * D]).astype(BF16)
            pre_ref[...] = pre
            t = pre.astype(F32)
            invh = lax.rsqrt(_seg_sum(t * t, e_ref[...]) * (1.0 / DH) + RMS_EPS)
            inv_ref[...] = invh
            return t * _seg_bcast(invh, et_ref[...]) * (gain_ref[...] * scale)

        ones_col, ones_row = _ones_at(lane, COL_BIAS), _ones_at(lane, ROW_BIAS)
        q_bias = _dot(_terms_cat(c - sh_ref[...]), pq_ref[...])
        _split_heads(normed(0, qp_ref, iq_ref, qg_ref, 1.0 / math.sqrt(DH)),
                     lambda hh: q_bias[:, hh * HA:(hh + 1) * HA] + ones_col, qa_ref, lane)
        k_bias = _dot(_terms_cat(-c), pk_ref[...])
        _split_heads(normed(1, kp_ref, ik_ref, kg_ref, 1.0),
                     lambda hh: k_bias[:, hh * HA:(hh + 1) * HA] + ones_row, ka_ref, lane)
        v = _dot(h, win_ref[:, 2 * D:3 * D])
        _split_heads(v, lambda hh: ones_col, va_ref, lane, vat_ref)
        z_ref[...] = _dot(h, win_ref[:, 3 * D:4 * D]).astype(BF16)

    row_bf = lambda: _rows(tm, D)
    row_sm = lambda: _rows(tm, LANES)
    heads = lambda: pl.BlockSpec((H, tm, HA), lambda i: (0, i, 0))
    return pl.pallas_call(
        body, name="attn_proj_fwd", grid=(steps,),
        in_specs=[_rows(tm, D), _const((1, D)), _const((D, 4 * D)), _const((D, LANES)), _const((1, LANES)),
                  _const((1, D)), _const((1, D)), _const((1, LANES)), _const((D, LANES)), _const((2 * LANES, D)),
                  _const((tm, tm)), _const((3 * LANES, H * HA)), _const((3 * LANES, H * HA))],
        out_specs=[row_bf() for _ in range(4)] + [heads() for _ in range(3)] + [
            pl.BlockSpec((H, None, AUG_ROWS, tm), lambda i: (0, i // per, 0, i % per))] + [row_sm() for _ in range(4)],
        out_shape=[jax.ShapeDtypeStruct((s, D), BF16)] * 4 + [jax.ShapeDtypeStruct((H, s, HA), BF16)] * 3 + [
            jax.ShapeDtypeStruct((H, s // group, AUG_ROWS, group), BF16)] + [jax.ShapeDtypeStruct((s, LANES), F32)] * 4,
        scratch_shapes=[pltpu.VMEM((8, LANES), F32)],
        compiler_params=_params("arbitrary"),
    )(x1, g2, w2in, wf, bf, qg, kg, shift, e, et, tril, place_k, place_q)


def _dot_nt(a, b):
    return lax.dot_general(a, b, (((1,), (1,)), ((), ())), preferred_element_type=F32)


def _dot_tn(a, b):
    return lax.dot_general(a, b, (((0,), (0,)), ((), ())), preferred_element_type=F32)


def _flash_fwd(bounded, qa, ka, vat):
    h_, nq, tq, _ = qa.shape
    s, tk = ka.shape[1], tq
    group = vat.shape[3]
    gt = group // tk
    shift = gt.bit_length() - 1

    def body(flag_ref, q_ref, k_ref, vt_ref, o_ref, qb_ref, m_ref, acct_ref):
        i = pl.program_id(1)
        lane = lax.broadcasted_iota(jnp.int32, (tq, HA), 1)
        in_bias = (lane >= ROW_BIAS) & (lane < ROW_BIAS + 3)
        acct_ref[...] = jnp.zeros_like(acct_ref)
        g = lax.shift_right_logical(i, shift)
        r = i & (gt - 1)

        def run(use_bound):
            q = q_ref[...]
            if not use_bound:
                q = jnp.where(in_bias, jnp.zeros_like(q), q)
                m_ref[...] = jnp.full_like(m_ref, MASKED)

            def step(first_key, vt, tiles, diagonal_at):
                keys = pl.ds(pl.multiple_of(first_key, tk), tiles * tk)
                zt = _dot_nt(k_ref[keys, :], q)
                if diagonal_at is not None:
                    key = lax.broadcasted_iota(jnp.int32, (tiles * tk, tq), 0)
                    qry = lax.broadcasted_iota(jnp.int32, (tiles * tk, tq), 1)
                    zt = jnp.where(key <= qry + diagonal_at * tk, zt, MASKED)
                if use_bound:
                    acct_ref[0:AUG_ROWS, :] += _dot(vt, jnp.exp(zt).astype(BF16))
                else:
                    m_old = m_ref[...]
                    m_new = jnp.maximum(m_old, jnp.max(zt, axis=0, keepdims=True))
                    pt = jnp.exp(zt - m_new)
                    acct_ref[0:AUG_ROWS, :] = jnp.exp(m_old - m_new) * acct_ref[0:AUG_ROWS, :] + _dot(
                        vt, pt.astype(BF16))
                    m_ref[...] = m_new

            def whole_group(jj, carry):
                step(jj * group, vt_ref[jj], gt, None)
                return carry

            lax.fori_loop(0, g, whole_group, 0)
            for rr in range(gt):
                pl.when(r == rr)(functools.partial(
                    lambda rr: step(g * group, vt_ref[g, :, 0:(rr + 1) * tk], rr + 1, rr), rr))

            if not use_bound:
                acct_ref[COL_BIAS + 1:COL_BIAS + 2, :] = m_ref[...]
            acc = acct_ref[...].T
            l = jnp.sum(jnp.where(lane == COL_BIAS, acc, 0.0), axis=1, keepdims=True)
            if use_bound:
                m = -jnp.sum(jnp.where(in_bias, q.astype(F32), 0.0), axis=1, keepdims=True)
            else:
                m = jnp.sum(jnp.where(lane == COL_BIAS + 1, acc, 0.0), axis=1, keepdims=True)
            lse = m + jnp.log(l)
            o_ref[...] = jnp.where(lane < DH, acc / l, lse)
            hi, mid, lo = _split3(-lse)
            qb_ref[...] = jnp.where(lane == ROW_BIAS, hi, jnp.where(lane == ROW_BIAS + 1, mid, jnp.where(
                lane == ROW_BIAS + 2, lo, q_ref[...])))

        use_bound = flag_ref[0, 0] > 0.5
        pl.when(use_bound)(lambda: run(True))
        pl.when(jnp.logical_not(use_bound))(lambda: run(False))

    tile = lambda: pl.BlockSpec((None, None, tq, HA), lambda h, i: (h, i, 0, 0))
    return pl.pallas_call(
        body, name="flash_fwd", grid=(h_, nq),
        in_specs=[pl.BlockSpec(memory_space=pltpu.SMEM), tile(), pl.BlockSpec((None, s, HA), lambda h, i: (h, 0, 0)),
                  pl.BlockSpec((None, s // group, AUG_ROWS, group), lambda h, i: (h, 0, 0, 0))],
        out_specs=[tile(), tile()],
        out_shape=[jax.ShapeDtypeStruct((h_, nq, tq, HA), F32), jax.ShapeDtypeStruct((h_, nq, tq, HA), BF16)],
        scratch_shapes=[pltpu.VMEM((1, tq), F32), pltpu.VMEM((HA, tq), F32)],
        compiler_params=_params("arbitrary", "arbitrary"),
    )(bounded, qa, ka, vat)


def _merge_heads(src, out_ref, lane):
    for hp in range(H // 2):
        out_ref[:, hp * HA:(hp + 1) * HA] = jnp.where(lane < DH, src(2 * hp), pltpu.roll(src(2 * hp + 1), DH, 1))


def _attn_out(ol, z, x1, tgt, w2out, w2out_t, e, place_k):
    s = z.shape[0]
    tm = min(TM_BWD, s)
    steps = s // tm

    def body(ol_ref, z_ref, x1_ref, t_ref, w_ref, wt_ref, e_ref, pk_ref, og_ref, dx2_ref, dz_ref, doa_ref, loss_ref,
             o_ref):
        @pl.when(pl.program_id(0) == 0)
        def _():
            loss_ref[...] = jnp.zeros_like(loss_ref)

        lane = lax.broadcasted_iota(jnp.int32, (tm, HA), 1)
        _merge_heads(lambda hh: ol_ref[hh], o_ref, lane)
        ov = o_ref[...]
        zv = z_ref[...].astype(F32)
        sg = _sigmoid(zv)
        sil = zv * sg
        og = (ov * sil).astype(BF16)
        og_ref[...] = og
        err = x1_ref[...] + _dot(og, w_ref[...]) - t_ref[...]
        loss_ref[...] += (0.5 / D) * jnp.sum(err * err, axis=0, keepdims=True)
        dx2 = err * (1.0 / D)
        dx2_ref[...] = dx2
        dog = _dot(dx2.astype(BF16), wt_ref[...])
        do = (dog * sil).astype(BF16).astype(F32)
        dz_ref[...] = (dog * ov * (sg * (1.0 + zv * (1.0 - sg)))).astype(BF16)
        delta = _seg_sum(do * ov, e_ref[...])
        d_bias = _dot(_terms_cat(-delta), pk_ref[...])
        _split_heads(do, lambda hh: d_bias[:, hh * HA:(hh + 1) * HA], doa_ref, lane)

    heads = lambda: pl.BlockSpec((H, tm, HA), lambda i: (0, i, 0))
    return pl.pallas_call(
        body, name="attn_out", grid=(steps,),
        in_specs=[heads(), _rows(tm, D), _rows(tm, D), _rows(tm, D), _const((D, D)), _const((D, D)),
                  _const((D, LANES)), _const((3 * LANES, H * HA))],
        out_specs=[_rows(tm, D), _rows(tm, D), _rows(tm, D), heads(), pl.BlockSpec((1, D), lambda i: (0, 0))],
        out_shape=[jax.ShapeDtypeStruct((s, D), BF16), jax.ShapeDtypeStruct((s, D), F32),
                   jax.ShapeDtypeStruct((s, D), BF16), jax.ShapeDtypeStruct((H, s, HA), BF16),
                   jax.ShapeDtypeStruct((1, D), F32)],
        scratch_shapes=[pltpu.VMEM((tm, D), F32)],
        compiler_params=_params("arbitrary"),
    )(ol, z, x1, tgt, w2out, w2out_t, e, place_k)


def _flash_bwd(qab, doa, ka, va):
    h_, nq, tq, _ = qab.shape
    nk, tk = ka.shape[1], ka.shape[2]
    s = nk * tk
    gt = min(BWD_TILES, nq)
    shift = gt.bit_length() - 1

    def body(qa_ref, da_ref, ka_ref, va_ref, dq_hbm, dk_ref, dv_ref, dq_acc, dkt_ref, dvt_ref, sem):
        hh = pl.program_id(0)
        j = pl.program_id(1)

        @pl.when(j == 0)
        def _():
            dq_acc[...] = jnp.zeros_like(dq_acc)

        dkt_ref[...] = jnp.zeros_like(dkt_ref)
        dvt_ref[...] = jnp.zeros_like(dvt_ref)
        ka_v = ka_ref[...]
        va_v = va_ref[...]

        def step(i, tiles, masked):
            rows = pl.ds(pl.multiple_of(i * tq, tq), tiles * tq)
            qa_i, da_i = qa_ref[rows, :], da_ref[rows, :]
            zz = _dot_nt(qa_i, ka_v)
            if masked:
                row = lax.broadcasted_iota(jnp.int32, (tiles * tq, tk), 0)
                col = lax.broadcasted_iota(jnp.int32, (tiles * tq, tk), 1)
                zz = jnp.where(col <= row, zz, MASKED)
            p = jnp.exp(zz)
            ds = (p * _dot_nt(da_i, va_v)).astype(BF16)
            pb = p.astype(BF16)
            dq_acc[rows, :] += _dot(ds, ka_v)
            dvt_ref[...] += _dot_tn(da_i, pb)
            dkt_ref[...] += _dot_tn(qa_i, ds)

        g0 = lax.shift_right_logical(j, shift)
        for rr in range(gt):
            pl.when((j & (gt - 1)) == rr)(functools.partial(step, j, gt - rr, True))

        def whole_group(g, carry):
            step(g * gt, gt, False)
            return carry

        lax.fori_loop(g0 + 1, nq // gt, whole_group, 0)
        dk_ref[...] = dkt_ref[...].T
        dv_ref[...] = dvt_ref[...].T.astype(BF16)

        @pl.when(j == nk - 1)
        def _():
            cp = pltpu.make_async_copy(dq_acc, dq_hbm.at[hh], sem)
            cp.start()
            cp.wait()

    whole = lambda: pl.BlockSpec((None, nq * tq, HA), lambda h, j: (h, 0, 0))
    tile = lambda: pl.BlockSpec((None, None, tk, HA), lambda h, j: (h, j, 0, 0))
    rows = lambda: pl.BlockSpec((None, tk, HA), lambda h, j: (h, j, 0))
    return pl.pallas_call(
        body, name="flash_bwd", grid=(h_, nk),
        in_specs=[whole(), whole(), tile(), tile()],
        out_specs=[pl.BlockSpec(memory_space=pl.ANY), rows(), rows()],
        out_shape=[jax.ShapeDtypeStruct((h_, s, HA), F32)] * 2 + [jax.ShapeDtypeStruct((h_, s, HA), BF16)],
        scratch_shapes=[pltpu.VMEM((s, HA), F32), pltpu.VMEM((HA, tk), F32), pltpu.VMEM((HA, tk), F32),
                        pltpu.SemaphoreType.DMA],
        compiler_params=_params("arbitrary", "arbitrary"),
    )(qab.reshape(h_, s, HA), doa.reshape(h_, s, HA), ka, va)


def _attn_proj_bwd(dq, dk, dv, dz, qpre, kpre, invq, invk, fl, x1, dx2, g2, qg, kg, w2in_t, wf_t, e, et,
                   triu, fold):
    s = x1.shape[0]
    tm = min(TM_BWD, s)
    steps = s // tm

    def body(dq_ref, dk_ref, dvh_ref, dz_ref, qp_ref, kp_ref, iq_ref, ik_ref, fl_ref, x1_ref, dx2_ref,
             g2_ref, qg_ref, kg_ref, wt_ref, wft_ref, e_ref, et_ref, triu_ref, fold_ref,
             dx1_ref, dp_ref, df_ref, dg2_ref, dqg_ref, dkg_ref, dbf_ref, carry_ref, rc_ref, qcol_ref, kcol_ref,
             dqs_ref, dkn_ref, dv_ref):
        step = pl.program_id(0)
        lane = lax.broadcasted_iota(jnp.int32, (tm, HA), 1)
        _merge_heads(lambda hh: dvh_ref[hh].astype(F32), dv_ref, lane)
        dc = jnp.zeros((tm, HA), F32)
        for hp in range(H // 2):
            dqh = [dq_ref[2 * hp], dq_ref[2 * hp + 1]]
            dkh = [dk_ref[2 * hp], dk_ref[2 * hp + 1]]
            dqs_ref[:, hp * HA:(hp + 1) * HA] = jnp.where(lane < DH, dqh[0], pltpu.roll(dqh[1], DH, 1))
            dkn_ref[:, hp * HA:(hp + 1) * HA] = jnp.where(lane < DH, dkh[0], pltpu.roll(dkh[1], DH, 1))
            for k in range(2):
                sums = jnp.where(lane == ROW_BIAS, dqh[k], 0.0) - pltpu.roll(
                    jnp.where(lane == COL_BIAS, dkh[k], 0.0), ROW_BIAS - COL_BIAS, 1)
                dc = dc + pltpu.roll(sums, (2 * hp + k - ROW_BIAS) % HA, 1)

        @pl.when(step == 0)
        def _():
            carry_ref[...] = jnp.zeros_like(carry_ref)
            dg2_ref[...] = jnp.zeros_like(dg2_ref)
            dbf_ref[...] = jnp.zeros_like(dbf_ref)
            qcol_ref[...] = jnp.zeros_like(qcol_ref)
            kcol_ref[...] = jnp.zeros_like(kcol_ref)
            dqg_ref[...] = jnp.zeros_like(dqg_ref)
            dkg_ref[...] = jnp.zeros_like(dkg_ref)

        rc_ref[...] = _tri_sum(triu_ref[...], dc) + carry_ref[0:1, :]
        carry_ref[0:1, :] = rc_ref[0:1, :]
        df = rc_ref[...] * _sigmoid(-fl_ref[...])
        dfb = df.astype(BF16)
        df_ref[...] = dfb
        dbf_ref[...] += jnp.sum(df, axis=0, keepdims=True)

        def norm_bwd(dn, pre_ref, inv_ref, gain_ref, col_ref):
            pre = pre_ref[...].astype(F32)
            invh = inv_ref[...]
            invb = _seg_bcast(invh, et_ref[...])
            col_ref[...] += jnp.sum(dn * pre * invb, axis=0, keepdims=True)
            gd = dn * gain_ref[...]
            mean = _seg_sum(gd * pre, e_ref[...]) * (1.0 / DH)
            return invb * gd - pre * _seg_bcast(mean * invh * invh * invh, et_ref[...])

        dq = norm_bwd(dqs_ref[...] * (1.0 / math.sqrt(DH)), qp_ref, iq_ref, qg_ref, qcol_ref).astype(BF16)
        dp_ref[:, 0:D] = dq
        dh = _dot(dq, wt_ref[0:D, :])
        dk = norm_bwd(dkn_ref[...], kp_ref, ik_ref, kg_ref, kcol_ref).astype(BF16)
        dp_ref[:, D:2 * D] = dk
        dh += _dot(dk, wt_ref[D:2 * D, :])
        dvb = dv_ref[...].astype(BF16)
        dp_ref[:, 2 * D:3 * D] = dvb
        dh += _dot(dvb, wt_ref[2 * D:3 * D, :])
        dzb = dz_ref[...]
        dp_ref[:, 3 * D:4 * D] = dzb
        dh += _dot(dzb, wt_ref[3 * D:4 * D, :])
        dh += _dot(dfb, wft_ref[...])

        xv = x1_ref[...]
        inv = lax.rsqrt(jnp.mean(xv * xv, axis=-1, keepdims=True) + RMS_EPS)
        dg2_ref[...] += jnp.sum(dh * xv * inv, axis=0, keepdims=True)
        gh = dh * g2_ref[...]
        dx1_ref[...] = dx2_ref[...] + inv * gh - xv * (inv * inv * inv * jnp.mean(gh * xv, axis=-1, keepdims=True))

        @pl.when(step == steps - 1)
        def _():
            dqg_ref[...] = _fold_heads(qcol_ref[...], fold_ref[...])
            dkg_ref[...] = _fold_heads(kcol_ref[...], fold_ref[...])

    rr = lambda n: _rows(tm, n, rev=True, steps=steps)
    acc = lambda n, r=1: pl.BlockSpec((r, n), lambda i: (0, 0))
    heads = lambda: pl.BlockSpec((H, tm, HA), lambda i: (0, steps - 1 - i, 0))
    return pl.pallas_call(
        body, name="attn_proj_bwd", grid=(steps,),
        in_specs=[heads(), heads(), heads(),
                  rr(D), rr(D), rr(D), rr(LANES), rr(LANES), rr(LANES), rr(D), rr(D),
                  _const((1, D)), _const((1, D)), _const((1, D)), _const((4 * D, D)), _const((LANES, D)),
                  _const((D, LANES)), _const((2 * LANES, D)), _const((tm, tm)), _const((D, LANES))],
        out_specs=[rr(D), rr(4 * D), rr(LANES), acc(D), acc(LANES, 8), acc(LANES, 8), acc(LANES)],
        out_shape=[jax.ShapeDtypeStruct((s, D), F32), jax.ShapeDtypeStruct((s, 4 * D), BF16),
                   jax.ShapeDtypeStruct((s, LANES), BF16), jax.ShapeDtypeStruct((1, D), F32),
                   jax.ShapeDtypeStruct((8, LANES), F32), jax.ShapeDtypeStruct((8, LANES), F32),
                   jax.ShapeDtypeStruct((1, LANES), F32)],
        scratch_shapes=[pltpu.VMEM((8, LANES), F32), pltpu.VMEM((tm, LANES), F32), pltpu.VMEM((1, D), F32),
                        pltpu.VMEM((1, D), F32), pltpu.VMEM((tm, D), F32), pltpu.VMEM((tm, D), F32),
                        pltpu.VMEM((tm, D), F32)],
        compiler_params=_params("arbitrary"),
    )(dq, dk, dv, dz, qpre, kpre, invq, invk, fl, x1, dx2, g2, qg, kg, w2in_t, wf_t, e, et, triu, fold)


def _fold_heads(col, fold):
    hi, mid, lo = _split3(jnp.broadcast_to(col, (8, D)))
    return _dot(hi, fold) + _dot(mid, fold) + _dot(lo, fold)


def _conv_bwd(dx1, p1, x, g1, cw, w1out_t, w1in_t, grads2):
    s = x.shape[0]
    tm = min(TM_BWD, s)
    steps = s // tm
    halo = tm // 8

    def body(dx1_ref, b_ref, c_ref, xi_ref, z_ref, ch_ref, xh_ref, x_ref, g_ref, cw_ref, wot_ref, wit_ref, g2_ref,
             gx_ref, dp_ref, dcw_ref, dg1_ref, parts_ref, head_ref, send_sems, recv_sems, local_sem):
        step = pl.program_id(0)

        @pl.when(step == 0)
        def _():
            for cp in _block_copies(g2_ref, parts_ref, send_sems, recv_sems, local_sem):
                cp.start()

        @pl.when(step == 0)
        def _():
            head_ref[...] = jnp.zeros_like(head_ref)
            dcw_ref[...] = jnp.zeros_like(dcw_ref)
            dg1_ref[...] = jnp.zeros_like(dg1_ref)

        first_tile = step == steps - 1
        dx1 = dx1_ref[...]
        row = lax.broadcasted_iota(jnp.int32, (tm, CH), 0)
        dyb = dx1.astype(BF16)
        for ci in range(D // CH):
            lo, hi = ci * CH, (ci + 1) * CH
            dyg = _dot(dyb, wot_ref[:, lo:hi])
            b = b_ref[:, lo:hi].astype(F32)
            c = c_ref[:, lo:hi].astype(F32)
            xin = xi_ref[:, lo:hi].astype(F32)
            z = z_ref[:, lo:hi].astype(F32)
            u = c * xin
            t6 = jnp.where(first_tile, 0.0, ch_ref[6:7, lo:hi].astype(F32) * xh_ref[6:7, lo:hi].astype(F32))
            t7 = jnp.where(first_tile, 0.0, ch_ref[7:8, lo:hi].astype(F32) * xh_ref[7:8, lo:hi].astype(F32))
            u1 = jnp.where(row == 0, t7, pltpu.roll(u, 1, 0))
            u2 = jnp.where(row == 0, t6, jnp.where(row == 1, t7, pltpu.roll(u, 2, 0)))
            w0, w1, w2 = cw_ref[0:1, lo:hi], cw_ref[1:2, lo:hi], cw_ref[2:3, lo:hi]
            y = w2 * u + w1 * u1 + w0 * u2
            sg = _sigmoid(z)
            sil = z * sg
            dp_ref[:, lo:hi] = (dyg * y * sil).astype(BF16)
            dy = dyg * b * sil
            dp_ref[:, 3 * D + lo:3 * D + hi] = (dyg * b * y * (sg * (1.0 + z * (1.0 - sg)))).astype(BF16)
            dcw_ref[2:3, lo:hi] += jnp.sum(dy * u, axis=0, keepdims=True)
            dcw_ref[1:2, lo:hi] += jnp.sum(dy * u1, axis=0, keepdims=True)
            dcw_ref[0:1, lo:hi] += jnp.sum(dy * u2, axis=0, keepdims=True)
            n0 = head_ref[0:1, lo:hi]
            n1 = head_ref[1:2, lo:hi]
            dyn1 = jnp.where(row == tm - 1, n0, pltpu.roll(dy, tm - 1, 0))
            dyn2 = jnp.where(row == tm - 2, n0, jnp.where(row == tm - 1, n1, pltpu.roll(dy, tm - 2, 0)))
            head_ref[:, lo:hi] = dy[0:8, :]
            du = w2 * dy + w1 * dyn1 + w0 * dyn2
            dp_ref[:, D + lo:D + hi] = (du * xin).astype(BF16)
            dp_ref[:, 2 * D + lo:2 * D + hi] = (du * c).astype(BF16)
        dh = _dot(dp_ref[:, 0:D], wit_ref[0:D, :])
        for k in range(1, 4):
            dh += _dot(dp_ref[:, k * D:(k + 1) * D], wit_ref[k * D:(k + 1) * D, :])
        xv = x_ref[...]
        inv = lax.rsqrt(jnp.mean(xv * xv, axis=-1, keepdims=True) + RMS_EPS)
        dg1_ref[...] += jnp.sum(dh * xv * inv, axis=0, keepdims=True)
        gh = dh * g_ref[...]
        gx_ref[...] = dx1 + inv * gh - xv * (inv * inv * inv * jnp.mean(gh * xv, axis=-1, keepdims=True))

        @pl.when(step == steps - 1)
        def _():
            for cp in _block_copies(g2_ref, parts_ref, send_sems, recv_sems, local_sem):
                cp.wait()

    rr = lambda n: _rows(tm, n, rev=True, steps=steps)
    part = lambda k: pl.BlockSpec((tm, D), lambda i: (steps - 1 - i, k))
    prev8 = lambda k: pl.BlockSpec((8, D), lambda i: (jnp.maximum((steps - 1 - i) * halo - 1, 0), k))
    return pl.pallas_call(
        body, name="conv_bwd", grid=(steps,),
        in_specs=[rr(D), part(0), part(1), part(2), part(3), prev8(1), prev8(2), rr(D), _const((1, D)),
                  _const((8, D)), _const((D, D)), _const((4 * D, D)), pl.BlockSpec(memory_space=pl.ANY)],
        out_specs=[rr(D), rr(4 * D), pl.BlockSpec((8, D), lambda i: (0, 0)), pl.BlockSpec((1, D), lambda i: (0, 0)),
                   pl.BlockSpec(memory_space=pl.ANY)],
        out_shape=[jax.ShapeDtypeStruct((s, D), F32), jax.ShapeDtypeStruct((s, 4 * D), BF16),
                   jax.ShapeDtypeStruct((8, D), F32), jax.ShapeDtypeStruct((1, D), F32),
                   jax.ShapeDtypeStruct(grads2.shape, grads2.dtype)],
        scratch_shapes=[pltpu.VMEM((8, D), F32)] + list(EXCHANGE_SEMS),
        compiler_params=_params("arbitrary"),
    )(dx1, p1, p1, p1, p1, p1, p1, x, g1, cw, w1out_t, w1in_t, grads2)


def _wgrad(a, g, name):
    s, k = a.shape
    n = g.shape[1]
    bk, bn = min(k, 1024), min(n, 1024)
    ts = min(1024, s)

    def body(a_ref, g_ref, out_ref, acc_ref):
        t = pl.program_id(2)

        @pl.when(t == 0)
        def _():
            acc_ref[...] = jnp.zeros_like(acc_ref)

        acc_ref[...] += _dot_tn(a_ref[...], g_ref[...].astype(BF16))

        @pl.when(t == s // ts - 1)
        def _():
            out_ref[...] = acc_ref[...].astype(BF16)

    return pl.pallas_call(
        body, name=name, grid=(k // bk, n // bn, s // ts),
        in_specs=[pl.BlockSpec((ts, bk), lambda i, j, t: (t, i)), pl.BlockSpec((ts, bn), lambda i, j, t: (t, j))],
        out_specs=pl.BlockSpec((bk, bn), lambda i, j, t: (i, j)),
        out_shape=jax.ShapeDtypeStruct((k, n), BF16),
        scratch_shapes=[pltpu.VMEM((bk, bn), F32)],
        compiler_params=_params("arbitrary", "arbitrary", "arbitrary"),
    )(a, g)


def _adamw_math(w, g, m, v):
    m = ADAM_B1 * m + (1.0 - ADAM_B1) * g
    v = ADAM_B2 * v + (1.0 - ADAM_B2) * (g * g)
    m_hat = m / (1.0 - ADAM_B1 ** ADAM_STEP)
    v_hat = v / (1.0 - ADAM_B2 ** ADAM_STEP)
    delta = -ADAM_LR * (m_hat / (jnp.sqrt(v_hat) + ADAM_EPS) + ADAM_WD * w)
    return delta, m, v


def _adamw_big(parts, w, m, v, name):
    _, r, c_ = parts.shape
    rb = r // ADAM_BLOCKS
    assert r % (16 * ADAM_BLOCKS) == 0

    def body(p_ref, w_ref, m_ref, v_ref, g_ref, d_ref, mo_ref, vo_ref):
        g = p_ref[0].astype(F32)
        for k in range(1, NDEV):
            g = g + p_ref[k].astype(F32)
        g_ref[...] = g
        d_ref[...], mo_ref[...], vo_ref[...] = _adamw_math(w_ref[...], g, m_ref[...], v_ref[...])

    blk = pl.BlockSpec((rb, c_), lambda i: (i, 0))
    return pl.pallas_call(
        body, name=name, grid=(r // rb,),
        in_specs=[pl.BlockSpec((NDEV, rb, c_), lambda i: (0, i, 0)), blk, blk, blk],
        out_specs=[blk, blk, blk, blk],
        out_shape=[jax.ShapeDtypeStruct((r, c_), F32)] * 4,
        compiler_params=_params("arbitrary"),
    )(parts, w, m, v)


def _sum_parts(parts, loss_row):
    _, r, c_ = parts.shape

    def body(p_ref, o_ref, l_ref):
        g = p_ref[0]
        for k in range(1, NDEV):
            g = g + p_ref[k]
        o_ref[...] = g
        per_row = jnp.sum(o_ref[loss_row:loss_row + 8, :], axis=1, keepdims=True)
        l_ref[...] = jnp.broadcast_to(jnp.sum(per_row, axis=0, keepdims=True), (8, c_))

    return pl.pallas_call(body, name="sum_small", out_shape=[jax.ShapeDtypeStruct((r, c_), F32),
                                                             jax.ShapeDtypeStruct((8, c_), F32)])(parts)


def _adamw_small(g, w, m, v):
    def body(g_ref, w_ref, m_ref, v_ref, d_ref, mo_ref, vo_ref):
        d_ref[...], mo_ref[...], vo_ref[...] = _adamw_math(w_ref[...], g_ref[...], m_ref[...], v_ref[...])

    return pl.pallas_call(body, name="adamw_small", out_shape=[jax.ShapeDtypeStruct(g.shape, F32)] * 3)(g, w, m, v)


def _pad_lanes(a):
    return jnp.pad(a, ((0, 0), (0, LANES - a.shape[1])))


def _tiles(a, t):
    return a.reshape(H, a.shape[1] // t, t, HA)


def kernel(x, conv_norm_g, conv_w_in, conv_w, conv_w_out, attn_norm_g, attn_w_in, attn_b_f, attn_q_norm_g, attn_k_norm_g, attn_w_out, loss_target, m_conv_norm_g, m_conv_w_in, m_conv_w, m_conv_w_out, m_attn_norm_g, m_attn_w_in, m_attn_b_f, m_attn_q_norm_g, m_attn_k_norm_g, m_attn_w_out, v_conv_norm_g, v_conv_w_in, v_conv_w, v_conv_w_out, v_attn_norm_g, v_attn_w_in, v_attn_b_f, v_attn_q_norm_g, v_attn_k_norm_g, v_attn_w_out):
    s = x.shape[1]
    tq = min(TQ, s)
    tmb = min(TM_BWD, s)
    me = 4 * lax.axis_index("x") + 2 * lax.axis_index("y") + lax.axis_index("c")
    xv, tgt = x[0], loss_target[0]

    def slab(shards, pad, dtype):
        rows = [t[0].reshape(-1, PACK_W) for t in shards]
        return jnp.concatenate(rows + ([jnp.zeros((pad, PACK_W), F32)] if pad else []), axis=0).astype(dtype)

    wg1 = _all_gather(slab([conv_w_in, conv_w_out], 0, BF16), "gather_weights")
    small_w = jnp.concatenate([conv_w[0], attn_norm_g, jnp.zeros((4, 128), F32)], axis=0)
    sg_ = _all_gather(small_w, "gather_small_weights")
    w1in = wg1[:, :ROWS_W1IN].transpose(1, 0, 2).reshape(D, 4 * D)
    w1out = wg1[:, ROWS_W1IN:].reshape(D, D)
    cw = jnp.concatenate([sg_[:, 0:3, :].transpose(1, 0, 2).reshape(3, D), jnp.zeros((5, D), F32)], axis=0)
    g2 = sg_[:, 3, :].reshape(1, D)
    qg_t, kg_t = jnp.tile(attn_q_norm_g, (1, H)), jnp.tile(attn_k_norm_g, (1, H))
    bf = _pad_lanes(attn_b_f)

    e = (jnp.arange(D)[:, None] // DH == jnp.arange(LANES)[None, :]).astype(BF16)
    et2 = jnp.concatenate([e.T, e.T], axis=0)
    fold = (jnp.arange(D)[:, None] % DH == jnp.arange(LANES)[None, :]).astype(BF16)
    tril = (jnp.arange(tmb)[:, None] >= jnp.arange(tmb)[None, :]).astype(BF16)
    triu = tril.T
    src = jnp.arange(3 * LANES)
    dst = jnp.arange(H * HA)[None, :] - (HA * (src % LANES) + src // LANES)[:, None]
    place_k = ((dst == COL_BIAS) & (src % LANES < H)[:, None]).astype(BF16)
    place_q = ((dst == ROW_BIAS) & (src % LANES < H)[:, None]).astype(BF16)

    x1, h1, p1, yg, wg2 = _conv_fwd(xv, conv_norm_g, w1in, cw, w1out, slab([attn_w_in, attn_w_out], PAD_ROWS_W2, BF16))
    w2all = wg2[:, :ROWS_W2IN].reshape(NDEV, D, 514).transpose(1, 0, 2).reshape(D, 4 * D + H)
    w2in, wf = w2all[:, :4 * D], _pad_lanes(w2all[:, 4 * D:])
    w2out = wg2[:, ROWS_W2IN:ROWS_W2IN + ROWS_WOUT].reshape(D, D)
    reach = 1.01 * math.sqrt(DH) * jnp.max(jnp.abs(attn_q_norm_g)) * jnp.max(jnp.abs(attn_k_norm_g))
    bounded = (2.0 * reach <= BOUNDED_SOFTMAX_REACH).astype(F32).reshape(1, 1)
    group = tq * min(FWD_TILES, s // tq)
    h2, qpre, kpre, z, qa, ka, va, vat, invq, invk, cc, fl = _attn_proj_fwd(
        x1, g2, w2in, wf, bf, qg_t, kg_t, jnp.broadcast_to(reach, (1, LANES)), e, et2, tril, place_k, place_q, group)
    ol, qab = _flash_fwd(bounded, _tiles(qa, tq), ka, vat)
    og, dx2, dz, doa, lossp = _attn_out(ol.reshape(H, s, HA), z, x1, tgt, w2out, w2out.T, e, place_k)

    dq, dk, dv = _flash_bwd(qab, _tiles(doa, tq), _tiles(ka, tq), _tiles(va, tq))
    dx1, dp2, df, dg2, dqg, dkg, dbf = _attn_proj_bwd(
        dq, dk, dv, dz, qpre, kpre, invq, invk, fl, x1, dx2, g2, qg_t, kg_t, w2in.T, wf.T, e, et2, triu, fold)
    dw2out = _wgrad(og, dx2, "wgrad_attn_out")
    dw2in_t = _wgrad(dp2, h2, "wgrad_attn_in")
    dwf_t = _wgrad(df, h2, "wgrad_attn_forget")
    grads2 = jnp.concatenate([
        jnp.concatenate([dw2in_t, dwf_t[:H]], axis=0).reshape(NDEV, ROWS_W2IN, PACK_W),
        dw2out.reshape(NDEV, ROWS_WOUT, PACK_W), jnp.zeros((NDEV, PAD_ROWS_W2, PACK_W), BF16)], axis=1)
    gx, dp1, dcw, dg1, parts2 = _conv_bwd(dx1, p1, xv, conv_norm_g, cw, w1out.T, w1in.T, grads2)
    dw1out = _wgrad(yg, dx1, "wgrad_conv_out")
    dw1in = _wgrad(h1, dp1, "wgrad_conv_in")
    grads1 = jnp.concatenate([dw1in.reshape(D, NDEV, 512).transpose(1, 0, 2),
                              dw1out.reshape(NDEV, ROWS_WOUT, PACK_W)], axis=1)
    parts1 = _exchange_blocks(grads1, "scatter_grads")

    tr = lambda t: t.transpose(0, 2, 1)
    big1 = _adamw_big(parts1, slab([conv_w_in, conv_w_out], 0, F32), slab([m_conv_w_in, m_conv_w_out], 0, F32),
                      slab([v_conv_w_in, v_conv_w_out], 0, F32), "adamw_conv")
    big2 = _adamw_big(parts2, slab([tr(attn_w_in), attn_w_out], PAD_ROWS_W2, F32),
                      slab([tr(m_attn_w_in), m_attn_w_out], PAD_ROWS_W2, F32),
                      slab([tr(v_attn_w_in), v_attn_w_out], PAD_ROWS_W2, F32), "adamw_attn")

    shard_rows = jnp.concatenate([dcw[0:3].reshape(3, NDEV, 128).transpose(1, 0, 2), dg2.reshape(NDEV, 1, 128),
                                  jnp.zeros((NDEV, 4, 128), F32)], axis=1).reshape(64, 128)
    small_g = jnp.concatenate([shard_rows, dg1.reshape(8, 128), dbf, dqg[0:1], dkg[0:1], jnp.zeros((5, 128), F32),
                               lossp.reshape(8, 128)], axis=0)
    gs, loss8 = _sum_parts(_all_gather(small_g, "gather_small_grads"), 80)
    loss = loss8[0, 0]
    mine = lax.dynamic_slice(gs, (8 * me, 0), (8, 128))
    g_small = jnp.concatenate([mine, gs[64:75], jnp.zeros((5, 128), F32)], axis=0)

    def pack_small(cwk, ang, cng, bfk, qgk, kgk):
        return jnp.concatenate([cwk[0], ang, jnp.zeros((4, 128), F32), cng.reshape(8, 128), _pad_lanes(bfk),
                                _pad_lanes(qgk), _pad_lanes(kgk), jnp.zeros((5, 128), F32)], axis=0)

    ds_, ms_, vs_ = _adamw_small(
        g_small, pack_small(conv_w, attn_norm_g, conv_norm_g, attn_b_f, attn_q_norm_g, attn_k_norm_g),
        pack_small(m_conv_w, m_attn_norm_g, m_conv_norm_g, m_attn_b_f, m_attn_q_norm_g, m_attn_k_norm_g),
        pack_small(v_conv_w, v_attn_norm_g, v_conv_norm_g, v_attn_b_f, v_attn_q_norm_g, v_attn_k_norm_g))

    def leaves(i, small):
        b1, b2 = big1[i], big2[i]
        return (small[8:16].reshape(1, D), b1[:ROWS_W1IN].reshape(1, D, 512), small[0:3].reshape(1, 3, 128),
                b1[ROWS_W1IN:].reshape(1, 128, D), small[3:4], tr(b2[:ROWS_W2IN].reshape(1, 514, D)),
                small[16:17, :H], small[17:18, :DH], small[18:19, :DH],
                b2[ROWS_W2IN:ROWS_W2IN + ROWS_WOUT].reshape(1, 128, D))

    return (loss, gx[None], *leaves(0, g_small), *leaves(1, ds_), *leaves(2, ms_), *leaves(3, vs_))
```

```python
import functools
import math

import jax
import jax.numpy as jnp
from jax import lax
from jax.experimental import pallas as pl
from jax.experimental.pallas import tpu as pltpu

F32 = jnp.float32
BF16 = jnp.bfloat16

D = 1024
H = 16
DH = 64
NDEV = 8
RMS_EPS = 1e-6
LANES = 128
HA = 128
TM_FWD = 512
TM_BWD = 256
TQ = 512
AUG_ROWS = 80
FWD_TILES = 8
BWD_TILES = 4
CH = 256
PACK_W = 512
ROWS_W1IN, ROWS_W2IN, ROWS_WOUT = 1024, 1028, 256
ADAM_BLOCKS = 4
PAD_ROWS2 = 60
ADAM_LR, ADAM_B1, ADAM_B2, ADAM_EPS, ADAM_WD, ADAM_STEP = 0.001, 0.9, 0.999, 1e-08, 0.01, 10
VMEM_LIMIT = 56 * 1024 * 1024
MASKED = -1e30
BOUNDED_SOFTMAX_REACH = 60.0
MESH = pl.DeviceIdType.MESH


def _params(*sem, vmem=VMEM_LIMIT):
    return pltpu.CompilerParams(dimension_semantics=sem or None, vmem_limit_bytes=vmem)


def _const(shape):
    nd = len(shape)
    return pl.BlockSpec(shape, lambda *_: (0,) * nd, pipeline_mode=pl.Buffered(1))


def _rows(tm, n, rev=False, steps=None):
    if rev:
        return pl.BlockSpec((tm, n), lambda i: (steps - 1 - i, 0))
    return pl.BlockSpec((tm, n), lambda i: (i, 0))


def _dot(a, b):
    return jnp.dot(a, b, preferred_element_type=F32)


def _top16(x):
    bits = lax.bitcast_convert_type(x, jnp.uint32) & jnp.uint32(0xFFFF0000)
    return lax.bitcast_convert_type(bits, F32)


def _split2(x):
    hi = _top16(x)
    return hi.astype(BF16), (x - hi).astype(BF16)


def _split3(x):
    hi = _top16(x)
    r = x - hi
    mid = _top16(r)
    return hi.astype(BF16), mid.astype(BF16), (r - mid).astype(BF16)


def _seg_sum(a, e):
    hi, lo = _split2(a)
    return _dot(hi, e) + _dot(lo, e)


def _seg_bcast(s, et2):
    return _dot(jnp.concatenate(_split2(s), axis=1), et2)


def _tri_sum(t, v):
    hi, mid, lo = _split3(v)
    return _dot(t, hi) + _dot(t, mid) + _dot(t, lo)


def _sigmoid(z):
    return 1.0 / (1.0 + jnp.exp(-z))


def _place():
    return lax.axis_index("x"), lax.axis_index("y"), lax.axis_index("c")


def _all_gather(xb, name):
    r, c_ = xb.shape

    def body(x_ref, out_ref, send_sems, recv_sems, local_sem):
        x, y, c = _place()
        me, sibling = (x, y, c), (x, y, 1 - c)
        chips = [(1 - x, y), (x, 1 - y), (1 - x, 1 - y)]

        def slab(px, py, pc):
            return out_ref.at[4 * px + 2 * py + pc]

        def copy(k, block, to, src=None):
            return pltpu.make_async_remote_copy(
                src_ref=slab(*block) if src is None else src, dst_ref=slab(*block),
                send_sem=send_sems.at[k], recv_sem=recv_sems.at[k], device_id=to, device_id_type=MESH)

        mine = pltpu.make_async_copy(x_ref, slab(*me), local_sem)
        mine.start()
        first = [copy(0, me, sibling, src=x_ref)]
        first += [copy(1 + j, me, (*chip, c), src=x_ref) for j, chip in enumerate(chips)]
        for cp in first:
            cp.start()
        passed = [copy(4 + j, (*chip, c), sibling) for j, chip in enumerate(chips)]
        for j, chip in enumerate(chips):
            copy(1 + j, (*chip, c), me).wait_recv()
            passed[j].start()
        copy(0, sibling, me).wait_recv()
        for j, chip in enumerate(chips):
            copy(4 + j, (*chip, 1 - c), me).wait_recv()
        for cp in first + passed:
            cp.wait_send()
        mine.wait()

    return pl.pallas_call(
        body, name=name,
        out_shape=jax.ShapeDtypeStruct((NDEV, r, c_), xb.dtype),
        in_specs=[pl.BlockSpec(memory_space=pl.ANY)],
        out_specs=pl.BlockSpec(memory_space=pl.ANY),
        scratch_shapes=[pltpu.SemaphoreType.DMA((7,)), pltpu.SemaphoreType.DMA((7,)), pltpu.SemaphoreType.DMA],
    )(xb)


def _block_copies(g_ref, out_ref, send_sems, recv_sems, local_sem):
    x, y, c = _place()
    copies = [pltpu.make_async_copy(g_ref.at[4 * x + 2 * y + c], out_ref.at[0], local_sem)]
    for k in range(1, NDEV):
        px = 1 - x if k & 4 else x
        py = 1 - y if k & 2 else y
        pc = 1 - c if k & 1 else c
        copies.append(pltpu.make_async_remote_copy(
            src_ref=g_ref.at[4 * px + 2 * py + pc], dst_ref=out_ref.at[k],
            send_sem=send_sems.at[k - 1], recv_sem=recv_sems.at[k - 1],
            device_id=(px, py, pc), device_id_type=MESH))
    return copies


def _gather_copies(x_ref, out_ref, send_sems, recv_sems, local_sem):
    x, y, c = _place()
    mine = out_ref.at[4 * x + 2 * y + c]
    copies = [pltpu.make_async_copy(x_ref, mine, local_sem)]
    for k in range(1, NDEV):
        peer = (1 - x if k & 4 else x, 1 - y if k & 2 else y, 1 - c if k & 1 else c)
        copies.append(pltpu.make_async_remote_copy(
            src_ref=x_ref, dst_ref=mine, send_sem=send_sems.at[k - 1], recv_sem=recv_sems.at[k - 1],
            device_id=peer, device_id_type=MESH))
    return copies


EXCHANGE_SEMS = [pltpu.SemaphoreType.DMA((7,)), pltpu.SemaphoreType.DMA((7,)), pltpu.SemaphoreType.DMA]


def _exchange_blocks(g, name):
    _, r, c_ = g.shape

    def body(g_ref, out_ref, send_sems, recv_sems, local_sem):
        copies = _block_copies(g_ref, out_ref, send_sems, recv_sems, local_sem)
        for cp in copies:
            cp.start()
        for cp in copies:
            cp.wait()

    return pl.pallas_call(
        body, name=name,
        out_shape=jax.ShapeDtypeStruct((NDEV, r, c_), g.dtype),
        in_specs=[pl.BlockSpec(memory_space=pl.ANY)],
        out_specs=pl.BlockSpec(memory_space=pl.ANY),
        scratch_shapes=list(EXCHANGE_SEMS),
    )(g)


def _conv_fwd(x, g1, w1in, cw, w1out, shard2):
    s = x.shape[0]
    tm = min(TM_FWD, s)
    steps = s // tm

    def body(x_ref, g_ref, win_ref, cw_ref, wout_ref, sh_ref, x1_ref, h1_ref, p1_ref, yg_ref, all_ref, tail_ref,
             send_sems, recv_sems, local_sem):
        @pl.when(pl.program_id(0) == 0)
        def _():
            tail_ref[...] = jnp.zeros_like(tail_ref)
            for cp in _gather_copies(sh_ref, all_ref, send_sems, recv_sems, local_sem):
                cp.start()

        xv = x_ref[...]
        inv = lax.rsqrt(jnp.mean(xv * xv, axis=-1, keepdims=True) + RMS_EPS)
        h = (xv * inv * g_ref[...]).astype(BF16)
        h1_ref[...] = h
        row = lax.broadcasted_iota(jnp.int32, (tm, CH), 0)
        for ci in range(D // CH):
            lo, hi = ci * CH, (ci + 1) * CH
            parts = []
            for k in range(4):
                pk = _dot(h, win_ref[:, k * D + lo:k * D + hi]).astype(BF16)
                p1_ref[:, k * D + lo:k * D + hi] = pk
                parts.append(pk.astype(F32))
            b, c, xin, z = parts
            u = c * xin
            t6 = tail_ref[6:7, lo:hi]
            t7 = tail_ref[7:8, lo:hi]
            u1 = jnp.where(row == 0, t7, pltpu.roll(u, 1, 0))
            u2 = jnp.where(row == 0, t6, jnp.where(row == 1, t7, pltpu.roll(u, 2, 0)))
            tail_ref[:, lo:hi] = u[tm - 8:, :]
            y = cw_ref[2:3, lo:hi] * u + cw_ref[1:2, lo:hi] * u1 + cw_ref[0:1, lo:hi] * u2
            yg_ref[:, lo:hi] = (b * y * (z * _sigmoid(z))).astype(BF16)
        x1_ref[...] = xv + _dot(yg_ref[...], wout_ref[...])

        @pl.when(pl.program_id(0) == steps - 1)
        def _():
            for cp in _gather_copies(sh_ref, all_ref, send_sems, recv_sems, local_sem):
                cp.wait()

    return pl.pallas_call(
        body, name="conv_fwd", grid=(steps,),
        in_specs=[_rows(tm, D), _const((1, D)), _const((D, 4 * D)), _const((8, D)), _const((D, D)),
                  pl.BlockSpec(memory_space=pl.ANY)],
        out_specs=[_rows(tm, D), _rows(tm, D), _rows(tm, 4 * D), _rows(tm, D), pl.BlockSpec(memory_space=pl.ANY)],
        out_shape=[jax.ShapeDtypeStruct((s, D), F32), jax.ShapeDtypeStruct((s, D), BF16),
                   jax.ShapeDtypeStruct((s, 4 * D), BF16), jax.ShapeDtypeStruct((s, D), BF16),
                   jax.ShapeDtypeStruct((NDEV,) + shard2.shape, shard2.dtype)],
        scratch_shapes=[pltpu.VMEM((8, D), F32)] + list(EXCHANGE_SEMS),
        compiler_params=_params("arbitrary"),
    )(x, g1, w1in, cw, w1out, shard2)


COL_BIAS = DH
ROW_BIAS = DH + 3


def _terms_cat(val):
    return jnp.concatenate(_split3(val), axis=1)


def _split_heads(x, aug, out_ref, lane, t_ref=None):
    for hp in range(H // 2):
        pair = x[:, hp * HA:(hp + 1) * HA]
        for k, feat in enumerate((pair, pltpu.roll(pair, DH, 1))):
            block = jnp.where(lane < DH, feat, aug(2 * hp + k))
            out_ref[2 * hp + k] = block.astype(BF16)
            if t_ref is not None:
                t_ref[2 * hp + k] = block.T[0:t_ref.shape[1], :].astype(BF16)


def _ones_at(lane, first):
    return jnp.where((lane >= first) & (lane < first + 3), 1.0, 0.0)


def _attn_proj_fwd(x1, g2, w2in, wf, bf, qg, kg, shift, e, et, tril, place_k, place_q, group):
    s = x1.shape[0]
    tm = min(TM_BWD, s)
    steps = s // tm
    per = group // tm

    def body(x_ref, g_ref, win_ref, wf_ref, bf_ref, qg_ref, kg_ref, sh_ref, e_ref, et_ref, tril_ref, pk_ref, pq_ref,
             h2_ref, qp_ref, kp_ref, z_ref, qa_ref, ka_ref, va_ref, vat_ref, iq_ref, ik_ref, c_ref, fl_ref, carry_ref):
        @pl.when(pl.program_id(0) == 0)
        def _():
            carry_ref[...] = jnp.zeros_like(carry_ref)

        xv = x_ref[...]
        inv = lax.rsqrt(jnp.mean(xv * xv, axis=-1, keepdims=True) + RMS_EPS)
        h = (xv * inv * g_ref[...]).astype(BF16)
        h2_ref[...] = h

        fl = _dot(h, wf_ref[...]) + bf_ref[...]
        fl_ref[...] = fl
        ex = jnp.exp(-jnp.abs(fl))
        up = 1.0 + ex
        log1p = jnp.where(up == 1.0, ex, jnp.log(up) * ex / (up - 1.0))
        lane = lax.broadcasted_iota(jnp.int32, (tm, LANES), 1)
        logf = jnp.where(lane < H, jnp.minimum(fl, 0.0) - log1p, 0.0)
        c_ref[...] = _tri_sum(tril_ref[...], logf) + carry_ref[0:1, :]
        carry_ref[0:1, :] = c_ref[tm - 1:tm, :]
        c = c_ref[...]

        def normed(col, pre_ref, inv_ref, gain_ref, scale):
            pre = _dot(h, win_ref[:, col * D:(col + 1) * D]).astype(BF16)
            pre_ref[...] = pre
            t = pre.astype(F32)
            invh = lax.rsqrt(_seg_sum(t * t, e_ref[...]) * (1.0 / DH) + RMS_EPS)
            inv_ref[...] = invh
            return t * _seg_bcast(invh, et_ref[...]) * (gain_ref[...] * scale)

        ones_col, ones_row = _ones_at(lane, COL_BIAS), _ones_at(lane, ROW_BIAS)
        q_bias = _dot(_terms_cat(c - sh_ref[...]), pq_ref[...])
        _split_heads(normed(0, qp_ref, iq_ref, qg_ref, 1.0 / math.sqrt(DH)),
                     lambda hh: q_bias[:, hh * HA:(hh + 1) * HA] + ones_col, qa_ref, lane)
        k_bias = _dot(_terms_cat(-c), pk_ref[...])
        _split_heads(normed(1, kp_ref, ik_ref, kg_ref, 1.0),
                     lambda hh: k_bias[:, hh * HA:(hh + 1) * HA] + ones_row, ka_ref, lane)
        v = _dot(h, win_ref[:, 2 * D:3 * D])
        _split_heads(v, lambda hh: ones_col, va_ref, lane, vat_ref)
        z_ref[...] = _dot(h, win_ref[:, 3 * D:4 * D]).astype(BF16)

    row_bf = lambda: _rows(tm, D)
    row_sm = lambda: _rows(tm, LANES)
    heads = lambda: pl.BlockSpec((H, tm, HA), lambda i: (0, i, 0))
    return pl.pallas_call(
        body, name="attn_proj_fwd", grid=(steps,),
        in_specs=[_rows(tm, D), _const((1, D)), _const((D, 4 * D)), _const((D, LANES)), _const((1, LANES)),
                  _const((1, D)), _const((1, D)), _const((1, LANES)), _const((D, LANES)), _const((2 * LANES, D)),
                  _const((tm, tm)), _const((3 * LANES, H * HA)), _const((3 * LANES, H * HA))],
        out_specs=[row_bf() for _ in range(4)] + [heads() for _ in range(3)] + [
            pl.BlockSpec((H, None, AUG_ROWS, tm), lambda i: (0, i // per, 0, i % per))] + [row_sm() for _ in range(4)],
        out_shape=[jax.ShapeDtypeStruct((s, D), BF16)] * 4 + [jax.ShapeDtypeStruct((H, s, HA), BF16)] * 3 + [
            jax.ShapeDtypeStruct((H, s // group, AUG_ROWS, group), BF16)] + [jax.ShapeDtypeStruct((s, LANES), F32)] * 4,
        scratch_shapes=[pltpu.VMEM((8, LANES), F32)],
        compiler_params=_params("arbitrary"),
    )(x1, g2, w2in, wf, bf, qg, kg, shift, e, et, tril, place_k, place_q)


def _dot_nt(a, b):
    return lax.dot_general(a, b, (((1,), (1,)), ((), ())), preferred_element_type=F32)


def _dot_tn(a, b):
    return lax.dot_general(a, b, (((0,), (0,)), ((), ())), preferred_element_type=F32)


def _flash_fwd(bounded, qa, ka, vat):
    h_, nq, tq, _ = qa.shape
    s, tk = ka.shape[1], tq
    group = vat.shape[3]
    gt = group // tk
    shift = gt.bit_length() - 1

    def body(flag_ref, q_ref, k_ref, vt_ref, o_ref, qb_ref, m_ref, acct_ref):
        i = pl.program_id(1)
        lane = lax.broadcasted_iota(jnp.int32, (tq, HA), 1)
        in_bias = (lane >= ROW_BIAS) & (lane < ROW_BIAS + 3)
        acct_ref[...] = jnp.zeros_like(acct_ref)
        g = lax.shift_right_logical(i, shift)
        r = i & (gt - 1)

        def run(use_bound):
            q = q_ref[...]
            if not use_bound:
                q = jnp.where(in_bias, jnp.zeros_like(q), q)
                m_ref[...] = jnp.full_like(m_ref, MASKED)

            def step(first_key, vt, tiles, diagonal_at):
                keys = pl.ds(pl.multiple_of(first_key, tk), tiles * tk)
                zt = _dot_nt(k_ref[keys, :], q)
                if diagonal_at is not None:
                    key = lax.broadcasted_iota(jnp.int32, (tiles * tk, tq), 0)
                    qry = lax.broadcasted_iota(jnp.int32, (tiles * tk, tq), 1)
                    zt = jnp.where(key <= qry + diagonal_at * tk, zt, MASKED)
                if use_bound:
                    acct_ref[0:AUG_ROWS, :] += _dot(vt, jnp.exp(zt).astype(BF16))
                else:
                    m_old = m_ref[...]
                    m_new = jnp.maximum(m_old, jnp.max(zt, axis=0, keepdims=True))
                    pt = jnp.exp(zt - m_new)
                    acct_ref[0:AUG_ROWS, :] = jnp.exp(m_old - m_new) * acct_ref[0:AUG_ROWS, :] + _dot(
                        vt, pt.astype(BF16))
                    m_ref[...] = m_new

            def whole_group(jj, carry):
                step(jj * group, vt_ref[jj], gt, None)
                return carry

            lax.fori_loop(0, g, whole_group, 0)
            for rr in range(gt):
                pl.when(r == rr)(functools.partial(
                    lambda rr: step(g * group, vt_ref[g, :, 0:(rr + 1) * tk], rr + 1, rr), rr))

            if not use_bound:
                acct_ref[COL_BIAS + 1:COL_BIAS + 2, :] = m_ref[...]
            acc = acct_ref[...].T
            l = jnp.sum(jnp.where(lane == COL_BIAS, acc, 0.0), axis=1, keepdims=True)
            if use_bound:
                m = -jnp.sum(jnp.where(in_bias, q.astype(F32), 0.0), axis=1, keepdims=True)
            else:
                m = jnp.sum(jnp.where(lane == COL_BIAS + 1, acc, 0.0), axis=1, keepdims=True)
            lse = m + jnp.log(l)
            o_ref[...] = jnp.where(lane < DH, acc / l, lse)
            hi, mid, lo = _split3(-lse)
            qb_ref[...] = jnp.where(lane == ROW_BIAS, hi, jnp.where(lane == ROW_BIAS + 1, mid, jnp.where(
                lane == ROW_BIAS + 2, lo, q_ref[...])))

        use_bound = flag_ref[0, 0] > 0.5
        pl.when(use_bound)(lambda: run(True))
        pl.when(jnp.logical_not(use_bound))(lambda: run(False))

    tile = lambda: pl.BlockSpec((None, None, tq, HA), lambda h, i: (h, i, 0, 0))
    return pl.pallas_call(
        body, name="flash_fwd", grid=(h_, nq),
        in_specs=[pl.BlockSpec(memory_space=pltpu.SMEM), tile(), pl.BlockSpec((None, s, HA), lambda h, i: (h, 0, 0)),
                  pl.BlockSpec((None, s // group, AUG_ROWS, group), lambda h, i: (h, 0, 0, 0))],
        out_specs=[tile(), tile()],
        out_shape=[jax.ShapeDtypeStruct((h_, nq, tq, HA), F32), jax.ShapeDtypeStruct((h_, nq, tq, HA), BF16)],
        scratch_shapes=[pltpu.VMEM((1, tq), F32), pltpu.VMEM((HA, tq), F32)],
        compiler_params=_params("arbitrary", "arbitrary"),
    )(bounded, qa, ka, vat)


def _merge_heads(src, out_ref, lane):
    for hp in range(H // 2):
        out_ref[:, hp * HA:(hp + 1) * HA] = jnp.where(lane < DH, src(2 * hp), pltpu.roll(src(2 * hp + 1), DH, 1))


def _attn_out(ol, z, x1, tgt, w2out, w2out_t, e, place_k):
    s = z.shape[0]
    tm = min(TM_BWD, s)
    steps = s // tm

    def body(ol_ref, z_ref, x1_ref, t_ref, w_ref, wt_ref, e_ref, pk_ref, og_ref, dx2_ref, dz_ref, doa_ref, loss_ref,
             o_ref):
        @pl.when(pl.program_id(0) == 0)
        def _():
            loss_ref[...] = jnp.zeros_like(loss_ref)

        lane = lax.broadcasted_iota(jnp.int32, (tm, HA), 1)
        _merge_heads(lambda hh: ol_ref[hh], o_ref, lane)
        ov = o_ref[...]
        zv = z_ref[...].astype(F32)
        sg = _sigmoid(zv)
        sil = zv * sg
        og = (ov * sil).astype(BF16)
        og_ref[...] = og
        err = x1_ref[...] + _dot(og, w_ref[...]) - t_ref[...]
        loss_ref[...] += (0.5 / D) * jnp.sum(err * err, axis=0, keepdims=True)
        dx2 = err * (1.0 / D)
        dx2_ref[...] = dx2
        dog = _dot(dx2.astype(BF16), wt_ref[...])
        do = (dog * sil).astype(BF16).astype(F32)
        dz_ref[...] = (dog * ov * (sg * (1.0 + zv * (1.0 - sg)))).astype(BF16)
        delta = _seg_sum(do * ov, e_ref[...])
        d_bias = _dot(_terms_cat(-delta), pk_ref[...])
        _split_heads(do, lambda hh: d_bias[:, hh * HA:(hh + 1) * HA], doa_ref, lane)

    heads = lambda: pl.BlockSpec((H, tm, HA), lambda i: (0, i, 0))
    return pl.pallas_call(
        body, name="attn_out", grid=(steps,),
        in_specs=[heads(), _rows(tm, D), _rows(tm, D), _rows(tm, D), _const((D, D)), _const((D, D)),
                  _const((D, LANES)), _const((3 * LANES, H * HA))],
        out_specs=[_rows(tm, D), _rows(tm, D), _rows(tm, D), heads(), pl.BlockSpec((1, D), lambda i: (0, 0))],
        out_shape=[jax.ShapeDtypeStruct((s, D), BF16), jax.ShapeDtypeStruct((s, D), F32),
                   jax.ShapeDtypeStruct((s, D), BF16), jax.ShapeDtypeStruct((H, s, HA), BF16),
                   jax.ShapeDtypeStruct((1, D), F32)],
        scratch_shapes=[pltpu.VMEM((tm, D), F32)],
        compiler_params=_params("arbitrary"),
    )(ol, z, x1, tgt, w2out, w2out_t, e, place_k)


def _flash_bwd(qab, doa, ka, va):
    h_, nq, tq, _ = qab.shape
    nk, tk = ka.shape[1], ka.shape[2]
    s = nk * tk
    gt = min(BWD_TILES, nq)
    shift = gt.bit_length() - 1

    def body(qa_ref, da_ref, ka_ref, va_ref, dq_hbm, dk_ref, dv_ref, dq_acc, dkt_ref, dvt_ref, sem):
        hh = pl.program_id(0)
        j = pl.program_id(1)

        @pl.when(j == 0)
        def _():
            dq_acc[...] = jnp.zeros_like(dq_acc)

        dkt_ref[...] = jnp.zeros_like(dkt_ref)
        dvt_ref[...] = jnp.zeros_like(dvt_ref)
        ka_v = ka_ref[...]
        va_v = va_ref[...]
        kat = ka_v.astype(F32).T.astype(BF16)

        def step(i, tiles, masked, add_dqt):
            rows = pl.ds(pl.multiple_of(i * tq, tq), tiles * tq)
            qa_i, da_i = qa_ref[rows, :], da_ref[rows, :]
            zz = _dot_nt(qa_i, ka_v)
            if masked:
                row = lax.broadcasted_iota(jnp.int32, (tiles * tq, tk), 0)
                col = lax.broadcasted_iota(jnp.int32, (tiles * tq, tk), 1)
                zz = jnp.where(col <= row, zz, MASKED)
            p = jnp.exp(zz)
            ds = (p * _dot_nt(da_i, va_v)).astype(BF16)
            pb = p.astype(BF16)
            add_dqt(_dot_nt(kat, ds))
            dvt_ref[...] += _dot_tn(da_i, pb)
            dkt_ref[...] += _dot_tn(qa_i, ds)

        g0 = lax.shift_right_logical(j, shift)
        for rr in range(gt):
            def from_diagonal(rr=rr):
                def add(x):
                    dq_acc[g0, :, rr * tq:gt * tq] += x

                step(j, gt - rr, True, add)

            pl.when((j & (gt - 1)) == rr)(from_diagonal)

        def whole_group(g, carry):
            def add(x):
                dq_acc[g] += x

            step(g * gt, gt, False, add)
            return carry

        lax.fori_loop(g0 + 1, nq // gt, whole_group, 0)
        dk_ref[...] = dkt_ref[...].T
        dv_ref[...] = dvt_ref[...].T.astype(BF16)

        @pl.when(j == nk - 1)
        def _():
            cp = pltpu.make_async_copy(dq_acc, dq_hbm.at[hh], sem)
            cp.start()
            cp.wait()

    whole = lambda: pl.BlockSpec((None, nq * tq, HA), lambda h, j: (h, 0, 0))
    tile = lambda: pl.BlockSpec((None, None, tk, HA), lambda h, j: (h, j, 0, 0))
    rows = lambda: pl.BlockSpec((None, tk, HA), lambda h, j: (h, j, 0))
    return pl.pallas_call(
        body, name="flash_bwd", grid=(h_, nk),
        in_specs=[whole(), whole(), tile(), tile()],
        out_specs=[pl.BlockSpec(memory_space=pl.ANY), rows(), rows()],
        out_shape=[jax.ShapeDtypeStruct((h_, nq // gt, HA, gt * tq), F32), jax.ShapeDtypeStruct((h_, s, HA), F32),
                   jax.ShapeDtypeStruct((h_, s, HA), BF16)],
        scratch_shapes=[pltpu.VMEM((nq // gt, HA, gt * tq), F32), pltpu.VMEM((HA, tk), F32), pltpu.VMEM((HA, tk), F32),
                        pltpu.SemaphoreType.DMA],
        compiler_params=_params("arbitrary", "arbitrary"),
    )(qab.reshape(h_, s, HA), doa.reshape(h_, s, HA), ka, va)


def _attn_proj_bwd(dq, dk, dv, dz, qpre, kpre, invq, invk, fl, x1, dx2, g2, qg, kg, w2in_t, wf_t, e, et,
                   triu, fold):
    s = x1.shape[0]
    tm = min(TM_BWD, s)
    steps = s // tm
    per = dq.shape[3] // tm

    def body(dq_ref, dk_ref, dvh_ref, dz_ref, qp_ref, kp_ref, iq_ref, ik_ref, fl_ref, x1_ref, dx2_ref,
             g2_ref, qg_ref, kg_ref, wt_ref, wft_ref, e_ref, et_ref, triu_ref, fold_ref,
             dx1_ref, dp_ref, df_ref, dg2_ref, dqg_ref, dkg_ref, dbf_ref, carry_ref, rc_ref, qcol_ref, kcol_ref,
             dqs_ref, dkn_ref, dv_ref):
        step = pl.program_id(0)
        lane = lax.broadcasted_iota(jnp.int32, (tm, HA), 1)
        _merge_heads(lambda hh: dvh_ref[hh].astype(F32), dv_ref, lane)
        dc = jnp.zeros((tm, HA), F32)
        for hp in range(H // 2):
            dqh = [dq_ref[2 * hp].T, dq_ref[2 * hp + 1].T]
            dkh = [dk_ref[2 * hp], dk_ref[2 * hp + 1]]
            dqs_ref[:, hp * HA:(hp + 1) * HA] = jnp.where(lane < DH, dqh[0], pltpu.roll(dqh[1], DH, 1))
            dkn_ref[:, hp * HA:(hp + 1) * HA] = jnp.where(lane < DH, dkh[0], pltpu.roll(dkh[1], DH, 1))
            for k in range(2):
                sums = jnp.where(lane == ROW_BIAS, dqh[k], 0.0) - pltpu.roll(
                    jnp.where(lane == COL_BIAS, dkh[k], 0.0), ROW_BIAS - COL_BIAS, 1)
                dc = dc + pltpu.roll(sums, (2 * hp + k - ROW_BIAS) % HA, 1)

        @pl.when(step == 0)
        def _():
            carry_ref[...] = jnp.zeros_like(carry_ref)
            dg2_ref[...] = jnp.zeros_like(dg2_ref)
            dbf_ref[...] = jnp.zeros_like(dbf_ref)
            qcol_ref[...] = jnp.zeros_like(qcol_ref)
            kcol_ref[...] = jnp.zeros_like(kcol_ref)
            dqg_ref[...] = jnp.zeros_like(dqg_ref)
            dkg_ref[...] = jnp.zeros_like(dkg_ref)

        rc_ref[...] = _tri_sum(triu_ref[...], dc) + carry_ref[0:1, :]
        carry_ref[0:1, :] = rc_ref[0:1, :]
        df = rc_ref[...] * _sigmoid(-fl_ref[...])
        dfb = df.astype(BF16)
        df_ref[...] = dfb
        dbf_ref[...] += jnp.sum(df, axis=0, keepdims=True)

        def norm_bwd(dn, pre_ref, inv_ref, gain_ref, col_ref):
            pre = pre_ref[...].astype(F32)
            invh = inv_ref[...]
            invb = _seg_bcast(invh, et_ref[...])
            col_ref[...] += jnp.sum(dn * pre * invb, axis=0, keepdims=True)
            gd = dn * gain_ref[...]
            mean = _seg_sum(gd * pre, e_ref[...]) * (1.0 / DH)
            return invb * gd - pre * _seg_bcast(mean * invh * invh * invh, et_ref[...])

        dq = norm_bwd(dqs_ref[...] * (1.0 / math.sqrt(DH)), qp_ref, iq_ref, qg_ref, qcol_ref).astype(BF16)
        dp_ref[:, 0:D] = dq
        dh = _dot(dq, wt_ref[0:D, :])
        dk = norm_bwd(dkn_ref[...], kp_ref, ik_ref, kg_ref, kcol_ref).astype(BF16)
        dp_ref[:, D:2 * D] = dk
        dh += _dot(dk, wt_ref[D:2 * D, :])
        dvb = dv_ref[...].astype(BF16)
        dp_ref[:, 2 * D:3 * D] = dvb
        dh += _dot(dvb, wt_ref[2 * D:3 * D, :])
        dzb = dz_ref[...]
        dp_ref[:, 3 * D:4 * D] = dzb
        dh += _dot(dzb, wt_ref[3 * D:4 * D, :])
        dh += _dot(dfb, wft_ref[...])

        xv = x1_ref[...]
        inv = lax.rsqrt(jnp.mean(xv * xv, axis=-1, keepdims=True) + RMS_EPS)
        dg2_ref[...] += jnp.sum(dh * xv * inv, axis=0, keepdims=True)
        gh = dh * g2_ref[...]
        dx1_ref[...] = dx2_ref[...] + inv * gh - xv * (inv * inv * inv * jnp.mean(gh * xv, axis=-1, keepdims=True))

        @pl.when(step == steps - 1)
        def _():
            dqg_ref[...] = _fold_heads(qcol_ref[...], fold_ref[...])
            dkg_ref[...] = _fold_heads(kcol_ref[...], fold_ref[...])

    rr = lambda n: _rows(tm, n, rev=True, steps=steps)
    acc = lambda n, r=1: pl.BlockSpec((r, n), lambda i: (0, 0))
    heads = lambda: pl.BlockSpec((H, tm, HA), lambda i: (0, steps - 1 - i, 0))
    return pl.pallas_call(
        body, name="attn_proj_bwd", grid=(steps,),
        in_specs=[pl.BlockSpec((H, None, HA, tm), lambda i: (0, (steps - 1 - i) // per, 0, (steps - 1 - i) % per)),
                  heads(), heads(),
                  rr(D), rr(D), rr(D), rr(LANES), rr(LANES), rr(LANES), rr(D), rr(D),
                  _const((1, D)), _const((1, D)), _const((1, D)), _const((4 * D, D)), _const((LANES, D)),
                  _const((D, LANES)), _const((2 * LANES, D)), _const((tm, tm)), _const((D, LANES))],
        out_specs=[rr(D), rr(4 * D), rr(LANES), acc(D), acc(LANES, 8), acc(LANES, 8), acc(LANES)],
        out_shape=[jax.ShapeDtypeStruct((s, D), F32), jax.ShapeDtypeStruct((s, 4 * D), BF16),
                   jax.ShapeDtypeStruct((s, LANES), BF16), jax.ShapeDtypeStruct((1, D), F32),
                   jax.ShapeDtypeStruct((8, LANES), F32), jax.ShapeDtypeStruct((8, LANES), F32),
                   jax.ShapeDtypeStruct((1, LANES), F32)],
        scratch_shapes=[pltpu.VMEM((8, LANES), F32), pltpu.VMEM((tm, LANES), F32), pltpu.VMEM((1, D), F32),
                        pltpu.VMEM((1, D), F32), pltpu.VMEM((tm, D), F32), pltpu.VMEM((tm, D), F32),
                        pltpu.VMEM((tm, D), F32)],
        compiler_params=_params("arbitrary"),
    )(dq, dk, dv, dz, qpre, kpre, invq, invk, fl, x1, dx2, g2, qg, kg, w2in_t, wf_t, e, et, triu, fold)


def _fold_heads(col, fold):
    hi, mid, lo = _split3(jnp.broadcast_to(col, (8, D)))
    return _dot(hi, fold) + _dot(mid, fold) + _dot(lo, fold)


def _conv_bwd(dx1, p1, x, g1, cw, w1out_t, w1in_t, grads2):
    s = x.shape[0]
    tm = min(TM_BWD, s)
    steps = s // tm
    halo = tm // 8

    def body(dx1_ref, b_ref, c_ref, xi_ref, z_ref, ch_ref, xh_ref, x_ref, g_ref, cw_ref, wot_ref, wit_ref, g2_ref,
             gx_ref, dp_ref, dcw_ref, dg1_ref, parts_ref, head_ref, send_sems, recv_sems, local_sem):
        step = pl.program_id(0)

        @pl.when(step == 0)
        def _():
            for cp in _block_copies(g2_ref, parts_ref, send_sems, recv_sems, local_sem):
                cp.start()

        @pl.when(step == 0)
        def _():
            head_ref[...] = jnp.zeros_like(head_ref)
            dcw_ref[...] = jnp.zeros_like(dcw_ref)
            dg1_ref[...] = jnp.zeros_like(dg1_ref)

        first_tile = step == steps - 1
        dx1 = dx1_ref[...]
        row = lax.broadcasted_iota(jnp.int32, (tm, CH), 0)
        dyb = dx1.astype(BF16)
        for ci in range(D // CH):
            lo, hi = ci * CH, (ci + 1) * CH
            dyg = _dot(dyb, wot_ref[:, lo:hi])
            b = b_ref[:, lo:hi].astype(F32)
            c = c_ref[:, lo:hi].astype(F32)
            xin = xi_ref[:, lo:hi].astype(F32)
            z = z_ref[:, lo:hi].astype(F32)
            u = c * xin
            t6 = jnp.where(first_tile, 0.0, ch_ref[6:7, lo:hi].astype(F32) * xh_ref[6:7, lo:hi].astype(F32))
            t7 = jnp.where(first_tile, 0.0, ch_ref[7:8, lo:hi].astype(F32) * xh_ref[7:8, lo:hi].astype(F32))
            u1 = jnp.where(row == 0, t7, pltpu.roll(u, 1, 0))
            u2 = jnp.where(row == 0, t6, jnp.where(row == 1, t7, pltpu.roll(u, 2, 0)))
            w0, w1, w2 = cw_ref[0:1, lo:hi], cw_ref[1:2, lo:hi], cw_ref[2:3, lo:hi]
            y = w2 * u + w1 * u1 + w0 * u2
            sg = _sigmoid(z)
            sil = z * sg
            dp_ref[:, lo:hi] = (dyg * y * sil).astype(BF16)
            dy = dyg * b * sil
            dp_ref[:, 3 * D + lo:3 * D + hi] = (dyg * b * y * (sg * (1.0 + z * (1.0 - sg)))).astype(BF16)
            dcw_ref[2:3, lo:hi] += jnp.sum(dy * u, axis=0, keepdims=True)
            dcw_ref[1:2, lo:hi] += jnp.sum(dy * u1, axis=0, keepdims=True)
            dcw_ref[0:1, lo:hi] += jnp.sum(dy * u2, axis=0, keepdims=True)
            n0 = head_ref[0:1, lo:hi]
            n1 = head_ref[1:2, lo:hi]
            dyn1 = jnp.where(row == tm - 1, n0, pltpu.roll(dy, tm - 1, 0))
            dyn2 = jnp.where(row == tm - 2, n0, jnp.where(row == tm - 1, n1, pltpu.roll(dy, tm - 2, 0)))
            head_ref[:, lo:hi] = dy[0:8, :]
            du = w2 * dy + w1 * dyn1 + w0 * dyn2
            dp_ref[:, D + lo:D + hi] = (du * xin).astype(BF16)
            dp_ref[:, 2 * D + lo:2 * D + hi] = (du * c).astype(BF16)
        dh = _dot(dp_ref[:, 0:D], wit_ref[0:D, :])
        for k in range(1, 4):
            dh += _dot(dp_ref[:, k * D:(k + 1) * D], wit_ref[k * D:(k + 1) * D, :])
        xv = x_ref[...]
        inv = lax.rsqrt(jnp.mean(xv * xv, axis=-1, keepdims=True) + RMS_EPS)
        dg1_ref[...] += jnp.sum(dh * xv * inv, axis=0, keepdims=True)
        gh = dh * g_ref[...]
        gx_ref[...] = dx1 + inv * gh - xv * (inv * inv * inv * jnp.mean(gh * xv, axis=-1, keepdims=True))

        @pl.when(step == steps - 1)
        def _():
            for cp in _block_copies(g2_ref, parts_ref, send_sems, recv_sems, local_sem):
                cp.wait()

    rr = lambda n: _rows(tm, n, rev=True, steps=steps)
    part = lambda k: pl.BlockSpec((tm, D), lambda i: (steps - 1 - i, k))
    prev8 = lambda k: pl.BlockSpec((8, D), lambda i: (jnp.maximum((steps - 1 - i) * halo - 1, 0), k))
    return pl.pallas_call(
        body, name="conv_bwd", grid=(steps,),
        in_specs=[rr(D), part(0), part(1), part(2), part(3), prev8(1), prev8(2), rr(D), _const((1, D)),
                  _const((8, D)), _const((D, D)), _const((4 * D, D)), pl.BlockSpec(memory_space=pl.ANY)],
        out_specs=[rr(D), rr(4 * D), pl.BlockSpec((8, D), lambda i: (0, 0)), pl.BlockSpec((1, D), lambda i: (0, 0)),
                   pl.BlockSpec(memory_space=pl.ANY)],
        out_shape=[jax.ShapeDtypeStruct((s, D), F32), jax.ShapeDtypeStruct((s, 4 * D), BF16),
                   jax.ShapeDtypeStruct((8, D), F32), jax.ShapeDtypeStruct((1, D), F32),
                   jax.ShapeDtypeStruct(grads2.shape, grads2.dtype)],
        scratch_shapes=[pltpu.VMEM((8, D), F32)] + list(EXCHANGE_SEMS),
        compiler_params=_params("arbitrary"),
    )(dx1, p1, p1, p1, p1, p1, p1, x, g1, cw, w1out_t, w1in_t, grads2)


def _wgrad(a, g, name):
    s, k = a.shape
    n = g.shape[1]
    bn = min(n, 1024)
    ts = min(1024, s)

    def body(a_ref, g_ref, out_ref, acc_ref):
        t = pl.program_id(1)

        @pl.when(t == 0)
        def _():
            acc_ref[...] = jnp.zeros_like(acc_ref)

        acc_ref[...] += _dot_tn(a_ref[...], g_ref[...].astype(BF16))

        @pl.when(t == s // ts - 1)
        def _():
            out_ref[...] = acc_ref[...].astype(BF16)

    return pl.pallas_call(
        body, name=name, grid=(n // bn, s // ts),
        in_specs=[pl.BlockSpec((ts, k), lambda j, t: (t, 0)), pl.BlockSpec((ts, bn), lambda j, t: (t, j))],
        out_specs=pl.BlockSpec((k, bn), lambda j, t: (0, j)),
        out_shape=jax.ShapeDtypeStruct((k, n), BF16),
        scratch_shapes=[pltpu.VMEM((k, bn), F32)],
        compiler_params=_params("arbitrary", "arbitrary"),
    )(a, g)


def _adamw_math(w, g, m, v):
    m = ADAM_B1 * m + (1.0 - ADAM_B1) * g
    v = ADAM_B2 * v + (1.0 - ADAM_B2) * (g * g)
    m_hat = m / (1.0 - ADAM_B1 ** ADAM_STEP)
    v_hat = v / (1.0 - ADAM_B2 ** ADAM_STEP)
    delta = -ADAM_LR * (m_hat / (jnp.sqrt(v_hat) + ADAM_EPS) + ADAM_WD * w)
    return delta, m, v


def _adamw_big(parts, w, m, v, name):
    _, r, c_ = parts.shape
    rb = r // ADAM_BLOCKS
    assert r % (16 * ADAM_BLOCKS) == 0

    def body(p_ref, w_ref, m_ref, v_ref, g_ref, d_ref, mo_ref, vo_ref):
        g = p_ref[0].astype(F32)
        for k in range(1, NDEV):
            g = g + p_ref[k].astype(F32)
        g_ref[...] = g
        d_ref[...], mo_ref[...], vo_ref[...] = _adamw_math(w_ref[...], g, m_ref[...], v_ref[...])

    blk = pl.BlockSpec((rb, c_), lambda i: (i, 0))
    return pl.pallas_call(
        body, name=name, grid=(r // rb,),
        in_specs=[pl.BlockSpec((NDEV, rb, c_), lambda i: (0, i, 0)), blk, blk, blk],
        out_specs=[blk, blk, blk, blk],
        out_shape=[jax.ShapeDtypeStruct((r, c_), F32)] * 4,
        compiler_params=_params("arbitrary"),
    )(parts, w, m, v)


def _sum_parts(parts, loss_row):
    _, r, c_ = parts.shape

    def body(p_ref, o_ref, l_ref):
        g = p_ref[0]
        for k in range(1, NDEV):
            g = g + p_ref[k]
        o_ref[...] = g
        per_row = jnp.sum(o_ref[loss_row:loss_row + 8, :], axis=1, keepdims=True)
        l_ref[...] = jnp.broadcast_to(jnp.sum(per_row, axis=0, keepdims=True), (8, c_))

    return pl.pallas_call(body, name="sum_small", out_shape=[jax.ShapeDtypeStruct((r, c_), F32),
                                                             jax.ShapeDtypeStruct((8, c_), F32)])(parts)


def _adamw_small(g, w, m, v):
    def body(g_ref, w_ref, m_ref, v_ref, d_ref, mo_ref, vo_ref):
        d_ref[...], mo_ref[...], vo_ref[...] = _adamw_math(w_ref[...], g_ref[...], m_ref[...], v_ref[...])

    return pl.pallas_call(body, name="adamw_small", out_shape=[jax.ShapeDtypeStruct(g.shape, F32)] * 3)(g, w, m, v)


def _pad_lanes(a):
    return jnp.pad(a, ((0, 0), (0, LANES - a.shape[1])))


def _tiles(a, t):
    return a.reshape(H, a.shape[1] // t, t, HA)


def kernel(x, conv_norm_g, conv_w_in, conv_w, conv_w_out, attn_norm_g, attn_w_in, attn_b_f, attn_q_norm_g, attn_k_norm_g, attn_w_out, loss_target, m_conv_norm_g, m_conv_w_in, m_conv_w, m_conv_w_out, m_attn_norm_g, m_attn_w_in, m_attn_b_f, m_attn_q_norm_g, m_attn_k_norm_g, m_attn_w_out, v_conv_norm_g, v_conv_w_in, v_conv_w, v_conv_w_out, v_attn_norm_g, v_attn_w_in, v_attn_b_f, v_attn_q_norm_g, v_attn_k_norm_g, v_attn_w_out):
    s = x.shape[1]
    tq = min(TQ, s)
    tmb = min(TM_BWD, s)
    me = 4 * lax.axis_index("x") + 2 * lax.axis_index("y") + lax.axis_index("c")
    xv, tgt = x[0], loss_target[0]

    def slab(t_in, t_out, pad, dtype):
        rows = [t_in[0].reshape(-1, PACK_W), t_out[0].reshape(ROWS_WOUT, PACK_W)]
        return jnp.concatenate(rows + ([jnp.zeros((pad, PACK_W), F32)] if pad else []), axis=0).astype(dtype)

    wg1 = _all_gather(slab(conv_w_in, conv_w_out, 0, BF16), "gather_weights")
    small_w = jnp.concatenate([conv_w[0], attn_norm_g, jnp.zeros((4, 128), F32)], axis=0)
    sg_ = _all_gather(small_w, "gather_small_weights")
    w1in = wg1[:, :ROWS_W1IN].transpose(1, 0, 2).reshape(D, 4 * D)
    w1out = wg1[:, ROWS_W1IN:].reshape(D, D)
    cw = jnp.concatenate([sg_[:, 0:3, :].transpose(1, 0, 2).reshape(3, D), jnp.zeros((5, D), F32)], axis=0)
    g2 = sg_[:, 3, :].reshape(1, D)
    qg_t, kg_t = jnp.tile(attn_q_norm_g, (1, H)), jnp.tile(attn_k_norm_g, (1, H))
    bf = _pad_lanes(attn_b_f)

    e = (jnp.arange(D)[:, None] // DH == jnp.arange(LANES)[None, :]).astype(BF16)
    et2 = jnp.concatenate([e.T, e.T], axis=0)
    fold = (jnp.arange(D)[:, None] % DH == jnp.arange(LANES)[None, :]).astype(BF16)
    tril = (jnp.arange(tmb)[:, None] >= jnp.arange(tmb)[None, :]).astype(BF16)
    triu = tril.T
    src = jnp.arange(3 * LANES)
    dst = jnp.arange(H * HA)[None, :] - (HA * (src % LANES) + src // LANES)[:, None]
    place_k = ((dst == COL_BIAS) & (src % LANES < H)[:, None]).astype(BF16)
    place_q = ((dst == ROW_BIAS) & (src % LANES < H)[:, None]).astype(BF16)

    x1, h1, p1, yg, wg2 = _conv_fwd(xv, conv_norm_g, w1in, cw, w1out, slab(attn_w_in, attn_w_out, PAD_ROWS2, BF16))
    w2all = wg2[:, :ROWS_W2IN].reshape(NDEV, D, 514).transpose(1, 0, 2).reshape(D, 4 * D + H)
    w2in, wf = w2all[:, :4 * D], _pad_lanes(w2all[:, 4 * D:])
    w2out = wg2[:, ROWS_W2IN:ROWS_W2IN + ROWS_WOUT].reshape(D, D)
    reach = 1.01 * math.sqrt(DH) * jnp.max(jnp.abs(attn_q_norm_g)) * jnp.max(jnp.abs(attn_k_norm_g))
    bounded = (2.0 * reach <= BOUNDED_SOFTMAX_REACH).astype(F32).reshape(1, 1)
    group = tq * min(FWD_TILES, s // tq)
    h2, qpre, kpre, z, qa, ka, va, vat, invq, invk, cc, fl = _attn_proj_fwd(
        x1, g2, w2in, wf, bf, qg_t, kg_t, jnp.broadcast_to(reach, (1, LANES)), e, et2, tril, place_k, place_q, group)
    ol, qab = _flash_fwd(bounded, _tiles(qa, tq), ka, vat)
    og, dx2, dz, doa, lossp = _attn_out(ol.reshape(H, s, HA), z, x1, tgt, w2out, w2out.T, e, place_k)

    dq, dk, dv = _flash_bwd(qab, _tiles(doa, tq), _tiles(ka, tq), _tiles(va, tq))
    dx1, dp2, df, dg2, dqg, dkg, dbf = _attn_proj_bwd(
        dq, dk, dv, dz, qpre, kpre, invq, invk, fl, x1, dx2, g2, qg_t, kg_t, w2in.T, wf.T, e, et2, triu, fold)
    dw2out = _wgrad(og, dx2, "wgrad_attn_out")
    dw2in = _wgrad(h2, dp2, "wgrad_attn_in")
    dwf = _wgrad(h2, df, "wgrad_attn_forget")
    dw2all = jnp.concatenate([dw2in, dwf[:, :H]], axis=1)
    grads2 = jnp.concatenate([
        dw2all.reshape(D, NDEV, 514).transpose(1, 0, 2).reshape(NDEV, ROWS_W2IN, PACK_W),
        dw2out.reshape(NDEV, ROWS_WOUT, PACK_W), jnp.zeros((NDEV, PAD_ROWS2, PACK_W), BF16)], axis=1)
    gx, dp1, dcw, dg1, parts2 = _conv_bwd(dx1, p1, xv, conv_norm_g, cw, w1out.T, w1in.T, grads2)
    dw1out = _wgrad(yg, dx1, "wgrad_conv_out")
    dw1in = _wgrad(h1, dp1, "wgrad_conv_in")
    grads1 = jnp.concatenate([dw1in.reshape(D, NDEV, 512).transpose(1, 0, 2),
                              dw1out.reshape(NDEV, ROWS_WOUT, PACK_W)], axis=1)
    parts1 = _exchange_blocks(grads1, "scatter_grads")

    big1 = _adamw_big(parts1, slab(conv_w_in, conv_w_out, 0, F32), slab(m_conv_w_in, m_conv_w_out, 0, F32),
                      slab(v_conv_w_in, v_conv_w_out, 0, F32), "adamw_conv")
    big2 = _adamw_big(parts2, slab(attn_w_in, attn_w_out, PAD_ROWS2, F32),
                      slab(m_attn_w_in, m_attn_w_out, PAD_ROWS2, F32),
                      slab(v_attn_w_in, v_attn_w_out, PAD_ROWS2, F32), "adamw_attn")

    shard_rows = jnp.concatenate([dcw[0:3].reshape(3, NDEV, 128).transpose(1, 0, 2), dg2.reshape(NDEV, 1, 128),
                                  jnp.zeros((NDEV, 4, 128), F32)], axis=1).reshape(64, 128)
    small_g = jnp.concatenate([shard_rows, dg1.reshape(8, 128), dbf, dqg[0:1], dkg[0:1], jnp.zeros((5, 128), F32),
                               lossp.reshape(8, 128)], axis=0)
    gs, loss8 = _sum_parts(_all_gather(small_g, "gather_small_grads"), 80)
    loss = loss8[0, 0]
    mine = lax.dynamic_slice(gs, (8 * me, 0), (8, 128))
    g_small = jnp.concatenate([mine, gs[64:75], jnp.zeros((5, 128), F32)], axis=0)

    def pack_small(cwk, ang, cng, bfk, qgk, kgk):
        return jnp.concatenate([cwk[0], ang, jnp.zeros((4, 128), F32), cng.reshape(8, 128), _pad_lanes(bfk),
                                _pad_lanes(qgk), _pad_lanes(kgk), jnp.zeros((5, 128), F32)], axis=0)

    ds_, ms_, vs_ = _adamw_small(
        g_small, pack_small(conv_w, attn_norm_g, conv_norm_g, attn_b_f, attn_q_norm_g, attn_k_norm_g),
        pack_small(m_conv_w, m_attn_norm_g, m_conv_norm_g, m_attn_b_f, m_attn_q_norm_g, m_attn_k_norm_g),
        pack_small(v_conv_w, v_attn_norm_g, v_conv_norm_g, v_attn_b_f, v_attn_q_norm_g, v_attn_k_norm_g))

    def leaves(i, small):
        b1, b2 = big1[i], big2[i]
        return (small[8:16].reshape(1, D), b1[:ROWS_W1IN].reshape(1, D, 512), small[0:3].reshape(1, 3, 128),
                b1[ROWS_W1IN:].reshape(1, 128, D), small[3:4], b2[:ROWS_W2IN].reshape(1, D, 514),
                small[16:17, :H], small[17:18, :DH], small[18:19, :DH],
                b2[ROWS_W2IN:ROWS_W2IN + ROWS_WOUT].reshape(1, 128, D))

    return (loss, gx[None], *leaves(0, g_small), *leaves(1, ds_), *leaves(2, ms_), *leaves(3, vs_))
```

```python
import functools
import math

import jax
import jax.numpy as jnp
from jax import lax
from jax.experimental import pallas as pl
from jax.experimental.pallas import tpu as pltpu

F32 = jnp.float32
BF16 = jnp.bfloat16

D = 1024
H = 16
DH = 64
NDEV = 8
RMS_EPS = 1e-6
LANES = 128
HA = 128
TM_FWD = 512
TM_BWD = 256
TQ = 512
AUG_ROWS = 80
FWD_TILES = 8
BWD_TILES = 4
CH = 256
PACK_W = 512
ROWS_W1IN, ROWS_W2IN, ROWS_WOUT = 1024, 1028, 256
ADAM_BLOCKS = 4
PAD_ROWS2 = 60
ADAM_LR, ADAM_B1, ADAM_B2, ADAM_EPS, ADAM_WD, ADAM_STEP = 0.001, 0.9, 0.999, 1e-08, 0.01, 10
VMEM_LIMIT = 56 * 1024 * 1024
MASKED = -1e30
BOUNDED_SOFTMAX_REACH = 60.0
MESH = pl.DeviceIdType.MESH


def _params(*sem, vmem=VMEM_LIMIT):
    return pltpu.CompilerParams(dimension_semantics=sem or None, vmem_limit_bytes=vmem)


def _const(shape):
    nd = len(shape)
    return pl.BlockSpec(shape, lambda *_: (0,) * nd, pipeline_mode=pl.Buffered(1))


def _rows(tm, n, rev=False, steps=None):
    if rev:
        return pl.BlockSpec((tm, n), lambda i: (steps - 1 - i, 0))
    return pl.BlockSpec((tm, n), lambda i: (i, 0))


def _dot(a, b):
    return jnp.dot(a, b, preferred_element_type=F32)


def _top16(x):
    bits = lax.bitcast_convert_type(x, jnp.uint32) & jnp.uint32(0xFFFF0000)
    return lax.bitcast_convert_type(bits, F32)


def _split2(x):
    hi = _top16(x)
    return hi.astype(BF16), (x - hi).astype(BF16)


def _split3(x):
    hi = _top16(x)
    r = x - hi
    mid = _top16(r)
    return hi.astype(BF16), mid.astype(BF16), (r - mid).astype(BF16)


def _seg_sum(a, e):
    hi, lo = _split2(a)
    return _dot(hi, e) + _dot(lo, e)


def _seg_bcast(s, et2):
    return _dot(jnp.concatenate(_split2(s), axis=1), et2)


def _tri_sum(t, v):
    hi, mid, lo = _split3(v)
    return _dot(t, hi) + _dot(t, mid) + _dot(t, lo)


def _sigmoid(z):
    return 1.0 / (1.0 + jnp.exp(-z))


def _place():
    return lax.axis_index("x"), lax.axis_index("y"), lax.axis_index("c")


def _all_gather(xb, name):
    r, c_ = xb.shape

    def body(x_ref, out_ref, send_sems, recv_sems, local_sem):
        x, y, c = _place()
        me, sibling = (x, y, c), (x, y, 1 - c)
        chips = [(1 - x, y), (x, 1 - y), (1 - x, 1 - y)]

        def slab(px, py, pc):
            return out_ref.at[4 * px + 2 * py + pc]

        def copy(k, block, to, src=None):
            return pltpu.make_async_remote_copy(
                src_ref=slab(*block) if src is None else src, dst_ref=slab(*block),
                send_sem=send_sems.at[k], recv_sem=recv_sems.at[k], device_id=to, device_id_type=MESH)

        mine = pltpu.make_async_copy(x_ref, slab(*me), local_sem)
        mine.start()
        first = [copy(0, me, sibling, src=x_ref)]
        first += [copy(1 + j, me, (*chip, c), src=x_ref) for j, chip in enumerate(chips)]
        for cp in first:
            cp.start()
        passed = [copy(4 + j, (*chip, c), sibling) for j, chip in enumerate(chips)]
        for j, chip in enumerate(chips):
            copy(1 + j, (*chip, c), me).wait_recv()
            passed[j].start()
        copy(0, sibling, me).wait_recv()
        for j, chip in enumerate(chips):
            copy(4 + j, (*chip, 1 - c), me).wait_recv()
        for cp in first + passed:
            cp.wait_send()
        mine.wait()

    return pl.pallas_call(
        body, name=name,
        out_shape=jax.ShapeDtypeStruct((NDEV, r, c_), xb.dtype),
        in_specs=[pl.BlockSpec(memory_space=pl.ANY)],
        out_specs=pl.BlockSpec(memory_space=pl.ANY),
        scratch_shapes=[pltpu.SemaphoreType.DMA((7,)), pltpu.SemaphoreType.DMA((7,)), pltpu.SemaphoreType.DMA],
    )(xb)


def _block_copies(g_ref, out_ref, send_sems, recv_sems, local_sem):
    x, y, c = _place()
    copies = [pltpu.make_async_copy(g_ref.at[4 * x + 2 * y + c], out_ref.at[0], local_sem)]
    for k in range(1, NDEV):
        px = 1 - x if k & 4 else x
        py = 1 - y if k & 2 else y
        pc = 1 - c if k & 1 else c
        copies.append(pltpu.make_async_remote_copy(
            src_ref=g_ref.at[4 * px + 2 * py + pc], dst_ref=out_ref.at[k],
            send_sem=send_sems.at[k - 1], recv_sem=recv_sems.at[k - 1],
            device_id=(px, py, pc), device_id_type=MESH))
    return copies


def _gather_copies(x_ref, out_ref, send_sems, recv_sems, local_sem):
    x, y, c = _place()
    mine = out_ref.at[4 * x + 2 * y + c]
    copies = [pltpu.make_async_copy(x_ref, mine, local_sem)]
    for k in range(1, NDEV):
        peer = (1 - x if k & 4 else x, 1 - y if k & 2 else y, 1 - c if k & 1 else c)
        copies.append(pltpu.make_async_remote_copy(
            src_ref=x_ref, dst_ref=mine, send_sem=send_sems.at[k - 1], recv_sem=recv_sems.at[k - 1],
            device_id=peer, device_id_type=MESH))
    return copies


EXCHANGE_SEMS = [pltpu.SemaphoreType.DMA((7,)), pltpu.SemaphoreType.DMA((7,)), pltpu.SemaphoreType.DMA]


def _exchange_blocks(g, name):
    _, r, c_ = g.shape

    def body(g_ref, out_ref, send_sems, recv_sems, local_sem):
        copies = _block_copies(g_ref, out_ref, send_sems, recv_sems, local_sem)
        for cp in copies:
            cp.start()
        for cp in copies:
            cp.wait()

    return pl.pallas_call(
        body, name=name,
        out_shape=jax.ShapeDtypeStruct((NDEV, r, c_), g.dtype),
        in_specs=[pl.BlockSpec(memory_space=pl.ANY)],
        out_specs=pl.BlockSpec(memory_space=pl.ANY),
        scratch_shapes=list(EXCHANGE_SEMS),
    )(g)


def _conv_fwd(x, g1, w1in, cw, w1out, shard2):
    s = x.shape[0]
    tm = min(TM_FWD, s)
    steps = s // tm

    def body(x_ref, g_ref, win_ref, cw_ref, wout_ref, sh_ref, x1_ref, h1_ref, p1_ref, yg_ref, all_ref, tail_ref,
             send_sems, recv_sems, local_sem):
        @pl.when(pl.program_id(0) == 0)
        def _():
            tail_ref[...] = jnp.zeros_like(tail_ref)
            for cp in _gather_copies(sh_ref, all_ref, send_sems, recv_sems, local_sem):
                cp.start()

        xv = x_ref[...]
        inv = lax.rsqrt(jnp.mean(xv * xv, axis=-1, keepdims=True) + RMS_EPS)
        h = (xv * inv * g_ref[...]).astype(BF16)
        h1_ref[...] = h
        row = lax.broadcasted_iota(jnp.int32, (tm, CH), 0)
        for ci in range(D // CH):
            lo, hi = ci * CH, (ci + 1) * CH
            parts = []
            for k in range(4):
                pk = _dot(h, win_ref[:, k * D + lo:k * D + hi]).astype(BF16)
                p1_ref[:, k * D + lo:k * D + hi] = pk
                parts.append(pk.astype(F32))
            b, c, xin, z = parts
            u = c * xin
            t6 = tail_ref[6:7, lo:hi]
            t7 = tail_ref[7:8, lo:hi]
            u1 = jnp.where(row == 0, t7, pltpu.roll(u, 1, 0))
            u2 = jnp.where(row == 0, t6, jnp.where(row == 1, t7, pltpu.roll(u, 2, 0)))
            tail_ref[:, lo:hi] = u[tm - 8:, :]
            y = cw_ref[2:3, lo:hi] * u + cw_ref[1:2, lo:hi] * u1 + cw_ref[0:1, lo:hi] * u2
            yg_ref[:, lo:hi] = (b * y * (z * _sigmoid(z))).astype(BF16)
        x1_ref[...] = xv + _dot(yg_ref[...], wout_ref[...])

        @pl.when(pl.program_id(0) == steps - 1)
        def _():
            for cp in _gather_copies(sh_ref, all_ref, send_sems, recv_sems, local_sem):
                cp.wait()

    return pl.pallas_call(
        body, name="conv_fwd", grid=(steps,),
        in_specs=[_rows(tm, D), _const((1, D)), _const((D, 4 * D)), _const((8, D)), _const((D, D)),
                  pl.BlockSpec(memory_space=pl.ANY)],
        out_specs=[_rows(tm, D), _rows(tm, D), _rows(tm, 4 * D), _rows(tm, D), pl.BlockSpec(memory_space=pl.ANY)],
        out_shape=[jax.ShapeDtypeStruct((s, D), F32), jax.ShapeDtypeStruct((s, D), BF16),
                   jax.ShapeDtypeStruct((s, 4 * D), BF16), jax.ShapeDtypeStruct((s, D), BF16),
                   jax.ShapeDtypeStruct((NDEV,) + shard2.shape, shard2.dtype)],
        scratch_shapes=[pltpu.VMEM((8, D), F32)] + list(EXCHANGE_SEMS),
        compiler_params=_params("arbitrary"),
    )(x, g1, w1in, cw, w1out, shard2)


COL_BIAS = DH
ROW_BIAS = DH + 3


def _terms_cat(val):
    return jnp.concatenate(_split3(val), axis=1)


def _split_heads(x, aug, out_ref, lane, t_ref=None):
    for hp in range(H // 2):
        pair = x[:, hp * HA:(hp + 1) * HA]
        for k, feat in enumerate((pair, pltpu.roll(pair, DH, 1))):
            block = jnp.where(lane < DH, feat, aug(2 * hp + k))
            out_ref[2 * hp + k] = block.astype(BF16)
            if t_ref is not None:
                t_ref[2 * hp + k] = block.T[0:t_ref.shape[1], :].astype(BF16)


def _ones_at(lane, first):
    return jnp.where((lane >= first) & (lane < first + 3), 1.0, 0.0)


def _attn_proj_fwd(x1, g2, w2in, wf, bf, qg, kg, shift, e, et, tril, place_k, place_q, group):
    s = x1.shape[0]
    tm = min(TM_BWD, s)
    steps = s // tm
    per = group // tm

    def body(x_ref, g_ref, win_ref, wf_ref, bf_ref, qg_ref, kg_ref, sh_ref, e_ref, et_ref, tril_ref, pk_ref, pq_ref,
             h2_ref, qp_ref, kp_ref, z_ref, qa_ref, ka_ref, va_ref, vat_ref, iq_ref, ik_ref, c_ref, fl_ref, carry_ref):
        @pl.when(pl.program_id(0) == 0)
        def _():
            carry_ref[...] = jnp.zeros_like(carry_ref)

        xv = x_ref[...]
        inv = lax.rsqrt(jnp.mean(xv * xv, axis=-1, keepdims=True) + RMS_EPS)
        h = (xv * inv * g_ref[...]).astype(BF16)
        h2_ref[...] = h

        fl = _dot(h, wf_ref[...]) + bf_ref[...]
        fl_ref[...] = fl
        ex = jnp.exp(-jnp.abs(fl))
        up = 1.0 + ex
        log1p = jnp.where(up == 1.0, ex, jnp.log(up) * ex / (up - 1.0))
        lane = lax.broadcasted_iota(jnp.int32, (tm, LANES), 1)
        logf = jnp.where(lane < H, jnp.minimum(fl, 0.0) - log1p, 0.0)
        c_ref[...] = _tri_sum(tril_ref[...], logf) + carry_ref[0:1, :]
        carry_ref[0:1, :] = c_ref[tm - 1:tm, :]
        c = c_ref[...]

        def normed(col, pre_ref, inv_ref, gain_ref, scale):
            pre = _dot(h, win_ref[:, col * D:(col + 1) * D]).astype(BF16)
            pre_ref[...] = pre
            t = pre.astype(F32)
            invh = lax.rsqrt(_seg_sum(t * t, e_ref[...]) * (1.0 / DH) + RMS_EPS)
            inv_ref[...] = invh
            return t * _seg_bcast(invh, et_ref[...]) * (gain_ref[...] * scale)

        ones_col, ones_row = _ones_at(lane, COL_BIAS), _ones_at(lane, ROW_BIAS)
        q_bias = _dot(_terms_cat(c - sh_ref[...]), pq_ref[...])
        _split_heads(normed(0, qp_ref, iq_ref, qg_ref, 1.0 / math.sqrt(DH)),
                     lambda hh: q_bias[:, hh * HA:(hh + 1) * HA] + ones_col, qa_ref, lane)
        k_bias = _dot(_terms_cat(-c), pk_ref[...])
        _split_heads(normed(1, kp_ref, ik_ref, kg_ref, 1.0),
                     lambda hh: k_bias[:, hh * HA:(hh + 1) * HA] + ones_row, ka_ref, lane)
        v = _dot(h, win_ref[:, 2 * D:3 * D])
        _split_heads(v, lambda hh: ones_col, va_ref, lane, vat_ref)
        z_ref[...] = _dot(h, win_ref[:, 3 * D:4 * D]).astype(BF16)

    row_bf = lambda: _rows(tm, D)
    row_sm = lambda: _rows(tm, LANES)
    heads = lambda: pl.BlockSpec((H, tm, HA), lambda i: (0, i, 0))
    return pl.pallas_call(
        body, name="attn_proj_fwd", grid=(steps,),
        in_specs=[_rows(tm, D), _const((1, D)), _const((D, 4 * D)), _const((D, LANES)), _const((1, LANES)),
                  _const((1, D)), _const((1, D)), _const((1, LANES)), _const((D, LANES)), _const((2 * LANES, D)),
                  _const((tm, tm)), _const((3 * LANES, H * HA)), _const((3 * LANES, H * HA))],
        out_specs=[row_bf() for _ in range(4)] + [heads() for _ in range(3)] + [
            pl.BlockSpec((H, None, AUG_ROWS, tm), lambda i: (0, i // per, 0, i % per))] + [row_sm() for _ in range(4)],
        out_shape=[jax.ShapeDtypeStruct((s, D), BF16)] * 4 + [jax.ShapeDtypeStruct((H, s, HA), BF16)] * 3 + [
            jax.ShapeDtypeStruct((H, s // group, AUG_ROWS, group), BF16)] + [jax.ShapeDtypeStruct((s, LANES), F32)] * 4,
        scratch_shapes=[pltpu.VMEM((8, LANES), F32)],
        compiler_params=_params("arbitrary"),
    )(x1, g2, w2in, wf, bf, qg, kg, shift, e, et, tril, place_k, place_q)


def _dot_nt(a, b):
    return lax.dot_general(a, b, (((1,), (1,)), ((), ())), preferred_element_type=F32)


def _dot_tn(a, b):
    return lax.dot_general(a, b, (((0,), (0,)), ((), ())), preferred_element_type=F32)


def _flash_fwd(bounded, qa, ka, vat):
    h_, nq, tq, _ = qa.shape
    s, tk = ka.shape[1], tq
    group = vat.shape[3]
    gt = group // tk
    shift = gt.bit_length() - 1

    def body(flag_ref, q_ref, k_ref, vt_ref, o_ref, qb_ref, m_ref, acct_ref):
        i = pl.program_id(1)
        lane = lax.broadcasted_iota(jnp.int32, (tq, HA), 1)
        in_bias = (lane >= ROW_BIAS) & (lane < ROW_BIAS + 3)
        acct_ref[...] = jnp.zeros_like(acct_ref)
        g = lax.shift_right_logical(i, shift)
        r = i & (gt - 1)

        def run(use_bound):
            q = q_ref[...]
            if not use_bound:
                q = jnp.where(in_bias, jnp.zeros_like(q), q)
                m_ref[...] = jnp.full_like(m_ref, MASKED)

            def step(first_key, vt, tiles, diagonal_at):
                keys = pl.ds(pl.multiple_of(first_key, tk), tiles * tk)
                zt = _dot_nt(k_ref[keys, :], q)
                if diagonal_at is not None:
                    key = lax.broadcasted_iota(jnp.int32, (tiles * tk, tq), 0)
                    qry = lax.broadcasted_iota(jnp.int32, (tiles * tk, tq), 1)
                    zt = jnp.where(key <= qry + diagonal_at * tk, zt, MASKED)
                if use_bound:
                    acct_ref[0:AUG_ROWS, :] += _dot(vt, jnp.exp(zt).astype(BF16))
                else:
                    m_old = m_ref[...]
                    m_new = jnp.maximum(m_old, jnp.max(zt, axis=0, keepdims=True))
                    pt = jnp.exp(zt - m_new)
                    acct_ref[0:AUG_ROWS, :] = jnp.exp(m_old - m_new) * acct_ref[0:AUG_ROWS, :] + _dot(
                        vt, pt.astype(BF16))
                    m_ref[...] = m_new

            def whole_group(jj, carry):
                step(jj * group, vt_ref[jj], gt, None)
                return carry

            lax.fori_loop(0, g, whole_group, 0)
            for rr in range(gt):
                pl.when(r == rr)(functools.partial(
                    lambda rr: step(g * group, vt_ref[g, :, 0:(rr + 1) * tk], rr + 1, rr), rr))

            if not use_bound:
                acct_ref[COL_BIAS + 1:COL_BIAS + 2, :] = m_ref[...]
            acc = acct_ref[...].T
            l = jnp.sum(jnp.where(lane == COL_BIAS, acc, 0.0), axis=1, keepdims=True)
            if use_bound:
                m = -jnp.sum(jnp.where(in_bias, q.astype(F32), 0.0), axis=1, keepdims=True)
            else:
                m = jnp.sum(jnp.where(lane == COL_BIAS + 1, acc, 0.0), axis=1, keepdims=True)
            lse = m + jnp.log(l)
            o_ref[...] = jnp.where(lane < DH, acc / l, lse)
            hi, mid, lo = _split3(-lse)
            qb_ref[...] = jnp.where(lane == ROW_BIAS, hi, jnp.where(lane == ROW_BIAS + 1, mid, jnp.where(
                lane == ROW_BIAS + 2, lo, q_ref[...])))

        use_bound = flag_ref[0, 0] > 0.5
        pl.when(use_bound)(lambda: run(True))
        pl.when(jnp.logical_not(use_bound))(lambda: run(False))

    tile = lambda: pl.BlockSpec((None, None, tq, HA), lambda h, i: (h, i, 0, 0))
    return pl.pallas_call(
        body, name="flash_fwd", grid=(h_, nq),
        in_specs=[pl.BlockSpec(memory_space=pltpu.SMEM), tile(), pl.BlockSpec((None, s, HA), lambda h, i: (h, 0, 0)),
                  pl.BlockSpec((None, s // group, AUG_ROWS, group), lambda h, i: (h, 0, 0, 0))],
        out_specs=[tile(), tile()],
        out_shape=[jax.ShapeDtypeStruct((h_, nq, tq, HA), F32), jax.ShapeDtypeStruct((h_, nq, tq, HA), BF16)],
        scratch_shapes=[pltpu.VMEM((1, tq), F32), pltpu.VMEM((HA, tq), F32)],
        compiler_params=_params("arbitrary", "arbitrary"),
    )(bounded, qa, ka, vat)


def _merge_heads(src, out_ref, lane):
    for hp in range(H // 2):
        out_ref[:, hp * HA:(hp + 1) * HA] = jnp.where(lane < DH, src(2 * hp), pltpu.roll(src(2 * hp + 1), DH, 1))


def _attn_out(ol, z, x1, tgt, w2out, w2out_t, e, place_k):
    s = z.shape[0]
    tm = min(TM_BWD, s)
    steps = s // tm

    def body(ol_ref, z_ref, x1_ref, t_ref, w_ref, wt_ref, e_ref, pk_ref, og_ref, dx2_ref, dz_ref, doa_ref, loss_ref,
             o_ref):
        @pl.when(pl.program_id(0) == 0)
        def _():
            loss_ref[...] = jnp.zeros_like(loss_ref)

        lane = lax.broadcasted_iota(jnp.int32, (tm, HA), 1)
        _merge_heads(lambda hh: ol_ref[hh], o_ref, lane)
        ov = o_ref[...]
        zv = z_ref[...].astype(F32)
        sg = _sigmoid(zv)
        sil = zv * sg
        og = (ov * sil).astype(BF16)
        og_ref[...] = og
        err = x1_ref[...] + _dot(og, w_ref[...]) - t_ref[...]
        loss_ref[...] += (0.5 / D) * jnp.sum(err * err, axis=0, keepdims=True)
        dx2 = err * (1.0 / D)
        dx2_ref[...] = dx2
        dog = _dot(dx2.astype(BF16), wt_ref[...])
        do = (dog * sil).astype(BF16).astype(F32)
        dz_ref[...] = (dog * ov * (sg * (1.0 + zv * (1.0 - sg)))).astype(BF16)
        delta = _seg_sum(do * ov, e_ref[...])
        d_bias = _dot(_terms_cat(-delta), pk_ref[...])
        _split_heads(do, lambda hh: d_bias[:, hh * HA:(hh + 1) * HA], doa_ref, lane)

    heads = lambda: pl.BlockSpec((H, tm, HA), lambda i: (0, i, 0))
    return pl.pallas_call(
        body, name="attn_out", grid=(steps,),
        in_specs=[heads(), _rows(tm, D), _rows(tm, D), _rows(tm, D), _const((D, D)), _const((D, D)),
                  _const((D, LANES)), _const((3 * LANES, H * HA))],
        out_specs=[_rows(tm, D), _rows(tm, D), _rows(tm, D), heads(), pl.BlockSpec((1, D), lambda i: (0, 0))],
        out_shape=[jax.ShapeDtypeStruct((s, D), BF16), jax.ShapeDtypeStruct((s, D), F32),
                   jax.ShapeDtypeStruct((s, D), BF16), jax.ShapeDtypeStruct((H, s, HA), BF16),
                   jax.ShapeDtypeStruct((1, D), F32)],
        scratch_shapes=[pltpu.VMEM((tm, D), F32)],
        compiler_params=_params("arbitrary"),
    )(ol, z, x1, tgt, w2out, w2out_t, e, place_k)


def _flash_bwd(qab, doa, ka, va):
    h_, nq, tq, _ = qab.shape
    nk, tk = ka.shape[1], ka.shape[2]
    s = nk * tk
    gt = min(BWD_TILES, nq)
    shift = gt.bit_length() - 1

    def body(qa_ref, da_ref, ka_ref, va_ref, dq_hbm, dk_ref, dv_ref, dq_acc, dkt_ref, dvt_ref, sem):
        hh = pl.program_id(0)
        j = pl.program_id(1)

        @pl.when(j == 0)
        def _():
            dq_acc[...] = jnp.zeros_like(dq_acc)

        dkt_ref[...] = jnp.zeros_like(dkt_ref)
        dvt_ref[...] = jnp.zeros_like(dvt_ref)
        ka_v = ka_ref[...]
        va_v = va_ref[...]

        def step(i, tiles, masked):
            rows = pl.ds(pl.multiple_of(i * tq, tq), tiles * tq)
            qa_i, da_i = qa_ref[rows, :], da_ref[rows, :]
            zt = _dot_nt(ka_v, qa_i)
            if masked:
                key = lax.broadcasted_iota(jnp.int32, (tk, tiles * tq), 0)
                qry = lax.broadcasted_iota(jnp.int32, (tk, tiles * tq), 1)
                zt = jnp.where(key <= qry, zt, MASKED)
            pt = jnp.exp(zt)
            dst = (pt * _dot_nt(va_v, da_i)).astype(BF16)
            dq_acc[rows, :] += _dot_tn(dst, ka_v)
            dvt_ref[...] += _dot(pt.astype(BF16), da_i)
            dkt_ref[...] += _dot(dst, qa_i)

        g0 = lax.shift_right_logical(j, shift)
        for rr in range(gt):
            pl.when((j & (gt - 1)) == rr)(functools.partial(step, j, gt - rr, True))

        def whole_group(g, carry):
            step(g * gt, gt, False)
            return carry

        lax.fori_loop(g0 + 1, nq // gt, whole_group, 0)
        dk_ref[...] = dkt_ref[...]
        dv_ref[...] = dvt_ref[...].astype(BF16)

        @pl.when(j == nk - 1)
        def _():
            cp = pltpu.make_async_copy(dq_acc, dq_hbm.at[hh], sem)
            cp.start()
            cp.wait()

    whole = lambda: pl.BlockSpec((None, nq * tq, HA), lambda h, j: (h, 0, 0))
    tile = lambda: pl.BlockSpec((None, None, tk, HA), lambda h, j: (h, j, 0, 0))
    rows = lambda: pl.BlockSpec((None, tk, HA), lambda h, j: (h, j, 0))
    return pl.pallas_call(
        body, name="flash_bwd", grid=(h_, nk),
        in_specs=[whole(), whole(), tile(), tile()],
        out_specs=[pl.BlockSpec(memory_space=pl.ANY), rows(), rows()],
        out_shape=[jax.ShapeDtypeStruct((h_, s, HA), F32)] * 2 + [jax.ShapeDtypeStruct((h_, s, HA), BF16)],
        scratch_shapes=[pltpu.VMEM((s, HA), F32), pltpu.VMEM((tk, HA), F32), pltpu.VMEM((tk, HA), F32),
                        pltpu.SemaphoreType.DMA],
        compiler_params=_params("arbitrary", "arbitrary"),
    )(qab.reshape(h_, s, HA), doa.reshape(h_, s, HA), ka, va)


def _attn_proj_bwd(dq, dk, dv, dz, qpre, kpre, invq, invk, fl, x1, dx2, g2, qg, kg, w2in_t, wf_t, e, et,
                   triu, fold):
    s = x1.shape[0]
    tm = min(TM_BWD, s)
    steps = s // tm

    def body(dq_ref, dk_ref, dvh_ref, dz_ref, qp_ref, kp_ref, iq_ref, ik_ref, fl_ref, x1_ref, dx2_ref,
             g2_ref, qg_ref, kg_ref, wt_ref, wft_ref, e_ref, et_ref, triu_ref, fold_ref,
             dx1_ref, dp_ref, df_ref, dg2_ref, dqg_ref, dkg_ref, dbf_ref, carry_ref, rc_ref, qcol_ref, kcol_ref,
             dqs_ref, dkn_ref, dv_ref):
        step = pl.program_id(0)
        lane = lax.broadcasted_iota(jnp.int32, (tm, HA), 1)
        _merge_heads(lambda hh: dvh_ref[hh].astype(F32), dv_ref, lane)
        dc = jnp.zeros((tm, HA), F32)
        for hp in range(H // 2):
            dqh = [dq_ref[2 * hp], dq_ref[2 * hp + 1]]
            dkh = [dk_ref[2 * hp], dk_ref[2 * hp + 1]]
            dqs_ref[:, hp * HA:(hp + 1) * HA] = jnp.where(lane < DH, dqh[0], pltpu.roll(dqh[1], DH, 1))
            dkn_ref[:, hp * HA:(hp + 1) * HA] = jnp.where(lane < DH, dkh[0], pltpu.roll(dkh[1], DH, 1))
            for k in range(2):
                sums = jnp.where(lane == ROW_BIAS, dqh[k], 0.0) - pltpu.roll(
                    jnp.where(lane == COL_BIAS, dkh[k], 0.0), ROW_BIAS - COL_BIAS, 1)
                dc = dc + pltpu.roll(sums, (2 * hp + k - ROW_BIAS) % HA, 1)

        @pl.when(step == 0)
        def _():
            carry_ref[...] = jnp.zeros_like(carry_ref)
            dg2_ref[...] = jnp.zeros_like(dg2_ref)
            dbf_ref[...] = jnp.zeros_like(dbf_ref)
            qcol_ref[...] = jnp.zeros_like(qcol_ref)
            kcol_ref[...] = jnp.zeros_like(kcol_ref)
            dqg_ref[...] = jnp.zeros_like(dqg_ref)
            dkg_ref[...] = jnp.zeros_like(dkg_ref)

        rc_ref[...] = _tri_sum(triu_ref[...], dc) + carry_ref[0:1, :]
        carry_ref[0:1, :] = rc_ref[0:1, :]
        df = rc_ref[...] * _sigmoid(-fl_ref[...])
        dfb = df.astype(BF16)
        df_ref[...] = dfb
        dbf_ref[...] += jnp.sum(df, axis=0, keepdims=True)

        def norm_bwd(dn, pre_ref, inv_ref, gain_ref, col_ref):
            pre = pre_ref[...].astype(F32)
            invh = inv_ref[...]
            invb = _seg_bcast(invh, et_ref[...])
            col_ref[...] += jnp.sum(dn * pre * invb, axis=0, keepdims=True)
            gd = dn * gain_ref[...]
            mean = _seg_sum(gd * pre, e_ref[...]) * (1.0 / DH)
            return invb * gd - pre * _seg_bcast(mean * invh * invh * invh, et_ref[...])

        dq = norm_bwd(dqs_ref[...] * (1.0 / math.sqrt(DH)), qp_ref, iq_ref, qg_ref, qcol_ref).astype(BF16)
        dp_ref[:, 0:D] = dq
        dh = _dot(dq, wt_ref[0:D, :])
        dk = norm_bwd(dkn_ref[...], kp_ref, ik_ref, kg_ref, kcol_ref).astype(BF16)
        dp_ref[:, D:2 * D] = dk
        dh += _dot(dk, wt_ref[D:2 * D, :])
        dvb = dv_ref[...].astype(BF16)
        dp_ref[:, 2 * D:3 * D] = dvb
        dh += _dot(dvb, wt_ref[2 * D:3 * D, :])
        dzb = dz_ref[...]
        dp_ref[:, 3 * D:4 * D] = dzb
        dh += _dot(dzb, wt_ref[3 * D:4 * D, :])
        dh += _dot(dfb, wft_ref[...])

        xv = x1_ref[...]
        inv = lax.rsqrt(jnp.mean(xv * xv, axis=-1, keepdims=True) + RMS_EPS)
        dg2_ref[...] += jnp.sum(dh * xv * inv, axis=0, keepdims=True)
        gh = dh * g2_ref[...]
        dx1_ref[...] = dx2_ref[...] + inv * gh - xv * (inv * inv * inv * jnp.mean(gh * xv, axis=-1, keepdims=True))

        @pl.when(step == steps - 1)
        def _():
            dqg_ref[...] = _fold_heads(qcol_ref[...], fold_ref[...])
            dkg_ref[...] = _fold_heads(kcol_ref[...], fold_ref[...])

    rr = lambda n: _rows(tm, n, rev=True, steps=steps)
    acc = lambda n, r=1: pl.BlockSpec((r, n), lambda i: (0, 0))
    heads = lambda: pl.BlockSpec((H, tm, HA), lambda i: (0, steps - 1 - i, 0))
    return pl.pallas_call(
        body, name="attn_proj_bwd", grid=(steps,),
        in_specs=[heads(), heads(), heads(),
                  rr(D), rr(D), rr(D), rr(LANES), rr(LANES), rr(LANES), rr(D), rr(D),
                  _const((1, D)), _const((1, D)), _const((1, D)), _const((4 * D, D)), _const((LANES, D)),
                  _const((D, LANES)), _const((2 * LANES, D)), _const((tm, tm)), _const((D, LANES))],
        out_specs=[rr(D), rr(4 * D), rr(LANES), acc(D), acc(LANES, 8), acc(LANES, 8), acc(LANES)],
        out_shape=[jax.ShapeDtypeStruct((s, D), F32), jax.ShapeDtypeStruct((s, 4 * D), BF16),
                   jax.ShapeDtypeStruct((s, LANES), BF16), jax.ShapeDtypeStruct((1, D), F32),
                   jax.ShapeDtypeStruct((8, LANES), F32), jax.ShapeDtypeStruct((8, LANES), F32),
                   jax.ShapeDtypeStruct((1, LANES), F32)],
        scratch_shapes=[pltpu.VMEM((8, LANES), F32), pltpu.VMEM((tm, LANES), F32), pltpu.VMEM((1, D), F32),
                        pltpu.VMEM((1, D), F32), pltpu.VMEM((tm, D), F32), pltpu.VMEM((tm, D), F32),
                        pltpu.VMEM((tm, D), F32)],
        compiler_params=_params("arbitrary"),
    )(dq, dk, dv, dz, qpre, kpre, invq, invk, fl, x1, dx2, g2, qg, kg, w2in_t, wf_t, e, et, triu, fold)


def _fold_heads(col, fold):
    hi, mid, lo = _split3(jnp.broadcast_to(col, (8, D)))
    return _dot(hi, fold) + _dot(mid, fold) + _dot(lo, fold)


def _conv_bwd(dx1, p1, x, g1, cw, w1out_t, w1in_t, grads2):
    s = x.shape[0]
    tm = min(TM_BWD, s)
    steps = s // tm
    halo = tm // 8

    def body(dx1_ref, b_ref, c_ref, xi_ref, z_ref, ch_ref, xh_ref, x_ref, g_ref, cw_ref, wot_ref, wit_ref, g2_ref,
             gx_ref, dp_ref, dcw_ref, dg1_ref, parts_ref, head_ref, send_sems, recv_sems, local_sem):
        step = pl.program_id(0)

        @pl.when(step == 0)
        def _():
            for cp in _block_copies(g2_ref, parts_ref, send_sems, recv_sems, local_sem):
                cp.start()

        @pl.when(step == 0)
        def _():
            head_ref[...] = jnp.zeros_like(head_ref)
            dcw_ref[...] = jnp.zeros_like(dcw_ref)
            dg1_ref[...] = jnp.zeros_like(dg1_ref)

        first_tile = step == steps - 1
        dx1 = dx1_ref[...]
        row = lax.broadcasted_iota(jnp.int32, (tm, CH), 0)
        dyb = dx1.astype(BF16)
        for ci in range(D // CH):
            lo, hi = ci * CH, (ci + 1) * CH
            dyg = _dot(dyb, wot_ref[:, lo:hi])
            b = b_ref[:, lo:hi].astype(F32)
            c = c_ref[:, lo:hi].astype(F32)
            xin = xi_ref[:, lo:hi].astype(F32)
            z = z_ref[:, lo:hi].astype(F32)
            u = c * xin
            t6 = jnp.where(first_tile, 0.0, ch_ref[6:7, lo:hi].astype(F32) * xh_ref[6:7, lo:hi].astype(F32))
            t7 = jnp.where(first_tile, 0.0, ch_ref[7:8, lo:hi].astype(F32) * xh_ref[7:8, lo:hi].astype(F32))
            u1 = jnp.where(row == 0, t7, pltpu.roll(u, 1, 0))
            u2 = jnp.where(row == 0, t6, jnp.where(row == 1, t7, pltpu.roll(u, 2, 0)))
            w0, w1, w2 = cw_ref[0:1, lo:hi], cw_ref[1:2, lo:hi], cw_ref[2:3, lo:hi]
            y = w2 * u + w1 * u1 + w0 * u2
            sg = _sigmoid(z)
            sil = z * sg
            dp_ref[:, lo:hi] = (dyg * y * sil).astype(BF16)
            dy = dyg * b * sil
            dp_ref[:, 3 * D + lo:3 * D + hi] = (dyg * b * y * (sg * (1.0 + z * (1.0 - sg)))).astype(BF16)
            dcw_ref[2:3, lo:hi] += jnp.sum(dy * u, axis=0, keepdims=True)
            dcw_ref[1:2, lo:hi] += jnp.sum(dy * u1, axis=0, keepdims=True)
            dcw_ref[0:1, lo:hi] += jnp.sum(dy * u2, axis=0, keepdims=True)
            n0 = head_ref[0:1, lo:hi]
            n1 = head_ref[1:2, lo:hi]
            dyn1 = jnp.where(row == tm - 1, n0, pltpu.roll(dy, tm - 1, 0))
            dyn2 = jnp.where(row == tm - 2, n0, jnp.where(row == tm - 1, n1, pltpu.roll(dy, tm - 2, 0)))
            head_ref[:, lo:hi] = dy[0:8, :]
            du = w2 * dy + w1 * dyn1 + w0 * dyn2
            dp_ref[:, D + lo:D + hi] = (du * xin).astype(BF16)
            dp_ref[:, 2 * D + lo:2 * D + hi] = (du * c).astype(BF16)
        dh = _dot(dp_ref[:, 0:D], wit_ref[0:D, :])
        for k in range(1, 4):
            dh += _dot(dp_ref[:, k * D:(k + 1) * D], wit_ref[k * D:(k + 1) * D, :])
        xv = x_ref[...]
        inv = lax.rsqrt(jnp.mean(xv * xv, axis=-1, keepdims=True) + RMS_EPS)
        dg1_ref[...] += jnp.sum(dh * xv * inv, axis=0, keepdims=True)
        gh = dh * g_ref[...]
        gx_ref[...] = dx1 + inv * gh - xv * (inv * inv * inv * jnp.mean(gh * xv, axis=-1, keepdims=True))

        @pl.when(step == steps - 1)
        def _():
            for cp in _block_copies(g2_ref, parts_ref, send_sems, recv_sems, local_sem):
                cp.wait()

    rr = lambda n: _rows(tm, n, rev=True, steps=steps)
    part = lambda k: pl.BlockSpec((tm, D), lambda i: (steps - 1 - i, k))
    prev8 = lambda k: pl.BlockSpec((8, D), lambda i: (jnp.maximum((steps - 1 - i) * halo - 1, 0), k))
    return pl.pallas_call(
        body, name="conv_bwd", grid=(steps,),
        in_specs=[rr(D), part(0), part(1), part(2), part(3), prev8(1), prev8(2), rr(D), _const((1, D)),
                  _const((8, D)), _const((D, D)), _const((4 * D, D)), pl.BlockSpec(memory_space=pl.ANY)],
        out_specs=[rr(D), rr(4 * D), pl.BlockSpec((8, D), lambda i: (0, 0)), pl.BlockSpec((1, D), lambda i: (0, 0)),
                   pl.BlockSpec(memory_space=pl.ANY)],
        out_shape=[jax.ShapeDtypeStruct((s, D), F32), jax.ShapeDtypeStruct((s, 4 * D), BF16),
                   jax.ShapeDtypeStruct((8, D), F32), jax.ShapeDtypeStruct((1, D), F32),
                   jax.ShapeDtypeStruct(grads2.shape, grads2.dtype)],
        scratch_shapes=[pltpu.VMEM((8, D), F32)] + list(EXCHANGE_SEMS),
        compiler_params=_params("arbitrary"),
    )(dx1, p1, p1, p1, p1, p1, p1, x, g1, cw, w1out_t, w1in_t, grads2)


def _wgrad(a, g, name):
    s, k = a.shape
    n = g.shape[1]
    bn = min(n, 1024)
    ts = min(1024, s)

    def body(a_ref, g_ref, out_ref, acc_ref):
        t = pl.program_id(1)

        @pl.when(t == 0)
        def _():
            acc_ref[...] = jnp.zeros_like(acc_ref)

        acc_ref[...] += _dot_tn(a_ref[...], g_ref[...].astype(BF16))

        @pl.when(t == s // ts - 1)
        def _():
            out_ref[...] = acc_ref[...].astype(BF16)

    return pl.pallas_call(
        body, name=name, grid=(n // bn, s // ts),
        in_specs=[pl.BlockSpec((ts, k), lambda j, t: (t, 0)), pl.BlockSpec((ts, bn), lambda j, t: (t, j))],
        out_specs=pl.BlockSpec((k, bn), lambda j, t: (0, j)),
        out_shape=jax.ShapeDtypeStruct((k, n), BF16),
        scratch_shapes=[pltpu.VMEM((k, bn), F32)],
        compiler_params=_params("arbitrary", "arbitrary"),
    )(a, g)


def _adamw_math(w, g, m, v):
    m = ADAM_B1 * m + (1.0 - ADAM_B1) * g
    v = ADAM_B2 * v + (1.0 - ADAM_B2) * (g * g)
    m_hat = m / (1.0 - ADAM_B1 ** ADAM_STEP)
    v_hat = v / (1.0 - ADAM_B2 ** ADAM_STEP)
    delta = -ADAM_LR * (m_hat / (jnp.sqrt(v_hat) + ADAM_EPS) + ADAM_WD * w)
    return delta, m, v


def _adamw_big(parts, w, m, v, name):
    _, r, c_ = parts.shape
    rb = r // ADAM_BLOCKS
    assert r % (16 * ADAM_BLOCKS) == 0

    def body(p_ref, w_ref, m_ref, v_ref, g_ref, d_ref, mo_ref, vo_ref):
        g = p_ref[0].astype(F32)
        for k in range(1, NDEV):
            g = g + p_ref[k].astype(F32)
        g_ref[...] = g
        d_ref[...], mo_ref[...], vo_ref[...] = _adamw_math(w_ref[...], g, m_ref[...], v_ref[...])

    blk = pl.BlockSpec((rb, c_), lambda i: (i, 0))
    return pl.pallas_call(
        body, name=name, grid=(r // rb,),
        in_specs=[pl.BlockSpec((NDEV, rb, c_), lambda i: (0, i, 0)), blk, blk, blk],
        out_specs=[blk, blk, blk, blk],
        out_shape=[jax.ShapeDtypeStruct((r, c_), F32)] * 4,
        compiler_params=_params("arbitrary"),
    )(parts, w, m, v)


def _sum_parts(parts, loss_row):
    _, r, c_ = parts.shape

    def body(p_ref, o_ref, l_ref):
        g = p_ref[0]
        for k in range(1, NDEV):
            g = g + p_ref[k]
        o_ref[...] = g
        per_row = jnp.sum(o_ref[loss_row:loss_row + 8, :], axis=1, keepdims=True)
        l_ref[...] = jnp.broadcast_to(jnp.sum(per_row, axis=0, keepdims=True), (8, c_))

    return pl.pallas_call(body, name="sum_small", out_shape=[jax.ShapeDtypeStruct((r, c_), F32),
                                                             jax.ShapeDtypeStruct((8, c_), F32)])(parts)


def _adamw_small(g, w, m, v):
    def body(g_ref, w_ref, m_ref, v_ref, d_ref, mo_ref, vo_ref):
        d_ref[...], mo_ref[...], vo_ref[...] = _adamw_math(w_ref[...], g_ref[...], m_ref[...], v_ref[...])

    return pl.pallas_call(body, name="adamw_small", out_shape=[jax.ShapeDtypeStruct(g.shape, F32)] * 3)(g, w, m, v)


def _pad_lanes(a):
    return jnp.pad(a, ((0, 0), (0, LANES - a.shape[1])))


def _tiles(a, t):
    return a.reshape(H, a.shape[1] // t, t, HA)


def kernel(x, conv_norm_g, conv_w_in, conv_w, conv_w_out, attn_norm_g, attn_w_in, attn_b_f, attn_q_norm_g, attn_k_norm_g, attn_w_out, loss_target, m_conv_norm_g, m_conv_w_in, m_conv_w, m_conv_w_out, m_attn_norm_g, m_attn_w_in, m_attn_b_f, m_attn_q_norm_g, m_attn_k_norm_g, m_attn_w_out, v_conv_norm_g, v_conv_w_in, v_conv_w, v_conv_w_out, v_attn_norm_g, v_attn_w_in, v_attn_b_f, v_attn_q_norm_g, v_attn_k_norm_g, v_attn_w_out):
    s = x.shape[1]
    tq = min(TQ, s)
    tmb = min(TM_BWD, s)
    me = 4 * lax.axis_index("x") + 2 * lax.axis_index("y") + lax.axis_index("c")
    xv, tgt = x[0], loss_target[0]

    def slab(t_in, t_out, pad, dtype):
        rows = [t_in[0].reshape(-1, PACK_W), t_out[0].reshape(ROWS_WOUT, PACK_W)]
        return jnp.concatenate(rows + ([jnp.zeros((pad, PACK_W), F32)] if pad else []), axis=0).astype(dtype)

    wg1 = _all_gather(slab(conv_w_in, conv_w_out, 0, BF16), "gather_weights")
    small_w = jnp.concatenate([conv_w[0], attn_norm_g, jnp.zeros((4, 128), F32)], axis=0)
    sg_ = _all_gather(small_w, "gather_small_weights")
    w1in = wg1[:, :ROWS_W1IN].transpose(1, 0, 2).reshape(D, 4 * D)
    w1out = wg1[:, ROWS_W1IN:].reshape(D, D)
    cw = jnp.concatenate([sg_[:, 0:3, :].transpose(1, 0, 2).reshape(3, D), jnp.zeros((5, D), F32)], axis=0)
    g2 = sg_[:, 3, :].reshape(1, D)
    qg_t, kg_t = jnp.tile(attn_q_norm_g, (1, H)), jnp.tile(attn_k_norm_g, (1, H))
    bf = _pad_lanes(attn_b_f)

    e = (jnp.arange(D)[:, None] // DH == jnp.arange(LANES)[None, :]).astype(BF16)
    et2 = jnp.concatenate([e.T, e.T], axis=0)
    fold = (jnp.arange(D)[:, None] % DH == jnp.arange(LANES)[None, :]).astype(BF16)
    tril = (jnp.arange(tmb)[:, None] >= jnp.arange(tmb)[None, :]).astype(BF16)
    triu = tril.T
    src = jnp.arange(3 * LANES)
    dst = jnp.arange(H * HA)[None, :] - (HA * (src % LANES) + src // LANES)[:, None]
    place_k = ((dst == COL_BIAS) & (src % LANES < H)[:, None]).astype(BF16)
    place_q = ((dst == ROW_BIAS) & (src % LANES < H)[:, None]).astype(BF16)

    x1, h1, p1, yg, wg2 = _conv_fwd(xv, conv_norm_g, w1in, cw, w1out, slab(attn_w_in, attn_w_out, PAD_ROWS2, BF16))
    w2all = wg2[:, :ROWS_W2IN].reshape(NDEV, D, 514).transpose(1, 0, 2).reshape(D, 4 * D + H)
    w2in, wf = w2all[:, :4 * D], _pad_lanes(w2all[:, 4 * D:])
    w2out = wg2[:, ROWS_W2IN:ROWS_W2IN + ROWS_WOUT].reshape(D, D)
    reach = 1.01 * math.sqrt(DH) * jnp.max(jnp.abs(attn_q_norm_g)) * jnp.max(jnp.abs(attn_k_norm_g))
    bounded = (2.0 * reach <= BOUNDED_SOFTMAX_REACH).astype(F32).reshape(1, 1)
    group = tq * min(FWD_TILES, s // tq)
    h2, qpre, kpre, z, qa, ka, va, vat, invq, invk, cc, fl = _attn_proj_fwd(
        x1, g2, w2in, wf, bf, qg_t, kg_t, jnp.broadcast_to(reach, (1, LANES)), e, et2, tril, place_k, place_q, group)
    ol, qab = _flash_fwd(bounded, _tiles(qa, tq), ka, vat)
    og, dx2, dz, doa, lossp = _attn_out(ol.reshape(H, s, HA), z, x1, tgt, w2out, w2out.T, e, place_k)

    dq, dk, dv = _flash_bwd(qab, _tiles(doa, tq), _tiles(ka, tq), _tiles(va, tq))
    dx1, dp2, df, dg2, dqg, dkg, dbf = _attn_proj_bwd(
        dq, dk, dv, dz, qpre, kpre, invq, invk, fl, x1, dx2, g2, qg_t, kg_t, w2in.T, wf.T, e, et2, triu, fold)
    dw2out = _wgrad(og, dx2, "wgrad_attn_out")
    dw2in = _wgrad(h2, dp2, "wgrad_attn_in")
    dwf = _wgrad(h2, df, "wgrad_attn_forget")
    dw2all = jnp.concatenate([dw2in, dwf[:, :H]], axis=1)
    grads2 = jnp.concatenate([
        dw2all.reshape(D, NDEV, 514).transpose(1, 0, 2).reshape(NDEV, ROWS_W2IN, PACK_W),
        dw2out.reshape(NDEV, ROWS_WOUT, PACK_W), jnp.zeros((NDEV, PAD_ROWS2, PACK_W), BF16)], axis=1)
    gx, dp1, dcw, dg1, parts2 = _conv_bwd(dx1, p1, xv, conv_norm_g, cw, w1out.T, w1in.T, grads2)
    dw1out = _wgrad(yg, dx1, "wgrad_conv_out")
    dw1in = _wgrad(h1, dp1, "wgrad_conv_in")
    grads1 = jnp.concatenate([dw1in.reshape(D, NDEV, 512).transpose(1, 0, 2),
                              dw1out.reshape(NDEV, ROWS_WOUT, PACK_W)], axis=1)
    parts1 = _exchange_blocks(grads1, "scatter_grads")

    big1 = _adamw_big(parts1, slab(conv_w_in, conv_w_out, 0, F32), slab(m_conv_w_in, m_conv_w_out, 0, F32),
                      slab(v_conv_w_in, v_conv_w_out, 0, F32), "adamw_conv")
    big2 = _adamw_big(parts2, slab(attn_w_in, attn_w_out, PAD_ROWS2, F32),
                      slab(m_attn_w_in, m_attn_w_out, PAD_ROWS2, F32),
                      slab(v_attn_w_in, v_attn_w_out, PAD_ROWS2, F32), "adamw_attn")

    shard_rows = jnp.concatenate([dcw[0:3].reshape(3, NDEV, 128).transpose(1, 0, 2), dg2.reshape(NDEV, 1, 128),
                                  jnp.zeros((NDEV, 4, 128), F32)], axis=1).reshape(64, 128)
    small_g = jnp.concatenate([shard_rows, dg1.reshape(8, 128), dbf, dqg[0:1], dkg[0:1], jnp.zeros((5, 128), F32),
                               lossp.reshape(8, 128)], axis=0)
    gs, loss8 = _sum_parts(_all_gather(small_g, "gather_small_grads"), 80)
    loss = loss8[0, 0]
    mine = lax.dynamic_slice(gs, (8 * me, 0), (8, 128))
    g_small = jnp.concatenate([mine, gs[64:75], jnp.zeros((5, 128), F32)], axis=0)

    def pack_small(cwk, ang, cng, bfk, qgk, kgk):
        return jnp.concatenate([cwk[0], ang, jnp.zeros((4, 128), F32), cng.reshape(8, 128), _pad_lanes(bfk),
                                _pad_lanes(qgk), _pad_lanes(kgk), jnp.zeros((5, 128), F32)], axis=0)

    ds_, ms_, vs_ = _adamw_small(
        g_small, pack_small(conv_w, attn_norm_g, conv_norm_g, attn_b_f, attn_q_norm_g, attn_k_norm_g),
        pack_small(m_conv_w, m_attn_norm_g, m_conv_norm_g, m_attn_b_f, m_attn_q_norm_g, m_attn_k_norm_g),
        pack_small(v_conv_w, v_attn_norm_g, v_conv_norm_g, v_attn_b_f, v_attn_q_norm_g, v_attn_k_norm_g))

    def leaves(i, small):
        b1, b2 = big1[i], big2[i]
        return (small[8:16].reshape(1, D), b1[:ROWS_W1IN].reshape(1, D, 512), small[0:3].reshape(1, 3, 128),
                b1[ROWS_W1IN:].reshape(1, 128, D), small[3:4], b2[:ROWS_W2IN].reshape(1, D, 514),
                small[16:17, :H], small[17:18, :DH], small[18:19, :DH],
                b2[ROWS_W2IN:ROWS_W2IN + ROWS_WOUT].reshape(1, 128, D))

    return (loss, gx[None], *leaves(0, g_small), *leaves(1, ds_), *leaves(2, ms_), *leaves(3, vs_))
```

```python
import functools
import math

import jax
import jax.numpy as jnp
from jax import lax
from jax.experimental import pallas as pl
from jax.experimental.pallas import tpu as pltpu

F32 = jnp.float32
BF16 = jnp.bfloat16

D = 1024
H = 16
DH = 64
NDEV = 8
RMS_EPS = 1e-6
LANES = 128
HA = 128
TM_FWD = 512
TM_BWD = 256
TQ = 512
AUG_ROWS = 80
FWD_TILES = 8
BWD_TILES = 4
CH = 256
PACK_W = 512
ROWS_W1IN, ROWS_W2IN, ROWS_WOUT = 1024, 1028, 256
ADAM_BLOCKS = 4
PAD_ROWS2 = 60
ADAM_LR, ADAM_B1, ADAM_B2, ADAM_EPS, ADAM_WD, ADAM_STEP = 0.001, 0.9, 0.999, 1e-08, 0.01, 10
VMEM_LIMIT = 56 * 1024 * 1024
MASKED = -1e30
BOUNDED_SOFTMAX_REACH = 60.0
MESH = pl.DeviceIdType.MESH


def _params(*sem, vmem=VMEM_LIMIT):
    return pltpu.CompilerParams(dimension_semantics=sem or None, vmem_limit_bytes=vmem)


def _const(shape):
    nd = len(shape)
    return pl.BlockSpec(shape, lambda *_: (0,) * nd, pipeline_mode=pl.Buffered(1))


def _rows(tm, n, rev=False, steps=None):
    if rev:
        return pl.BlockSpec((tm, n), lambda i: (steps - 1 - i, 0))
    return pl.BlockSpec((tm, n), lambda i: (i, 0))


def _dot(a, b):
    return jnp.dot(a, b, preferred_element_type=F32)


def _top16(x):
    bits = lax.bitcast_convert_type(x, jnp.uint32) & jnp.uint32(0xFFFF0000)
    return lax.bitcast_convert_type(bits, F32)


def _split2(x):
    hi = _top16(x)
    return hi.astype(BF16), (x - hi).astype(BF16)


def _split3(x):
    hi = _top16(x)
    r = x - hi
    mid = _top16(r)
    return hi.astype(BF16), mid.astype(BF16), (r - mid).astype(BF16)


def _seg_sum(a, e):
    hi, lo = _split2(a)
    return _dot(hi, e) + _dot(lo, e)


def _seg_bcast(s, et2):
    return _dot(jnp.concatenate(_split2(s), axis=1), et2)


def _tri_sum(t, v):
    hi, mid, lo = _split3(v)
    return _dot(t, hi) + _dot(t, mid) + _dot(t, lo)


def _sigmoid(z):
    return 1.0 / (1.0 + jnp.exp(-z))


def _place():
    return lax.axis_index("x"), lax.axis_index("y"), lax.axis_index("c")


def _all_gather(xb, name):
    r, c_ = xb.shape

    def body(x_ref, out_ref, send_sems, recv_sems, local_sem):
        x, y, c = _place()
        me, sibling = (x, y, c), (x, y, 1 - c)
        chips = [(1 - x, y), (x, 1 - y), (1 - x, 1 - y)]

        def slab(px, py, pc):
            return out_ref.at[4 * px + 2 * py + pc]

        def copy(k, block, to, src=None):
            return pltpu.make_async_remote_copy(
                src_ref=slab(*block) if src is None else src, dst_ref=slab(*block),
                send_sem=send_sems.at[k], recv_sem=recv_sems.at[k], device_id=to, device_id_type=MESH)

        mine = pltpu.make_async_copy(x_ref, slab(*me), local_sem)
        mine.start()
        first = [copy(0, me, sibling, src=x_ref)]
        first += [copy(1 + j, me, (*chip, c), src=x_ref) for j, chip in enumerate(chips)]
        for cp in first:
            cp.start()
        passed = [copy(4 + j, (*chip, c), sibling) for j, chip in enumerate(chips)]
        for j, chip in enumerate(chips):
            copy(1 + j, (*chip, c), me).wait_recv()
            passed[j].start()
        copy(0, sibling, me).wait_recv()
        for j, chip in enumerate(chips):
            copy(4 + j, (*chip, 1 - c), me).wait_recv()
        for cp in first + passed:
            cp.wait_send()
        mine.wait()

    return pl.pallas_call(
        body, name=name,
        out_shape=jax.ShapeDtypeStruct((NDEV, r, c_), xb.dtype),
        in_specs=[pl.BlockSpec(memory_space=pl.ANY)],
        out_specs=pl.BlockSpec(memory_space=pl.ANY),
        scratch_shapes=[pltpu.SemaphoreType.DMA((7,)), pltpu.SemaphoreType.DMA((7,)), pltpu.SemaphoreType.DMA],
    )(xb)


def _block_copies(g_ref, out_ref, send_sems, recv_sems, local_sem, first_row=0):
    x, y, c = _place()
    window = pl.ds(first_row, g_ref.shape[1])
    copies = [pltpu.make_async_copy(g_ref.at[4 * x + 2 * y + c], out_ref.at[0, window], local_sem)]
    for k in range(1, NDEV):
        px = 1 - x if k & 4 else x
        py = 1 - y if k & 2 else y
        pc = 1 - c if k & 1 else c
        copies.append(pltpu.make_async_remote_copy(
            src_ref=g_ref.at[4 * px + 2 * py + pc], dst_ref=out_ref.at[k, window],
            send_sem=send_sems.at[k - 1], recv_sem=recv_sems.at[k - 1],
            device_id=(px, py, pc), device_id_type=MESH))
    return copies


def _gather_copies(x_ref, out_ref, send_sems, recv_sems, local_sem):
    x, y, c = _place()
    mine = out_ref.at[4 * x + 2 * y + c]
    copies = [pltpu.make_async_copy(x_ref, mine, local_sem)]
    for k in range(1, NDEV):
        peer = (1 - x if k & 4 else x, 1 - y if k & 2 else y, 1 - c if k & 1 else c)
        copies.append(pltpu.make_async_remote_copy(
            src_ref=x_ref, dst_ref=mine, send_sem=send_sems.at[k - 1], recv_sem=recv_sems.at[k - 1],
            device_id=peer, device_id_type=MESH))
    return copies


EXCHANGE_SEMS = [pltpu.SemaphoreType.DMA((7,)), pltpu.SemaphoreType.DMA((7,)), pltpu.SemaphoreType.DMA]


def _conv_fwd(x, g1, w1in, cw, w1out, shard2):
    s = x.shape[0]
    tm = min(TM_FWD, s)
    steps = s // tm

    def body(x_ref, g_ref, win_ref, cw_ref, wout_ref, sh_ref, x1_ref, h1_ref, p1_ref, yg_ref, all_ref, tail_ref,
             send_sems, recv_sems, local_sem):
        @pl.when(pl.program_id(0) == 0)
        def _():
            tail_ref[...] = jnp.zeros_like(tail_ref)
            for cp in _gather_copies(sh_ref, all_ref, send_sems, recv_sems, local_sem):
                cp.start()

        xv = x_ref[...]
        inv = lax.rsqrt(jnp.mean(xv * xv, axis=-1, keepdims=True) + RMS_EPS)
        h = (xv * inv * g_ref[...]).astype(BF16)
        h1_ref[...] = h
        row = lax.broadcasted_iota(jnp.int32, (tm, CH), 0)
        for ci in range(D // CH):
            lo, hi = ci * CH, (ci + 1) * CH
            parts = []
            for k in range(4):
                pk = _dot(h, win_ref[:, k * D + lo:k * D + hi]).astype(BF16)
                p1_ref[:, k * D + lo:k * D + hi] = pk
                parts.append(pk.astype(F32))
            b, c, xin, z = parts
            u = c * xin
            t6 = tail_ref[6:7, lo:hi]
            t7 = tail_ref[7:8, lo:hi]
            u1 = jnp.where(row == 0, t7, pltpu.roll(u, 1, 0))
            u2 = jnp.where(row == 0, t6, jnp.where(row == 1, t7, pltpu.roll(u, 2, 0)))
            tail_ref[:, lo:hi] = u[tm - 8:, :]
            y = cw_ref[2:3, lo:hi] * u + cw_ref[1:2, lo:hi] * u1 + cw_ref[0:1, lo:hi] * u2
            yg_ref[:, lo:hi] = (b * y * (z * _sigmoid(z))).astype(BF16)
        x1_ref[...] = xv + _dot(yg_ref[...], wout_ref[...])

        @pl.when(pl.program_id(0) == steps - 1)
        def _():
            for cp in _gather_copies(sh_ref, all_ref, send_sems, recv_sems, local_sem):
                cp.wait()

    return pl.pallas_call(
        body, name="conv_fwd", grid=(steps,),
        in_specs=[_rows(tm, D), _const((1, D)), _const((D, 4 * D)), _const((8, D)), _const((D, D)),
                  pl.BlockSpec(memory_space=pl.ANY)],
        out_specs=[_rows(tm, D), _rows(tm, D), _rows(tm, 4 * D), _rows(tm, D), pl.BlockSpec(memory_space=pl.ANY)],
        out_shape=[jax.ShapeDtypeStruct((s, D), F32), jax.ShapeDtypeStruct((s, D), BF16),
                   jax.ShapeDtypeStruct((s, 4 * D), BF16), jax.ShapeDtypeStruct((s, D), BF16),
                   jax.ShapeDtypeStruct((NDEV,) + shard2.shape, shard2.dtype)],
        scratch_shapes=[pltpu.VMEM((8, D), F32)] + list(EXCHANGE_SEMS),
        compiler_params=_params("arbitrary"),
    )(x, g1, w1in, cw, w1out, shard2)


COL_BIAS = DH
ROW_BIAS = DH + 3


def _terms_cat(val):
    return jnp.concatenate(_split3(val), axis=1)


def _split_heads(x, aug, out_ref, lane, t_ref=None):
    for hp in range(H // 2):
        pair = x[:, hp * HA:(hp + 1) * HA]
        for k, feat in enumerate((pair, pltpu.roll(pair, DH, 1))):
            block = jnp.where(lane < DH, feat, aug(2 * hp + k))
            out_ref[2 * hp + k] = block.astype(BF16)
            if t_ref is not None:
                t_ref[2 * hp + k] = block.T[0:t_ref.shape[1], :].astype(BF16)


def _ones_at(lane, first):
    return jnp.where((lane >= first) & (lane < first + 3), 1.0, 0.0)


def _attn_proj_fwd(x1, g2, w2in, wf, bf, qg, kg, shift, e, et, tril, place_k, place_q, group):
    s = x1.shape[0]
    tm = min(TM_BWD, s)
    steps = s // tm
    per = group // tm

    def body(x_ref, g_ref, win_ref, wf_ref, bf_ref, qg_ref, kg_ref, sh_ref, e_ref, et_ref, tril_ref, pk_ref, pq_ref,
             h2_ref, qp_ref, kp_ref, z_ref, qa_ref, ka_ref, va_ref, vat_ref, iq_ref, ik_ref, c_ref, fl_ref, carry_ref):
        @pl.when(pl.program_id(0) == 0)
        def _():
            carry_ref[...] = jnp.zeros_like(carry_ref)

        xv = x_ref[...]
        inv = lax.rsqrt(jnp.mean(xv * xv, axis=-1, keepdims=True) + RMS_EPS)
        h = (xv * inv * g_ref[...]).astype(BF16)
        h2_ref[...] = h

        fl = _dot(h, wf_ref[...]) + bf_ref[...]
        fl_ref[...] = fl
        ex = jnp.exp(-jnp.abs(fl))
        up = 1.0 + ex
        log1p = jnp.where(up == 1.0, ex, jnp.log(up) * ex / (up - 1.0))
        lane = lax.broadcasted_iota(jnp.int32, (tm, LANES), 1)
        logf = jnp.where(lane < H, jnp.minimum(fl, 0.0) - log1p, 0.0)
        c_ref[...] = _tri_sum(tril_ref[...], logf) + carry_ref[0:1, :]
        carry_ref[0:1, :] = c_ref[tm - 1:tm, :]
        c = c_ref[...]

        def normed(col, pre_ref, inv_ref, gain_ref, scale):
            pre = _dot(h, win_ref[:, col * D:(col + 1) * D]).astype(BF16)
            pre_ref[...] = pre
            t = pre.astype(F32)
            invh = lax.rsqrt(_seg_sum(t * t, e_ref[...]) * (1.0 / DH) + RMS_EPS)
            inv_ref[...] = invh
            return t * _seg_bcast(invh, et_ref[...]) * (gain_ref[...] * scale)

        ones_col, ones_row = _ones_at(lane, COL_BIAS), _ones_at(lane, ROW_BIAS)
        q_bias = _dot(_terms_cat(c - sh_ref[...]), pq_ref[...])
        _split_heads(normed(0, qp_ref, iq_ref, qg_ref, 1.0 / math.sqrt(DH)),
                     lambda hh: q_bias[:, hh * HA:(hh + 1) * HA] + ones_col, qa_ref, lane)
        k_bias = _dot(_terms_cat(-c), pk_ref[...])
        _split_heads(normed(1, kp_ref, ik_ref, kg_ref, 1.0),
                     lambda hh: k_bias[:, hh * HA:(hh + 1) * HA] + ones_row, ka_ref, lane)
        v = _dot(h, win_ref[:, 2 * D:3 * D])
        _split_heads(v, lambda hh: ones_col, va_ref, lane, vat_ref)
        z_ref[...] = _dot(h, win_ref[:, 3 * D:4 * D]).astype(BF16)

    row_bf = lambda: _rows(tm, D)
    row_sm = lambda: _rows(tm, LANES)
    heads = lambda: pl.BlockSpec((H, tm, HA), lambda i: (0, i, 0))
    return pl.pallas_call(
        body, name="attn_proj_fwd", grid=(steps,),
        in_specs=[_rows(tm, D), _const((1, D)), _const((D, 4 * D)), _const((D, LANES)), _const((1, LANES)),
                  _const((1, D)), _const((1, D)), _const((1, LANES)), _const((D, LANES)), _const((2 * LANES, D)),
                  _const((tm, tm)), _const((3 * LANES, H * HA)), _const((3 * LANES, H * HA))],
        out_specs=[row_bf() for _ in range(4)] + [heads() for _ in range(3)] + [
            pl.BlockSpec((H, None, AUG_ROWS, tm), lambda i: (0, i // per, 0, i % per))] + [row_sm() for _ in range(4)],
        out_shape=[jax.ShapeDtypeStruct((s, D), BF16)] * 4 + [jax.ShapeDtypeStruct((H, s, HA), BF16)] * 3 + [
            jax.ShapeDtypeStruct((H, s // group, AUG_ROWS, group), BF16)] + [jax.ShapeDtypeStruct((s, LANES), F32)] * 4,
        scratch_shapes=[pltpu.VMEM((8, LANES), F32)],
        compiler_params=_params("arbitrary"),
    )(x1, g2, w2in, wf, bf, qg, kg, shift, e, et, tril, place_k, place_q)


def _dot_nt(a, b):
    return lax.dot_general(a, b, (((1,), (1,)), ((), ())), preferred_element_type=F32)


def _dot_tn(a, b):
    return lax.dot_general(a, b, (((0,), (0,)), ((), ())), preferred_element_type=F32)


def _flash_fwd(bounded, qa, ka, vat):
    h_, nq, tq, _ = qa.shape
    s, tk = ka.shape[1], tq
    group = vat.shape[3]
    gt = group // tk
    shift = gt.bit_length() - 1

    def body(flag_ref, q_ref, k_ref, vt_ref, o_ref, qb_ref, m_ref, acct_ref):
        i = pl.program_id(1)
        lane = lax.broadcasted_iota(jnp.int32, (tq, HA), 1)
        in_bias = (lane >= ROW_BIAS) & (lane < ROW_BIAS + 3)
        acct_ref[...] = jnp.zeros_like(acct_ref)
        g = lax.shift_right_logical(i, shift)
        r = i & (gt - 1)

        def run(use_bound):
            q = q_ref[...]
            if not use_bound:
                q = jnp.where(in_bias, jnp.zeros_like(q), q)
                m_ref[...] = jnp.full_like(m_ref, MASKED)

            def step(first_key, vt, tiles, diagonal_at):
                keys = pl.ds(pl.multiple_of(first_key, tk), tiles * tk)
                zt = _dot_nt(k_ref[keys, :], q)
                if diagonal_at is not None:
                    key = lax.broadcasted_iota(jnp.int32, (tiles * tk, tq), 0)
                    qry = lax.broadcasted_iota(jnp.int32, (tiles * tk, tq), 1)
                    zt = jnp.where(key <= qry + diagonal_at * tk, zt, MASKED)
                if use_bound:
                    acct_ref[0:AUG_ROWS, :] += _dot(vt, jnp.exp(zt).astype(BF16))
                else:
                    m_old = m_ref[...]
                    m_new = jnp.maximum(m_old, jnp.max(zt, axis=0, keepdims=True))
                    pt = jnp.exp(zt - m_new)
                    acct_ref[0:AUG_ROWS, :] = jnp.exp(m_old - m_new) * acct_ref[0:AUG_ROWS, :] + _dot(
                        vt, pt.astype(BF16))
                    m_ref[...] = m_new

            def whole_group(jj, carry):
                step(jj * group, vt_ref[jj], gt, None)
                return carry

            lax.fori_loop(0, g, whole_group, 0)
            for rr in range(gt):
                pl.when(r == rr)(functools.partial(
                    lambda rr: step(g * group, vt_ref[g, :, 0:(rr + 1) * tk], rr + 1, rr), rr))

            if not use_bound:
                acct_ref[COL_BIAS + 1:COL_BIAS + 2, :] = m_ref[...]
            acc = acct_ref[...].T
            l = jnp.sum(jnp.where(lane == COL_BIAS, acc, 0.0), axis=1, keepdims=True)
            if use_bound:
                m = -jnp.sum(jnp.where(in_bias, q.astype(F32), 0.0), axis=1, keepdims=True)
            else:
                m = jnp.sum(jnp.where(lane == COL_BIAS + 1, acc, 0.0), axis=1, keepdims=True)
            lse = m + jnp.log(l)
            o_ref[...] = jnp.where(lane < DH, acc / l, lse)
            hi, mid, lo = _split3(-lse)
            qb_ref[...] = jnp.where(lane == ROW_BIAS, hi, jnp.where(lane == ROW_BIAS + 1, mid, jnp.where(
                lane == ROW_BIAS + 2, lo, q_ref[...])))

        use_bound = flag_ref[0, 0] > 0.5
        pl.when(use_bound)(lambda: run(True))
        pl.when(jnp.logical_not(use_bound))(lambda: run(False))

    tile = lambda: pl.BlockSpec((None, None, tq, HA), lambda h, i: (h, i, 0, 0))
    return pl.pallas_call(
        body, name="flash_fwd", grid=(h_, nq),
        in_specs=[pl.BlockSpec(memory_space=pltpu.SMEM), tile(), pl.BlockSpec((None, s, HA), lambda h, i: (h, 0, 0)),
                  pl.BlockSpec((None, s // group, AUG_ROWS, group), lambda h, i: (h, 0, 0, 0))],
        out_specs=[tile(), tile()],
        out_shape=[jax.ShapeDtypeStruct((h_, nq, tq, HA), F32), jax.ShapeDtypeStruct((h_, nq, tq, HA), BF16)],
        scratch_shapes=[pltpu.VMEM((1, tq), F32), pltpu.VMEM((HA, tq), F32)],
        compiler_params=_params("arbitrary", "arbitrary"),
    )(bounded, qa, ka, vat)


def _merge_heads(src, out_ref, lane):
    for hp in range(H // 2):
        out_ref[:, hp * HA:(hp + 1) * HA] = jnp.where(lane < DH, src(2 * hp), pltpu.roll(src(2 * hp + 1), DH, 1))


def _attn_out(ol, z, x1, tgt, w2out, w2out_t, e, place_k):
    s = z.shape[0]
    tm = min(TM_BWD, s)
    steps = s // tm

    def body(ol_ref, z_ref, x1_ref, t_ref, w_ref, wt_ref, e_ref, pk_ref, og_ref, dx2_ref, dz_ref, doa_ref, loss_ref,
             o_ref):
        @pl.when(pl.program_id(0) == 0)
        def _():
            loss_ref[...] = jnp.zeros_like(loss_ref)

        lane = lax.broadcasted_iota(jnp.int32, (tm, HA), 1)
        _merge_heads(lambda hh: ol_ref[hh], o_ref, lane)
        ov = o_ref[...]
        zv = z_ref[...].astype(F32)
        sg = _sigmoid(zv)
        sil = zv * sg
        og = (ov * sil).astype(BF16)
        og_ref[...] = og
        err = x1_ref[...] + _dot(og, w_ref[...]) - t_ref[...]
        loss_ref[...] += (0.5 / D) * jnp.sum(err * err, axis=0, keepdims=True)
        dx2 = err * (1.0 / D)
        dx2_ref[...] = dx2
        dog = _dot(dx2.astype(BF16), wt_ref[...])
        do = (dog * sil).astype(BF16).astype(F32)
        dz_ref[...] = (dog * ov * (sg * (1.0 + zv * (1.0 - sg)))).astype(BF16)
        delta = _seg_sum(do * ov, e_ref[...])
        d_bias = _dot(_terms_cat(-delta), pk_ref[...])
        _split_heads(do, lambda hh: d_bias[:, hh * HA:(hh + 1) * HA], doa_ref, lane)

    heads = lambda: pl.BlockSpec((H, tm, HA), lambda i: (0, i, 0))
    return pl.pallas_call(
        body, name="attn_out", grid=(steps,),
        in_specs=[heads(), _rows(tm, D), _rows(tm, D), _rows(tm, D), _const((D, D)), _const((D, D)),
                  _const((D, LANES)), _const((3 * LANES, H * HA))],
        out_specs=[_rows(tm, D), _rows(tm, D), _rows(tm, D), heads(), pl.BlockSpec((1, D), lambda i: (0, 0))],
        out_shape=[jax.ShapeDtypeStruct((s, D), BF16), jax.ShapeDtypeStruct((s, D), F32),
                   jax.ShapeDtypeStruct((s, D), BF16), jax.ShapeDtypeStruct((H, s, HA), BF16),
                   jax.ShapeDtypeStruct((1, D), F32)],
        scratch_shapes=[pltpu.VMEM((tm, D), F32)],
        compiler_params=_params("arbitrary"),
    )(ol, z, x1, tgt, w2out, w2out_t, e, place_k)


def _flash_bwd(qab, doa, ka, va):
    h_, nq, tq, _ = qab.shape
    nk, tk = ka.shape[1], ka.shape[2]
    s = nk * tk
    gt = min(BWD_TILES, nq)
    shift = gt.bit_length() - 1

    def body(qa_ref, da_ref, ka_ref, va_ref, dq_hbm, dk_ref, dv_ref, dq_acc, dkt_ref, dvt_ref, sem):
        hh = pl.program_id(0)
        j = pl.program_id(1)

        @pl.when(j == 0)
        def _():
            dq_acc[...] = jnp.zeros_like(dq_acc)

        dkt_ref[...] = jnp.zeros_like(dkt_ref)
        dvt_ref[...] = jnp.zeros_like(dvt_ref)
        ka_v = ka_ref[...]
        va_v = va_ref[...]

        def step(i, tiles, masked):
            rows = pl.ds(pl.multiple_of(i * tq, tq), tiles * tq)
            qa_i, da_i = qa_ref[rows, :], da_ref[rows, :]
            zz = _dot_nt(qa_i, ka_v)
            if masked:
                row = lax.broadcasted_iota(jnp.int32, (tiles * tq, tk), 0)
                col = lax.broadcasted_iota(jnp.int32, (tiles * tq, tk), 1)
                zz = jnp.where(col <= row, zz, MASKED)
            p = jnp.exp(zz)
            ds = (p * _dot_nt(da_i, va_v)).astype(BF16)
            pb = p.astype(BF16)
            dq_acc[rows, :] += _dot(ds, ka_v)
            dvt_ref[...] += _dot_tn(da_i, pb)
            dkt_ref[...] += _dot_tn(qa_i, ds)

        g0 = lax.shift_right_logical(j, shift)
        for rr in range(gt):
            pl.when((j & (gt - 1)) == rr)(functools.partial(step, j, gt - rr, True))

        def whole_group(g, carry):
            step(g * gt, gt, False)
            return carry

        lax.fori_loop(g0 + 1, nq // gt, whole_group, 0)
        dk_ref[...] = dkt_ref[...].T
        dv_ref[...] = dvt_ref[...].T.astype(BF16)

        @pl.when(j == nk - 1)
        def _():
            cp = pltpu.make_async_copy(dq_acc, dq_hbm.at[hh], sem)
            cp.start()
            cp.wait()

    whole = lambda: pl.BlockSpec((None, nq * tq, HA), lambda h, j: (h, 0, 0))
    tile = lambda: pl.BlockSpec((None, None, tk, HA), lambda h, j: (h, j, 0, 0))
    rows = lambda: pl.BlockSpec((None, tk, HA), lambda h, j: (h, j, 0))
    return pl.pallas_call(
        body, name="flash_bwd", grid=(h_, nk),
        in_specs=[whole(), whole(), tile(), tile()],
        out_specs=[pl.BlockSpec(memory_space=pl.ANY), rows(), rows()],
        out_shape=[jax.ShapeDtypeStruct((h_, s, HA), F32)] * 2 + [jax.ShapeDtypeStruct((h_, s, HA), BF16)],
        scratch_shapes=[pltpu.VMEM((s, HA), F32), pltpu.VMEM((HA, tk), F32), pltpu.VMEM((HA, tk), F32),
                        pltpu.SemaphoreType.DMA],
        compiler_params=_params("arbitrary", "arbitrary"),
    )(qab.reshape(h_, s, HA), doa.reshape(h_, s, HA), ka, va)


def _attn_proj_bwd(dq, dk, dv, dz, qpre, kpre, invq, invk, fl, x1, dx2, g2, qg, kg, w2in_t, wf_t, e, et,
                   triu, fold):
    s = x1.shape[0]
    tm = min(TM_BWD, s)
    steps = s // tm

    def body(dq_ref, dk_ref, dvh_ref, dz_ref, qp_ref, kp_ref, iq_ref, ik_ref, fl_ref, x1_ref, dx2_ref,
             g2_ref, qg_ref, kg_ref, wt_ref, wft_ref, e_ref, et_ref, triu_ref, fold_ref,
             dx1_ref, dp_ref, df_ref, dg2_ref, dqg_ref, dkg_ref, dbf_ref, carry_ref, rc_ref, qcol_ref, kcol_ref,
             dqs_ref, dkn_ref, dv_ref):
        step = pl.program_id(0)
        lane = lax.broadcasted_iota(jnp.int32, (tm, HA), 1)
        _merge_heads(lambda hh: dvh_ref[hh].astype(F32), dv_ref, lane)
        dc = jnp.zeros((tm, HA), F32)
        for hp in range(H // 2):
            dqh = [dq_ref[2 * hp], dq_ref[2 * hp + 1]]
            dkh = [dk_ref[2 * hp], dk_ref[2 * hp + 1]]
            dqs_ref[:, hp * HA:(hp + 1) * HA] = jnp.where(lane < DH, dqh[0], pltpu.roll(dqh[1], DH, 1))
            dkn_ref[:, hp * HA:(hp + 1) * HA] = jnp.where(lane < DH, dkh[0], pltpu.roll(dkh[1], DH, 1))
            for k in range(2):
                sums = jnp.where(lane == ROW_BIAS, dqh[k], 0.0) - pltpu.roll(
                    jnp.where(lane == COL_BIAS, dkh[k], 0.0), ROW_BIAS - COL_BIAS, 1)
                dc = dc + pltpu.roll(sums, (2 * hp + k - ROW_BIAS) % HA, 1)

        @pl.when(step == 0)
        def _():
            carry_ref[...] = jnp.zeros_like(carry_ref)
            dg2_ref[...] = jnp.zeros_like(dg2_ref)
            dbf_ref[...] = jnp.zeros_like(dbf_ref)
            qcol_ref[...] = jnp.zeros_like(qcol_ref)
            kcol_ref[...] = jnp.zeros_like(kcol_ref)
            dqg_ref[...] = jnp.zeros_like(dqg_ref)
            dkg_ref[...] = jnp.zeros_like(dkg_ref)

        rc_ref[...] = _tri_sum(triu_ref[...], dc) + carry_ref[0:1, :]
        carry_ref[0:1, :] = rc_ref[0:1, :]
        df = rc_ref[...] * _sigmoid(-fl_ref[...])
        dfb = df.astype(BF16)
        df_ref[...] = dfb
        dbf_ref[...] += jnp.sum(df, axis=0, keepdims=True)

        def norm_bwd(dn, pre_ref, inv_ref, gain_ref, col_ref):
            pre = pre_ref[...].astype(F32)
            invh = inv_ref[...]
            invb = _seg_bcast(invh, et_ref[...])
            col_ref[...] += jnp.sum(dn * pre * invb, axis=0, keepdims=True)
            gd = dn * gain_ref[...]
            mean = _seg_sum(gd * pre, e_ref[...]) * (1.0 / DH)
            return invb * gd - pre * _seg_bcast(mean * invh * invh * invh, et_ref[...])

        dq = norm_bwd(dqs_ref[...] * (1.0 / math.sqrt(DH)), qp_ref, iq_ref, qg_ref, qcol_ref).astype(BF16)
        dp_ref[:, 0:D] = dq
        dh = _dot(dq, wt_ref[0:D, :])
        dk = norm_bwd(dkn_ref[...], kp_ref, ik_ref, kg_ref, kcol_ref).astype(BF16)
        dp_ref[:, D:2 * D] = dk
        dh += _dot(dk, wt_ref[D:2 * D, :])
        dvb = dv_ref[...].astype(BF16)
        dp_ref[:, 2 * D:3 * D] = dvb
        dh += _dot(dvb, wt_ref[2 * D:3 * D, :])
        dzb = dz_ref[...]
        dp_ref[:, 3 * D:4 * D] = dzb
        dh += _dot(dzb, wt_ref[3 * D:4 * D, :])
        dh += _dot(dfb, wft_ref[...])

        xv = x1_ref[...]
        inv = lax.rsqrt(jnp.mean(xv * xv, axis=-1, keepdims=True) + RMS_EPS)
        dg2_ref[...] += jnp.sum(dh * xv * inv, axis=0, keepdims=True)
        gh = dh * g2_ref[...]
        dx1_ref[...] = dx2_ref[...] + inv * gh - xv * (inv * inv * inv * jnp.mean(gh * xv, axis=-1, keepdims=True))

        @pl.when(step == steps - 1)
        def _():
            dqg_ref[...] = _fold_heads(qcol_ref[...], fold_ref[...])
            dkg_ref[...] = _fold_heads(kcol_ref[...], fold_ref[...])

    rr = lambda n: _rows(tm, n, rev=True, steps=steps)
    acc = lambda n, r=1: pl.BlockSpec((r, n), lambda i: (0, 0))
    heads = lambda: pl.BlockSpec((H, tm, HA), lambda i: (0, steps - 1 - i, 0))
    return pl.pallas_call(
        body, name="attn_proj_bwd", grid=(steps,),
        in_specs=[heads(), heads(), heads(),
                  rr(D), rr(D), rr(D), rr(LANES), rr(LANES), rr(LANES), rr(D), rr(D),
                  _const((1, D)), _const((1, D)), _const((1, D)), _const((4 * D, D)), _const((LANES, D)),
                  _const((D, LANES)), _const((2 * LANES, D)), _const((tm, tm)), _const((D, LANES))],
        out_specs=[rr(D), rr(4 * D), rr(LANES), acc(D), acc(LANES, 8), acc(LANES, 8), acc(LANES)],
        out_shape=[jax.ShapeDtypeStruct((s, D), F32), jax.ShapeDtypeStruct((s, 4 * D), BF16),
                   jax.ShapeDtypeStruct((s, LANES), BF16), jax.ShapeDtypeStruct((1, D), F32),
                   jax.ShapeDtypeStruct((8, LANES), F32), jax.ShapeDtypeStruct((8, LANES), F32),
                   jax.ShapeDtypeStruct((1, LANES), F32)],
        scratch_shapes=[pltpu.VMEM((8, LANES), F32), pltpu.VMEM((tm, LANES), F32), pltpu.VMEM((1, D), F32),
                        pltpu.VMEM((1, D), F32), pltpu.VMEM((tm, D), F32), pltpu.VMEM((tm, D), F32),
                        pltpu.VMEM((tm, D), F32)],
        compiler_params=_params("arbitrary"),
    )(dq, dk, dv, dz, qpre, kpre, invq, invk, fl, x1, dx2, g2, qg, kg, w2in_t, wf_t, e, et, triu, fold)


def _fold_heads(col, fold):
    hi, mid, lo = _split3(jnp.broadcast_to(col, (8, D)))
    return _dot(hi, fold) + _dot(mid, fold) + _dot(lo, fold)


def _conv_bwd(dx1, p1, x, g1, cw, w1out_t, w1in_t, grads2):
    s = x.shape[0]
    tm = min(TM_BWD, s)
    steps = s // tm
    halo = tm // 8

    def body(dx1_ref, b_ref, c_ref, xi_ref, z_ref, ch_ref, xh_ref, x_ref, g_ref, cw_ref, wot_ref, wit_ref, g2_ref,
             gx_ref, dp_ref, dcw_ref, dg1_ref, parts_ref, head_ref, send_sems, recv_sems, local_sem):
        step = pl.program_id(0)

        @pl.when(step == 0)
        def _():
            for cp in _block_copies(g2_ref, parts_ref, send_sems, recv_sems, local_sem):
                cp.start()

        @pl.when(step == 0)
        def _():
            head_ref[...] = jnp.zeros_like(head_ref)
            dcw_ref[...] = jnp.zeros_like(dcw_ref)
            dg1_ref[...] = jnp.zeros_like(dg1_ref)

        first_tile = step == steps - 1
        dx1 = dx1_ref[...]
        row = lax.broadcasted_iota(jnp.int32, (tm, CH), 0)
        dyb = dx1.astype(BF16)
        for ci in range(D // CH):
            lo, hi = ci * CH, (ci + 1) * CH
            dyg = _dot(dyb, wot_ref[:, lo:hi])
            b = b_ref[:, lo:hi].astype(F32)
            c = c_ref[:, lo:hi].astype(F32)
            xin = xi_ref[:, lo:hi].astype(F32)
            z = z_ref[:, lo:hi].astype(F32)
            u = c * xin
            t6 = jnp.where(first_tile, 0.0, ch_ref[6:7, lo:hi].astype(F32) * xh_ref[6:7, lo:hi].astype(F32))
            t7 = jnp.where(first_tile, 0.0, ch_ref[7:8, lo:hi].astype(F32) * xh_ref[7:8, lo:hi].astype(F32))
            u1 = jnp.where(row == 0, t7, pltpu.roll(u, 1, 0))
            u2 = jnp.where(row == 0, t6, jnp.where(row == 1, t7, pltpu.roll(u, 2, 0)))
            w0, w1, w2 = cw_ref[0:1, lo:hi], cw_ref[1:2, lo:hi], cw_ref[2:3, lo:hi]
            y = w2 * u + w1 * u1 + w0 * u2
            sg = _sigmoid(z)
            sil = z * sg
            dp_ref[:, lo:hi] = (dyg * y * sil).astype(BF16)
            dy = dyg * b * sil
            dp_ref[:, 3 * D + lo:3 * D + hi] = (dyg * b * y * (sg * (1.0 + z * (1.0 - sg)))).astype(BF16)
            dcw_ref[2:3, lo:hi] += jnp.sum(dy * u, axis=0, keepdims=True)
            dcw_ref[1:2, lo:hi] += jnp.sum(dy * u1, axis=0, keepdims=True)
            dcw_ref[0:1, lo:hi] += jnp.sum(dy * u2, axis=0, keepdims=True)
            n0 = head_ref[0:1, lo:hi]
            n1 = head_ref[1:2, lo:hi]
            dyn1 = jnp.where(row == tm - 1, n0, pltpu.roll(dy, tm - 1, 0))
            dyn2 = jnp.where(row == tm - 2, n0, jnp.where(row == tm - 1, n1, pltpu.roll(dy, tm - 2, 0)))
            head_ref[:, lo:hi] = dy[0:8, :]
            du = w2 * dy + w1 * dyn1 + w0 * dyn2
            dp_ref[:, D + lo:D + hi] = (du * xin).astype(BF16)
            dp_ref[:, 2 * D + lo:2 * D + hi] = (du * c).astype(BF16)
        dh = _dot(dp_ref[:, 0:D], wit_ref[0:D, :])
        for k in range(1, 4):
            dh += _dot(dp_ref[:, k * D:(k + 1) * D], wit_ref[k * D:(k + 1) * D, :])
        xv = x_ref[...]
        inv = lax.rsqrt(jnp.mean(xv * xv, axis=-1, keepdims=True) + RMS_EPS)
        dg1_ref[...] += jnp.sum(dh * xv * inv, axis=0, keepdims=True)
        gh = dh * g_ref[...]
        gx_ref[...] = dx1 + inv * gh - xv * (inv * inv * inv * jnp.mean(gh * xv, axis=-1, keepdims=True))

        @pl.when(step == steps - 1)
        def _():
            for cp in _block_copies(g2_ref, parts_ref, send_sems, recv_sems, local_sem):
                cp.wait()

    rr = lambda n: _rows(tm, n, rev=True, steps=steps)
    part = lambda k: pl.BlockSpec((tm, D), lambda i: (steps - 1 - i, k))
    prev8 = lambda k: pl.BlockSpec((8, D), lambda i: (jnp.maximum((steps - 1 - i) * halo - 1, 0), k))
    return pl.pallas_call(
        body, name="conv_bwd", grid=(steps,),
        in_specs=[rr(D), part(0), part(1), part(2), part(3), prev8(1), prev8(2), rr(D), _const((1, D)),
                  _const((8, D)), _const((D, D)), _const((4 * D, D)), pl.BlockSpec(memory_space=pl.ANY)],
        out_specs=[rr(D), rr(4 * D), pl.BlockSpec((8, D), lambda i: (0, 0)), pl.BlockSpec((1, D), lambda i: (0, 0)),
                   pl.BlockSpec(memory_space=pl.ANY)],
        out_shape=[jax.ShapeDtypeStruct((s, D), F32), jax.ShapeDtypeStruct((s, 4 * D), BF16),
                   jax.ShapeDtypeStruct((8, D), F32), jax.ShapeDtypeStruct((1, D), F32),
                   jax.ShapeDtypeStruct(grads2.shape, grads2.dtype)],
        scratch_shapes=[pltpu.VMEM((8, D), F32)] + list(EXCHANGE_SEMS),
        compiler_params=_params("arbitrary"),
    )(dx1, p1, p1, p1, p1, p1, p1, x, g1, cw, w1out_t, w1in_t, grads2)


def _wgrad(a, g, name):
    s, k = a.shape
    n = g.shape[1]
    bn = min(n, 1024)
    ts = min(1024, s)

    def body(a_ref, g_ref, out_ref, acc_ref):
        t = pl.program_id(1)

        @pl.when(t == 0)
        def _():
            acc_ref[...] = jnp.zeros_like(acc_ref)

        acc_ref[...] += _dot_tn(a_ref[...], g_ref[...].astype(BF16))

        @pl.when(t == s // ts - 1)
        def _():
            out_ref[...] = acc_ref[...].astype(BF16)

    return pl.pallas_call(
        body, name=name, grid=(n // bn, s // ts),
        in_specs=[pl.BlockSpec((ts, k), lambda j, t: (t, 0)), pl.BlockSpec((ts, bn), lambda j, t: (t, j))],
        out_specs=pl.BlockSpec((k, bn), lambda j, t: (0, j)),
        out_shape=jax.ShapeDtypeStruct((k, n), BF16),
        scratch_shapes=[pltpu.VMEM((k, bn), F32)],
        compiler_params=_params("arbitrary", "arbitrary"),
    )(a, g)


def _wgrad_scatter(a, g, ready, name):
    s, k_ = a.shape
    _, r_, c_ = ready.shape
    bn, ts = 2 * c_, min(1024, s)
    assert g.shape[1] == NDEV * c_
    last_t, last_j = s // ts - 1, NDEV // 2 - 1

    def body(a_ref, g_ref, ready_ref, parts_ref, acc_ref, stage_ref, send_a, recv_a, local_a, send_b, recv_b, local_b):
        j, t = pl.program_id(0), pl.program_id(1)
        x, y, c = _place()
        me = 4 * x + 2 * y + c
        early = _block_copies(ready_ref, parts_ref, send_b, recv_b, local_b, first_row=k_)
        fresh = _block_copies(stage_ref, parts_ref, send_a, recv_a, local_a)

        @pl.when(jnp.logical_and(j == 0, t == 0))
        def _():
            for cp in early:
                cp.start()

        @pl.when(t == 0)
        def _():
            acc_ref[...] = jnp.zeros_like(acc_ref)

        acc_ref[...] += _dot_tn(a_ref[...], g_ref[...])

        @pl.when(t == last_t)
        def _():
            stage_ref[2 * j] = acc_ref[:, 0:c_].astype(BF16)
            stage_ref[2 * j + 1] = acc_ref[:, c_:bn].astype(BF16)

        for k, cp in enumerate(fresh):
            pl.when(jnp.logical_and(t == last_t, lax.shift_right_logical(me ^ k, 1) == j))(cp.start)

        @pl.when(jnp.logical_and(j == last_j, t == last_t))
        def _():
            for cp in early + fresh:
                cp.wait()

    return pl.pallas_call(
        body, name=name, grid=(NDEV // 2, s // ts),
        in_specs=[pl.BlockSpec((ts, k_), lambda j, t: (t, 0)), pl.BlockSpec((ts, bn), lambda j, t: (t, j)),
                  pl.BlockSpec(memory_space=pl.ANY)],
        out_specs=pl.BlockSpec(memory_space=pl.ANY),
        out_shape=jax.ShapeDtypeStruct((NDEV, k_ + r_, c_), BF16),
        scratch_shapes=[pltpu.VMEM((k_, bn), F32), pltpu.VMEM((NDEV, k_, c_), BF16)] + 2 * list(EXCHANGE_SEMS),
        compiler_params=_params("arbitrary", "arbitrary"),
    )(a, g, ready)


def _adamw_math(w, g, m, v):
    m = ADAM_B1 * m + (1.0 - ADAM_B1) * g
    v = ADAM_B2 * v + (1.0 - ADAM_B2) * (g * g)
    m_hat = m / (1.0 - ADAM_B1 ** ADAM_STEP)
    v_hat = v / (1.0 - ADAM_B2 ** ADAM_STEP)
    delta = -ADAM_LR * (m_hat / (jnp.sqrt(v_hat) + ADAM_EPS) + ADAM_WD * w)
    return delta, m, v


def _adamw_big(parts, w, m, v, name):
    _, r, c_ = parts.shape
    rb = r // ADAM_BLOCKS
    assert r % (16 * ADAM_BLOCKS) == 0

    def body(p_ref, w_ref, m_ref, v_ref, g_ref, d_ref, mo_ref, vo_ref):
        g = p_ref[0].astype(F32)
        for k in range(1, NDEV):
            g = g + p_ref[k].astype(F32)
        g_ref[...] = g
        d_ref[...], mo_ref[...], vo_ref[...] = _adamw_math(w_ref[...], g, m_ref[...], v_ref[...])

    blk = pl.BlockSpec((rb, c_), lambda i: (i, 0))
    return pl.pallas_call(
        body, name=name, grid=(r // rb,),
        in_specs=[pl.BlockSpec((NDEV, rb, c_), lambda i: (0, i, 0)), blk, blk, blk],
        out_specs=[blk, blk, blk, blk],
        out_shape=[jax.ShapeDtypeStruct((r, c_), F32)] * 4,
        compiler_params=_params("arbitrary"),
    )(parts, w, m, v)


def _sum_parts(parts, loss_row):
    _, r, c_ = parts.shape

    def body(p_ref, o_ref, l_ref):
        g = p_ref[0]
        for k in range(1, NDEV):
            g = g + p_ref[k]
        o_ref[...] = g
        per_row = jnp.sum(o_ref[loss_row:loss_row + 8, :], axis=1, keepdims=True)
        l_ref[...] = jnp.broadcast_to(jnp.sum(per_row, axis=0, keepdims=True), (8, c_))

    return pl.pallas_call(body, name="sum_small", out_shape=[jax.ShapeDtypeStruct((r, c_), F32),
                                                             jax.ShapeDtypeStruct((8, c_), F32)])(parts)


def _adamw_small(g, w, m, v):
    def body(g_ref, w_ref, m_ref, v_ref, d_ref, mo_ref, vo_ref):
        d_ref[...], mo_ref[...], vo_ref[...] = _adamw_math(w_ref[...], g_ref[...], m_ref[...], v_ref[...])

    return pl.pallas_call(body, name="adamw_small", out_shape=[jax.ShapeDtypeStruct(g.shape, F32)] * 3)(g, w, m, v)


def _pad_lanes(a):
    return jnp.pad(a, ((0, 0), (0, LANES - a.shape[1])))


def _tiles(a, t):
    return a.reshape(H, a.shape[1] // t, t, HA)


def kernel(x, conv_norm_g, conv_w_in, conv_w, conv_w_out, attn_norm_g, attn_w_in, attn_b_f, attn_q_norm_g, attn_k_norm_g, attn_w_out, loss_target, m_conv_norm_g, m_conv_w_in, m_conv_w, m_conv_w_out, m_attn_norm_g, m_attn_w_in, m_attn_b_f, m_attn_q_norm_g, m_attn_k_norm_g, m_attn_w_out, v_conv_norm_g, v_conv_w_in, v_conv_w, v_conv_w_out, v_attn_norm_g, v_attn_w_in, v_attn_b_f, v_attn_q_norm_g, v_attn_k_norm_g, v_attn_w_out):
    s = x.shape[1]
    tq = min(TQ, s)
    tmb = min(TM_BWD, s)
    me = 4 * lax.axis_index("x") + 2 * lax.axis_index("y") + lax.axis_index("c")
    xv, tgt = x[0], loss_target[0]

    def slab(t_in, t_out, pad, dtype):
        rows = [t_in[0].reshape(-1, PACK_W), t_out[0].reshape(ROWS_WOUT, PACK_W)]
        return jnp.concatenate(rows + ([jnp.zeros((pad, PACK_W), F32)] if pad else []), axis=0).astype(dtype)

    wg1 = _all_gather(slab(conv_w_in, conv_w_out, 0, BF16), "gather_weights")
    small_w = jnp.concatenate([conv_w[0], attn_norm_g, jnp.zeros((4, 128), F32)], axis=0)
    sg_ = _all_gather(small_w, "gather_small_weights")
    w1in = wg1[:, :ROWS_W1IN].transpose(1, 0, 2).reshape(D, 4 * D)
    w1out = wg1[:, ROWS_W1IN:].reshape(D, D)
    cw = jnp.concatenate([sg_[:, 0:3, :].transpose(1, 0, 2).reshape(3, D), jnp.zeros((5, D), F32)], axis=0)
    g2 = sg_[:, 3, :].reshape(1, D)
    qg_t, kg_t = jnp.tile(attn_q_norm_g, (1, H)), jnp.tile(attn_k_norm_g, (1, H))
    bf = _pad_lanes(attn_b_f)

    e = (jnp.arange(D)[:, None] // DH == jnp.arange(LANES)[None, :]).astype(BF16)
    et2 = jnp.concatenate([e.T, e.T], axis=0)
    fold = (jnp.arange(D)[:, None] % DH == jnp.arange(LANES)[None, :]).astype(BF16)
    tril = (jnp.arange(tmb)[:, None] >= jnp.arange(tmb)[None, :]).astype(BF16)
    triu = tril.T
    src = jnp.arange(3 * LANES)
    dst = jnp.arange(H * HA)[None, :] - (HA * (src % LANES) + src // LANES)[:, None]
    place_k = ((dst == COL_BIAS) & (src % LANES < H)[:, None]).astype(BF16)
    place_q = ((dst == ROW_BIAS) & (src % LANES < H)[:, None]).astype(BF16)

    x1, h1, p1, yg, wg2 = _conv_fwd(xv, conv_norm_g, w1in, cw, w1out, slab(attn_w_in, attn_w_out, PAD_ROWS2, BF16))
    w2all = wg2[:, :ROWS_W2IN].reshape(NDEV, D, 514).transpose(1, 0, 2).reshape(D, 4 * D + H)
    w2in, wf = w2all[:, :4 * D], _pad_lanes(w2all[:, 4 * D:])
    w2out = wg2[:, ROWS_W2IN:ROWS_W2IN + ROWS_WOUT].reshape(D, D)
    reach = 1.01 * math.sqrt(DH) * jnp.max(jnp.abs(attn_q_norm_g)) * jnp.max(jnp.abs(attn_k_norm_g))
    bounded = (2.0 * reach <= BOUNDED_SOFTMAX_REACH).astype(F32).reshape(1, 1)
    group = tq * min(FWD_TILES, s // tq)
    h2, qpre, kpre, z, qa, ka, va, vat, invq, invk, cc, fl = _attn_proj_fwd(
        x1, g2, w2in, wf, bf, qg_t, kg_t, jnp.broadcast_to(reach, (1, LANES)), e, et2, tril, place_k, place_q, group)
    ol, qab = _flash_fwd(bounded, _tiles(qa, tq), ka, vat)
    og, dx2, dz, doa, lossp = _attn_out(ol.reshape(H, s, HA), z, x1, tgt, w2out, w2out.T, e, place_k)

    dq, dk, dv = _flash_bwd(qab, _tiles(doa, tq), _tiles(ka, tq), _tiles(va, tq))
    dx1, dp2, df, dg2, dqg, dkg, dbf = _attn_proj_bwd(
        dq, dk, dv, dz, qpre, kpre, invq, invk, fl, x1, dx2, g2, qg_t, kg_t, w2in.T, wf.T, e, et2, triu, fold)
    dw2out = _wgrad(og, dx2, "wgrad_attn_out")
    dw2in = _wgrad(h2, dp2, "wgrad_attn_in")
    dwf = _wgrad(h2, df, "wgrad_attn_forget")
    dw2all = jnp.concatenate([dw2in, dwf[:, :H]], axis=1)
    grads2 = jnp.concatenate([
        dw2all.reshape(D, NDEV, 514).transpose(1, 0, 2).reshape(NDEV, ROWS_W2IN, PACK_W),
        dw2out.reshape(NDEV, ROWS_WOUT, PACK_W), jnp.zeros((NDEV, PAD_ROWS2, PACK_W), BF16)], axis=1)
    gx, dp1, dcw, dg1, parts2 = _conv_bwd(dx1, p1, xv, conv_norm_g, cw, w1out.T, w1in.T, grads2)
    dw1out = _wgrad(yg, dx1, "wgrad_conv_out")
    parts1 = _wgrad_scatter(h1, dp1, dw1out.reshape(NDEV, ROWS_WOUT, PACK_W), "wgrad_conv_in_scatter")

    big1 = _adamw_big(parts1, slab(conv_w_in, conv_w_out, 0, F32), slab(m_conv_w_in, m_conv_w_out, 0, F32),
                      slab(v_conv_w_in, v_conv_w_out, 0, F32), "adamw_conv")
    big2 = _adamw_big(parts2, slab(attn_w_in, attn_w_out, PAD_ROWS2, F32),
                      slab(m_attn_w_in, m_attn_w_out, PAD_ROWS2, F32),
                      slab(v_attn_w_in, v_attn_w_out, PAD_ROWS2, F32), "adamw_attn")

    shard_rows = jnp.concatenate([dcw[0:3].reshape(3, NDEV, 128).transpose(1, 0, 2), dg2.reshape(NDEV, 1, 128),
                                  jnp.zeros((NDEV, 4, 128), F32)], axis=1).reshape(64, 128)
    small_g = jnp.concatenate([shard_rows, dg1.reshape(8, 128), dbf, dqg[0:1], dkg[0:1], jnp.zeros((5, 128), F32),
                               lossp.reshape(8, 128)], axis=0)
    gs, loss8 = _sum_parts(_all_gather(small_g, "gather_small_grads"), 80)
    loss = loss8[0, 0]
    mine = lax.dynamic_slice(gs, (8 * me, 0), (8, 128))
    g_small = jnp.concatenate([mine, gs[64:75], jnp.zeros((5, 128), F32)], axis=0)

    def pack_small(cwk, ang, cng, bfk, qgk, kgk):
        return jnp.concatenate([cwk[0], ang, jnp.zeros((4, 128), F32), cng.reshape(8, 128), _pad_lanes(bfk),
                                _pad_lanes(qgk), _pad_lanes(kgk), jnp.zeros((5, 128), F32)], axis=0)

    ds_, ms_, vs_ = _adamw_small(
        g_small, pack_small(conv_w, attn_norm_g, conv_norm_g, attn_b_f, attn_q_norm_g, attn_k_norm_g),
        pack_small(m_conv_w, m_attn_norm_g, m_conv_norm_g, m_attn_b_f, m_attn_q_norm_g, m_attn_k_norm_g),
        pack_small(v_conv_w, v_attn_norm_g, v_conv_norm_g, v_attn_b_f, v_attn_q_norm_g, v_attn_k_norm_g))

    def leaves(i, small):
        b1, b2 = big1[i], big2[i]
        return (small[8:16].reshape(1, D), b1[:ROWS_W1IN].reshape(1, D, 512), small[0:3].reshape(1, 3, 128),
                b1[ROWS_W1IN:].reshape(1, 128, D), small[3:4], b2[:ROWS_W2IN].reshape(1, D, 514),
                small[16:17, :H], small[17:18, :DH], small[18:19, :DH],
                b2[ROWS_W2IN:ROWS_W2IN + ROWS_WOUT].reshape(1, 128, D))

    return (loss, gx[None], *leaves(0, g_small), *leaves(1, ds_), *leaves(2, ms_), *leaves(3, vs_))
```
